```python
import math
import jax, jax.numpy as jnp
from jax import lax
import numpy as np

D_MODEL = 1024
BATCH = 8
SEQ = 16384
DEPTH = 2

MLA_HEADS = 8
MLA_Q_LORA = 256
MLA_KV_LORA = 128
MLA_NOPE = 64
MLA_ROPE = 32
MLA_V = 64
Q_BLOCK = 128
ROPE_THETA = 10000.0
SWA_HEADS = 8
SWA_KV_HEADS = 2
SWA_HD = 64
WINDOW = 128
BLOCK = 128
REL_BUCKETS = 32
REL_MAX_DIST = 128
MEM_LEN = 256
MEM_HEADS = 4
MEM_HD = 128
N_BRANCH = 3
D_FF = 4 * D_MODEL
EPS = 1e-6

IN_SIZES = (
    MLA_Q_LORA,
    MLA_KV_LORA + MLA_ROPE,
    SWA_HEADS * SWA_HD,
    SWA_KV_HEADS * SWA_HD,
    SWA_KV_HEADS * SWA_HD,
    MEM_HEADS * MEM_HD,
    N_BRANCH * D_MODEL,
)
IN_COLS = sum(IN_SIZES)

kernel_name = "hybrid_mla_swa_mem_gated_block"


def rms_norm(x, g):
    x32 = x.astype(jnp.float32)
    y = x32 * lax.rsqrt(jnp.mean(x32 * x32, axis=-1, keepdims=True) + EPS)
    return y.astype(x.dtype) * g


def split_points():
    pts, acc = [], 0
    for s in IN_SIZES[:-1]:
        acc += s
        pts.append(acc)
    return pts


def rope_tables(seq):
    pos = jnp.arange(seq, dtype=jnp.float32)
    inv = 1.0 / (ROPE_THETA ** (jnp.arange(0, MLA_ROPE, 2, dtype=jnp.float32) / MLA_ROPE))
    ang = pos[:, None] * inv[None, :]
    return jnp.cos(ang), jnp.sin(ang)


def apply_rope(t, cos, sin):
    cos = cos.astype(t.dtype)
    sin = sin.astype(t.dtype)
    t1, t2 = jnp.split(t, 2, axis=-1)
    return jnp.concatenate([t1 * cos - t2 * sin, t2 * cos + t1 * sin], axis=-1)


def t5_bucket(dist):
    n = jnp.maximum(dist, 0)
    max_exact = REL_BUCKETS // 2
    nf = jnp.maximum(n, 1).astype(jnp.float32)
    large = max_exact + (jnp.log(nf / max_exact) / math.log(REL_MAX_DIST / max_exact)
                         * (REL_BUCKETS - max_exact)).astype(jnp.int32)
    large = jnp.minimum(large, REL_BUCKETS - 1)
    return jnp.where(n < max_exact, n, large)


def mla_attention(q_nope, q_rope, k_nope, k_rope, v):
    B, S = q_nope.shape[0], q_nope.shape[1]
    nb = S // Q_BLOCK
    scale = (MLA_NOPE + MLA_ROPE) ** -0.5
    k_pos = jnp.arange(S)

    def to_blocks(t):
        return jnp.moveaxis(t.reshape(B, nb, Q_BLOCK, *t.shape[2:]), 1, 0)

    def one_block(args):
        qn, qr, i = args
        s = (jnp.einsum('bqhd,bkhd->bhqk', qn, k_nope)
             + jnp.einsum('bqhd,bkd->bhqk', qr, k_rope)).astype(jnp.float32) * scale
        q_pos = i * Q_BLOCK + jnp.arange(Q_BLOCK)
        s = jnp.where(k_pos[None, :] <= q_pos[:, None], s, -jnp.inf)
        p = jax.nn.softmax(s, axis=-1).astype(v.dtype)
        return jnp.einsum('bhqk,bkhd->bqhd', p, v)

    out = lax.map(one_block, (to_blocks(q_nope), to_blocks(q_rope), jnp.arange(nb)))
    return jnp.moveaxis(out, 0, 1).reshape(B, S, MLA_HEADS * MLA_V)


def swa_attention(q, k, v, rel_bias, sinks):
    B, S = q.shape[0], q.shape[1]
    nb = S // BLOCK
    G, R = SWA_KV_HEADS, SWA_HEADS // SWA_KV_HEADS
    qb = q.reshape(B, nb, BLOCK, G, R, SWA_HD)
    kb = k.reshape(B, nb, BLOCK, G, SWA_HD)
    vb = v.reshape(B, nb, BLOCK, G, SWA_HD)

    def with_prev(t):
        prev = jnp.pad(t, ((0, 0), (1, 0), (0, 0), (0, 0), (0, 0)))[:, :-1]
        return jnp.concatenate([prev, t], axis=2)

    kk, vv = with_prev(kb), with_prev(vb)
    s = jnp.einsum('bnqgrd,bnkgd->bngrqk', qb, kk).astype(jnp.float32) * (SWA_HD ** -0.5)

    qi = jnp.arange(BLOCK)[:, None]
    kj = jnp.arange(2 * BLOCK)[None, :]
    dist = qi + BLOCK - kj
    bias = rel_bias.astype(jnp.float32)[t5_bucket(dist)]
    bias = jnp.transpose(bias, (2, 0, 1)).reshape(G, R, BLOCK, 2 * BLOCK)
    band = (dist >= 0) & (dist < WINDOW)
    has_prev = (jnp.arange(nb) > 0)[:, None, None] | (kj >= BLOCK)[None]
    valid = band[None] & has_prev
    s = jnp.where(valid[None, :, None, None], s + bias, -jnp.inf)

    sink = jnp.broadcast_to(sinks.astype(jnp.float32).reshape(1, 1, G, R, 1, 1),
                            s.shape[:-1] + (1,))
    p = jax.nn.softmax(jnp.concatenate([s, sink], axis=-1), axis=-1)[..., :-1].astype(v.dtype)
    o = jnp.einsum('bngrqk,bnkgd->bnqgrd', p, vv)
    return o.reshape(B, S, SWA_HEADS * SWA_HD)


def mem_attention(q, km, vm):
    B, S = q.shape[0], q.shape[1]
    s = jnp.einsum('bshd,bmhd->bhsm', q, km).astype(jnp.float32) * (MEM_HD ** -0.5)
    p = jax.nn.softmax(s, axis=-1).astype(vm.dtype)
    return jnp.einsum('bhsm,bmhd->bshd', p, vm).reshape(B, S, MEM_HEADS * MEM_HD)


def _fwd_setup_inputs(seed: int = 0) -> dict:
    key = jax.random.key(seed)
    ks = jax.random.split(key, 24)

    def nrm(k, shape, scale):
        return jax.random.normal(k, shape, jnp.float32) * scale

    def gain(k, shape):
        return 1.0 + 0.02 * jax.random.normal(k, shape, jnp.float32)

    L, D = DEPTH, D_MODEL
    return {
        "x": nrm(ks[0], (BATCH, SEQ, D), 1.0),
        "mem": nrm(ks[1], (BATCH, MEM_LEN, D), 1.0),
        "rel_bias": nrm(ks[2], (REL_BUCKETS, SWA_HEADS), 0.5),
        "attn_norm": gain(ks[3], (L, D)),
        "mem_norm": gain(ks[4], (L, D)),
        "w_in": nrm(ks[5], (L, D, IN_COLS), D ** -0.5),
        "b_gate": nrm(ks[6], (L, N_BRANCH * D), 0.02),
        "mla_q_norm": gain(ks[7], (L, MLA_Q_LORA)),
        "w_uq": nrm(ks[8], (L, MLA_Q_LORA, MLA_HEADS * (MLA_NOPE + MLA_ROPE)), MLA_Q_LORA ** -0.5),
        "mla_kv_norm": gain(ks[9], (L, MLA_KV_LORA)),
        "w_ukv": nrm(ks[10], (L, MLA_KV_LORA, MLA_HEADS * (MLA_NOPE + MLA_V)), MLA_KV_LORA ** -0.5),
        "attn_sinks": nrm(ks[11], (L, SWA_HEADS), 0.5),
        "w_mem_kv": nrm(ks[12], (L, D, 2 * MEM_HEADS * MEM_HD), D ** -0.5),
        "w_o_mla": nrm(ks[13], (L, MLA_HEADS * MLA_V, D), (MLA_HEADS * MLA_V) ** -0.5),
        "w_o_swa": nrm(ks[14], (L, SWA_HEADS * SWA_HD, D), (SWA_HEADS * SWA_HD) ** -0.5),
        "w_o_mem": nrm(ks[15], (L, MEM_HEADS * MEM_HD, D), (MEM_HEADS * MEM_HD) ** -0.5),
        "w_out": nrm(ks[16], (L, D, D), D ** -0.5),
        "mlp_norm": gain(ks[17], (L, D)),
        "w_up": nrm(ks[18], (L, D, D_FF), D ** -0.5),
        "w_down": nrm(ks[19], (L, D_FF, D), D_FF ** -0.5),
        "final_norm": gain(ks[20], (D,)),
    }


def _fwd_reference(x, mem, rel_bias, attn_norm, mem_norm, w_in, b_gate, mla_q_norm, w_uq,
              mla_kv_norm, w_ukv, attn_sinks, w_mem_kv, w_o_mla, w_o_swa, w_o_mem,
              w_out, mlp_norm, w_up, w_down, final_norm):
    B, S, D = x.shape
    cos, sin = rope_tables(S)
    pts = split_points()

    for l in range(DEPTH):
        h = rms_norm(x, attn_norm[l])
        proj = h @ w_in[l]
        c_q, kv_a, q_s, k_s, v_s, q_m, gates = jnp.split(proj, pts, axis=-1)

        c_q = rms_norm(c_q, mla_q_norm[l])
        q = (c_q @ w_uq[l]).reshape(B, S, MLA_HEADS, MLA_NOPE + MLA_ROPE)
        q_nope = q[..., :MLA_NOPE]
        q_pe = apply_rope(q[..., MLA_NOPE:], cos[:, None, :], sin[:, None, :])
        c_kv = rms_norm(kv_a[..., :MLA_KV_LORA], mla_kv_norm[l])
        k_pe = apply_rope(kv_a[..., MLA_KV_LORA:], cos, sin)
        kv = (c_kv @ w_ukv[l]).reshape(B, S, MLA_HEADS, MLA_NOPE + MLA_V)
        o_mla = mla_attention(q_nope, q_pe, kv[..., :MLA_NOPE], k_pe, kv[..., MLA_NOPE:])

        o_swa = swa_attention(q_s.reshape(B, S, SWA_HEADS, SWA_HD),
                              k_s.reshape(B, S, SWA_KV_HEADS, SWA_HD),
                              v_s.reshape(B, S, SWA_KV_HEADS, SWA_HD),
                              rel_bias, attn_sinks[l])

        mn = rms_norm(mem, mem_norm[l])
        kvm = (mn @ w_mem_kv[l]).reshape(mem.shape[0], MEM_LEN, 2, MEM_HEADS, MEM_HD)
        o_mem = mem_attention(q_m.reshape(B, S, MEM_HEADS, MEM_HD), kvm[:, :, 0], kvm[:, :, 1])

        g = jax.nn.sigmoid(gates + b_gate[l]).reshape(B, S, N_BRANCH, D)
        y = (g[..., 0, :] * (o_mla @ w_o_mla[l])
             + g[..., 1, :] * (o_swa @ w_o_swa[l])
             + g[..., 2, :] * (o_mem @ w_o_mem[l]))
        x = x + y @ w_out[l]

        h = rms_norm(x, mlp_norm[l])
        x = x + jnp.square(jax.nn.relu(h @ w_up[l])) @ w_down[l]

    return rms_norm(x, final_norm)


import jax as _jax
import jax.numpy as _jnp

TWIN_FORMAT = 'train_step'
FWD_PARAMS = ['x', 'mem', 'rel_bias', 'attn_norm', 'mem_norm', 'w_in', 'b_gate', 'mla_q_norm', 'w_uq', 'mla_kv_norm', 'w_ukv', 'attn_sinks', 'w_mem_kv', 'w_o_mla', 'w_o_swa', 'w_o_mem', 'w_out', 'mlp_norm', 'w_up', 'w_down', 'final_norm']
TWIN_WEIGHTS = ['rel_bias', 'attn_norm', 'mem_norm', 'w_in', 'b_gate', 'mla_q_norm', 'w_uq', 'mla_kv_norm', 'w_ukv', 'attn_sinks', 'w_mem_kv', 'w_o_mla', 'w_o_swa', 'w_o_mem', 'w_out', 'mlp_norm', 'w_up', 'w_down', 'final_norm']
TWIN_DIFF_INPUT = 'x'
TWIN_INPUTS = ['x', 'mem', 'rel_bias', 'attn_norm', 'mem_norm', 'w_in', 'b_gate', 'mla_q_norm', 'w_uq', 'mla_kv_norm', 'w_ukv', 'attn_sinks', 'w_mem_kv', 'w_o_mla', 'w_o_swa', 'w_o_mem', 'w_out', 'mlp_norm', 'w_up', 'w_down', 'final_norm', 'loss_target', 'm_rel_bias', 'm_attn_norm', 'm_mem_norm', 'm_w_in', 'm_b_gate', 'm_mla_q_norm', 'm_w_uq', 'm_mla_kv_norm', 'm_w_ukv', 'm_attn_sinks', 'm_w_mem_kv', 'm_w_o_mla', 'm_w_o_swa', 'm_w_o_mem', 'm_w_out', 'm_mlp_norm', 'm_w_up', 'm_w_down', 'm_final_norm', 'v_rel_bias', 'v_attn_norm', 'v_mem_norm', 'v_w_in', 'v_b_gate', 'v_mla_q_norm', 'v_w_uq', 'v_mla_kv_norm', 'v_w_ukv', 'v_attn_sinks', 'v_w_mem_kv', 'v_w_o_mla', 'v_w_o_swa', 'v_w_o_mem', 'v_w_out', 'v_mlp_norm', 'v_w_up', 'v_w_down', 'v_final_norm']
TWIN_OUTPUTS = ['loss', 'grad_x', 'grad_rel_bias', 'grad_attn_norm', 'grad_mem_norm', 'grad_w_in', 'grad_b_gate', 'grad_mla_q_norm', 'grad_w_uq', 'grad_mla_kv_norm', 'grad_w_ukv', 'grad_attn_sinks', 'grad_w_mem_kv', 'grad_w_o_mla', 'grad_w_o_swa', 'grad_w_o_mem', 'grad_w_out', 'grad_mlp_norm', 'grad_w_up', 'grad_w_down', 'grad_final_norm', 'delta_rel_bias', 'delta_attn_norm', 'delta_mem_norm', 'delta_w_in', 'delta_b_gate', 'delta_mla_q_norm', 'delta_w_uq', 'delta_mla_kv_norm', 'delta_w_ukv', 'delta_attn_sinks', 'delta_w_mem_kv', 'delta_w_o_mla', 'delta_w_o_swa', 'delta_w_o_mem', 'delta_w_out', 'delta_mlp_norm', 'delta_w_up', 'delta_w_down', 'delta_final_norm', 'new_m_rel_bias', 'new_m_attn_norm', 'new_m_mem_norm', 'new_m_w_in', 'new_m_b_gate', 'new_m_mla_q_norm', 'new_m_w_uq', 'new_m_mla_kv_norm', 'new_m_w_ukv', 'new_m_attn_sinks', 'new_m_w_mem_kv', 'new_m_w_o_mla', 'new_m_w_o_swa', 'new_m_w_o_mem', 'new_m_w_out', 'new_m_mlp_norm', 'new_m_w_up', 'new_m_w_down', 'new_m_final_norm', 'new_v_rel_bias', 'new_v_attn_norm', 'new_v_mem_norm', 'new_v_w_in', 'new_v_b_gate', 'new_v_mla_q_norm', 'new_v_w_uq', 'new_v_mla_kv_norm', 'new_v_w_ukv', 'new_v_attn_sinks', 'new_v_w_mem_kv', 'new_v_w_o_mla', 'new_v_w_o_swa', 'new_v_w_o_mem', 'new_v_w_out', 'new_v_mlp_norm', 'new_v_w_up', 'new_v_w_down', 'new_v_final_norm']
TWIN_LEAF_KINDS = {'loss': 'loss', 'grad_x': 'grad_x', 'grad_rel_bias': 'grad_w', 'grad_attn_norm': 'grad_w', 'grad_mem_norm': 'grad_w', 'grad_w_in': 'grad_w', 'grad_b_gate': 'grad_w', 'grad_mla_q_norm': 'grad_w', 'grad_w_uq': 'grad_w', 'grad_mla_kv_norm': 'grad_w', 'grad_w_ukv': 'grad_w', 'grad_attn_sinks': 'grad_w', 'grad_w_mem_kv': 'grad_w', 'grad_w_o_mla': 'grad_w', 'grad_w_o_swa': 'grad_w', 'grad_w_o_mem': 'grad_w', 'grad_w_out': 'grad_w', 'grad_mlp_norm': 'grad_w', 'grad_w_up': 'grad_w', 'grad_w_down': 'grad_w', 'grad_final_norm': 'grad_w', 'delta_rel_bias': 'delta_w', 'delta_attn_norm': 'delta_w', 'delta_mem_norm': 'delta_w', 'delta_w_in': 'delta_w', 'delta_b_gate': 'delta_w', 'delta_mla_q_norm': 'delta_w', 'delta_w_uq': 'delta_w', 'delta_mla_kv_norm': 'delta_w', 'delta_w_ukv': 'delta_w', 'delta_attn_sinks': 'delta_w', 'delta_w_mem_kv': 'delta_w', 'delta_w_o_mla': 'delta_w', 'delta_w_o_swa': 'delta_w', 'delta_w_o_mem': 'delta_w', 'delta_w_out': 'delta_w', 'delta_mlp_norm': 'delta_w', 'delta_w_up': 'delta_w', 'delta_w_down': 'delta_w', 'delta_final_norm': 'delta_w', 'new_m_rel_bias': 'new_m', 'new_m_attn_norm': 'new_m', 'new_m_mem_norm': 'new_m', 'new_m_w_in': 'new_m', 'new_m_b_gate': 'new_m', 'new_m_mla_q_norm': 'new_m', 'new_m_w_uq': 'new_m', 'new_m_mla_kv_norm': 'new_m', 'new_m_w_ukv': 'new_m', 'new_m_attn_sinks': 'new_m', 'new_m_w_mem_kv': 'new_m', 'new_m_w_o_mla': 'new_m', 'new_m_w_o_swa': 'new_m', 'new_m_w_o_mem': 'new_m', 'new_m_w_out': 'new_m', 'new_m_mlp_norm': 'new_m', 'new_m_w_up': 'new_m', 'new_m_w_down': 'new_m', 'new_m_final_norm': 'new_m', 'new_v_rel_bias': 'new_v', 'new_v_attn_norm': 'new_v', 'new_v_mem_norm': 'new_v', 'new_v_w_in': 'new_v', 'new_v_b_gate': 'new_v', 'new_v_mla_q_norm': 'new_v', 'new_v_w_uq': 'new_v', 'new_v_mla_kv_norm': 'new_v', 'new_v_w_ukv': 'new_v', 'new_v_attn_sinks': 'new_v', 'new_v_w_mem_kv': 'new_v', 'new_v_w_o_mla': 'new_v', 'new_v_w_o_swa': 'new_v', 'new_v_w_o_mem': 'new_v', 'new_v_w_out': 'new_v', 'new_v_mlp_norm': 'new_v', 'new_v_w_up': 'new_v', 'new_v_w_down': 'new_v', 'new_v_final_norm': 'new_v'}


def _forward(args):
    return _fwd_reference(*[args[k] for k in FWD_PARAMS])


def _output_shape():
    def fwd():
        inp = _fwd_setup_inputs(0)
        return _fwd_reference(*[inp[k] for k in FWD_PARAMS])
    out = _jax.eval_shape(fwd)
    return out.shape, out.dtype

N_MICROBATCH = 1
ADAM_LR = 0.001
ADAM_B1 = 0.9
ADAM_B2 = 0.999
ADAM_EPS = 1e-08
ADAM_WD = 0.01
ADAM_STEP = 10
PER_EXAMPLE_BATCH_AXIS = {'x': 0, 'mem': 0, 'loss_target': 0}
SHARED_INPUTS = []
_WEIGHT_DTYPES = {'rel_bias': _jnp.float32, 'attn_norm': _jnp.float32, 'mem_norm': _jnp.float32, 'w_in': _jnp.float32, 'b_gate': _jnp.float32, 'mla_q_norm': _jnp.float32, 'w_uq': _jnp.float32, 'mla_kv_norm': _jnp.float32, 'w_ukv': _jnp.float32, 'attn_sinks': _jnp.float32, 'w_mem_kv': _jnp.float32, 'w_o_mla': _jnp.float32, 'w_o_swa': _jnp.float32, 'w_o_mem': _jnp.float32, 'w_out': _jnp.float32, 'mlp_norm': _jnp.float32, 'w_up': _jnp.float32, 'w_down': _jnp.float32, 'final_norm': _jnp.float32}
MOMENT_SCALE = {'rel_bias': 9.585898e-02, 'attn_norm': 1.000970e-01, 'mem_norm': 3.121926e-02, 'w_in': 4.713548e-02, 'b_gate': 1.928246e-02, 'mla_q_norm': 6.117369e-02, 'w_uq': 3.716782e-02, 'mla_kv_norm': 1.812595e-01, 'w_ukv': 5.808291e-02, 'attn_sinks': 5.112156e-02, 'w_mem_kv': 2.870661e-02, 'w_o_mla': 5.288856e-02, 'w_o_swa': 5.506281e-02, 'w_o_mem': 2.113832e-02, 'w_out': 7.565908e-02, 'mlp_norm': 3.118131e-01, 'w_up': 1.537018e-01, 'w_down': 3.726371e-01, 'final_norm': 1.310223e+02}


def _to_microbatches(a, axis):
    t = _jnp.moveaxis(a, axis, 0)
    t = t.reshape((N_MICROBATCH, t.shape[0] // N_MICROBATCH) + t.shape[1:])
    return _jnp.moveaxis(t, 1, axis + 1)


def setup_inputs(seed: int = 0) -> dict:
    inp = _fwd_setup_inputs(seed)
    key = _jax.random.fold_in(_jax.random.key(seed), 7919)
    shape, _ = _output_shape()
    out = dict(inp)
    out["loss_target"] = _jax.random.normal(_jax.random.fold_in(key, 0), shape, _jnp.float32)
    for i, name in enumerate(TWIN_WEIGHTS):
        w = inp[name].astype(_jnp.float32)
        if MOMENT_SCALE is None:
            s = _jnp.sqrt(_jnp.mean(_jnp.square(w)) + 1e-30)
        else:
            s = MOMENT_SCALE[name]
        km, kv = _jax.random.split(_jax.random.fold_in(key, i + 1))
        out[name] = w
        out["m_" + name] = s * _jax.random.normal(km, w.shape, _jnp.float32)
        out["v_" + name] = (s * s) * _jax.random.uniform(kv, w.shape, _jnp.float32, 0.5, 1.5)
    if N_MICROBATCH > 1:
        for name, axis in PER_EXAMPLE_BATCH_AXIS.items():
            out[name] = _to_microbatches(out[name], axis)
    return {'x': out['x'], 'mem': out['mem'], 'rel_bias': out['rel_bias'], 'attn_norm': out['attn_norm'], 'mem_norm': out['mem_norm'], 'w_in': out['w_in'], 'b_gate': out['b_gate'], 'mla_q_norm': out['mla_q_norm'], 'w_uq': out['w_uq'], 'mla_kv_norm': out['mla_kv_norm'], 'w_ukv': out['w_ukv'], 'attn_sinks': out['attn_sinks'], 'w_mem_kv': out['w_mem_kv'], 'w_o_mla': out['w_o_mla'], 'w_o_swa': out['w_o_swa'], 'w_o_mem': out['w_o_mem'], 'w_out': out['w_out'], 'mlp_norm': out['mlp_norm'], 'w_up': out['w_up'], 'w_down': out['w_down'], 'final_norm': out['final_norm'], 'loss_target': out['loss_target'], 'm_rel_bias': out['m_rel_bias'], 'm_attn_norm': out['m_attn_norm'], 'm_mem_norm': out['m_mem_norm'], 'm_w_in': out['m_w_in'], 'm_b_gate': out['m_b_gate'], 'm_mla_q_norm': out['m_mla_q_norm'], 'm_w_uq': out['m_w_uq'], 'm_mla_kv_norm': out['m_mla_kv_norm'], 'm_w_ukv': out['m_w_ukv'], 'm_attn_sinks': out['m_attn_sinks'], 'm_w_mem_kv': out['m_w_mem_kv'], 'm_w_o_mla': out['m_w_o_mla'], 'm_w_o_swa': out['m_w_o_swa'], 'm_w_o_mem': out['m_w_o_mem'], 'm_w_out': out['m_w_out'], 'm_mlp_norm': out['m_mlp_norm'], 'm_w_up': out['m_w_up'], 'm_w_down': out['m_w_down'], 'm_final_norm': out['m_final_norm'], 'v_rel_bias': out['v_rel_bias'], 'v_attn_norm': out['v_attn_norm'], 'v_mem_norm': out['v_mem_norm'], 'v_w_in': out['v_w_in'], 'v_b_gate': out['v_b_gate'], 'v_mla_q_norm': out['v_mla_q_norm'], 'v_w_uq': out['v_w_uq'], 'v_mla_kv_norm': out['v_mla_kv_norm'], 'v_w_ukv': out['v_w_ukv'], 'v_attn_sinks': out['v_attn_sinks'], 'v_w_mem_kv': out['v_w_mem_kv'], 'v_w_o_mla': out['v_w_o_mla'], 'v_w_o_swa': out['v_w_o_swa'], 'v_w_o_mem': out['v_w_o_mem'], 'v_w_out': out['v_w_out'], 'v_mlp_norm': out['v_mlp_norm'], 'v_w_up': out['v_w_up'], 'v_w_down': out['v_w_down'], 'v_final_norm': out['v_final_norm']}


def _loss(weights, diff, rest, loss_target):
    with _jax.named_scope("forward"):
        args = {**rest, TWIN_DIFF_INPUT: diff, **{k: w.astype(_WEIGHT_DTYPES[k]) for k, w in weights.items()}}
        y = _forward(args)
    with _jax.named_scope("loss_head"):
        err = _jnp.square(y.astype(_jnp.float32) - loss_target)
        return 0.5 * _jnp.sum(_jnp.mean(err, axis=-1)) if err.ndim else 0.5 * err


def _adamw(w, g, m, v):
    m = ADAM_B1 * m + (1.0 - ADAM_B1) * g
    v = ADAM_B2 * v + (1.0 - ADAM_B2) * _jnp.square(g)
    m_hat = m / (1.0 - ADAM_B1 ** ADAM_STEP)
    v_hat = v / (1.0 - ADAM_B2 ** ADAM_STEP)
    delta = -ADAM_LR * (m_hat / (_jnp.sqrt(v_hat) + ADAM_EPS) + ADAM_WD * w)
    return delta, m, v


def reference(x, mem, rel_bias, attn_norm, mem_norm, w_in, b_gate, mla_q_norm, w_uq, mla_kv_norm, w_ukv, attn_sinks, w_mem_kv, w_o_mla, w_o_swa, w_o_mem, w_out, mlp_norm, w_up, w_down, final_norm, loss_target, m_rel_bias, m_attn_norm, m_mem_norm, m_w_in, m_b_gate, m_mla_q_norm, m_w_uq, m_mla_kv_norm, m_w_ukv, m_attn_sinks, m_w_mem_kv, m_w_o_mla, m_w_o_swa, m_w_o_mem, m_w_out, m_mlp_norm, m_w_up, m_w_down, m_final_norm, v_rel_bias, v_attn_norm, v_mem_norm, v_w_in, v_b_gate, v_mla_q_norm, v_w_uq, v_mla_kv_norm, v_w_ukv, v_attn_sinks, v_w_mem_kv, v_w_o_mla, v_w_o_swa, v_w_o_mem, v_w_out, v_mlp_norm, v_w_up, v_w_down, v_final_norm):
    given = dict(x=x, mem=mem, rel_bias=rel_bias, attn_norm=attn_norm, mem_norm=mem_norm, w_in=w_in, b_gate=b_gate, mla_q_norm=mla_q_norm, w_uq=w_uq, mla_kv_norm=mla_kv_norm, w_ukv=w_ukv, attn_sinks=attn_sinks, w_mem_kv=w_mem_kv, w_o_mla=w_o_mla, w_o_swa=w_o_swa, w_o_mem=w_o_mem, w_out=w_out, mlp_norm=mlp_norm, w_up=w_up, w_down=w_down, final_norm=final_norm, loss_target=loss_target, m_rel_bias=m_rel_bias, m_attn_norm=m_attn_norm, m_mem_norm=m_mem_norm, m_w_in=m_w_in, m_b_gate=m_b_gate, m_mla_q_norm=m_mla_q_norm, m_w_uq=m_w_uq, m_mla_kv_norm=m_mla_kv_norm, m_w_ukv=m_w_ukv, m_attn_sinks=m_attn_sinks, m_w_mem_kv=m_w_mem_kv, m_w_o_mla=m_w_o_mla, m_w_o_swa=m_w_o_swa, m_w_o_mem=m_w_o_mem, m_w_out=m_w_out, m_mlp_norm=m_mlp_norm, m_w_up=m_w_up, m_w_down=m_w_down, m_final_norm=m_final_norm, v_rel_bias=v_rel_bias, v_attn_norm=v_attn_norm, v_mem_norm=v_mem_norm, v_w_in=v_w_in, v_b_gate=v_b_gate, v_mla_q_norm=v_mla_q_norm, v_w_uq=v_w_uq, v_mla_kv_norm=v_mla_kv_norm, v_w_ukv=v_w_ukv, v_attn_sinks=v_attn_sinks, v_w_mem_kv=v_w_mem_kv, v_w_o_mla=v_w_o_mla, v_w_o_swa=v_w_o_swa, v_w_o_mem=v_w_o_mem, v_w_out=v_w_out, v_mlp_norm=v_mlp_norm, v_w_up=v_w_up, v_w_down=v_w_down, v_final_norm=v_final_norm)
    weights = {n: given[n] for n in TWIN_WEIGHTS}
    shared = {n: given[n] for n in SHARED_INPUTS}
    per_example = {n: given[n] for n in ['x', 'mem']}
    grad_fn = _jax.value_and_grad(_loss, argnums=(0, 1))

    def one_microbatch(ex, loss_target):
        ex = dict(ex)
        diff = ex.pop(TWIN_DIFF_INPUT)
        return grad_fn(weights, diff, {**shared, **ex}, loss_target)

    if N_MICROBATCH == 1:
        loss, (grad_w, grad_x) = one_microbatch(per_example, given["loss_target"])
    else:
        def body(carry, xs):
            loss_sum, grad_sum = carry
            l_k, (gw_k, gx_k) = one_microbatch(xs[0], xs[1])
            with _jax.named_scope("update"):
                return (loss_sum + l_k, _jax.tree.map(_jnp.add, grad_sum, gw_k)), gx_k

        init = (_jnp.zeros((), _jnp.float32), _jax.tree.map(_jnp.zeros_like, weights))
        (loss, grad_w), grad_x = _jax.lax.scan(body, init, (per_example, given["loss_target"]))
    with _jax.named_scope("update"):
        delta_w, new_m, new_v = {}, {}, {}
        for n in TWIN_WEIGHTS:
            delta_w[n], new_m[n], new_v[n] = _adamw(weights[n], grad_w[n], given["m_" + n], given["v_" + n])
    return (loss, grad_x, *[grad_w[n] for n in TWIN_WEIGHTS], *[delta_w[n] for n in TWIN_WEIGHTS],
            *[new_m[n] for n in TWIN_WEIGHTS], *[new_v[n] for n in TWIN_WEIGHTS])
```

```python
import functools
import math

import numpy as np
import jax
import jax.numpy as jnp
from jax import lax
from jax.experimental import pallas as pl
from jax.experimental.pallas import tpu as pltpu

F32 = jnp.float32
BF16 = jnp.bfloat16

D = 1024
DFF = 4096
DEPTH = 2
EPS = 1e-6
LANE = 128
NH = 8
SWA_R = 4
MEM_H = 4
MEM_LEN = 256
WIN = 128
NEG = -1e30
MLA_SCALE = 96 ** -0.5
SWA_SCALE = 64 ** -0.5
MEM_SCALE = 128 ** -0.5
REL_BUCKETS = 32
ROPE_THETA = 10000.0

C_CQ, C_CKV, C_KPE, C_QS, C_KS, C_VS, C_QM, C_G, C_END = 0, 256, 384, 512, 1536, 1664, 1792, 2304, 5376
IN_COLS = 4768

ADAM_LR = 0.001
ADAM_B1 = 0.9
ADAM_B2 = 0.999
ADAM_EPS = 1e-08
ADAM_WD = 0.01
ADAM_STEP = 10

VMEM_LIMIT = 56 * 1024 * 1024

SHARDED = ("w_in", "w_uq", "w_ukv", "w_mem_kv", "w_o_mla", "w_o_swa", "w_o_mem", "w_out", "w_up", "w_down")
SHARD_AXIS = {"w_in": 2, "w_uq": 2, "w_ukv": 2, "w_mem_kv": 1, "w_o_mla": 2, "w_o_swa": 2, "w_o_mem": 2,
              "w_out": 1, "w_up": 2, "w_down": 1}
SMALL = ("rel_bias", "attn_norm", "mem_norm", "b_gate", "mla_q_norm", "mla_kv_norm", "attn_sinks", "mlp_norm",
         "final_norm")
WEIGHTS = ("rel_bias", "attn_norm", "mem_norm", "w_in", "b_gate", "mla_q_norm", "w_uq", "mla_kv_norm", "w_ukv",
           "attn_sinks", "w_mem_kv", "w_o_mla", "w_o_swa", "w_o_mem", "w_out", "mlp_norm", "w_up", "w_down",
           "final_norm")
FLAT_W = 1024
FLAT_TILE = 256
SMALL_ROWS = 16
MESH_ID = pl.DeviceIdType.MESH


def _pcall(body, *, name, grid, in_specs, out_specs, out_shape, scratch=(), prefetch=0, sem=None):
    params = pltpu.CompilerParams(dimension_semantics=sem, vmem_limit_bytes=VMEM_LIMIT)
    if prefetch:
        spec = pltpu.PrefetchScalarGridSpec(num_scalar_prefetch=prefetch, grid=grid, in_specs=in_specs,
                                            out_specs=out_specs, scratch_shapes=scratch)
        return pl.pallas_call(body, name=name, grid_spec=spec, out_shape=out_shape, compiler_params=params)
    return pl.pallas_call(body, name=name, grid=grid, in_specs=in_specs, out_specs=out_specs, out_shape=out_shape,
                          scratch_shapes=scratch, compiler_params=params)


def _tok(ts, w):
    return pl.BlockSpec((ts, w), lambda i: (i, 0))


def _full(*shape):
    return pl.BlockSpec(shape, lambda *_: (0,) * len(shape))


def _sds(shape, dtype):
    return jax.ShapeDtypeStruct(shape, dtype)


def _dot(a, b):
    return jnp.dot(a, b, preferred_element_type=F32)


def _dot_nt(a, b):
    return lax.dot_general(a, b, (((1,), (1,)), ((), ())), preferred_element_type=F32)


def _dot_tn(a, b):
    return lax.dot_general(a, b, (((0,), (0,)), ((), ())), preferred_element_type=F32)


def _rms(x):
    r = lax.rsqrt(jnp.mean(x * x, axis=-1, keepdims=True) + EPS)
    return x * r, r


def _rms_bwd(dyg, n, r):
    return r * (dyg - n * jnp.mean(n * dyg, axis=-1, keepdims=True))


def _rope(t, c, s1, s2):
    return t * c + pltpu.roll(t, 16, 1) * s1 + pltpu.roll(t, LANE - 16, 1) * s2


def _rope_bwd(dy, c, s1, s2):
    return dy * c + pltpu.roll(dy * s1, LANE - 16, 1) + pltpu.roll(dy * s2, 16, 1)


def _hs(h):
    return slice(h * LANE, (h + 1) * LANE)


def _colsum(t):
    return jnp.sum(t, axis=0, keepdims=True)


def _pre_fwd(x, an, win, bg, qn, kvn, wuq, wukv, rc, rs1, rs2, ts):
    S = x.shape[0]

    def body(x_ref, an_ref, win_ref, bg_ref, qn_ref, kvn_ref, wuq_ref, wukv_ref, rc_ref, rs1_ref, rs2_ref,
             q_ref, k_ref, v_ref, qs_ref, ks_ref, vs_ref, qm_ref, g_ref):
        n, _ = _rms(x_ref[...])
        hb = (n * an_ref[...]).astype(BF16)
        pa = _dot(hb, win_ref[:, C_CQ:C_QS])
        ncq, _ = _rms(pa[:, 0:256])
        cqn = (ncq * qn_ref[...]).astype(BF16)
        nkv, _ = _rms(pa[:, 256:384])
        ckvn = (nkv * kvn_ref[...]).astype(BF16)
        c, s1, s2 = rc_ref[...], rs1_ref[...], rs2_ref[...]
        kper = _rope(pa[:, 384:512], c, s1, s2)
        qp = _dot(cqn, wuq_ref[...])
        kv = _dot(ckvn, wukv_ref[...])
        for h in range(NH):
            q_ref[:, _hs(h)] = _rope(qp[:, _hs(h)], c, s1, s2).astype(BF16)
            k_ref[:, _hs(h)] = (kv[:, _hs(h)] + kper).astype(BF16)
        v_ref[...] = kv[:, NH * LANE:].astype(BF16)
        pb = _dot(hb, win_ref[:, C_QS:C_G])
        qs_ref[...] = pb[:, 0:1024].astype(BF16)
        ks_ref[...] = pb[:, 1024:1152].astype(BF16)
        vs_ref[...] = pb[:, 1152:1280].astype(BF16)
        qm_ref[...] = pb[:, 1280:1792].astype(BF16)
        g_ref[...] = jax.nn.sigmoid(_dot(hb, win_ref[:, C_G:C_END]) + bg_ref[...])

    return _pcall(
        body, name="pre_fwd", grid=(S // ts,),
        in_specs=[_tok(ts, D), _full(1, D), _full(D, C_END), _full(1, 3 * D), _full(1, 256), _full(1, 128),
                  _full(256, NH * LANE), _full(128, 2 * NH * LANE), _tok(ts, LANE), _tok(ts, LANE), _tok(ts, LANE)],
        out_specs=[_tok(ts, 1024), _tok(ts, 1024), _tok(ts, 1024), _tok(ts, 1024), _tok(ts, 128), _tok(ts, 128),
                   _tok(ts, 512), _tok(ts, 3 * D)],
        out_shape=[_sds((S, 1024), BF16), _sds((S, 1024), BF16), _sds((S, 1024), BF16), _sds((S, 1024), BF16),
                   _sds((S, 128), BF16), _sds((S, 128), BF16), _sds((S, 512), BF16), _sds((S, 3 * D), F32)],
        sem=("arbitrary",),
    )(x, an, win, bg, qn, kvn, wuq, wukv, rc, rs1, rs2)


def _merge_fwd(x, g, oa, ob, oc, wa, wb, wc, wout, ts):
    S = x.shape[0]

    def body(x_ref, g_ref, oa_ref, ob_ref, oc_ref, wa_ref, wb_ref, wc_ref, wout_ref, x1_ref, yb_ref):
        y = g_ref[:, 0:D] * _dot(oa_ref[...], wa_ref[...])
        y = y + g_ref[:, D:2 * D] * _dot(ob_ref[...], wb_ref[...])
        y = y + g_ref[:, 2 * D:3 * D] * _dot(oc_ref[...], wc_ref[...])
        yb = y.astype(BF16)
        yb_ref[...] = yb
        x1_ref[...] = x_ref[...] + _dot(yb, wout_ref[...])

    return _pcall(
        body, name="merge_fwd", grid=(S // ts,),
        in_specs=[_tok(ts, D), _tok(ts, 3 * D), _tok(ts, 1024), _tok(ts, 1024), _tok(ts, 512),
                  _full(1024, D), _full(1024, D), _full(512, D), _full(D, D)],
        out_specs=[_tok(ts, D), _tok(ts, D)],
        out_shape=[_sds((S, D), F32), _sds((S, D), BF16)],
        sem=("arbitrary",),
    )(x, g, oa, ob, oc, wa, wb, wc, wout)


def _mlp_fwd(x1, mn, wup, wdown, ts):
    S = x1.shape[0]

    def body(x_ref, mn_ref, wup_ref, wdown_ref, x2_ref):
        xv = x_ref[...]
        n, _ = _rms(xv)
        u = _dot((n * mn_ref[...]).astype(BF16), wup_ref[...])
        a = jnp.square(jnp.maximum(u, 0.0))
        x2_ref[...] = xv + _dot(a.astype(BF16), wdown_ref[...])

    return _pcall(
        body, name="mlp_fwd", grid=(S // ts,),
        in_specs=[_tok(ts, D), _full(1, D), _full(D, DFF), _full(DFF, D)],
        out_specs=_tok(ts, D), out_shape=_sds((S, D), F32), sem=("arbitrary",),
    )(x1, mn, wup, wdown)


def _loss_kernel(x, fn, tgt, ts):
    S = x.shape[0]

    def body(x_ref, fn_ref, t_ref, loss_ref, dx_ref, dfn_ref):
        @pl.when(pl.program_id(0) == 0)
        def _():
            loss_ref[...] = jnp.zeros_like(loss_ref)
            dfn_ref[...] = jnp.zeros_like(dfn_ref)

        n, r = _rms(x_ref[...])
        err = n * fn_ref[...] - t_ref[...]
        loss_ref[...] += jnp.sum(err * err)
        dy = err * (1.0 / D)
        dfn_ref[...] += _colsum(dy * n)
        dx_ref[...] = _rms_bwd(dy * fn_ref[...], n, r)

    return _pcall(
        body, name="loss_head", grid=(S // ts,),
        in_specs=[_tok(ts, D), _full(1, D), _tok(ts, D)],
        out_specs=[_full(1, LANE), _tok(ts, D), _full(1, D)],
        out_shape=[_sds((1, LANE), F32), _sds((S, D), F32), _sds((1, D), F32)],
        sem=("arbitrary",),
    )(x, fn, tgt)


def _mlp_bwd(dx2, x1, mn, wup, wdown, ts):
    S = x1.shape[0]

    def body(dx_ref, x_ref, mn_ref, wup_ref, wdown_ref, dx1_ref, hb_ref, dub_ref, ab_ref, dxb_ref, dmn_ref):
        @pl.when(pl.program_id(0) == 0)
        def _():
            dmn_ref[...] = jnp.zeros_like(dmn_ref)

        dx = dx_ref[...]
        n, r = _rms(x_ref[...])
        g = mn_ref[...]
        hb = (n * g).astype(BF16)
        hb_ref[...] = hb
        rl = jnp.maximum(_dot(hb, wup_ref[...]), 0.0)
        ab_ref[...] = jnp.square(rl).astype(BF16)
        dxb = dx.astype(BF16)
        dxb_ref[...] = dxb
        dub = (_dot_nt(dxb, wdown_ref[...]) * (2.0 * rl)).astype(BF16)
        dub_ref[...] = dub
        dh = _dot_nt(dub, wup_ref[...])
        dmn_ref[...] += _colsum(dh * n)
        dx1_ref[...] = dx + _rms_bwd(dh * g, n, r)

    return _pcall(
        body, name="mlp_bwd", grid=(S // ts,),
        in_specs=[_tok(ts, D), _tok(ts, D), _full(1, D), _full(D, DFF), _full(DFF, D)],
        out_specs=[_tok(ts, D), _tok(ts, D), _tok(ts, DFF), _tok(ts, DFF), _tok(ts, D), _full(1, D)],
        out_shape=[_sds((S, D), F32), _sds((S, D), BF16), _sds((S, DFF), BF16), _sds((S, DFF), BF16),
                   _sds((S, D), BF16), _sds((1, D), F32)],
        sem=("arbitrary",),
    )(dx2, x1, mn, wup, wdown)


def _merge_bwd(dx1, g, oa, ob, oc, wa, wb, wc, wout, ts):
    S = dx1.shape[0]

    def body(dx_ref, g_ref, oa_ref, ob_ref, oc_ref, wa_ref, wb_ref, wc_ref, wout_ref,
             dgp_ref, dyo_ref, doa_ref, dob_ref, doc_ref, dla_ref, dxb_ref, dbg_ref):
        @pl.when(pl.program_id(0) == 0)
        def _():
            dbg_ref[...] = jnp.zeros_like(dbg_ref)

        dxb = dx_ref[...].astype(BF16)
        dxb_ref[...] = dxb
        dy = _dot_nt(dxb, wout_ref[...])
        branches = ((oa_ref, wa_ref, doa_ref), (ob_ref, wb_ref, dob_ref), (oc_ref, wc_ref, doc_ref))
        for b, (o_ref, w_ref, do_ref) in enumerate(branches):
            cols = slice(b * D, (b + 1) * D)
            gb = g_ref[:, cols]
            o = o_ref[...]
            dgpre = dy * _dot(o, w_ref[...]) * gb * (1.0 - gb)
            dgp_ref[:, cols] = dgpre.astype(BF16)
            dbg_ref[:, cols] += _colsum(dgpre)
            dyo = (dy * gb).astype(BF16)
            dyo_ref[:, cols] = dyo
            do = _dot_nt(dyo, w_ref[...])
            do_ref[...] = do.astype(BF16)
            if b == 0:
                for h in range(NH):
                    dl = jnp.sum(do[:, _hs(h)] * o[:, _hs(h)].astype(F32), axis=1, keepdims=True)
                    dla_ref[:, _hs(h)] = jnp.broadcast_to(dl, (ts, LANE))

    return _pcall(
        body, name="merge_bwd", grid=(S // ts,),
        in_specs=[_tok(ts, D), _tok(ts, 3 * D), _tok(ts, 1024), _tok(ts, 1024), _tok(ts, 512),
                  _full(1024, D), _full(1024, D), _full(512, D), _full(D, D)],
        out_specs=[_tok(ts, 3 * D), _tok(ts, 3 * D), _tok(ts, 1024), _tok(ts, 1024), _tok(ts, 512), _tok(ts, 1024),
                   _tok(ts, D), _full(1, 3 * D)],
        out_shape=[_sds((S, 3 * D), BF16), _sds((S, 3 * D), BF16), _sds((S, 1024), BF16), _sds((S, 1024), BF16),
                   _sds((S, 512), BF16), _sds((S, 1024), F32), _sds((S, D), BF16), _sds((1, 3 * D), F32)],
        sem=("arbitrary",),
    )(dx1, g, oa, ob, oc, wa, wb, wc, wout)


def _pre_bwd(x, dx1, dq, dk, dv, dqs, dks, dvs, dqm, dgp, an, qn, kvn, win, wuq, wukv, rc, rs1, rs2, ts):
    S = x.shape[0]

    def body(x_ref, dx1_ref, dq_ref, dk_ref, dv_ref, dqs_ref, dks_ref, dvs_ref, dqm_ref, dgp_ref,
             an_ref, qn_ref, kvn_ref, win_ref, wuq_ref, wukv_ref, rc_ref, rs1_ref, rs2_ref,
             dx_ref, dproj_ref, hb_ref, cqn_ref, ckvn_ref, dqpre_ref, dkv_ref, dan_ref, dqn_ref, dkvn_ref):
        @pl.when(pl.program_id(0) == 0)
        def _():
            dan_ref[...] = jnp.zeros_like(dan_ref)
            dqn_ref[...] = jnp.zeros_like(dqn_ref)
            dkvn_ref[...] = jnp.zeros_like(dkvn_ref)

        n, r = _rms(x_ref[...])
        hb = (n * an_ref[...]).astype(BF16)
        hb_ref[...] = hb
        pa = _dot(hb, win_ref[:, C_CQ:C_KPE])
        ncq, rq = _rms(pa[:, 0:256])
        cqn_ref[...] = (ncq * qn_ref[...]).astype(BF16)
        nkv, rkv = _rms(pa[:, 256:384])
        ckvn_ref[...] = (nkv * kvn_ref[...]).astype(BF16)
        c, s1, s2 = rc_ref[...], rs1_ref[...], rs2_ref[...]

        dkper = jnp.zeros((ts, LANE), F32)
        for h in range(NH):
            dqpre_ref[:, _hs(h)] = _rope_bwd(dq_ref[:, _hs(h)], c, s1, s2).astype(BF16)
            dkh = dk_ref[:, _hs(h)]
            dkper = dkper + dkh
            dkv_ref[:, _hs(h)] = dkh.astype(BF16)
        dkv_ref[:, NH * LANE:] = dv_ref[...].astype(BF16)

        dcqn = _dot_nt(dqpre_ref[...], wuq_ref[...])
        dqn_ref[...] += _colsum(dcqn * ncq)
        dproj_ref[:, C_CQ:C_CKV] = _rms_bwd(dcqn * qn_ref[...], ncq, rq).astype(BF16)
        dckvn = _dot_nt(dkv_ref[...], wukv_ref[...])
        dkvn_ref[...] += _colsum(dckvn * nkv)
        dproj_ref[:, C_CKV:C_KPE] = _rms_bwd(dckvn * kvn_ref[...], nkv, rkv).astype(BF16)
        lane = lax.broadcasted_iota(jnp.int32, (ts, LANE), 1)
        dkpe = jnp.where((lane >= 64) & (lane < 96), _rope_bwd(dkper, c, s1, s2), 0.0)
        dproj_ref[:, C_KPE:C_QS] = dkpe.astype(BF16)
        dproj_ref[:, C_QS:C_KS] = dqs_ref[...]
        dproj_ref[:, C_KS:C_VS] = dks_ref[...].astype(BF16)
        dproj_ref[:, C_VS:C_QM] = dvs_ref[...].astype(BF16)
        dproj_ref[:, C_QM:C_G] = dqm_ref[...]
        dproj_ref[:, C_G:C_END] = dgp_ref[...]

        dh = _dot_nt(dproj_ref[...], win_ref[...])
        dan_ref[...] += _colsum(dh * n)
        dx_ref[...] = dx1_ref[...] + _rms_bwd(dh * an_ref[...], n, r)

    return _pcall(
        body, name="pre_bwd", grid=(S // ts,),
        in_specs=[_tok(ts, D), _tok(ts, D), _tok(ts, 1024), _tok(ts, 1024), _tok(ts, 1024), _tok(ts, 1024),
                  _tok(ts, 128), _tok(ts, 128), _tok(ts, 512), _tok(ts, 3 * D),
                  _full(1, D), _full(1, 256), _full(1, 128), _full(D, C_END), _full(256, NH * LANE),
                  _full(128, 2 * NH * LANE), _tok(ts, LANE), _tok(ts, LANE), _tok(ts, LANE)],
        out_specs=[_tok(ts, D), _tok(ts, C_END), _tok(ts, D), _tok(ts, 256), _tok(ts, 128), _tok(ts, 1024),
                   _tok(ts, 2048), _full(1, D), _full(1, 256), _full(1, 128)],
        out_shape=[_sds((S, D), F32), _sds((S, C_END), BF16), _sds((S, D), BF16), _sds((S, 256), BF16),
                   _sds((S, 128), BF16), _sds((S, 1024), BF16), _sds((S, 2048), BF16), _sds((1, D), F32),
                   _sds((1, 256), F32), _sds((1, 128), F32)],
        sem=("arbitrary",),
    )(x, dx1, dq, dk, dv, dqs, dks, dvs, dqm, dgp, an, qn, kvn, win, wuq, wukv, rc, rs1, rs2)


def _pick_tile(n, cap):
    best = LANE
    for t in range(LANE, min(n, cap) + 1, LANE):
        if n % t == 0:
            best = t
    return best


def _matmul_tn(a, b, name):
    S, M = a.shape
    N = b.shape[1]
    tm = _pick_tile(M, 1024)
    tn = _pick_tile(N, 2048)
    ts = min(S, 1024)

    def body(a_ref, b_ref, o_ref):
        @pl.when(pl.program_id(2) == 0)
        def _():
            o_ref[...] = jnp.zeros_like(o_ref)

        o_ref[...] += _dot_tn(a_ref[...], b_ref[...])

    return _pcall(
        body, name=name, grid=(M // tm, N // tn, S // ts),
        in_specs=[pl.BlockSpec((ts, tm), lambda i, j, k: (k, i)), pl.BlockSpec((ts, tn), lambda i, j, k: (k, j))],
        out_specs=pl.BlockSpec((tm, tn), lambda i, j, k: (i, j)),
        out_shape=_sds((M, N), F32), sem=("parallel", "parallel", "arbitrary"),
    )(a, b)


def _mla_fwd(q, k, v, tq):
    S = q.shape[0]
    nq = S // tq
    pairs = [(i, j) for i in range(nq) for j in range(i + 1)]
    qi = jnp.asarray(np.array([p[0] for p in pairs], np.int32))
    kj = jnp.asarray(np.array([p[1] for p in pairs], np.int32))

    def body(qi_ref, kj_ref, q_ref, k_ref, v_ref, o_ref, lse_ref, m_s, l_s, acc_s):
        t = pl.program_id(1)
        i, j = qi_ref[t], kj_ref[t]

        @pl.when(j == 0)
        def _():
            m_s[...] = jnp.full_like(m_s, NEG)
            l_s[...] = jnp.zeros_like(l_s)
            acc_s[...] = jnp.zeros_like(acc_s)

        def step(masked):
            s = _dot_nt(q_ref[...], k_ref[...]) * MLA_SCALE
            if masked:
                row = lax.broadcasted_iota(jnp.int32, (tq, tq), 0)
                col = lax.broadcasted_iota(jnp.int32, (tq, tq), 1)
                s = jnp.where(col <= row, s, NEG)
            m_old = m_s[...]
            m_new = jnp.maximum(m_old, jnp.max(s, axis=1, keepdims=True))
            alpha = jnp.exp(m_old - m_new)
            p = jnp.exp(s - m_new)
            l_s[...] = alpha * l_s[...] + jnp.sum(p, axis=1, keepdims=True)
            acc_s[...] = alpha * acc_s[...] + _dot(p.astype(BF16), v_ref[...])
            m_s[...] = m_new

        @pl.when(j < i)
        def _():
            step(False)

        @pl.when(j == i)
        def _():
            step(True)
            l = l_s[...]
            o_ref[...] = (acc_s[...] / l).astype(BF16)
            lse_ref[...] = jnp.broadcast_to(m_s[...] + jnp.log(l), (tq, LANE))

    qmap = lambda h, t, qi_r, kj_r: (qi_r[t], h)
    kmap = lambda h, t, qi_r, kj_r: (kj_r[t], h)
    return _pcall(
        body, name="mla_fwd", grid=(NH, len(pairs)), prefetch=2,
        in_specs=[pl.BlockSpec((tq, LANE), qmap), pl.BlockSpec((tq, LANE), kmap), pl.BlockSpec((tq, LANE), kmap)],
        out_specs=[pl.BlockSpec((tq, LANE), qmap), pl.BlockSpec((tq, LANE), qmap)],
        out_shape=[_sds((S, NH * LANE), BF16), _sds((S, NH * LANE), F32)],
        scratch=[pltpu.VMEM((tq, 1), F32), pltpu.VMEM((tq, 1), F32), pltpu.VMEM((tq, LANE), F32)],
        sem=("arbitrary", "arbitrary"),
    )(qi, kj, q, k, v)


def _mla_bwd(q, k, v, do, lse, delta, tq):
    S = q.shape[0]
    nq = S // tq
    pairs = [(i, j) for j in range(nq) for i in range(j, nq)]
    qi = jnp.asarray(np.array([p[0] for p in pairs], np.int32))
    kj = jnp.asarray(np.array([p[1] for p in pairs], np.int32))

    def body(qi_ref, kj_ref, q_ref, k_ref, v_ref, do_ref, lse_ref, dl_ref, dq_ref, dk_ref, dv_ref,
             dq_s, dk_s, dv_s):
        t = pl.program_id(1)
        i, j = qi_ref[t], kj_ref[t]

        @pl.when(t == 0)
        def _():
            dq_s[...] = jnp.zeros_like(dq_s)

        @pl.when(i == j)
        def _():
            dk_s[...] = jnp.zeros_like(dk_s)
            dv_s[...] = jnp.zeros_like(dv_s)

        rows = pl.ds(pl.multiple_of(i * tq, tq), tq)

        def step(masked):
            qb, kb, dob = q_ref[...], k_ref[...], do_ref[...]
            s = _dot_nt(qb, kb) * MLA_SCALE
            p = jnp.exp(s - lse_ref[:, 0:1])
            if masked:
                row = lax.broadcasted_iota(jnp.int32, (tq, tq), 0)
                col = lax.broadcasted_iota(jnp.int32, (tq, tq), 1)
                p = jnp.where(col <= row, p, 0.0)
            dv_s[...] += _dot_tn(p.astype(BF16), dob)
            dp = _dot_nt(dob, v_ref[...])
            ds = (p * (dp - dl_ref[:, 0:1]) * MLA_SCALE).astype(BF16)
            dk_s[...] += _dot_tn(ds, qb)
            dq_s[rows, :] += _dot(ds, kb)

        @pl.when(i > j)
        def _():
            step(False)

        @pl.when(i == j)
        def _():
            step(True)
            dq_ref[...] = dq_s[rows, :]

        @pl.when(i == nq - 1)
        def _():
            dk_ref[...] = dk_s[...]
            dv_ref[...] = dv_s[...]

    qmap = lambda h, t, qi_r, kj_r: (qi_r[t], h)
    kmap = lambda h, t, qi_r, kj_r: (kj_r[t], h)
    return _pcall(
        body, name="mla_bwd", grid=(NH, len(pairs)), prefetch=2,
        in_specs=[pl.BlockSpec((tq, LANE), qmap), pl.BlockSpec((tq, LANE), kmap), pl.BlockSpec((tq, LANE), kmap),
                  pl.BlockSpec((tq, LANE), qmap), pl.BlockSpec((tq, LANE), qmap), pl.BlockSpec((tq, LANE), qmap)],
        out_specs=[pl.BlockSpec((tq, LANE), kmap), pl.BlockSpec((tq, LANE), kmap), pl.BlockSpec((tq, LANE), kmap)],
        out_shape=[_sds((S, NH * LANE), F32), _sds((S, NH * LANE), F32), _sds((S, NH * LANE), F32)],
        scratch=[pltpu.VMEM((S, LANE), F32), pltpu.VMEM((tq, LANE), F32), pltpu.VMEM((tq, LANE), F32)],
        sem=("arbitrary", "arbitrary"),
    )(qi, kj, q, k, v, do, lse, delta)


SWA_SUB = 4
SWA_T = SWA_SUB * WIN


def _swa_specs(nsteps, rev):
    step = (lambda i: nsteps - 1 - i) if rev else (lambda i: i)
    cur = lambda w: pl.BlockSpec((SWA_T, w), lambda i: (step(i), 0))
    prev = pl.BlockSpec((WIN, LANE), lambda i: (jnp.maximum(step(i) * SWA_SUB - 1, 0), 0))
    return step, cur, prev


def _swa_probs(qh, kkb, bias_h, sink, first_mask):
    s = _dot_nt(qh, kkb) * SWA_SCALE + bias_h
    if first_mask is not None:
        s = jnp.where(first_mask, NEG, s)
    m = jnp.maximum(jnp.max(s, axis=1, keepdims=True), sink)
    e = jnp.exp(s - m)
    es = jnp.exp(sink - m)
    inv = 1.0 / (jnp.sum(e, axis=1, keepdims=True) + es)
    return e * inv, es * inv


def _swa_fwd(qs, ks, vs, bias, sinks):
    S = qs.shape[0]
    nsteps = S // SWA_T
    step, cur, prev = _swa_specs(nsteps, False)

    def body(qs_ref, kc_ref, kp_ref, vc_ref, vp_ref, bias_ref, sk_ref, o_ref):
        first = pl.program_id(0) == 0
        kk = jnp.concatenate([kp_ref[...], kc_ref[...]], axis=0)
        vv = jnp.concatenate([vp_ref[...], vc_ref[...]], axis=0)
        col = lax.broadcasted_iota(jnp.int32, (WIN, 2 * WIN), 1)
        for b in range(SWA_SUB):
            kkb = kk[b * WIN:(b + 2) * WIN]
            vvb = vv[b * WIN:(b + 2) * WIN]
            fm = (first & (col < WIN)) if b == 0 else None
            for h in range(NH):
                qh = qs_ref[b * WIN:(b + 1) * WIN, _hs(h)]
                p, _ = _swa_probs(qh, kkb, bias_ref[h], sk_ref[h:h + 1, 0:1], fm)
                o_ref[b * WIN:(b + 1) * WIN, _hs(h)] = _dot(p.astype(BF16), vvb).astype(BF16)

    return _pcall(
        body, name="swa_fwd", grid=(nsteps,),
        in_specs=[cur(NH * LANE), cur(LANE), prev, cur(LANE), prev, _full(NH, WIN, 2 * WIN), _full(NH, LANE)],
        out_specs=cur(NH * LANE), out_shape=_sds((S, NH * LANE), BF16), sem=("arbitrary",),
    )(qs, ks, ks, vs, vs, bias, sinks)


def _swa_bwd(qs, ks, vs, do, bias, sinks):
    S = qs.shape[0]
    nsteps = S // SWA_T
    step, cur, prev = _swa_specs(nsteps, True)

    def body(qs_ref, kc_ref, kp_ref, vc_ref, vp_ref, do_ref, bias_ref, sk_ref,
             dqs_ref, dks_ref, dvs_ref, dbias_ref, dsk_ref, dkk_s, dvv_s, ck_s, cv_s):
        pid = pl.program_id(0)
        first = step(pid) == 0

        @pl.when(pid == 0)
        def _():
            dbias_ref[...] = jnp.zeros_like(dbias_ref)
            dsk_ref[...] = jnp.zeros_like(dsk_ref)
            ck_s[...] = jnp.zeros_like(ck_s)
            cv_s[...] = jnp.zeros_like(cv_s)

        dkk_s[...] = jnp.zeros_like(dkk_s)
        dvv_s[...] = jnp.zeros_like(dvv_s)
        kk = jnp.concatenate([kp_ref[...], kc_ref[...]], axis=0)
        vv = jnp.concatenate([vp_ref[...], vc_ref[...]], axis=0)
        col = lax.broadcasted_iota(jnp.int32, (WIN, 2 * WIN), 1)
        for b in range(SWA_SUB):
            kkb = kk[b * WIN:(b + 2) * WIN]
            vvb = vv[b * WIN:(b + 2) * WIN]
            fm = (first & (col < WIN)) if b == 0 else None
            rows = slice(b * WIN, (b + 1) * WIN)
            keys = slice(b * WIN, (b + 2) * WIN)
            for h in range(NH):
                qh = qs_ref[rows, _hs(h)]
                doh = do_ref[rows, _hs(h)]
                p, ps = _swa_probs(qh, kkb, bias_ref[h], sk_ref[h:h + 1, 0:1], fm)
                dp = _dot_nt(doh, vvb)
                dl = jnp.sum(p * dp, axis=1, keepdims=True)
                ds = p * (dp - dl)
                dsk_ref[h:h + 1, :] += -jnp.sum(ps * dl)
                dbias_ref[h] += ds
                dsb = (ds * SWA_SCALE).astype(BF16)
                dqs_ref[rows, _hs(h)] = _dot(dsb, kkb).astype(BF16)
                dkk_s[keys, :] += _dot_tn(dsb, qh)
                dvv_s[keys, :] += _dot_tn(p.astype(BF16), doh)
        dks_ref[...] = dkk_s[WIN:, :]
        dvs_ref[...] = dvv_s[WIN:, :]
        dks_ref[SWA_T - WIN:, :] += ck_s[...]
        dvs_ref[SWA_T - WIN:, :] += cv_s[...]
        ck_s[...] = dkk_s[0:WIN, :]
        cv_s[...] = dvv_s[0:WIN, :]

    return _pcall(
        body, name="swa_bwd", grid=(nsteps,),
        in_specs=[cur(NH * LANE), cur(LANE), prev, cur(LANE), prev, cur(NH * LANE), _full(NH, WIN, 2 * WIN),
                  _full(NH, LANE)],
        out_specs=[cur(NH * LANE), cur(LANE), cur(LANE), _full(NH, WIN, 2 * WIN), _full(NH, LANE)],
        out_shape=[_sds((S, NH * LANE), BF16), _sds((S, LANE), F32), _sds((S, LANE), F32),
                   _sds((NH, WIN, 2 * WIN), F32), _sds((NH, LANE), F32)],
        scratch=[pltpu.VMEM((SWA_T + WIN, LANE), F32), pltpu.VMEM((SWA_T + WIN, LANE), F32),
                 pltpu.VMEM((WIN, LANE), F32), pltpu.VMEM((WIN, LANE), F32)],
        sem=("arbitrary",),
    )(qs, ks, ks, vs, vs, do, bias, sinks)


def _bias_build(rel_bias, bmap):
    def body(rb_ref, bmap_ref, o_ref):
        bm = bmap_ref[...]
        for h in range(NH):
            acc = jnp.full((WIN, 2 * WIN), NEG, F32)
            for b in range(REL_BUCKETS):
                acc = jnp.where(bm == b, rb_ref[b, h], acc)
            o_ref[h] = acc

    return _pcall(
        body, name="bias_build", grid=(1,),
        in_specs=[pl.BlockSpec(memory_space=pltpu.SMEM), _full(WIN, 2 * WIN)],
        out_specs=_full(NH, WIN, 2 * WIN), out_shape=_sds((NH, WIN, 2 * WIN), F32), sem=("arbitrary",),
    )(rel_bias, bmap)


def _bias_reduce(dbias, bmap):
    def body(db_ref, bmap_ref, o_ref):
        bm = bmap_ref[...]
        for h in range(NH):
            dbh = db_ref[h]
            for b in range(REL_BUCKETS):
                o_ref[b, h] = jnp.sum(jnp.where(bm == b, dbh, 0.0))

    return _pcall(
        body, name="bias_reduce", grid=(1,),
        in_specs=[_full(NH, WIN, 2 * WIN), _full(WIN, 2 * WIN)],
        out_specs=pl.BlockSpec(memory_space=pltpu.SMEM), out_shape=_sds((REL_BUCKETS, NH), F32), sem=("arbitrary",),
    )(dbias, bmap)


def _memkv_fwd(mem, mnorm, wkv):
    def body(mem_ref, g_ref, w_ref, o_ref):
        n, _ = _rms(mem_ref[...])
        o_ref[...] = _dot((n * g_ref[...]).astype(BF16), w_ref[...]).astype(BF16)

    return _pcall(
        body, name="memkv_fwd", grid=(1,), in_specs=[_full(MEM_LEN, D), _full(1, D), _full(D, D)],
        out_specs=_full(MEM_LEN, D), out_shape=_sds((MEM_LEN, D), BF16), sem=("arbitrary",),
    )(mem, mnorm, wkv)


def _mem_probs(qh, kh):
    s = _dot_nt(qh, kh) * MEM_SCALE
    e = jnp.exp(s - jnp.max(s, axis=1, keepdims=True))
    return e / jnp.sum(e, axis=1, keepdims=True)


def _mem_fwd(qm, kvm, ts):
    S = qm.shape[0]

    def body(q_ref, kv_ref, o_ref):
        for h in range(MEM_H):
            p = _mem_probs(q_ref[:, _hs(h)], kv_ref[:, _hs(h)])
            o_ref[:, _hs(h)] = _dot(p.astype(BF16), kv_ref[:, _hs(MEM_H + h)]).astype(BF16)

    return _pcall(
        body, name="mem_fwd", grid=(S // ts,), in_specs=[_tok(ts, 512), _full(MEM_LEN, D)],
        out_specs=_tok(ts, 512), out_shape=_sds((S, 512), BF16), sem=("arbitrary",),
    )(qm, kvm)


def _mem_bwd(qm, kvm, do, ts):
    S = qm.shape[0]

    def body(q_ref, kv_ref, do_ref, dq_ref, dkv_ref):
        @pl.when(pl.program_id(0) == 0)
        def _():
            dkv_ref[...] = jnp.zeros_like(dkv_ref)

        for h in range(MEM_H):
            qh, kh, vh, doh = q_ref[:, _hs(h)], kv_ref[:, _hs(h)], kv_ref[:, _hs(MEM_H + h)], do_ref[:, _hs(h)]
            p = _mem_probs(qh, kh)
            dp = _dot_nt(doh, vh)
            ds = (p * (dp - jnp.sum(p * dp, axis=1, keepdims=True)) * MEM_SCALE).astype(BF16)
            dq_ref[:, _hs(h)] = _dot(ds, kh).astype(BF16)
            dkv_ref[:, _hs(h)] += _dot_tn(ds, qh)
            dkv_ref[:, _hs(MEM_H + h)] += _dot_tn(p.astype(BF16), doh)

    return _pcall(
        body, name="mem_bwd", grid=(S // ts,), in_specs=[_tok(ts, 512), _full(MEM_LEN, D), _tok(ts, 512)],
        out_specs=[_tok(ts, 512), _full(MEM_LEN, D)],
        out_shape=[_sds((S, 512), BF16), _sds((MEM_LEN, D), F32)], sem=("arbitrary",),
    )(qm, kvm, do)


def _memkv_bwd(mem, mnorm, wkv, dkvm):
    def body(mem_ref, g_ref, w_ref, dkv_ref, dw_ref, dg_ref):
        n, _ = _rms(mem_ref[...])
        dkvb = dkv_ref[...].astype(BF16)
        dw_ref[...] = _dot_tn((n * g_ref[...]).astype(BF16), dkvb)
        dg_ref[...] = _colsum(_dot_nt(dkvb, w_ref[...]) * n)

    return _pcall(
        body, name="memkv_bwd", grid=(1,), in_specs=[_full(MEM_LEN, D), _full(1, D), _full(D, D), _full(MEM_LEN, D)],
        out_specs=[_full(D, D), _full(1, D)], out_shape=[_sds((D, D), F32), _sds((1, D), F32)], sem=("arbitrary",),
    )(mem, mnorm, wkv, dkvm)


def _adamw(w, g, m, v, name):
    rows = w.shape[0]
    tr = min(rows, FLAT_TILE)

    def body(w_ref, g_ref, m_ref, v_ref, d_ref, nm_ref, nv_ref):
        gv = g_ref[...]
        nm = ADAM_B1 * m_ref[...] + (1.0 - ADAM_B1) * gv
        nv = ADAM_B2 * v_ref[...] + (1.0 - ADAM_B2) * jnp.square(gv)
        m_hat = nm / (1.0 - ADAM_B1 ** ADAM_STEP)
        v_hat = nv / (1.0 - ADAM_B2 ** ADAM_STEP)
        d_ref[...] = -ADAM_LR * (m_hat / (jnp.sqrt(v_hat) + ADAM_EPS) + ADAM_WD * w_ref[...])
        nm_ref[...] = nm
        nv_ref[...] = nv

    spec = _tok(tr, FLAT_W)
    return _pcall(
        body, name=name, grid=(rows // tr,), in_specs=[spec] * 4, out_specs=[spec] * 3,
        out_shape=[_sds((rows, FLAT_W), F32)] * 3, sem=("arbitrary",),
    )(w, g, m, v)


def _my_place():
    return lax.axis_index("x"), lax.axis_index("y"), lax.axis_index("c")


def _gather_shards(wl):
    R, W = wl.shape

    def body(w_ref, out_ref, send_sems, recv_sems, local_sem):
        x, y, c = _my_place()
        chips = [(1 - x, y), (x, 1 - y), (1 - x, 1 - y)]
        mine = pltpu.make_async_copy(w_ref, out_ref.at[2 * x + y], local_sem)
        mine.start()

        def copy(k, slot, to):
            return pltpu.make_async_remote_copy(src_ref=w_ref, dst_ref=out_ref.at[slot], send_sem=send_sems.at[k],
                                                recv_sem=recv_sems.at[k], device_id=to, device_id_type=MESH_ID)

        sends = [copy(k, 2 * x + y, (px, py, c)) for k, (px, py) in enumerate(chips)]
        for cp in sends:
            cp.start()
        for k, (px, py) in enumerate(chips):
            copy(k, 2 * px + py, (px, py, c)).wait_recv()
        for cp in sends:
            cp.wait_send()
        mine.wait()

    return pl.pallas_call(
        body, name="gather_shards", out_shape=_sds((4, R, W), wl.dtype),
        in_specs=[pl.BlockSpec(memory_space=pl.ANY)], out_specs=pl.BlockSpec(memory_space=pl.ANY),
        scratch_shapes=[pltpu.SemaphoreType.DMA((3,)), pltpu.SemaphoreType.DMA((3,)), pltpu.SemaphoreType.DMA],
    )(wl)


def _scatter_grads(gbig, gsmall):
    _, _, Rh, W = gbig.shape

    def body(gb_ref, gs_ref, rb_ref, rs_ref, send_sems, recv_sems, local_sems):
        x, y, c = _my_place()
        me = 4 * x + 2 * y + c
        lb = pltpu.make_async_copy(gb_ref.at[2 * x + y, c], rb_ref.at[me], local_sems.at[0])
        ls = pltpu.make_async_copy(gs_ref, rs_ref.at[me], local_sems.at[1])
        lb.start()
        ls.start()

        def peer(m):
            return (x ^ ((m >> 2) & 1), y ^ ((m >> 1) & 1), c ^ (m & 1))

        def big(m, src_slot, dst_slot, to):
            return pltpu.make_async_remote_copy(src_ref=gb_ref.at[src_slot[0], src_slot[1]], dst_ref=rb_ref.at[dst_slot],
                                                send_sem=send_sems.at[2 * m], recv_sem=recv_sems.at[2 * m],
                                                device_id=to, device_id_type=MESH_ID)

        def small(m, dst_slot, to):
            return pltpu.make_async_remote_copy(src_ref=gs_ref, dst_ref=rs_ref.at[dst_slot],
                                                send_sem=send_sems.at[2 * m + 1], recv_sem=recv_sems.at[2 * m + 1],
                                                device_id=to, device_id_type=MESH_ID)

        sends = []
        for m in range(1, 8):
            px, py, pc = peer(m)
            sends.append(big(m, (2 * px + py, pc), me, (px, py, pc)))
            sends.append(small(m, me, (px, py, pc)))
        for cp in sends:
            cp.start()
        for m in range(1, 8):
            px, py, pc = peer(m)
            them = 4 * px + 2 * py + pc
            big(m, (2 * x + y, c), them, (px, py, pc)).wait_recv()
            small(m, them, (px, py, pc)).wait_recv()
        for cp in sends:
            cp.wait_send()
        lb.wait()
        ls.wait()

    return pl.pallas_call(
        body, name="scatter_grads", out_shape=[_sds((8, Rh, W), F32), _sds((8, SMALL_ROWS, W), F32)],
        in_specs=[pl.BlockSpec(memory_space=pl.ANY)] * 2, out_specs=[pl.BlockSpec(memory_space=pl.ANY)] * 2,
        scratch_shapes=[pltpu.SemaphoreType.DMA((16,)), pltpu.SemaphoreType.DMA((16,)), pltpu.SemaphoreType.DMA((2,))],
    )(gbig, gsmall)


def _sum_slots(rb, rs):
    _, Rh, W = rb.shape
    tr = FLAT_TILE

    def body(rb_ref, rs_ref, ob_ref, os_ref):
        acc = rb_ref[0]
        for d in range(1, 8):
            acc = acc + rb_ref[d]
        ob_ref[...] = acc

        @pl.when(pl.program_id(0) == 0)
        def _():
            small = rs_ref[0]
            for d in range(1, 8):
                small = small + rs_ref[d]
            os_ref[...] = small

    return _pcall(
        body, name="sum_slots", grid=(Rh // tr,),
        in_specs=[pl.BlockSpec((8, tr, W), lambda i: (0, i, 0)), _full(8, SMALL_ROWS, W)],
        out_specs=[_tok(tr, W), _full(SMALL_ROWS, W)],
        out_shape=[_sds((Rh, W), F32), _sds((SMALL_ROWS, W), F32)], sem=("arbitrary",),
    )(rb, rs)


def _swap_halves(red):
    Rh, W = red.shape

    def body(r_ref, out_ref, send_sem, recv_sem, local_sem):
        x, y, c = _my_place()
        mine = pltpu.make_async_copy(r_ref, out_ref.at[c], local_sem)
        mine.start()
        send = pltpu.make_async_remote_copy(src_ref=r_ref, dst_ref=out_ref.at[c], send_sem=send_sem, recv_sem=recv_sem,
                                            device_id=(x, y, 1 - c), device_id_type=MESH_ID)
        send.start()
        pltpu.make_async_remote_copy(src_ref=r_ref, dst_ref=out_ref.at[1 - c], send_sem=send_sem, recv_sem=recv_sem,
                                     device_id=(x, y, 1 - c), device_id_type=MESH_ID).wait_recv()
        send.wait_send()
        mine.wait()

    return pl.pallas_call(
        body, name="swap_halves", out_shape=_sds((2, Rh, W), F32),
        in_specs=[pl.BlockSpec(memory_space=pl.ANY)], out_specs=pl.BlockSpec(memory_space=pl.ANY),
        scratch_shapes=[pltpu.SemaphoreType.DMA, pltpu.SemaphoreType.DMA, pltpu.SemaphoreType.DMA],
    )(red)


def _pad_heads_cols(w, per, width=LANE):
    k = w.shape[0]
    t = w.reshape(k, -1, per)
    return jnp.pad(t, ((0, 0), (0, 0), (0, width - per))).reshape(k, -1)


def _unpad_heads_cols(w, per, width=LANE):
    k = w.shape[0]
    return w.reshape(k, -1, width)[:, :, :per].reshape(k, -1)


def _swa_place(t):
    z = jnp.zeros_like(t)
    lo = jnp.concatenate([t, z], axis=1)
    hi = jnp.concatenate([z, t], axis=1)
    group = (jnp.arange(NH) // SWA_R).reshape((NH,) + (1,) * (t.ndim - 1))
    full = jnp.where(group == 0, lo, hi)
    return full.reshape((NH * LANE,) + t.shape[2:])


def _swa_unplace(t):
    t = t.reshape((NH, 2, 64) + t.shape[1:])
    return jnp.concatenate([t[:SWA_R, 0], t[SWA_R:, 1]], axis=0)


def _pad_w_in(w):
    z = lambda n: jnp.zeros((D, n), w.dtype)
    cq, ckv, kpe = w[:, 0:256], w[:, 256:384], w[:, 384:416]
    qs, ks, vs, qm, gt = w[:, 416:928], w[:, 928:1056], w[:, 1056:1184], w[:, 1184:1696], w[:, 1696:4768]
    qs_p = _swa_place(qs.T.reshape(NH, 64, D)).T
    return jnp.concatenate([cq, ckv, z(64), kpe, z(32), qs_p, ks, vs, qm, gt], axis=1)


def _unpad_w_in(g):
    qs = _swa_unplace(g[:, C_QS:C_KS].T).reshape(NH * 64, D).T
    return jnp.concatenate([g[:, C_CQ:C_KPE], g[:, C_KPE + 64:C_KPE + 96], qs, g[:, C_KS:C_END]], axis=1)


def _pad_w_ukv(w):
    t = w.reshape(128, NH, 2, 64)
    pad = lambda a: jnp.pad(a, ((0, 0), (0, 0), (0, 64))).reshape(128, NH * LANE)
    return jnp.concatenate([pad(t[:, :, 0]), pad(t[:, :, 1])], axis=1)


def _unpad_w_ukv(g):
    k = g[:, :NH * LANE].reshape(128, NH, LANE)[:, :, :64]
    v = g[:, NH * LANE:].reshape(128, NH, LANE)[:, :, :64]
    return jnp.stack([k, v], axis=2).reshape(128, NH * LANE)


def _pad_w_o_mla(w):
    return jnp.pad(w.reshape(NH, 64, D), ((0, 0), (0, 64), (0, 0))).reshape(NH * LANE, D)


def _unpad_w_o_mla(g):
    return g.reshape(NH, LANE, D)[:, :64].reshape(NH * 64, D)


def _rope_tables(S):
    pos = jnp.arange(S, dtype=F32)
    inv = 1.0 / (ROPE_THETA ** (jnp.arange(0, 32, 2, dtype=F32) / 32))
    ang = pos[:, None] * inv[None, :]
    cos, sin = jnp.cos(ang), jnp.sin(ang)
    one, zero = jnp.ones((S, 64), F32), jnp.zeros((S, 16), F32)
    rc = jnp.concatenate([one, cos, cos, jnp.ones((S, 32), F32)], axis=1)
    rs1 = jnp.concatenate([jnp.zeros((S, 64), F32), zero, sin, jnp.zeros((S, 32), F32)], axis=1)
    rs2 = jnp.concatenate([jnp.zeros((S, 64), F32), -sin, zero, jnp.zeros((S, 32), F32)], axis=1)
    return rc, rs1, rs2


def _bucket_map():
    qi = jnp.arange(WIN)[:, None]
    kj = jnp.arange(2 * WIN)[None, :]
    dist = qi + WIN - kj
    n = jnp.maximum(dist, 0)
    max_exact = REL_BUCKETS // 2
    nf = jnp.maximum(n, 1).astype(F32)
    large = max_exact + (jnp.log(nf / max_exact) / math.log(128 / max_exact)
                         * (REL_BUCKETS - max_exact)).astype(jnp.int32)
    large = jnp.minimum(large, REL_BUCKETS - 1)
    bucket = jnp.where(n < max_exact, n, large)
    return jnp.where((dist >= 0) & (dist < WIN), bucket, -1).astype(jnp.int32)


TS = 256
TQ = 512


def _layer_weights(wf, l):
    return dict(
        win=_pad_w_in(wf["w_in"][l]),
        wuq=_pad_heads_cols(wf["w_uq"][l], 96),
        wukv=_pad_w_ukv(wf["w_ukv"][l]),
        wmem=wf["w_mem_kv"][l],
        wa=_pad_w_o_mla(wf["w_o_mla"][l]),
        wb=_swa_place(wf["w_o_swa"][l].reshape(NH, 64, D)),
        wc=wf["w_o_mem"][l],
        wout=wf["w_out"][l],
        wup=wf["w_up"][l],
        wdown=wf["w_down"][l],
    )


def _local_step(x, mem, tgt, wf, sp):
    S = x.shape[0]
    ts = min(TS, S)
    tq = min(TQ, S)
    rc, rs1, rs2 = _rope_tables(S)
    bmap = _bucket_map()
    bias = _bias_build(sp["rel_bias"], bmap)
    row = lambda v: v.reshape(1, -1)

    saved = []
    for l in range(DEPTH):
        w = _layer_weights(wf, l)
        an, qn, kvn = row(sp["attn_norm"][l]), row(sp["mla_q_norm"][l]), row(sp["mla_kv_norm"][l])
        bg, mnorm, mlpn = row(sp["b_gate"][l]), row(sp["mem_norm"][l]), row(sp["mlp_norm"][l])
        sinks = jnp.broadcast_to(sp["attn_sinks"][l][:, None], (NH, LANE))
        q, k, v, qs, ks, vs, qm, g = _pre_fwd(x, an, w["win"], bg, qn, kvn, w["wuq"], w["wukv"], rc, rs1, rs2, ts)
        oa, lse = _mla_fwd(q, k, v, tq)
        ob = _swa_fwd(qs, ks, vs, bias, sinks)
        kvm = _memkv_fwd(mem, mnorm, w["wmem"])
        oc = _mem_fwd(qm, kvm, ts)
        x1, yb = _merge_fwd(x, g, oa, ob, oc, w["wa"], w["wb"], w["wc"], w["wout"], ts)
        x2 = _mlp_fwd(x1, mlpn, w["wup"], w["wdown"], ts)
        saved.append(dict(w=w, x=x, x1=x1, q=q, k=k, v=v, qs=qs, ks=ks, vs=vs, qm=qm, g=g, oa=oa, lse=lse, ob=ob,
                          oc=oc, kvm=kvm, yb=yb, an=an, qn=qn, kvn=kvn, mnorm=mnorm, mlpn=mlpn, sinks=sinks))
        x = x2

    sq, dx, dfn = _loss_kernel(x, row(sp["final_norm"]), tgt, ts)

    grads = {n: [None] * DEPTH for n in WEIGHTS if n not in ("rel_bias", "final_norm")}
    dbias_total = None
    for l in reversed(range(DEPTH)):
        sv = saved[l]
        w = sv["w"]
        dx1, hb2, dub, ab, dxb, dmlpn = _mlp_bwd(dx, sv["x1"], sv["mlpn"], w["wup"], w["wdown"], ts)
        grads["w_up"][l] = _matmul_tn(hb2, dub, "dw_up")
        grads["w_down"][l] = _matmul_tn(ab, dxb, "dw_down")
        grads["mlp_norm"][l] = dmlpn[0]

        dgp, dyo, doa, dob, doc, dla, dx1b, dbg = _merge_bwd(dx1, sv["g"], sv["oa"], sv["ob"], sv["oc"], w["wa"],
                                                             w["wb"], w["wc"], w["wout"], ts)
        grads["w_out"][l] = _matmul_tn(sv["yb"], dx1b, "dw_out")
        grads["w_o_mla"][l] = _unpad_w_o_mla(_matmul_tn(sv["oa"], dyo[:, 0:D], "dw_o_mla"))
        grads["w_o_swa"][l] = _swa_unplace(_matmul_tn(sv["ob"], dyo[:, D:2 * D], "dw_o_swa")).reshape(NH * 64, D)
        grads["w_o_mem"][l] = _matmul_tn(sv["oc"], dyo[:, 2 * D:3 * D], "dw_o_mem")
        grads["b_gate"][l] = dbg[0]

        dqm, dkvm = _mem_bwd(sv["qm"], sv["kvm"], doc, ts)
        dwmem, dmnorm = _memkv_bwd(mem, sv["mnorm"], w["wmem"], dkvm)
        grads["w_mem_kv"][l] = dwmem
        grads["mem_norm"][l] = dmnorm[0]

        dqs, dks, dvs, dbias, dsink = _swa_bwd(sv["qs"], sv["ks"], sv["vs"], dob, bias, sv["sinks"])
        dbias_total = dbias if dbias_total is None else dbias_total + dbias
        grads["attn_sinks"][l] = dsink[:, 0]

        dq, dk, dv = _mla_bwd(sv["q"], sv["k"], sv["v"], doa, sv["lse"], dla, tq)

        dx, dproj, hb, cqn, ckvn, dqpre, dkv, dan, dqn, dkvn = _pre_bwd(
            sv["x"], dx1, dq, dk, dv, dqs, dks, dvs, dqm, dgp, sv["an"], sv["qn"], sv["kvn"], w["win"], w["wuq"],
            w["wukv"], rc, rs1, rs2, ts)
        grads["w_in"][l] = _unpad_w_in(_matmul_tn(hb, dproj, "dw_in"))
        grads["w_uq"][l] = _unpad_heads_cols(_matmul_tn(cqn, dqpre, "dw_uq"), 96)
        grads["w_ukv"][l] = _unpad_w_ukv(_matmul_tn(ckvn, dkv, "dw_ukv"))
        grads["attn_norm"][l] = dan[0]
        grads["mla_q_norm"][l] = dqn[0]
        grads["mla_kv_norm"][l] = dkvn[0]

    out = {n: jnp.stack(v) for n, v in grads.items()}
    out["rel_bias"] = _bias_reduce(dbias_total, bmap)
    out["final_norm"] = dfn[0]
    return sq, dx, out


def _flat_rows(n_elems, multiple):
    rows = -(-n_elems // FLAT_W)
    return -(-rows // multiple) * multiple


def _flatten(parts, rows):
    flat = jnp.concatenate([p.reshape(-1) for p in parts])
    return jnp.pad(flat, (0, rows * FLAT_W - flat.shape[0])).reshape(rows, FLAT_W)


def _unflatten(buf, shapes):
    flat = buf.reshape(-1)
    out, at = [], 0
    for s in shapes:
        n = int(np.prod(s))
        out.append(flat[at:at + n].reshape(s))
        at += n
    return out


def _shard_of(g, axis, s):
    n = g.shape[axis] // 4
    return lax.slice_in_dim(g, s * n, (s + 1) * n, axis=axis)


def kernel(x, mem, rel_bias, attn_norm, mem_norm, w_in, b_gate, mla_q_norm, w_uq, mla_kv_norm, w_ukv, attn_sinks, w_mem_kv, w_o_mla, w_o_swa, w_o_mem, w_out, mlp_norm, w_up, w_down, final_norm, loss_target, m_rel_bias, m_attn_norm, m_mem_norm, m_w_in, m_b_gate, m_mla_q_norm, m_w_uq, m_mla_kv_norm, m_w_ukv, m_attn_sinks, m_w_mem_kv, m_w_o_mla, m_w_o_swa, m_w_o_mem, m_w_out, m_mlp_norm, m_w_up, m_w_down, m_final_norm, v_rel_bias, v_attn_norm, v_mem_norm, v_w_in, v_b_gate, v_mla_q_norm, v_w_uq, v_mla_kv_norm, v_w_ukv, v_attn_sinks, v_w_mem_kv, v_w_o_mla, v_w_o_swa, v_w_o_mem, v_w_out, v_mlp_norm, v_w_up, v_w_down, v_final_norm):
    args = dict(locals())
    W = {n: args[n] for n in WEIGHTS}
    M = {n: args["m_" + n] for n in WEIGHTS}
    V = {n: args["v_" + n] for n in WEIGHTS}
    shard_shapes = [W[n].shape for n in SHARDED]
    small_shapes = [W[n].shape for n in SMALL]
    R = _flat_rows(sum(int(np.prod(s)) for s in shard_shapes), 2 * FLAT_TILE)

    wl = _flatten([W[n].astype(BF16) for n in SHARDED], R)
    gathered = _gather_shards(wl)
    per_chip = [_unflatten(gathered[s], shard_shapes) for s in range(4)]
    wf = {n: jnp.concatenate([per_chip[s][i] for s in range(4)], axis=SHARD_AXIS[n]) for i, n in enumerate(SHARDED)}
    sp = {n: W[n] for n in SMALL}

    sq, dx, grads = _local_step(x[0], mem[0], loss_target[0], wf, sp)

    gbig = jnp.stack([_flatten([_shard_of(grads[n], SHARD_AXIS[n], s) for n in SHARDED], R) for s in range(4)])
    gsmall = _flatten([grads[n] for n in SMALL], SMALL_ROWS)
    rb, rs = _scatter_grads(gbig.reshape(4, 2, R // 2, FLAT_W), gsmall)
    red_half, red_small = _sum_slots(rb, rs)
    g_shard = _swap_halves(red_half).reshape(R, FLAT_W)

    d_b, m_b, v_b = _adamw(_flatten([W[n] for n in SHARDED], R), g_shard, _flatten([M[n] for n in SHARDED], R),
                           _flatten([V[n] for n in SHARDED], R), "adamw_shard")
    d_s, m_s, v_s = _adamw(_flatten([W[n] for n in SMALL], SMALL_ROWS), red_small,
                           _flatten([M[n] for n in SMALL], SMALL_ROWS), _flatten([V[n] for n in SMALL], SMALL_ROWS),
                           "adamw_small")

    def named(big, small):
        out = dict(zip(SHARDED, _unflatten(big, shard_shapes)))
        out.update(zip(SMALL, _unflatten(small, small_shapes)))
        return out

    G, DW, NM, NV = named(g_shard, red_small), named(d_b, d_s), named(m_b, m_s), named(v_b, v_s)
    loss = lax.psum(0.5 * sq[0, 0] / D, ("x", "y", "c"))
    return (loss, dx[None], *[G[n] for n in WEIGHTS], *[DW[n] for n in WEIGHTS], *[NM[n] for n in WEIGHTS],
            *[NV[n] for n in WEIGHTS])
```

```python
import functools
import math

import numpy as np
import jax
import jax.numpy as jnp
from jax import lax
from jax.experimental import pallas as pl
from jax.experimental.pallas import tpu as pltpu

F32 = jnp.float32
BF16 = jnp.bfloat16

D = 1024
DFF = 4096
DEPTH = 2
EPS = 1e-6
LANE = 128
NH = 8
SWA_R = 4
MEM_H = 4
MEM_LEN = 256
WIN = 128
NEG = -1e30
MLA_SCALE = 96 ** -0.5
SWA_SCALE = 64 ** -0.5
MEM_SCALE = 128 ** -0.5
REL_BUCKETS = 32
ROPE_THETA = 10000.0

C_CQ, C_CKV, C_KPE, C_QS, C_KS, C_VS, C_QM, C_G, C_END = 0, 256, 384, 512, 1536, 1664, 1792, 2304, 5376
IN_COLS = 4768

ADAM_LR = 0.001
ADAM_B1 = 0.9
ADAM_B2 = 0.999
ADAM_EPS = 1e-08
ADAM_WD = 0.01
ADAM_STEP = 10

VMEM_LIMIT = 56 * 1024 * 1024

SHARDED = ("w_in", "w_uq", "w_ukv", "w_mem_kv", "w_o_mla", "w_o_swa", "w_o_mem", "w_out", "w_up", "w_down")
SHARD_AXIS = {"w_in": 2, "w_uq": 2, "w_ukv": 2, "w_mem_kv": 1, "w_o_mla": 2, "w_o_swa": 2, "w_o_mem": 2,
              "w_out": 1, "w_up": 2, "w_down": 1}
SMALL = ("rel_bias", "attn_norm", "mem_norm", "b_gate", "mla_q_norm", "mla_kv_norm", "attn_sinks", "mlp_norm",
         "final_norm")
WEIGHTS = ("rel_bias", "attn_norm", "mem_norm", "w_in", "b_gate", "mla_q_norm", "w_uq", "mla_kv_norm", "w_ukv",
           "attn_sinks", "w_mem_kv", "w_o_mla", "w_o_swa", "w_o_mem", "w_out", "mlp_norm", "w_up", "w_down",
           "final_norm")
FLAT_W = 1024
FLAT_TILE = 256
SMALL_ROWS = 16
MESH_ID = pl.DeviceIdType.MESH


def _pcall(body, *, name, grid, in_specs, out_specs, out_shape, scratch=(), prefetch=0, sem=None):
    params = pltpu.CompilerParams(dimension_semantics=sem, vmem_limit_bytes=VMEM_LIMIT)
    if prefetch:
        spec = pltpu.PrefetchScalarGridSpec(num_scalar_prefetch=prefetch, grid=grid, in_specs=in_specs,
                                            out_specs=out_specs, scratch_shapes=scratch)
        return pl.pallas_call(body, name=name, grid_spec=spec, out_shape=out_shape, compiler_params=params)
    return pl.pallas_call(body, name=name, grid=grid, in_specs=in_specs, out_specs=out_specs, out_shape=out_shape,
                          scratch_shapes=scratch, compiler_params=params)


def _tok(ts, w):
    return pl.BlockSpec((ts, w), lambda i: (i, 0))


def _full(*shape):
    return pl.BlockSpec(shape, lambda *_: (0,) * len(shape))


def _sds(shape, dtype):
    return jax.ShapeDtypeStruct(shape, dtype)


def _dot(a, b):
    return jnp.dot(a, b, preferred_element_type=F32)


def _dot_nt(a, b):
    return lax.dot_general(a, b, (((1,), (1,)), ((), ())), preferred_element_type=F32)


def _dot_tn(a, b):
    return lax.dot_general(a, b, (((0,), (0,)), ((), ())), preferred_element_type=F32)


def _rms(x):
    r = lax.rsqrt(jnp.mean(x * x, axis=-1, keepdims=True) + EPS)
    return x * r, r


def _rms_bwd(dyg, n, r):
    return r * (dyg - n * jnp.mean(n * dyg, axis=-1, keepdims=True))


def _rope(t, c, s1, s2):
    return t * c + pltpu.roll(t, 16, 1) * s1 + pltpu.roll(t, LANE - 16, 1) * s2


def _rope_bwd(dy, c, s1, s2):
    return dy * c + pltpu.roll(dy * s1, LANE - 16, 1) + pltpu.roll(dy * s2, 16, 1)


def _hs(h):
    return slice(h * LANE, (h + 1) * LANE)


def _colsum(t):
    return jnp.sum(t, axis=0, keepdims=True)


def _pre_fwd(x, an, win, bg, qn, kvn, wuq, wukv, rc, rs1, rs2, ts):
    S = x.shape[0]

    def body(x_ref, an_ref, win_ref, bg_ref, qn_ref, kvn_ref, wuq_ref, wukv_ref, rc_ref, rs1_ref, rs2_ref,
             q_ref, k_ref, v_ref, qs_ref, ks_ref, vs_ref, qm_ref, g_ref):
        n, _ = _rms(x_ref[...])
        hb = (n * an_ref[...]).astype(BF16)
        pa = _dot(hb, win_ref[:, C_CQ:C_QS])
        ncq, _ = _rms(pa[:, 0:256])
        cqn = (ncq * qn_ref[...]).astype(BF16)
        nkv, _ = _rms(pa[:, 256:384])
        ckvn = (nkv * kvn_ref[...]).astype(BF16)
        c, s1, s2 = rc_ref[...], rs1_ref[...], rs2_ref[...]
        kper = _rope(pa[:, 384:512], c, s1, s2)
        qp = _dot(cqn, wuq_ref[...])
        kv = _dot(ckvn, wukv_ref[...])
        for h in range(NH):
            q_ref[:, _hs(h)] = (_rope(qp[:, _hs(h)], c, s1, s2) * MLA_QSCALE).astype(BF16)
            k_ref[:, _hs(h)] = (kv[:, _hs(h)] + kper).astype(BF16)
        v_ref[...] = kv[:, NH * LANE:].astype(BF16)
        pb = _dot(hb, win_ref[:, C_QS:C_G])
        qs_ref[...] = pb[:, 0:1024].astype(BF16)
        ks_ref[...] = pb[:, 1024:1152].astype(BF16)
        vs_ref[...] = pb[:, 1152:1280].astype(BF16)
        qm_ref[...] = pb[:, 1280:1792].astype(BF16)
        g_ref[...] = jax.nn.sigmoid(_dot(hb, win_ref[:, C_G:C_END]) + bg_ref[...])

    return _pcall(
        body, name="pre_fwd", grid=(S // ts,),
        in_specs=[_tok(ts, D), _full(1, D), _full(D, C_END), _full(1, 3 * D), _full(1, 256), _full(1, 128),
                  _full(256, NH * LANE), _full(128, 2 * NH * LANE), _tok(ts, LANE), _tok(ts, LANE), _tok(ts, LANE)],
        out_specs=[_tok(ts, 1024), _tok(ts, 1024), _tok(ts, 1024), _tok(ts, 1024), _tok(ts, 128), _tok(ts, 128),
                   _tok(ts, 512), _tok(ts, 3 * D)],
        out_shape=[_sds((S, 1024), BF16), _sds((S, 1024), BF16), _sds((S, 1024), BF16), _sds((S, 1024), BF16),
                   _sds((S, 128), BF16), _sds((S, 128), BF16), _sds((S, 512), BF16), _sds((S, 3 * D), F32)],
        sem=("arbitrary",),
    )(x, an, win, bg, qn, kvn, wuq, wukv, rc, rs1, rs2)


def _merge_fwd(x, g, oa, ob, oc, wa, wb, wc, wout, ts):
    S = x.shape[0]

    def body(x_ref, g_ref, oa_ref, ob_ref, oc_ref, wa_ref, wb_ref, wc_ref, wout_ref, x1_ref, yb_ref):
        y = g_ref[:, 0:D] * _dot(oa_ref[...], wa_ref[...])
        y = y + g_ref[:, D:2 * D] * _dot(ob_ref[...], wb_ref[...])
        y = y + g_ref[:, 2 * D:3 * D] * _dot(oc_ref[...], wc_ref[...])
        yb = y.astype(BF16)
        yb_ref[...] = yb
        x1_ref[...] = x_ref[...] + _dot(yb, wout_ref[...])

    return _pcall(
        body, name="merge_fwd", grid=(S // ts,),
        in_specs=[_tok(ts, D), _tok(ts, 3 * D), _tok(ts, 1024), _tok(ts, 1024), _tok(ts, 512),
                  _full(1024, D), _full(1024, D), _full(512, D), _full(D, D)],
        out_specs=[_tok(ts, D), _tok(ts, D)],
        out_shape=[_sds((S, D), F32), _sds((S, D), BF16)],
        sem=("arbitrary",),
    )(x, g, oa, ob, oc, wa, wb, wc, wout)


def _mlp_fwd(x1, mn, wup, wdown, ts):
    S = x1.shape[0]

    def body(x_ref, mn_ref, wup_ref, wdown_ref, x2_ref):
        xv = x_ref[...]
        n, _ = _rms(xv)
        u = _dot((n * mn_ref[...]).astype(BF16), wup_ref[...])
        a = jnp.square(jnp.maximum(u, 0.0))
        x2_ref[...] = xv + _dot(a.astype(BF16), wdown_ref[...])

    return _pcall(
        body, name="mlp_fwd", grid=(S // ts,),
        in_specs=[_tok(ts, D), _full(1, D), _full(D, DFF), _full(DFF, D)],
        out_specs=_tok(ts, D), out_shape=_sds((S, D), F32), sem=("arbitrary",),
    )(x1, mn, wup, wdown)


def _loss_kernel(x, fn, tgt, ts):
    S = x.shape[0]

    def body(x_ref, fn_ref, t_ref, loss_ref, dx_ref, dfn_ref):
        @pl.when(pl.program_id(0) == 0)
        def _():
            loss_ref[...] = jnp.zeros_like(loss_ref)
            dfn_ref[...] = jnp.zeros_like(dfn_ref)

        n, r = _rms(x_ref[...])
        err = n * fn_ref[...] - t_ref[...]
        loss_ref[...] += jnp.sum(err * err)
        dy = err * (1.0 / D)
        dfn_ref[...] += _colsum(dy * n)
        dx_ref[...] = _rms_bwd(dy * fn_ref[...], n, r)

    return _pcall(
        body, name="loss_head", grid=(S // ts,),
        in_specs=[_tok(ts, D), _full(1, D), _tok(ts, D)],
        out_specs=[_full(1, LANE), _tok(ts, D), _full(1, D)],
        out_shape=[_sds((1, LANE), F32), _sds((S, D), F32), _sds((1, D), F32)],
        sem=("arbitrary",),
    )(x, fn, tgt)


def _mlp_bwd(dx2, x1, mn, wup, wdown, ts):
    S = x1.shape[0]

    def body(dx_ref, x_ref, mn_ref, wup_ref, wdown_ref, dx1_ref, hb_ref, dub_ref, ab_ref, dxb_ref, dmn_ref):
        @pl.when(pl.program_id(0) == 0)
        def _():
            dmn_ref[...] = jnp.zeros_like(dmn_ref)

        dx = dx_ref[...]
        n, r = _rms(x_ref[...])
        g = mn_ref[...]
        hb = (n * g).astype(BF16)
        hb_ref[...] = hb
        rl = jnp.maximum(_dot(hb, wup_ref[...]), 0.0)
        ab_ref[...] = jnp.square(rl).astype(BF16)
        dxb = dx.astype(BF16)
        dxb_ref[...] = dxb
        dub = (_dot_nt(dxb, wdown_ref[...]) * (2.0 * rl)).astype(BF16)
        dub_ref[...] = dub
        dh = _dot_nt(dub, wup_ref[...])
        dmn_ref[...] += _colsum(dh * n)
        dx1_ref[...] = dx + _rms_bwd(dh * g, n, r)

    return _pcall(
        body, name="mlp_bwd", grid=(S // ts,),
        in_specs=[_tok(ts, D), _tok(ts, D), _full(1, D), _full(D, DFF), _full(DFF, D)],
        out_specs=[_tok(ts, D), _tok(ts, D), _tok(ts, DFF), _tok(ts, DFF), _tok(ts, D), _full(1, D)],
        out_shape=[_sds((S, D), F32), _sds((S, D), BF16), _sds((S, DFF), BF16), _sds((S, DFF), BF16),
                   _sds((S, D), BF16), _sds((1, D), F32)],
        sem=("arbitrary",),
    )(dx2, x1, mn, wup, wdown)


def _merge_bwd(dx1, g, oa, ob, oc, wa, wb, wc, wout, ts):
    S = dx1.shape[0]

    def body(dx_ref, g_ref, oa_ref, ob_ref, oc_ref, wa_ref, wb_ref, wc_ref, wout_ref,
             dgp_ref, dyo_ref, doa_ref, dob_ref, doc_ref, dla_ref, dxb_ref, dbg_ref):
        @pl.when(pl.program_id(0) == 0)
        def _():
            dbg_ref[...] = jnp.zeros_like(dbg_ref)

        dxb = dx_ref[...].astype(BF16)
        dxb_ref[...] = dxb
        dy = _dot_nt(dxb, wout_ref[...])
        branches = ((oa_ref, wa_ref, doa_ref), (ob_ref, wb_ref, dob_ref), (oc_ref, wc_ref, doc_ref))
        for b, (o_ref, w_ref, do_ref) in enumerate(branches):
            cols = slice(b * D, (b + 1) * D)
            gb = g_ref[:, cols]
            o = o_ref[...]
            dgpre = dy * _dot(o, w_ref[...]) * gb * (1.0 - gb)
            dgp_ref[:, cols] = dgpre.astype(BF16)
            dbg_ref[:, cols] += _colsum(dgpre)
            dyo = (dy * gb).astype(BF16)
            dyo_ref[:, cols] = dyo
            do = _dot_nt(dyo, w_ref[...])
            do_ref[...] = do.astype(BF16)
            if b == 0:
                for h in range(NH):
                    dl = jnp.sum(do[:, _hs(h)] * o[:, _hs(h)].astype(F32), axis=1, keepdims=True)
                    dla_ref[:, _hs(h)] = jnp.broadcast_to(dl, (ts, LANE))

    return _pcall(
        body, name="merge_bwd", grid=(S // ts,),
        in_specs=[_tok(ts, D), _tok(ts, 3 * D), _tok(ts, 1024), _tok(ts, 1024), _tok(ts, 512),
                  _full(1024, D), _full(1024, D), _full(512, D), _full(D, D)],
        out_specs=[_tok(ts, 3 * D), _tok(ts, 3 * D), _tok(ts, 1024), _tok(ts, 1024), _tok(ts, 512), _tok(ts, 1024),
                   _tok(ts, D), _full(1, 3 * D)],
        out_shape=[_sds((S, 3 * D), BF16), _sds((S, 3 * D), BF16), _sds((S, 1024), BF16), _sds((S, 1024), BF16),
                   _sds((S, 512), BF16), _sds((S, 1024), F32), _sds((S, D), BF16), _sds((1, 3 * D), F32)],
        sem=("arbitrary",),
    )(dx1, g, oa, ob, oc, wa, wb, wc, wout)


def _pre_bwd(x, dx1, dq, dk, dv, dqs, dks, dvs, dqm, dgp, an, qn, kvn, win, wuq, wukv, rc, rs1, rs2, ts):
    S = x.shape[0]

    def body(x_ref, dx1_ref, dq_ref, dk_ref, dv_ref, dqs_ref, dks_ref, dvs_ref, dqm_ref, dgp_ref,
             an_ref, qn_ref, kvn_ref, win_ref, wuq_ref, wukv_ref, rc_ref, rs1_ref, rs2_ref,
             dx_ref, dproj_ref, hb_ref, cqn_ref, ckvn_ref, dqpre_ref, dkv_ref, dan_ref, dqn_ref, dkvn_ref):
        @pl.when(pl.program_id(0) == 0)
        def _():
            dan_ref[...] = jnp.zeros_like(dan_ref)
            dqn_ref[...] = jnp.zeros_like(dqn_ref)
            dkvn_ref[...] = jnp.zeros_like(dkvn_ref)

        n, r = _rms(x_ref[...])
        hb = (n * an_ref[...]).astype(BF16)
        hb_ref[...] = hb
        pa = _dot(hb, win_ref[:, C_CQ:C_KPE])
        ncq, rq = _rms(pa[:, 0:256])
        cqn_ref[...] = (ncq * qn_ref[...]).astype(BF16)
        nkv, rkv = _rms(pa[:, 256:384])
        ckvn_ref[...] = (nkv * kvn_ref[...]).astype(BF16)
        c, s1, s2 = rc_ref[...], rs1_ref[...], rs2_ref[...]

        dkper = jnp.zeros((ts, LANE), F32)
        for h in range(NH):
            dqpre_ref[:, _hs(h)] = _rope_bwd(dq_ref[:, _hs(h)], c, s1, s2).astype(BF16)
            dkh = dk_ref[:, _hs(h)]
            dkper = dkper + dkh
            dkv_ref[:, _hs(h)] = dkh.astype(BF16)
        dkv_ref[:, NH * LANE:] = dv_ref[...].astype(BF16)

        dcqn = _dot_nt(dqpre_ref[...], wuq_ref[...])
        dqn_ref[...] += _colsum(dcqn * ncq)
        dproj_ref[:, C_CQ:C_CKV] = _rms_bwd(dcqn * qn_ref[...], ncq, rq).astype(BF16)
        dckvn = _dot_nt(dkv_ref[...], wukv_ref[...])
        dkvn_ref[...] += _colsum(dckvn * nkv)
        dproj_ref[:, C_CKV:C_KPE] = _rms_bwd(dckvn * kvn_ref[...], nkv, rkv).astype(BF16)
        lane = lax.broadcasted_iota(jnp.int32, (ts, LANE), 1)
        dkpe = jnp.where((lane >= 64) & (lane < 96), _rope_bwd(dkper, c, s1, s2), 0.0)
        dproj_ref[:, C_KPE:C_QS] = dkpe.astype(BF16)
        dproj_ref[:, C_QS:C_KS] = dqs_ref[...]
        dproj_ref[:, C_KS:C_VS] = dks_ref[...].astype(BF16)
        dproj_ref[:, C_VS:C_QM] = dvs_ref[...].astype(BF16)
        dproj_ref[:, C_QM:C_G] = dqm_ref[...]
        dproj_ref[:, C_G:C_END] = dgp_ref[...]

        dh = _dot_nt(dproj_ref[...], win_ref[...])
        dan_ref[...] += _colsum(dh * n)
        dx_ref[...] = dx1_ref[...] + _rms_bwd(dh * an_ref[...], n, r)

    return _pcall(
        body, name="pre_bwd", grid=(S // ts,),
        in_specs=[_tok(ts, D), _tok(ts, D), _tok(ts, 1024), _tok(ts, 1024), _tok(ts, 1024), _tok(ts, 1024),
                  _tok(ts, 128), _tok(ts, 128), _tok(ts, 512), _tok(ts, 3 * D),
                  _full(1, D), _full(1, 256), _full(1, 128), _full(D, C_END), _full(256, NH * LANE),
                  _full(128, 2 * NH * LANE), _tok(ts, LANE), _tok(ts, LANE), _tok(ts, LANE)],
        out_specs=[_tok(ts, D), _tok(ts, C_END), _tok(ts, D), _tok(ts, 256), _tok(ts, 128), _tok(ts, 1024),
                   _tok(ts, 2048), _full(1, D), _full(1, 256), _full(1, 128)],
        out_shape=[_sds((S, D), F32), _sds((S, C_END), BF16), _sds((S, D), BF16), _sds((S, 256), BF16),
                   _sds((S, 128), BF16), _sds((S, 1024), BF16), _sds((S, 2048), BF16), _sds((1, D), F32),
                   _sds((1, 256), F32), _sds((1, 128), F32)],
        sem=("arbitrary",),
    )(x, dx1, dq, dk, dv, dqs, dks, dvs, dqm, dgp, an, qn, kvn, win, wuq, wukv, rc, rs1, rs2)


def _pick_tile(n, cap):
    best = LANE
    for t in range(LANE, min(n, cap) + 1, LANE):
        if n % t == 0:
            best = t
    return best


def _matmul_tn(a, b, name):
    S, M = a.shape
    N = b.shape[1]
    tm = _pick_tile(M, 1024)
    tn = _pick_tile(N, 2048)
    ts = min(S, 1024)

    def body(a_ref, b_ref, o_ref):
        @pl.when(pl.program_id(2) == 0)
        def _():
            o_ref[...] = jnp.zeros_like(o_ref)

        o_ref[...] += _dot_tn(a_ref[...], b_ref[...])

    return _pcall(
        body, name=name, grid=(M // tm, N // tn, S // ts),
        in_specs=[pl.BlockSpec((ts, tm), lambda i, j, k: (k, i)), pl.BlockSpec((ts, tn), lambda i, j, k: (k, j))],
        out_specs=pl.BlockSpec((tm, tn), lambda i, j, k: (i, j)),
        out_shape=_sds((M, N), F32), sem=("parallel", "parallel", "arbitrary"),
    )(a, b)


MLA_RC = 128
LOG2E = math.log2(math.e)
MLA_QSCALE = MLA_SCALE * LOG2E


def _mla_fwd(q, k, v, tq):
    S = q.shape[0]
    nq = S // tq
    nt = tq // LANE
    pairs = [(i, j) for i in range(nq) for j in range(i + 1)]
    qi = jnp.asarray(np.array([p[0] for p in pairs], np.int32))
    kj = jnp.asarray(np.array([p[1] for p in pairs], np.int32))

    def body(qi_ref, kj_ref, q_ref, k_ref, v_ref, o_ref, lse_ref, m_s, l_s, acc_s):
        t = pl.program_id(1)
        i, j = qi_ref[t], kj_ref[t]

        @pl.when(j == 0)
        def _():
            m_s[...] = jnp.full_like(m_s, NEG)
            l_s[...] = jnp.zeros_like(l_s)
            acc_s[...] = jnp.zeros_like(acc_s)

        def step(masked):
            kb, vb = k_ref[...], v_ref[...]
            nc = tq // MLA_RC
            scores = [_dot_nt(q_ref[c * MLA_RC:(c + 1) * MLA_RC, :], kb) for c in range(nc)]
            for c in range(nc):
                rows = slice(c * MLA_RC, (c + 1) * MLA_RC)
                s = scores[c]
                if masked:
                    row = lax.broadcasted_iota(jnp.int32, (MLA_RC, tq), 0) + c * MLA_RC
                    col = lax.broadcasted_iota(jnp.int32, (MLA_RC, tq), 1)
                    s = jnp.where(col <= row, s, NEG)
                tiles = [s[:, _hs(u)] for u in range(nt)]
                mx = functools.reduce(jnp.maximum, tiles)
                m_old = m_s[rows, :]
                m_new = jnp.maximum(m_old, jnp.max(mx, axis=1, keepdims=True))
                alpha = jnp.exp2(m_old - m_new)
                ps = [jnp.exp2(u - m_new) for u in tiles]
                l_s[rows, :] = alpha * l_s[rows, :] + functools.reduce(jnp.add, ps)
                p = jnp.concatenate([u.astype(BF16) for u in ps], axis=1)
                acc_s[rows, :] = alpha * acc_s[rows, :] + _dot(p, vb)
                m_s[rows, :] = m_new

        @pl.when(j < i)
        def _():
            step(False)

        @pl.when(j == i)
        def _():
            step(True)
            l = jnp.sum(l_s[...], axis=1, keepdims=True)
            o_ref[...] = (acc_s[...] / l).astype(BF16)
            lse_ref[...] = m_s[...] + jnp.log2(l)

    qmap = lambda h, t, qi_r, kj_r: (qi_r[t], h)
    kmap = lambda h, t, qi_r, kj_r: (kj_r[t], h)
    return _pcall(
        body, name="mla_fwd", grid=(NH, len(pairs)), prefetch=2,
        in_specs=[pl.BlockSpec((tq, LANE), qmap), pl.BlockSpec((tq, LANE), kmap), pl.BlockSpec((tq, LANE), kmap)],
        out_specs=[pl.BlockSpec((tq, LANE), qmap), pl.BlockSpec((tq, LANE), qmap)],
        out_shape=[_sds((S, NH * LANE), BF16), _sds((S, NH * LANE), F32)],
        scratch=[pltpu.VMEM((tq, LANE), F32), pltpu.VMEM((tq, LANE), F32), pltpu.VMEM((tq, LANE), F32)],
        sem=("arbitrary", "arbitrary"),
    )(qi, kj, q, k, v)


def _mla_bwd(q, k, v, do, lse, delta, tq):
    S = q.shape[0]
    nq = S // tq
    pairs = [(i, j) for j in range(nq) for i in range(j, nq)]
    qi = jnp.asarray(np.array([p[0] for p in pairs], np.int32))
    kj = jnp.asarray(np.array([p[1] for p in pairs], np.int32))

    def body(qi_ref, kj_ref, q_ref, k_ref, v_ref, do_ref, lse_ref, dl_ref, dq_ref, dk_ref, dv_ref,
             dq_s, dk_s, dv_s):
        t = pl.program_id(1)
        i, j = qi_ref[t], kj_ref[t]

        @pl.when(t == 0)
        def _():
            dq_s[...] = jnp.zeros_like(dq_s)

        @pl.when(i == j)
        def _():
            dk_s[...] = jnp.zeros_like(dk_s)
            dv_s[...] = jnp.zeros_like(dv_s)

        qrows = pl.ds(pl.multiple_of(i * tq, tq), tq)

        def step(masked):
            qb, dob = q_ref[...], do_ref[...]
            lse_r, dl_r = lse_ref[0], dl_ref[0]
            dq = jnp.zeros((tq, LANE), F32)
            nc = tq // MLA_RC
            sts = [_dot_nt(k_ref[c * MLA_RC:(c + 1) * MLA_RC, :], qb) for c in range(nc)]
            dpts = [_dot_nt(v_ref[c * MLA_RC:(c + 1) * MLA_RC, :], dob) for c in range(nc)]
            for c in range(nc):
                rows = slice(c * MLA_RC, (c + 1) * MLA_RC)
                pt = jnp.exp2(sts[c] - lse_r)
                if masked:
                    key = lax.broadcasted_iota(jnp.int32, (MLA_RC, tq), 0) + c * MLA_RC
                    qry = lax.broadcasted_iota(jnp.int32, (MLA_RC, tq), 1)
                    pt = jnp.where(key <= qry, pt, 0.0)
                dv_s[rows, :] += _dot(pt.astype(BF16), dob)
                gt = (pt * (dpts[c] - dl_r)).astype(BF16)
                dk_s[rows, :] += _dot(gt, qb)
                dq = dq + _dot_tn(gt, k_ref[rows, :])
            dq_s[qrows, :] += dq

        @pl.when(i > j)
        def _():
            step(False)

        @pl.when(i == j)
        def _():
            step(True)
            dq_ref[...] = dq_s[qrows, :] * MLA_SCALE

        @pl.when(i == nq - 1)
        def _():
            dk_ref[...] = dk_s[...] * (1.0 / LOG2E)
            dv_ref[...] = dv_s[...]

    qmap = lambda h, t, qi_r, kj_r: (qi_r[t], h)
    kmap = lambda h, t, qi_r, kj_r: (kj_r[t], h)
    rmap = lambda h, t, qi_r, kj_r: (h, 0, qi_r[t])
    return _pcall(
        body, name="mla_bwd", grid=(NH, len(pairs)), prefetch=2,
        in_specs=[pl.BlockSpec((tq, LANE), qmap), pl.BlockSpec((tq, LANE), kmap), pl.BlockSpec((tq, LANE), kmap),
                  pl.BlockSpec((tq, LANE), qmap), pl.BlockSpec((1, 1, tq), rmap), pl.BlockSpec((1, 1, tq), rmap)],
        out_specs=[pl.BlockSpec((tq, LANE), kmap), pl.BlockSpec((tq, LANE), kmap), pl.BlockSpec((tq, LANE), kmap)],
        out_shape=[_sds((S, NH * LANE), F32), _sds((S, NH * LANE), F32), _sds((S, NH * LANE), F32)],
        scratch=[pltpu.VMEM((S, LANE), F32), pltpu.VMEM((tq, LANE), F32), pltpu.VMEM((tq, LANE), F32)],
        sem=("arbitrary", "arbitrary"),
    )(qi, kj, q, k, v, do, lse, delta)


def _rows_of(t):
    return t[:, ::LANE].T.reshape(NH, 1, -1)


SWA_SUB = 4
SWA_T = SWA_SUB * WIN


def _swa_specs(nsteps, rev):
    step = (lambda i: nsteps - 1 - i) if rev else (lambda i: i)
    cur = lambda w: pl.BlockSpec((SWA_T, w), lambda i: (step(i), 0))
    prev = pl.BlockSpec((WIN, LANE), lambda i: (jnp.maximum(step(i) * SWA_SUB - 1, 0), 0))
    return step, cur, prev


def _swa_probs(qh, kkb, bias_h, sink, first_mask):
    s = _dot_nt(qh, kkb) * SWA_SCALE + bias_h
    if first_mask is not None:
        s = jnp.where(first_mask, NEG, s)
    m = jnp.maximum(jnp.max(s, axis=1, keepdims=True), sink)
    e = jnp.exp(s - m)
    es = jnp.exp(sink - m)
    inv = 1.0 / (jnp.sum(e, axis=1, keepdims=True) + es)
    return e * inv, es * inv


def _swa_fwd(qs, ks, vs, bias, sinks):
    S = qs.shape[0]
    nsteps = S // SWA_T
    step, cur, prev = _swa_specs(nsteps, False)

    def body(qs_ref, kc_ref, kp_ref, vc_ref, vp_ref, bias_ref, sk_ref, o_ref):
        first = pl.program_id(0) == 0
        kk = jnp.concatenate([kp_ref[...], kc_ref[...]], axis=0)
        vv = jnp.concatenate([vp_ref[...], vc_ref[...]], axis=0)
        col = lax.broadcasted_iota(jnp.int32, (WIN, 2 * WIN), 1)
        for b in range(SWA_SUB):
            kkb = kk[b * WIN:(b + 2) * WIN]
            vvb = vv[b * WIN:(b + 2) * WIN]
            fm = (first & (col < WIN)) if b == 0 else None
            for h in range(NH):
                qh = qs_ref[b * WIN:(b + 1) * WIN, _hs(h)]
                p, _ = _swa_probs(qh, kkb, bias_ref[h], sk_ref[h:h + 1, 0:1], fm)
                o_ref[b * WIN:(b + 1) * WIN, _hs(h)] = _dot(p.astype(BF16), vvb).astype(BF16)

    return _pcall(
        body, name="swa_fwd", grid=(nsteps,),
        in_specs=[cur(NH * LANE), cur(LANE), prev, cur(LANE), prev, _full(NH, WIN, 2 * WIN), _full(NH, LANE)],
        out_specs=cur(NH * LANE), out_shape=_sds((S, NH * LANE), BF16), sem=("arbitrary",),
    )(qs, ks, ks, vs, vs, bias, sinks)


def _swa_bwd(qs, ks, vs, do, bias, sinks):
    S = qs.shape[0]
    nsteps = S // SWA_T
    step, cur, prev = _swa_specs(nsteps, True)

    def body(qs_ref, kc_ref, kp_ref, vc_ref, vp_ref, do_ref, bias_ref, sk_ref,
             dqs_ref, dks_ref, dvs_ref, dbias_ref, dsk_ref, dkk_s, dvv_s, ck_s, cv_s):
        pid = pl.program_id(0)
        first = step(pid) == 0

        @pl.when(pid == 0)
        def _():
            dbias_ref[...] = jnp.zeros_like(dbias_ref)
            dsk_ref[...] = jnp.zeros_like(dsk_ref)
            ck_s[...] = jnp.zeros_like(ck_s)
            cv_s[...] = jnp.zeros_like(cv_s)

        dkk_s[...] = jnp.zeros_like(dkk_s)
        dvv_s[...] = jnp.zeros_like(dvv_s)
        kk = jnp.concatenate([kp_ref[...], kc_ref[...]], axis=0)
        vv = jnp.concatenate([vp_ref[...], vc_ref[...]], axis=0)
        col = lax.broadcasted_iota(jnp.int32, (WIN, 2 * WIN), 1)
        for b in range(SWA_SUB):
            kkb = kk[b * WIN:(b + 2) * WIN]
            vvb = vv[b * WIN:(b + 2) * WIN]
            fm = (first & (col < WIN)) if b == 0 else None
            rows = slice(b * WIN, (b + 1) * WIN)
            keys = slice(b * WIN, (b + 2) * WIN)
            for h in range(NH):
                qh = qs_ref[rows, _hs(h)]
                doh = do_ref[rows, _hs(h)]
                p, ps = _swa_probs(qh, kkb, bias_ref[h], sk_ref[h:h + 1, 0:1], fm)
                dp = _dot_nt(doh, vvb)
                dl = jnp.sum(p * dp, axis=1, keepdims=True)
                ds = p * (dp - dl)
                dsk_ref[h:h + 1, :] += -jnp.sum(ps * dl)
                dbias_ref[h] += ds
                dsb = (ds * SWA_SCALE).astype(BF16)
                dqs_ref[rows, _hs(h)] = _dot(dsb, kkb).astype(BF16)
                dkk_s[keys, :] += _dot_tn(dsb, qh)
                dvv_s[keys, :] += _dot_tn(p.astype(BF16), doh)
        dks_ref[...] = dkk_s[WIN:, :]
        dvs_ref[...] = dvv_s[WIN:, :]
        dks_ref[SWA_T - WIN:, :] += ck_s[...]
        dvs_ref[SWA_T - WIN:, :] += cv_s[...]
        ck_s[...] = dkk_s[0:WIN, :]
        cv_s[...] = dvv_s[0:WIN, :]

    return _pcall(
        body, name="swa_bwd", grid=(nsteps,),
        in_specs=[cur(NH * LANE), cur(LANE), prev, cur(LANE), prev, cur(NH * LANE), _full(NH, WIN, 2 * WIN),
                  _full(NH, LANE)],
        out_specs=[cur(NH * LANE), cur(LANE), cur(LANE), _full(NH, WIN, 2 * WIN), _full(NH, LANE)],
        out_shape=[_sds((S, NH * LANE), BF16), _sds((S, LANE), F32), _sds((S, LANE), F32),
                   _sds((NH, WIN, 2 * WIN), F32), _sds((NH, LANE), F32)],
        scratch=[pltpu.VMEM((SWA_T + WIN, LANE), F32), pltpu.VMEM((SWA_T + WIN, LANE), F32),
                 pltpu.VMEM((WIN, LANE), F32), pltpu.VMEM((WIN, LANE), F32)],
        sem=("arbitrary",),
    )(qs, ks, ks, vs, vs, do, bias, sinks)


def _bias_build(rel_bias, bmap):
    def body(rb_ref, bmap_ref, o_ref):
        bm = bmap_ref[...]
        for h in range(NH):
            acc = jnp.full((WIN, 2 * WIN), NEG, F32)
            for b in range(REL_BUCKETS):
                acc = jnp.where(bm == b, rb_ref[b, h], acc)
            o_ref[h] = acc

    return _pcall(
        body, name="bias_build", grid=(1,),
        in_specs=[pl.BlockSpec(memory_space=pltpu.SMEM), _full(WIN, 2 * WIN)],
        out_specs=_full(NH, WIN, 2 * WIN), out_shape=_sds((NH, WIN, 2 * WIN), F32), sem=("arbitrary",),
    )(rel_bias, bmap)


def _bias_reduce(dbias, bmap):
    def body(db_ref, bmap_ref, o_ref):
        bm = bmap_ref[...]
        for h in range(NH):
            dbh = db_ref[h]
            for b in range(REL_BUCKETS):
                o_ref[b, h] = jnp.sum(jnp.where(bm == b, dbh, 0.0))

    return _pcall(
        body, name="bias_reduce", grid=(1,),
        in_specs=[_full(NH, WIN, 2 * WIN), _full(WIN, 2 * WIN)],
        out_specs=pl.BlockSpec(memory_space=pltpu.SMEM), out_shape=_sds((REL_BUCKETS, NH), F32), sem=("arbitrary",),
    )(dbias, bmap)


def _memkv_fwd(mem, mnorm, wkv):
    def body(mem_ref, g_ref, w_ref, o_ref):
        n, _ = _rms(mem_ref[...])
        o_ref[...] = _dot((n * g_ref[...]).astype(BF16), w_ref[...]).astype(BF16)

    return _pcall(
        body, name="memkv_fwd", grid=(1,), in_specs=[_full(MEM_LEN, D), _full(1, D), _full(D, D)],
        out_specs=_full(MEM_LEN, D), out_shape=_sds((MEM_LEN, D), BF16), sem=("arbitrary",),
    )(mem, mnorm, wkv)


def _mem_probs(qh, kh):
    s = _dot_nt(qh, kh) * MEM_SCALE
    e = jnp.exp(s - jnp.max(s, axis=1, keepdims=True))
    return e / jnp.sum(e, axis=1, keepdims=True)


def _mem_fwd(qm, kvm, ts):
    S = qm.shape[0]

    def body(q_ref, kv_ref, o_ref):
        for h in range(MEM_H):
            p = _mem_probs(q_ref[:, _hs(h)], kv_ref[:, _hs(h)])
            o_ref[:, _hs(h)] = _dot(p.astype(BF16), kv_ref[:, _hs(MEM_H + h)]).astype(BF16)

    return _pcall(
        body, name="mem_fwd", grid=(S // ts,), in_specs=[_tok(ts, 512), _full(MEM_LEN, D)],
        out_specs=_tok(ts, 512), out_shape=_sds((S, 512), BF16), sem=("arbitrary",),
    )(qm, kvm)


def _mem_bwd(qm, kvm, do, ts):
    S = qm.shape[0]

    def body(q_ref, kv_ref, do_ref, dq_ref, dkv_ref):
        @pl.when(pl.program_id(0) == 0)
        def _():
            dkv_ref[...] = jnp.zeros_like(dkv_ref)

        for h in range(MEM_H):
            qh, kh, vh, doh = q_ref[:, _hs(h)], kv_ref[:, _hs(h)], kv_ref[:, _hs(MEM_H + h)], do_ref[:, _hs(h)]
            p = _mem_probs(qh, kh)
            dp = _dot_nt(doh, vh)
            ds = (p * (dp - jnp.sum(p * dp, axis=1, keepdims=True)) * MEM_SCALE).astype(BF16)
            dq_ref[:, _hs(h)] = _dot(ds, kh).astype(BF16)
            dkv_ref[:, _hs(h)] += _dot_tn(ds, qh)
            dkv_ref[:, _hs(MEM_H + h)] += _dot_tn(p.astype(BF16), doh)

    return _pcall(
        body, name="mem_bwd", grid=(S // ts,), in_specs=[_tok(ts, 512), _full(MEM_LEN, D), _tok(ts, 512)],
        out_specs=[_tok(ts, 512), _full(MEM_LEN, D)],
        out_shape=[_sds((S, 512), BF16), _sds((MEM_LEN, D), F32)], sem=("arbitrary",),
    )(qm, kvm, do)


def _memkv_bwd(mem, mnorm, wkv, dkvm):
    def body(mem_ref, g_ref, w_ref, dkv_ref, dw_ref, dg_ref):
        n, _ = _rms(mem_ref[...])
        dkvb = dkv_ref[...].astype(BF16)
        dw_ref[...] = _dot_tn((n * g_ref[...]).astype(BF16), dkvb)
        dg_ref[...] = _colsum(_dot_nt(dkvb, w_ref[...]) * n)

    return _pcall(
        body, name="memkv_bwd", grid=(1,), in_specs=[_full(MEM_LEN, D), _full(1, D), _full(D, D), _full(MEM_LEN, D)],
        out_specs=[_full(D, D), _full(1, D)], out_shape=[_sds((D, D), F32), _sds((1, D), F32)], sem=("arbitrary",),
    )(mem, mnorm, wkv, dkvm)


def _adamw(w, g, m, v, name):
    rows = w.shape[0]
    tr = min(rows, FLAT_TILE)

    def body(w_ref, g_ref, m_ref, v_ref, d_ref, nm_ref, nv_ref):
        gv = g_ref[...]
        nm = ADAM_B1 * m_ref[...] + (1.0 - ADAM_B1) * gv
        nv = ADAM_B2 * v_ref[...] + (1.0 - ADAM_B2) * jnp.square(gv)
        m_hat = nm / (1.0 - ADAM_B1 ** ADAM_STEP)
        v_hat = nv / (1.0 - ADAM_B2 ** ADAM_STEP)
        d_ref[...] = -ADAM_LR * (m_hat / (jnp.sqrt(v_hat) + ADAM_EPS) + ADAM_WD * w_ref[...])
        nm_ref[...] = nm
        nv_ref[...] = nv

    spec = _tok(tr, FLAT_W)
    return _pcall(
        body, name=name, grid=(rows // tr,), in_specs=[spec] * 4, out_specs=[spec] * 3,
        out_shape=[_sds((rows, FLAT_W), F32)] * 3, sem=("arbitrary",),
    )(w, g, m, v)


def _my_place():
    return lax.axis_index("x"), lax.axis_index("y"), lax.axis_index("c")


def _gather_shards(wl):
    R, W = wl.shape

    def body(w_ref, out_ref, send_sems, recv_sems, local_sem):
        x, y, c = _my_place()
        chips = [(1 - x, y), (x, 1 - y), (1 - x, 1 - y)]
        mine = pltpu.make_async_copy(w_ref, out_ref.at[2 * x + y], local_sem)
        mine.start()

        def copy(k, slot, to):
            return pltpu.make_async_remote_copy(src_ref=w_ref, dst_ref=out_ref.at[slot], send_sem=send_sems.at[k],
                                                recv_sem=recv_sems.at[k], device_id=to, device_id_type=MESH_ID)

        sends = [copy(k, 2 * x + y, (px, py, c)) for k, (px, py) in enumerate(chips)]
        for cp in sends:
            cp.start()
        for k, (px, py) in enumerate(chips):
            copy(k, 2 * px + py, (px, py, c)).wait_recv()
        for cp in sends:
            cp.wait_send()
        mine.wait()

    return pl.pallas_call(
        body, name="gather_shards", out_shape=_sds((4, R, W), wl.dtype),
        in_specs=[pl.BlockSpec(memory_space=pl.ANY)], out_specs=pl.BlockSpec(memory_space=pl.ANY),
        scratch_shapes=[pltpu.SemaphoreType.DMA((3,)), pltpu.SemaphoreType.DMA((3,)), pltpu.SemaphoreType.DMA],
    )(wl)


def _scatter_grads(gbig, gsmall):
    _, _, Rh, W = gbig.shape

    def body(gb_ref, gs_ref, rb_ref, rs_ref, send_sems, recv_sems, local_sems):
        x, y, c = _my_place()
        me = 4 * x + 2 * y + c
        lb = pltpu.make_async_copy(gb_ref.at[2 * x + y, c], rb_ref.at[me], local_sems.at[0])
        ls = pltpu.make_async_copy(gs_ref, rs_ref.at[me], local_sems.at[1])
        lb.start()
        ls.start()

        def peer(m):
            return (x ^ ((m >> 2) & 1), y ^ ((m >> 1) & 1), c ^ (m & 1))

        def big(m, src_slot, dst_slot, to):
            return pltpu.make_async_remote_copy(src_ref=gb_ref.at[src_slot[0], src_slot[1]], dst_ref=rb_ref.at[dst_slot],
                                                send_sem=send_sems.at[2 * m], recv_sem=recv_sems.at[2 * m],
                                                device_id=to, device_id_type=MESH_ID)

        def small(m, dst_slot, to):
            return pltpu.make_async_remote_copy(src_ref=gs_ref, dst_ref=rs_ref.at[dst_slot],
                                                send_sem=send_sems.at[2 * m + 1], recv_sem=recv_sems.at[2 * m + 1],
                                                device_id=to, device_id_type=MESH_ID)

        sends = []
        for m in range(1, 8):
            px, py, pc = peer(m)
            sends.append(big(m, (2 * px + py, pc), me, (px, py, pc)))
            sends.append(small(m, me, (px, py, pc)))
        for cp in sends:
            cp.start()
        for m in range(1, 8):
            px, py, pc = peer(m)
            them = 4 * px + 2 * py + pc
            big(m, (2 * x + y, c), them, (px, py, pc)).wait_recv()
            small(m, them, (px, py, pc)).wait_recv()
        for cp in sends:
            cp.wait_send()
        lb.wait()
        ls.wait()

    return pl.pallas_call(
        body, name="scatter_grads", out_shape=[_sds((8, Rh, W), F32), _sds((8, SMALL_ROWS, W), F32)],
        in_specs=[pl.BlockSpec(memory_space=pl.ANY)] * 2, out_specs=[pl.BlockSpec(memory_space=pl.ANY)] * 2,
        scratch_shapes=[pltpu.SemaphoreType.DMA((16,)), pltpu.SemaphoreType.DMA((16,)), pltpu.SemaphoreType.DMA((2,))],
    )(gbig, gsmall)


def _sum_slots(rb, rs):
    _, Rh, W = rb.shape
    tr = FLAT_TILE

    def body(rb_ref, rs_ref, ob_ref, os_ref):
        acc = rb_ref[0]
        for d in range(1, 8):
            acc = acc + rb_ref[d]
        ob_ref[...] = acc

        @pl.when(pl.program_id(0) == 0)
        def _():
            small = rs_ref[0]
            for d in range(1, 8):
                small = small + rs_ref[d]
            os_ref[...] = small

    return _pcall(
        body, name="sum_slots", grid=(Rh // tr,),
        in_specs=[pl.BlockSpec((8, tr, W), lambda i: (0, i, 0)), _full(8, SMALL_ROWS, W)],
        out_specs=[_tok(tr, W), _full(SMALL_ROWS, W)],
        out_shape=[_sds((Rh, W), F32), _sds((SMALL_ROWS, W), F32)], sem=("arbitrary",),
    )(rb, rs)


def _swap_halves(red):
    Rh, W = red.shape

    def body(r_ref, out_ref, send_sem, recv_sem, local_sem):
        x, y, c = _my_place()
        mine = pltpu.make_async_copy(r_ref, out_ref.at[c], local_sem)
        mine.start()
        send = pltpu.make_async_remote_copy(src_ref=r_ref, dst_ref=out_ref.at[c], send_sem=send_sem, recv_sem=recv_sem,
                                            device_id=(x, y, 1 - c), device_id_type=MESH_ID)
        send.start()
        pltpu.make_async_remote_copy(src_ref=r_ref, dst_ref=out_ref.at[1 - c], send_sem=send_sem, recv_sem=recv_sem,
                                     device_id=(x, y, 1 - c), device_id_type=MESH_ID).wait_recv()
        send.wait_send()
        mine.wait()

    return pl.pallas_call(
        body, name="swap_halves", out_shape=_sds((2, Rh, W), F32),
        in_specs=[pl.BlockSpec(memory_space=pl.ANY)], out_specs=pl.BlockSpec(memory_space=pl.ANY),
        scratch_shapes=[pltpu.SemaphoreType.DMA, pltpu.SemaphoreType.DMA, pltpu.SemaphoreType.DMA],
    )(red)


def _pad_heads_cols(w, per, width=LANE):
    k = w.shape[0]
    t = w.reshape(k, -1, per)
    return jnp.pad(t, ((0, 0), (0, 0), (0, width - per))).reshape(k, -1)


def _unpad_heads_cols(w, per, width=LANE):
    k = w.shape[0]
    return w.reshape(k, -1, width)[:, :, :per].reshape(k, -1)


def _swa_place(t):
    z = jnp.zeros_like(t)
    lo = jnp.concatenate([t, z], axis=1)
    hi = jnp.concatenate([z, t], axis=1)
    group = (jnp.arange(NH) // SWA_R).reshape((NH,) + (1,) * (t.ndim - 1))
    full = jnp.where(group == 0, lo, hi)
    return full.reshape((NH * LANE,) + t.shape[2:])


def _swa_unplace(t):
    t = t.reshape((NH, 2, 64) + t.shape[1:])
    return jnp.concatenate([t[:SWA_R, 0], t[SWA_R:, 1]], axis=0)


def _pad_w_in(w):
    z = lambda n: jnp.zeros((D, n), w.dtype)
    cq, ckv, kpe = w[:, 0:256], w[:, 256:384], w[:, 384:416]
    qs, ks, vs, qm, gt = w[:, 416:928], w[:, 928:1056], w[:, 1056:1184], w[:, 1184:1696], w[:, 1696:4768]
    qs_p = _swa_place(qs.T.reshape(NH, 64, D)).T
    return jnp.concatenate([cq, ckv, z(64), kpe, z(32), qs_p, ks, vs, qm, gt], axis=1)


def _unpad_w_in(g):
    qs = _swa_unplace(g[:, C_QS:C_KS].T).reshape(NH * 64, D).T
    return jnp.concatenate([g[:, C_CQ:C_KPE], g[:, C_KPE + 64:C_KPE + 96], qs, g[:, C_KS:C_END]], axis=1)


def _pad_w_ukv(w):
    t = w.reshape(128, NH, 2, 64)
    pad = lambda a: jnp.pad(a, ((0, 0), (0, 0), (0, 64))).reshape(128, NH * LANE)
    return jnp.concatenate([pad(t[:, :, 0]), pad(t[:, :, 1])], axis=1)


def _unpad_w_ukv(g):
    k = g[:, :NH * LANE].reshape(128, NH, LANE)[:, :, :64]
    v = g[:, NH * LANE:].reshape(128, NH, LANE)[:, :, :64]
    return jnp.stack([k, v], axis=2).reshape(128, NH * LANE)


def _pad_w_o_mla(w):
    return jnp.pad(w.reshape(NH, 64, D), ((0, 0), (0, 64), (0, 0))).reshape(NH * LANE, D)


def _unpad_w_o_mla(g):
    return g.reshape(NH, LANE, D)[:, :64].reshape(NH * 64, D)


def _rope_tables(S):
    pos = jnp.arange(S, dtype=F32)
    inv = 1.0 / (ROPE_THETA ** (jnp.arange(0, 32, 2, dtype=F32) / 32))
    ang = pos[:, None] * inv[None, :]
    cos, sin = jnp.cos(ang), jnp.sin(ang)
    one, zero = jnp.ones((S, 64), F32), jnp.zeros((S, 16), F32)
    rc = jnp.concatenate([one, cos, cos, jnp.ones((S, 32), F32)], axis=1)
    rs1 = jnp.concatenate([jnp.zeros((S, 64), F32), zero, sin, jnp.zeros((S, 32), F32)], axis=1)
    rs2 = jnp.concatenate([jnp.zeros((S, 64), F32), -sin, zero, jnp.zeros((S, 32), F32)], axis=1)
    return rc, rs1, rs2


def _bucket_map():
    qi = jnp.arange(WIN)[:, None]
    kj = jnp.arange(2 * WIN)[None, :]
    dist = qi + WIN - kj
    n = jnp.maximum(dist, 0)
    max_exact = REL_BUCKETS // 2
    nf = jnp.maximum(n, 1).astype(F32)
    large = max_exact + (jnp.log(nf / max_exact) / math.log(128 / max_exact)
                         * (REL_BUCKETS - max_exact)).astype(jnp.int32)
    large = jnp.minimum(large, REL_BUCKETS - 1)
    bucket = jnp.where(n < max_exact, n, large)
    return jnp.where((dist >= 0) & (dist < WIN), bucket, -1).astype(jnp.int32)


TS = 256
TQ = 1024
TQ_FWD = 1024


def _layer_weights(wf, l):
    return dict(
        win=_pad_w_in(wf["w_in"][l]),
        wuq=_pad_heads_cols(wf["w_uq"][l], 96),
        wukv=_pad_w_ukv(wf["w_ukv"][l]),
        wmem=wf["w_mem_kv"][l],
        wa=_pad_w_o_mla(wf["w_o_mla"][l]),
        wb=_swa_place(wf["w_o_swa"][l].reshape(NH, 64, D)),
        wc=wf["w_o_mem"][l],
        wout=wf["w_out"][l],
        wup=wf["w_up"][l],
        wdown=wf["w_down"][l],
    )


def _local_step(x, mem, tgt, wf, sp):
    S = x.shape[0]
    ts = min(TS, S)
    tq = min(TQ, S)
    rc, rs1, rs2 = _rope_tables(S)
    bmap = _bucket_map()
    bias = _bias_build(sp["rel_bias"], bmap)
    row = lambda v: v.reshape(1, -1)

    saved = []
    for l in range(DEPTH):
        w = _layer_weights(wf, l)
        an, qn, kvn = row(sp["attn_norm"][l]), row(sp["mla_q_norm"][l]), row(sp["mla_kv_norm"][l])
        bg, mnorm, mlpn = row(sp["b_gate"][l]), row(sp["mem_norm"][l]), row(sp["mlp_norm"][l])
        sinks = jnp.broadcast_to(sp["attn_sinks"][l][:, None], (NH, LANE))
        q, k, v, qs, ks, vs, qm, g = _pre_fwd(x, an, w["win"], bg, qn, kvn, w["wuq"], w["wukv"], rc, rs1, rs2, ts)
        oa, lse = _mla_fwd(q, k, v, min(TQ_FWD, S))
        ob = _swa_fwd(qs, ks, vs, bias, sinks)
        kvm = _memkv_fwd(mem, mnorm, w["wmem"])
        oc = _mem_fwd(qm, kvm, ts)
        x1, yb = _merge_fwd(x, g, oa, ob, oc, w["wa"], w["wb"], w["wc"], w["wout"], ts)
        x2 = _mlp_fwd(x1, mlpn, w["wup"], w["wdown"], ts)
        saved.append(dict(w=w, x=x, x1=x1, q=q, k=k, v=v, qs=qs, ks=ks, vs=vs, qm=qm, g=g, oa=oa, lse=lse, ob=ob,
                          oc=oc, kvm=kvm, yb=yb, an=an, qn=qn, kvn=kvn, mnorm=mnorm, mlpn=mlpn, sinks=sinks))
        x = x2

    sq, dx, dfn = _loss_kernel(x, row(sp["final_norm"]), tgt, ts)

    grads = {n: [None] * DEPTH for n in WEIGHTS if n not in ("rel_bias", "final_norm")}
    dbias_total = None
    for l in reversed(range(DEPTH)):
        sv = saved[l]
        w = sv["w"]
        dx1, hb2, dub, ab, dxb, dmlpn = _mlp_bwd(dx, sv["x1"], sv["mlpn"], w["wup"], w["wdown"], ts)
        grads["w_up"][l] = _matmul_tn(hb2, dub, "dw_up")
        grads["w_down"][l] = _matmul_tn(ab, dxb, "dw_down")
        grads["mlp_norm"][l] = dmlpn[0]

        dgp, dyo, doa, dob, doc, dla, dx1b, dbg = _merge_bwd(dx1, sv["g"], sv["oa"], sv["ob"], sv["oc"], w["wa"],
                                                             w["wb"], w["wc"], w["wout"], ts)
        grads["w_out"][l] = _matmul_tn(sv["yb"], dx1b, "dw_out")
        grads["w_o_mla"][l] = _unpad_w_o_mla(_matmul_tn(sv["oa"], dyo[:, 0:D], "dw_o_mla"))
        grads["w_o_swa"][l] = _swa_unplace(_matmul_tn(sv["ob"], dyo[:, D:2 * D], "dw_o_swa")).reshape(NH * 64, D)
        grads["w_o_mem"][l] = _matmul_tn(sv["oc"], dyo[:, 2 * D:3 * D], "dw_o_mem")
        grads["b_gate"][l] = dbg[0]

        dqm, dkvm = _mem_bwd(sv["qm"], sv["kvm"], doc, ts)
        dwmem, dmnorm = _memkv_bwd(mem, sv["mnorm"], w["wmem"], dkvm)
        grads["w_mem_kv"][l] = dwmem
        grads["mem_norm"][l] = dmnorm[0]

        dqs, dks, dvs, dbias, dsink = _swa_bwd(sv["qs"], sv["ks"], sv["vs"], dob, bias, sv["sinks"])
        dbias_total = dbias if dbias_total is None else dbias_total + dbias
        grads["attn_sinks"][l] = dsink[:, 0]

        dq, dk, dv = _mla_bwd(sv["q"], sv["k"], sv["v"], doa, _rows_of(sv["lse"]), _rows_of(dla), tq)

        dx, dproj, hb, cqn, ckvn, dqpre, dkv, dan, dqn, dkvn = _pre_bwd(
            sv["x"], dx1, dq, dk, dv, dqs, dks, dvs, dqm, dgp, sv["an"], sv["qn"], sv["kvn"], w["win"], w["wuq"],
            w["wukv"], rc, rs1, rs2, ts)
        grads["w_in"][l] = _unpad_w_in(_matmul_tn(hb, dproj, "dw_in"))
        grads["w_uq"][l] = _unpad_heads_cols(_matmul_tn(cqn, dqpre, "dw_uq"), 96)
        grads["w_ukv"][l] = _unpad_w_ukv(_matmul_tn(ckvn, dkv, "dw_ukv"))
        grads["attn_norm"][l] = dan[0]
        grads["mla_q_norm"][l] = dqn[0]
        grads["mla_kv_norm"][l] = dkvn[0]

    out = {n: jnp.stack(v) for n, v in grads.items()}
    out["rel_bias"] = _bias_reduce(dbias_total, bmap)
    out["final_norm"] = dfn[0]
    return sq, dx, out


def _flat_rows(n_elems, multiple):
    rows = -(-n_elems // FLAT_W)
    return -(-rows // multiple) * multiple


def _flatten(parts, rows):
    flat = jnp.concatenate([p.reshape(-1) for p in parts])
    return jnp.pad(flat, (0, rows * FLAT_W - flat.shape[0])).reshape(rows, FLAT_W)


def _unflatten(buf, shapes):
    flat = buf.reshape(-1)
    out, at = [], 0
    for s in shapes:
        n = int(np.prod(s))
        out.append(flat[at:at + n].reshape(s))
        at += n
    return out


def _shard_of(g, axis, s):
    n = g.shape[axis] // 4
    return lax.slice_in_dim(g, s * n, (s + 1) * n, axis=axis)


def kernel(x, mem, rel_bias, attn_norm, mem_norm, w_in, b_gate, mla_q_norm, w_uq, mla_kv_norm, w_ukv, attn_sinks, w_mem_kv, w_o_mla, w_o_swa, w_o_mem, w_out, mlp_norm, w_up, w_down, final_norm, loss_target, m_rel_bias, m_attn_norm, m_mem_norm, m_w_in, m_b_gate, m_mla_q_norm, m_w_uq, m_mla_kv_norm, m_w_ukv, m_attn_sinks, m_w_mem_kv, m_w_o_mla, m_w_o_swa, m_w_o_mem, m_w_out, m_mlp_norm, m_w_up, m_w_down, m_final_norm, v_rel_bias, v_attn_norm, v_mem_norm, v_w_in, v_b_gate, v_mla_q_norm, v_w_uq, v_mla_kv_norm, v_w_ukv, v_attn_sinks, v_w_mem_kv, v_w_o_mla, v_w_o_swa, v_w_o_mem, v_w_out, v_mlp_norm, v_w_up, v_w_down, v_final_norm):
    args = dict(locals())
    W = {n: args[n] for n in WEIGHTS}
    M = {n: args["m_" + n] for n in WEIGHTS}
    V = {n: args["v_" + n] for n in WEIGHTS}
    shard_shapes = [W[n].shape for n in SHARDED]
    small_shapes = [W[n].shape for n in SMALL]
    R = _flat_rows(sum(int(np.prod(s)) for s in shard_shapes), 2 * FLAT_TILE)

    wl = _flatten([W[n].astype(BF16) for n in SHARDED], R)
    gathered = _gather_shards(wl)
    per_chip = [_unflatten(gathered[s], shard_shapes) for s in range(4)]
    wf = {n: jnp.concatenate([per_chip[s][i] for s in range(4)], axis=SHARD_AXIS[n]) for i, n in enumerate(SHARDED)}
    sp = {n: W[n] for n in SMALL}

    sq, dx, grads = _local_step(x[0], mem[0], loss_target[0], wf, sp)

    gbig = jnp.stack([_flatten([_shard_of(grads[n], SHARD_AXIS[n], s) for n in SHARDED], R) for s in range(4)])
    gsmall = _flatten([grads[n] for n in SMALL], SMALL_ROWS)
    rb, rs = _scatter_grads(gbig.reshape(4, 2, R // 2, FLAT_W), gsmall)
    red_half, red_small = _sum_slots(rb, rs)
    g_shard = _swap_halves(red_half).reshape(R, FLAT_W)

    d_b, m_b, v_b = _adamw(_flatten([W[n] for n in SHARDED], R), g_shard, _flatten([M[n] for n in SHARDED], R),
                           _flatten([V[n] for n in SHARDED], R), "adamw_shard")
    d_s, m_s, v_s = _adamw(_flatten([W[n] for n in SMALL], SMALL_ROWS), red_small,
                           _flatten([M[n] for n in SMALL], SMALL_ROWS), _flatten([V[n] for n in SMALL], SMALL_ROWS),
                           "adamw_small")

    def named(big, small):
        out = dict(zip(SHARDED, _unflatten(big, shard_shapes)))
        out.update(zip(SMALL, _unflatten(small, small_shapes)))
        return out

    G, DW, NM, NV = named(g_shard, red_small), named(d_b, d_s), named(m_b, m_s), named(v_b, v_s)
    loss = lax.psum(0.5 * sq[0, 0] / D, ("x", "y", "c"))
    return (loss, dx[None], *[G[n] for n in WEIGHTS], *[DW[n] for n in WEIGHTS], *[NM[n] for n in WEIGHTS],
            *[NV[n] for n in WEIGHTS])
```

```python
import functools
import math

import numpy as np
import jax
import jax.numpy as jnp
from jax import lax
from jax.experimental import pallas as pl
from jax.experimental.pallas import tpu as pltpu

F32 = jnp.float32
BF16 = jnp.bfloat16

D = 1024
DFF = 4096
DEPTH = 2
EPS = 1e-6
LANE = 128
NH = 8
SWA_R = 4
MEM_H = 4
MEM_LEN = 256
WIN = 128
NEG = -1e30
MLA_SCALE = 96 ** -0.5
SWA_SCALE = 64 ** -0.5
MEM_SCALE = 128 ** -0.5
REL_BUCKETS = 32
ROPE_THETA = 10000.0

C_CQ, C_CKV, C_KPE, C_QS, C_KS, C_VS, C_QM, C_G, C_END = 0, 256, 384, 512, 1536, 1664, 1792, 2304, 5376
IN_COLS = 4768

ADAM_LR = 0.001
ADAM_B1 = 0.9
ADAM_B2 = 0.999
ADAM_EPS = 1e-08
ADAM_WD = 0.01
ADAM_STEP = 10

VMEM_LIMIT = 56 * 1024 * 1024

SHARDED = ("w_in", "w_uq", "w_ukv", "w_mem_kv", "w_o_mla", "w_o_swa", "w_o_mem", "w_out", "w_up", "w_down")
SHARD_AXIS = {"w_in": 2, "w_uq": 2, "w_ukv": 2, "w_mem_kv": 1, "w_o_mla": 2, "w_o_swa": 2, "w_o_mem": 2,
              "w_out": 1, "w_up": 2, "w_down": 1}
SMALL = ("rel_bias", "attn_norm", "mem_norm", "b_gate", "mla_q_norm", "mla_kv_norm", "attn_sinks", "mlp_norm",
         "final_norm")
WEIGHTS = ("rel_bias", "attn_norm", "mem_norm", "w_in", "b_gate", "mla_q_norm", "w_uq", "mla_kv_norm", "w_ukv",
           "attn_sinks", "w_mem_kv", "w_o_mla", "w_o_swa", "w_o_mem", "w_out", "mlp_norm", "w_up", "w_down",
           "final_norm")
FLAT_W = 1024
FLAT_TILE = 256
SMALL_ROWS = 16
SWAP_STREAMS = 8
MESH_ID = pl.DeviceIdType.MESH


def _pcall(body, *, name, grid, in_specs, out_specs, out_shape, scratch=(), prefetch=0, sem=None):
    params = pltpu.CompilerParams(dimension_semantics=sem, vmem_limit_bytes=VMEM_LIMIT)
    if prefetch:
        spec = pltpu.PrefetchScalarGridSpec(num_scalar_prefetch=prefetch, grid=grid, in_specs=in_specs,
                                            out_specs=out_specs, scratch_shapes=scratch)
        return pl.pallas_call(body, name=name, grid_spec=spec, out_shape=out_shape, compiler_params=params)
    return pl.pallas_call(body, name=name, grid=grid, in_specs=in_specs, out_specs=out_specs, out_shape=out_shape,
                          scratch_shapes=scratch, compiler_params=params)


def _tok(ts, w):
    return pl.BlockSpec((ts, w), lambda i: (i, 0))


def _full(*shape):
    return pl.BlockSpec(shape, lambda *_: (0,) * len(shape))


def _sds(shape, dtype):
    return jax.ShapeDtypeStruct(shape, dtype)


def _dot(a, b):
    return jnp.dot(a, b, preferred_element_type=F32)


def _dot_nt(a, b):
    return lax.dot_general(a, b, (((1,), (1,)), ((), ())), preferred_element_type=F32)


def _dot_tn(a, b):
    return lax.dot_general(a, b, (((0,), (0,)), ((), ())), preferred_element_type=F32)


def _rms(x):
    r = lax.rsqrt(jnp.mean(x * x, axis=-1, keepdims=True) + EPS)
    return x * r, r


def _rms_bwd(dyg, n, r):
    return r * (dyg - n * jnp.mean(n * dyg, axis=-1, keepdims=True))


def _rope(t, c, s1, s2):
    return t * c + pltpu.roll(t, 16, 1) * s1 + pltpu.roll(t, LANE - 16, 1) * s2


def _rope_bwd(dy, c, s1, s2):
    return dy * c + pltpu.roll(dy * s1, LANE - 16, 1) + pltpu.roll(dy * s2, 16, 1)


def _hs(h):
    return slice(h * LANE, (h + 1) * LANE)


def _colsum(t):
    return jnp.sum(t, axis=0, keepdims=True)


def _pre_fwd(x, an, win, bg, qn, kvn, wuq, wukv, rc, rs1, rs2, ts):
    S = x.shape[0]

    def body(x_ref, an_ref, win_ref, bg_ref, qn_ref, kvn_ref, wuq_ref, wukv_ref, rc_ref, rs1_ref, rs2_ref,
             q_ref, k_ref, v_ref, qs_ref, ks_ref, vs_ref, qm_ref, g_ref):
        n, _ = _rms(x_ref[...])
        hb = (n * an_ref[...]).astype(BF16)
        pa = _dot(hb, win_ref[:, C_CQ:C_QS])
        ncq, _ = _rms(pa[:, 0:256])
        cqn = (ncq * qn_ref[...]).astype(BF16)
        nkv, _ = _rms(pa[:, 256:384])
        ckvn = (nkv * kvn_ref[...]).astype(BF16)
        c, s1, s2 = rc_ref[...], rs1_ref[...], rs2_ref[...]
        kper = _rope(pa[:, 384:512], c, s1, s2)
        qp = _dot(cqn, wuq_ref[...])
        kv = _dot(ckvn, wukv_ref[...])
        for h in range(NH):
            q_ref[:, _hs(h)] = (_rope(qp[:, _hs(h)], c, s1, s2) * MLA_QSCALE).astype(BF16)
            k_ref[:, _hs(h)] = (kv[:, _hs(h)] + kper).astype(BF16)
        v_ref[...] = kv[:, NH * LANE:].astype(BF16)
        pb = _dot(hb, win_ref[:, C_QS:C_G])
        qs_ref[...] = pb[:, 0:1024].astype(BF16)
        ks_ref[...] = pb[:, 1024:1152].astype(BF16)
        vs_ref[...] = pb[:, 1152:1280].astype(BF16)
        qm_ref[...] = pb[:, 1280:1792].astype(BF16)
        g_ref[...] = jax.nn.sigmoid(_dot(hb, win_ref[:, C_G:C_END]) + bg_ref[...])

    return _pcall(
        body, name="pre_fwd", grid=(S // ts,),
        in_specs=[_tok(ts, D), _full(1, D), _full(D, C_END), _full(1, 3 * D), _full(1, 256), _full(1, 128),
                  _full(256, NH * LANE), _full(128, 2 * NH * LANE), _tok(ts, LANE), _tok(ts, LANE), _tok(ts, LANE)],
        out_specs=[_tok(ts, 1024), _tok(ts, 1024), _tok(ts, 1024), _tok(ts, 1024), _tok(ts, 128), _tok(ts, 128),
                   _tok(ts, 512), _tok(ts, 3 * D)],
        out_shape=[_sds((S, 1024), BF16), _sds((S, 1024), BF16), _sds((S, 1024), BF16), _sds((S, 1024), BF16),
                   _sds((S, 128), BF16), _sds((S, 128), BF16), _sds((S, 512), BF16), _sds((S, 3 * D), F32)],
        sem=("arbitrary",),
    )(x, an, win, bg, qn, kvn, wuq, wukv, rc, rs1, rs2)


def _merge_fwd(x, g, oa, ob, oc, wa, wb, wc, wout, ts):
    S = x.shape[0]

    def body(x_ref, g_ref, oa_ref, ob_ref, oc_ref, wa_ref, wb_ref, wc_ref, wout_ref, x1_ref, yb_ref):
        y = g_ref[:, 0:D] * _dot(oa_ref[...], wa_ref[...])
        y = y + g_ref[:, D:2 * D] * _dot(ob_ref[...], wb_ref[...])
        y = y + g_ref[:, 2 * D:3 * D] * _dot(oc_ref[...], wc_ref[...])
        yb = y.astype(BF16)
        yb_ref[...] = yb
        x1_ref[...] = x_ref[...] + _dot(yb, wout_ref[...])

    return _pcall(
        body, name="merge_fwd", grid=(S // ts,),
        in_specs=[_tok(ts, D), _tok(ts, 3 * D), _tok(ts, 1024), _tok(ts, 1024), _tok(ts, 512),
                  _full(1024, D), _full(1024, D), _full(512, D), _full(D, D)],
        out_specs=[_tok(ts, D), _tok(ts, D)],
        out_shape=[_sds((S, D), F32), _sds((S, D), BF16)],
        sem=("arbitrary",),
    )(x, g, oa, ob, oc, wa, wb, wc, wout)


def _mlp_fwd(x1, mn, wup, wdown, ts):
    S = x1.shape[0]

    def body(x_ref, mn_ref, wup_ref, wdown_ref, x2_ref):
        xv = x_ref[...]
        n, _ = _rms(xv)
        u = _dot((n * mn_ref[...]).astype(BF16), wup_ref[...])
        a = jnp.square(jnp.maximum(u, 0.0))
        x2_ref[...] = xv + _dot(a.astype(BF16), wdown_ref[...])

    return _pcall(
        body, name="mlp_fwd", grid=(S // ts,),
        in_specs=[_tok(ts, D), _full(1, D), _full(D, DFF), _full(DFF, D)],
        out_specs=_tok(ts, D), out_shape=_sds((S, D), F32), sem=("arbitrary",),
    )(x1, mn, wup, wdown)


def _loss_kernel(x, fn, tgt, ts):
    S = x.shape[0]

    def body(x_ref, fn_ref, t_ref, loss_ref, dx_ref, dfn_ref):
        @pl.when(pl.program_id(0) == 0)
        def _():
            loss_ref[...] = jnp.zeros_like(loss_ref)
            dfn_ref[...] = jnp.zeros_like(dfn_ref)

        n, r = _rms(x_ref[...])
        err = n * fn_ref[...] - t_ref[...]
        loss_ref[...] += jnp.sum(err * err)
        dy = err * (1.0 / D)
        dfn_ref[...] += _colsum(dy * n)
        dx_ref[...] = _rms_bwd(dy * fn_ref[...], n, r)

    return _pcall(
        body, name="loss_head", grid=(S // ts,),
        in_specs=[_tok(ts, D), _full(1, D), _tok(ts, D)],
        out_specs=[_full(1, LANE), _tok(ts, D), _full(1, D)],
        out_shape=[_sds((1, LANE), F32), _sds((S, D), F32), _sds((1, D), F32)],
        sem=("arbitrary",),
    )(x, fn, tgt)


def _mlp_bwd(dx2, x1, mn, wup, wdown, ts):
    S = x1.shape[0]

    def body(dx_ref, x_ref, mn_ref, wup_ref, wdown_ref, dx1_ref, hb_ref, dub_ref, ab_ref, dxb_ref, dmn_ref):
        @pl.when(pl.program_id(0) == 0)
        def _():
            dmn_ref[...] = jnp.zeros_like(dmn_ref)

        dx = dx_ref[...]
        n, r = _rms(x_ref[...])
        g = mn_ref[...]
        hb = (n * g).astype(BF16)
        hb_ref[...] = hb
        rl = jnp.maximum(_dot(hb, wup_ref[...]), 0.0)
        ab_ref[...] = jnp.square(rl).astype(BF16)
        dxb = dx.astype(BF16)
        dxb_ref[...] = dxb
        dub = (_dot_nt(dxb, wdown_ref[...]) * (2.0 * rl)).astype(BF16)
        dub_ref[...] = dub
        dh = _dot_nt(dub, wup_ref[...])
        dmn_ref[...] += _colsum(dh * n)
        dx1_ref[...] = dx + _rms_bwd(dh * g, n, r)

    return _pcall(
        body, name="mlp_bwd", grid=(S // ts,),
        in_specs=[_tok(ts, D), _tok(ts, D), _full(1, D), _full(D, DFF), _full(DFF, D)],
        out_specs=[_tok(ts, D), _tok(ts, D), _tok(ts, DFF), _tok(ts, DFF), _tok(ts, D), _full(1, D)],
        out_shape=[_sds((S, D), F32), _sds((S, D), BF16), _sds((S, DFF), BF16), _sds((S, DFF), BF16),
                   _sds((S, D), BF16), _sds((1, D), F32)],
        sem=("arbitrary",),
    )(dx2, x1, mn, wup, wdown)


def _merge_bwd(dx1, g, oa, ob, oc, wa, wb, wc, wout, ts):
    S = dx1.shape[0]

    def body(dx_ref, g_ref, oa_ref, ob_ref, oc_ref, wa_ref, wb_ref, wc_ref, wout_ref,
             dgp_ref, dyo_ref, doa_ref, dob_ref, doc_ref, dla_ref, dxb_ref, dbg_ref):
        @pl.when(pl.program_id(0) == 0)
        def _():
            dbg_ref[...] = jnp.zeros_like(dbg_ref)

        dxb = dx_ref[...].astype(BF16)
        dxb_ref[...] = dxb
        dy = _dot_nt(dxb, wout_ref[...])
        branches = ((oa_ref, wa_ref, doa_ref), (ob_ref, wb_ref, dob_ref), (oc_ref, wc_ref, doc_ref))
        for b, (o_ref, w_ref, do_ref) in enumerate(branches):
            cols = slice(b * D, (b + 1) * D)
            gb = g_ref[:, cols]
            o = o_ref[...]
            dgpre = dy * _dot(o, w_ref[...]) * gb * (1.0 - gb)
            dgp_ref[:, cols] = dgpre.astype(BF16)
            dbg_ref[:, cols] += _colsum(dgpre)
            dyo = (dy * gb).astype(BF16)
            dyo_ref[:, cols] = dyo
            do = _dot_nt(dyo, w_ref[...])
            do_ref[...] = do.astype(BF16)
            if b == 0:
                for h in range(NH):
                    dl = jnp.sum(do[:, _hs(h)] * o[:, _hs(h)].astype(F32), axis=1, keepdims=True)
                    dla_ref[:, _hs(h)] = jnp.broadcast_to(dl, (ts, LANE))

    return _pcall(
        body, name="merge_bwd", grid=(S // ts,),
        in_specs=[_tok(ts, D), _tok(ts, 3 * D), _tok(ts, 1024), _tok(ts, 1024), _tok(ts, 512),
                  _full(1024, D), _full(1024, D), _full(512, D), _full(D, D)],
        out_specs=[_tok(ts, 3 * D), _tok(ts, 3 * D), _tok(ts, 1024), _tok(ts, 1024), _tok(ts, 512), _tok(ts, 1024),
                   _tok(ts, D), _full(1, 3 * D)],
        out_shape=[_sds((S, 3 * D), BF16), _sds((S, 3 * D), BF16), _sds((S, 1024), BF16), _sds((S, 1024), BF16),
                   _sds((S, 512), BF16), _sds((S, 1024), F32), _sds((S, D), BF16), _sds((1, 3 * D), F32)],
        sem=("arbitrary",),
    )(dx1, g, oa, ob, oc, wa, wb, wc, wout)


def _pre_bwd(x, dx1, dq, dk, dv, dqs, dks, dvs, dqm, dgp, an, qn, kvn, win, wuq, wukv, rc, rs1, rs2, ts):
    S = x.shape[0]

    def body(x_ref, dx1_ref, dq_ref, dk_ref, dv_ref, dqs_ref, dks_ref, dvs_ref, dqm_ref, dgp_ref,
             an_ref, qn_ref, kvn_ref, win_ref, wuq_ref, wukv_ref, rc_ref, rs1_ref, rs2_ref,
             dx_ref, dproj_ref, hb_ref, cqn_ref, ckvn_ref, dqpre_ref, dkv_ref, dan_ref, dqn_ref, dkvn_ref):
        @pl.when(pl.program_id(0) == 0)
        def _():
            dan_ref[...] = jnp.zeros_like(dan_ref)
            dqn_ref[...] = jnp.zeros_like(dqn_ref)
            dkvn_ref[...] = jnp.zeros_like(dkvn_ref)

        n, r = _rms(x_ref[...])
        hb = (n * an_ref[...]).astype(BF16)
        hb_ref[...] = hb
        pa = _dot(hb, win_ref[:, C_CQ:C_KPE])
        ncq, rq = _rms(pa[:, 0:256])
        cqn_ref[...] = (ncq * qn_ref[...]).astype(BF16)
        nkv, rkv = _rms(pa[:, 256:384])
        ckvn_ref[...] = (nkv * kvn_ref[...]).astype(BF16)
        c, s1, s2 = rc_ref[...], rs1_ref[...], rs2_ref[...]

        dkper = jnp.zeros((ts, LANE), F32)
        for h in range(NH):
            dqpre_ref[:, _hs(h)] = _rope_bwd(dq_ref[:, _hs(h)], c, s1, s2).astype(BF16)
            dkh = dk_ref[:, _hs(h)]
            dkper = dkper + dkh
            dkv_ref[:, _hs(h)] = dkh.astype(BF16)
        dkv_ref[:, NH * LANE:] = dv_ref[...].astype(BF16)

        dcqn = _dot_nt(dqpre_ref[...], wuq_ref[...])
        dqn_ref[...] += _colsum(dcqn * ncq)
        dproj_ref[:, C_CQ:C_CKV] = _rms_bwd(dcqn * qn_ref[...], ncq, rq).astype(BF16)
        dckvn = _dot_nt(dkv_ref[...], wukv_ref[...])
        dkvn_ref[...] += _colsum(dckvn * nkv)
        dproj_ref[:, C_CKV:C_KPE] = _rms_bwd(dckvn * kvn_ref[...], nkv, rkv).astype(BF16)
        lane = lax.broadcasted_iota(jnp.int32, (ts, LANE), 1)
        dkpe = jnp.where((lane >= 64) & (lane < 96), _rope_bwd(dkper, c, s1, s2), 0.0)
        dproj_ref[:, C_KPE:C_QS] = dkpe.astype(BF16)
        dproj_ref[:, C_QS:C_KS] = dqs_ref[...]
        dproj_ref[:, C_KS:C_VS] = dks_ref[...].astype(BF16)
        dproj_ref[:, C_VS:C_QM] = dvs_ref[...].astype(BF16)
        dproj_ref[:, C_QM:C_G] = dqm_ref[...]
        dproj_ref[:, C_G:C_END] = dgp_ref[...]

        dh = _dot_nt(dproj_ref[...], win_ref[...])
        dan_ref[...] += _colsum(dh * n)
        dx_ref[...] = dx1_ref[...] + _rms_bwd(dh * an_ref[...], n, r)

    return _pcall(
        body, name="pre_bwd", grid=(S // ts,),
        in_specs=[_tok(ts, D), _tok(ts, D), _tok(ts, 1024), _tok(ts, 1024), _tok(ts, 1024), _tok(ts, 1024),
                  _tok(ts, 128), _tok(ts, 128), _tok(ts, 512), _tok(ts, 3 * D),
                  _full(1, D), _full(1, 256), _full(1, 128), _full(D, C_END), _full(256, NH * LANE),
                  _full(128, 2 * NH * LANE), _tok(ts, LANE), _tok(ts, LANE), _tok(ts, LANE)],
        out_specs=[_tok(ts, D), _tok(ts, C_END), _tok(ts, D), _tok(ts, 256), _tok(ts, 128), _tok(ts, 1024),
                   _tok(ts, 2048), _full(1, D), _full(1, 256), _full(1, 128)],
        out_shape=[_sds((S, D), F32), _sds((S, C_END), BF16), _sds((S, D), BF16), _sds((S, 256), BF16),
                   _sds((S, 128), BF16), _sds((S, 1024), BF16), _sds((S, 2048), BF16), _sds((1, D), F32),
                   _sds((1, 256), F32), _sds((1, 128), F32)],
        sem=("arbitrary",),
    )(x, dx1, dq, dk, dv, dqs, dks, dvs, dqm, dgp, an, qn, kvn, win, wuq, wukv, rc, rs1, rs2)


def _pick_tile(n, cap):
    best = LANE
    for t in range(LANE, min(n, cap) + 1, LANE):
        if n % t == 0:
            best = t
    return best


def _matmul_tn(a, b, name):
    S, M = a.shape
    N = b.shape[1]
    tm = _pick_tile(M, 1024)
    tn = _pick_tile(N, 2048)
    ts = min(S, 1024)

    def body(a_ref, b_ref, o_ref):
        @pl.when(pl.program_id(2) == 0)
        def _():
            o_ref[...] = jnp.zeros_like(o_ref)

        o_ref[...] += _dot_tn(a_ref[...], b_ref[...])

    return _pcall(
        body, name=name, grid=(M // tm, N // tn, S // ts),
        in_specs=[pl.BlockSpec((ts, tm), lambda i, j, k: (k, i)), pl.BlockSpec((ts, tn), lambda i, j, k: (k, j))],
        out_specs=pl.BlockSpec((tm, tn), lambda i, j, k: (i, j)),
        out_shape=_sds((M, N), F32), sem=("parallel", "parallel", "arbitrary"),
    )(a, b)


MLA_RC = 128
LOG2E = math.log2(math.e)
MLA_QSCALE = MLA_SCALE * LOG2E


def _mla_fwd(q, k, v, tq):
    S = q.shape[0]
    nq = S // tq
    nt = tq // LANE
    pairs = [(i, j) for i in range(nq) for j in range(i + 1)]
    qi = jnp.asarray(np.array([p[0] for p in pairs], np.int32))
    kj = jnp.asarray(np.array([p[1] for p in pairs], np.int32))

    def body(qi_ref, kj_ref, q_ref, k_ref, v_ref, o_ref, lse_ref, m_s, l_s, acc_s):
        t = pl.program_id(1)
        i, j = qi_ref[t], kj_ref[t]

        @pl.when(j == 0)
        def _():
            m_s[...] = jnp.full_like(m_s, NEG)
            l_s[...] = jnp.zeros_like(l_s)
            acc_s[...] = jnp.zeros_like(acc_s)

        def step(masked):
            kb, vb = k_ref[...], v_ref[...]
            nc = tq // MLA_RC
            scores = [_dot_nt(q_ref[c * MLA_RC:(c + 1) * MLA_RC, :], kb) for c in range(nc)]
            for c in range(nc):
                rows = slice(c * MLA_RC, (c + 1) * MLA_RC)
                s = scores[c]
                if masked:
                    row = lax.broadcasted_iota(jnp.int32, (MLA_RC, tq), 0) + c * MLA_RC
                    col = lax.broadcasted_iota(jnp.int32, (MLA_RC, tq), 1)
                    s = jnp.where(col <= row, s, NEG)
                tiles = [s[:, _hs(u)] for u in range(nt)]
                mx = functools.reduce(jnp.maximum, tiles)
                m_old = m_s[rows, :]
                m_new = jnp.maximum(m_old, jnp.max(mx, axis=1, keepdims=True))
                alpha = jnp.exp2(m_old - m_new)
                ps = [jnp.exp2(u - m_new) for u in tiles]
                l_s[rows, :] = alpha * l_s[rows, :] + functools.reduce(jnp.add, ps)
                p = jnp.concatenate([u.astype(BF16) for u in ps], axis=1)
                acc_s[rows, :] = alpha * acc_s[rows, :] + _dot(p, vb)
                m_s[rows, :] = m_new

        @pl.when(j < i)
        def _():
            step(False)

        @pl.when(j == i)
        def _():
            step(True)
            l = jnp.sum(l_s[...], axis=1, keepdims=True)
            o_ref[...] = (acc_s[...] / l).astype(BF16)
            lse_ref[...] = m_s[...] + jnp.log2(l)

    qmap = lambda h, t, qi_r, kj_r: (qi_r[t], h)
    kmap = lambda h, t, qi_r, kj_r: (kj_r[t], h)
    return _pcall(
        body, name="mla_fwd", grid=(NH, len(pairs)), prefetch=2,
        in_specs=[pl.BlockSpec((tq, LANE), qmap), pl.BlockSpec((tq, LANE), kmap), pl.BlockSpec((tq, LANE), kmap)],
        out_specs=[pl.BlockSpec((tq, LANE), qmap), pl.BlockSpec((tq, LANE), qmap)],
        out_shape=[_sds((S, NH * LANE), BF16), _sds((S, NH * LANE), F32)],
        scratch=[pltpu.VMEM((tq, LANE), F32), pltpu.VMEM((tq, LANE), F32), pltpu.VMEM((tq, LANE), F32)],
        sem=("arbitrary", "arbitrary"),
    )(qi, kj, q, k, v)


def _mla_bwd(q, k, v, do, lse, delta, tq):
    S = q.shape[0]
    nq = S // tq
    pairs = [(i, j) for j in range(nq) for i in range(j, nq)]
    qi = jnp.asarray(np.array([p[0] for p in pairs], np.int32))
    kj = jnp.asarray(np.array([p[1] for p in pairs], np.int32))

    def body(qi_ref, kj_ref, q_ref, k_ref, v_ref, do_ref, lse_ref, dl_ref, dq_ref, dk_ref, dv_ref,
             dq_s, dk_s, dv_s):
        t = pl.program_id(1)
        i, j = qi_ref[t], kj_ref[t]

        @pl.when(t == 0)
        def _():
            dq_s[...] = jnp.zeros_like(dq_s)

        @pl.when(i == j)
        def _():
            dk_s[...] = jnp.zeros_like(dk_s)
            dv_s[...] = jnp.zeros_like(dv_s)

        qrows = pl.ds(pl.multiple_of(i * tq, tq), tq)

        def step(masked):
            qb, dob = q_ref[...], do_ref[...]
            lse_r, dl_r = lse_ref[0], dl_ref[0]
            dq = jnp.zeros((tq, LANE), F32)
            nc = tq // MLA_RC
            sts = [_dot_nt(k_ref[c * MLA_RC:(c + 1) * MLA_RC, :], qb) for c in range(nc)]
            dpts = [_dot_nt(v_ref[c * MLA_RC:(c + 1) * MLA_RC, :], dob) for c in range(nc)]
            for c in range(nc):
                rows = slice(c * MLA_RC, (c + 1) * MLA_RC)
                pt = jnp.exp2(sts[c] - lse_r)
                if masked:
                    key = lax.broadcasted_iota(jnp.int32, (MLA_RC, tq), 0) + c * MLA_RC
                    qry = lax.broadcasted_iota(jnp.int32, (MLA_RC, tq), 1)
                    pt = jnp.where(key <= qry, pt, 0.0)
                dv_s[rows, :] += _dot(pt.astype(BF16), dob)
                gt = (pt * (dpts[c] - dl_r)).astype(BF16)
                dk_s[rows, :] += _dot(gt, qb)
                dq = dq + _dot_tn(gt, k_ref[rows, :])
            dq_s[qrows, :] += dq

        @pl.when(i > j)
        def _():
            step(False)

        @pl.when(i == j)
        def _():
            step(True)
            dq_ref[...] = dq_s[qrows, :] * MLA_SCALE

        @pl.when(i == nq - 1)
        def _():
            dk_ref[...] = dk_s[...] * (1.0 / LOG2E)
            dv_ref[...] = dv_s[...]

    qmap = lambda h, t, qi_r, kj_r: (qi_r[t], h)
    kmap = lambda h, t, qi_r, kj_r: (kj_r[t], h)
    rmap = lambda h, t, qi_r, kj_r: (h, 0, qi_r[t])
    return _pcall(
        body, name="mla_bwd", grid=(NH, len(pairs)), prefetch=2,
        in_specs=[pl.BlockSpec((tq, LANE), qmap), pl.BlockSpec((tq, LANE), kmap), pl.BlockSpec((tq, LANE), kmap),
                  pl.BlockSpec((tq, LANE), qmap), pl.BlockSpec((1, 1, tq), rmap), pl.BlockSpec((1, 1, tq), rmap)],
        out_specs=[pl.BlockSpec((tq, LANE), kmap), pl.BlockSpec((tq, LANE), kmap), pl.BlockSpec((tq, LANE), kmap)],
        out_shape=[_sds((S, NH * LANE), F32), _sds((S, NH * LANE), F32), _sds((S, NH * LANE), F32)],
        scratch=[pltpu.VMEM((S, LANE), F32), pltpu.VMEM((tq, LANE), F32), pltpu.VMEM((tq, LANE), F32)],
        sem=("arbitrary", "arbitrary"),
    )(qi, kj, q, k, v, do, lse, delta)


def _rows_of(t):
    return t[:, ::LANE].T.reshape(NH, 1, -1)


SWA_SUB = 4
SWA_T = SWA_SUB * WIN


def _swa_specs(nsteps, rev):
    step = (lambda i: nsteps - 1 - i) if rev else (lambda i: i)
    cur = lambda w: pl.BlockSpec((SWA_T, w), lambda i: (step(i), 0))
    prev = pl.BlockSpec((WIN, LANE), lambda i: (jnp.maximum(step(i) * SWA_SUB - 1, 0), 0))
    return step, cur, prev


def _swa_probs(qk, bias_h, sink, first_mask):
    s = qk * SWA_SCALE + bias_h
    if first_mask is not None:
        s = jnp.where(first_mask, NEG, s)
    m = jnp.maximum(jnp.max(s, axis=1, keepdims=True), sink)
    e = jnp.exp(s - m)
    es = jnp.exp(sink - m)
    inv = 1.0 / (jnp.sum(e, axis=1, keepdims=True) + es)
    return e * inv, es * inv


def _swa_fwd(qs, ks, vs, bias, sinks):
    S = qs.shape[0]
    nsteps = S // SWA_T
    step, cur, prev = _swa_specs(nsteps, False)

    def body(qs_ref, kc_ref, kp_ref, vc_ref, vp_ref, bias_ref, sk_ref, o_ref):
        first = pl.program_id(0) == 0
        kk = jnp.concatenate([kp_ref[...], kc_ref[...]], axis=0)
        vv = jnp.concatenate([vp_ref[...], vc_ref[...]], axis=0)
        col = lax.broadcasted_iota(jnp.int32, (WIN, 2 * WIN), 1)
        for b in range(SWA_SUB):
            kkb = kk[b * WIN:(b + 2) * WIN]
            vvb = vv[b * WIN:(b + 2) * WIN]
            fm = (first & (col < WIN)) if b == 0 else None
            rows = slice(b * WIN, (b + 1) * WIN)
            qks = [_dot_nt(qs_ref[rows, _hs(h)], kkb) for h in range(NH)]
            for h in range(NH):
                p, _ = _swa_probs(qks[h], bias_ref[h], sk_ref[h:h + 1, 0:1], fm)
                o_ref[rows, _hs(h)] = _dot(p.astype(BF16), vvb).astype(BF16)

    return _pcall(
        body, name="swa_fwd", grid=(nsteps,),
        in_specs=[cur(NH * LANE), cur(LANE), prev, cur(LANE), prev, _full(NH, WIN, 2 * WIN), _full(NH, LANE)],
        out_specs=cur(NH * LANE), out_shape=_sds((S, NH * LANE), BF16), sem=("arbitrary",),
    )(qs, ks, ks, vs, vs, bias, sinks)


def _swa_bwd(qs, ks, vs, do, bias, sinks):
    S = qs.shape[0]
    nsteps = S // SWA_T
    step, cur, prev = _swa_specs(nsteps, True)

    def body(qs_ref, kc_ref, kp_ref, vc_ref, vp_ref, do_ref, bias_ref, sk_ref,
             dqs_ref, dks_ref, dvs_ref, dbias_ref, dsk_ref, dkk_s, dvv_s, ck_s, cv_s):
        pid = pl.program_id(0)
        first = step(pid) == 0

        @pl.when(pid == 0)
        def _():
            dbias_ref[...] = jnp.zeros_like(dbias_ref)
            dsk_ref[...] = jnp.zeros_like(dsk_ref)
            ck_s[...] = jnp.zeros_like(ck_s)
            cv_s[...] = jnp.zeros_like(cv_s)

        dkk_s[...] = jnp.zeros_like(dkk_s)
        dvv_s[...] = jnp.zeros_like(dvv_s)
        kk = jnp.concatenate([kp_ref[...], kc_ref[...]], axis=0)
        vv = jnp.concatenate([vp_ref[...], vc_ref[...]], axis=0)
        col = lax.broadcasted_iota(jnp.int32, (WIN, 2 * WIN), 1)
        for b in range(SWA_SUB):
            kkb = kk[b * WIN:(b + 2) * WIN]
            vvb = vv[b * WIN:(b + 2) * WIN]
            fm = (first & (col < WIN)) if b == 0 else None
            rows = slice(b * WIN, (b + 1) * WIN)
            keys = slice(b * WIN, (b + 2) * WIN)
            qks = [_dot_nt(qs_ref[rows, _hs(h)], kkb) for h in range(NH)]
            dps = [_dot_nt(do_ref[rows, _hs(h)], vvb) for h in range(NH)]
            for h in range(NH):
                qh = qs_ref[rows, _hs(h)]
                doh = do_ref[rows, _hs(h)]
                p, ps = _swa_probs(qks[h], bias_ref[h], sk_ref[h:h + 1, 0:1], fm)
                dp = dps[h]
                dl = jnp.sum(p * dp, axis=1, keepdims=True)
                ds = p * (dp - dl)
                dsk_ref[h:h + 1, :] += -jnp.sum(ps * dl)
                dbias_ref[h] += ds
                dsb = (ds * SWA_SCALE).astype(BF16)
                dqs_ref[rows, _hs(h)] = _dot(dsb, kkb).astype(BF16)
                dkk_s[keys, :] += _dot_tn(dsb, qh)
                dvv_s[keys, :] += _dot_tn(p.astype(BF16), doh)
        dks_ref[...] = dkk_s[WIN:, :]
        dvs_ref[...] = dvv_s[WIN:, :]
        dks_ref[SWA_T - WIN:, :] += ck_s[...]
        dvs_ref[SWA_T - WIN:, :] += cv_s[...]
        ck_s[...] = dkk_s[0:WIN, :]
        cv_s[...] = dvv_s[0:WIN, :]

    return _pcall(
        body, name="swa_bwd", grid=(nsteps,),
        in_specs=[cur(NH * LANE), cur(LANE), prev, cur(LANE), prev, cur(NH * LANE), _full(NH, WIN, 2 * WIN),
                  _full(NH, LANE)],
        out_specs=[cur(NH * LANE), cur(LANE), cur(LANE), _full(NH, WIN, 2 * WIN), _full(NH, LANE)],
        out_shape=[_sds((S, NH * LANE), BF16), _sds((S, LANE), F32), _sds((S, LANE), F32),
                   _sds((NH, WIN, 2 * WIN), F32), _sds((NH, LANE), F32)],
        scratch=[pltpu.VMEM((SWA_T + WIN, LANE), F32), pltpu.VMEM((SWA_T + WIN, LANE), F32),
                 pltpu.VMEM((WIN, LANE), F32), pltpu.VMEM((WIN, LANE), F32)],
        sem=("arbitrary",),
    )(qs, ks, ks, vs, vs, do, bias, sinks)


def _bias_build(rel_bias, bmap):
    def body(rb_ref, bmap_ref, o_ref):
        bm = bmap_ref[...]
        for h in range(NH):
            acc = jnp.full((WIN, 2 * WIN), NEG, F32)
            for b in range(REL_BUCKETS):
                acc = jnp.where(bm == b, rb_ref[b, h], acc)
            o_ref[h] = acc

    return _pcall(
        body, name="bias_build", grid=(1,),
        in_specs=[pl.BlockSpec(memory_space=pltpu.SMEM), _full(WIN, 2 * WIN)],
        out_specs=_full(NH, WIN, 2 * WIN), out_shape=_sds((NH, WIN, 2 * WIN), F32), sem=("arbitrary",),
    )(rel_bias, bmap)


def _bias_reduce(dbias, bmap):
    def body(db_ref, bmap_ref, o_ref):
        bm = bmap_ref[...]
        for h in range(NH):
            dbh = db_ref[h]
            for b in range(REL_BUCKETS):
                o_ref[b, h] = jnp.sum(jnp.where(bm == b, dbh, 0.0))

    return _pcall(
        body, name="bias_reduce", grid=(1,),
        in_specs=[_full(NH, WIN, 2 * WIN), _full(WIN, 2 * WIN)],
        out_specs=pl.BlockSpec(memory_space=pltpu.SMEM), out_shape=_sds((REL_BUCKETS, NH), F32), sem=("arbitrary",),
    )(dbias, bmap)


def _memkv_fwd(mem, mnorm, wkv):
    def body(mem_ref, g_ref, w_ref, o_ref):
        n, _ = _rms(mem_ref[...])
        o_ref[...] = _dot((n * g_ref[...]).astype(BF16), w_ref[...]).astype(BF16)

    return _pcall(
        body, name="memkv_fwd", grid=(1,), in_specs=[_full(MEM_LEN, D), _full(1, D), _full(D, D)],
        out_specs=_full(MEM_LEN, D), out_shape=_sds((MEM_LEN, D), BF16), sem=("arbitrary",),
    )(mem, mnorm, wkv)


def _mem_probs(qk):
    s = qk * MEM_SCALE
    e = jnp.exp(s - jnp.max(s, axis=1, keepdims=True))
    return e / jnp.sum(e, axis=1, keepdims=True)


def _mem_fwd(qm, kvm, ts):
    S = qm.shape[0]

    def body(q_ref, kv_ref, o_ref):
        qks = [_dot_nt(q_ref[:, _hs(h)], kv_ref[:, _hs(h)]) for h in range(MEM_H)]
        for h in range(MEM_H):
            p = _mem_probs(qks[h])
            o_ref[:, _hs(h)] = _dot(p.astype(BF16), kv_ref[:, _hs(MEM_H + h)]).astype(BF16)

    return _pcall(
        body, name="mem_fwd", grid=(S // ts,), in_specs=[_tok(ts, 512), _full(MEM_LEN, D)],
        out_specs=_tok(ts, 512), out_shape=_sds((S, 512), BF16), sem=("arbitrary",),
    )(qm, kvm)


def _mem_bwd(qm, kvm, do, ts):
    S = qm.shape[0]

    def body(q_ref, kv_ref, do_ref, dq_ref, dkv_ref):
        @pl.when(pl.program_id(0) == 0)
        def _():
            dkv_ref[...] = jnp.zeros_like(dkv_ref)

        qks = [_dot_nt(q_ref[:, _hs(h)], kv_ref[:, _hs(h)]) for h in range(MEM_H)]
        dps = [_dot_nt(do_ref[:, _hs(h)], kv_ref[:, _hs(MEM_H + h)]) for h in range(MEM_H)]
        for h in range(MEM_H):
            qh, kh, doh = q_ref[:, _hs(h)], kv_ref[:, _hs(h)], do_ref[:, _hs(h)]
            p = _mem_probs(qks[h])
            dp = dps[h]
            ds = (p * (dp - jnp.sum(p * dp, axis=1, keepdims=True)) * MEM_SCALE).astype(BF16)
            dq_ref[:, _hs(h)] = _dot(ds, kh).astype(BF16)
            dkv_ref[:, _hs(h)] += _dot_tn(ds, qh)
            dkv_ref[:, _hs(MEM_H + h)] += _dot_tn(p.astype(BF16), doh)

    return _pcall(
        body, name="mem_bwd", grid=(S // ts,), in_specs=[_tok(ts, 512), _full(MEM_LEN, D), _tok(ts, 512)],
        out_specs=[_tok(ts, 512), _full(MEM_LEN, D)],
        out_shape=[_sds((S, 512), BF16), _sds((MEM_LEN, D), F32)], sem=("arbitrary",),
    )(qm, kvm, do)


def _memkv_bwd(mem, mnorm, wkv, dkvm):
    def body(mem_ref, g_ref, w_ref, dkv_ref, dw_ref, dg_ref):
        n, _ = _rms(mem_ref[...])
        dkvb = dkv_ref[...].astype(BF16)
        dw_ref[...] = _dot_tn((n * g_ref[...]).astype(BF16), dkvb)
        dg_ref[...] = _colsum(_dot_nt(dkvb, w_ref[...]) * n)

    return _pcall(
        body, name="memkv_bwd", grid=(1,), in_specs=[_full(MEM_LEN, D), _full(1, D), _full(D, D), _full(MEM_LEN, D)],
        out_specs=[_full(D, D), _full(1, D)], out_shape=[_sds((D, D), F32), _sds((1, D), F32)], sem=("arbitrary",),
    )(mem, mnorm, wkv, dkvm)


def _adamw(w, g, m, v, name):
    rows, cols = w.shape
    tr = min(rows, FLAT_TILE)
    assert rows % tr == 0

    def body(w_ref, g_ref, m_ref, v_ref, d_ref, nm_ref, nv_ref):
        gv = g_ref[...]
        nm = ADAM_B1 * m_ref[...] + (1.0 - ADAM_B1) * gv
        nv = ADAM_B2 * v_ref[...] + (1.0 - ADAM_B2) * jnp.square(gv)
        m_hat = nm / (1.0 - ADAM_B1 ** ADAM_STEP)
        v_hat = nv / (1.0 - ADAM_B2 ** ADAM_STEP)
        d_ref[...] = -ADAM_LR * (m_hat / (jnp.sqrt(v_hat) + ADAM_EPS) + ADAM_WD * w_ref[...])
        nm_ref[...] = nm
        nv_ref[...] = nv

    spec = _tok(tr, cols)
    return _pcall(
        body, name=name, grid=(rows // tr,), in_specs=[spec] * 4, out_specs=[spec] * 3,
        out_shape=[_sds((rows, cols), F32)] * 3, sem=("arbitrary",),
    )(w, g, m, v)


def _my_place():
    return lax.axis_index("x"), lax.axis_index("y"), lax.axis_index("c")


def _gather_shards(wl):
    R, W = wl.shape

    def body(w_ref, out_ref, send_sems, recv_sems, local_sem):
        x, y, c = _my_place()
        chips = [(1 - x, y), (x, 1 - y), (1 - x, 1 - y)]
        mine = pltpu.make_async_copy(w_ref, out_ref.at[2 * x + y], local_sem)
        mine.start()

        def copy(k, slot, to):
            return pltpu.make_async_remote_copy(src_ref=w_ref, dst_ref=out_ref.at[slot], send_sem=send_sems.at[k],
                                                recv_sem=recv_sems.at[k], device_id=to, device_id_type=MESH_ID)

        sends = [copy(k, 2 * x + y, (px, py, c)) for k, (px, py) in enumerate(chips)]
        for cp in sends:
            cp.start()
        for k, (px, py) in enumerate(chips):
            copy(k, 2 * px + py, (px, py, c)).wait_recv()
        for cp in sends:
            cp.wait_send()
        mine.wait()

    return pl.pallas_call(
        body, name="gather_shards", out_shape=_sds((4, R, W), wl.dtype),
        in_specs=[pl.BlockSpec(memory_space=pl.ANY)], out_specs=pl.BlockSpec(memory_space=pl.ANY),
        scratch_shapes=[pltpu.SemaphoreType.DMA((3,)), pltpu.SemaphoreType.DMA((3,)), pltpu.SemaphoreType.DMA],
    )(wl)


def _scatter_grads(gbig, gsmall):
    _, _, Rh, W = gbig.shape

    def body(gb_ref, gs_ref, rb_ref, rs_ref, send_sems, recv_sems, local_sems):
        x, y, c = _my_place()
        me = 4 * x + 2 * y + c
        lb = pltpu.make_async_copy(gb_ref.at[2 * x + y, c], rb_ref.at[me], local_sems.at[0])
        ls = pltpu.make_async_copy(gs_ref, rs_ref.at[me], local_sems.at[1])
        lb.start()
        ls.start()

        def peer(m):
            return (x ^ ((m >> 2) & 1), y ^ ((m >> 1) & 1), c ^ (m & 1))

        def big(m, src_slot, dst_slot, to):
            return pltpu.make_async_remote_copy(src_ref=gb_ref.at[src_slot[0], src_slot[1]], dst_ref=rb_ref.at[dst_slot],
                                                send_sem=send_sems.at[2 * m], recv_sem=recv_sems.at[2 * m],
                                                device_id=to, device_id_type=MESH_ID)

        def small(m, dst_slot, to):
            return pltpu.make_async_remote_copy(src_ref=gs_ref, dst_ref=rs_ref.at[dst_slot],
                                                send_sem=send_sems.at[2 * m + 1], recv_sem=recv_sems.at[2 * m + 1],
                                                device_id=to, device_id_type=MESH_ID)

        sends = []
        for m in range(1, 8):
            px, py, pc = peer(m)
            sends.append(big(m, (2 * px + py, pc), me, (px, py, pc)))
            sends.append(small(m, me, (px, py, pc)))
        for cp in sends:
            cp.start()
        for m in range(1, 8):
            px, py, pc = peer(m)
            them = 4 * px + 2 * py + pc
            big(m, (2 * x + y, c), them, (px, py, pc)).wait_recv()
            small(m, them, (px, py, pc)).wait_recv()
        for cp in sends:
            cp.wait_send()
        lb.wait()
        ls.wait()

    return pl.pallas_call(
        body, name="scatter_grads", out_shape=[_sds((8, Rh, W), gbig.dtype), _sds((8, SMALL_ROWS, W), F32)],
        in_specs=[pl.BlockSpec(memory_space=pl.ANY)] * 2, out_specs=[pl.BlockSpec(memory_space=pl.ANY)] * 2,
        scratch_shapes=[pltpu.SemaphoreType.DMA((16,)), pltpu.SemaphoreType.DMA((16,)), pltpu.SemaphoreType.DMA((2,))],
    )(gbig, gsmall)


def _sum_slots(rb, rs):
    _, Rh, W = rb.shape
    tr = FLAT_TILE

    def body(rb_ref, rs_ref, ob_ref, os_ref):
        acc = rb_ref[0].astype(F32)
        for d in range(1, 8):
            acc = acc + rb_ref[d].astype(F32)
        ob_ref[...] = acc

        @pl.when(pl.program_id(0) == 0)
        def _():
            small = rs_ref[0]
            for d in range(1, 8):
                small = small + rs_ref[d]
            os_ref[...] = small

    return _pcall(
        body, name="sum_slots", grid=(Rh // tr,),
        in_specs=[pl.BlockSpec((8, tr, W), lambda i: (0, i, 0)), _full(8, SMALL_ROWS, W)],
        out_specs=[_tok(tr, W), _full(SMALL_ROWS, W)],
        out_shape=[_sds((Rh, W), F32), _sds((SMALL_ROWS, W), F32)], sem=("arbitrary",),
    )(rb, rs)


def _swap_halves(red):
    Rh, W = red.shape
    rows = Rh // SWAP_STREAMS

    def body(r_ref, out_ref, send_sems, recv_sems, local_sem):
        x, y, c = _my_place()
        mine = pltpu.make_async_copy(r_ref, out_ref.at[c], local_sem)
        mine.start()

        def copy(k, slot):
            part = pl.ds(k * rows, rows)
            return pltpu.make_async_remote_copy(src_ref=r_ref.at[part], dst_ref=out_ref.at[slot, part],
                                                send_sem=send_sems.at[k], recv_sem=recv_sems.at[k],
                                                device_id=(x, y, 1 - c), device_id_type=MESH_ID)

        sends = [copy(k, c) for k in range(SWAP_STREAMS)]
        for cp in sends:
            cp.start()
        for k in range(SWAP_STREAMS):
            copy(k, 1 - c).wait_recv()
        for cp in sends:
            cp.wait_send()
        mine.wait()

    return pl.pallas_call(
        body, name="swap_halves", out_shape=_sds((2, Rh, W), F32),
        in_specs=[pl.BlockSpec(memory_space=pl.ANY)], out_specs=pl.BlockSpec(memory_space=pl.ANY),
        scratch_shapes=[pltpu.SemaphoreType.DMA((SWAP_STREAMS,)), pltpu.SemaphoreType.DMA((SWAP_STREAMS,)),
                        pltpu.SemaphoreType.DMA],
    )(red)


def _pad_heads_cols(w, per, width=LANE):
    k = w.shape[0]
    t = w.reshape(k, -1, per)
    return jnp.pad(t, ((0, 0), (0, 0), (0, width - per))).reshape(k, -1)


def _unpad_heads_cols(w, per, width=LANE):
    k = w.shape[0]
    return w.reshape(k, -1, width)[:, :, :per].reshape(k, -1)


def _swa_place(t):
    z = jnp.zeros_like(t)
    lo = jnp.concatenate([t, z], axis=1)
    hi = jnp.concatenate([z, t], axis=1)
    group = (jnp.arange(NH) // SWA_R).reshape((NH,) + (1,) * (t.ndim - 1))
    full = jnp.where(group == 0, lo, hi)
    return full.reshape((NH * LANE,) + t.shape[2:])


def _swa_unplace(t):
    t = t.reshape((NH, 2, 64) + t.shape[1:])
    return jnp.concatenate([t[:SWA_R, 0], t[SWA_R:, 1]], axis=0)


def _pad_w_in(w):
    z = lambda n: jnp.zeros((D, n), w.dtype)
    cq, ckv, kpe = w[:, 0:256], w[:, 256:384], w[:, 384:416]
    qs, ks, vs, qm, gt = w[:, 416:928], w[:, 928:1056], w[:, 1056:1184], w[:, 1184:1696], w[:, 1696:4768]
    qs_p = _swa_place(qs.T.reshape(NH, 64, D)).T
    return jnp.concatenate([cq, ckv, z(64), kpe, z(32), qs_p, ks, vs, qm, gt], axis=1)


def _unpad_w_in(g):
    qs = _swa_unplace(g[:, C_QS:C_KS].T).reshape(NH * 64, D).T
    return jnp.concatenate([g[:, C_CQ:C_KPE], g[:, C_KPE + 64:C_KPE + 96], qs, g[:, C_KS:C_END]], axis=1)


def _pad_w_ukv(w):
    t = w.reshape(128, NH, 2, 64)
    pad = lambda a: jnp.pad(a, ((0, 0), (0, 0), (0, 64))).reshape(128, NH * LANE)
    return jnp.concatenate([pad(t[:, :, 0]), pad(t[:, :, 1])], axis=1)


def _unpad_w_ukv(g):
    k = g[:, :NH * LANE].reshape(128, NH, LANE)[:, :, :64]
    v = g[:, NH * LANE:].reshape(128, NH, LANE)[:, :, :64]
    return jnp.stack([k, v], axis=2).reshape(128, NH * LANE)


def _pad_w_o_mla(w):
    return jnp.pad(w.reshape(NH, 64, D), ((0, 0), (0, 64), (0, 0))).reshape(NH * LANE, D)


def _unpad_w_o_mla(g):
    return g.reshape(NH, LANE, D)[:, :64].reshape(NH * 64, D)


def _rope_tables(S):
    pos = jnp.arange(S, dtype=F32)
    inv = 1.0 / (ROPE_THETA ** (jnp.arange(0, 32, 2, dtype=F32) / 32))
    ang = pos[:, None] * inv[None, :]
    cos, sin = jnp.cos(ang), jnp.sin(ang)
    one, zero = jnp.ones((S, 64), F32), jnp.zeros((S, 16), F32)
    rc = jnp.concatenate([one, cos, cos, jnp.ones((S, 32), F32)], axis=1)
    rs1 = jnp.concatenate([jnp.zeros((S, 64), F32), zero, sin, jnp.zeros((S, 32), F32)], axis=1)
    rs2 = jnp.concatenate([jnp.zeros((S, 64), F32), -sin, zero, jnp.zeros((S, 32), F32)], axis=1)
    return rc, rs1, rs2


def _bucket_map():
    qi = jnp.arange(WIN)[:, None]
    kj = jnp.arange(2 * WIN)[None, :]
    dist = qi + WIN - kj
    n = jnp.maximum(dist, 0)
    max_exact = REL_BUCKETS // 2
    nf = jnp.maximum(n, 1).astype(F32)
    large = max_exact + (jnp.log(nf / max_exact) / math.log(128 / max_exact)
                         * (REL_BUCKETS - max_exact)).astype(jnp.int32)
    large = jnp.minimum(large, REL_BUCKETS - 1)
    bucket = jnp.where(n < max_exact, n, large)
    return jnp.where((dist >= 0) & (dist < WIN), bucket, -1).astype(jnp.int32)


TS = 256
TQ = 1024
TQ_FWD = 1024


def _layer_weights(wf, l):
    return dict(
        win=_pad_w_in(wf["w_in"][l]),
        wuq=_pad_heads_cols(wf["w_uq"][l], 96),
        wukv=_pad_w_ukv(wf["w_ukv"][l]),
        wmem=wf["w_mem_kv"][l],
        wa=_pad_w_o_mla(wf["w_o_mla"][l]),
        wb=_swa_place(wf["w_o_swa"][l].reshape(NH, 64, D)),
        wc=wf["w_o_mem"][l],
        wout=wf["w_out"][l],
        wup=wf["w_up"][l],
        wdown=wf["w_down"][l],
    )


def _local_step(x, mem, tgt, wf, sp):
    S = x.shape[0]
    ts = min(TS, S)
    tq = min(TQ, S)
    rc, rs1, rs2 = _rope_tables(S)
    bmap = _bucket_map()
    bias = _bias_build(sp["rel_bias"], bmap)
    row = lambda v: v.reshape(1, -1)

    saved = []
    for l in range(DEPTH):
        w = _layer_weights(wf, l)
        an, qn, kvn = row(sp["attn_norm"][l]), row(sp["mla_q_norm"][l]), row(sp["mla_kv_norm"][l])
        bg, mnorm, mlpn = row(sp["b_gate"][l]), row(sp["mem_norm"][l]), row(sp["mlp_norm"][l])
        sinks = jnp.broadcast_to(sp["attn_sinks"][l][:, None], (NH, LANE))
        q, k, v, qs, ks, vs, qm, g = _pre_fwd(x, an, w["win"], bg, qn, kvn, w["wuq"], w["wukv"], rc, rs1, rs2, ts)
        oa, lse = _mla_fwd(q, k, v, min(TQ_FWD, S))
        ob = _swa_fwd(qs, ks, vs, bias, sinks)
        kvm = _memkv_fwd(mem, mnorm, w["wmem"])
        oc = _mem_fwd(qm, kvm, ts)
        x1, yb = _merge_fwd(x, g, oa, ob, oc, w["wa"], w["wb"], w["wc"], w["wout"], ts)
        x2 = _mlp_fwd(x1, mlpn, w["wup"], w["wdown"], ts)
        saved.append(dict(w=w, x=x, x1=x1, q=q, k=k, v=v, qs=qs, ks=ks, vs=vs, qm=qm, g=g, oa=oa, lse=lse, ob=ob,
                          oc=oc, kvm=kvm, yb=yb, an=an, qn=qn, kvn=kvn, mnorm=mnorm, mlpn=mlpn, sinks=sinks))
        x = x2

    sq, dx, dfn = _loss_kernel(x, row(sp["final_norm"]), tgt, ts)

    grads = {n: [None] * DEPTH for n in WEIGHTS if n not in ("rel_bias", "final_norm")}
    dbias_total = None
    for l in reversed(range(DEPTH)):
        sv = saved[l]
        w = sv["w"]
        dx1, hb2, dub, ab, dxb, dmlpn = _mlp_bwd(dx, sv["x1"], sv["mlpn"], w["wup"], w["wdown"], ts)
        grads["w_up"][l] = _matmul_tn(hb2, dub, "dw_up")
        grads["w_down"][l] = _matmul_tn(ab, dxb, "dw_down")
        grads["mlp_norm"][l] = dmlpn[0]

        dgp, dyo, doa, dob, doc, dla, dx1b, dbg = _merge_bwd(dx1, sv["g"], sv["oa"], sv["ob"], sv["oc"], w["wa"],
                                                             w["wb"], w["wc"], w["wout"], ts)
        grads["w_out"][l] = _matmul_tn(sv["yb"], dx1b, "dw_out")
        grads["w_o_mla"][l] = _unpad_w_o_mla(_matmul_tn(sv["oa"], dyo[:, 0:D], "dw_o_mla"))
        grads["w_o_swa"][l] = _swa_unplace(_matmul_tn(sv["ob"], dyo[:, D:2 * D], "dw_o_swa")).reshape(NH * 64, D)
        grads["w_o_mem"][l] = _matmul_tn(sv["oc"], dyo[:, 2 * D:3 * D], "dw_o_mem")
        grads["b_gate"][l] = dbg[0]

        dqm, dkvm = _mem_bwd(sv["qm"], sv["kvm"], doc, ts)
        dwmem, dmnorm = _memkv_bwd(mem, sv["mnorm"], w["wmem"], dkvm)
        grads["w_mem_kv"][l] = dwmem
        grads["mem_norm"][l] = dmnorm[0]

        dqs, dks, dvs, dbias, dsink = _swa_bwd(sv["qs"], sv["ks"], sv["vs"], dob, bias, sv["sinks"])
        dbias_total = dbias if dbias_total is None else dbias_total + dbias
        grads["attn_sinks"][l] = dsink[:, 0]

        dq, dk, dv = _mla_bwd(sv["q"], sv["k"], sv["v"], doa, _rows_of(sv["lse"]), _rows_of(dla), tq)

        dx, dproj, hb, cqn, ckvn, dqpre, dkv, dan, dqn, dkvn = _pre_bwd(
            sv["x"], dx1, dq, dk, dv, dqs, dks, dvs, dqm, dgp, sv["an"], sv["qn"], sv["kvn"], w["win"], w["wuq"],
            w["wukv"], rc, rs1, rs2, ts)
        grads["w_in"][l] = _unpad_w_in(_matmul_tn(hb, dproj, "dw_in"))
        grads["w_uq"][l] = _unpad_heads_cols(_matmul_tn(cqn, dqpre, "dw_uq"), 96)
        grads["w_ukv"][l] = _unpad_w_ukv(_matmul_tn(ckvn, dkv, "dw_ukv"))
        grads["attn_norm"][l] = dan[0]
        grads["mla_q_norm"][l] = dqn[0]
        grads["mla_kv_norm"][l] = dkvn[0]

    out = {n: jnp.stack(v) for n, v in grads.items()}
    out["rel_bias"] = _bias_reduce(dbias_total, bmap)
    out["final_norm"] = dfn[0]
    return sq, dx, out


def _flat_rows(n_elems, multiple):
    rows = -(-n_elems // FLAT_W)
    return -(-rows // multiple) * multiple


def _flatten(parts, rows):
    flat = jnp.concatenate([p.reshape(-1) for p in parts])
    return jnp.pad(flat, (0, rows * FLAT_W - flat.shape[0])).reshape(rows, FLAT_W)


def _unflatten(buf, shapes):
    flat = buf.reshape(-1)
    out, at = [], 0
    for s in shapes:
        n = int(np.prod(s))
        out.append(flat[at:at + n].reshape(s))
        at += n
    return out


def _shard_major(g, axis):
    L, r, c = g.shape
    if axis == 2:
        t = g.reshape(L, r, 4, c // 4).transpose(2, 0, 1, 3)
    else:
        t = g.reshape(L, 4, r // 4, c).transpose(1, 0, 2, 3)
    return t.reshape(4, -1)


def _from_shard_major(t, shard_shape, axis):
    L, r, c = shard_shape
    t = t.reshape(4, L, r, c)
    if axis == 2:
        return t.transpose(1, 2, 0, 3).reshape(L, r, 4 * c)
    return t.transpose(1, 0, 2, 3).reshape(L, 4 * r, c)


def kernel(x, mem, rel_bias, attn_norm, mem_norm, w_in, b_gate, mla_q_norm, w_uq, mla_kv_norm, w_ukv, attn_sinks, w_mem_kv, w_o_mla, w_o_swa, w_o_mem, w_out, mlp_norm, w_up, w_down, final_norm, loss_target, m_rel_bias, m_attn_norm, m_mem_norm, m_w_in, m_b_gate, m_mla_q_norm, m_w_uq, m_mla_kv_norm, m_w_ukv, m_attn_sinks, m_w_mem_kv, m_w_o_mla, m_w_o_swa, m_w_o_mem, m_w_out, m_mlp_norm, m_w_up, m_w_down, m_final_norm, v_rel_bias, v_attn_norm, v_mem_norm, v_w_in, v_b_gate, v_mla_q_norm, v_w_uq, v_mla_kv_norm, v_w_ukv, v_attn_sinks, v_w_mem_kv, v_w_o_mla, v_w_o_swa, v_w_o_mem, v_w_out, v_mlp_norm, v_w_up, v_w_down, v_final_norm):
    args = dict(locals())
    W = {n: args[n] for n in WEIGHTS}
    M = {n: args["m_" + n] for n in WEIGHTS}
    V = {n: args["v_" + n] for n in WEIGHTS}
    shard_shapes = [W[n].shape for n in SHARDED]
    small_shapes = [W[n].shape for n in SMALL]
    counts = [int(np.prod(s)) for s in shard_shapes]
    starts = [sum(counts[:i]) for i in range(len(counts))]
    R = _flat_rows(sum(counts), 2 * FLAT_TILE)

    wl = _flatten([W[n].astype(BF16) for n in SHARDED], R)
    gathered = _gather_shards(wl).reshape(4, R * FLAT_W)
    wf = {n: _from_shard_major(gathered[:, starts[i]:starts[i] + counts[i]], shard_shapes[i], SHARD_AXIS[n])
          for i, n in enumerate(SHARDED)}
    sp = {n: W[n] for n in SMALL}

    sq, dx, grads = _local_step(x[0], mem[0], loss_target[0], wf, sp)

    gflat = jnp.concatenate([_shard_major(grads[n], SHARD_AXIS[n]) for n in SHARDED], axis=1)
    gbig = jnp.pad(gflat, ((0, 0), (0, R * FLAT_W - gflat.shape[1]))).astype(BF16)
    gsmall = _flatten([grads[n] for n in SMALL], SMALL_ROWS)
    rb, rs = _scatter_grads(gbig.reshape(4, 2, R // 2, FLAT_W), gsmall)
    red_half, red_small = _sum_slots(rb, rs)
    g_shard = _swap_halves(red_half).reshape(R * FLAT_W)

    G, DW, NM, NV = {}, {}, {}, {}
    for i, n in enumerate(SHARDED):
        shape = shard_shapes[i]
        two_d = lambda a: a.reshape(-1, shape[-1])
        G[n] = g_shard[starts[i]:starts[i] + counts[i]].reshape(shape)
        d, nm, nv = _adamw(two_d(W[n]), two_d(G[n]), two_d(M[n]), two_d(V[n]), "adamw_" + n)
        DW[n], NM[n], NV[n] = d.reshape(shape), nm.reshape(shape), nv.reshape(shape)
    d_s, m_s, v_s = _adamw(_flatten([W[n] for n in SMALL], SMALL_ROWS), red_small,
                           _flatten([M[n] for n in SMALL], SMALL_ROWS), _flatten([V[n] for n in SMALL], SMALL_ROWS),
                           "adamw_small")
    for out, buf in ((G, red_small), (DW, d_s), (NM, m_s), (NV, v_s)):
        out.update(zip(SMALL, _unflatten(buf, small_shapes)))
    loss = lax.psum(0.5 * sq[0, 0] / D, ("x", "y", "c"))
    return (loss, dx[None], *[G[n] for n in WEIGHTS], *[DW[n] for n in WEIGHTS], *[NM[n] for n in WEIGHTS],
            *[NV[n] for n in WEIGHTS])
```

```python
import functools
import math

import numpy as np
import jax
import jax.numpy as jnp
from jax import lax
from jax.experimental import pallas as pl
from jax.experimental.pallas import tpu as pltpu

F32 = jnp.float32
BF16 = jnp.bfloat16

D = 1024
DFF = 4096
DEPTH = 2
EPS = 1e-6
LANE = 128
NH = 8
SWA_R = 4
MEM_H = 4
MEM_LEN = 256
WIN = 128
NEG = -1e30
MLA_SCALE = 96 ** -0.5
SWA_SCALE = 64 ** -0.5
MEM_SCALE = 128 ** -0.5
REL_BUCKETS = 32
ROPE_THETA = 10000.0

C_CQ, C_CKV, C_KPE, C_QS, C_KS, C_VS, C_QM, C_G, C_END = 0, 256, 384, 512, 1536, 1664, 1792, 2304, 5376
IN_COLS = 4768

ADAM_LR = 0.001
ADAM_B1 = 0.9
ADAM_B2 = 0.999
ADAM_EPS = 1e-08
ADAM_WD = 0.01
ADAM_STEP = 10

VMEM_LIMIT = 56 * 1024 * 1024

SHARDED = ("w_in", "w_uq", "w_ukv", "w_mem_kv", "w_o_mla", "w_o_swa", "w_o_mem", "w_out", "w_up", "w_down")
SHARD_AXIS = {"w_in": 2, "w_uq": 2, "w_ukv": 2, "w_mem_kv": 1, "w_o_mla": 2, "w_o_swa": 2, "w_o_mem": 2,
              "w_out": 1, "w_up": 2, "w_down": 1}
SMALL = ("rel_bias", "attn_norm", "mem_norm", "b_gate", "mla_q_norm", "mla_kv_norm", "attn_sinks", "mlp_norm",
         "final_norm")
WEIGHTS = ("rel_bias", "attn_norm", "mem_norm", "w_in", "b_gate", "mla_q_norm", "w_uq", "mla_kv_norm", "w_ukv",
           "attn_sinks", "w_mem_kv", "w_o_mla", "w_o_swa", "w_o_mem", "w_out", "mlp_norm", "w_up", "w_down",
           "final_norm")
FLAT_W = 1024
FLAT_TILE = 256
SMALL_ROWS = 16
MESH_ID = pl.DeviceIdType.MESH


def _pcall(body, *, name, grid, in_specs, out_specs, out_shape, scratch=(), prefetch=0, sem=None):
    params = pltpu.CompilerParams(dimension_semantics=sem, vmem_limit_bytes=VMEM_LIMIT)
    if prefetch:
        spec = pltpu.PrefetchScalarGridSpec(num_scalar_prefetch=prefetch, grid=grid, in_specs=in_specs,
                                            out_specs=out_specs, scratch_shapes=scratch)
        return pl.pallas_call(body, name=name, grid_spec=spec, out_shape=out_shape, compiler_params=params)
    return pl.pallas_call(body, name=name, grid=grid, in_specs=in_specs, out_specs=out_specs, out_shape=out_shape,
                          scratch_shapes=scratch, compiler_params=params)


def _tok(ts, w):
    return pl.BlockSpec((ts, w), lambda i: (i, 0))


def _full(*shape):
    return pl.BlockSpec(shape, lambda *_: (0,) * len(shape))


def _sds(shape, dtype):
    return jax.ShapeDtypeStruct(shape, dtype)


def _dot(a, b):
    return jnp.dot(a, b, preferred_element_type=F32)


def _dot_nt(a, b):
    return lax.dot_general(a, b, (((1,), (1,)), ((), ())), preferred_element_type=F32)


def _dot_tn(a, b):
    return lax.dot_general(a, b, (((0,), (0,)), ((), ())), preferred_element_type=F32)


def _rms(x):
    r = lax.rsqrt(jnp.mean(x * x, axis=-1, keepdims=True) + EPS)
    return x * r, r


def _rms_bwd(dyg, n, r):
    return r * (dyg - n * jnp.mean(n * dyg, axis=-1, keepdims=True))


def _rope(t, c, s1, s2):
    return t * c + pltpu.roll(t, 16, 1) * s1 + pltpu.roll(t, LANE - 16, 1) * s2


def _rope_bwd(dy, c, s1, s2):
    return dy * c + pltpu.roll(dy * s1, LANE - 16, 1) + pltpu.roll(dy * s2, 16, 1)


def _hs(h):
    return slice(h * LANE, (h + 1) * LANE)


def _colsum(t):
    return jnp.sum(t, axis=0, keepdims=True)


def _pre_fwd(x, an, win, bg, qn, kvn, wuq, wukv, rc, rs1, rs2, ts):
    S = x.shape[0]

    def body(x_ref, an_ref, win_ref, bg_ref, qn_ref, kvn_ref, wuq_ref, wukv_ref, rc_ref, rs1_ref, rs2_ref,
             q_ref, k_ref, v_ref, qs_ref, ks_ref, vs_ref, qm_ref, g_ref):
        n, _ = _rms(x_ref[...])
        hb = (n * an_ref[...]).astype(BF16)
        pa = _dot(hb, win_ref[:, C_CQ:C_QS])
        ncq, _ = _rms(pa[:, 0:256])
        cqn = (ncq * qn_ref[...]).astype(BF16)
        nkv, _ = _rms(pa[:, 256:384])
        ckvn = (nkv * kvn_ref[...]).astype(BF16)
        c, s1, s2 = rc_ref[...], rs1_ref[...], rs2_ref[...]
        kper = _rope(pa[:, 384:512], c, s1, s2)
        qp = _dot(cqn, wuq_ref[...])
        kv = _dot(ckvn, wukv_ref[...])
        for h in range(NH):
            q_ref[:, _hs(h)] = (_rope(qp[:, _hs(h)], c, s1, s2) * MLA_QSCALE).astype(BF16)
            k_ref[:, _hs(h)] = (kv[:, _hs(h)] + kper).astype(BF16)
        v_ref[...] = kv[:, NH * LANE:].astype(BF16)
        pb = _dot(hb, win_ref[:, C_QS:C_G])
        qs_ref[...] = pb[:, 0:1024].astype(BF16)
        ks_ref[...] = pb[:, 1024:1152].astype(BF16)
        vs_ref[...] = pb[:, 1152:1280].astype(BF16)
        qm_ref[...] = pb[:, 1280:1792].astype(BF16)
        g_ref[...] = jax.nn.sigmoid(_dot(hb, win_ref[:, C_G:C_END]) + bg_ref[...])

    return _pcall(
        body, name="pre_fwd", grid=(S // ts,),
        in_specs=[_tok(ts, D), _full(1, D), _full(D, C_END), _full(1, 3 * D), _full(1, 256), _full(1, 128),
                  _full(256, NH * LANE), _full(128, 2 * NH * LANE), _tok(ts, LANE), _tok(ts, LANE), _tok(ts, LANE)],
        out_specs=[_tok(ts, 1024), _tok(ts, 1024), _tok(ts, 1024), _tok(ts, 1024), _tok(ts, 128), _tok(ts, 128),
                   _tok(ts, 512), _tok(ts, 3 * D)],
        out_shape=[_sds((S, 1024), BF16), _sds((S, 1024), BF16), _sds((S, 1024), BF16), _sds((S, 1024), BF16),
                   _sds((S, 128), BF16), _sds((S, 128), BF16), _sds((S, 512), BF16), _sds((S, 3 * D), F32)],
        sem=("arbitrary",),
    )(x, an, win, bg, qn, kvn, wuq, wukv, rc, rs1, rs2)


def _merge_fwd(x, g, oa, ob, oc, wa, wb, wc, wout, ts):
    S = x.shape[0]

    def body(x_ref, g_ref, oa_ref, ob_ref, oc_ref, wa_ref, wb_ref, wc_ref, wout_ref, x1_ref, yb_ref):
        y = g_ref[:, 0:D] * _dot(oa_ref[...], wa_ref[...])
        y = y + g_ref[:, D:2 * D] * _dot(ob_ref[...], wb_ref[...])
        y = y + g_ref[:, 2 * D:3 * D] * _dot(oc_ref[...], wc_ref[...])
        yb = y.astype(BF16)
        yb_ref[...] = yb
        x1_ref[...] = x_ref[...] + _dot(yb, wout_ref[...])

    return _pcall(
        body, name="merge_fwd", grid=(S // ts,),
        in_specs=[_tok(ts, D), _tok(ts, 3 * D), _tok(ts, 1024), _tok(ts, 1024), _tok(ts, 512),
                  _full(1024, D), _full(1024, D), _full(512, D), _full(D, D)],
        out_specs=[_tok(ts, D), _tok(ts, D)],
        out_shape=[_sds((S, D), F32), _sds((S, D), BF16)],
        sem=("arbitrary",),
    )(x, g, oa, ob, oc, wa, wb, wc, wout)


def _mlp_fwd(x1, mn, wup, wdown, ts):
    S = x1.shape[0]

    def body(x_ref, mn_ref, wup_ref, wdown_ref, x2_ref):
        xv = x_ref[...]
        n, _ = _rms(xv)
        u = _dot((n * mn_ref[...]).astype(BF16), wup_ref[...])
        a = jnp.square(jnp.maximum(u, 0.0))
        x2_ref[...] = xv + _dot(a.astype(BF16), wdown_ref[...])

    return _pcall(
        body, name="mlp_fwd", grid=(S // ts,),
        in_specs=[_tok(ts, D), _full(1, D), _full(D, DFF), _full(DFF, D)],
        out_specs=_tok(ts, D), out_shape=_sds((S, D), F32), sem=("arbitrary",),
    )(x1, mn, wup, wdown)


def _loss_kernel(x, fn, tgt, ts):
    S = x.shape[0]

    def body(x_ref, fn_ref, t_ref, loss_ref, dx_ref, dfn_ref):
        @pl.when(pl.program_id(0) == 0)
        def _():
            loss_ref[...] = jnp.zeros_like(loss_ref)
            dfn_ref[...] = jnp.zeros_like(dfn_ref)

        n, r = _rms(x_ref[...])
        err = n * fn_ref[...] - t_ref[...]
        loss_ref[...] += jnp.sum(err * err)
        dy = err * (1.0 / D)
        dfn_ref[...] += _colsum(dy * n)
        dx_ref[...] = _rms_bwd(dy * fn_ref[...], n, r)

    return _pcall(
        body, name="loss_head", grid=(S // ts,),
        in_specs=[_tok(ts, D), _full(1, D), _tok(ts, D)],
        out_specs=[_full(1, LANE), _tok(ts, D), _full(1, D)],
        out_shape=[_sds((1, LANE), F32), _sds((S, D), F32), _sds((1, D), F32)],
        sem=("arbitrary",),
    )(x, fn, tgt)


def _mlp_bwd(dx2, x1, mn, wup, wdown, ts):
    S = x1.shape[0]

    def body(dx_ref, x_ref, mn_ref, wup_ref, wdown_ref, dx1_ref, hb_ref, dub_ref, ab_ref, dxb_ref, dmn_ref):
        @pl.when(pl.program_id(0) == 0)
        def _():
            dmn_ref[...] = jnp.zeros_like(dmn_ref)

        dx = dx_ref[...]
        n, r = _rms(x_ref[...])
        g = mn_ref[...]
        hb = (n * g).astype(BF16)
        hb_ref[...] = hb
        rl = jnp.maximum(_dot(hb, wup_ref[...]), 0.0)
        ab_ref[...] = jnp.square(rl).astype(BF16)
        dxb = dx.astype(BF16)
        dxb_ref[...] = dxb
        dub = (_dot_nt(dxb, wdown_ref[...]) * (2.0 * rl)).astype(BF16)
        dub_ref[...] = dub
        dh = _dot_nt(dub, wup_ref[...])
        dmn_ref[...] += _colsum(dh * n)
        dx1_ref[...] = dx + _rms_bwd(dh * g, n, r)

    return _pcall(
        body, name="mlp_bwd", grid=(S // ts,),
        in_specs=[_tok(ts, D), _tok(ts, D), _full(1, D), _full(D, DFF), _full(DFF, D)],
        out_specs=[_tok(ts, D), _tok(ts, D), _tok(ts, DFF), _tok(ts, DFF), _tok(ts, D), _full(1, D)],
        out_shape=[_sds((S, D), F32), _sds((S, D), BF16), _sds((S, DFF), BF16), _sds((S, DFF), BF16),
                   _sds((S, D), BF16), _sds((1, D), F32)],
        sem=("arbitrary",),
    )(dx2, x1, mn, wup, wdown)


def _merge_bwd(dx1, g, oa, ob, oc, wa, wb, wc, wout, ts):
    S = dx1.shape[0]

    def body(dx_ref, g_ref, oa_ref, ob_ref, oc_ref, wa_ref, wb_ref, wc_ref, wout_ref,
             dgp_ref, dyo_ref, doa_ref, dob_ref, doc_ref, dla_ref, dxb_ref, dbg_ref):
        @pl.when(pl.program_id(0) == 0)
        def _():
            dbg_ref[...] = jnp.zeros_like(dbg_ref)

        dxb = dx_ref[...].astype(BF16)
        dxb_ref[...] = dxb
        dy = _dot_nt(dxb, wout_ref[...])
        branches = ((oa_ref, wa_ref, doa_ref), (ob_ref, wb_ref, dob_ref), (oc_ref, wc_ref, doc_ref))
        for b, (o_ref, w_ref, do_ref) in enumerate(branches):
            cols = slice(b * D, (b + 1) * D)
            gb = g_ref[:, cols]
            o = o_ref[...]
            dgpre = dy * _dot(o, w_ref[...]) * gb * (1.0 - gb)
            dgp_ref[:, cols] = dgpre.astype(BF16)
            dbg_ref[:, cols] += _colsum(dgpre)
            dyo = (dy * gb).astype(BF16)
            dyo_ref[:, cols] = dyo
            do = _dot_nt(dyo, w_ref[...])
            do_ref[...] = do.astype(BF16)
            if b == 0:
                for h in range(NH):
                    dl = jnp.sum(do[:, _hs(h)] * o[:, _hs(h)].astype(F32), axis=1, keepdims=True)
                    dla_ref[:, _hs(h)] = jnp.broadcast_to(dl, (ts, LANE))

    return _pcall(
        body, name="merge_bwd", grid=(S // ts,),
        in_specs=[_tok(ts, D), _tok(ts, 3 * D), _tok(ts, 1024), _tok(ts, 1024), _tok(ts, 512),
                  _full(1024, D), _full(1024, D), _full(512, D), _full(D, D)],
        out_specs=[_tok(ts, 3 * D), _tok(ts, 3 * D), _tok(ts, 1024), _tok(ts, 1024), _tok(ts, 512), _tok(ts, 1024),
                   _tok(ts, D), _full(1, 3 * D)],
        out_shape=[_sds((S, 3 * D), BF16), _sds((S, 3 * D), BF16), _sds((S, 1024), BF16), _sds((S, 1024), BF16),
                   _sds((S, 512), BF16), _sds((S, 1024), F32), _sds((S, D), BF16), _sds((1, 3 * D), F32)],
        sem=("arbitrary",),
    )(dx1, g, oa, ob, oc, wa, wb, wc, wout)


def _pre_bwd(x, dx1, dq, dk, dv, dqs, dks, dvs, dqm, dgp, an, qn, kvn, win, wuq, wukv, rc, rs1, rs2, ts):
    S = x.shape[0]

    def body(x_ref, dx1_ref, dq_ref, dk_ref, dv_ref, dqs_ref, dks_ref, dvs_ref, dqm_ref, dgp_ref,
             an_ref, qn_ref, kvn_ref, win_ref, wuq_ref, wukv_ref, rc_ref, rs1_ref, rs2_ref,
             dx_ref, dproj_ref, hb_ref, cqn_ref, ckvn_ref, dqpre_ref, dkv_ref, dan_ref, dqn_ref, dkvn_ref):
        @pl.when(pl.program_id(0) == 0)
        def _():
            dan_ref[...] = jnp.zeros_like(dan_ref)
            dqn_ref[...] = jnp.zeros_like(dqn_ref)
            dkvn_ref[...] = jnp.zeros_like(dkvn_ref)

        n, r = _rms(x_ref[...])
        hb = (n * an_ref[...]).astype(BF16)
        hb_ref[...] = hb
        pa = _dot(hb, win_ref[:, C_CQ:C_KPE])
        ncq, rq = _rms(pa[:, 0:256])
        cqn_ref[...] = (ncq * qn_ref[...]).astype(BF16)
        nkv, rkv = _rms(pa[:, 256:384])
        ckvn_ref[...] = (nkv * kvn_ref[...]).astype(BF16)
        c, s1, s2 = rc_ref[...], rs1_ref[...], rs2_ref[...]

        dkper = jnp.zeros((ts, LANE), F32)
        for h in range(NH):
            dqpre_ref[:, _hs(h)] = _rope_bwd(dq_ref[:, _hs(h)], c, s1, s2).astype(BF16)
            dkh = dk_ref[:, _hs(h)]
            dkper = dkper + dkh
            dkv_ref[:, _hs(h)] = dkh.astype(BF16)
        dkv_ref[:, NH * LANE:] = dv_ref[...].astype(BF16)

        dcqn = _dot_nt(dqpre_ref[...], wuq_ref[...])
        dqn_ref[...] += _colsum(dcqn * ncq)
        dproj_ref[:, C_CQ:C_CKV] = _rms_bwd(dcqn * qn_ref[...], ncq, rq).astype(BF16)
        dckvn = _dot_nt(dkv_ref[...], wukv_ref[...])
        dkvn_ref[...] += _colsum(dckvn * nkv)
        dproj_ref[:, C_CKV:C_KPE] = _rms_bwd(dckvn * kvn_ref[...], nkv, rkv).astype(BF16)
        lane = lax.broadcasted_iota(jnp.int32, (ts, LANE), 1)
        dkpe = jnp.where((lane >= 64) & (lane < 96), _rope_bwd(dkper, c, s1, s2), 0.0)
        dproj_ref[:, C_KPE:C_QS] = dkpe.astype(BF16)
        dproj_ref[:, C_QS:C_KS] = dqs_ref[...]
        dproj_ref[:, C_KS:C_VS] = dks_ref[...].astype(BF16)
        dproj_ref[:, C_VS:C_QM] = dvs_ref[...].astype(BF16)
        dproj_ref[:, C_QM:C_G] = dqm_ref[...]
        dproj_ref[:, C_G:C_END] = dgp_ref[...]

        dh = _dot_nt(dproj_ref[...], win_ref[...])
        dan_ref[...] += _colsum(dh * n)
        dx_ref[...] = dx1_ref[...] + _rms_bwd(dh * an_ref[...], n, r)

    return _pcall(
        body, name="pre_bwd", grid=(S // ts,),
        in_specs=[_tok(ts, D), _tok(ts, D), _tok(ts, 1024), _tok(ts, 1024), _tok(ts, 1024), _tok(ts, 1024),
                  _tok(ts, 128), _tok(ts, 128), _tok(ts, 512), _tok(ts, 3 * D),
                  _full(1, D), _full(1, 256), _full(1, 128), _full(D, C_END), _full(256, NH * LANE),
                  _full(128, 2 * NH * LANE), _tok(ts, LANE), _tok(ts, LANE), _tok(ts, LANE)],
        out_specs=[_tok(ts, D), _tok(ts, C_END), _tok(ts, D), _tok(ts, 256), _tok(ts, 128), _tok(ts, 1024),
                   _tok(ts, 2048), _full(1, D), _full(1, 256), _full(1, 128)],
        out_shape=[_sds((S, D), F32), _sds((S, C_END), BF16), _sds((S, D), BF16), _sds((S, 256), BF16),
                   _sds((S, 128), BF16), _sds((S, 1024), BF16), _sds((S, 2048), BF16), _sds((1, D), F32),
                   _sds((1, 256), F32), _sds((1, 128), F32)],
        sem=("arbitrary",),
    )(x, dx1, dq, dk, dv, dqs, dks, dvs, dqm, dgp, an, qn, kvn, win, wuq, wukv, rc, rs1, rs2)


def _pick_tile(n, cap):
    best = LANE
    for t in range(LANE, min(n, cap) + 1, LANE):
        if n % t == 0:
            best = t
    return best


def _matmul_tn(a, b, name, shard_axis=None):
    S, M = a.shape
    N = b.shape[1]
    tm = _pick_tile(M // 4 if shard_axis == 0 else M, 1024)
    tn = _pick_tile(N // 4 if shard_axis == 1 else N, 2048)
    ts = min(S, 1024)
    nk = S // ts

    def body(a_ref, b_ref, o_ref, *acc):
        acc_ref = acc[0] if acc else o_ref

        @pl.when(pl.program_id(2) == 0)
        def _():
            acc_ref[...] = jnp.zeros_like(acc_ref)

        acc_ref[...] += _dot_tn(a_ref[...], b_ref[...])
        if acc:
            @pl.when(pl.program_id(2) == nk - 1)
            def _():
                o_ref[0] = acc_ref[...].astype(o_ref.dtype)

    if shard_axis is None:
        out_spec = pl.BlockSpec((tm, tn), lambda i, j, k: (i, j))
        out_shape, scratch = _sds((M, N), F32), ()
    elif shard_axis == 0:
        per = (M // 4) // tm
        out_spec = pl.BlockSpec((1, tm, tn), lambda i, j, k: (i // per, i % per, j))
        out_shape, scratch = _sds((4, M // 4, N), BF16), (pltpu.VMEM((tm, tn), F32),)
    else:
        per = (N // 4) // tn
        out_spec = pl.BlockSpec((1, tm, tn), lambda i, j, k: (j // per, i, j % per))
        out_shape, scratch = _sds((4, M, N // 4), BF16), (pltpu.VMEM((tm, tn), F32),)
    return _pcall(
        body, name=name, grid=(M // tm, N // tn, nk),
        in_specs=[pl.BlockSpec((ts, tm), lambda i, j, k: (k, i)), pl.BlockSpec((ts, tn), lambda i, j, k: (k, j))],
        out_specs=out_spec, out_shape=out_shape, scratch=scratch, sem=("parallel", "parallel", "arbitrary"),
    )(a, b)


def _select_fwd(a, sel, name):
    _, L, M, K = a.shape
    N = sel.shape[2]
    tn = _pick_tile(N, 1792)

    def body(a_ref, s_ref, o_ref, acc_ref):
        s = pl.program_id(2)

        @pl.when(s == 0)
        def _():
            acc_ref[...] = jnp.zeros_like(acc_ref)

        acc_ref[...] += _dot(a_ref[0, 0], s_ref[0])

        @pl.when(s == 3)
        def _():
            o_ref[0] = acc_ref[...].astype(BF16)

    return _pcall(
        body, name=name, grid=(L, N // tn, 4),
        in_specs=[pl.BlockSpec((1, 1, M, K), lambda l, j, s: (s, l, 0, 0)),
                  pl.BlockSpec((1, K, tn), lambda l, j, s: (s, 0, j))],
        out_specs=pl.BlockSpec((1, M, tn), lambda l, j, s: (l, 0, j)),
        out_shape=_sds((L, M, N), BF16), scratch=(pltpu.VMEM((M, tn), F32),),
        sem=("parallel", "parallel", "arbitrary"),
    )(a, sel)


def _select_bwd(dw, sel, name):
    L, M, N = dw.shape
    K = sel.shape[1]
    tk = _pick_tile(N, 1792)
    nk = N // tk

    def body(d_ref, s_ref, o_ref, acc_ref):
        k = pl.program_id(2)

        @pl.when(k == 0)
        def _():
            acc_ref[...] = jnp.zeros_like(acc_ref)

        acc_ref[...] += _dot_nt(d_ref[0].astype(BF16), s_ref[0])

        @pl.when(k == nk - 1)
        def _():
            o_ref[0, 0] = acc_ref[...].astype(BF16)

    return _pcall(
        body, name=name, grid=(L, 4, nk),
        in_specs=[pl.BlockSpec((1, M, tk), lambda l, s, k: (l, 0, k)),
                  pl.BlockSpec((1, K, tk), lambda l, s, k: (s, 0, k))],
        out_specs=pl.BlockSpec((1, 1, M, K), lambda l, s, k: (l, s, 0, 0)),
        out_shape=_sds((L, 4, M, K), BF16), scratch=(pltpu.VMEM((M, K), F32),),
        sem=("parallel", "parallel", "arbitrary"),
    )(dw, sel)


MLA_RC = 128
LOG2E = math.log2(math.e)
MLA_QSCALE = MLA_SCALE * LOG2E


def _mla_fwd(q, k, v, tq):
    S = q.shape[0]
    nq = S // tq
    nt = tq // LANE
    pairs = [(i, j) for i in range(nq) for j in range(i + 1)]
    qi = jnp.asarray(np.array([p[0] for p in pairs], np.int32))
    kj = jnp.asarray(np.array([p[1] for p in pairs], np.int32))

    def body(qi_ref, kj_ref, q_ref, k_ref, v_ref, o_ref, lse_ref, m_s, l_s, acc_s):
        t = pl.program_id(1)
        i, j = qi_ref[t], kj_ref[t]

        @pl.when(j == 0)
        def _():
            m_s[...] = jnp.full_like(m_s, NEG)
            l_s[...] = jnp.zeros_like(l_s)
            acc_s[...] = jnp.zeros_like(acc_s)

        def step(masked):
            kb, vb = k_ref[...], v_ref[...]
            nc = tq // MLA_RC
            scores = [_dot_nt(q_ref[c * MLA_RC:(c + 1) * MLA_RC, :], kb) for c in range(nc)]
            for c in range(nc):
                rows = slice(c * MLA_RC, (c + 1) * MLA_RC)
                s = scores[c]
                if masked:
                    row = lax.broadcasted_iota(jnp.int32, (MLA_RC, tq), 0) + c * MLA_RC
                    col = lax.broadcasted_iota(jnp.int32, (MLA_RC, tq), 1)
                    s = jnp.where(col <= row, s, NEG)
                tiles = [s[:, _hs(u)] for u in range(nt)]
                mx = functools.reduce(jnp.maximum, tiles)
                m_old = m_s[rows, :]
                m_new = jnp.maximum(m_old, jnp.max(mx, axis=1, keepdims=True))
                alpha = jnp.exp2(m_old - m_new)
                ps = [jnp.exp2(u - m_new) for u in tiles]
                l_s[rows, :] = alpha * l_s[rows, :] + functools.reduce(jnp.add, ps)
                p = jnp.concatenate([u.astype(BF16) for u in ps], axis=1)
                acc_s[rows, :] = alpha * acc_s[rows, :] + _dot(p, vb)
                m_s[rows, :] = m_new

        @pl.when(j < i)
        def _():
            step(False)

        @pl.when(j == i)
        def _():
            step(True)
            l = jnp.sum(l_s[...], axis=1, keepdims=True)
            o_ref[...] = (acc_s[...] / l).astype(BF16)
            lse_ref[...] = m_s[...] + jnp.log2(l)

    qmap = lambda h, t, qi_r, kj_r: (qi_r[t], h)
    kmap = lambda h, t, qi_r, kj_r: (kj_r[t], h)
    return _pcall(
        body, name="mla_fwd", grid=(NH, len(pairs)), prefetch=2,
        in_specs=[pl.BlockSpec((tq, LANE), qmap), pl.BlockSpec((tq, LANE), kmap), pl.BlockSpec((tq, LANE), kmap)],
        out_specs=[pl.BlockSpec((tq, LANE), qmap), pl.BlockSpec((tq, LANE), qmap)],
        out_shape=[_sds((S, NH * LANE), BF16), _sds((S, NH * LANE), F32)],
        scratch=[pltpu.VMEM((tq, LANE), F32), pltpu.VMEM((tq, LANE), F32), pltpu.VMEM((tq, LANE), F32)],
        sem=("arbitrary", "arbitrary"),
    )(qi, kj, q, k, v)


def _mla_bwd(q, k, v, do, lse, delta, tq):
    S = q.shape[0]
    nq = S // tq
    pairs = [(i, j) for j in range(nq) for i in range(j, nq)]
    qi = jnp.asarray(np.array([p[0] for p in pairs], np.int32))
    kj = jnp.asarray(np.array([p[1] for p in pairs], np.int32))

    def body(qi_ref, kj_ref, q_ref, k_ref, v_ref, do_ref, lse_ref, dl_ref, dq_ref, dk_ref, dv_ref,
             dq_s, dk_s, dv_s):
        t = pl.program_id(1)
        i, j = qi_ref[t], kj_ref[t]

        @pl.when(t == 0)
        def _():
            dq_s[...] = jnp.zeros_like(dq_s)

        @pl.when(i == j)
        def _():
            dk_s[...] = jnp.zeros_like(dk_s)
            dv_s[...] = jnp.zeros_like(dv_s)

        qrows = pl.ds(pl.multiple_of(i * tq, tq), tq)

        def step(masked):
            qb, dob = q_ref[...], do_ref[...]
            lse_r, dl_r = lse_ref[0], dl_ref[0]
            dq = jnp.zeros((tq, LANE), F32)
            nc = tq // MLA_RC
            sts = [_dot_nt(k_ref[c * MLA_RC:(c + 1) * MLA_RC, :], qb) for c in range(nc)]
            dpts = [_dot_nt(v_ref[c * MLA_RC:(c + 1) * MLA_RC, :], dob) for c in range(nc)]
            for c in range(nc):
                rows = slice(c * MLA_RC, (c + 1) * MLA_RC)
                pt = jnp.exp2(sts[c] - lse_r)
                if masked:
                    key = lax.broadcasted_iota(jnp.int32, (MLA_RC, tq), 0) + c * MLA_RC
                    qry = lax.broadcasted_iota(jnp.int32, (MLA_RC, tq), 1)
                    pt = jnp.where(key <= qry, pt, 0.0)
                dv_s[rows, :] += _dot(pt.astype(BF16), dob)
                gt = (pt * (dpts[c] - dl_r)).astype(BF16)
                dk_s[rows, :] += _dot(gt, qb)
                dq = dq + _dot_tn(gt, k_ref[rows, :])
            dq_s[qrows, :] += dq

        @pl.when(i > j)
        def _():
            step(False)

        @pl.when(i == j)
        def _():
            step(True)
            dq_ref[...] = dq_s[qrows, :] * MLA_SCALE

        @pl.when(i == nq - 1)
        def _():
            dk_ref[...] = dk_s[...] * (1.0 / LOG2E)
            dv_ref[...] = dv_s[...]

    qmap = lambda h, t, qi_r, kj_r: (qi_r[t], h)
    kmap = lambda h, t, qi_r, kj_r: (kj_r[t], h)
    rmap = lambda h, t, qi_r, kj_r: (h, 0, qi_r[t])
    return _pcall(
        body, name="mla_bwd", grid=(NH, len(pairs)), prefetch=2,
        in_specs=[pl.BlockSpec((tq, LANE), qmap), pl.BlockSpec((tq, LANE), kmap), pl.BlockSpec((tq, LANE), kmap),
                  pl.BlockSpec((tq, LANE), qmap), pl.BlockSpec((1, 1, tq), rmap), pl.BlockSpec((1, 1, tq), rmap)],
        out_specs=[pl.BlockSpec((tq, LANE), kmap), pl.BlockSpec((tq, LANE), kmap), pl.BlockSpec((tq, LANE), kmap)],
        out_shape=[_sds((S, NH * LANE), F32), _sds((S, NH * LANE), F32), _sds((S, NH * LANE), F32)],
        scratch=[pltpu.VMEM((S, LANE), F32), pltpu.VMEM((tq, LANE), F32), pltpu.VMEM((tq, LANE), F32)],
        sem=("arbitrary", "arbitrary"),
    )(qi, kj, q, k, v, do, lse, delta)


def _rows_of(t):
    return t[:, ::LANE].T.reshape(NH, 1, -1)


SWA_SUB = 4
SWA_T = SWA_SUB * WIN


def _swa_specs(nsteps, rev):
    step = (lambda i: nsteps - 1 - i) if rev else (lambda i: i)
    cur = lambda w: pl.BlockSpec((SWA_T, w), lambda i: (step(i), 0))
    prev = pl.BlockSpec((WIN, LANE), lambda i: (jnp.maximum(step(i) * SWA_SUB - 1, 0), 0))
    return step, cur, prev


def _swa_probs(qk, bias_h, sink, first_mask):
    s = qk * SWA_SCALE + bias_h
    if first_mask is not None:
        s = jnp.where(first_mask, NEG, s)
    m = jnp.maximum(jnp.max(s, axis=1, keepdims=True), sink)
    e = jnp.exp(s - m)
    es = jnp.exp(sink - m)
    inv = 1.0 / (jnp.sum(e, axis=1, keepdims=True) + es)
    return e * inv, es * inv


def _swa_fwd(qs, ks, vs, bias, sinks):
    S = qs.shape[0]
    nsteps = S // SWA_T
    step, cur, prev = _swa_specs(nsteps, False)

    def body(qs_ref, kc_ref, kp_ref, vc_ref, vp_ref, bias_ref, sk_ref, o_ref):
        first = pl.program_id(0) == 0
        kk = jnp.concatenate([kp_ref[...], kc_ref[...]], axis=0)
        vv = jnp.concatenate([vp_ref[...], vc_ref[...]], axis=0)
        col = lax.broadcasted_iota(jnp.int32, (WIN, 2 * WIN), 1)
        for b in range(SWA_SUB):
            kkb = kk[b * WIN:(b + 2) * WIN]
            vvb = vv[b * WIN:(b + 2) * WIN]
            fm = (first & (col < WIN)) if b == 0 else None
            rows = slice(b * WIN, (b + 1) * WIN)
            qks = [_dot_nt(qs_ref[rows, _hs(h)], kkb) for h in range(NH)]
            for h in range(NH):
                p, _ = _swa_probs(qks[h], bias_ref[h], sk_ref[h:h + 1, 0:1], fm)
                o_ref[rows, _hs(h)] = _dot(p.astype(BF16), vvb).astype(BF16)

    return _pcall(
        body, name="swa_fwd", grid=(nsteps,),
        in_specs=[cur(NH * LANE), cur(LANE), prev, cur(LANE), prev, _full(NH, WIN, 2 * WIN), _full(NH, LANE)],
        out_specs=cur(NH * LANE), out_shape=_sds((S, NH * LANE), BF16), sem=("arbitrary",),
    )(qs, ks, ks, vs, vs, bias, sinks)


def _swa_bwd(qs, ks, vs, do, bias, sinks):
    S = qs.shape[0]
    nsteps = S // SWA_T
    step, cur, prev = _swa_specs(nsteps, True)

    def body(qs_ref, kc_ref, kp_ref, vc_ref, vp_ref, do_ref, bias_ref, sk_ref,
             dqs_ref, dks_ref, dvs_ref, dbias_ref, dsk_ref, dkk_s, dvv_s, ck_s, cv_s):
        pid = pl.program_id(0)
        first = step(pid) == 0

        @pl.when(pid == 0)
        def _():
            dbias_ref[...] = jnp.zeros_like(dbias_ref)
            dsk_ref[...] = jnp.zeros_like(dsk_ref)
            ck_s[...] = jnp.zeros_like(ck_s)
            cv_s[...] = jnp.zeros_like(cv_s)

        dkk_s[...] = jnp.zeros_like(dkk_s)
        dvv_s[...] = jnp.zeros_like(dvv_s)
        kk = jnp.concatenate([kp_ref[...], kc_ref[...]], axis=0)
        vv = jnp.concatenate([vp_ref[...], vc_ref[...]], axis=0)
        col = lax.broadcasted_iota(jnp.int32, (WIN, 2 * WIN), 1)
        for b in range(SWA_SUB):
            kkb = kk[b * WIN:(b + 2) * WIN]
            vvb = vv[b * WIN:(b + 2) * WIN]
            fm = (first & (col < WIN)) if b == 0 else None
            rows = slice(b * WIN, (b + 1) * WIN)
            keys = slice(b * WIN, (b + 2) * WIN)
            qks = [_dot_nt(qs_ref[rows, _hs(h)], kkb) for h in range(NH)]
            dps = [_dot_nt(do_ref[rows, _hs(h)], vvb) for h in range(NH)]
            for h in range(NH):
                qh = qs_ref[rows, _hs(h)]
                doh = do_ref[rows, _hs(h)]
                p, ps = _swa_probs(qks[h], bias_ref[h], sk_ref[h:h + 1, 0:1], fm)
                dp = dps[h]
                dl = jnp.sum(p * dp, axis=1, keepdims=True)
                ds = p * (dp - dl)
                dsk_ref[h:h + 1, :] += -jnp.sum(ps * dl)
                dbias_ref[h] += ds
                dsb = (ds * SWA_SCALE).astype(BF16)
                dqs_ref[rows, _hs(h)] = _dot(dsb, kkb).astype(BF16)
                dkk_s[keys, :] += _dot_tn(dsb, qh)
                dvv_s[keys, :] += _dot_tn(p.astype(BF16), doh)
        dks_ref[...] = dkk_s[WIN:, :]
        dvs_ref[...] = dvv_s[WIN:, :]
        dks_ref[SWA_T - WIN:, :] += ck_s[...]
        dvs_ref[SWA_T - WIN:, :] += cv_s[...]
        ck_s[...] = dkk_s[0:WIN, :]
        cv_s[...] = dvv_s[0:WIN, :]

    return _pcall(
        body, name="swa_bwd", grid=(nsteps,),
        in_specs=[cur(NH * LANE), cur(LANE), prev, cur(LANE), prev, cur(NH * LANE), _full(NH, WIN, 2 * WIN),
                  _full(NH, LANE)],
        out_specs=[cur(NH * LANE), cur(LANE), cur(LANE), _full(NH, WIN, 2 * WIN), _full(NH, LANE)],
        out_shape=[_sds((S, NH * LANE), BF16), _sds((S, LANE), F32), _sds((S, LANE), F32),
                   _sds((NH, WIN, 2 * WIN), F32), _sds((NH, LANE), F32)],
        scratch=[pltpu.VMEM((SWA_T + WIN, LANE), F32), pltpu.VMEM((SWA_T + WIN, LANE), F32),
                 pltpu.VMEM((WIN, LANE), F32), pltpu.VMEM((WIN, LANE), F32)],
        sem=("arbitrary",),
    )(qs, ks, ks, vs, vs, do, bias, sinks)


def _bias_build(rel_bias, bmap):
    def body(rb_ref, bmap_ref, o_ref):
        bm = bmap_ref[...]
        for h in range(NH):
            acc = jnp.full((WIN, 2 * WIN), NEG, F32)
            for b in range(REL_BUCKETS):
                acc = jnp.where(bm == b, rb_ref[b, h], acc)
            o_ref[h] = acc

    return _pcall(
        body, name="bias_build", grid=(1,),
        in_specs=[pl.BlockSpec(memory_space=pltpu.SMEM), _full(WIN, 2 * WIN)],
        out_specs=_full(NH, WIN, 2 * WIN), out_shape=_sds((NH, WIN, 2 * WIN), F32), sem=("arbitrary",),
    )(rel_bias, bmap)


def _bias_reduce(dbias, bmap):
    def body(db_ref, bmap_ref, o_ref):
        bm = bmap_ref[...]
        for h in range(NH):
            dbh = db_ref[h]
            for b in range(REL_BUCKETS):
                o_ref[b, h] = jnp.sum(jnp.where(bm == b, dbh, 0.0))

    return _pcall(
        body, name="bias_reduce", grid=(1,),
        in_specs=[_full(NH, WIN, 2 * WIN), _full(WIN, 2 * WIN)],
        out_specs=pl.BlockSpec(memory_space=pltpu.SMEM), out_shape=_sds((REL_BUCKETS, NH), F32), sem=("arbitrary",),
    )(dbias, bmap)


def _memkv_fwd(mem, mnorm, wkv):
    def body(mem_ref, g_ref, w_ref, o_ref):
        n, _ = _rms(mem_ref[...])
        o_ref[...] = _dot((n * g_ref[...]).astype(BF16), w_ref[...]).astype(BF16)

    return _pcall(
        body, name="memkv_fwd", grid=(1,), in_specs=[_full(MEM_LEN, D), _full(1, D), _full(D, D)],
        out_specs=_full(MEM_LEN, D), out_shape=_sds((MEM_LEN, D), BF16), sem=("arbitrary",),
    )(mem, mnorm, wkv)


def _mem_probs(qk):
    s = qk * MEM_SCALE
    e = jnp.exp(s - jnp.max(s, axis=1, keepdims=True))
    return e / jnp.sum(e, axis=1, keepdims=True)


def _mem_fwd(qm, kvm, ts):
    S = qm.shape[0]

    def body(q_ref, kv_ref, o_ref):
        qks = [_dot_nt(q_ref[:, _hs(h)], kv_ref[:, _hs(h)]) for h in range(MEM_H)]
        for h in range(MEM_H):
            p = _mem_probs(qks[h])
            o_ref[:, _hs(h)] = _dot(p.astype(BF16), kv_ref[:, _hs(MEM_H + h)]).astype(BF16)

    return _pcall(
        body, name="mem_fwd", grid=(S // ts,), in_specs=[_tok(ts, 512), _full(MEM_LEN, D)],
        out_specs=_tok(ts, 512), out_shape=_sds((S, 512), BF16), sem=("arbitrary",),
    )(qm, kvm)


def _mem_bwd(qm, kvm, do, ts):
    S = qm.shape[0]

    def body(q_ref, kv_ref, do_ref, dq_ref, dkv_ref):
        @pl.when(pl.program_id(0) == 0)
        def _():
            dkv_ref[...] = jnp.zeros_like(dkv_ref)

        qks = [_dot_nt(q_ref[:, _hs(h)], kv_ref[:, _hs(h)]) for h in range(MEM_H)]
        dps = [_dot_nt(do_ref[:, _hs(h)], kv_ref[:, _hs(MEM_H + h)]) for h in range(MEM_H)]
        for h in range(MEM_H):
            qh, kh, doh = q_ref[:, _hs(h)], kv_ref[:, _hs(h)], do_ref[:, _hs(h)]
            p = _mem_probs(qks[h])
            dp = dps[h]
            ds = (p * (dp - jnp.sum(p * dp, axis=1, keepdims=True)) * MEM_SCALE).astype(BF16)
            dq_ref[:, _hs(h)] = _dot(ds, kh).astype(BF16)
            dkv_ref[:, _hs(h)] += _dot_tn(ds, qh)
            dkv_ref[:, _hs(MEM_H + h)] += _dot_tn(p.astype(BF16), doh)

    return _pcall(
        body, name="mem_bwd", grid=(S // ts,), in_specs=[_tok(ts, 512), _full(MEM_LEN, D), _tok(ts, 512)],
        out_specs=[_tok(ts, 512), _full(MEM_LEN, D)],
        out_shape=[_sds((S, 512), BF16), _sds((MEM_LEN, D), F32)], sem=("arbitrary",),
    )(qm, kvm, do)


def _memkv_bwd(mem, mnorm, wkv, dkvm):
    def body(mem_ref, g_ref, w_ref, dkv_ref, dw_ref, dg_ref):
        n, _ = _rms(mem_ref[...])
        dkvb = dkv_ref[...].astype(BF16)
        dw_ref[...] = _dot_tn((n * g_ref[...]).astype(BF16), dkvb)
        dg_ref[...] = _colsum(_dot_nt(dkvb, w_ref[...]) * n)

    return _pcall(
        body, name="memkv_bwd", grid=(1,), in_specs=[_full(MEM_LEN, D), _full(1, D), _full(D, D), _full(MEM_LEN, D)],
        out_specs=[_full(D, D), _full(1, D)], out_shape=[_sds((D, D), F32), _sds((1, D), F32)], sem=("arbitrary",),
    )(mem, mnorm, wkv, dkvm)


def _adamw(w, g, m, v, name):
    rows, cols = w.shape
    tr = min(rows, FLAT_TILE)
    assert rows % tr == 0

    def body(w_ref, g_ref, m_ref, v_ref, d_ref, nm_ref, nv_ref):
        gv = g_ref[...]
        nm = ADAM_B1 * m_ref[...] + (1.0 - ADAM_B1) * gv
        nv = ADAM_B2 * v_ref[...] + (1.0 - ADAM_B2) * jnp.square(gv)
        m_hat = nm / (1.0 - ADAM_B1 ** ADAM_STEP)
        v_hat = nv / (1.0 - ADAM_B2 ** ADAM_STEP)
        d_ref[...] = -ADAM_LR * (m_hat / (jnp.sqrt(v_hat) + ADAM_EPS) + ADAM_WD * w_ref[...])
        nm_ref[...] = nm
        nv_ref[...] = nv

    spec = _tok(tr, cols)
    return _pcall(
        body, name=name, grid=(rows // tr,), in_specs=[spec] * 4, out_specs=[spec] * 3,
        out_shape=[_sds((rows, cols), F32)] * 3, sem=("arbitrary",),
    )(w, g, m, v)


def _my_place():
    return lax.axis_index("x"), lax.axis_index("y"), lax.axis_index("c")


def _remote(src, dst, send_sems, recv_sems, k, to):
    return pltpu.make_async_remote_copy(src_ref=src, dst_ref=dst, send_sem=send_sems.at[k], recv_sem=recv_sems.at[k],
                                        device_id=to, device_id_type=MESH_ID)


def _gather_weights(shards):
    n = len(shards)

    def body(*refs):
        ins, outs = refs[:n], refs[n:2 * n]
        send_sems, recv_sems, local_sems = refs[2 * n:]
        x, y, c = _my_place()
        chips = [(1 - x, y), (x, 1 - y), (1 - x, 1 - y)]
        mine = [pltpu.make_async_copy(ins[i], outs[i].at[2 * x + y], local_sems.at[i]) for i in range(n)]
        for cp in mine:
            cp.start()

        def copy(i, k, slot, to):
            return _remote(ins[i], outs[i].at[slot], send_sems, recv_sems, 3 * i + k, to)

        sends = [copy(i, k, 2 * x + y, (px, py, c)) for i in range(n) for k, (px, py) in enumerate(chips)]
        for cp in sends:
            cp.start()
        for i in range(n):
            for k, (px, py) in enumerate(chips):
                copy(i, k, 2 * px + py, (px, py, c)).wait_recv()
        for cp in sends:
            cp.wait_send()
        for cp in mine:
            cp.wait()

    hbm = pl.BlockSpec(memory_space=pl.ANY)
    return pl.pallas_call(
        body, name="gather_weights", out_shape=[_sds((4,) + s.shape, s.dtype) for s in shards],
        in_specs=[hbm] * n, out_specs=[hbm] * n,
        scratch_shapes=[pltpu.SemaphoreType.DMA((3 * n,)), pltpu.SemaphoreType.DMA((3 * n,)),
                        pltpu.SemaphoreType.DMA((n,))],
    )(*shards)


def _scatter_grads(gws, gsmall):
    n = len(gws)

    def body(*refs):
        ins, gs_ref = refs[:n], refs[n]
        outs, rs_ref = refs[n + 1:2 * n + 1], refs[2 * n + 1]
        send_sems, recv_sems, local_sems = refs[2 * n + 2:]
        x, y, c = _my_place()
        me = 4 * x + 2 * y + c
        mine = [pltpu.make_async_copy(ins[i].at[c, 2 * x + y], outs[i].at[me], local_sems.at[i]) for i in range(n)]
        mine.append(pltpu.make_async_copy(gs_ref, rs_ref.at[me], local_sems.at[n]))
        for cp in mine:
            cp.start()

        def peer(m):
            return (x ^ ((m >> 2) & 1), y ^ ((m >> 1) & 1), c ^ (m & 1))

        def big(i, m, layer, shard, slot, to):
            return _remote(ins[i].at[layer, shard], outs[i].at[slot], send_sems, recv_sems, (m - 1) * (n + 1) + i, to)

        def small(m, slot, to):
            return _remote(gs_ref, rs_ref.at[slot], send_sems, recv_sems, (m - 1) * (n + 1) + n, to)

        sends = []
        for m in range(1, 8):
            px, py, pc = peer(m)
            sends += [big(i, m, pc, 2 * px + py, me, (px, py, pc)) for i in range(n)]
            sends.append(small(m, me, (px, py, pc)))
        for cp in sends:
            cp.start()
        for m in range(1, 8):
            px, py, pc = peer(m)
            them = 4 * px + 2 * py + pc
            for i in range(n):
                big(i, m, c, 2 * x + y, them, (px, py, pc)).wait_recv()
            small(m, them, (px, py, pc)).wait_recv()
        for cp in sends:
            cp.wait_send()
        for cp in mine:
            cp.wait()

    hbm = pl.BlockSpec(memory_space=pl.ANY)
    nsem = 7 * (n + 1)
    return pl.pallas_call(
        body, name="scatter_grads",
        out_shape=[_sds((8,) + g.shape[2:], g.dtype) for g in gws] + [_sds((8,) + gsmall.shape, gsmall.dtype)],
        in_specs=[hbm] * (n + 1), out_specs=[hbm] * (n + 1),
        scratch_shapes=[pltpu.SemaphoreType.DMA((nsem,)), pltpu.SemaphoreType.DMA((nsem,)),
                        pltpu.SemaphoreType.DMA((n + 1,))],
    )(*gws, gsmall)


def _sum_slots(slots, name):
    _, r, c = slots.shape
    tr = min(r, FLAT_TILE)
    assert r % tr == 0

    def body(s_ref, o_ref):
        acc = s_ref[0].astype(F32)
        for d in range(1, 8):
            acc = acc + s_ref[d].astype(F32)
        o_ref[...] = acc

    return _pcall(
        body, name=name, grid=(r // tr,), in_specs=[pl.BlockSpec((8, tr, c), lambda i: (0, i, 0))],
        out_specs=_tok(tr, c), out_shape=_sds((r, c), F32), sem=("arbitrary",),
    )(slots)


def _swap_layers(reds):
    n = len(reds)

    def body(*refs):
        ins, outs = refs[:n], refs[n:2 * n]
        send_sems, recv_sems, local_sems = refs[2 * n:]
        x, y, c = _my_place()
        sibling = (x, y, 1 - c)
        mine = [pltpu.make_async_copy(ins[i], outs[i].at[c], local_sems.at[i]) for i in range(n)]
        for cp in mine:
            cp.start()
        sends = [_remote(ins[i], outs[i].at[c], send_sems, recv_sems, i, sibling) for i in range(n)]
        for cp in sends:
            cp.start()
        for i in range(n):
            _remote(ins[i], outs[i].at[1 - c], send_sems, recv_sems, i, sibling).wait_recv()
        for cp in sends:
            cp.wait_send()
        for cp in mine:
            cp.wait()

    hbm = pl.BlockSpec(memory_space=pl.ANY)
    return pl.pallas_call(
        body, name="swap_layers", out_shape=[_sds((2,) + r.shape, r.dtype) for r in reds],
        in_specs=[hbm] * n, out_specs=[hbm] * n,
        scratch_shapes=[pltpu.SemaphoreType.DMA((n,)), pltpu.SemaphoreType.DMA((n,)), pltpu.SemaphoreType.DMA((n,))],
    )(*reds)


W_IN_SHARD, W_IN_SHARD_PAD = 1192, 1280
W_UQ_SHARD, W_UQ_SHARD_PAD = 192, 256


def _swa_place(t):
    z = jnp.zeros_like(t)
    lo = jnp.concatenate([t, z], axis=1)
    hi = jnp.concatenate([z, t], axis=1)
    group = (jnp.arange(NH) // SWA_R).reshape((NH,) + (1,) * (t.ndim - 1))
    full = jnp.where(group == 0, lo, hi)
    return full.reshape((NH * LANE,) + t.shape[2:])


def _swa_unplace(t):
    t = t.reshape((NH, 2, 64) + t.shape[1:])
    return jnp.concatenate([t[:SWA_R, 0], t[SWA_R:, 1]], axis=0)


def _pad_w_o_mla(w):
    return jnp.pad(w.reshape(NH, 64, D), ((0, 0), (0, 64), (0, 0))).reshape(NH * LANE, D)


def _unpad_w_o_mla(g):
    return g.reshape(NH, LANE, D)[:, :64].reshape(NH * 64, D)


def _w_in_cols():
    src = np.full((C_END,), -1, np.int64)
    src[C_CQ:C_KPE] = np.arange(0, 384)
    src[C_KPE + 64:C_KPE + 96] = np.arange(384, 416)
    for h in range(NH):
        at = C_QS + h * LANE + 64 * (h // SWA_R)
        src[at:at + 64] = 416 + h * 64 + np.arange(64)
    src[C_KS:C_END] = np.arange(928, IN_COLS)
    return src


def _w_uq_cols():
    src = np.full((NH * LANE,), -1, np.int64)
    for h in range(NH):
        src[h * LANE:h * LANE + 96] = h * 96 + np.arange(96)
    return src


def _w_ukv_cols():
    src = np.full((2 * NH * LANE,), -1, np.int64)
    for h in range(NH):
        src[h * LANE:h * LANE + 64] = h * 128 + np.arange(64)
        src[NH * LANE + h * LANE:NH * LANE + h * LANE + 64] = h * 128 + 64 + np.arange(64)
    return src


def _selection(src_cols, width, width_pad):
    want = jnp.asarray(np.asarray(src_cols, np.int32))[None, None, :]
    k = jnp.arange(width_pad, dtype=jnp.int32)[None, :, None]
    have = jnp.where(k < width, jnp.arange(4, dtype=jnp.int32)[:, None, None] * width + k, -2)
    return (want == have).astype(BF16)


def _selections():
    return dict(w_in=_selection(_w_in_cols(), W_IN_SHARD, W_IN_SHARD_PAD),
                w_uq=_selection(_w_uq_cols(), W_UQ_SHARD, W_UQ_SHARD_PAD),
                w_ukv=_selection(_w_ukv_cols(), 256, 256))


def _pad_last(a, width):
    return jnp.pad(a, ((0, 0),) * (a.ndim - 1) + ((0, width - a.shape[-1]),))


def _wire_shards(W):
    out = {n: W[n].astype(BF16) for n in SHARDED}
    out["w_in"] = _pad_last(out["w_in"], W_IN_SHARD_PAD)
    out["w_uq"] = _pad_last(out["w_uq"], W_UQ_SHARD_PAD)
    return out


def _join_shards(t, axis):
    _, L, r, c = t.shape
    if axis == 2:
        return t.transpose(1, 2, 0, 3).reshape(L, r, 4 * c)
    return t.transpose(1, 0, 2, 3).reshape(L, 4 * r, c)


def _kernel_weights(gathered, sels):
    win = _select_fwd(gathered["w_in"], sels["w_in"], "lay_w_in")
    wuq = _select_fwd(gathered["w_uq"], sels["w_uq"], "lay_w_uq")
    wukv = _select_fwd(gathered["w_ukv"], sels["w_ukv"], "lay_w_ukv")
    whole = {n: _join_shards(gathered[n], SHARD_AXIS[n])
             for n in ("w_mem_kv", "w_o_mla", "w_o_swa", "w_o_mem", "w_out", "w_up", "w_down")}
    return [dict(win=win[l], wuq=wuq[l], wukv=wukv[l], wmem=whole["w_mem_kv"][l],
                 wa=_pad_w_o_mla(whole["w_o_mla"][l]), wb=_swa_place(whole["w_o_swa"][l].reshape(NH, 64, D)),
                 wc=whole["w_o_mem"][l], wout=whole["w_out"][l], wup=whole["w_up"][l], wdown=whole["w_down"][l])
            for l in range(DEPTH)]


def _cols_to_shards(g):
    r, c4 = g.shape
    return g.reshape(r, 4, c4 // 4).transpose(1, 0, 2).astype(BF16)


def _rope_tables(S):
    pos = jnp.arange(S, dtype=F32)
    inv = 1.0 / (ROPE_THETA ** (jnp.arange(0, 32, 2, dtype=F32) / 32))
    ang = pos[:, None] * inv[None, :]
    cos, sin = jnp.cos(ang), jnp.sin(ang)
    one, zero = jnp.ones((S, 64), F32), jnp.zeros((S, 16), F32)
    rc = jnp.concatenate([one, cos, cos, jnp.ones((S, 32), F32)], axis=1)
    rs1 = jnp.concatenate([jnp.zeros((S, 64), F32), zero, sin, jnp.zeros((S, 32), F32)], axis=1)
    rs2 = jnp.concatenate([jnp.zeros((S, 64), F32), -sin, zero, jnp.zeros((S, 32), F32)], axis=1)
    return rc, rs1, rs2


def _bucket_map():
    qi = jnp.arange(WIN)[:, None]
    kj = jnp.arange(2 * WIN)[None, :]
    dist = qi + WIN - kj
    n = jnp.maximum(dist, 0)
    max_exact = REL_BUCKETS // 2
    nf = jnp.maximum(n, 1).astype(F32)
    large = max_exact + (jnp.log(nf / max_exact) / math.log(128 / max_exact)
                         * (REL_BUCKETS - max_exact)).astype(jnp.int32)
    large = jnp.minimum(large, REL_BUCKETS - 1)
    bucket = jnp.where(n < max_exact, n, large)
    return jnp.where((dist >= 0) & (dist < WIN), bucket, -1).astype(jnp.int32)


TS = 256
TQ = 1024
TQ_FWD = 1024


def _local_step(x, mem, tgt, kw, sp, sels):
    S = x.shape[0]
    ts = min(TS, S)
    tq = min(TQ, S)
    rc, rs1, rs2 = _rope_tables(S)
    bmap = _bucket_map()
    bias = _bias_build(sp["rel_bias"], bmap)
    row = lambda v: v.reshape(1, -1)

    saved = []
    for l in range(DEPTH):
        w = kw[l]
        an, qn, kvn = row(sp["attn_norm"][l]), row(sp["mla_q_norm"][l]), row(sp["mla_kv_norm"][l])
        bg, mnorm, mlpn = row(sp["b_gate"][l]), row(sp["mem_norm"][l]), row(sp["mlp_norm"][l])
        sinks = jnp.broadcast_to(sp["attn_sinks"][l][:, None], (NH, LANE))
        q, k, v, qs, ks, vs, qm, g = _pre_fwd(x, an, w["win"], bg, qn, kvn, w["wuq"], w["wukv"], rc, rs1, rs2, ts)
        oa, lse = _mla_fwd(q, k, v, min(TQ_FWD, S))
        ob = _swa_fwd(qs, ks, vs, bias, sinks)
        kvm = _memkv_fwd(mem, mnorm, w["wmem"])
        oc = _mem_fwd(qm, kvm, ts)
        x1, yb = _merge_fwd(x, g, oa, ob, oc, w["wa"], w["wb"], w["wc"], w["wout"], ts)
        x2 = _mlp_fwd(x1, mlpn, w["wup"], w["wdown"], ts)
        saved.append(dict(w=w, x=x, x1=x1, q=q, k=k, v=v, qs=qs, ks=ks, vs=vs, qm=qm, g=g, oa=oa, lse=lse, ob=ob,
                          oc=oc, kvm=kvm, yb=yb, an=an, qn=qn, kvn=kvn, mnorm=mnorm, mlpn=mlpn, sinks=sinks))
        x = x2

    sq, dx, dfn = _loss_kernel(x, row(sp["final_norm"]), tgt, ts)

    big = {n: [None] * DEPTH for n in SHARDED}
    small = {n: [None] * DEPTH for n in SMALL if n not in ("rel_bias", "final_norm")}
    dbias_total = None
    for l in reversed(range(DEPTH)):
        sv = saved[l]
        w = sv["w"]
        dx1, hb2, dub, ab, dxb, dmlpn = _mlp_bwd(dx, sv["x1"], sv["mlpn"], w["wup"], w["wdown"], ts)
        big["w_up"][l] = _matmul_tn(hb2, dub, "dw_up", shard_axis=1)
        big["w_down"][l] = _matmul_tn(ab, dxb, "dw_down", shard_axis=0)
        small["mlp_norm"][l] = dmlpn[0]

        dgp, dyo, doa, dob, doc, dla, dx1b, dbg = _merge_bwd(dx1, sv["g"], sv["oa"], sv["ob"], sv["oc"], w["wa"],
                                                             w["wb"], w["wc"], w["wout"], ts)
        big["w_out"][l] = _matmul_tn(sv["yb"], dx1b, "dw_out", shard_axis=0)
        big["w_o_mla"][l] = _cols_to_shards(_unpad_w_o_mla(_matmul_tn(sv["oa"], dyo[:, 0:D], "dw_o_mla")))
        big["w_o_swa"][l] = _cols_to_shards(
            _swa_unplace(_matmul_tn(sv["ob"], dyo[:, D:2 * D], "dw_o_swa")).reshape(NH * 64, D))
        big["w_o_mem"][l] = _matmul_tn(sv["oc"], dyo[:, 2 * D:3 * D], "dw_o_mem", shard_axis=1)
        small["b_gate"][l] = dbg[0]

        dqm, dkvm = _mem_bwd(sv["qm"], sv["kvm"], doc, ts)
        dwmem, dmnorm = _memkv_bwd(mem, sv["mnorm"], w["wmem"], dkvm)
        big["w_mem_kv"][l] = dwmem.reshape(4, D // 4, D).astype(BF16)
        small["mem_norm"][l] = dmnorm[0]

        dqs, dks, dvs, dbias, dsink = _swa_bwd(sv["qs"], sv["ks"], sv["vs"], dob, bias, sv["sinks"])
        dbias_total = dbias if dbias_total is None else dbias_total + dbias
        small["attn_sinks"][l] = dsink[:, 0]

        dq, dk, dv = _mla_bwd(sv["q"], sv["k"], sv["v"], doa, _rows_of(sv["lse"]), _rows_of(dla), tq)

        dx, dproj, hb, cqn, ckvn, dqpre, dkv, dan, dqn, dkvn = _pre_bwd(
            sv["x"], dx1, dq, dk, dv, dqs, dks, dvs, dqm, dgp, sv["an"], sv["qn"], sv["kvn"], w["win"], w["wuq"],
            w["wukv"], rc, rs1, rs2, ts)
        big["w_in"][l] = _matmul_tn(hb, dproj, "dw_in")
        big["w_uq"][l] = _matmul_tn(cqn, dqpre, "dw_uq")
        big["w_ukv"][l] = _matmul_tn(ckvn, dkv, "dw_ukv")
        small["attn_norm"][l] = dan[0]
        small["mla_q_norm"][l] = dqn[0]
        small["mla_kv_norm"][l] = dkvn[0]

    gw = {n: jnp.stack(v) for n, v in big.items()}
    for n in ("w_in", "w_uq", "w_ukv"):
        gw[n] = _select_bwd(gw[n], sels[n], "shard_d" + n)
    gs = {n: jnp.stack(v) for n, v in small.items()}
    gs["rel_bias"] = _bias_reduce(dbias_total, bmap)
    gs["final_norm"] = dfn[0]
    return sq, dx, gw, gs


def _flatten(parts, rows):
    flat = jnp.concatenate([p.reshape(-1) for p in parts])
    return jnp.pad(flat, (0, rows * FLAT_W - flat.shape[0])).reshape(rows, FLAT_W)


def _unflatten(buf, shapes):
    flat = buf.reshape(-1)
    out, at = [], 0
    for s in shapes:
        n = int(np.prod(s))
        out.append(flat[at:at + n].reshape(s))
        at += n
    return out


def kernel(x, mem, rel_bias, attn_norm, mem_norm, w_in, b_gate, mla_q_norm, w_uq, mla_kv_norm, w_ukv, attn_sinks, w_mem_kv, w_o_mla, w_o_swa, w_o_mem, w_out, mlp_norm, w_up, w_down, final_norm, loss_target, m_rel_bias, m_attn_norm, m_mem_norm, m_w_in, m_b_gate, m_mla_q_norm, m_w_uq, m_mla_kv_norm, m_w_ukv, m_attn_sinks, m_w_mem_kv, m_w_o_mla, m_w_o_swa, m_w_o_mem, m_w_out, m_mlp_norm, m_w_up, m_w_down, m_final_norm, v_rel_bias, v_attn_norm, v_mem_norm, v_w_in, v_b_gate, v_mla_q_norm, v_w_uq, v_mla_kv_norm, v_w_ukv, v_attn_sinks, v_w_mem_kv, v_w_o_mla, v_w_o_swa, v_w_o_mem, v_w_out, v_mlp_norm, v_w_up, v_w_down, v_final_norm):
    args = dict(locals())
    W = {n: args[n] for n in WEIGHTS}
    M = {n: args["m_" + n] for n in WEIGHTS}
    V = {n: args["v_" + n] for n in WEIGHTS}
    small_shapes = [W[n].shape for n in SMALL]
    sels = _selections()

    wire = _wire_shards(W)
    gathered = dict(zip(SHARDED, _gather_weights([wire[n] for n in SHARDED])))
    kw = _kernel_weights(gathered, sels)
    sp = {n: W[n] for n in SMALL}

    sq, dx, gw, gs = _local_step(x[0], mem[0], loss_target[0], kw, sp, sels)

    gsmall = _flatten([gs[n] for n in SMALL], SMALL_ROWS)
    *slots, small_slots = _scatter_grads([gw[n] for n in SHARDED], gsmall)
    reds = [_sum_slots(s, "sum_" + n) for n, s in zip(SHARDED, slots)]
    red_small = _sum_slots(small_slots, "sum_small")
    G = dict(zip(SHARDED, _swap_layers(reds)))
    G["w_in"] = G["w_in"][..., :W_IN_SHARD]
    G["w_uq"] = G["w_uq"][..., :W_UQ_SHARD]

    DW, NM, NV = {}, {}, {}
    for n in SHARDED:
        shape = W[n].shape
        two_d = lambda a: a.reshape(-1, shape[-1])
        d, nm, nv = _adamw(two_d(W[n]), two_d(G[n]), two_d(M[n]), two_d(V[n]), "adamw_" + n)
        DW[n], NM[n], NV[n] = d.reshape(shape), nm.reshape(shape), nv.reshape(shape)
    d_s, m_s, v_s = _adamw(_flatten([W[n] for n in SMALL], SMALL_ROWS), red_small,
                           _flatten([M[n] for n in SMALL], SMALL_ROWS), _flatten([V[n] for n in SMALL], SMALL_ROWS),
                           "adamw_small")
    for out, buf in ((G, red_small), (DW, d_s), (NM, m_s), (NV, v_s)):
        out.update(zip(SMALL, _unflatten(buf, small_shapes)))
    loss = lax.psum(0.5 * sq[0, 0] / D, ("x", "y", "c"))
    return (loss, dx[None], *[G[n] for n in WEIGHTS], *[DW[n] for n in WEIGHTS], *[NM[n] for n in WEIGHTS],
            *[NV[n] for n in WEIGHTS])
```

```python
import functools
import math

import numpy as np
import jax
import jax.numpy as jnp
from jax import lax
from jax.experimental import pallas as pl
from jax.experimental.pallas import tpu as pltpu

F32 = jnp.float32
BF16 = jnp.bfloat16

D = 1024
DFF = 4096
DEPTH = 2
EPS = 1e-6
LANE = 128
NH = 8
SWA_R = 4
MEM_H = 4
MEM_LEN = 256
WIN = 128
NEG = -1e30
MLA_SCALE = 96 ** -0.5
SWA_SCALE = 64 ** -0.5
MEM_SCALE = 128 ** -0.5
REL_BUCKETS = 32
ROPE_THETA = 10000.0

C_CQ, C_CKV, C_KPE, C_QS, C_KS, C_VS, C_QM, C_G, C_END = 0, 256, 384, 512, 1536, 1664, 1792, 2304, 5376
IN_COLS = 4768

ADAM_LR = 0.001
ADAM_B1 = 0.9
ADAM_B2 = 0.999
ADAM_EPS = 1e-08
ADAM_WD = 0.01
ADAM_STEP = 10

VMEM_LIMIT = 56 * 1024 * 1024

SHARDED = ("w_in", "w_uq", "w_ukv", "w_mem_kv", "w_o_mla", "w_o_swa", "w_o_mem", "w_out", "w_up", "w_down")
SHARD_AXIS = {"w_in": 2, "w_uq": 2, "w_ukv": 2, "w_mem_kv": 1, "w_o_mla": 2, "w_o_swa": 2, "w_o_mem": 2,
              "w_out": 1, "w_up": 2, "w_down": 1}
SMALL = ("rel_bias", "attn_norm", "mem_norm", "b_gate", "mla_q_norm", "mla_kv_norm", "attn_sinks", "mlp_norm",
         "final_norm")
WEIGHTS = ("rel_bias", "attn_norm", "mem_norm", "w_in", "b_gate", "mla_q_norm", "w_uq", "mla_kv_norm", "w_ukv",
           "attn_sinks", "w_mem_kv", "w_o_mla", "w_o_swa", "w_o_mem", "w_out", "mlp_norm", "w_up", "w_down",
           "final_norm")
FLAT_W = 1024
FLAT_TILE = 256
SMALL_ROWS = 16
MESH_ID = pl.DeviceIdType.MESH


def _pcall(body, *, name, grid, in_specs, out_specs, out_shape, scratch=(), prefetch=0, sem=None):
    params = pltpu.CompilerParams(dimension_semantics=sem, vmem_limit_bytes=VMEM_LIMIT)
    if prefetch:
        spec = pltpu.PrefetchScalarGridSpec(num_scalar_prefetch=prefetch, grid=grid, in_specs=in_specs,
                                            out_specs=out_specs, scratch_shapes=scratch)
        return pl.pallas_call(body, name=name, grid_spec=spec, out_shape=out_shape, compiler_params=params)
    return pl.pallas_call(body, name=name, grid=grid, in_specs=in_specs, out_specs=out_specs, out_shape=out_shape,
                          scratch_shapes=scratch, compiler_params=params)


def _tok(ts, w):
    return pl.BlockSpec((ts, w), lambda i: (i, 0))


def _full(*shape):
    return pl.BlockSpec(shape, lambda *_: (0,) * len(shape))


def _sds(shape, dtype):
    return jax.ShapeDtypeStruct(shape, dtype)


def _dot(a, b):
    return jnp.dot(a, b, preferred_element_type=F32)


def _dot_nt(a, b):
    return lax.dot_general(a, b, (((1,), (1,)), ((), ())), preferred_element_type=F32)


def _dot_tn(a, b):
    return lax.dot_general(a, b, (((0,), (0,)), ((), ())), preferred_element_type=F32)


def _rms(x):
    r = lax.rsqrt(jnp.mean(x * x, axis=-1, keepdims=True) + EPS)
    return x * r, r


def _rms_bwd(dyg, n, r):
    return r * (dyg - n * jnp.mean(n * dyg, axis=-1, keepdims=True))


def _rope(t, c, s1, s2):
    return t * c + pltpu.roll(t, 16, 1) * s1 + pltpu.roll(t, LANE - 16, 1) * s2


def _rope_bwd(dy, c, s1, s2):
    return dy * c + pltpu.roll(dy * s1, LANE - 16, 1) + pltpu.roll(dy * s2, 16, 1)


def _hs(h):
    return slice(h * LANE, (h + 1) * LANE)


def _colsum(t):
    return jnp.sum(t, axis=0, keepdims=True)


def _pre_fwd(x, an, win, bg, qn, kvn, wuq, wukv, rc, rs1, rs2, ts):
    S = x.shape[0]

    def body(x_ref, an_ref, win_ref, bg_ref, qn_ref, kvn_ref, wuq_ref, wukv_ref, rc_ref, rs1_ref, rs2_ref,
             q_ref, k_ref, v_ref, qs_ref, ks_ref, vs_ref, qm_ref, g_ref):
        n, _ = _rms(x_ref[...])
        hb = (n * an_ref[...]).astype(BF16)
        pa = _dot(hb, win_ref[:, C_CQ:C_QS])
        ncq, _ = _rms(pa[:, 0:256])
        cqn = (ncq * qn_ref[...]).astype(BF16)
        nkv, _ = _rms(pa[:, 256:384])
        ckvn = (nkv * kvn_ref[...]).astype(BF16)
        c, s1, s2 = rc_ref[...], rs1_ref[...], rs2_ref[...]
        kper = _rope(pa[:, 384:512], c, s1, s2)
        qp = _dot(cqn, wuq_ref[...])
        kv = _dot(ckvn, wukv_ref[...])
        for h in range(NH):
            q_ref[:, _hs(h)] = (_rope(qp[:, _hs(h)], c, s1, s2) * MLA_QSCALE).astype(BF16)
            k_ref[:, _hs(h)] = (kv[:, _hs(h)] + kper).astype(BF16)
        v_ref[...] = kv[:, NH * LANE:].astype(BF16)
        pb = _dot(hb, win_ref[:, C_QS:C_G])
        qs_ref[...] = pb[:, 0:1024].astype(BF16)
        ks_ref[...] = pb[:, 1024:1152].astype(BF16)
        vs_ref[...] = pb[:, 1152:1280].astype(BF16)
        qm_ref[...] = pb[:, 1280:1792].astype(BF16)
        g_ref[...] = jax.nn.sigmoid(_dot(hb, win_ref[:, C_G:C_END]) + bg_ref[...])

    return _pcall(
        body, name="pre_fwd", grid=(S // ts,),
        in_specs=[_tok(ts, D), _full(1, D), _full(D, C_END), _full(1, 3 * D), _full(1, 256), _full(1, 128),
                  _full(256, NH * LANE), _full(128, 2 * NH * LANE), _tok(ts, LANE), _tok(ts, LANE), _tok(ts, LANE)],
        out_specs=[_tok(ts, 1024), _tok(ts, 1024), _tok(ts, 1024), _tok(ts, 1024), _tok(ts, 128), _tok(ts, 128),
                   _tok(ts, 512), _tok(ts, 3 * D)],
        out_shape=[_sds((S, 1024), BF16), _sds((S, 1024), BF16), _sds((S, 1024), BF16), _sds((S, 1024), BF16),
                   _sds((S, 128), BF16), _sds((S, 128), BF16), _sds((S, 512), BF16), _sds((S, 3 * D), F32)],
        sem=("arbitrary",),
    )(x, an, win, bg, qn, kvn, wuq, wukv, rc, rs1, rs2)


def _merge_fwd(x, g, oa, ob, oc, wa, wb, wc, wout, ts):
    S = x.shape[0]

    def body(x_ref, g_ref, oa_ref, ob_ref, oc_ref, wa_ref, wb_ref, wc_ref, wout_ref, x1_ref, yb_ref):
        y = g_ref[:, 0:D] * _dot(oa_ref[...], wa_ref[...])
        y = y + g_ref[:, D:2 * D] * _dot(ob_ref[...], wb_ref[...])
        y = y + g_ref[:, 2 * D:3 * D] * _dot(oc_ref[...], wc_ref[...])
        yb = y.astype(BF16)
        yb_ref[...] = yb
        x1_ref[...] = x_ref[...] + _dot(yb, wout_ref[...])

    return _pcall(
        body, name="merge_fwd", grid=(S // ts,),
        in_specs=[_tok(ts, D), _tok(ts, 3 * D), _tok(ts, 1024), _tok(ts, 1024), _tok(ts, 512),
                  _full(1024, D), _full(1024, D), _full(512, D), _full(D, D)],
        out_specs=[_tok(ts, D), _tok(ts, D)],
        out_shape=[_sds((S, D), F32), _sds((S, D), BF16)],
        sem=("arbitrary",),
    )(x, g, oa, ob, oc, wa, wb, wc, wout)


def _mlp_fwd(x1, mn, wup, wdown, ts):
    S = x1.shape[0]

    def body(x_ref, mn_ref, wup_ref, wdown_ref, x2_ref):
        xv = x_ref[...]
        n, _ = _rms(xv)
        u = _dot((n * mn_ref[...]).astype(BF16), wup_ref[...])
        a = jnp.square(jnp.maximum(u, 0.0))
        x2_ref[...] = xv + _dot(a.astype(BF16), wdown_ref[...])

    return _pcall(
        body, name="mlp_fwd", grid=(S // ts,),
        in_specs=[_tok(ts, D), _full(1, D), _full(D, DFF), _full(DFF, D)],
        out_specs=_tok(ts, D), out_shape=_sds((S, D), F32), sem=("arbitrary",),
    )(x1, mn, wup, wdown)


def _loss_kernel(x, fn, tgt, ts):
    S = x.shape[0]

    def body(x_ref, fn_ref, t_ref, loss_ref, dx_ref, dfn_ref):
        @pl.when(pl.program_id(0) == 0)
        def _():
            loss_ref[...] = jnp.zeros_like(loss_ref)
            dfn_ref[...] = jnp.zeros_like(dfn_ref)

        n, r = _rms(x_ref[...])
        err = n * fn_ref[...] - t_ref[...]
        loss_ref[...] += jnp.sum(err * err)
        dy = err * (1.0 / D)
        dfn_ref[...] += _colsum(dy * n)
        dx_ref[...] = _rms_bwd(dy * fn_ref[...], n, r)

    return _pcall(
        body, name="loss_head", grid=(S // ts,),
        in_specs=[_tok(ts, D), _full(1, D), _tok(ts, D)],
        out_specs=[_full(1, LANE), _tok(ts, D), _full(1, D)],
        out_shape=[_sds((1, LANE), F32), _sds((S, D), F32), _sds((1, D), F32)],
        sem=("arbitrary",),
    )(x, fn, tgt)


def _mlp_bwd(dx2, x1, mn, wup, wdown, ts):
    S = x1.shape[0]

    def body(dx_ref, x_ref, mn_ref, wup_ref, wdown_ref, dx1_ref, hb_ref, dub_ref, ab_ref, dxb_ref, dmn_ref):
        @pl.when(pl.program_id(0) == 0)
        def _():
            dmn_ref[...] = jnp.zeros_like(dmn_ref)

        dx = dx_ref[...]
        n, r = _rms(x_ref[...])
        g = mn_ref[...]
        hb = (n * g).astype(BF16)
        hb_ref[...] = hb
        rl = jnp.maximum(_dot(hb, wup_ref[...]), 0.0)
        ab_ref[...] = jnp.square(rl).astype(BF16)
        dxb = dx.astype(BF16)
        dxb_ref[...] = dxb
        dub = (_dot_nt(dxb, wdown_ref[...]) * (2.0 * rl)).astype(BF16)
        dub_ref[...] = dub
        dh = _dot_nt(dub, wup_ref[...])
        dmn_ref[...] += _colsum(dh * n)
        dx1_ref[...] = dx + _rms_bwd(dh * g, n, r)

    return _pcall(
        body, name="mlp_bwd", grid=(S // ts,),
        in_specs=[_tok(ts, D), _tok(ts, D), _full(1, D), _full(D, DFF), _full(DFF, D)],
        out_specs=[_tok(ts, D), _tok(ts, D), _tok(ts, DFF), _tok(ts, DFF), _tok(ts, D), _full(1, D)],
        out_shape=[_sds((S, D), F32), _sds((S, D), BF16), _sds((S, DFF), BF16), _sds((S, DFF), BF16),
                   _sds((S, D), BF16), _sds((1, D), F32)],
        sem=("arbitrary",),
    )(dx2, x1, mn, wup, wdown)


def _merge_bwd(dx1, g, oa, ob, oc, wa, wb, wc, wout, ts):
    S = dx1.shape[0]

    def body(dx_ref, g_ref, oa_ref, ob_ref, oc_ref, wa_ref, wb_ref, wc_ref, wout_ref,
             dgp_ref, dyo_ref, doa_ref, dob_ref, doc_ref, dla_ref, dxb_ref, dbg_ref):
        @pl.when(pl.program_id(0) == 0)
        def _():
            dbg_ref[...] = jnp.zeros_like(dbg_ref)

        dxb = dx_ref[...].astype(BF16)
        dxb_ref[...] = dxb
        dy = _dot_nt(dxb, wout_ref[...])
        branches = ((oa_ref, wa_ref, doa_ref), (ob_ref, wb_ref, dob_ref), (oc_ref, wc_ref, doc_ref))
        for b, (o_ref, w_ref, do_ref) in enumerate(branches):
            cols = slice(b * D, (b + 1) * D)
            gb = g_ref[:, cols]
            o = o_ref[...]
            dgpre = dy * _dot(o, w_ref[...]) * gb * (1.0 - gb)
            dgp_ref[:, cols] = dgpre.astype(BF16)
            dbg_ref[:, cols] += _colsum(dgpre)
            dyo = (dy * gb).astype(BF16)
            dyo_ref[:, cols] = dyo
            do = _dot_nt(dyo, w_ref[...])
            do_ref[...] = do.astype(BF16)
            if b == 0:
                lane = lax.broadcasted_iota(jnp.int32, (ts, LANE), 1)
                dls = jnp.zeros((ts, LANE), F32)
                for h in range(NH):
                    dl = jnp.sum(do[:, _hs(h)] * o[:, _hs(h)].astype(F32), axis=1, keepdims=True)
                    dls = jnp.where(lane == h, dl, dls)
                dla_ref[:, 0, :] = jnp.transpose(dls)[0:NH, :]

    return _pcall(
        body, name="merge_bwd", grid=(S // ts,),
        in_specs=[_tok(ts, D), _tok(ts, 3 * D), _tok(ts, 1024), _tok(ts, 1024), _tok(ts, 512),
                  _full(1024, D), _full(1024, D), _full(512, D), _full(D, D)],
        out_specs=[_tok(ts, 3 * D), _tok(ts, 3 * D), _tok(ts, 1024), _tok(ts, 1024), _tok(ts, 512),
                   pl.BlockSpec((NH, 1, ts), lambda i: (0, 0, i)), _tok(ts, D), _full(1, 3 * D)],
        out_shape=[_sds((S, 3 * D), BF16), _sds((S, 3 * D), BF16), _sds((S, 1024), BF16), _sds((S, 1024), BF16),
                   _sds((S, 512), BF16), _sds((NH, 1, S), F32), _sds((S, D), BF16), _sds((1, 3 * D), F32)],
        sem=("arbitrary",),
    )(dx1, g, oa, ob, oc, wa, wb, wc, wout)


def _pre_bwd(x, dx1, dq, dk, dv, dqs, dks, dvs, dqm, dgp, an, qn, kvn, win, wuq, wukv, rc, rs1, rs2, ts):
    S = x.shape[0]

    def body(x_ref, dx1_ref, dq_ref, dk_ref, dv_ref, dqs_ref, dks_ref, dvs_ref, dqm_ref, dgp_ref,
             an_ref, qn_ref, kvn_ref, win_ref, wuq_ref, wukv_ref, rc_ref, rs1_ref, rs2_ref,
             dx_ref, dproj_ref, hb_ref, cqn_ref, ckvn_ref, dqpre_ref, dkv_ref, dan_ref, dqn_ref, dkvn_ref):
        @pl.when(pl.program_id(0) == 0)
        def _():
            dan_ref[...] = jnp.zeros_like(dan_ref)
            dqn_ref[...] = jnp.zeros_like(dqn_ref)
            dkvn_ref[...] = jnp.zeros_like(dkvn_ref)

        n, r = _rms(x_ref[...])
        hb = (n * an_ref[...]).astype(BF16)
        hb_ref[...] = hb
        pa = _dot(hb, win_ref[:, C_CQ:C_KPE])
        ncq, rq = _rms(pa[:, 0:256])
        cqn_ref[...] = (ncq * qn_ref[...]).astype(BF16)
        nkv, rkv = _rms(pa[:, 256:384])
        ckvn_ref[...] = (nkv * kvn_ref[...]).astype(BF16)
        c, s1, s2 = rc_ref[...], rs1_ref[...], rs2_ref[...]

        dkper = jnp.zeros((ts, LANE), F32)
        for h in range(NH):
            dqpre_ref[:, _hs(h)] = _rope_bwd(dq_ref[:, _hs(h)], c, s1, s2).astype(BF16)
            dkh = dk_ref[:, _hs(h)]
            dkper = dkper + dkh
            dkv_ref[:, _hs(h)] = dkh.astype(BF16)
        dkv_ref[:, NH * LANE:] = dv_ref[...].astype(BF16)

        dcqn = _dot_nt(dqpre_ref[...], wuq_ref[...])
        dqn_ref[...] += _colsum(dcqn * ncq)
        dproj_ref[:, C_CQ:C_CKV] = _rms_bwd(dcqn * qn_ref[...], ncq, rq).astype(BF16)
        dckvn = _dot_nt(dkv_ref[...], wukv_ref[...])
        dkvn_ref[...] += _colsum(dckvn * nkv)
        dproj_ref[:, C_CKV:C_KPE] = _rms_bwd(dckvn * kvn_ref[...], nkv, rkv).astype(BF16)
        lane = lax.broadcasted_iota(jnp.int32, (ts, LANE), 1)
        dkpe = jnp.where((lane >= 64) & (lane < 96), _rope_bwd(dkper, c, s1, s2), 0.0)
        dproj_ref[:, C_KPE:C_QS] = dkpe.astype(BF16)
        dproj_ref[:, C_QS:C_KS] = dqs_ref[...]
        dproj_ref[:, C_KS:C_VS] = dks_ref[...].astype(BF16)
        dproj_ref[:, C_VS:C_QM] = dvs_ref[...].astype(BF16)
        dproj_ref[:, C_QM:C_G] = dqm_ref[...]
        dproj_ref[:, C_G:C_END] = dgp_ref[...]

        dh = _dot_nt(dproj_ref[...], win_ref[...])
        dan_ref[...] += _colsum(dh * n)
        dx_ref[...] = dx1_ref[...] + _rms_bwd(dh * an_ref[...], n, r)

    return _pcall(
        body, name="pre_bwd", grid=(S // ts,),
        in_specs=[_tok(ts, D), _tok(ts, D), _tok(ts, 1024), _tok(ts, 1024), _tok(ts, 1024), _tok(ts, 1024),
                  _tok(ts, 128), _tok(ts, 128), _tok(ts, 512), _tok(ts, 3 * D),
                  _full(1, D), _full(1, 256), _full(1, 128), _full(D, C_END), _full(256, NH * LANE),
                  _full(128, 2 * NH * LANE), _tok(ts, LANE), _tok(ts, LANE), _tok(ts, LANE)],
        out_specs=[_tok(ts, D), _tok(ts, C_END), _tok(ts, D), _tok(ts, 256), _tok(ts, 128), _tok(ts, 1024),
                   _tok(ts, 2048), _full(1, D), _full(1, 256), _full(1, 128)],
        out_shape=[_sds((S, D), F32), _sds((S, C_END), BF16), _sds((S, D), BF16), _sds((S, 256), BF16),
                   _sds((S, 128), BF16), _sds((S, 1024), BF16), _sds((S, 2048), BF16), _sds((1, D), F32),
                   _sds((1, 256), F32), _sds((1, 128), F32)],
        sem=("arbitrary",),
    )(x, dx1, dq, dk, dv, dqs, dks, dvs, dqm, dgp, an, qn, kvn, win, wuq, wukv, rc, rs1, rs2)


def _pick_tile(n, cap):
    best = LANE
    for t in range(LANE, min(n, cap) + 1, LANE):
        if n % t == 0:
            best = t
    return best


def _matmul_tn(a, b, name, shard_axis=None):
    S, M = a.shape
    N = b.shape[1]
    tm = _pick_tile(M // 4 if shard_axis == 0 else M, 1024)
    tn = _pick_tile(N // 4 if shard_axis == 1 else N, 2048)
    ts = min(S, 1024)
    nk = S // ts

    def body(a_ref, b_ref, o_ref, *acc):
        acc_ref = acc[0] if acc else o_ref

        @pl.when(pl.program_id(2) == 0)
        def _():
            acc_ref[...] = jnp.zeros_like(acc_ref)

        acc_ref[...] += _dot_tn(a_ref[...], b_ref[...])
        if acc:
            @pl.when(pl.program_id(2) == nk - 1)
            def _():
                o_ref[0] = acc_ref[...].astype(o_ref.dtype)

    if shard_axis is None:
        out_spec = pl.BlockSpec((tm, tn), lambda i, j, k: (i, j))
        out_shape, scratch = _sds((M, N), F32), ()
    elif shard_axis == 0:
        per = (M // 4) // tm
        out_spec = pl.BlockSpec((1, tm, tn), lambda i, j, k: (i // per, i % per, j))
        out_shape, scratch = _sds((4, M // 4, N), BF16), (pltpu.VMEM((tm, tn), F32),)
    else:
        per = (N // 4) // tn
        out_spec = pl.BlockSpec((1, tm, tn), lambda i, j, k: (j // per, i, j % per))
        out_shape, scratch = _sds((4, M, N // 4), BF16), (pltpu.VMEM((tm, tn), F32),)
    return _pcall(
        body, name=name, grid=(M // tm, N // tn, nk),
        in_specs=[pl.BlockSpec((ts, tm), lambda i, j, k: (k, i)), pl.BlockSpec((ts, tn), lambda i, j, k: (k, j))],
        out_specs=out_spec, out_shape=out_shape, scratch=scratch, sem=("parallel", "parallel", "arbitrary"),
    )(a, b)


def _select_fwd(a, sel, name):
    _, L, M, K = a.shape
    N = sel.shape[2]
    tn = _pick_tile(N, 1792)

    def body(a_ref, s_ref, o_ref, acc_ref):
        s = pl.program_id(2)

        @pl.when(s == 0)
        def _():
            acc_ref[...] = jnp.zeros_like(acc_ref)

        acc_ref[...] += _dot(a_ref[0, 0], s_ref[0])

        @pl.when(s == 3)
        def _():
            o_ref[0] = acc_ref[...].astype(BF16)

    return _pcall(
        body, name=name, grid=(L, N // tn, 4),
        in_specs=[pl.BlockSpec((1, 1, M, K), lambda l, j, s: (s, l, 0, 0)),
                  pl.BlockSpec((1, K, tn), lambda l, j, s: (s, 0, j))],
        out_specs=pl.BlockSpec((1, M, tn), lambda l, j, s: (l, 0, j)),
        out_shape=_sds((L, M, N), BF16), scratch=(pltpu.VMEM((M, tn), F32),),
        sem=("parallel", "parallel", "arbitrary"),
    )(a, sel)


def _select_bwd(dw, sel, name):
    L, M, N = dw.shape
    K = sel.shape[1]
    tk = _pick_tile(N, 1792)
    nk = N // tk

    def body(d_ref, s_ref, o_ref, acc_ref):
        k = pl.program_id(2)

        @pl.when(k == 0)
        def _():
            acc_ref[...] = jnp.zeros_like(acc_ref)

        acc_ref[...] += _dot_nt(d_ref[0].astype(BF16), s_ref[0])

        @pl.when(k == nk - 1)
        def _():
            o_ref[0, 0] = acc_ref[...].astype(BF16)

    return _pcall(
        body, name=name, grid=(L, 4, nk),
        in_specs=[pl.BlockSpec((1, M, tk), lambda l, s, k: (l, 0, k)),
                  pl.BlockSpec((1, K, tk), lambda l, s, k: (s, 0, k))],
        out_specs=pl.BlockSpec((1, 1, M, K), lambda l, s, k: (l, s, 0, 0)),
        out_shape=_sds((L, 4, M, K), BF16), scratch=(pltpu.VMEM((M, K), F32),),
        sem=("parallel", "parallel", "arbitrary"),
    )(dw, sel)


MLA_RC = 128
LOG2E = math.log2(math.e)
MLA_QSCALE = MLA_SCALE * LOG2E


def _mla_fwd(q, k, v, tq):
    S = q.shape[0]
    nq = S // tq
    pairs = [(i, j) for i in range(nq) for j in range(i + 1)]
    qi = jnp.asarray(np.array([p[0] for p in pairs], np.int32))
    kj = jnp.asarray(np.array([p[1] for p in pairs], np.int32))

    def body(qi_ref, kj_ref, q_ref, k_ref, v_ref, o_ref, lse_ref, m_s, l_s, acc_s):
        t = pl.program_id(1)
        i, j = qi_ref[t], kj_ref[t]

        @pl.when(j == 0)
        def _():
            m_s[...] = jnp.full_like(m_s, NEG)
            l_s[...] = jnp.zeros_like(l_s)
            acc_s[...] = jnp.zeros_like(acc_s)

        def step(masked):
            nc = tq // MLA_RC
            keys = [(c + 1) * MLA_RC if masked else tq for c in range(nc)]
            scores = [_dot_nt(q_ref[c * MLA_RC:(c + 1) * MLA_RC, :], k_ref[0:keys[c], :]) for c in range(nc)]
            for c in range(nc):
                rows = slice(c * MLA_RC, (c + 1) * MLA_RC)
                s = scores[c]
                if masked:
                    row = lax.broadcasted_iota(jnp.int32, (MLA_RC, keys[c]), 0) + c * MLA_RC
                    col = lax.broadcasted_iota(jnp.int32, (MLA_RC, keys[c]), 1)
                    s = jnp.where(col <= row, s, NEG)
                tiles = [s[:, _hs(u)] for u in range(keys[c] // LANE)]
                mx = functools.reduce(jnp.maximum, tiles)
                m_old = m_s[rows, :]
                m_new = jnp.maximum(m_old, jnp.max(mx, axis=1, keepdims=True))
                alpha = jnp.exp2(m_old - m_new)
                ps = [jnp.exp2(u - m_new) for u in tiles]
                l_s[rows, :] = alpha * l_s[rows, :] + functools.reduce(jnp.add, ps)
                p = jnp.concatenate([u.astype(BF16) for u in ps], axis=1)
                acc_s[rows, :] = alpha * acc_s[rows, :] + _dot(p, v_ref[0:keys[c], :])
                m_s[rows, :] = m_new

        @pl.when(j < i)
        def _():
            step(False)

        @pl.when(j == i)
        def _():
            step(True)
            l = jnp.sum(l_s[...], axis=1, keepdims=True)
            o_ref[...] = (acc_s[...] / l).astype(BF16)
            lse_ref[0] = jnp.transpose(m_s[...] + jnp.log2(l))[0:1, :]

    qmap = lambda h, t, qi_r, kj_r: (qi_r[t], h)
    kmap = lambda h, t, qi_r, kj_r: (kj_r[t], h)
    return _pcall(
        body, name="mla_fwd", grid=(NH, len(pairs)), prefetch=2,
        in_specs=[pl.BlockSpec((tq, LANE), qmap), pl.BlockSpec((tq, LANE), kmap), pl.BlockSpec((tq, LANE), kmap)],
        out_specs=[pl.BlockSpec((tq, LANE), qmap),
                   pl.BlockSpec((1, 1, tq), lambda h, t, qi_r, kj_r: (h, 0, qi_r[t]))],
        out_shape=[_sds((S, NH * LANE), BF16), _sds((NH, 1, S), F32)],
        scratch=[pltpu.VMEM((tq, LANE), F32), pltpu.VMEM((tq, LANE), F32), pltpu.VMEM((tq, LANE), F32)],
        sem=("arbitrary", "arbitrary"),
    )(qi, kj, q, k, v)


def _mla_bwd(q, k, v, do, lse, delta, tq):
    S = q.shape[0]
    nq = S // tq
    pairs = [(i, j) for j in range(nq) for i in range(j, nq)]
    qi = jnp.asarray(np.array([p[0] for p in pairs], np.int32))
    kj = jnp.asarray(np.array([p[1] for p in pairs], np.int32))

    def body(qi_ref, kj_ref, q_ref, k_ref, v_ref, do_ref, lse_ref, dl_ref, dq_ref, dk_ref, dv_ref,
             dq_s, dk_s, dv_s):
        t = pl.program_id(1)
        i, j = qi_ref[t], kj_ref[t]

        @pl.when(t == 0)
        def _():
            dq_s[...] = jnp.zeros_like(dq_s)

        @pl.when(i == j)
        def _():
            dk_s[...] = jnp.zeros_like(dk_s)
            dv_s[...] = jnp.zeros_like(dv_s)

        qrows = pl.ds(pl.multiple_of(i * tq, tq), tq)

        def step(masked):
            lse_r, dl_r = lse_ref[0], dl_ref[0]
            dq = jnp.zeros((tq, LANE), F32)
            nc = tq // MLA_RC
            q0 = [c * MLA_RC if masked else 0 for c in range(nc)]
            sts = [_dot_nt(k_ref[c * MLA_RC:(c + 1) * MLA_RC, :], q_ref[q0[c]:, :]) for c in range(nc)]
            dpts = [_dot_nt(v_ref[c * MLA_RC:(c + 1) * MLA_RC, :], do_ref[q0[c]:, :]) for c in range(nc)]
            for c in range(nc):
                rows = slice(c * MLA_RC, (c + 1) * MLA_RC)
                qb, dob = q_ref[q0[c]:, :], do_ref[q0[c]:, :]
                pt = jnp.exp2(sts[c] - lse_r[:, q0[c]:])
                if masked:
                    key = lax.broadcasted_iota(jnp.int32, (MLA_RC, tq - q0[c]), 0)
                    qry = lax.broadcasted_iota(jnp.int32, (MLA_RC, tq - q0[c]), 1)
                    pt = jnp.where(key <= qry, pt, 0.0)
                dv_s[rows, :] += _dot(pt.astype(BF16), dob)
                gt = (pt * (dpts[c] - dl_r[:, q0[c]:])).astype(BF16)
                dk_s[rows, :] += _dot(gt, qb)
                part = _dot_tn(gt, k_ref[rows, :])
                if masked:
                    at = pl.multiple_of(i * tq + q0[c], MLA_RC)
                    dq_s[pl.ds(at, tq - q0[c]), :] += part
                else:
                    dq = dq + part
            if not masked:
                dq_s[qrows, :] += dq

        @pl.when(i > j)
        def _():
            step(False)

        @pl.when(i == j)
        def _():
            step(True)
            dq_ref[...] = dq_s[qrows, :] * MLA_SCALE

        @pl.when(i == nq - 1)
        def _():
            dk_ref[...] = dk_s[...] * (1.0 / LOG2E)
            dv_ref[...] = dv_s[...]

    qmap = lambda h, t, qi_r, kj_r: (qi_r[t], h)
    kmap = lambda h, t, qi_r, kj_r: (kj_r[t], h)
    rmap = lambda h, t, qi_r, kj_r: (h, 0, qi_r[t])
    return _pcall(
        body, name="mla_bwd", grid=(NH, len(pairs)), prefetch=2,
        in_specs=[pl.BlockSpec((tq, LANE), qmap), pl.BlockSpec((tq, LANE), kmap), pl.BlockSpec((tq, LANE), kmap),
                  pl.BlockSpec((tq, LANE), qmap), pl.BlockSpec((1, 1, tq), rmap), pl.BlockSpec((1, 1, tq), rmap)],
        out_specs=[pl.BlockSpec((tq, LANE), kmap), pl.BlockSpec((tq, LANE), kmap), pl.BlockSpec((tq, LANE), kmap)],
        out_shape=[_sds((S, NH * LANE), F32), _sds((S, NH * LANE), F32), _sds((S, NH * LANE), F32)],
        scratch=[pltpu.VMEM((S, LANE), F32), pltpu.VMEM((tq, LANE), F32), pltpu.VMEM((tq, LANE), F32)],
        sem=("arbitrary", "arbitrary"),
    )(qi, kj, q, k, v, do, lse, delta)


SWA_SUB = 4
SWA_T = SWA_SUB * WIN


def _swa_specs(nsteps, rev):
    step = (lambda i: nsteps - 1 - i) if rev else (lambda i: i)
    cur = lambda w: pl.BlockSpec((SWA_T, w), lambda i: (step(i), 0))
    prev = pl.BlockSpec((WIN, LANE), lambda i: (jnp.maximum(step(i) * SWA_SUB - 1, 0), 0))
    return step, cur, prev


def _swa_probs(qk, bias_h, sink, first_mask):
    s = qk * SWA_SCALE + bias_h
    if first_mask is not None:
        s = jnp.where(first_mask, NEG, s)
    m = jnp.maximum(jnp.max(s, axis=1, keepdims=True), sink)
    e = jnp.exp(s - m)
    es = jnp.exp(sink - m)
    inv = 1.0 / (jnp.sum(e, axis=1, keepdims=True) + es)
    return e * inv, es * inv


def _swa_fwd(qs, ks, vs, bias, sinks):
    S = qs.shape[0]
    nsteps = S // SWA_T
    step, cur, prev = _swa_specs(nsteps, False)

    def body(qs_ref, kc_ref, kp_ref, vc_ref, vp_ref, bias_ref, sk_ref, o_ref):
        first = pl.program_id(0) == 0
        kk = jnp.concatenate([kp_ref[...], kc_ref[...]], axis=0)
        vv = jnp.concatenate([vp_ref[...], vc_ref[...]], axis=0)
        col = lax.broadcasted_iota(jnp.int32, (WIN, 2 * WIN), 1)
        for b in range(SWA_SUB):
            kkb = kk[b * WIN:(b + 2) * WIN]
            vvb = vv[b * WIN:(b + 2) * WIN]
            fm = (first & (col < WIN)) if b == 0 else None
            rows = slice(b * WIN, (b + 1) * WIN)
            qks = [_dot_nt(qs_ref[rows, _hs(h)], kkb) for h in range(NH)]
            for h in range(NH):
                p, _ = _swa_probs(qks[h], bias_ref[h], sk_ref[h:h + 1, 0:1], fm)
                o_ref[rows, _hs(h)] = _dot(p.astype(BF16), vvb).astype(BF16)

    return _pcall(
        body, name="swa_fwd", grid=(nsteps,),
        in_specs=[cur(NH * LANE), cur(LANE), prev, cur(LANE), prev, _full(NH, WIN, 2 * WIN), _full(NH, LANE)],
        out_specs=cur(NH * LANE), out_shape=_sds((S, NH * LANE), BF16), sem=("arbitrary",),
    )(qs, ks, ks, vs, vs, bias, sinks)


def _swa_bwd(qs, ks, vs, do, bias, sinks):
    S = qs.shape[0]
    nsteps = S // SWA_T
    step, cur, prev = _swa_specs(nsteps, True)

    def body(qs_ref, kc_ref, kp_ref, vc_ref, vp_ref, do_ref, bias_ref, sk_ref,
             dqs_ref, dks_ref, dvs_ref, dbias_ref, dsk_ref, dkk_s, dvv_s, ck_s, cv_s):
        pid = pl.program_id(0)
        first = step(pid) == 0

        @pl.when(pid == 0)
        def _():
            dbias_ref[...] = jnp.zeros_like(dbias_ref)
            dsk_ref[...] = jnp.zeros_like(dsk_ref)
            ck_s[...] = jnp.zeros_like(ck_s)
            cv_s[...] = jnp.zeros_like(cv_s)

        dkk_s[...] = jnp.zeros_like(dkk_s)
        dvv_s[...] = jnp.zeros_like(dvv_s)
        kk = jnp.concatenate([kp_ref[...], kc_ref[...]], axis=0)
        vv = jnp.concatenate([vp_ref[...], vc_ref[...]], axis=0)
        col = lax.broadcasted_iota(jnp.int32, (WIN, 2 * WIN), 1)
        for b in range(SWA_SUB):
            kkb = kk[b * WIN:(b + 2) * WIN]
            vvb = vv[b * WIN:(b + 2) * WIN]
            fm = (first & (col < WIN)) if b == 0 else None
            rows = slice(b * WIN, (b + 1) * WIN)
            keys = slice(b * WIN, (b + 2) * WIN)
            qks = [_dot_nt(qs_ref[rows, _hs(h)], kkb) for h in range(NH)]
            dps = [_dot_nt(do_ref[rows, _hs(h)], vvb) for h in range(NH)]
            for h in range(NH):
                qh = qs_ref[rows, _hs(h)]
                doh = do_ref[rows, _hs(h)]
                p, ps = _swa_probs(qks[h], bias_ref[h], sk_ref[h:h + 1, 0:1], fm)
                dp = dps[h]
                dl = jnp.sum(p * dp, axis=1, keepdims=True)
                ds = p * (dp - dl)
                dsk_ref[h:h + 1, :] += -jnp.sum(ps * dl)
                dbias_ref[h] += ds
                dsb = (ds * SWA_SCALE).astype(BF16)
                dqs_ref[rows, _hs(h)] = _dot(dsb, kkb).astype(BF16)
                dkk_s[keys, :] += _dot_tn(dsb, qh)
                dvv_s[keys, :] += _dot_tn(p.astype(BF16), doh)
        dks_ref[...] = dkk_s[WIN:, :]
        dvs_ref[...] = dvv_s[WIN:, :]
        dks_ref[SWA_T - WIN:, :] += ck_s[...]
        dvs_ref[SWA_T - WIN:, :] += cv_s[...]
        ck_s[...] = dkk_s[0:WIN, :]
        cv_s[...] = dvv_s[0:WIN, :]

    return _pcall(
        body, name="swa_bwd", grid=(nsteps,),
        in_specs=[cur(NH * LANE), cur(LANE), prev, cur(LANE), prev, cur(NH * LANE), _full(NH, WIN, 2 * WIN),
                  _full(NH, LANE)],
        out_specs=[cur(NH * LANE), cur(LANE), cur(LANE), _full(NH, WIN, 2 * WIN), _full(NH, LANE)],
        out_shape=[_sds((S, NH * LANE), BF16), _sds((S, LANE), F32), _sds((S, LANE), F32),
                   _sds((NH, WIN, 2 * WIN), F32), _sds((NH, LANE), F32)],
        scratch=[pltpu.VMEM((SWA_T + WIN, LANE), F32), pltpu.VMEM((SWA_T + WIN, LANE), F32),
                 pltpu.VMEM((WIN, LANE), F32), pltpu.VMEM((WIN, LANE), F32)],
        sem=("arbitrary",),
    )(qs, ks, ks, vs, vs, do, bias, sinks)


def _bias_build(rel_bias, bmap):
    def body(rb_ref, bmap_ref, o_ref):
        bm = bmap_ref[...]
        for h in range(NH):
            acc = jnp.full((WIN, 2 * WIN), NEG, F32)
            for b in range(REL_BUCKETS):
                acc = jnp.where(bm == b, rb_ref[b, h], acc)
            o_ref[h] = acc

    return _pcall(
        body, name="bias_build", grid=(1,),
        in_specs=[pl.BlockSpec(memory_space=pltpu.SMEM), _full(WIN, 2 * WIN)],
        out_specs=_full(NH, WIN, 2 * WIN), out_shape=_sds((NH, WIN, 2 * WIN), F32), sem=("arbitrary",),
    )(rel_bias, bmap)


def _bias_reduce(dbias, bmap):
    def body(db_ref, bmap_ref, o_ref):
        bm = bmap_ref[...]
        for h in range(NH):
            dbh = db_ref[h]
            for b in range(REL_BUCKETS):
                o_ref[b, h] = jnp.sum(jnp.where(bm == b, dbh, 0.0))

    return _pcall(
        body, name="bias_reduce", grid=(1,),
        in_specs=[_full(NH, WIN, 2 * WIN), _full(WIN, 2 * WIN)],
        out_specs=pl.BlockSpec(memory_space=pltpu.SMEM), out_shape=_sds((REL_BUCKETS, NH), F32), sem=("arbitrary",),
    )(dbias, bmap)


def _memkv_fwd(mem, mnorm, wkv):
    def body(mem_ref, g_ref, w_ref, o_ref):
        n, _ = _rms(mem_ref[...])
        o_ref[...] = _dot((n * g_ref[...]).astype(BF16), w_ref[...]).astype(BF16)

    return _pcall(
        body, name="memkv_fwd", grid=(1,), in_specs=[_full(MEM_LEN, D), _full(1, D), _full(D, D)],
        out_specs=_full(MEM_LEN, D), out_shape=_sds((MEM_LEN, D), BF16), sem=("arbitrary",),
    )(mem, mnorm, wkv)


def _mem_probs(qk):
    s = qk * MEM_SCALE
    e = jnp.exp(s - jnp.max(s, axis=1, keepdims=True))
    return e / jnp.sum(e, axis=1, keepdims=True)


def _mem_fwd(qm, kvm, ts):
    S = qm.shape[0]

    def body(q_ref, kv_ref, o_ref):
        qks = [_dot_nt(q_ref[:, _hs(h)], kv_ref[:, _hs(h)]) for h in range(MEM_H)]
        for h in range(MEM_H):
            p = _mem_probs(qks[h])
            o_ref[:, _hs(h)] = _dot(p.astype(BF16), kv_ref[:, _hs(MEM_H + h)]).astype(BF16)

    return _pcall(
        body, name="mem_fwd", grid=(S // ts,), in_specs=[_tok(ts, 512), _full(MEM_LEN, D)],
        out_specs=_tok(ts, 512), out_shape=_sds((S, 512), BF16), sem=("arbitrary",),
    )(qm, kvm)


def _mem_bwd(qm, kvm, do, ts):
    S = qm.shape[0]

    def body(q_ref, kv_ref, do_ref, dq_ref, dkv_ref):
        @pl.when(pl.program_id(0) == 0)
        def _():
            dkv_ref[...] = jnp.zeros_like(dkv_ref)

        qks = [_dot_nt(q_ref[:, _hs(h)], kv_ref[:, _hs(h)]) for h in range(MEM_H)]
        dps = [_dot_nt(do_ref[:, _hs(h)], kv_ref[:, _hs(MEM_H + h)]) for h in range(MEM_H)]
        for h in range(MEM_H):
            qh, kh, doh = q_ref[:, _hs(h)], kv_ref[:, _hs(h)], do_ref[:, _hs(h)]
            p = _mem_probs(qks[h])
            dp = dps[h]
            ds = (p * (dp - jnp.sum(p * dp, axis=1, keepdims=True)) * MEM_SCALE).astype(BF16)
            dq_ref[:, _hs(h)] = _dot(ds, kh).astype(BF16)
            dkv_ref[:, _hs(h)] += _dot_tn(ds, qh)
            dkv_ref[:, _hs(MEM_H + h)] += _dot_tn(p.astype(BF16), doh)

    return _pcall(
        body, name="mem_bwd", grid=(S // ts,), in_specs=[_tok(ts, 512), _full(MEM_LEN, D), _tok(ts, 512)],
        out_specs=[_tok(ts, 512), _full(MEM_LEN, D)],
        out_shape=[_sds((S, 512), BF16), _sds((MEM_LEN, D), F32)], sem=("arbitrary",),
    )(qm, kvm, do)


def _memkv_bwd(mem, mnorm, wkv, dkvm):
    def body(mem_ref, g_ref, w_ref, dkv_ref, dw_ref, dg_ref):
        n, _ = _rms(mem_ref[...])
        dkvb = dkv_ref[...].astype(BF16)
        dw_ref[...] = _dot_tn((n * g_ref[...]).astype(BF16), dkvb)
        dg_ref[...] = _colsum(_dot_nt(dkvb, w_ref[...]) * n)

    return _pcall(
        body, name="memkv_bwd", grid=(1,), in_specs=[_full(MEM_LEN, D), _full(1, D), _full(D, D), _full(MEM_LEN, D)],
        out_specs=[_full(D, D), _full(1, D)], out_shape=[_sds((D, D), F32), _sds((1, D), F32)], sem=("arbitrary",),
    )(mem, mnorm, wkv, dkvm)


def _adamw(w, g, m, v, name):
    rows, cols = w.shape
    tr = min(rows, FLAT_TILE)
    assert rows % tr == 0

    def body(w_ref, g_ref, m_ref, v_ref, d_ref, nm_ref, nv_ref):
        gv = g_ref[...]
        nm = ADAM_B1 * m_ref[...] + (1.0 - ADAM_B1) * gv
        nv = ADAM_B2 * v_ref[...] + (1.0 - ADAM_B2) * jnp.square(gv)
        m_hat = nm / (1.0 - ADAM_B1 ** ADAM_STEP)
        v_hat = nv / (1.0 - ADAM_B2 ** ADAM_STEP)
        d_ref[...] = -ADAM_LR * (m_hat / (jnp.sqrt(v_hat) + ADAM_EPS) + ADAM_WD * w_ref[...])
        nm_ref[...] = nm
        nv_ref[...] = nv

    spec = _tok(tr, cols)
    return _pcall(
        body, name=name, grid=(rows // tr,), in_specs=[spec] * 4, out_specs=[spec] * 3,
        out_shape=[_sds((rows, cols), F32)] * 3, sem=("arbitrary",),
    )(w, g, m, v)


def _my_place():
    return lax.axis_index("x"), lax.axis_index("y"), lax.axis_index("c")


def _remote(src, dst, send_sems, recv_sems, k, to):
    return pltpu.make_async_remote_copy(src_ref=src, dst_ref=dst, send_sem=send_sems.at[k], recv_sem=recv_sems.at[k],
                                        device_id=to, device_id_type=MESH_ID)


def _gather_weights(shards):
    n = len(shards)

    def body(*refs):
        ins, outs = refs[:n], refs[n:2 * n]
        send_sems, recv_sems, local_sems = refs[2 * n:]
        x, y, c = _my_place()
        chips = [(1 - x, y), (x, 1 - y), (1 - x, 1 - y)]
        mine = [pltpu.make_async_copy(ins[i], outs[i].at[2 * x + y], local_sems.at[i]) for i in range(n)]
        for cp in mine:
            cp.start()

        def copy(i, k, slot, to):
            return _remote(ins[i], outs[i].at[slot], send_sems, recv_sems, 3 * i + k, to)

        sends = [copy(i, k, 2 * x + y, (px, py, c)) for i in range(n) for k, (px, py) in enumerate(chips)]
        for cp in sends:
            cp.start()
        for i in range(n):
            for k, (px, py) in enumerate(chips):
                copy(i, k, 2 * px + py, (px, py, c)).wait_recv()
        for cp in sends:
            cp.wait_send()
        for cp in mine:
            cp.wait()

    hbm = pl.BlockSpec(memory_space=pl.ANY)
    return pl.pallas_call(
        body, name="gather_weights", out_shape=[_sds((4,) + s.shape, s.dtype) for s in shards],
        in_specs=[hbm] * n, out_specs=[hbm] * n,
        scratch_shapes=[pltpu.SemaphoreType.DMA((3 * n,)), pltpu.SemaphoreType.DMA((3 * n,)),
                        pltpu.SemaphoreType.DMA((n,))],
    )(*shards)


def _scatter_grads(gws, gsmall):
    n = len(gws)

    def body(*refs):
        ins, gs_ref = refs[:n], refs[n]
        outs, rs_ref = refs[n + 1:2 * n + 1], refs[2 * n + 1]
        send_sems, recv_sems, local_sems = refs[2 * n + 2:]
        x, y, c = _my_place()
        me = 4 * x + 2 * y + c
        mine = [pltpu.make_async_copy(ins[i].at[c, 2 * x + y], outs[i].at[me], local_sems.at[i]) for i in range(n)]
        mine.append(pltpu.make_async_copy(gs_ref, rs_ref.at[me], local_sems.at[n]))
        for cp in mine:
            cp.start()

        def peer(m):
            return (x ^ ((m >> 2) & 1), y ^ ((m >> 1) & 1), c ^ (m & 1))

        def big(i, m, layer, shard, slot, to):
            return _remote(ins[i].at[layer, shard], outs[i].at[slot], send_sems, recv_sems, (m - 1) * (n + 1) + i, to)

        def small(m, slot, to):
            return _remote(gs_ref, rs_ref.at[slot], send_sems, recv_sems, (m - 1) * (n + 1) + n, to)

        sends = []
        for m in range(1, 8):
            px, py, pc = peer(m)
            sends += [big(i, m, pc, 2 * px + py, me, (px, py, pc)) for i in range(n)]
            sends.append(small(m, me, (px, py, pc)))
        for cp in sends:
            cp.start()
        for m in range(1, 8):
            px, py, pc = peer(m)
            them = 4 * px + 2 * py + pc
            for i in range(n):
                big(i, m, c, 2 * x + y, them, (px, py, pc)).wait_recv()
            small(m, them, (px, py, pc)).wait_recv()
        for cp in sends:
            cp.wait_send()
        for cp in mine:
            cp.wait()

    hbm = pl.BlockSpec(memory_space=pl.ANY)
    nsem = 7 * (n + 1)
    return pl.pallas_call(
        body, name="scatter_grads",
        out_shape=[_sds((8,) + g.shape[2:], g.dtype) for g in gws] + [_sds((8,) + gsmall.shape, gsmall.dtype)],
        in_specs=[hbm] * (n + 1), out_specs=[hbm] * (n + 1),
        scratch_shapes=[pltpu.SemaphoreType.DMA((nsem,)), pltpu.SemaphoreType.DMA((nsem,)),
                        pltpu.SemaphoreType.DMA((n + 1,))],
    )(*gws, gsmall)


def _sum_slots(slots, name):
    _, r, c = slots.shape
    tr = min(r, FLAT_TILE)
    assert r % tr == 0

    def body(s_ref, o_ref):
        acc = s_ref[0].astype(F32)
        for d in range(1, 8):
            acc = acc + s_ref[d].astype(F32)
        o_ref[...] = acc

    return _pcall(
        body, name=name, grid=(r // tr,), in_specs=[pl.BlockSpec((8, tr, c), lambda i: (0, i, 0))],
        out_specs=_tok(tr, c), out_shape=_sds((r, c), F32), sem=("arbitrary",),
    )(slots)


def _swap_layers(reds):
    n = len(reds)
    tiles = [min(r.shape[0], FLAT_TILE) for r in reds]

    def body(*refs):
        ins, outs, bufs = refs[:n], refs[n:2 * n], refs[2 * n:3 * n]
        load_sems, send_sems, recv_sems, local_sems = refs[3 * n:]
        x, y, c = _my_place()
        sibling = (x, y, 1 - c)
        mine = [pltpu.make_async_copy(ins[i], outs[i].at[c], local_sems.at[i]) for i in range(n)]
        for cp in mine:
            cp.start()
        in_flight = []
        for i in range(n):
            tr = tiles[i]
            busy = {}
            for t in range(reds[i].shape[0] // tr):
                slot, rows = t % 2, pl.ds(t * tr, tr)
                if slot in busy:
                    busy[slot].wait_send()
                load = pltpu.make_async_copy(ins[i].at[rows], bufs[i].at[slot], load_sems.at[slot])
                load.start()
                load.wait()
                busy[slot] = pltpu.make_async_remote_copy(
                    src_ref=bufs[i].at[slot], dst_ref=outs[i].at[c, rows], send_sem=send_sems.at[2 * i + slot],
                    recv_sem=recv_sems.at[i], device_id=sibling, device_id_type=MESH_ID)
                busy[slot].start()
            in_flight += list(busy.values())
        for i in range(n):
            _remote(ins[i], outs[i].at[1 - c], send_sems, recv_sems, i, sibling).wait_recv()
        for cp in in_flight:
            cp.wait_send()
        for cp in mine:
            cp.wait()

    hbm = pl.BlockSpec(memory_space=pl.ANY)
    return pl.pallas_call(
        body, name="swap_layers", out_shape=[_sds((2,) + r.shape, r.dtype) for r in reds],
        in_specs=[hbm] * n, out_specs=[hbm] * n,
        scratch_shapes=[pltpu.VMEM((2, tiles[i]) + reds[i].shape[1:], reds[i].dtype) for i in range(n)]
        + [pltpu.SemaphoreType.DMA((2,)), pltpu.SemaphoreType.DMA((2 * n,)), pltpu.SemaphoreType.DMA((n,)),
           pltpu.SemaphoreType.DMA((n,))],
        compiler_params=pltpu.CompilerParams(vmem_limit_bytes=VMEM_LIMIT),
    )(*reds)


W_IN_SHARD, W_IN_SHARD_PAD = 1192, 1280
W_UQ_SHARD, W_UQ_SHARD_PAD = 192, 256


def _swa_place(t):
    z = jnp.zeros_like(t)
    lo = jnp.concatenate([t, z], axis=1)
    hi = jnp.concatenate([z, t], axis=1)
    group = (jnp.arange(NH) // SWA_R).reshape((NH,) + (1,) * (t.ndim - 1))
    full = jnp.where(group == 0, lo, hi)
    return full.reshape((NH * LANE,) + t.shape[2:])


def _swa_unplace(t):
    t = t.reshape((NH, 2, 64) + t.shape[1:])
    return jnp.concatenate([t[:SWA_R, 0], t[SWA_R:, 1]], axis=0)


def _pad_w_o_mla(w):
    return jnp.pad(w.reshape(NH, 64, D), ((0, 0), (0, 64), (0, 0))).reshape(NH * LANE, D)


def _unpad_w_o_mla(g):
    return g.reshape(NH, LANE, D)[:, :64].reshape(NH * 64, D)


def _w_in_cols():
    src = np.full((C_END,), -1, np.int64)
    src[C_CQ:C_KPE] = np.arange(0, 384)
    src[C_KPE + 64:C_KPE + 96] = np.arange(384, 416)
    for h in range(NH):
        at = C_QS + h * LANE + 64 * (h // SWA_R)
        src[at:at + 64] = 416 + h * 64 + np.arange(64)
    src[C_KS:C_END] = np.arange(928, IN_COLS)
    return src


def _w_uq_cols():
    src = np.full((NH * LANE,), -1, np.int64)
    for h in range(NH):
        src[h * LANE:h * LANE + 96] = h * 96 + np.arange(96)
    return src


def _w_ukv_cols():
    src = np.full((2 * NH * LANE,), -1, np.int64)
    for h in range(NH):
        src[h * LANE:h * LANE + 64] = h * 128 + np.arange(64)
        src[NH * LANE + h * LANE:NH * LANE + h * LANE + 64] = h * 128 + 64 + np.arange(64)
    return src


def _selection(src_cols, width, width_pad):
    want = jnp.asarray(np.asarray(src_cols, np.int32))[None, None, :]
    k = jnp.arange(width_pad, dtype=jnp.int32)[None, :, None]
    have = jnp.where(k < width, jnp.arange(4, dtype=jnp.int32)[:, None, None] * width + k, -2)
    return (want == have).astype(BF16)


def _selections():
    return dict(w_in=_selection(_w_in_cols(), W_IN_SHARD, W_IN_SHARD_PAD),
                w_uq=_selection(_w_uq_cols(), W_UQ_SHARD, W_UQ_SHARD_PAD),
                w_ukv=_selection(_w_ukv_cols(), 256, 256))


def _pad_last(a, width):
    return jnp.pad(a, ((0, 0),) * (a.ndim - 1) + ((0, width - a.shape[-1]),))


def _wire_shards(W):
    out = {n: W[n].astype(BF16) for n in SHARDED}
    out["w_in"] = _pad_last(out["w_in"], W_IN_SHARD_PAD)
    out["w_uq"] = _pad_last(out["w_uq"], W_UQ_SHARD_PAD)
    return out


def _join_shards(t, axis):
    _, L, r, c = t.shape
    if axis == 2:
        return t.transpose(1, 2, 0, 3).reshape(L, r, 4 * c)
    return t.transpose(1, 0, 2, 3).reshape(L, 4 * r, c)


def _kernel_weights(gathered, sels):
    win = _select_fwd(gathered["w_in"], sels["w_in"], "lay_w_in")
    wuq = _select_fwd(gathered["w_uq"], sels["w_uq"], "lay_w_uq")
    wukv = _select_fwd(gathered["w_ukv"], sels["w_ukv"], "lay_w_ukv")
    whole = {n: _join_shards(gathered[n], SHARD_AXIS[n])
             for n in ("w_mem_kv", "w_o_mla", "w_o_swa", "w_o_mem", "w_out", "w_up", "w_down")}
    return [dict(win=win[l], wuq=wuq[l], wukv=wukv[l], wmem=whole["w_mem_kv"][l],
                 wa=_pad_w_o_mla(whole["w_o_mla"][l]), wb=_swa_place(whole["w_o_swa"][l].reshape(NH, 64, D)),
                 wc=whole["w_o_mem"][l], wout=whole["w_out"][l], wup=whole["w_up"][l], wdown=whole["w_down"][l])
            for l in range(DEPTH)]


def _cols_to_shards(g):
    r, c4 = g.shape
    return g.reshape(r, 4, c4 // 4).transpose(1, 0, 2).astype(BF16)


def _rope_tables(S):
    pos = jnp.arange(S, dtype=F32)
    inv = 1.0 / (ROPE_THETA ** (jnp.arange(0, 32, 2, dtype=F32) / 32))
    ang = pos[:, None] * inv[None, :]
    cos, sin = jnp.cos(ang), jnp.sin(ang)
    one, zero = jnp.ones((S, 64), F32), jnp.zeros((S, 16), F32)
    rc = jnp.concatenate([one, cos, cos, jnp.ones((S, 32), F32)], axis=1)
    rs1 = jnp.concatenate([jnp.zeros((S, 64), F32), zero, sin, jnp.zeros((S, 32), F32)], axis=1)
    rs2 = jnp.concatenate([jnp.zeros((S, 64), F32), -sin, zero, jnp.zeros((S, 32), F32)], axis=1)
    return rc, rs1, rs2


def _bucket_map():
    qi = jnp.arange(WIN)[:, None]
    kj = jnp.arange(2 * WIN)[None, :]
    dist = qi + WIN - kj
    n = jnp.maximum(dist, 0)
    max_exact = REL_BUCKETS // 2
    nf = jnp.maximum(n, 1).astype(F32)
    large = max_exact + (jnp.log(nf / max_exact) / math.log(128 / max_exact)
                         * (REL_BUCKETS - max_exact)).astype(jnp.int32)
    large = jnp.minimum(large, REL_BUCKETS - 1)
    bucket = jnp.where(n < max_exact, n, large)
    return jnp.where((dist >= 0) & (dist < WIN), bucket, -1).astype(jnp.int32)


TS = 256
TQ = 1024
TQ_FWD = 1024


def _local_step(x, mem, tgt, kw, sp, sels):
    S = x.shape[0]
    ts = min(TS, S)
    tq = min(TQ, S)
    rc, rs1, rs2 = _rope_tables(S)
    bmap = _bucket_map()
    bias = _bias_build(sp["rel_bias"], bmap)
    row = lambda v: v.reshape(1, -1)

    saved = []
    for l in range(DEPTH):
        w = kw[l]
        an, qn, kvn = row(sp["attn_norm"][l]), row(sp["mla_q_norm"][l]), row(sp["mla_kv_norm"][l])
        bg, mnorm, mlpn = row(sp["b_gate"][l]), row(sp["mem_norm"][l]), row(sp["mlp_norm"][l])
        sinks = jnp.broadcast_to(sp["attn_sinks"][l][:, None], (NH, LANE))
        q, k, v, qs, ks, vs, qm, g = _pre_fwd(x, an, w["win"], bg, qn, kvn, w["wuq"], w["wukv"], rc, rs1, rs2, ts)
        oa, lse = _mla_fwd(q, k, v, min(TQ_FWD, S))
        ob = _swa_fwd(qs, ks, vs, bias, sinks)
        kvm = _memkv_fwd(mem, mnorm, w["wmem"])
        oc = _mem_fwd(qm, kvm, ts)
        x1, yb = _merge_fwd(x, g, oa, ob, oc, w["wa"], w["wb"], w["wc"], w["wout"], ts)
        x2 = _mlp_fwd(x1, mlpn, w["wup"], w["wdown"], ts)
        saved.append(dict(w=w, x=x, x1=x1, q=q, k=k, v=v, qs=qs, ks=ks, vs=vs, qm=qm, g=g, oa=oa, lse=lse, ob=ob,
                          oc=oc, kvm=kvm, yb=yb, an=an, qn=qn, kvn=kvn, mnorm=mnorm, mlpn=mlpn, sinks=sinks))
        x = x2

    sq, dx, dfn = _loss_kernel(x, row(sp["final_norm"]), tgt, ts)

    big = {n: [None] * DEPTH for n in SHARDED}
    small = {n: [None] * DEPTH for n in SMALL if n not in ("rel_bias", "final_norm")}
    dbias_total = None
    for l in reversed(range(DEPTH)):
        sv = saved[l]
        w = sv["w"]
        dx1, hb2, dub, ab, dxb, dmlpn = _mlp_bwd(dx, sv["x1"], sv["mlpn"], w["wup"], w["wdown"], ts)
        big["w_up"][l] = _matmul_tn(hb2, dub, "dw_up", shard_axis=1)
        big["w_down"][l] = _matmul_tn(ab, dxb, "dw_down", shard_axis=0)
        small["mlp_norm"][l] = dmlpn[0]

        dgp, dyo, doa, dob, doc, dla, dx1b, dbg = _merge_bwd(dx1, sv["g"], sv["oa"], sv["ob"], sv["oc"], w["wa"],
                                                             w["wb"], w["wc"], w["wout"], ts)
        big["w_out"][l] = _matmul_tn(sv["yb"], dx1b, "dw_out", shard_axis=0)
        big["w_o_mla"][l] = _cols_to_shards(_unpad_w_o_mla(_matmul_tn(sv["oa"], dyo[:, 0:D], "dw_o_mla")))
        big["w_o_swa"][l] = _cols_to_shards(
            _swa_unplace(_matmul_tn(sv["ob"], dyo[:, D:2 * D], "dw_o_swa")).reshape(NH * 64, D))
        big["w_o_mem"][l] = _matmul_tn(sv["oc"], dyo[:, 2 * D:3 * D], "dw_o_mem", shard_axis=1)
        small["b_gate"][l] = dbg[0]

        dqm, dkvm = _mem_bwd(sv["qm"], sv["kvm"], doc, ts)
        dwmem, dmnorm = _memkv_bwd(mem, sv["mnorm"], w["wmem"], dkvm)
        big["w_mem_kv"][l] = dwmem.reshape(4, D // 4, D).astype(BF16)
        small["mem_norm"][l] = dmnorm[0]

        dqs, dks, dvs, dbias, dsink = _swa_bwd(sv["qs"], sv["ks"], sv["vs"], dob, bias, sv["sinks"])
        dbias_total = dbias if dbias_total is None else dbias_total + dbias
        small["attn_sinks"][l] = dsink[:, 0]

        dq, dk, dv = _mla_bwd(sv["q"], sv["k"], sv["v"], doa, sv["lse"], dla, tq)

        dx, dproj, hb, cqn, ckvn, dqpre, dkv, dan, dqn, dkvn = _pre_bwd(
            sv["x"], dx1, dq, dk, dv, dqs, dks, dvs, dqm, dgp, sv["an"], sv["qn"], sv["kvn"], w["win"], w["wuq"],
            w["wukv"], rc, rs1, rs2, ts)
        big["w_in"][l] = _matmul_tn(hb, dproj, "dw_in")
        big["w_uq"][l] = _matmul_tn(cqn, dqpre, "dw_uq")
        big["w_ukv"][l] = _matmul_tn(ckvn, dkv, "dw_ukv")
        small["attn_norm"][l] = dan[0]
        small["mla_q_norm"][l] = dqn[0]
        small["mla_kv_norm"][l] = dkvn[0]

    gw = {n: jnp.stack(v) for n, v in big.items()}
    for n in ("w_in", "w_uq", "w_ukv"):
        gw[n] = _select_bwd(gw[n], sels[n], "shard_d" + n)
    gs = {n: jnp.stack(v) for n, v in small.items()}
    gs["rel_bias"] = _bias_reduce(dbias_total, bmap)
    gs["final_norm"] = dfn[0]
    return sq, dx, gw, gs


def _flatten(parts, rows):
    flat = jnp.concatenate([p.reshape(-1) for p in parts])
    return jnp.pad(flat, (0, rows * FLAT_W - flat.shape[0])).reshape(rows, FLAT_W)


def _unflatten(buf, shapes):
    flat = buf.reshape(-1)
    out, at = [], 0
    for s in shapes:
        n = int(np.prod(s))
        out.append(flat[at:at + n].reshape(s))
        at += n
    return out


def kernel(x, mem, rel_bias, attn_norm, mem_norm, w_in, b_gate, mla_q_norm, w_uq, mla_kv_norm, w_ukv, attn_sinks, w_mem_kv, w_o_mla, w_o_swa, w_o_mem, w_out, mlp_norm, w_up, w_down, final_norm, loss_target, m_rel_bias, m_attn_norm, m_mem_norm, m_w_in, m_b_gate, m_mla_q_norm, m_w_uq, m_mla_kv_norm, m_w_ukv, m_attn_sinks, m_w_mem_kv, m_w_o_mla, m_w_o_swa, m_w_o_mem, m_w_out, m_mlp_norm, m_w_up, m_w_down, m_final_norm, v_rel_bias, v_attn_norm, v_mem_norm, v_w_in, v_b_gate, v_mla_q_norm, v_w_uq, v_mla_kv_norm, v_w_ukv, v_attn_sinks, v_w_mem_kv, v_w_o_mla, v_w_o_swa, v_w_o_mem, v_w_out, v_mlp_norm, v_w_up, v_w_down, v_final_norm):
    args = dict(locals())
    W = {n: args[n] for n in WEIGHTS}
    M = {n: args["m_" + n] for n in WEIGHTS}
    V = {n: args["v_" + n] for n in WEIGHTS}
    small_shapes = [W[n].shape for n in SMALL]
    sels = _selections()

    wire = _wire_shards(W)
    gathered = dict(zip(SHARDED, _gather_weights([wire[n] for n in SHARDED])))
    kw = _kernel_weights(gathered, sels)
    sp = {n: W[n] for n in SMALL}

    sq, dx, gw, gs = _local_step(x[0], mem[0], loss_target[0], kw, sp, sels)

    gsmall = _flatten([gs[n] for n in SMALL], SMALL_ROWS)
    *slots, small_slots = _scatter_grads([gw[n] for n in SHARDED], gsmall)
    reds = [_sum_slots(s, "sum_" + n) for n, s in zip(SHARDED, slots)]
    red_small = _sum_slots(small_slots, "sum_small")
    G = dict(zip(SHARDED, _swap_layers(reds)))
    G["w_in"] = G["w_in"][..., :W_IN_SHARD]
    G["w_uq"] = G["w_uq"][..., :W_UQ_SHARD]

    DW, NM, NV = {}, {}, {}
    for n in SHARDED:
        shape = W[n].shape
        two_d = lambda a: a.reshape(-1, shape[-1])
        d, nm, nv = _adamw(two_d(W[n]), two_d(G[n]), two_d(M[n]), two_d(V[n]), "adamw_" + n)
        DW[n], NM[n], NV[n] = d.reshape(shape), nm.reshape(shape), nv.reshape(shape)
    d_s, m_s, v_s = _adamw(_flatten([W[n] for n in SMALL], SMALL_ROWS), red_small,
                           _flatten([M[n] for n in SMALL], SMALL_ROWS), _flatten([V[n] for n in SMALL], SMALL_ROWS),
                           "adamw_small")
    for out, buf in ((G, red_small), (DW, d_s), (NM, m_s), (NV, v_s)):
        out.update(zip(SMALL, _unflatten(buf, small_shapes)))
    loss = lax.psum(0.5 * sq[0, 0] / D, ("x", "y", "c"))
    return (loss, dx[None], *[G[n] for n in WEIGHTS], *[DW[n] for n in WEIGHTS], *[NM[n] for n in WEIGHTS],
            *[NV[n] for n in WEIGHTS])
```

```python
import functools
import math

import numpy as np
import jax
import jax.numpy as jnp
from jax import lax
from jax.experimental import pallas as pl
from jax.experimental.pallas import tpu as pltpu

F32 = jnp.float32
BF16 = jnp.bfloat16

D = 1024
DFF = 4096
DEPTH = 2
EPS = 1e-6
LANE = 128
NH = 8
SWA_R = 4
MEM_H = 4
MEM_LEN = 256
WIN = 128
NEG = -1e30
MLA_SCALE = 96 ** -0.5
SWA_SCALE = 64 ** -0.5
MEM_SCALE = 128 ** -0.5
REL_BUCKETS = 32
ROPE_THETA = 10000.0

C_CQ, C_CKV, C_KPE, C_QS, C_KS, C_VS, C_QM, C_G, C_END = 0, 256, 384, 512, 1536, 1664, 1792, 2304, 5376
IN_COLS = 4768

ADAM_LR = 0.001
ADAM_B1 = 0.9
ADAM_B2 = 0.999
ADAM_EPS = 1e-08
ADAM_WD = 0.01
ADAM_STEP = 10

VMEM_LIMIT = 56 * 1024 * 1024

SHARDED = ("w_in", "w_uq", "w_ukv", "w_mem_kv", "w_o_mla", "w_o_swa", "w_o_mem", "w_out", "w_up", "w_down")
SHARD_AXIS = {"w_in": 2, "w_uq": 2, "w_ukv": 2, "w_mem_kv": 1, "w_o_mla": 2, "w_o_swa": 2, "w_o_mem": 2,
              "w_out": 1, "w_up": 2, "w_down": 1}
SMALL = ("rel_bias", "attn_norm", "mem_norm", "b_gate", "mla_q_norm", "mla_kv_norm", "attn_sinks", "mlp_norm",
         "final_norm")
WEIGHTS = ("rel_bias", "attn_norm", "mem_norm", "w_in", "b_gate", "mla_q_norm", "w_uq", "mla_kv_norm", "w_ukv",
           "attn_sinks", "w_mem_kv", "w_o_mla", "w_o_swa", "w_o_mem", "w_out", "mlp_norm", "w_up", "w_down",
           "final_norm")
FLAT_W = 1024
FLAT_TILE = 256
SMALL_ROWS = 16
MESH_ID = pl.DeviceIdType.MESH


def _pcall(body, *, name, grid, in_specs, out_specs, out_shape, scratch=(), prefetch=0, sem=None):
    params = pltpu.CompilerParams(dimension_semantics=sem, vmem_limit_bytes=VMEM_LIMIT)
    if prefetch:
        spec = pltpu.PrefetchScalarGridSpec(num_scalar_prefetch=prefetch, grid=grid, in_specs=in_specs,
                                            out_specs=out_specs, scratch_shapes=scratch)
        return pl.pallas_call(body, name=name, grid_spec=spec, out_shape=out_shape, compiler_params=params)
    return pl.pallas_call(body, name=name, grid=grid, in_specs=in_specs, out_specs=out_specs, out_shape=out_shape,
                          scratch_shapes=scratch, compiler_params=params)


def _tok(ts, w):
    return pl.BlockSpec((ts, w), lambda i: (i, 0))


def _full(*shape):
    return pl.BlockSpec(shape, lambda *_: (0,) * len(shape))


def _sds(shape, dtype):
    return jax.ShapeDtypeStruct(shape, dtype)


def _dot(a, b):
    return jnp.dot(a, b, preferred_element_type=F32)


def _dot_nt(a, b):
    return lax.dot_general(a, b, (((1,), (1,)), ((), ())), preferred_element_type=F32)


def _dot_tn(a, b):
    return lax.dot_general(a, b, (((0,), (0,)), ((), ())), preferred_element_type=F32)


def _rms(x):
    r = lax.rsqrt(jnp.mean(x * x, axis=-1, keepdims=True) + EPS)
    return x * r, r


def _rms_bwd(dyg, n, r):
    return r * (dyg - n * jnp.mean(n * dyg, axis=-1, keepdims=True))


def _rope(t, c, s1, s2):
    return t * c + pltpu.roll(t, 16, 1) * s1 + pltpu.roll(t, LANE - 16, 1) * s2


def _rope_bwd(dy, c, s1, s2):
    return dy * c + pltpu.roll(dy * s1, LANE - 16, 1) + pltpu.roll(dy * s2, 16, 1)


def _hs(h):
    return slice(h * LANE, (h + 1) * LANE)


def _colsum(t):
    return jnp.sum(t, axis=0, keepdims=True)


def _pre_fwd(x, an, win, bg, qn, kvn, wuq, wukv, rc, rs1, rs2, ts):
    S = x.shape[0]

    def body(x_ref, an_ref, win_ref, bg_ref, qn_ref, kvn_ref, wuq_ref, wukv_ref, rc_ref, rs1_ref, rs2_ref,
             q_ref, k_ref, v_ref, qs_ref, ks_ref, vs_ref, qm_ref, g_ref):
        n, _ = _rms(x_ref[...])
        hb = (n * an_ref[...]).astype(BF16)
        pa = _dot(hb, win_ref[:, C_CQ:C_QS])
        ncq, _ = _rms(pa[:, 0:256])
        cqn = (ncq * qn_ref[...]).astype(BF16)
        nkv, _ = _rms(pa[:, 256:384])
        ckvn = (nkv * kvn_ref[...]).astype(BF16)
        c, s1, s2 = rc_ref[...], rs1_ref[...], rs2_ref[...]
        kper = _rope(pa[:, 384:512], c, s1, s2)
        qp = _dot(cqn, wuq_ref[...])
        kv = _dot(ckvn, wukv_ref[...])
        for h in range(NH):
            q_ref[:, _hs(h)] = (_rope(qp[:, _hs(h)], c, s1, s2) * MLA_QSCALE).astype(BF16)
            k_ref[:, _hs(h)] = (kv[:, _hs(h)] + kper).astype(BF16)
        v_ref[...] = kv[:, NH * LANE:].astype(BF16)
        pb = _dot(hb, win_ref[:, C_QS:C_G])
        qs_ref[...] = pb[:, 0:1024].astype(BF16)
        ks_ref[...] = pb[:, 1024:1152].astype(BF16)
        vs_ref[...] = pb[:, 1152:1280].astype(BF16)
        qm_ref[...] = pb[:, 1280:1792].astype(BF16)
        g_ref[...] = jax.nn.sigmoid(_dot(hb, win_ref[:, C_G:C_END]) + bg_ref[...])

    return _pcall(
        body, name="pre_fwd", grid=(S // ts,),
        in_specs=[_tok(ts, D), _full(1, D), _full(D, C_END), _full(1, 3 * D), _full(1, 256), _full(1, 128),
                  _full(256, NH * LANE), _full(128, 2 * NH * LANE), _tok(ts, LANE), _tok(ts, LANE), _tok(ts, LANE)],
        out_specs=[_tok(ts, 1024), _tok(ts, 1024), _tok(ts, 1024), _tok(ts, 1024), _tok(ts, 128), _tok(ts, 128),
                   _tok(ts, 512), _tok(ts, 3 * D)],
        out_shape=[_sds((S, 1024), BF16), _sds((S, 1024), BF16), _sds((S, 1024), BF16), _sds((S, 1024), BF16),
                   _sds((S, 128), BF16), _sds((S, 128), BF16), _sds((S, 512), BF16), _sds((S, 3 * D), F32)],
        sem=("arbitrary",),
    )(x, an, win, bg, qn, kvn, wuq, wukv, rc, rs1, rs2)


def _merge_fwd(x, g, oa, ob, oc, wa, wb, wc, wout, ts):
    S = x.shape[0]

    def body(x_ref, g_ref, oa_ref, ob_ref, oc_ref, wa_ref, wb_ref, wc_ref, wout_ref, x1_ref, yb_ref):
        y = g_ref[:, 0:D] * _dot(oa_ref[...], wa_ref[...])
        y = y + g_ref[:, D:2 * D] * _dot(ob_ref[...], wb_ref[...])
        y = y + g_ref[:, 2 * D:3 * D] * _dot(oc_ref[...], wc_ref[...])
        yb = y.astype(BF16)
        yb_ref[...] = yb
        x1_ref[...] = x_ref[...] + _dot(yb, wout_ref[...])

    return _pcall(
        body, name="merge_fwd", grid=(S // ts,),
        in_specs=[_tok(ts, D), _tok(ts, 3 * D), _tok(ts, 1024), _tok(ts, 1024), _tok(ts, 512),
                  _full(1024, D), _full(1024, D), _full(512, D), _full(D, D)],
        out_specs=[_tok(ts, D), _tok(ts, D)],
        out_shape=[_sds((S, D), F32), _sds((S, D), BF16)],
        sem=("arbitrary",),
    )(x, g, oa, ob, oc, wa, wb, wc, wout)


def _mlp_fwd(x1, mn, wup, wdown, ts):
    S = x1.shape[0]

    def body(x_ref, mn_ref, wup_ref, wdown_ref, x2_ref):
        xv = x_ref[...]
        n, _ = _rms(xv)
        u = _dot((n * mn_ref[...]).astype(BF16), wup_ref[...])
        a = jnp.square(jnp.maximum(u, 0.0))
        x2_ref[...] = xv + _dot(a.astype(BF16), wdown_ref[...])

    return _pcall(
        body, name="mlp_fwd", grid=(S // ts,),
        in_specs=[_tok(ts, D), _full(1, D), _full(D, DFF), _full(DFF, D)],
        out_specs=_tok(ts, D), out_shape=_sds((S, D), F32), sem=("arbitrary",),
    )(x1, mn, wup, wdown)


def _loss_kernel(x, fn, tgt, ts):
    S = x.shape[0]

    def body(x_ref, fn_ref, t_ref, loss_ref, dx_ref, dfn_ref):
        @pl.when(pl.program_id(0) == 0)
        def _():
            loss_ref[...] = jnp.zeros_like(loss_ref)
            dfn_ref[...] = jnp.zeros_like(dfn_ref)

        n, r = _rms(x_ref[...])
        err = n * fn_ref[...] - t_ref[...]
        loss_ref[...] += jnp.sum(err * err)
        dy = err * (1.0 / D)
        dfn_ref[...] += _colsum(dy * n)
        dx_ref[...] = _rms_bwd(dy * fn_ref[...], n, r)

    return _pcall(
        body, name="loss_head", grid=(S // ts,),
        in_specs=[_tok(ts, D), _full(1, D), _tok(ts, D)],
        out_specs=[_full(1, LANE), _tok(ts, D), _full(1, D)],
        out_shape=[_sds((1, LANE), F32), _sds((S, D), F32), _sds((1, D), F32)],
        sem=("arbitrary",),
    )(x, fn, tgt)


def _mlp_bwd(dx2, x1, mn, wup, wdown, ts):
    S = x1.shape[0]

    def body(dx_ref, x_ref, mn_ref, wup_ref, wdown_ref, dx1_ref, hb_ref, dub_ref, ab_ref, dxb_ref, dmn_ref):
        @pl.when(pl.program_id(0) == 0)
        def _():
            dmn_ref[...] = jnp.zeros_like(dmn_ref)

        dx = dx_ref[...]
        n, r = _rms(x_ref[...])
        g = mn_ref[...]
        hb = (n * g).astype(BF16)
        hb_ref[...] = hb
        rl = jnp.maximum(_dot(hb, wup_ref[...]), 0.0)
        ab_ref[...] = jnp.square(rl).astype(BF16)
        dxb = dx.astype(BF16)
        dxb_ref[...] = dxb
        dub = (_dot_nt(dxb, wdown_ref[...]) * (2.0 * rl)).astype(BF16)
        dub_ref[...] = dub
        dh = _dot_nt(dub, wup_ref[...])
        dmn_ref[...] += _colsum(dh * n)
        dx1_ref[...] = dx + _rms_bwd(dh * g, n, r)

    return _pcall(
        body, name="mlp_bwd", grid=(S // ts,),
        in_specs=[_tok(ts, D), _tok(ts, D), _full(1, D), _full(D, DFF), _full(DFF, D)],
        out_specs=[_tok(ts, D), _tok(ts, D), _tok(ts, DFF), _tok(ts, DFF), _tok(ts, D), _full(1, D)],
        out_shape=[_sds((S, D), F32), _sds((S, D), BF16), _sds((S, DFF), BF16), _sds((S, DFF), BF16),
                   _sds((S, D), BF16), _sds((1, D), F32)],
        sem=("arbitrary",),
    )(dx2, x1, mn, wup, wdown)


def _merge_bwd(dx1, g, oa, ob, oc, wa, wb, wc, wout, ts):
    S = dx1.shape[0]

    def body(dx_ref, g_ref, oa_ref, ob_ref, oc_ref, wa_ref, wb_ref, wc_ref, wout_ref,
             dgp_ref, dyo_ref, doa_ref, dob_ref, doc_ref, dla_ref, dxb_ref, dbg_ref):
        @pl.when(pl.program_id(0) == 0)
        def _():
            dbg_ref[...] = jnp.zeros_like(dbg_ref)

        dxb = dx_ref[...].astype(BF16)
        dxb_ref[...] = dxb
        dy = _dot_nt(dxb, wout_ref[...])
        branches = ((oa_ref, wa_ref, doa_ref), (ob_ref, wb_ref, dob_ref), (oc_ref, wc_ref, doc_ref))
        for b, (o_ref, w_ref, do_ref) in enumerate(branches):
            cols = slice(b * D, (b + 1) * D)
            gb = g_ref[:, cols]
            o = o_ref[...]
            dgpre = dy * _dot(o, w_ref[...]) * gb * (1.0 - gb)
            dgp_ref[:, cols] = dgpre.astype(BF16)
            dbg_ref[:, cols] += _colsum(dgpre)
            dyo = (dy * gb).astype(BF16)
            dyo_ref[:, cols] = dyo
            do = _dot_nt(dyo, w_ref[...])
            do_ref[...] = do.astype(BF16)
            if b == 0:
                lane = lax.broadcasted_iota(jnp.int32, (ts, LANE), 1)
                dls = jnp.zeros((ts, LANE), F32)
                for h in range(NH):
                    dl = jnp.sum(do[:, _hs(h)] * o[:, _hs(h)].astype(F32), axis=1, keepdims=True)
                    dls = jnp.where(lane == h, dl, dls)
                dla_ref[:, 0, :] = jnp.transpose(dls)[0:NH, :]

    return _pcall(
        body, name="merge_bwd", grid=(S // ts,),
        in_specs=[_tok(ts, D), _tok(ts, 3 * D), _tok(ts, 1024), _tok(ts, 1024), _tok(ts, 512),
                  _full(1024, D), _full(1024, D), _full(512, D), _full(D, D)],
        out_specs=[_tok(ts, 3 * D), _tok(ts, 3 * D), _tok(ts, 1024), _tok(ts, 1024), _tok(ts, 512),
                   pl.BlockSpec((NH, 1, ts), lambda i: (0, 0, i)), _tok(ts, D), _full(1, 3 * D)],
        out_shape=[_sds((S, 3 * D), BF16), _sds((S, 3 * D), BF16), _sds((S, 1024), BF16), _sds((S, 1024), BF16),
                   _sds((S, 512), BF16), _sds((NH, 1, S), F32), _sds((S, D), BF16), _sds((1, 3 * D), F32)],
        sem=("arbitrary",),
    )(dx1, g, oa, ob, oc, wa, wb, wc, wout)


def _pre_bwd(x, dx1, dq, dk, dv, dqs, dks, dvs, dqm, dgp, an, qn, kvn, win, wuq, wukv, rc, rs1, rs2, ts):
    S = x.shape[0]

    def body(x_ref, dx1_ref, dq_ref, dk_ref, dv_ref, dqs_ref, dks_ref, dvs_ref, dqm_ref, dgp_ref,
             an_ref, qn_ref, kvn_ref, win_ref, wuq_ref, wukv_ref, rc_ref, rs1_ref, rs2_ref,
             dx_ref, dproj_ref, hb_ref, cqn_ref, ckvn_ref, dqpre_ref, dkv_ref, dan_ref, dqn_ref, dkvn_ref):
        @pl.when(pl.program_id(0) == 0)
        def _():
            dan_ref[...] = jnp.zeros_like(dan_ref)
            dqn_ref[...] = jnp.zeros_like(dqn_ref)
            dkvn_ref[...] = jnp.zeros_like(dkvn_ref)

        n, r = _rms(x_ref[...])
        hb = (n * an_ref[...]).astype(BF16)
        hb_ref[...] = hb
        pa = _dot(hb, win_ref[:, C_CQ:C_KPE])
        ncq, rq = _rms(pa[:, 0:256])
        cqn_ref[...] = (ncq * qn_ref[...]).astype(BF16)
        nkv, rkv = _rms(pa[:, 256:384])
        ckvn_ref[...] = (nkv * kvn_ref[...]).astype(BF16)
        c, s1, s2 = rc_ref[...], rs1_ref[...], rs2_ref[...]

        dkper = jnp.zeros((ts, LANE), F32)
        for h in range(NH):
            dqpre_ref[:, _hs(h)] = _rope_bwd(dq_ref[:, _hs(h)], c, s1, s2).astype(BF16)
            dkh = dk_ref[:, _hs(h)]
            dkper = dkper + dkh
            dkv_ref[:, _hs(h)] = dkh.astype(BF16)
        dkv_ref[:, NH * LANE:] = dv_ref[...].astype(BF16)

        dcqn = _dot_nt(dqpre_ref[...], wuq_ref[...])
        dqn_ref[...] += _colsum(dcqn * ncq)
        dproj_ref[:, C_CQ:C_CKV] = _rms_bwd(dcqn * qn_ref[...], ncq, rq).astype(BF16)
        dckvn = _dot_nt(dkv_ref[...], wukv_ref[...])
        dkvn_ref[...] += _colsum(dckvn * nkv)
        dproj_ref[:, C_CKV:C_KPE] = _rms_bwd(dckvn * kvn_ref[...], nkv, rkv).astype(BF16)
        lane = lax.broadcasted_iota(jnp.int32, (ts, LANE), 1)
        dkpe = jnp.where((lane >= 64) & (lane < 96), _rope_bwd(dkper, c, s1, s2), 0.0)
        dproj_ref[:, C_KPE:C_QS] = dkpe.astype(BF16)
        dproj_ref[:, C_QS:C_KS] = dqs_ref[...]
        dproj_ref[:, C_KS:C_VS] = dks_ref[...].astype(BF16)
        dproj_ref[:, C_VS:C_QM] = dvs_ref[...].astype(BF16)
        dproj_ref[:, C_QM:C_G] = dqm_ref[...]
        dproj_ref[:, C_G:C_END] = dgp_ref[...]

        dh = _dot_nt(dproj_ref[...], win_ref[...])
        dan_ref[...] += _colsum(dh * n)
        dx_ref[...] = dx1_ref[...] + _rms_bwd(dh * an_ref[...], n, r)

    return _pcall(
        body, name="pre_bwd", grid=(S // ts,),
        in_specs=[_tok(ts, D), _tok(ts, D), _tok(ts, 1024), _tok(ts, 1024), _tok(ts, 1024), _tok(ts, 1024),
                  _tok(ts, 128), _tok(ts, 128), _tok(ts, 512), _tok(ts, 3 * D),
                  _full(1, D), _full(1, 256), _full(1, 128), _full(D, C_END), _full(256, NH * LANE),
                  _full(128, 2 * NH * LANE), _tok(ts, LANE), _tok(ts, LANE), _tok(ts, LANE)],
        out_specs=[_tok(ts, D), _tok(ts, C_END), _tok(ts, D), _tok(ts, 256), _tok(ts, 128), _tok(ts, 1024),
                   _tok(ts, 2048), _full(1, D), _full(1, 256), _full(1, 128)],
        out_shape=[_sds((S, D), F32), _sds((S, C_END), BF16), _sds((S, D), BF16), _sds((S, 256), BF16),
                   _sds((S, 128), BF16), _sds((S, 1024), BF16), _sds((S, 2048), BF16), _sds((1, D), F32),
                   _sds((1, 256), F32), _sds((1, 128), F32)],
        sem=("arbitrary",),
    )(x, dx1, dq, dk, dv, dqs, dks, dvs, dqm, dgp, an, qn, kvn, win, wuq, wukv, rc, rs1, rs2)


def _pick_tile(n, cap):
    best = LANE
    for t in range(LANE, min(n, cap) + 1, LANE):
        if n % t == 0:
            best = t
    return best


def _matmul_tn(a, b, name, shard_axis=None):
    S, M = a.shape
    N = b.shape[1]
    tm = _pick_tile(M // 4 if shard_axis == 0 else M, 1024)
    tn = _pick_tile(N // 4 if shard_axis == 1 else N, 2048)
    ts = min(S, 1024)
    nk = S // ts

    def body(a_ref, b_ref, o_ref, *acc):
        acc_ref = acc[0] if acc else o_ref

        @pl.when(pl.program_id(2) == 0)
        def _():
            acc_ref[...] = jnp.zeros_like(acc_ref)

        acc_ref[...] += _dot_tn(a_ref[...], b_ref[...])
        if acc:
            @pl.when(pl.program_id(2) == nk - 1)
            def _():
                o_ref[0] = acc_ref[...].astype(o_ref.dtype)

    if shard_axis is None:
        out_spec = pl.BlockSpec((tm, tn), lambda i, j, k: (i, j))
        out_shape, scratch = _sds((M, N), F32), ()
    elif shard_axis == 0:
        per = (M // 4) // tm
        out_spec = pl.BlockSpec((1, tm, tn), lambda i, j, k: (i // per, i % per, j))
        out_shape, scratch = _sds((4, M // 4, N), BF16), (pltpu.VMEM((tm, tn), F32),)
    else:
        per = (N // 4) // tn
        out_spec = pl.BlockSpec((1, tm, tn), lambda i, j, k: (j // per, i, j % per))
        out_shape, scratch = _sds((4, M, N // 4), BF16), (pltpu.VMEM((tm, tn), F32),)
    return _pcall(
        body, name=name, grid=(M // tm, N // tn, nk),
        in_specs=[pl.BlockSpec((ts, tm), lambda i, j, k: (k, i)), pl.BlockSpec((ts, tn), lambda i, j, k: (k, j))],
        out_specs=out_spec, out_shape=out_shape, scratch=scratch, sem=("parallel", "parallel", "arbitrary"),
    )(a, b)


def _select_fwd(a, sel, name):
    _, L, M, K = a.shape
    N = sel.shape[2]
    tn = _pick_tile(N, 1792)

    def body(a_ref, s_ref, o_ref, acc_ref):
        s = pl.program_id(2)

        @pl.when(s == 0)
        def _():
            acc_ref[...] = jnp.zeros_like(acc_ref)

        acc_ref[...] += _dot(a_ref[0, 0], s_ref[0])

        @pl.when(s == 3)
        def _():
            o_ref[0] = acc_ref[...].astype(BF16)

    return _pcall(
        body, name=name, grid=(L, N // tn, 4),
        in_specs=[pl.BlockSpec((1, 1, M, K), lambda l, j, s: (s, l, 0, 0)),
                  pl.BlockSpec((1, K, tn), lambda l, j, s: (s, 0, j))],
        out_specs=pl.BlockSpec((1, M, tn), lambda l, j, s: (l, 0, j)),
        out_shape=_sds((L, M, N), BF16), scratch=(pltpu.VMEM((M, tn), F32),),
        sem=("parallel", "parallel", "arbitrary"),
    )(a, sel)


def _select_bwd(dw, sel, name):
    L, M, N = dw.shape
    K = sel.shape[1]
    tk = _pick_tile(N, 1792)
    nk = N // tk

    def body(d_ref, s_ref, o_ref, acc_ref):
        k = pl.program_id(2)

        @pl.when(k == 0)
        def _():
            acc_ref[...] = jnp.zeros_like(acc_ref)

        acc_ref[...] += _dot_nt(d_ref[0].astype(BF16), s_ref[0])

        @pl.when(k == nk - 1)
        def _():
            o_ref[0, 0] = acc_ref[...].astype(BF16)

    return _pcall(
        body, name=name, grid=(L, 4, nk),
        in_specs=[pl.BlockSpec((1, M, tk), lambda l, s, k: (l, 0, k)),
                  pl.BlockSpec((1, K, tk), lambda l, s, k: (s, 0, k))],
        out_specs=pl.BlockSpec((1, 1, M, K), lambda l, s, k: (l, s, 0, 0)),
        out_shape=_sds((L, 4, M, K), BF16), scratch=(pltpu.VMEM((M, K), F32),),
        sem=("parallel", "parallel", "arbitrary"),
    )(dw, sel)


MLA_RC = 128
MLA_RC_FWD = 128
LOG2E = math.log2(math.e)
MLA_QSCALE = MLA_SCALE * LOG2E


def _mla_fwd(q, k, v, tq):
    S = q.shape[0]
    nq = S // tq
    pairs = [(i, j) for i in range(nq) for j in range(i + 1)]
    qi = jnp.asarray(np.array([p[0] for p in pairs], np.int32))
    kj = jnp.asarray(np.array([p[1] for p in pairs], np.int32))

    def body(qi_ref, kj_ref, q_ref, k_ref, v_ref, o_ref, lse_ref, m_s, l_s, acc_s):
        t = pl.program_id(1)
        i, j = qi_ref[t], kj_ref[t]

        @pl.when(j == 0)
        def _():
            m_s[...] = jnp.full_like(m_s, NEG)
            l_s[...] = jnp.zeros_like(l_s)
            acc_s[...] = jnp.zeros_like(acc_s)

        def step(masked):
            rc = min(MLA_RC_FWD, tq)
            nc = tq // rc
            keys = [(c + 1) * rc if masked else tq for c in range(nc)]
            scores = [_dot_nt(q_ref[c * rc:(c + 1) * rc, :], k_ref[0:keys[c], :]) for c in range(nc)]
            for c in range(nc):
                rows = slice(c * rc, (c + 1) * rc)
                s = scores[c]
                if masked:
                    row = lax.broadcasted_iota(jnp.int32, (rc, keys[c]), 0) + c * rc
                    col = lax.broadcasted_iota(jnp.int32, (rc, keys[c]), 1)
                    s = jnp.where(col <= row, s, NEG)
                tiles = [s[:, _hs(u)] for u in range(keys[c] // LANE)]
                mx = functools.reduce(jnp.maximum, tiles)
                m_old = m_s[rows, :]
                m_new = jnp.maximum(m_old, jnp.max(mx, axis=1, keepdims=True))
                alpha = jnp.exp2(m_old - m_new)
                ps = [jnp.exp2(u - m_new) for u in tiles]
                l_s[rows, :] = alpha * l_s[rows, :] + functools.reduce(jnp.add, ps)
                p = jnp.concatenate([u.astype(BF16) for u in ps], axis=1)
                acc_s[rows, :] = alpha * acc_s[rows, :] + _dot(p, v_ref[0:keys[c], :])
                m_s[rows, :] = m_new

        @pl.when(j < i)
        def _():
            step(False)

        @pl.when(j == i)
        def _():
            step(True)
            l = jnp.sum(l_s[...], axis=1, keepdims=True)
            o_ref[...] = (acc_s[...] / l).astype(BF16)
            lse_ref[0] = jnp.transpose(m_s[...] + jnp.log2(l))[0:1, :]

    qmap = lambda h, t, qi_r, kj_r: (qi_r[t], h)
    kmap = lambda h, t, qi_r, kj_r: (kj_r[t], h)
    return _pcall(
        body, name="mla_fwd", grid=(NH, len(pairs)), prefetch=2,
        in_specs=[pl.BlockSpec((tq, LANE), qmap), pl.BlockSpec((tq, LANE), kmap), pl.BlockSpec((tq, LANE), kmap)],
        out_specs=[pl.BlockSpec((tq, LANE), qmap),
                   pl.BlockSpec((1, 1, tq), lambda h, t, qi_r, kj_r: (h, 0, qi_r[t]))],
        out_shape=[_sds((S, NH * LANE), BF16), _sds((NH, 1, S), F32)],
        scratch=[pltpu.VMEM((tq, LANE), F32), pltpu.VMEM((tq, LANE), F32), pltpu.VMEM((tq, LANE), F32)],
        sem=("arbitrary", "arbitrary"),
    )(qi, kj, q, k, v)


def _mla_bwd(q, k, v, do, lse, delta, tq):
    S = q.shape[0]
    nq = S // tq
    pairs = [(i, j) for j in range(nq) for i in range(j, nq)]
    qi = jnp.asarray(np.array([p[0] for p in pairs], np.int32))
    kj = jnp.asarray(np.array([p[1] for p in pairs], np.int32))

    def body(qi_ref, kj_ref, q_ref, k_ref, v_ref, do_ref, lse_ref, dl_ref, dq_ref, dk_ref, dv_ref,
             dq_s, dk_s, dv_s):
        t = pl.program_id(1)
        i, j = qi_ref[t], kj_ref[t]

        @pl.when(t == 0)
        def _():
            dq_s[...] = jnp.zeros_like(dq_s)

        @pl.when(i == j)
        def _():
            dk_s[...] = jnp.zeros_like(dk_s)
            dv_s[...] = jnp.zeros_like(dv_s)

        qrows = pl.ds(pl.multiple_of(i * tq, tq), tq)

        def step(masked):
            lse_r, dl_r = lse_ref[0], dl_ref[0]
            dq = jnp.zeros((tq, LANE), F32)
            nc = tq // MLA_RC
            q0 = [c * MLA_RC if masked else 0 for c in range(nc)]
            sts = [_dot_nt(k_ref[c * MLA_RC:(c + 1) * MLA_RC, :], q_ref[q0[c]:, :]) for c in range(nc)]
            dpts = [_dot_nt(v_ref[c * MLA_RC:(c + 1) * MLA_RC, :], do_ref[q0[c]:, :]) for c in range(nc)]
            for c in range(nc):
                rows = slice(c * MLA_RC, (c + 1) * MLA_RC)
                qb, dob = q_ref[q0[c]:, :], do_ref[q0[c]:, :]
                pt = jnp.exp2(sts[c] - lse_r[:, q0[c]:])
                if masked:
                    key = lax.broadcasted_iota(jnp.int32, (MLA_RC, tq - q0[c]), 0)
                    qry = lax.broadcasted_iota(jnp.int32, (MLA_RC, tq - q0[c]), 1)
                    pt = jnp.where(key <= qry, pt, 0.0)
                dv_s[rows, :] += _dot(pt.astype(BF16), dob)
                gt = (pt * (dpts[c] - dl_r[:, q0[c]:])).astype(BF16)
                dk_s[rows, :] += _dot(gt, qb)
                part = _dot_tn(gt, k_ref[rows, :])
                if masked:
                    at = pl.multiple_of(i * tq + q0[c], MLA_RC)
                    dq_s[pl.ds(at, tq - q0[c]), :] += part
                else:
                    dq = dq + part
            if not masked:
                dq_s[qrows, :] += dq

        @pl.when(i > j)
        def _():
            step(False)

        @pl.when(i == j)
        def _():
            step(True)
            dq_ref[...] = dq_s[qrows, :] * MLA_SCALE

        @pl.when(i == nq - 1)
        def _():
            dk_ref[...] = dk_s[...] * (1.0 / LOG2E)
            dv_ref[...] = dv_s[...]

    qmap = lambda h, t, qi_r, kj_r: (qi_r[t], h)
    kmap = lambda h, t, qi_r, kj_r: (kj_r[t], h)
    rmap = lambda h, t, qi_r, kj_r: (h, 0, qi_r[t])
    return _pcall(
        body, name="mla_bwd", grid=(NH, len(pairs)), prefetch=2,
        in_specs=[pl.BlockSpec((tq, LANE), qmap), pl.BlockSpec((tq, LANE), kmap), pl.BlockSpec((tq, LANE), kmap),
                  pl.BlockSpec((tq, LANE), qmap), pl.BlockSpec((1, 1, tq), rmap), pl.BlockSpec((1, 1, tq), rmap)],
        out_specs=[pl.BlockSpec((tq, LANE), kmap), pl.BlockSpec((tq, LANE), kmap), pl.BlockSpec((tq, LANE), kmap)],
        out_shape=[_sds((S, NH * LANE), F32), _sds((S, NH * LANE), F32), _sds((S, NH * LANE), F32)],
        scratch=[pltpu.VMEM((S, LANE), F32), pltpu.VMEM((tq, LANE), F32), pltpu.VMEM((tq, LANE), F32)],
        sem=("arbitrary", "arbitrary"),
    )(qi, kj, q, k, v, do, lse, delta)


SWA_SUB = 4
SWA_T = SWA_SUB * WIN


def _swa_specs(nsteps, rev):
    step = (lambda i: nsteps - 1 - i) if rev else (lambda i: i)
    cur = lambda w: pl.BlockSpec((SWA_T, w), lambda i: (step(i), 0))
    prev = pl.BlockSpec((WIN, LANE), lambda i: (jnp.maximum(step(i) * SWA_SUB - 1, 0), 0))
    return step, cur, prev


def _swa_probs(qk, bias_h, sink, first_mask):
    s = qk * SWA_SCALE + bias_h
    if first_mask is not None:
        s = jnp.where(first_mask, NEG, s)
    m = jnp.maximum(jnp.max(s, axis=1, keepdims=True), sink)
    e = jnp.exp(s - m)
    es = jnp.exp(sink - m)
    inv = 1.0 / (jnp.sum(e, axis=1, keepdims=True) + es)
    return e * inv, es * inv


SWA_GR = SWA_R * WIN


def _swa_group(ref, rows, g):
    return jnp.concatenate([ref[rows, _hs(g * SWA_R + r)] for r in range(SWA_R)], axis=0)


def _swa_rows(bias, sinks):
    sink_rows = jnp.broadcast_to(sinks[:, None, :], (NH, WIN, LANE)).reshape(NH * WIN, LANE)
    return bias.reshape(NH * WIN, 2 * WIN), sink_rows


def _swa_fwd(qs, ks, vs, bias, sinks):
    S = qs.shape[0]
    nsteps = S // SWA_T
    step, cur, prev = _swa_specs(nsteps, False)

    def body(qs_ref, kc_ref, kp_ref, vc_ref, vp_ref, bias_ref, sk_ref, o_ref):
        first = pl.program_id(0) == 0
        kk = jnp.concatenate([kp_ref[...], kc_ref[...]], axis=0)
        vv = jnp.concatenate([vp_ref[...], vc_ref[...]], axis=0)
        col = lax.broadcasted_iota(jnp.int32, (WIN, 2 * WIN), 1)
        for b in range(SWA_SUB):
            kkb = kk[b * WIN:(b + 2) * WIN]
            vvb = vv[b * WIN:(b + 2) * WIN]
            fm = (first & (col < WIN)) if b == 0 else None
            rows = slice(b * WIN, (b + 1) * WIN)
            qks = [_dot_nt(qs_ref[rows, _hs(h)], kkb) for h in range(NH)]
            for h in range(NH):
                p, _ = _swa_probs(qks[h], bias_ref[h], sk_ref[h:h + 1, 0:1], fm)
                o_ref[rows, _hs(h)] = _dot(p.astype(BF16), vvb).astype(BF16)

    return _pcall(
        body, name="swa_fwd", grid=(nsteps,),
        in_specs=[cur(NH * LANE), cur(LANE), prev, cur(LANE), prev, _full(NH, WIN, 2 * WIN), _full(NH, LANE)],
        out_specs=cur(NH * LANE), out_shape=_sds((S, NH * LANE), BF16), sem=("arbitrary",),
    )(qs, ks, ks, vs, vs, bias, sinks)


def _swa_bwd(qs, ks, vs, do, bias, sinks):
    S = qs.shape[0]
    nsteps = S // SWA_T
    step, cur, prev = _swa_specs(nsteps, True)
    bias_rows, sink_rows = _swa_rows(bias, sinks)

    def body(qs_ref, kc_ref, kp_ref, vc_ref, vp_ref, do_ref, bias_ref, sk_ref,
             dqs_ref, dks_ref, dvs_ref, dbias_ref, dsk_ref, dkk_s, dvv_s, ck_s, cv_s):
        pid = pl.program_id(0)
        first = step(pid) == 0

        @pl.when(pid == 0)
        def _():
            dbias_ref[...] = jnp.zeros_like(dbias_ref)
            dsk_ref[...] = jnp.zeros_like(dsk_ref)
            ck_s[...] = jnp.zeros_like(ck_s)
            cv_s[...] = jnp.zeros_like(cv_s)

        dkk_s[...] = jnp.zeros_like(dkk_s)
        dvv_s[...] = jnp.zeros_like(dvv_s)
        kk = jnp.concatenate([kp_ref[...], kc_ref[...]], axis=0)
        vv = jnp.concatenate([vp_ref[...], vc_ref[...]], axis=0)
        col = lax.broadcasted_iota(jnp.int32, (SWA_GR, 2 * WIN), 1)
        for b in range(SWA_SUB):
            kkb = kk[b * WIN:(b + 2) * WIN]
            vvb = vv[b * WIN:(b + 2) * WIN]
            fm = (first & (col < WIN)) if b == 0 else None
            rows = slice(b * WIN, (b + 1) * WIN)
            keys = slice(b * WIN, (b + 2) * WIN)
            qg = [_swa_group(qs_ref, rows, g) for g in range(2)]
            dog = [_swa_group(do_ref, rows, g) for g in range(2)]
            qks = [_dot_nt(qg[g], kkb) for g in range(2)]
            dps = [_dot_nt(dog[g], vvb) for g in range(2)]
            for g in range(2):
                grows = slice(g * SWA_GR, (g + 1) * SWA_GR)
                p, ps = _swa_probs(qks[g], bias_ref[grows, :], sk_ref[grows, 0:1], fm)
                dl = jnp.sum(p * dps[g], axis=1, keepdims=True)
                ds = p * (dps[g] - dl)
                sink_part = ps * dl
                for r in range(SWA_R):
                    h = g * SWA_R + r
                    dsk_ref[h:h + 1, :] += -jnp.sum(sink_part[r * WIN:(r + 1) * WIN])
                dbias_ref[grows, :] += ds
                dsb = (ds * SWA_SCALE).astype(BF16)
                dq = _dot(dsb, kkb).astype(BF16)
                for r in range(SWA_R):
                    dqs_ref[rows, _hs(g * SWA_R + r)] = dq[r * WIN:(r + 1) * WIN]
                dkk_s[keys, :] += _dot_tn(dsb, qg[g])
                dvv_s[keys, :] += _dot_tn(p.astype(BF16), dog[g])
        dks_ref[...] = dkk_s[WIN:, :]
        dvs_ref[...] = dvv_s[WIN:, :]
        dks_ref[SWA_T - WIN:, :] += ck_s[...]
        dvs_ref[SWA_T - WIN:, :] += cv_s[...]
        ck_s[...] = dkk_s[0:WIN, :]
        cv_s[...] = dvv_s[0:WIN, :]

    dqs, dks, dvs, dbias, dsink = _pcall(
        body, name="swa_bwd", grid=(nsteps,),
        in_specs=[cur(NH * LANE), cur(LANE), prev, cur(LANE), prev, cur(NH * LANE), _full(NH * WIN, 2 * WIN),
                  _full(NH * WIN, LANE)],
        out_specs=[cur(NH * LANE), cur(LANE), cur(LANE), _full(NH * WIN, 2 * WIN), _full(NH, LANE)],
        out_shape=[_sds((S, NH * LANE), BF16), _sds((S, LANE), F32), _sds((S, LANE), F32),
                   _sds((NH * WIN, 2 * WIN), F32), _sds((NH, LANE), F32)],
        scratch=[pltpu.VMEM((SWA_T + WIN, LANE), F32), pltpu.VMEM((SWA_T + WIN, LANE), F32),
                 pltpu.VMEM((WIN, LANE), F32), pltpu.VMEM((WIN, LANE), F32)],
        sem=("arbitrary",),
    )(qs, ks, ks, vs, vs, do, bias_rows, sink_rows)
    return dqs, dks, dvs, dbias.reshape(NH, WIN, 2 * WIN), dsink


def _bias_build(rel_bias, bmap):
    def body(rb_ref, bmap_ref, o_ref):
        bm = bmap_ref[...]
        for h in range(NH):
            acc = jnp.full((WIN, 2 * WIN), NEG, F32)
            for b in range(REL_BUCKETS):
                acc = jnp.where(bm == b, rb_ref[b, h], acc)
            o_ref[h] = acc

    return _pcall(
        body, name="bias_build", grid=(1,),
        in_specs=[pl.BlockSpec(memory_space=pltpu.SMEM), _full(WIN, 2 * WIN)],
        out_specs=_full(NH, WIN, 2 * WIN), out_shape=_sds((NH, WIN, 2 * WIN), F32), sem=("arbitrary",),
    )(rel_bias, bmap)


def _bias_reduce(dbias, bmap):
    def body(db_ref, bmap_ref, o_ref):
        bm = bmap_ref[...]
        for h in range(NH):
            dbh = db_ref[h]
            for b in range(REL_BUCKETS):
                o_ref[b, h] = jnp.sum(jnp.where(bm == b, dbh, 0.0))

    return _pcall(
        body, name="bias_reduce", grid=(1,),
        in_specs=[_full(NH, WIN, 2 * WIN), _full(WIN, 2 * WIN)],
        out_specs=pl.BlockSpec(memory_space=pltpu.SMEM), out_shape=_sds((REL_BUCKETS, NH), F32), sem=("arbitrary",),
    )(dbias, bmap)


def _memkv_fwd(mem, mnorm, wkv):
    def body(mem_ref, g_ref, w_ref, o_ref):
        n, _ = _rms(mem_ref[...])
        o_ref[...] = _dot((n * g_ref[...]).astype(BF16), w_ref[...]).astype(BF16)

    return _pcall(
        body, name="memkv_fwd", grid=(1,), in_specs=[_full(MEM_LEN, D), _full(1, D), _full(D, D)],
        out_specs=_full(MEM_LEN, D), out_shape=_sds((MEM_LEN, D), BF16), sem=("arbitrary",),
    )(mem, mnorm, wkv)


def _mem_probs(qk):
    s = qk * MEM_SCALE
    e = jnp.exp(s - jnp.max(s, axis=1, keepdims=True))
    return e / jnp.sum(e, axis=1, keepdims=True)


def _mem_fwd(qm, kvm, ts):
    S = qm.shape[0]

    def body(q_ref, kv_ref, o_ref):
        qks = [_dot_nt(q_ref[:, _hs(h)], kv_ref[:, _hs(h)]) for h in range(MEM_H)]
        for h in range(MEM_H):
            p = _mem_probs(qks[h])
            o_ref[:, _hs(h)] = _dot(p.astype(BF16), kv_ref[:, _hs(MEM_H + h)]).astype(BF16)

    return _pcall(
        body, name="mem_fwd", grid=(S // ts,), in_specs=[_tok(ts, 512), _full(MEM_LEN, D)],
        out_specs=_tok(ts, 512), out_shape=_sds((S, 512), BF16), sem=("arbitrary",),
    )(qm, kvm)


def _mem_bwd(qm, kvm, do, ts):
    S = qm.shape[0]

    def body(q_ref, kv_ref, do_ref, dq_ref, dkv_ref):
        @pl.when(pl.program_id(0) == 0)
        def _():
            dkv_ref[...] = jnp.zeros_like(dkv_ref)

        qks = [_dot_nt(q_ref[:, _hs(h)], kv_ref[:, _hs(h)]) for h in range(MEM_H)]
        dps = [_dot_nt(do_ref[:, _hs(h)], kv_ref[:, _hs(MEM_H + h)]) for h in range(MEM_H)]
        for h in range(MEM_H):
            qh, kh, doh = q_ref[:, _hs(h)], kv_ref[:, _hs(h)], do_ref[:, _hs(h)]
            p = _mem_probs(qks[h])
            dp = dps[h]
            ds = (p * (dp - jnp.sum(p * dp, axis=1, keepdims=True)) * MEM_SCALE).astype(BF16)
            dq_ref[:, _hs(h)] = _dot(ds, kh).astype(BF16)
            dkv_ref[:, _hs(h)] += _dot_tn(ds, qh)
            dkv_ref[:, _hs(MEM_H + h)] += _dot_tn(p.astype(BF16), doh)

    return _pcall(
        body, name="mem_bwd", grid=(S // ts,), in_specs=[_tok(ts, 512), _full(MEM_LEN, D), _tok(ts, 512)],
        out_specs=[_tok(ts, 512), _full(MEM_LEN, D)],
        out_shape=[_sds((S, 512), BF16), _sds((MEM_LEN, D), F32)], sem=("arbitrary",),
    )(qm, kvm, do)


def _memkv_bwd(mem, mnorm, wkv, dkvm):
    def body(mem_ref, g_ref, w_ref, dkv_ref, dw_ref, dg_ref):
        n, _ = _rms(mem_ref[...])
        dkvb = dkv_ref[...].astype(BF16)
        dw_ref[...] = _dot_tn((n * g_ref[...]).astype(BF16), dkvb)
        dg_ref[...] = _colsum(_dot_nt(dkvb, w_ref[...]) * n)

    return _pcall(
        body, name="memkv_bwd", grid=(1,), in_specs=[_full(MEM_LEN, D), _full(1, D), _full(D, D), _full(MEM_LEN, D)],
        out_specs=[_full(D, D), _full(1, D)], out_shape=[_sds((D, D), F32), _sds((1, D), F32)], sem=("arbitrary",),
    )(mem, mnorm, wkv, dkvm)


def _adamw(w, g, m, v, name):
    rows, cols = w.shape
    tr = min(rows, FLAT_TILE)
    assert rows % tr == 0

    def body(w_ref, g_ref, m_ref, v_ref, d_ref, nm_ref, nv_ref):
        gv = g_ref[...]
        nm = ADAM_B1 * m_ref[...] + (1.0 - ADAM_B1) * gv
        nv = ADAM_B2 * v_ref[...] + (1.0 - ADAM_B2) * jnp.square(gv)
        m_hat = nm / (1.0 - ADAM_B1 ** ADAM_STEP)
        v_hat = nv / (1.0 - ADAM_B2 ** ADAM_STEP)
        d_ref[...] = -ADAM_LR * (m_hat / (jnp.sqrt(v_hat) + ADAM_EPS) + ADAM_WD * w_ref[...])
        nm_ref[...] = nm
        nv_ref[...] = nv

    spec = _tok(tr, cols)
    return _pcall(
        body, name=name, grid=(rows // tr,), in_specs=[spec] * 4, out_specs=[spec] * 3,
        out_shape=[_sds((rows, cols), F32)] * 3, sem=("arbitrary",),
    )(w, g, m, v)


def _my_place():
    return lax.axis_index("x"), lax.axis_index("y"), lax.axis_index("c")


def _remote(src, dst, send_sems, recv_sems, k, to):
    return pltpu.make_async_remote_copy(src_ref=src, dst_ref=dst, send_sem=send_sems.at[k], recv_sem=recv_sems.at[k],
                                        device_id=to, device_id_type=MESH_ID)


def _gather_weights(shards):
    n = len(shards)

    def body(*refs):
        ins, outs = refs[:n], refs[n:2 * n]
        send_sems, recv_sems, local_sems = refs[2 * n:]
        x, y, c = _my_place()
        chips = [(1 - x, y), (x, 1 - y), (1 - x, 1 - y)]
        mine = [pltpu.make_async_copy(ins[i], outs[i].at[2 * x + y], local_sems.at[i]) for i in range(n)]
        for cp in mine:
            cp.start()

        def copy(i, k, slot, to):
            return _remote(ins[i], outs[i].at[slot], send_sems, recv_sems, 3 * i + k, to)

        sends = [copy(i, k, 2 * x + y, (px, py, c)) for i in range(n) for k, (px, py) in enumerate(chips)]
        for cp in sends:
            cp.start()
        for i in range(n):
            for k, (px, py) in enumerate(chips):
                copy(i, k, 2 * px + py, (px, py, c)).wait_recv()
        for cp in sends:
            cp.wait_send()
        for cp in mine:
            cp.wait()

    hbm = pl.BlockSpec(memory_space=pl.ANY)
    return pl.pallas_call(
        body, name="gather_weights", out_shape=[_sds((4,) + s.shape, s.dtype) for s in shards],
        in_specs=[hbm] * n, out_specs=[hbm] * n,
        scratch_shapes=[pltpu.SemaphoreType.DMA((3 * n,)), pltpu.SemaphoreType.DMA((3 * n,)),
                        pltpu.SemaphoreType.DMA((n,))],
    )(*shards)


def _scatter_grads(gws, gsmall):
    n = len(gws)

    def body(*refs):
        ins, gs_ref = refs[:n], refs[n]
        outs, rs_ref = refs[n + 1:2 * n + 1], refs[2 * n + 1]
        send_sems, recv_sems, local_sems = refs[2 * n + 2:]
        x, y, c = _my_place()
        me = 4 * x + 2 * y + c
        mine = [pltpu.make_async_copy(ins[i].at[c, 2 * x + y], outs[i].at[me], local_sems.at[i]) for i in range(n)]
        mine.append(pltpu.make_async_copy(gs_ref, rs_ref.at[me], local_sems.at[n]))
        for cp in mine:
            cp.start()

        def peer(m):
            return (x ^ ((m >> 2) & 1), y ^ ((m >> 1) & 1), c ^ (m & 1))

        def big(i, m, layer, shard, slot, to):
            return _remote(ins[i].at[layer, shard], outs[i].at[slot], send_sems, recv_sems, (m - 1) * (n + 1) + i, to)

        def small(m, slot, to):
            return _remote(gs_ref, rs_ref.at[slot], send_sems, recv_sems, (m - 1) * (n + 1) + n, to)

        sends = []
        for m in range(1, 8):
            px, py, pc = peer(m)
            sends += [big(i, m, pc, 2 * px + py, me, (px, py, pc)) for i in range(n)]
            sends.append(small(m, me, (px, py, pc)))
        for cp in sends:
            cp.start()
        for m in range(1, 8):
            px, py, pc = peer(m)
            them = 4 * px + 2 * py + pc
            for i in range(n):
                big(i, m, c, 2 * x + y, them, (px, py, pc)).wait_recv()
            small(m, them, (px, py, pc)).wait_recv()
        for cp in sends:
            cp.wait_send()
        for cp in mine:
            cp.wait()

    hbm = pl.BlockSpec(memory_space=pl.ANY)
    nsem = 7 * (n + 1)
    return pl.pallas_call(
        body, name="scatter_grads",
        out_shape=[_sds((8,) + g.shape[2:], g.dtype) for g in gws] + [_sds((8,) + gsmall.shape, gsmall.dtype)],
        in_specs=[hbm] * (n + 1), out_specs=[hbm] * (n + 1),
        scratch_shapes=[pltpu.SemaphoreType.DMA((nsem,)), pltpu.SemaphoreType.DMA((nsem,)),
                        pltpu.SemaphoreType.DMA((n + 1,))],
    )(*gws, gsmall)


def _sum_slots(slots, name):
    _, r, c = slots.shape
    tr = min(r, FLAT_TILE)
    assert r % tr == 0

    def body(s_ref, o_ref):
        acc = s_ref[0].astype(F32)
        for d in range(1, 8):
            acc = acc + s_ref[d].astype(F32)
        o_ref[...] = acc

    return _pcall(
        body, name=name, grid=(r // tr,), in_specs=[pl.BlockSpec((8, tr, c), lambda i: (0, i, 0))],
        out_specs=_tok(tr, c), out_shape=_sds((r, c), F32), sem=("arbitrary",),
    )(slots)


def _swap_layers(reds):
    n = len(reds)
    tiles = [min(r.shape[0], FLAT_TILE) for r in reds]

    def body(*refs):
        ins, outs, bufs = refs[:n], refs[n:2 * n], refs[2 * n:3 * n]
        load_sems, send_sems, recv_sems, local_sems = refs[3 * n:]
        x, y, c = _my_place()
        sibling = (x, y, 1 - c)
        mine = [pltpu.make_async_copy(ins[i], outs[i].at[c], local_sems.at[i]) for i in range(n)]
        for cp in mine:
            cp.start()
        in_flight = []
        for i in range(n):
            tr = tiles[i]
            busy = {}
            for t in range(reds[i].shape[0] // tr):
                slot, rows = t % 2, pl.ds(t * tr, tr)
                if slot in busy:
                    busy[slot].wait_send()
                load = pltpu.make_async_copy(ins[i].at[rows], bufs[i].at[slot], load_sems.at[slot])
                load.start()
                load.wait()
                busy[slot] = pltpu.make_async_remote_copy(
                    src_ref=bufs[i].at[slot], dst_ref=outs[i].at[c, rows], send_sem=send_sems.at[2 * i + slot],
                    recv_sem=recv_sems.at[i], device_id=sibling, device_id_type=MESH_ID)
                busy[slot].start()
            in_flight += list(busy.values())
        for i in range(n):
            _remote(ins[i], outs[i].at[1 - c], send_sems, recv_sems, i, sibling).wait_recv()
        for cp in in_flight:
            cp.wait_send()
        for cp in mine:
            cp.wait()

    hbm = pl.BlockSpec(memory_space=pl.ANY)
    return pl.pallas_call(
        body, name="swap_layers", out_shape=[_sds((2,) + r.shape, r.dtype) for r in reds],
        in_specs=[hbm] * n, out_specs=[hbm] * n,
        scratch_shapes=[pltpu.VMEM((2, tiles[i]) + reds[i].shape[1:], reds[i].dtype) for i in range(n)]
        + [pltpu.SemaphoreType.DMA((2,)), pltpu.SemaphoreType.DMA((2 * n,)), pltpu.SemaphoreType.DMA((n,)),
           pltpu.SemaphoreType.DMA((n,))],
        compiler_params=pltpu.CompilerParams(vmem_limit_bytes=VMEM_LIMIT),
    )(*reds)


W_IN_SHARD, W_IN_SHARD_PAD = 1192, 1280
W_UQ_SHARD, W_UQ_SHARD_PAD = 192, 256


def _swa_place(t):
    z = jnp.zeros_like(t)
    lo = jnp.concatenate([t, z], axis=1)
    hi = jnp.concatenate([z, t], axis=1)
    group = (jnp.arange(NH) // SWA_R).reshape((NH,) + (1,) * (t.ndim - 1))
    full = jnp.where(group == 0, lo, hi)
    return full.reshape((NH * LANE,) + t.shape[2:])


def _swa_unplace(t):
    t = t.reshape((NH, 2, 64) + t.shape[1:])
    return jnp.concatenate([t[:SWA_R, 0], t[SWA_R:, 1]], axis=0)


def _pad_w_o_mla(w):
    return jnp.pad(w.reshape(NH, 64, D), ((0, 0), (0, 64), (0, 0))).reshape(NH * LANE, D)


def _unpad_w_o_mla(g):
    return g.reshape(NH, LANE, D)[:, :64].reshape(NH * 64, D)


def _w_in_cols():
    src = np.full((C_END,), -1, np.int64)
    src[C_CQ:C_KPE] = np.arange(0, 384)
    src[C_KPE + 64:C_KPE + 96] = np.arange(384, 416)
    for h in range(NH):
        at = C_QS + h * LANE + 64 * (h // SWA_R)
        src[at:at + 64] = 416 + h * 64 + np.arange(64)
    src[C_KS:C_END] = np.arange(928, IN_COLS)
    return src


def _w_uq_cols():
    src = np.full((NH * LANE,), -1, np.int64)
    for h in range(NH):
        src[h * LANE:h * LANE + 96] = h * 96 + np.arange(96)
    return src


def _w_ukv_cols():
    src = np.full((2 * NH * LANE,), -1, np.int64)
    for h in range(NH):
        src[h * LANE:h * LANE + 64] = h * 128 + np.arange(64)
        src[NH * LANE + h * LANE:NH * LANE + h * LANE + 64] = h * 128 + 64 + np.arange(64)
    return src


def _selection(src_cols, width, width_pad):
    want = jnp.asarray(np.asarray(src_cols, np.int32))[None, None, :]
    k = jnp.arange(width_pad, dtype=jnp.int32)[None, :, None]
    have = jnp.where(k < width, jnp.arange(4, dtype=jnp.int32)[:, None, None] * width + k, -2)
    return (want == have).astype(BF16)


def _selections():
    return dict(w_in=_selection(_w_in_cols(), W_IN_SHARD, W_IN_SHARD_PAD),
                w_uq=_selection(_w_uq_cols(), W_UQ_SHARD, W_UQ_SHARD_PAD),
                w_ukv=_selection(_w_ukv_cols(), 256, 256))


def _pad_last(a, width):
    return jnp.pad(a, ((0, 0),) * (a.ndim - 1) + ((0, width - a.shape[-1]),))


def _wire_shards(W):
    out = {n: W[n].astype(BF16) for n in SHARDED}
    out["w_in"] = _pad_last(out["w_in"], W_IN_SHARD_PAD)
    out["w_uq"] = _pad_last(out["w_uq"], W_UQ_SHARD_PAD)
    return out


def _join_shards(t, axis):
    _, L, r, c = t.shape
    if axis == 2:
        return t.transpose(1, 2, 0, 3).reshape(L, r, 4 * c)
    return t.transpose(1, 0, 2, 3).reshape(L, 4 * r, c)


def _kernel_weights(gathered, sels):
    win = _select_fwd(gathered["w_in"], sels["w_in"], "lay_w_in")
    wuq = _select_fwd(gathered["w_uq"], sels["w_uq"], "lay_w_uq")
    wukv = _select_fwd(gathered["w_ukv"], sels["w_ukv"], "lay_w_ukv")
    whole = {n: _join_shards(gathered[n], SHARD_AXIS[n])
             for n in ("w_mem_kv", "w_o_mla", "w_o_swa", "w_o_mem", "w_out", "w_up", "w_down")}
    return [dict(win=win[l], wuq=wuq[l], wukv=wukv[l], wmem=whole["w_mem_kv"][l],
                 wa=_pad_w_o_mla(whole["w_o_mla"][l]), wb=_swa_place(whole["w_o_swa"][l].reshape(NH, 64, D)),
                 wc=whole["w_o_mem"][l], wout=whole["w_out"][l], wup=whole["w_up"][l], wdown=whole["w_down"][l])
            for l in range(DEPTH)]


def _cols_to_shards(g):
    r, c4 = g.shape
    return g.reshape(r, 4, c4 // 4).transpose(1, 0, 2).astype(BF16)


def _rope_tables(S):
    pos = jnp.arange(S, dtype=F32)
    inv = 1.0 / (ROPE_THETA ** (jnp.arange(0, 32, 2, dtype=F32) / 32))
    ang = pos[:, None] * inv[None, :]
    cos, sin = jnp.cos(ang), jnp.sin(ang)
    one, zero = jnp.ones((S, 64), F32), jnp.zeros((S, 16), F32)
    rc = jnp.concatenate([one, cos, cos, jnp.ones((S, 32), F32)], axis=1)
    rs1 = jnp.concatenate([jnp.zeros((S, 64), F32), zero, sin, jnp.zeros((S, 32), F32)], axis=1)
    rs2 = jnp.concatenate([jnp.zeros((S, 64), F32), -sin, zero, jnp.zeros((S, 32), F32)], axis=1)
    return rc, rs1, rs2


def _bucket_map():
    qi = jnp.arange(WIN)[:, None]
    kj = jnp.arange(2 * WIN)[None, :]
    dist = qi + WIN - kj
    n = jnp.maximum(dist, 0)
    max_exact = REL_BUCKETS // 2
    nf = jnp.maximum(n, 1).astype(F32)
    large = max_exact + (jnp.log(nf / max_exact) / math.log(128 / max_exact)
                         * (REL_BUCKETS - max_exact)).astype(jnp.int32)
    large = jnp.minimum(large, REL_BUCKETS - 1)
    bucket = jnp.where(n < max_exact, n, large)
    return jnp.where((dist >= 0) & (dist < WIN), bucket, -1).astype(jnp.int32)


TS = 256
TQ = 1024
TQ_FWD = 1024


def _local_step(x, mem, tgt, kw, sp, sels):
    S = x.shape[0]
    ts = min(TS, S)
    tq = min(TQ, S)
    rc, rs1, rs2 = _rope_tables(S)
    bmap = _bucket_map()
    bias = _bias_build(sp["rel_bias"], bmap)
    row = lambda v: v.reshape(1, -1)

    saved = []
    for l in range(DEPTH):
        w = kw[l]
        an, qn, kvn = row(sp["attn_norm"][l]), row(sp["mla_q_norm"][l]), row(sp["mla_kv_norm"][l])
        bg, mnorm, mlpn = row(sp["b_gate"][l]), row(sp["mem_norm"][l]), row(sp["mlp_norm"][l])
        sinks = jnp.broadcast_to(sp["attn_sinks"][l][:, None], (NH, LANE))
        q, k, v, qs, ks, vs, qm, g = _pre_fwd(x, an, w["win"], bg, qn, kvn, w["wuq"], w["wukv"], rc, rs1, rs2, ts)
        oa, lse = _mla_fwd(q, k, v, min(TQ_FWD, S))
        ob = _swa_fwd(qs, ks, vs, bias, sinks)
        kvm = _memkv_fwd(mem, mnorm, w["wmem"])
        oc = _mem_fwd(qm, kvm, ts)
        x1, yb = _merge_fwd(x, g, oa, ob, oc, w["wa"], w["wb"], w["wc"], w["wout"], ts)
        x2 = _mlp_fwd(x1, mlpn, w["wup"], w["wdown"], ts)
        saved.append(dict(w=w, x=x, x1=x1, q=q, k=k, v=v, qs=qs, ks=ks, vs=vs, qm=qm, g=g, oa=oa, lse=lse, ob=ob,
                          oc=oc, kvm=kvm, yb=yb, an=an, qn=qn, kvn=kvn, mnorm=mnorm, mlpn=mlpn, sinks=sinks))
        x = x2

    sq, dx, dfn = _loss_kernel(x, row(sp["final_norm"]), tgt, ts)

    big = {n: [None] * DEPTH for n in SHARDED}
    small = {n: [None] * DEPTH for n in SMALL if n not in ("rel_bias", "final_norm")}
    dbias_total = None
    for l in reversed(range(DEPTH)):
        sv = saved[l]
        w = sv["w"]
        dx1, hb2, dub, ab, dxb, dmlpn = _mlp_bwd(dx, sv["x1"], sv["mlpn"], w["wup"], w["wdown"], ts)
        big["w_up"][l] = _matmul_tn(hb2, dub, "dw_up", shard_axis=1)
        big["w_down"][l] = _matmul_tn(ab, dxb, "dw_down", shard_axis=0)
        small["mlp_norm"][l] = dmlpn[0]

        dgp, dyo, doa, dob, doc, dla, dx1b, dbg = _merge_bwd(dx1, sv["g"], sv["oa"], sv["ob"], sv["oc"], w["wa"],
                                                             w["wb"], w["wc"], w["wout"], ts)
        big["w_out"][l] = _matmul_tn(sv["yb"], dx1b, "dw_out", shard_axis=0)
        big["w_o_mla"][l] = _cols_to_shards(_unpad_w_o_mla(_matmul_tn(sv["oa"], dyo[:, 0:D], "dw_o_mla")))
        big["w_o_swa"][l] = _cols_to_shards(
            _swa_unplace(_matmul_tn(sv["ob"], dyo[:, D:2 * D], "dw_o_swa")).reshape(NH * 64, D))
        big["w_o_mem"][l] = _matmul_tn(sv["oc"], dyo[:, 2 * D:3 * D], "dw_o_mem", shard_axis=1)
        small["b_gate"][l] = dbg[0]

        dqm, dkvm = _mem_bwd(sv["qm"], sv["kvm"], doc, ts)
        dwmem, dmnorm = _memkv_bwd(mem, sv["mnorm"], w["wmem"], dkvm)
        big["w_mem_kv"][l] = dwmem.reshape(4, D // 4, D).astype(BF16)
        small["mem_norm"][l] = dmnorm[0]

        dqs, dks, dvs, dbias, dsink = _swa_bwd(sv["qs"], sv["ks"], sv["vs"], dob, bias, sv["sinks"])
        dbias_total = dbias if dbias_total is None else dbias_total + dbias
        small["attn_sinks"][l] = dsink[:, 0]

        dq, dk, dv = _mla_bwd(sv["q"], sv["k"], sv["v"], doa, sv["lse"], dla, tq)

        dx, dproj, hb, cqn, ckvn, dqpre, dkv, dan, dqn, dkvn = _pre_bwd(
            sv["x"], dx1, dq, dk, dv, dqs, dks, dvs, dqm, dgp, sv["an"], sv["qn"], sv["kvn"], w["win"], w["wuq"],
            w["wukv"], rc, rs1, rs2, ts)
        big["w_in"][l] = _matmul_tn(hb, dproj, "dw_in")
        big["w_uq"][l] = _matmul_tn(cqn, dqpre, "dw_uq")
        big["w_ukv"][l] = _matmul_tn(ckvn, dkv, "dw_ukv")
        small["attn_norm"][l] = dan[0]
        small["mla_q_norm"][l] = dqn[0]
        small["mla_kv_norm"][l] = dkvn[0]

    gw = {n: jnp.stack(v) for n, v in big.items()}
    for n in ("w_in", "w_uq", "w_ukv"):
        gw[n] = _select_bwd(gw[n], sels[n], "shard_d" + n)
    gs = {n: jnp.stack(v) for n, v in small.items()}
    gs["rel_bias"] = _bias_reduce(dbias_total, bmap)
    gs["final_norm"] = dfn[0]
    return sq, dx, gw, gs


def _flatten(parts, rows):
    flat = jnp.concatenate([p.reshape(-1) for p in parts])
    return jnp.pad(flat, (0, rows * FLAT_W - flat.shape[0])).reshape(rows, FLAT_W)


def _unflatten(buf, shapes):
    flat = buf.reshape(-1)
    out, at = [], 0
    for s in shapes:
        n = int(np.prod(s))
        out.append(flat[at:at + n].reshape(s))
        at += n
    return out


def kernel(x, mem, rel_bias, attn_norm, mem_norm, w_in, b_gate, mla_q_norm, w_uq, mla_kv_norm, w_ukv, attn_sinks, w_mem_kv, w_o_mla, w_o_swa, w_o_mem, w_out, mlp_norm, w_up, w_down, final_norm, loss_target, m_rel_bias, m_attn_norm, m_mem_norm, m_w_in, m_b_gate, m_mla_q_norm, m_w_uq, m_mla_kv_norm, m_w_ukv, m_attn_sinks, m_w_mem_kv, m_w_o_mla, m_w_o_swa, m_w_o_mem, m_w_out, m_mlp_norm, m_w_up, m_w_down, m_final_norm, v_rel_bias, v_attn_norm, v_mem_norm, v_w_in, v_b_gate, v_mla_q_norm, v_w_uq, v_mla_kv_norm, v_w_ukv, v_attn_sinks, v_w_mem_kv, v_w_o_mla, v_w_o_swa, v_w_o_mem, v_w_out, v_mlp_norm, v_w_up, v_w_down, v_final_norm):
    args = dict(locals())
    W = {n: args[n] for n in WEIGHTS}
    M = {n: args["m_" + n] for n in WEIGHTS}
    V = {n: args["v_" + n] for n in WEIGHTS}
    small_shapes = [W[n].shape for n in SMALL]
    sels = _selections()

    wire = _wire_shards(W)
    gathered = dict(zip(SHARDED, _gather_weights([wire[n] for n in SHARDED])))
    kw = _kernel_weights(gathered, sels)
    sp = {n: W[n] for n in SMALL}

    sq, dx, gw, gs = _local_step(x[0], mem[0], loss_target[0], kw, sp, sels)

    gsmall = _flatten([gs[n] for n in SMALL], SMALL_ROWS)
    *slots, small_slots = _scatter_grads([gw[n] for n in SHARDED], gsmall)
    reds = [_sum_slots(s, "sum_" + n) for n, s in zip(SHARDED, slots)]
    red_small = _sum_slots(small_slots, "sum_small")
    G = dict(zip(SHARDED, _swap_layers(reds)))
    G["w_in"] = G["w_in"][..., :W_IN_SHARD]
    G["w_uq"] = G["w_uq"][..., :W_UQ_SHARD]

    DW, NM, NV = {}, {}, {}
    for n in SHARDED:
        shape = W[n].shape
        two_d = lambda a: a.reshape(-1, shape[-1])
        d, nm, nv = _adamw(two_d(W[n]), two_d(G[n]), two_d(M[n]), two_d(V[n]), "adamw_" + n)
        DW[n], NM[n], NV[n] = d.reshape(shape), nm.reshape(shape), nv.reshape(shape)
    d_s, m_s, v_s = _adamw(_flatten([W[n] for n in SMALL], SMALL_ROWS), red_small,
                           _flatten([M[n] for n in SMALL], SMALL_ROWS), _flatten([V[n] for n in SMALL], SMALL_ROWS),
                           "adamw_small")
    for out, buf in ((G, red_small), (DW, d_s), (NM, m_s), (NV, v_s)):
        out.update(zip(SMALL, _unflatten(buf, small_shapes)))
    loss = lax.psum(0.5 * sq[0, 0] / D, ("x", "y", "c"))
    return (loss, dx[None], *[G[n] for n in WEIGHTS], *[DW[n] for n in WEIGHTS], *[NM[n] for n in WEIGHTS],
            *[NV[n] for n in WEIGHTS])
```

```python
import functools
import math
from typing import Callable, NamedTuple

import numpy as np
import jax
import jax.numpy as jnp
from jax import lax
from jax.experimental import pallas as pl
from jax.experimental.pallas import tpu as pltpu

F32 = jnp.float32
BF16 = jnp.bfloat16

D = 1024
DFF = 4096
DEPTH = 2
EPS = 1e-6
LANE = 128
NH = 8
SWA_R = 4
MEM_H = 4
MEM_LEN = 256
WIN = 128
NEG = -1e30
MLA_SCALE = 96 ** -0.5
SWA_SCALE = 64 ** -0.5
MEM_SCALE = 128 ** -0.5
REL_BUCKETS = 32
ROPE_THETA = 10000.0

C_CQ, C_CKV, C_KPE, C_QS, C_KS, C_VS, C_QM, C_G, C_END = 0, 256, 384, 512, 1536, 1664, 1792, 2304, 5376
IN_COLS = 4768

ADAM_LR = 0.001
ADAM_B1 = 0.9
ADAM_B2 = 0.999
ADAM_EPS = 1e-08
ADAM_WD = 0.01
ADAM_STEP = 10

VMEM_LIMIT = 56 * 1024 * 1024

SHARDED = ("w_in", "w_uq", "w_ukv", "w_mem_kv", "w_o_mla", "w_o_swa", "w_o_mem", "w_out", "w_up", "w_down")
SHARD_AXIS = {"w_in": 2, "w_uq": 2, "w_ukv": 2, "w_mem_kv": 1, "w_o_mla": 2, "w_o_swa": 2, "w_o_mem": 2,
              "w_out": 1, "w_up": 2, "w_down": 1}
SMALL = ("rel_bias", "attn_norm", "mem_norm", "b_gate", "mla_q_norm", "mla_kv_norm", "attn_sinks", "mlp_norm",
         "final_norm")
WEIGHTS = ("rel_bias", "attn_norm", "mem_norm", "w_in", "b_gate", "mla_q_norm", "w_uq", "mla_kv_norm", "w_ukv",
           "attn_sinks", "w_mem_kv", "w_o_mla", "w_o_swa", "w_o_mem", "w_out", "mlp_norm", "w_up", "w_down",
           "final_norm")
FLAT_W = 1024
FLAT_TILE = 256
SMALL_ROWS = 16
MESH_ID = pl.DeviceIdType.MESH


def _pcall(body, *, name, grid, in_specs, out_specs, out_shape, scratch=(), prefetch=0, sem=None):
    params = pltpu.CompilerParams(dimension_semantics=sem, vmem_limit_bytes=VMEM_LIMIT)
    if prefetch:
        spec = pltpu.PrefetchScalarGridSpec(num_scalar_prefetch=prefetch, grid=grid, in_specs=in_specs,
                                            out_specs=out_specs, scratch_shapes=scratch)
        return pl.pallas_call(body, name=name, grid_spec=spec, out_shape=out_shape, compiler_params=params)
    return pl.pallas_call(body, name=name, grid=grid, in_specs=in_specs, out_specs=out_specs, out_shape=out_shape,
                          scratch_shapes=scratch, compiler_params=params)


def _tok(ts, w):
    return pl.BlockSpec((ts, w), lambda i: (i, 0))


def _full(*shape):
    return pl.BlockSpec(shape, lambda *_: (0,) * len(shape))


def _sds(shape, dtype):
    return jax.ShapeDtypeStruct(shape, dtype)


def _dot(a, b):
    return jnp.dot(a, b, preferred_element_type=F32)


def _dot_nt(a, b):
    return lax.dot_general(a, b, (((1,), (1,)), ((), ())), preferred_element_type=F32)


def _dot_tn(a, b):
    return lax.dot_general(a, b, (((0,), (0,)), ((), ())), preferred_element_type=F32)


def _rms(x):
    r = lax.rsqrt(jnp.mean(x * x, axis=-1, keepdims=True) + EPS)
    return x * r, r


def _rms_bwd(dyg, n, r):
    return r * (dyg - n * jnp.mean(n * dyg, axis=-1, keepdims=True))


def _rope(t, c, s1, s2):
    return t * c + pltpu.roll(t, 16, 1) * s1 + pltpu.roll(t, LANE - 16, 1) * s2


def _rope_bwd(dy, c, s1, s2):
    return dy * c + pltpu.roll(dy * s1, LANE - 16, 1) + pltpu.roll(dy * s2, 16, 1)


def _hs(h):
    return slice(h * LANE, (h + 1) * LANE)


def _colsum(t):
    return jnp.sum(t, axis=0, keepdims=True)


def _pre_fwd(x, an, win, bg, qn, kvn, wuq, wukv, rc, rs1, rs2, ts):
    S = x.shape[0]

    def body(x_ref, an_ref, win_ref, bg_ref, qn_ref, kvn_ref, wuq_ref, wukv_ref, rc_ref, rs1_ref, rs2_ref,
             q_ref, k_ref, v_ref, qs_ref, ks_ref, vs_ref, qm_ref, g_ref):
        n, _ = _rms(x_ref[...])
        hb = (n * an_ref[...]).astype(BF16)
        pa = _dot(hb, win_ref[:, C_CQ:C_QS])
        ncq, _ = _rms(pa[:, 0:256])
        cqn = (ncq * qn_ref[...]).astype(BF16)
        nkv, _ = _rms(pa[:, 256:384])
        ckvn = (nkv * kvn_ref[...]).astype(BF16)
        c, s1, s2 = rc_ref[...], rs1_ref[...], rs2_ref[...]
        kper = _rope(pa[:, 384:512], c, s1, s2)
        qp = _dot(cqn, wuq_ref[...])
        kv = _dot(ckvn, wukv_ref[...])
        for h in range(NH):
            q_ref[:, _hs(h)] = (_rope(qp[:, _hs(h)], c, s1, s2) * MLA_QSCALE).astype(BF16)
            k_ref[:, _hs(h)] = (kv[:, _hs(h)] + kper).astype(BF16)
        v_ref[...] = kv[:, NH * LANE:].astype(BF16)
        pb = _dot(hb, win_ref[:, C_QS:C_G])
        qs_ref[...] = pb[:, 0:1024].astype(BF16)
        ks_ref[...] = pb[:, 1024:1152].astype(BF16)
        vs_ref[...] = pb[:, 1152:1280].astype(BF16)
        qm_ref[...] = pb[:, 1280:1792].astype(BF16)
        g_ref[...] = jax.nn.sigmoid(_dot(hb, win_ref[:, C_G:C_END]) + bg_ref[...])

    return _pcall(
        body, name="pre_fwd", grid=(S // ts,),
        in_specs=[_tok(ts, D), _full(1, D), _full(D, C_END), _full(1, 3 * D), _full(1, 256), _full(1, 128),
                  _full(256, NH * LANE), _full(128, 2 * NH * LANE), _tok(ts, LANE), _tok(ts, LANE), _tok(ts, LANE)],
        out_specs=[_tok(ts, 1024), _tok(ts, 1024), _tok(ts, 1024), _tok(ts, 1024), _tok(ts, 128), _tok(ts, 128),
                   _tok(ts, 512), _tok(ts, 3 * D)],
        out_shape=[_sds((S, 1024), BF16), _sds((S, 1024), BF16), _sds((S, 1024), BF16), _sds((S, 1024), BF16),
                   _sds((S, 128), BF16), _sds((S, 128), BF16), _sds((S, 512), BF16), _sds((S, 3 * D), F32)],
        sem=("arbitrary",),
    )(x, an, win, bg, qn, kvn, wuq, wukv, rc, rs1, rs2)


def _merge_fwd(x, g, oa, ob, oc, wa, wb, wc, wout, ts):
    S = x.shape[0]

    def body(x_ref, g_ref, oa_ref, ob_ref, oc_ref, wa_ref, wb_ref, wc_ref, wout_ref, x1_ref, yb_ref):
        y = g_ref[:, 0:D] * _dot(oa_ref[...], wa_ref[...])
        y = y + g_ref[:, D:2 * D] * _dot(ob_ref[...], wb_ref[...])
        y = y + g_ref[:, 2 * D:3 * D] * _dot(oc_ref[...], wc_ref[...])
        yb = y.astype(BF16)
        yb_ref[...] = yb
        x1_ref[...] = x_ref[...] + _dot(yb, wout_ref[...])

    return _pcall(
        body, name="merge_fwd", grid=(S // ts,),
        in_specs=[_tok(ts, D), _tok(ts, 3 * D), _tok(ts, 1024), _tok(ts, 1024), _tok(ts, 512),
                  _full(1024, D), _full(1024, D), _full(512, D), _full(D, D)],
        out_specs=[_tok(ts, D), _tok(ts, D)],
        out_shape=[_sds((S, D), F32), _sds((S, D), BF16)],
        sem=("arbitrary",),
    )(x, g, oa, ob, oc, wa, wb, wc, wout)


def _mlp_fwd(x1, mn, wup, wdown, ts):
    S = x1.shape[0]

    def body(x_ref, mn_ref, wup_ref, wdown_ref, x2_ref):
        xv = x_ref[...]
        n, _ = _rms(xv)
        u = _dot((n * mn_ref[...]).astype(BF16), wup_ref[...])
        a = jnp.square(jnp.maximum(u, 0.0))
        x2_ref[...] = xv + _dot(a.astype(BF16), wdown_ref[...])

    return _pcall(
        body, name="mlp_fwd", grid=(S // ts,),
        in_specs=[_tok(ts, D), _full(1, D), _full(D, DFF), _full(DFF, D)],
        out_specs=_tok(ts, D), out_shape=_sds((S, D), F32), sem=("arbitrary",),
    )(x1, mn, wup, wdown)


def _loss_kernel(x, fn, tgt, ts):
    S = x.shape[0]

    def body(x_ref, fn_ref, t_ref, loss_ref, dx_ref, dfn_ref):
        @pl.when(pl.program_id(0) == 0)
        def _():
            loss_ref[...] = jnp.zeros_like(loss_ref)
            dfn_ref[...] = jnp.zeros_like(dfn_ref)

        n, r = _rms(x_ref[...])
        err = n * fn_ref[...] - t_ref[...]
        loss_ref[...] += jnp.sum(err * err)
        dy = err * (1.0 / D)
        dfn_ref[...] += _colsum(dy * n)
        dx_ref[...] = _rms_bwd(dy * fn_ref[...], n, r)

    return _pcall(
        body, name="loss_head", grid=(S // ts,),
        in_specs=[_tok(ts, D), _full(1, D), _tok(ts, D)],
        out_specs=[_full(1, LANE), _tok(ts, D), _full(1, D)],
        out_shape=[_sds((1, LANE), F32), _sds((S, D), F32), _sds((1, D), F32)],
        sem=("arbitrary",),
    )(x, fn, tgt)


def _mlp_bwd(dx2, x1, mn, wup, wdown, ts):
    S = x1.shape[0]

    def body(dx_ref, x_ref, mn_ref, wup_ref, wdown_ref, dx1_ref, hb_ref, dub_ref, ab_ref, dxb_ref, dmn_ref):
        @pl.when(pl.program_id(0) == 0)
        def _():
            dmn_ref[...] = jnp.zeros_like(dmn_ref)

        dx = dx_ref[...]
        n, r = _rms(x_ref[...])
        g = mn_ref[...]
        hb = (n * g).astype(BF16)
        hb_ref[...] = hb
        rl = jnp.maximum(_dot(hb, wup_ref[...]), 0.0)
        ab_ref[...] = jnp.square(rl).astype(BF16)
        dxb = dx.astype(BF16)
        dxb_ref[...] = dxb
        dub = (_dot_nt(dxb, wdown_ref[...]) * (2.0 * rl)).astype(BF16)
        dub_ref[...] = dub
        dh = _dot_nt(dub, wup_ref[...])
        dmn_ref[...] += _colsum(dh * n)
        dx1_ref[...] = dx + _rms_bwd(dh * g, n, r)

    return _pcall(
        body, name="mlp_bwd", grid=(S // ts,),
        in_specs=[_tok(ts, D), _tok(ts, D), _full(1, D), _full(D, DFF), _full(DFF, D)],
        out_specs=[_tok(ts, D), _tok(ts, D), _tok(ts, DFF), _tok(ts, DFF), _tok(ts, D), _full(1, D)],
        out_shape=[_sds((S, D), F32), _sds((S, D), BF16), _sds((S, DFF), BF16), _sds((S, DFF), BF16),
                   _sds((S, D), BF16), _sds((1, D), F32)],
        sem=("arbitrary",),
    )(dx2, x1, mn, wup, wdown)


def _merge_bwd(dx1, g, oa, ob, oc, wa, wb, wc, wout, ts):
    S = dx1.shape[0]

    def body(dx_ref, g_ref, oa_ref, ob_ref, oc_ref, wa_ref, wb_ref, wc_ref, wout_ref,
             dgp_ref, dyo_ref, doa_ref, dob_ref, doc_ref, dla_ref, dxb_ref, dbg_ref):
        @pl.when(pl.program_id(0) == 0)
        def _():
            dbg_ref[...] = jnp.zeros_like(dbg_ref)

        dxb = dx_ref[...].astype(BF16)
        dxb_ref[...] = dxb
        dy = _dot_nt(dxb, wout_ref[...])
        branches = ((oa_ref, wa_ref, doa_ref), (ob_ref, wb_ref, dob_ref), (oc_ref, wc_ref, doc_ref))
        for b, (o_ref, w_ref, do_ref) in enumerate(branches):
            cols = slice(b * D, (b + 1) * D)
            gb = g_ref[:, cols]
            o = o_ref[...]
            dgpre = dy * _dot(o, w_ref[...]) * gb * (1.0 - gb)
            dgp_ref[:, cols] = dgpre.astype(BF16)
            dbg_ref[:, cols] += _colsum(dgpre)
            dyo = (dy * gb).astype(BF16)
            dyo_ref[:, cols] = dyo
            do = _dot_nt(dyo, w_ref[...])
            do_ref[...] = do.astype(BF16)
            if b == 0:
                lane = lax.broadcasted_iota(jnp.int32, (ts, LANE), 1)
                dls = jnp.zeros((ts, LANE), F32)
                for h in range(NH):
                    dl = jnp.sum(do[:, _hs(h)] * o[:, _hs(h)].astype(F32), axis=1, keepdims=True)
                    dls = jnp.where(lane == h, dl, dls)
                dla_ref[:, 0, :] = jnp.transpose(dls)[0:NH, :]

    return _pcall(
        body, name="merge_bwd", grid=(S // ts,),
        in_specs=[_tok(ts, D), _tok(ts, 3 * D), _tok(ts, 1024), _tok(ts, 1024), _tok(ts, 512),
                  _full(1024, D), _full(1024, D), _full(512, D), _full(D, D)],
        out_specs=[_tok(ts, 3 * D), _tok(ts, 3 * D), _tok(ts, 1024), _tok(ts, 1024), _tok(ts, 512),
                   pl.BlockSpec((NH, 1, ts), lambda i: (0, 0, i)), _tok(ts, D), _full(1, 3 * D)],
        out_shape=[_sds((S, 3 * D), BF16), _sds((S, 3 * D), BF16), _sds((S, 1024), BF16), _sds((S, 1024), BF16),
                   _sds((S, 512), BF16), _sds((NH, 1, S), F32), _sds((S, D), BF16), _sds((1, 3 * D), F32)],
        sem=("arbitrary",),
    )(dx1, g, oa, ob, oc, wa, wb, wc, wout)


def _pre_bwd(x, dx1, dq, dk, dv, dqs, dks, dvs, dqm, dgp, an, qn, kvn, win, wuq, wukv, rc, rs1, rs2, ts):
    S = x.shape[0]

    def body(x_ref, dx1_ref, dq_ref, dk_ref, dv_ref, dqs_ref, dks_ref, dvs_ref, dqm_ref, dgp_ref,
             an_ref, qn_ref, kvn_ref, win_ref, wuq_ref, wukv_ref, rc_ref, rs1_ref, rs2_ref,
             dx_ref, dproj_ref, hb_ref, cqn_ref, ckvn_ref, dqpre_ref, dkv_ref, dan_ref, dqn_ref, dkvn_ref):
        @pl.when(pl.program_id(0) == 0)
        def _():
            dan_ref[...] = jnp.zeros_like(dan_ref)
            dqn_ref[...] = jnp.zeros_like(dqn_ref)
            dkvn_ref[...] = jnp.zeros_like(dkvn_ref)

        n, r = _rms(x_ref[...])
        hb = (n * an_ref[...]).astype(BF16)
        hb_ref[...] = hb
        pa = _dot(hb, win_ref[:, C_CQ:C_KPE])
        ncq, rq = _rms(pa[:, 0:256])
        cqn_ref[...] = (ncq * qn_ref[...]).astype(BF16)
        nkv, rkv = _rms(pa[:, 256:384])
        ckvn_ref[...] = (nkv * kvn_ref[...]).astype(BF16)
        c, s1, s2 = rc_ref[...], rs1_ref[...], rs2_ref[...]

        dkper = jnp.zeros((ts, LANE), F32)
        for h in range(NH):
            dqpre_ref[:, _hs(h)] = _rope_bwd(dq_ref[:, _hs(h)], c, s1, s2).astype(BF16)
            dkh = dk_ref[:, _hs(h)]
            dkper = dkper + dkh
            dkv_ref[:, _hs(h)] = dkh.astype(BF16)
        dkv_ref[:, NH * LANE:] = dv_ref[...].astype(BF16)

        dcqn = _dot_nt(dqpre_ref[...], wuq_ref[...])
        dqn_ref[...] += _colsum(dcqn * ncq)
        dproj_ref[:, C_CQ:C_CKV] = _rms_bwd(dcqn * qn_ref[...], ncq, rq).astype(BF16)
        dckvn = _dot_nt(dkv_ref[...], wukv_ref[...])
        dkvn_ref[...] += _colsum(dckvn * nkv)
        dproj_ref[:, C_CKV:C_KPE] = _rms_bwd(dckvn * kvn_ref[...], nkv, rkv).astype(BF16)
        lane = lax.broadcasted_iota(jnp.int32, (ts, LANE), 1)
        dkpe = jnp.where((lane >= 64) & (lane < 96), _rope_bwd(dkper, c, s1, s2), 0.0)
        dproj_ref[:, C_KPE:C_QS] = dkpe.astype(BF16)
        dproj_ref[:, C_QS:C_KS] = dqs_ref[...]
        dproj_ref[:, C_KS:C_VS] = dks_ref[...].astype(BF16)
        dproj_ref[:, C_VS:C_QM] = dvs_ref[...].astype(BF16)
        dproj_ref[:, C_QM:C_G] = dqm_ref[...]
        dproj_ref[:, C_G:C_END] = dgp_ref[...]

        dh = _dot_nt(dproj_ref[...], win_ref[...])
        dan_ref[...] += _colsum(dh * n)
        dx_ref[...] = dx1_ref[...] + _rms_bwd(dh * an_ref[...], n, r)

    return _pcall(
        body, name="pre_bwd", grid=(S // ts,),
        in_specs=[_tok(ts, D), _tok(ts, D), _tok(ts, 1024), _tok(ts, 1024), _tok(ts, 1024), _tok(ts, 1024),
                  _tok(ts, 128), _tok(ts, 128), _tok(ts, 512), _tok(ts, 3 * D),
                  _full(1, D), _full(1, 256), _full(1, 128), _full(D, C_END), _full(256, NH * LANE),
                  _full(128, 2 * NH * LANE), _tok(ts, LANE), _tok(ts, LANE), _tok(ts, LANE)],
        out_specs=[_tok(ts, D), _tok(ts, C_END), _tok(ts, D), _tok(ts, 256), _tok(ts, 128), _tok(ts, 1024),
                   _tok(ts, 2048), _full(1, D), _full(1, 256), _full(1, 128)],
        out_shape=[_sds((S, D), F32), _sds((S, C_END), BF16), _sds((S, D), BF16), _sds((S, 256), BF16),
                   _sds((S, 128), BF16), _sds((S, 1024), BF16), _sds((S, 2048), BF16), _sds((1, D), F32),
                   _sds((1, 256), F32), _sds((1, 128), F32)],
        sem=("arbitrary",),
    )(x, dx1, dq, dk, dv, dqs, dks, dvs, dqm, dgp, an, qn, kvn, win, wuq, wukv, rc, rs1, rs2)


def _pick_tile(n, cap):
    best = LANE
    for t in range(LANE, min(n, cap) + 1, LANE):
        if n % t == 0:
            best = t
    return best


def _matmul_tn(a, b, name, shard_axis=None):
    S, M = a.shape
    N = b.shape[1]
    tm = _pick_tile(M // 4 if shard_axis == 0 else M, 1024)
    tn = _pick_tile(N // 4 if shard_axis == 1 else N, 2048)
    ts = min(S, 1024)
    nk = S // ts

    def body(a_ref, b_ref, o_ref, *acc):
        acc_ref = acc[0] if acc else o_ref

        @pl.when(pl.program_id(2) == 0)
        def _():
            acc_ref[...] = jnp.zeros_like(acc_ref)

        acc_ref[...] += _dot_tn(a_ref[...], b_ref[...])
        if acc:
            @pl.when(pl.program_id(2) == nk - 1)
            def _():
                o_ref[0] = acc_ref[...].astype(o_ref.dtype)

    if shard_axis is None:
        out_spec = pl.BlockSpec((tm, tn), lambda i, j, k: (i, j))
        out_shape, scratch = _sds((M, N), F32), ()
    elif shard_axis == 0:
        per = (M // 4) // tm
        out_spec = pl.BlockSpec((1, tm, tn), lambda i, j, k: (i // per, i % per, j))
        out_shape, scratch = _sds((4, M // 4, N), BF16), (pltpu.VMEM((tm, tn), F32),)
    else:
        per = (N // 4) // tn
        out_spec = pl.BlockSpec((1, tm, tn), lambda i, j, k: (j // per, i, j % per))
        out_shape, scratch = _sds((4, M, N // 4), BF16), (pltpu.VMEM((tm, tn), F32),)
    return _pcall(
        body, name=name, grid=(M // tm, N // tn, nk),
        in_specs=[pl.BlockSpec((ts, tm), lambda i, j, k: (k, i)), pl.BlockSpec((ts, tn), lambda i, j, k: (k, j))],
        out_specs=out_spec, out_shape=out_shape, scratch=scratch, sem=("parallel", "parallel", "arbitrary"),
    )(a, b)


def _select_fwd(a, sel, name):
    _, L, M, K = a.shape
    N = sel.shape[2]
    tn = _pick_tile(N, 1792)

    def body(a_ref, s_ref, o_ref, acc_ref):
        s = pl.program_id(2)

        @pl.when(s == 0)
        def _():
            acc_ref[...] = jnp.zeros_like(acc_ref)

        acc_ref[...] += _dot(a_ref[0, 0], s_ref[0])

        @pl.when(s == 3)
        def _():
            o_ref[0] = acc_ref[...].astype(BF16)

    return _pcall(
        body, name=name, grid=(L, N // tn, 4),
        in_specs=[pl.BlockSpec((1, 1, M, K), lambda l, j, s: (s, l, 0, 0)),
                  pl.BlockSpec((1, K, tn), lambda l, j, s: (s, 0, j))],
        out_specs=pl.BlockSpec((1, M, tn), lambda l, j, s: (l, 0, j)),
        out_shape=_sds((L, M, N), BF16), scratch=(pltpu.VMEM((M, tn), F32),),
        sem=("parallel", "parallel", "arbitrary"),
    )(a, sel)


def _select_bwd(dw, sel, name):
    L, M, N = dw.shape
    K = sel.shape[1]
    tk = _pick_tile(N, 1792)
    nk = N // tk

    def body(d_ref, s_ref, o_ref, acc_ref):
        k = pl.program_id(2)

        @pl.when(k == 0)
        def _():
            acc_ref[...] = jnp.zeros_like(acc_ref)

        acc_ref[...] += _dot_nt(d_ref[0].astype(BF16), s_ref[0])

        @pl.when(k == nk - 1)
        def _():
            o_ref[0, 0] = acc_ref[...].astype(BF16)

    return _pcall(
        body, name=name, grid=(L, 4, nk),
        in_specs=[pl.BlockSpec((1, M, tk), lambda l, s, k: (l, 0, k)),
                  pl.BlockSpec((1, K, tk), lambda l, s, k: (s, 0, k))],
        out_specs=pl.BlockSpec((1, 1, M, K), lambda l, s, k: (l, s, 0, 0)),
        out_shape=_sds((L, 4, M, K), BF16), scratch=(pltpu.VMEM((M, K), F32),),
        sem=("parallel", "parallel", "arbitrary"),
    )(dw, sel)


MLA_RC = 128
MLA_RC_FWD = 128
LOG2E = math.log2(math.e)
MLA_QSCALE = MLA_SCALE * LOG2E


def _ride_refs(ride, rest, n_out, n_scratch):
    ni, no = (len(ride.inputs), len(ride.out_shapes)) if ride else (0, 0)
    own_out = rest[ni:ni + n_out]
    own_scratch = rest[ni + n_out + no:ni + n_out + no + n_scratch]
    parts = rest[:ni], rest[ni + n_out:ni + n_out + no], rest[ni + n_out + no + n_scratch:]
    return own_out, own_scratch, parts


def _ride_specs(ride):
    hbm = pl.BlockSpec(memory_space=pl.ANY)
    if not ride:
        return [], [], [], [], []
    return ([hbm] * len(ride.inputs), [hbm] * len(ride.out_shapes), list(ride.out_shapes), list(ride.sems),
            list(ride.inputs))


def _mla_fwd(q, k, v, tq, ride=None):
    S = q.shape[0]
    nq = S // tq
    pairs = [(i, j) for i in range(nq) for j in range(i + 1)]
    qi = jnp.asarray(np.array([p[0] for p in pairs], np.int32))
    kj = jnp.asarray(np.array([p[1] for p in pairs], np.int32))

    def body(qi_ref, kj_ref, q_ref, k_ref, v_ref, *rest):
        (o_ref, lse_ref), (m_s, l_s, acc_s), riding = _ride_refs(ride, rest, 2, 3)
        t = pl.program_id(1)
        i, j = qi_ref[t], kj_ref[t]
        if ride:
            @pl.when((pl.program_id(0) == 0) & (t == 0))
            def _():
                ride.start(*riding)

        @pl.when(j == 0)
        def _():
            m_s[...] = jnp.full_like(m_s, NEG)
            l_s[...] = jnp.zeros_like(l_s)
            acc_s[...] = jnp.zeros_like(acc_s)

        def step(masked):
            rc = min(MLA_RC_FWD, tq)
            nc = tq // rc
            keys = [(c + 1) * rc if masked else tq for c in range(nc)]
            scores = [_dot_nt(q_ref[c * rc:(c + 1) * rc, :], k_ref[0:keys[c], :]) for c in range(nc)]
            for c in range(nc):
                rows = slice(c * rc, (c + 1) * rc)
                s = scores[c]
                if masked:
                    row = lax.broadcasted_iota(jnp.int32, (rc, keys[c]), 0) + c * rc
                    col = lax.broadcasted_iota(jnp.int32, (rc, keys[c]), 1)
                    s = jnp.where(col <= row, s, NEG)
                tiles = [s[:, _hs(u)] for u in range(keys[c] // LANE)]
                mx = functools.reduce(jnp.maximum, tiles)
                m_old = m_s[rows, :]
                m_new = jnp.maximum(m_old, jnp.max(mx, axis=1, keepdims=True))
                alpha = jnp.exp2(m_old - m_new)
                ps = [jnp.exp2(u - m_new) for u in tiles]
                l_s[rows, :] = alpha * l_s[rows, :] + functools.reduce(jnp.add, ps)
                p = jnp.concatenate([u.astype(BF16) for u in ps], axis=1)
                acc_s[rows, :] = alpha * acc_s[rows, :] + _dot(p, v_ref[0:keys[c], :])
                m_s[rows, :] = m_new

        @pl.when(j < i)
        def _():
            step(False)

        @pl.when(j == i)
        def _():
            step(True)
            l = jnp.sum(l_s[...], axis=1, keepdims=True)
            o_ref[...] = (acc_s[...] / l).astype(BF16)
            lse_ref[0] = jnp.transpose(m_s[...] + jnp.log2(l))[0:1, :]

        if ride:
            @pl.when((pl.program_id(0) == NH - 1) & (t == len(pairs) - 1))
            def _():
                ride.finish(*riding)

    qmap = lambda h, t, qi_r, kj_r: (qi_r[t], h)
    kmap = lambda h, t, qi_r, kj_r: (kj_r[t], h)
    r_in, r_out, r_shapes, r_sems, r_args = _ride_specs(ride)
    o, lse, *carried = _pcall(
        body, name="mla_fwd", grid=(NH, len(pairs)), prefetch=2,
        in_specs=[pl.BlockSpec((tq, LANE), qmap), pl.BlockSpec((tq, LANE), kmap), pl.BlockSpec((tq, LANE), kmap)]
        + r_in,
        out_specs=[pl.BlockSpec((tq, LANE), qmap),
                   pl.BlockSpec((1, 1, tq), lambda h, t, qi_r, kj_r: (h, 0, qi_r[t]))] + r_out,
        out_shape=[_sds((S, NH * LANE), BF16), _sds((NH, 1, S), F32)] + r_shapes,
        scratch=[pltpu.VMEM((tq, LANE), F32), pltpu.VMEM((tq, LANE), F32), pltpu.VMEM((tq, LANE), F32)] + r_sems,
        sem=("arbitrary", "arbitrary"),
    )(qi, kj, q, k, v, *r_args)
    return o, lse, carried


def _mla_bwd(q, k, v, do, lse, delta, tq, ride=None):
    S = q.shape[0]
    nq = S // tq
    pairs = [(i, j) for j in range(nq) for i in range(j, nq)]
    qi = jnp.asarray(np.array([p[0] for p in pairs], np.int32))
    kj = jnp.asarray(np.array([p[1] for p in pairs], np.int32))

    def body(qi_ref, kj_ref, q_ref, k_ref, v_ref, do_ref, lse_ref, dl_ref, *rest):
        (dq_ref, dk_ref, dv_ref), (dq_s, dk_s, dv_s), riding = _ride_refs(ride, rest, 3, 3)
        t = pl.program_id(1)
        i, j = qi_ref[t], kj_ref[t]
        if ride:
            @pl.when((pl.program_id(0) == 0) & (t == 0))
            def _():
                ride.start(*riding)

        @pl.when(t == 0)
        def _():
            dq_s[...] = jnp.zeros_like(dq_s)

        @pl.when(i == j)
        def _():
            dk_s[...] = jnp.zeros_like(dk_s)
            dv_s[...] = jnp.zeros_like(dv_s)

        qrows = pl.ds(pl.multiple_of(i * tq, tq), tq)

        def step(masked):
            lse_r, dl_r = lse_ref[0], dl_ref[0]
            dq = jnp.zeros((tq, LANE), F32)
            nc = tq // MLA_RC
            q0 = [c * MLA_RC if masked else 0 for c in range(nc)]
            sts = [_dot_nt(k_ref[c * MLA_RC:(c + 1) * MLA_RC, :], q_ref[q0[c]:, :]) for c in range(nc)]
            dpts = [_dot_nt(v_ref[c * MLA_RC:(c + 1) * MLA_RC, :], do_ref[q0[c]:, :]) for c in range(nc)]
            for c in range(nc):
                rows = slice(c * MLA_RC, (c + 1) * MLA_RC)
                qb, dob = q_ref[q0[c]:, :], do_ref[q0[c]:, :]
                pt = jnp.exp2(sts[c] - lse_r[:, q0[c]:])
                if masked:
                    key = lax.broadcasted_iota(jnp.int32, (MLA_RC, tq - q0[c]), 0)
                    qry = lax.broadcasted_iota(jnp.int32, (MLA_RC, tq - q0[c]), 1)
                    pt = jnp.where(key <= qry, pt, 0.0)
                dv_s[rows, :] += _dot(pt.astype(BF16), dob)
                gt = (pt * (dpts[c] - dl_r[:, q0[c]:])).astype(BF16)
                dk_s[rows, :] += _dot(gt, qb)
                part = _dot_tn(gt, k_ref[rows, :])
                if masked:
                    at = pl.multiple_of(i * tq + q0[c], MLA_RC)
                    dq_s[pl.ds(at, tq - q0[c]), :] += part
                else:
                    dq = dq + part
            if not masked:
                dq_s[qrows, :] += dq

        @pl.when(i > j)
        def _():
            step(False)

        @pl.when(i == j)
        def _():
            step(True)
            dq_ref[...] = dq_s[qrows, :] * MLA_SCALE

        @pl.when(i == nq - 1)
        def _():
            dk_ref[...] = dk_s[...] * (1.0 / LOG2E)
            dv_ref[...] = dv_s[...]

        if ride:
            @pl.when((pl.program_id(0) == NH - 1) & (t == len(pairs) - 1))
            def _():
                ride.finish(*riding)

    qmap = lambda h, t, qi_r, kj_r: (qi_r[t], h)
    kmap = lambda h, t, qi_r, kj_r: (kj_r[t], h)
    rmap = lambda h, t, qi_r, kj_r: (h, 0, qi_r[t])
    r_in, r_out, r_shapes, r_sems, r_args = _ride_specs(ride)
    dq, dk, dv, *carried = _pcall(
        body, name="mla_bwd", grid=(NH, len(pairs)), prefetch=2,
        in_specs=[pl.BlockSpec((tq, LANE), qmap), pl.BlockSpec((tq, LANE), kmap), pl.BlockSpec((tq, LANE), kmap),
                  pl.BlockSpec((tq, LANE), qmap), pl.BlockSpec((1, 1, tq), rmap), pl.BlockSpec((1, 1, tq), rmap)]
        + r_in,
        out_specs=[pl.BlockSpec((tq, LANE), kmap), pl.BlockSpec((tq, LANE), kmap), pl.BlockSpec((tq, LANE), kmap)]
        + r_out,
        out_shape=[_sds((S, NH * LANE), F32), _sds((S, NH * LANE), F32), _sds((S, NH * LANE), F32)] + r_shapes,
        scratch=[pltpu.VMEM((S, LANE), F32), pltpu.VMEM((tq, LANE), F32), pltpu.VMEM((tq, LANE), F32)] + r_sems,
        sem=("arbitrary", "arbitrary"),
    )(qi, kj, q, k, v, do, lse, delta, *r_args)
    return dq, dk, dv, carried


SWA_SUB = 4
SWA_T = SWA_SUB * WIN


def _swa_specs(nsteps, rev):
    step = (lambda i: nsteps - 1 - i) if rev else (lambda i: i)
    cur = lambda w: pl.BlockSpec((SWA_T, w), lambda i: (step(i), 0))
    prev = pl.BlockSpec((WIN, LANE), lambda i: (jnp.maximum(step(i) * SWA_SUB - 1, 0), 0))
    return step, cur, prev


def _swa_probs(qk, bias_h, sink, first_mask):
    s = qk * SWA_SCALE + bias_h
    if first_mask is not None:
        s = jnp.where(first_mask, NEG, s)
    m = jnp.maximum(jnp.max(s, axis=1, keepdims=True), sink)
    e = jnp.exp(s - m)
    es = jnp.exp(sink - m)
    inv = 1.0 / (jnp.sum(e, axis=1, keepdims=True) + es)
    return e * inv, es * inv


SWA_GR = SWA_R * WIN


def _swa_group(ref, rows, g):
    return jnp.concatenate([ref[rows, _hs(g * SWA_R + r)] for r in range(SWA_R)], axis=0)


def _swa_rows(bias, sinks):
    sink_rows = jnp.broadcast_to(sinks[:, None, :], (NH, WIN, LANE)).reshape(NH * WIN, LANE)
    return bias.reshape(NH * WIN, 2 * WIN), sink_rows


def _swa_fwd(qs, ks, vs, bias, sinks):
    S = qs.shape[0]
    nsteps = S // SWA_T
    step, cur, prev = _swa_specs(nsteps, False)

    def body(qs_ref, kc_ref, kp_ref, vc_ref, vp_ref, bias_ref, sk_ref, o_ref):
        first = pl.program_id(0) == 0
        kk = jnp.concatenate([kp_ref[...], kc_ref[...]], axis=0)
        vv = jnp.concatenate([vp_ref[...], vc_ref[...]], axis=0)
        col = lax.broadcasted_iota(jnp.int32, (WIN, 2 * WIN), 1)
        for b in range(SWA_SUB):
            kkb = kk[b * WIN:(b + 2) * WIN]
            vvb = vv[b * WIN:(b + 2) * WIN]
            fm = (first & (col < WIN)) if b == 0 else None
            rows = slice(b * WIN, (b + 1) * WIN)
            qks = [_dot_nt(qs_ref[rows, _hs(h)], kkb) for h in range(NH)]
            for h in range(NH):
                p, _ = _swa_probs(qks[h], bias_ref[h], sk_ref[h:h + 1, 0:1], fm)
                o_ref[rows, _hs(h)] = _dot(p.astype(BF16), vvb).astype(BF16)

    return _pcall(
        body, name="swa_fwd", grid=(nsteps,),
        in_specs=[cur(NH * LANE), cur(LANE), prev, cur(LANE), prev, _full(NH, WIN, 2 * WIN), _full(NH, LANE)],
        out_specs=cur(NH * LANE), out_shape=_sds((S, NH * LANE), BF16), sem=("arbitrary",),
    )(qs, ks, ks, vs, vs, bias, sinks)


def _swa_bwd(qs, ks, vs, do, bias, sinks):
    S = qs.shape[0]
    nsteps = S // SWA_T
    step, cur, prev = _swa_specs(nsteps, True)
    bias_rows, sink_rows = _swa_rows(bias, sinks)

    def body(qs_ref, kc_ref, kp_ref, vc_ref, vp_ref, do_ref, bias_ref, sk_ref,
             dqs_ref, dks_ref, dvs_ref, dbias_ref, dsk_ref, dkk_s, dvv_s, ck_s, cv_s):
        pid = pl.program_id(0)
        first = step(pid) == 0

        @pl.when(pid == 0)
        def _():
            dbias_ref[...] = jnp.zeros_like(dbias_ref)
            dsk_ref[...] = jnp.zeros_like(dsk_ref)
            ck_s[...] = jnp.zeros_like(ck_s)
            cv_s[...] = jnp.zeros_like(cv_s)

        dkk_s[...] = jnp.zeros_like(dkk_s)
        dvv_s[...] = jnp.zeros_like(dvv_s)
        kk = jnp.concatenate([kp_ref[...], kc_ref[...]], axis=0)
        vv = jnp.concatenate([vp_ref[...], vc_ref[...]], axis=0)
        col = lax.broadcasted_iota(jnp.int32, (SWA_GR, 2 * WIN), 1)
        for b in range(SWA_SUB):
            kkb = kk[b * WIN:(b + 2) * WIN]
            vvb = vv[b * WIN:(b + 2) * WIN]
            fm = (first & (col < WIN)) if b == 0 else None
            rows = slice(b * WIN, (b + 1) * WIN)
            keys = slice(b * WIN, (b + 2) * WIN)
            qg = [_swa_group(qs_ref, rows, g) for g in range(2)]
            dog = [_swa_group(do_ref, rows, g) for g in range(2)]
            qks = [_dot_nt(qg[g], kkb) for g in range(2)]
            dps = [_dot_nt(dog[g], vvb) for g in range(2)]
            for g in range(2):
                grows = slice(g * SWA_GR, (g + 1) * SWA_GR)
                p, ps = _swa_probs(qks[g], bias_ref[grows, :], sk_ref[grows, 0:1], fm)
                dl = jnp.sum(p * dps[g], axis=1, keepdims=True)
                ds = p * (dps[g] - dl)
                sink_part = ps * dl
                for r in range(SWA_R):
                    h = g * SWA_R + r
                    dsk_ref[h:h + 1, :] += -jnp.sum(sink_part[r * WIN:(r + 1) * WIN])
                dbias_ref[grows, :] += ds
                dsb = (ds * SWA_SCALE).astype(BF16)
                dq = _dot(dsb, kkb).astype(BF16)
                for r in range(SWA_R):
                    dqs_ref[rows, _hs(g * SWA_R + r)] = dq[r * WIN:(r + 1) * WIN]
                dkk_s[keys, :] += _dot_tn(dsb, qg[g])
                dvv_s[keys, :] += _dot_tn(p.astype(BF16), dog[g])
        dks_ref[...] = dkk_s[WIN:, :]
        dvs_ref[...] = dvv_s[WIN:, :]
        dks_ref[SWA_T - WIN:, :] += ck_s[...]
        dvs_ref[SWA_T - WIN:, :] += cv_s[...]
        ck_s[...] = dkk_s[0:WIN, :]
        cv_s[...] = dvv_s[0:WIN, :]

    dqs, dks, dvs, dbias, dsink = _pcall(
        body, name="swa_bwd", grid=(nsteps,),
        in_specs=[cur(NH * LANE), cur(LANE), prev, cur(LANE), prev, cur(NH * LANE), _full(NH * WIN, 2 * WIN),
                  _full(NH * WIN, LANE)],
        out_specs=[cur(NH * LANE), cur(LANE), cur(LANE), _full(NH * WIN, 2 * WIN), _full(NH, LANE)],
        out_shape=[_sds((S, NH * LANE), BF16), _sds((S, LANE), F32), _sds((S, LANE), F32),
                   _sds((NH * WIN, 2 * WIN), F32), _sds((NH, LANE), F32)],
        scratch=[pltpu.VMEM((SWA_T + WIN, LANE), F32), pltpu.VMEM((SWA_T + WIN, LANE), F32),
                 pltpu.VMEM((WIN, LANE), F32), pltpu.VMEM((WIN, LANE), F32)],
        sem=("arbitrary",),
    )(qs, ks, ks, vs, vs, do, bias_rows, sink_rows)
    return dqs, dks, dvs, dbias.reshape(NH, WIN, 2 * WIN), dsink


def _bias_build(rel_bias, bmap):
    def body(rb_ref, bmap_ref, o_ref):
        bm = bmap_ref[...]
        for h in range(NH):
            acc = jnp.full((WIN, 2 * WIN), NEG, F32)
            for b in range(REL_BUCKETS):
                acc = jnp.where(bm == b, rb_ref[b, h], acc)
            o_ref[h] = acc

    return _pcall(
        body, name="bias_build", grid=(1,),
        in_specs=[pl.BlockSpec(memory_space=pltpu.SMEM), _full(WIN, 2 * WIN)],
        out_specs=_full(NH, WIN, 2 * WIN), out_shape=_sds((NH, WIN, 2 * WIN), F32), sem=("arbitrary",),
    )(rel_bias, bmap)


def _bias_reduce(dbias, bmap):
    def body(db_ref, bmap_ref, o_ref):
        bm = bmap_ref[...]
        for h in range(NH):
            dbh = db_ref[h]
            for b in range(REL_BUCKETS):
                o_ref[b, h] = jnp.sum(jnp.where(bm == b, dbh, 0.0))

    return _pcall(
        body, name="bias_reduce", grid=(1,),
        in_specs=[_full(NH, WIN, 2 * WIN), _full(WIN, 2 * WIN)],
        out_specs=pl.BlockSpec(memory_space=pltpu.SMEM), out_shape=_sds((REL_BUCKETS, NH), F32), sem=("arbitrary",),
    )(dbias, bmap)


def _memkv_fwd(mem, mnorm, wkv):
    def body(mem_ref, g_ref, w_ref, o_ref):
        n, _ = _rms(mem_ref[...])
        o_ref[...] = _dot((n * g_ref[...]).astype(BF16), w_ref[...]).astype(BF16)

    return _pcall(
        body, name="memkv_fwd", grid=(1,), in_specs=[_full(MEM_LEN, D), _full(1, D), _full(D, D)],
        out_specs=_full(MEM_LEN, D), out_shape=_sds((MEM_LEN, D), BF16), sem=("arbitrary",),
    )(mem, mnorm, wkv)


def _mem_probs(qk):
    s = qk * MEM_SCALE
    e = jnp.exp(s - jnp.max(s, axis=1, keepdims=True))
    return e / jnp.sum(e, axis=1, keepdims=True)


def _mem_fwd(qm, kvm, ts):
    S = qm.shape[0]

    def body(q_ref, kv_ref, o_ref):
        qks = [_dot_nt(q_ref[:, _hs(h)], kv_ref[:, _hs(h)]) for h in range(MEM_H)]
        for h in range(MEM_H):
            p = _mem_probs(qks[h])
            o_ref[:, _hs(h)] = _dot(p.astype(BF16), kv_ref[:, _hs(MEM_H + h)]).astype(BF16)

    return _pcall(
        body, name="mem_fwd", grid=(S // ts,), in_specs=[_tok(ts, 512), _full(MEM_LEN, D)],
        out_specs=_tok(ts, 512), out_shape=_sds((S, 512), BF16), sem=("arbitrary",),
    )(qm, kvm)


def _mem_bwd(qm, kvm, do, ts):
    S = qm.shape[0]

    def body(q_ref, kv_ref, do_ref, dq_ref, dkv_ref):
        @pl.when(pl.program_id(0) == 0)
        def _():
            dkv_ref[...] = jnp.zeros_like(dkv_ref)

        qks = [_dot_nt(q_ref[:, _hs(h)], kv_ref[:, _hs(h)]) for h in range(MEM_H)]
        dps = [_dot_nt(do_ref[:, _hs(h)], kv_ref[:, _hs(MEM_H + h)]) for h in range(MEM_H)]
        for h in range(MEM_H):
            qh, kh, doh = q_ref[:, _hs(h)], kv_ref[:, _hs(h)], do_ref[:, _hs(h)]
            p = _mem_probs(qks[h])
            dp = dps[h]
            ds = (p * (dp - jnp.sum(p * dp, axis=1, keepdims=True)) * MEM_SCALE).astype(BF16)
            dq_ref[:, _hs(h)] = _dot(ds, kh).astype(BF16)
            dkv_ref[:, _hs(h)] += _dot_tn(ds, qh)
            dkv_ref[:, _hs(MEM_H + h)] += _dot_tn(p.astype(BF16), doh)

    return _pcall(
        body, name="mem_bwd", grid=(S // ts,), in_specs=[_tok(ts, 512), _full(MEM_LEN, D), _tok(ts, 512)],
        out_specs=[_tok(ts, 512), _full(MEM_LEN, D)],
        out_shape=[_sds((S, 512), BF16), _sds((MEM_LEN, D), F32)], sem=("arbitrary",),
    )(qm, kvm, do)


def _memkv_bwd(mem, mnorm, wkv, dkvm):
    def body(mem_ref, g_ref, w_ref, dkv_ref, dw_ref, dg_ref):
        n, _ = _rms(mem_ref[...])
        dkvb = dkv_ref[...].astype(BF16)
        dw_ref[...] = _dot_tn((n * g_ref[...]).astype(BF16), dkvb)
        dg_ref[...] = _colsum(_dot_nt(dkvb, w_ref[...]) * n)

    return _pcall(
        body, name="memkv_bwd", grid=(1,), in_specs=[_full(MEM_LEN, D), _full(1, D), _full(D, D), _full(MEM_LEN, D)],
        out_specs=[_full(D, D), _full(1, D)], out_shape=[_sds((D, D), F32), _sds((1, D), F32)], sem=("arbitrary",),
    )(mem, mnorm, wkv, dkvm)


def _adamw(w, g, m, v, name):
    rows, cols = w.shape
    tr = min(rows, FLAT_TILE)
    assert rows % tr == 0

    def body(w_ref, g_ref, m_ref, v_ref, d_ref, nm_ref, nv_ref):
        gv = g_ref[...]
        nm = ADAM_B1 * m_ref[...] + (1.0 - ADAM_B1) * gv
        nv = ADAM_B2 * v_ref[...] + (1.0 - ADAM_B2) * jnp.square(gv)
        m_hat = nm / (1.0 - ADAM_B1 ** ADAM_STEP)
        v_hat = nv / (1.0 - ADAM_B2 ** ADAM_STEP)
        d_ref[...] = -ADAM_LR * (m_hat / (jnp.sqrt(v_hat) + ADAM_EPS) + ADAM_WD * w_ref[...])
        nm_ref[...] = nm
        nv_ref[...] = nv

    spec = _tok(tr, cols)
    return _pcall(
        body, name=name, grid=(rows // tr,), in_specs=[spec] * 4, out_specs=[spec] * 3,
        out_shape=[_sds((rows, cols), F32)] * 3, sem=("arbitrary",),
    )(w, g, m, v)


def _my_place():
    return lax.axis_index("x"), lax.axis_index("y"), lax.axis_index("c")


def _remote(src, dst, send_sems, recv_sems, k, to):
    return pltpu.make_async_remote_copy(src_ref=src, dst_ref=dst, send_sem=send_sems.at[k], recv_sem=recv_sems.at[k],
                                        device_id=to, device_id_type=MESH_ID)


class _Rider(NamedTuple):
    inputs: list
    out_shapes: list
    sems: list
    start: Callable
    finish: Callable


def _run_rider(rider, name):
    ni, no = len(rider.inputs), len(rider.out_shapes)

    def body(*refs):
        parts = refs[:ni], refs[ni:ni + no], refs[ni + no:]
        rider.start(*parts)
        rider.finish(*parts)

    hbm = pl.BlockSpec(memory_space=pl.ANY)
    return pl.pallas_call(body, name=name, out_shape=rider.out_shapes, in_specs=[hbm] * ni, out_specs=[hbm] * no,
                          scratch_shapes=rider.sems)(*rider.inputs)


def _gather_rider(shards):
    n = len(shards)

    def copies(ins, outs, sems):
        send_sems, recv_sems, local_sems = sems
        x, y, c = _my_place()
        chips = [(1 - x, y), (x, 1 - y), (1 - x, 1 - y)]
        mine = [pltpu.make_async_copy(ins[i], outs[i].at[2 * x + y], local_sems.at[i]) for i in range(n)]

        def copy(i, k, slot, to):
            return _remote(ins[i], outs[i].at[slot], send_sems, recv_sems, 3 * i + k, to)

        sends = [copy(i, k, 2 * x + y, (px, py, c)) for i in range(n) for k, (px, py) in enumerate(chips)]
        lands = [copy(i, k, 2 * px + py, (px, py, c)) for i in range(n) for k, (px, py) in enumerate(chips)]
        return mine, sends, lands

    def start(ins, outs, sems):
        mine, sends, _ = copies(ins, outs, sems)
        for cp in mine + sends:
            cp.start()

    def finish(ins, outs, sems):
        mine, sends, lands = copies(ins, outs, sems)
        for cp in lands:
            cp.wait_recv()
        for cp in sends:
            cp.wait_send()
        for cp in mine:
            cp.wait()

    return _Rider(list(shards), [_sds((4,) + s.shape, s.dtype) for s in shards],
                  [pltpu.SemaphoreType.DMA((3 * n,)), pltpu.SemaphoreType.DMA((3 * n,)), pltpu.SemaphoreType.DMA((n,))],
                  start, finish)


def _scatter_rider(gws, layer, gsmall=None):
    n = len(gws)
    per = n + (gsmall is not None)

    def plan(ins, outs, sems):
        send_sems, recv_sems, local_sems = sems
        x, y, c = _my_place()
        me = 4 * x + 2 * y + c

        def peer(m):
            return (x ^ ((m >> 2) & 1), y ^ ((m >> 1) & 1), c ^ (m & 1))

        def slot_of(m):
            px, py, pc = peer(m)
            return 4 * px + 2 * py + pc

        def piece(i, m, slot):
            px, py, pc = peer(m)
            return _remote(ins[i].at[2 * px + py], outs[i].at[slot], send_sems, recv_sems, (m - 1) * per + i,
                           (px, py, pc))

        def small(m, slot):
            return _remote(ins[n], outs[n].at[slot], send_sems, recv_sems, (m - 1) * per + n, peer(m))

        own = [pltpu.make_async_copy(ins[i].at[2 * x + y], outs[i].at[me], local_sems.at[i]) for i in range(n)]
        return c, me, slot_of, piece, small, own

    def start(ins, outs, sems):
        c, me, slot_of, piece, small, own = plan(ins, outs, sems)

        @pl.when(c == layer)
        def _():
            for cp in own:
                cp.start()
            for m in (2, 4, 6):
                for i in range(n):
                    piece(i, m, me).start()

        @pl.when(c != layer)
        def _():
            for m in (1, 3, 5, 7):
                for i in range(n):
                    piece(i, m, me).start()

        if gsmall is not None:
            pltpu.make_async_copy(ins[n], outs[n].at[me], sems[2].at[n]).start()
            for m in range(1, 8):
                small(m, me).start()

    def finish(ins, outs, sems):
        c, me, slot_of, piece, small, own = plan(ins, outs, sems)

        @pl.when(c == layer)
        def _():
            for m in range(1, 8):
                for i in range(n):
                    piece(i, m, slot_of(m)).wait_recv()
            for m in (2, 4, 6):
                for i in range(n):
                    piece(i, m, me).wait_send()
            for cp in own:
                cp.wait()

        @pl.when(c != layer)
        def _():
            for m in (1, 3, 5, 7):
                for i in range(n):
                    piece(i, m, me).wait_send()

        if gsmall is not None:
            for m in range(1, 8):
                small(m, slot_of(m)).wait_recv()
            for m in range(1, 8):
                small(m, me).wait_send()
            pltpu.make_async_copy(ins[n], outs[n].at[me], sems[2].at[n]).wait()

    inputs = list(gws) + ([gsmall] if gsmall is not None else [])
    shapes = [_sds((8,) + g.shape[1:], g.dtype) for g in gws]
    if gsmall is not None:
        shapes.append(_sds((8,) + gsmall.shape, gsmall.dtype))
    nsem = 7 * per
    return _Rider(inputs, shapes, [pltpu.SemaphoreType.DMA((nsem,)), pltpu.SemaphoreType.DMA((nsem,)),
                                   pltpu.SemaphoreType.DMA((per,))], start, finish)


def _sum_slots(slots, name, other=None):
    _, r, c = slots.shape
    tr = min(r, FLAT_TILE)
    assert r % tr == 0

    def total(ref):
        acc = ref[0].astype(F32)
        for d in range(1, 8):
            acc = acc + ref[d].astype(F32)
        return acc

    def body(*refs):
        o_ref = refs[-1]
        if other is None:
            o_ref[...] = total(refs[0])
        else:
            core = lax.axis_index("c")

            @pl.when(core == 0)
            def _():
                o_ref[...] = total(refs[0])

            @pl.when(core == 1)
            def _():
                o_ref[...] = total(refs[1])

    spec = pl.BlockSpec((8, tr, c), lambda i: (0, i, 0))
    ins = [slots] if other is None else [slots, other]
    return _pcall(
        body, name=name, grid=(r // tr,), in_specs=[spec] * len(ins),
        out_specs=_tok(tr, c), out_shape=_sds((r, c), F32), sem=("arbitrary",),
    )(*ins)


def _swap_layers(reds):
    n = len(reds)
    tiles = [min(r.shape[0], FLAT_TILE) for r in reds]

    def body(*refs):
        ins, outs, bufs = refs[:n], refs[n:2 * n], refs[2 * n:3 * n]
        load_sems, send_sems, recv_sems, local_sems = refs[3 * n:]
        x, y, c = _my_place()
        sibling = (x, y, 1 - c)
        mine = [pltpu.make_async_copy(ins[i], outs[i].at[c], local_sems.at[i]) for i in range(n)]
        for cp in mine:
            cp.start()
        in_flight = []
        for i in range(n):
            tr = tiles[i]
            busy = {}
            for t in range(reds[i].shape[0] // tr):
                slot, rows = t % 2, pl.ds(t * tr, tr)
                if slot in busy:
                    busy[slot].wait_send()
                load = pltpu.make_async_copy(ins[i].at[rows], bufs[i].at[slot], load_sems.at[slot])
                load.start()
                load.wait()
                busy[slot] = pltpu.make_async_remote_copy(
                    src_ref=bufs[i].at[slot], dst_ref=outs[i].at[c, rows], send_sem=send_sems.at[2 * i + slot],
                    recv_sem=recv_sems.at[i], device_id=sibling, device_id_type=MESH_ID)
                busy[slot].start()
            in_flight += list(busy.values())
        for i in range(n):
            _remote(ins[i], outs[i].at[1 - c], send_sems, recv_sems, i, sibling).wait_recv()
        for cp in in_flight:
            cp.wait_send()
        for cp in mine:
            cp.wait()

    hbm = pl.BlockSpec(memory_space=pl.ANY)
    return pl.pallas_call(
        body, name="swap_layers", out_shape=[_sds((2,) + r.shape, r.dtype) for r in reds],
        in_specs=[hbm] * n, out_specs=[hbm] * n,
        scratch_shapes=[pltpu.VMEM((2, tiles[i]) + reds[i].shape[1:], reds[i].dtype) for i in range(n)]
        + [pltpu.SemaphoreType.DMA((2,)), pltpu.SemaphoreType.DMA((2 * n,)), pltpu.SemaphoreType.DMA((n,)),
           pltpu.SemaphoreType.DMA((n,))],
        compiler_params=pltpu.CompilerParams(vmem_limit_bytes=VMEM_LIMIT),
    )(*reds)


W_IN_SHARD, W_IN_SHARD_PAD = 1192, 1280
W_UQ_SHARD, W_UQ_SHARD_PAD = 192, 256


def _swa_place(t):
    z = jnp.zeros_like(t)
    lo = jnp.concatenate([t, z], axis=1)
    hi = jnp.concatenate([z, t], axis=1)
    group = (jnp.arange(NH) // SWA_R).reshape((NH,) + (1,) * (t.ndim - 1))
    full = jnp.where(group == 0, lo, hi)
    return full.reshape((NH * LANE,) + t.shape[2:])


def _swa_unplace(t):
    t = t.reshape((NH, 2, 64) + t.shape[1:])
    return jnp.concatenate([t[:SWA_R, 0], t[SWA_R:, 1]], axis=0)


def _pad_w_o_mla(w):
    return jnp.pad(w.reshape(NH, 64, D), ((0, 0), (0, 64), (0, 0))).reshape(NH * LANE, D)


def _unpad_w_o_mla(g):
    return g.reshape(NH, LANE, D)[:, :64].reshape(NH * 64, D)


def _w_in_cols():
    src = np.full((C_END,), -1, np.int64)
    src[C_CQ:C_KPE] = np.arange(0, 384)
    src[C_KPE + 64:C_KPE + 96] = np.arange(384, 416)
    for h in range(NH):
        at = C_QS + h * LANE + 64 * (h // SWA_R)
        src[at:at + 64] = 416 + h * 64 + np.arange(64)
    src[C_KS:C_END] = np.arange(928, IN_COLS)
    return src


def _w_uq_cols():
    src = np.full((NH * LANE,), -1, np.int64)
    for h in range(NH):
        src[h * LANE:h * LANE + 96] = h * 96 + np.arange(96)
    return src


def _w_ukv_cols():
    src = np.full((2 * NH * LANE,), -1, np.int64)
    for h in range(NH):
        src[h * LANE:h * LANE + 64] = h * 128 + np.arange(64)
        src[NH * LANE + h * LANE:NH * LANE + h * LANE + 64] = h * 128 + 64 + np.arange(64)
    return src


def _selection(src_cols, width, width_pad):
    want = jnp.asarray(np.asarray(src_cols, np.int32))[None, None, :]
    k = jnp.arange(width_pad, dtype=jnp.int32)[None, :, None]
    have = jnp.where(k < width, jnp.arange(4, dtype=jnp.int32)[:, None, None] * width + k, -2)
    return (want == have).astype(BF16)


def _selections():
    return dict(w_in=_selection(_w_in_cols(), W_IN_SHARD, W_IN_SHARD_PAD),
                w_uq=_selection(_w_uq_cols(), W_UQ_SHARD, W_UQ_SHARD_PAD),
                w_ukv=_selection(_w_ukv_cols(), 256, 256))


def _pad_last(a, width):
    return jnp.pad(a, ((0, 0),) * (a.ndim - 1) + ((0, width - a.shape[-1]),))


def _wire_shards(W, l):
    out = {n: W[n][l].astype(BF16) for n in SHARDED}
    out["w_in"] = _pad_last(out["w_in"], W_IN_SHARD_PAD)
    out["w_uq"] = _pad_last(out["w_uq"], W_UQ_SHARD_PAD)
    return [out[n] for n in SHARDED]


def _join_shards(t, axis):
    _, L, r, c = t.shape
    if axis == 2:
        return t.transpose(1, 2, 0, 3).reshape(L, r, 4 * c)
    return t.transpose(1, 0, 2, 3).reshape(L, 4 * r, c)


def _kernel_weights(gathered, sels):
    g = {n: a[:, None] for n, a in zip(SHARDED, gathered)}
    lay = {n: _select_fwd(g[n], sels[n], "lay_" + n)[0] for n in ("w_in", "w_uq", "w_ukv")}
    whole = {n: _join_shards(g[n], SHARD_AXIS[n])[0]
             for n in ("w_mem_kv", "w_o_mla", "w_o_swa", "w_o_mem", "w_out", "w_up", "w_down")}
    return dict(win=lay["w_in"], wuq=lay["w_uq"], wukv=lay["w_ukv"], wmem=whole["w_mem_kv"],
                wa=_pad_w_o_mla(whole["w_o_mla"]), wb=_swa_place(whole["w_o_swa"].reshape(NH, 64, D)),
                wc=whole["w_o_mem"], wout=whole["w_out"], wup=whole["w_up"], wdown=whole["w_down"])


def _cols_to_shards(g):
    r, c4 = g.shape
    return g.reshape(r, 4, c4 // 4).transpose(1, 0, 2).astype(BF16)


def _rope_tables(S):
    pos = jnp.arange(S, dtype=F32)
    inv = 1.0 / (ROPE_THETA ** (jnp.arange(0, 32, 2, dtype=F32) / 32))
    ang = pos[:, None] * inv[None, :]
    cos, sin = jnp.cos(ang), jnp.sin(ang)
    one, zero = jnp.ones((S, 64), F32), jnp.zeros((S, 16), F32)
    rc = jnp.concatenate([one, cos, cos, jnp.ones((S, 32), F32)], axis=1)
    rs1 = jnp.concatenate([jnp.zeros((S, 64), F32), zero, sin, jnp.zeros((S, 32), F32)], axis=1)
    rs2 = jnp.concatenate([jnp.zeros((S, 64), F32), -sin, zero, jnp.zeros((S, 32), F32)], axis=1)
    return rc, rs1, rs2


def _bucket_map():
    qi = jnp.arange(WIN)[:, None]
    kj = jnp.arange(2 * WIN)[None, :]
    dist = qi + WIN - kj
    n = jnp.maximum(dist, 0)
    max_exact = REL_BUCKETS // 2
    nf = jnp.maximum(n, 1).astype(F32)
    large = max_exact + (jnp.log(nf / max_exact) / math.log(128 / max_exact)
                         * (REL_BUCKETS - max_exact)).astype(jnp.int32)
    large = jnp.minimum(large, REL_BUCKETS - 1)
    bucket = jnp.where(n < max_exact, n, large)
    return jnp.where((dist >= 0) & (dist < WIN), bucket, -1).astype(jnp.int32)


TS = 256
TQ = 1024
TQ_FWD = 1024


def _local_step(x, mem, tgt, kw0, sp, sels, kw1=None, wire1=None):
    S = x.shape[0]
    ts = min(TS, S)
    tq = min(TQ, S)
    rc, rs1, rs2 = _rope_tables(S)
    bmap = _bucket_map()
    bias = _bias_build(sp["rel_bias"], bmap)
    row = lambda v: v.reshape(1, -1)

    saved = []
    kw = [kw0, kw1]
    for l in range(DEPTH):
        w = kw[l]
        an, qn, kvn = row(sp["attn_norm"][l]), row(sp["mla_q_norm"][l]), row(sp["mla_kv_norm"][l])
        bg, mnorm, mlpn = row(sp["b_gate"][l]), row(sp["mem_norm"][l]), row(sp["mlp_norm"][l])
        sinks = jnp.broadcast_to(sp["attn_sinks"][l][:, None], (NH, LANE))
        q, k, v, qs, ks, vs, qm, g = _pre_fwd(x, an, w["win"], bg, qn, kvn, w["wuq"], w["wukv"], rc, rs1, rs2, ts)
        carry = _gather_rider(wire1) if (l == 0 and wire1 is not None) else None
        oa, lse, gathered1 = _mla_fwd(q, k, v, min(TQ_FWD, S), carry)
        if carry:
            kw[1] = _kernel_weights(gathered1, sels)
        ob = _swa_fwd(qs, ks, vs, bias, sinks)
        kvm = _memkv_fwd(mem, mnorm, w["wmem"])
        oc = _mem_fwd(qm, kvm, ts)
        x1, yb = _merge_fwd(x, g, oa, ob, oc, w["wa"], w["wb"], w["wc"], w["wout"], ts)
        x2 = _mlp_fwd(x1, mlpn, w["wup"], w["wdown"], ts)
        saved.append(dict(w=w, x=x, x1=x1, q=q, k=k, v=v, qs=qs, ks=ks, vs=vs, qm=qm, g=g, oa=oa, lse=lse, ob=ob,
                          oc=oc, kvm=kvm, yb=yb, an=an, qn=qn, kvn=kvn, mnorm=mnorm, mlpn=mlpn, sinks=sinks))
        x = x2

    sq, dx, dfn = _loss_kernel(x, row(sp["final_norm"]), tgt, ts)

    big = {n: [None] * DEPTH for n in SHARDED}
    small = {n: [None] * DEPTH for n in SMALL if n not in ("rel_bias", "final_norm")}
    dbias_total = None
    slots1 = None
    for l in reversed(range(DEPTH)):
        sv = saved[l]
        w = sv["w"]
        dx1, hb2, dub, ab, dxb, dmlpn = _mlp_bwd(dx, sv["x1"], sv["mlpn"], w["wup"], w["wdown"], ts)
        big["w_up"][l] = _matmul_tn(hb2, dub, "dw_up", shard_axis=1)
        big["w_down"][l] = _matmul_tn(ab, dxb, "dw_down", shard_axis=0)
        small["mlp_norm"][l] = dmlpn[0]

        dgp, dyo, doa, dob, doc, dla, dx1b, dbg = _merge_bwd(dx1, sv["g"], sv["oa"], sv["ob"], sv["oc"], w["wa"],
                                                             w["wb"], w["wc"], w["wout"], ts)
        big["w_out"][l] = _matmul_tn(sv["yb"], dx1b, "dw_out", shard_axis=0)
        big["w_o_mla"][l] = _cols_to_shards(_unpad_w_o_mla(_matmul_tn(sv["oa"], dyo[:, 0:D], "dw_o_mla")))
        big["w_o_swa"][l] = _cols_to_shards(
            _swa_unplace(_matmul_tn(sv["ob"], dyo[:, D:2 * D], "dw_o_swa")).reshape(NH * 64, D))
        big["w_o_mem"][l] = _matmul_tn(sv["oc"], dyo[:, 2 * D:3 * D], "dw_o_mem", shard_axis=1)
        small["b_gate"][l] = dbg[0]

        dqm, dkvm = _mem_bwd(sv["qm"], sv["kvm"], doc, ts)
        dwmem, dmnorm = _memkv_bwd(mem, sv["mnorm"], w["wmem"], dkvm)
        big["w_mem_kv"][l] = dwmem.reshape(4, D // 4, D).astype(BF16)
        small["mem_norm"][l] = dmnorm[0]

        dqs, dks, dvs, dbias, dsink = _swa_bwd(sv["qs"], sv["ks"], sv["vs"], dob, bias, sv["sinks"])
        dbias_total = dbias if dbias_total is None else dbias_total + dbias
        small["attn_sinks"][l] = dsink[:, 0]

        carry = _scatter_rider([big[n][1] for n in SHARDED], 1) if (l == 0 and wire1 is not None) else None
        dq, dk, dv, slots = _mla_bwd(sv["q"], sv["k"], sv["v"], doa, sv["lse"], dla, tq, carry)
        if carry:
            slots1 = slots

        dx, dproj, hb, cqn, ckvn, dqpre, dkv, dan, dqn, dkvn = _pre_bwd(
            sv["x"], dx1, dq, dk, dv, dqs, dks, dvs, dqm, dgp, sv["an"], sv["qn"], sv["kvn"], w["win"], w["wuq"],
            w["wukv"], rc, rs1, rs2, ts)
        for n, (a, b) in (("w_in", (hb, dproj)), ("w_uq", (cqn, dqpre)), ("w_ukv", (ckvn, dkv))):
            big[n][l] = _select_bwd(_matmul_tn(a, b, "d" + n)[None], sels[n], "shard_d" + n)[0]
        small["attn_norm"][l] = dan[0]
        small["mla_q_norm"][l] = dqn[0]
        small["mla_kv_norm"][l] = dkvn[0]

    gw = [[big[n][l] for n in SHARDED] for l in range(DEPTH)]
    gs = {n: jnp.stack(v) for n, v in small.items()}
    gs["rel_bias"] = _bias_reduce(dbias_total, bmap)
    gs["final_norm"] = dfn[0]
    return sq, dx, gw, slots1, gs


def _flatten(parts, rows):
    flat = jnp.concatenate([p.reshape(-1) for p in parts])
    return jnp.pad(flat, (0, rows * FLAT_W - flat.shape[0])).reshape(rows, FLAT_W)


def _unflatten(buf, shapes):
    flat = buf.reshape(-1)
    out, at = [], 0
    for s in shapes:
        n = int(np.prod(s))
        out.append(flat[at:at + n].reshape(s))
        at += n
    return out


def kernel(x, mem, rel_bias, attn_norm, mem_norm, w_in, b_gate, mla_q_norm, w_uq, mla_kv_norm, w_ukv, attn_sinks, w_mem_kv, w_o_mla, w_o_swa, w_o_mem, w_out, mlp_norm, w_up, w_down, final_norm, loss_target, m_rel_bias, m_attn_norm, m_mem_norm, m_w_in, m_b_gate, m_mla_q_norm, m_w_uq, m_mla_kv_norm, m_w_ukv, m_attn_sinks, m_w_mem_kv, m_w_o_mla, m_w_o_swa, m_w_o_mem, m_w_out, m_mlp_norm, m_w_up, m_w_down, m_final_norm, v_rel_bias, v_attn_norm, v_mem_norm, v_w_in, v_b_gate, v_mla_q_norm, v_w_uq, v_mla_kv_norm, v_w_ukv, v_attn_sinks, v_w_mem_kv, v_w_o_mla, v_w_o_swa, v_w_o_mem, v_w_out, v_mlp_norm, v_w_up, v_w_down, v_final_norm):
    args = dict(locals())
    W = {n: args[n] for n in WEIGHTS}
    M = {n: args["m_" + n] for n in WEIGHTS}
    V = {n: args["v_" + n] for n in WEIGHTS}
    small_shapes = [W[n].shape for n in SMALL]
    sels = _selections()

    gathered0 = _run_rider(_gather_rider(_wire_shards(W, 0)), "gather_weights")
    kw0 = _kernel_weights(gathered0, sels)
    sp = {n: W[n] for n in SMALL}

    sq, dx, gw, slots1, gs = _local_step(x[0], mem[0], loss_target[0], kw0, sp, sels, wire1=_wire_shards(W, 1))

    gsmall = _flatten([gs[n] for n in SMALL], SMALL_ROWS)
    *slots0, small_slots = _run_rider(_scatter_rider(gw[0], 0, gsmall), "scatter_grads")
    reds = [_sum_slots(s0, "sum_" + n, s1) for n, s0, s1 in zip(SHARDED, slots0, slots1)]
    red_small = _sum_slots(small_slots, "sum_small")
    G = dict(zip(SHARDED, _swap_layers(reds)))
    G["w_in"] = G["w_in"][..., :W_IN_SHARD]
    G["w_uq"] = G["w_uq"][..., :W_UQ_SHARD]

    DW, NM, NV = {}, {}, {}
    for n in SHARDED:
        shape = W[n].shape
        two_d = lambda a: a.reshape(-1, shape[-1])
        d, nm, nv = _adamw(two_d(W[n]), two_d(G[n]), two_d(M[n]), two_d(V[n]), "adamw_" + n)
        DW[n], NM[n], NV[n] = d.reshape(shape), nm.reshape(shape), nv.reshape(shape)
    d_s, m_s, v_s = _adamw(_flatten([W[n] for n in SMALL], SMALL_ROWS), red_small,
                           _flatten([M[n] for n in SMALL], SMALL_ROWS), _flatten([V[n] for n in SMALL], SMALL_ROWS),
                           "adamw_small")
    for out, buf in ((G, red_small), (DW, d_s), (NM, m_s), (NV, v_s)):
        out.update(zip(SMALL, _unflatten(buf, small_shapes)))
    loss = lax.psum(0.5 * sq[0, 0] / D, ("x", "y", "c"))
    return (loss, dx[None], *[G[n] for n in WEIGHTS], *[DW[n] for n in WEIGHTS], *[NM[n] for n in WEIGHTS],
            *[NV[n] for n in WEIGHTS])
```

```python
import functools
import math
from typing import Callable, NamedTuple

import numpy as np
import jax
import jax.numpy as jnp
from jax import lax
from jax.experimental import pallas as pl
from jax.experimental.pallas import tpu as pltpu

F32 = jnp.float32
BF16 = jnp.bfloat16

D = 1024
DFF = 4096
DEPTH = 2
EPS = 1e-6
LANE = 128
NH = 8
SWA_R = 4
MEM_H = 4
MEM_LEN = 256
WIN = 128
NEG = -1e30
MLA_SCALE = 96 ** -0.5
SWA_SCALE = 64 ** -0.5
MEM_SCALE = 128 ** -0.5
REL_BUCKETS = 32
ROPE_THETA = 10000.0

C_CQ, C_CKV, C_KPE, C_QS, C_KS, C_VS, C_QM, C_G, C_END = 0, 256, 384, 512, 1536, 1664, 1792, 2304, 5376
IN_COLS = 4768

ADAM_LR = 0.001
ADAM_B1 = 0.9
ADAM_B2 = 0.999
ADAM_EPS = 1e-08
ADAM_WD = 0.01
ADAM_STEP = 10

VMEM_LIMIT = 56 * 1024 * 1024

SHARDED = ("w_in", "w_uq", "w_ukv", "w_mem_kv", "w_o_mla", "w_o_swa", "w_o_mem", "w_out", "w_up", "w_down")
SHARD_AXIS = {"w_in": 2, "w_uq": 2, "w_ukv": 2, "w_mem_kv": 1, "w_o_mla": 2, "w_o_swa": 2, "w_o_mem": 2,
              "w_out": 1, "w_up": 2, "w_down": 1}
SMALL = ("rel_bias", "attn_norm", "mem_norm", "b_gate", "mla_q_norm", "mla_kv_norm", "attn_sinks", "mlp_norm",
         "final_norm")
WEIGHTS = ("rel_bias", "attn_norm", "mem_norm", "w_in", "b_gate", "mla_q_norm", "w_uq", "mla_kv_norm", "w_ukv",
           "attn_sinks", "w_mem_kv", "w_o_mla", "w_o_swa", "w_o_mem", "w_out", "mlp_norm", "w_up", "w_down",
           "final_norm")
FLAT_W = 1024
FLAT_TILE = 256
SMALL_ROWS = 16
MESH_ID = pl.DeviceIdType.MESH


def _pcall(body, *, name, grid, in_specs, out_specs, out_shape, scratch=(), prefetch=0, sem=None):
    params = pltpu.CompilerParams(dimension_semantics=sem, vmem_limit_bytes=VMEM_LIMIT)
    if prefetch:
        spec = pltpu.PrefetchScalarGridSpec(num_scalar_prefetch=prefetch, grid=grid, in_specs=in_specs,
                                            out_specs=out_specs, scratch_shapes=scratch)
        return pl.pallas_call(body, name=name, grid_spec=spec, out_shape=out_shape, compiler_params=params)
    return pl.pallas_call(body, name=name, grid=grid, in_specs=in_specs, out_specs=out_specs, out_shape=out_shape,
                          scratch_shapes=scratch, compiler_params=params)


def _tok(ts, w):
    return pl.BlockSpec((ts, w), lambda i: (i, 0))


def _full(*shape):
    return pl.BlockSpec(shape, lambda *_: (0,) * len(shape))


def _sds(shape, dtype):
    return jax.ShapeDtypeStruct(shape, dtype)


def _dot(a, b):
    return jnp.dot(a, b, preferred_element_type=F32)


def _dot_nt(a, b):
    return lax.dot_general(a, b, (((1,), (1,)), ((), ())), preferred_element_type=F32)


def _dot_tn(a, b):
    return lax.dot_general(a, b, (((0,), (0,)), ((), ())), preferred_element_type=F32)


def _rms(x):
    r = lax.rsqrt(jnp.mean(x * x, axis=-1, keepdims=True) + EPS)
    return x * r, r


def _rms_bwd(dyg, n, r):
    return r * (dyg - n * jnp.mean(n * dyg, axis=-1, keepdims=True))


def _rope(t, c, s1, s2):
    return t * c + pltpu.roll(t, 16, 1) * s1 + pltpu.roll(t, LANE - 16, 1) * s2


def _rope_bwd(dy, c, s1, s2):
    return dy * c + pltpu.roll(dy * s1, LANE - 16, 1) + pltpu.roll(dy * s2, 16, 1)


def _hs(h):
    return slice(h * LANE, (h + 1) * LANE)


def _colsum(t):
    return jnp.sum(t, axis=0, keepdims=True)


def _pre_fwd(x, an, win, bg, qn, kvn, wuq, wukv, rc, rs1, rs2, ts):
    S = x.shape[0]

    def body(x_ref, an_ref, win_ref, bg_ref, qn_ref, kvn_ref, wuq_ref, wukv_ref, rc_ref, rs1_ref, rs2_ref,
             q_ref, k_ref, v_ref, qs_ref, ks_ref, vs_ref, qm_ref, g_ref):
        n, _ = _rms(x_ref[...])
        hb = (n * an_ref[...]).astype(BF16)
        pa = _dot(hb, win_ref[:, C_CQ:C_QS])
        ncq, _ = _rms(pa[:, 0:256])
        cqn = (ncq * qn_ref[...]).astype(BF16)
        nkv, _ = _rms(pa[:, 256:384])
        ckvn = (nkv * kvn_ref[...]).astype(BF16)
        c, s1, s2 = rc_ref[...], rs1_ref[...], rs2_ref[...]
        kper = _rope(pa[:, 384:512], c, s1, s2)
        qp = _dot(cqn, wuq_ref[...])
        kv = _dot(ckvn, wukv_ref[...])
        for h in range(NH):
            q_ref[:, _hs(h)] = (_rope(qp[:, _hs(h)], c, s1, s2) * MLA_QSCALE).astype(BF16)
            k_ref[:, _hs(h)] = (kv[:, _hs(h)] + kper).astype(BF16)
        v_ref[...] = kv[:, NH * LANE:].astype(BF16)
        pb = _dot(hb, win_ref[:, C_QS:C_G])
        qs_ref[...] = pb[:, 0:1024].astype(BF16)
        ks_ref[...] = pb[:, 1024:1152].astype(BF16)
        vs_ref[...] = pb[:, 1152:1280].astype(BF16)
        qm_ref[...] = pb[:, 1280:1792].astype(BF16)
        g_ref[...] = jax.nn.sigmoid(_dot(hb, win_ref[:, C_G:C_END]) + bg_ref[...])

    return _pcall(
        body, name="pre_fwd", grid=(S // ts,),
        in_specs=[_tok(ts, D), _full(1, D), _full(D, C_END), _full(1, 3 * D), _full(1, 256), _full(1, 128),
                  _full(256, NH * LANE), _full(128, 2 * NH * LANE), _tok(ts, LANE), _tok(ts, LANE), _tok(ts, LANE)],
        out_specs=[_tok(ts, 1024), _tok(ts, 1024), _tok(ts, 1024), _tok(ts, 1024), _tok(ts, 128), _tok(ts, 128),
                   _tok(ts, 512), _tok(ts, 3 * D)],
        out_shape=[_sds((S, 1024), BF16), _sds((S, 1024), BF16), _sds((S, 1024), BF16), _sds((S, 1024), BF16),
                   _sds((S, 128), BF16), _sds((S, 128), BF16), _sds((S, 512), BF16), _sds((S, 3 * D), F32)],
        sem=("arbitrary",),
    )(x, an, win, bg, qn, kvn, wuq, wukv, rc, rs1, rs2)


def _merge_fwd(x, g, oa, ob, oc, wa, wb, wc, wout, ts):
    S = x.shape[0]

    def body(x_ref, g_ref, oa_ref, ob_ref, oc_ref, wa_ref, wb_ref, wc_ref, wout_ref, x1_ref, yb_ref):
        y = g_ref[:, 0:D] * _dot(oa_ref[...], wa_ref[...])
        y = y + g_ref[:, D:2 * D] * _dot(ob_ref[...], wb_ref[...])
        y = y + g_ref[:, 2 * D:3 * D] * _dot(oc_ref[...], wc_ref[...])
        yb = y.astype(BF16)
        yb_ref[...] = yb
        x1_ref[...] = x_ref[...] + _dot(yb, wout_ref[...])

    return _pcall(
        body, name="merge_fwd", grid=(S // ts,),
        in_specs=[_tok(ts, D), _tok(ts, 3 * D), _tok(ts, 1024), _tok(ts, 1024), _tok(ts, 512),
                  _full(1024, D), _full(1024, D), _full(512, D), _full(D, D)],
        out_specs=[_tok(ts, D), _tok(ts, D)],
        out_shape=[_sds((S, D), F32), _sds((S, D), BF16)],
        sem=("arbitrary",),
    )(x, g, oa, ob, oc, wa, wb, wc, wout)


def _mlp_fwd(x1, mn, wup, wdown, ts):
    S = x1.shape[0]

    def body(x_ref, mn_ref, wup_ref, wdown_ref, x2_ref):
        xv = x_ref[...]
        n, _ = _rms(xv)
        u = _dot((n * mn_ref[...]).astype(BF16), wup_ref[...])
        a = jnp.square(jnp.maximum(u, 0.0))
        x2_ref[...] = xv + _dot(a.astype(BF16), wdown_ref[...])

    return _pcall(
        body, name="mlp_fwd", grid=(S // ts,),
        in_specs=[_tok(ts, D), _full(1, D), _full(D, DFF), _full(DFF, D)],
        out_specs=_tok(ts, D), out_shape=_sds((S, D), F32), sem=("arbitrary",),
    )(x1, mn, wup, wdown)


def _loss_kernel(x, fn, tgt, ts):
    S = x.shape[0]

    def body(x_ref, fn_ref, t_ref, loss_ref, dx_ref, dfn_ref):
        @pl.when(pl.program_id(0) == 0)
        def _():
            loss_ref[...] = jnp.zeros_like(loss_ref)
            dfn_ref[...] = jnp.zeros_like(dfn_ref)

        n, r = _rms(x_ref[...])
        err = n * fn_ref[...] - t_ref[...]
        loss_ref[...] += jnp.sum(err * err)
        dy = err * (1.0 / D)
        dfn_ref[...] += _colsum(dy * n)
        dx_ref[...] = _rms_bwd(dy * fn_ref[...], n, r)

    return _pcall(
        body, name="loss_head", grid=(S // ts,),
        in_specs=[_tok(ts, D), _full(1, D), _tok(ts, D)],
        out_specs=[_full(1, LANE), _tok(ts, D), _full(1, D)],
        out_shape=[_sds((1, LANE), F32), _sds((S, D), F32), _sds((1, D), F32)],
        sem=("arbitrary",),
    )(x, fn, tgt)


def _mlp_bwd(dx2, x1, mn, wup, wdown, ts):
    S = x1.shape[0]

    def body(dx_ref, x_ref, mn_ref, wup_ref, wdown_ref, dx1_ref, hb_ref, dub_ref, ab_ref, dxb_ref, dmn_ref):
        @pl.when(pl.program_id(0) == 0)
        def _():
            dmn_ref[...] = jnp.zeros_like(dmn_ref)

        dx = dx_ref[...]
        n, r = _rms(x_ref[...])
        g = mn_ref[...]
        hb = (n * g).astype(BF16)
        hb_ref[...] = hb
        rl = jnp.maximum(_dot(hb, wup_ref[...]), 0.0)
        ab_ref[...] = jnp.square(rl).astype(BF16)
        dxb = dx.astype(BF16)
        dxb_ref[...] = dxb
        dub = (_dot_nt(dxb, wdown_ref[...]) * (2.0 * rl)).astype(BF16)
        dub_ref[...] = dub
        dh = _dot_nt(dub, wup_ref[...])
        dmn_ref[...] += _colsum(dh * n)
        dx1_ref[...] = dx + _rms_bwd(dh * g, n, r)

    return _pcall(
        body, name="mlp_bwd", grid=(S // ts,),
        in_specs=[_tok(ts, D), _tok(ts, D), _full(1, D), _full(D, DFF), _full(DFF, D)],
        out_specs=[_tok(ts, D), _tok(ts, D), _tok(ts, DFF), _tok(ts, DFF), _tok(ts, D), _full(1, D)],
        out_shape=[_sds((S, D), F32), _sds((S, D), BF16), _sds((S, DFF), BF16), _sds((S, DFF), BF16),
                   _sds((S, D), BF16), _sds((1, D), F32)],
        sem=("arbitrary",),
    )(dx2, x1, mn, wup, wdown)


def _merge_bwd(dx1, g, oa, ob, oc, wa, wb, wc, wout, ts):
    S = dx1.shape[0]

    def body(dx_ref, g_ref, oa_ref, ob_ref, oc_ref, wa_ref, wb_ref, wc_ref, wout_ref,
             dgp_ref, dyo_ref, doa_ref, dob_ref, doc_ref, dla_ref, dxb_ref, dbg_ref):
        @pl.when(pl.program_id(0) == 0)
        def _():
            dbg_ref[...] = jnp.zeros_like(dbg_ref)

        dxb = dx_ref[...].astype(BF16)
        dxb_ref[...] = dxb
        dy = _dot_nt(dxb, wout_ref[...])
        branches = ((oa_ref, wa_ref, doa_ref), (ob_ref, wb_ref, dob_ref), (oc_ref, wc_ref, doc_ref))
        for b, (o_ref, w_ref, do_ref) in enumerate(branches):
            cols = slice(b * D, (b + 1) * D)
            gb = g_ref[:, cols]
            o = o_ref[...]
            dgpre = dy * _dot(o, w_ref[...]) * gb * (1.0 - gb)
            dgp_ref[:, cols] = dgpre.astype(BF16)
            dbg_ref[:, cols] += _colsum(dgpre)
            dyo = (dy * gb).astype(BF16)
            dyo_ref[:, cols] = dyo
            do = _dot_nt(dyo, w_ref[...])
            do_ref[...] = do.astype(BF16)
            if b == 0:
                lane = lax.broadcasted_iota(jnp.int32, (ts, LANE), 1)
                dls = jnp.zeros((ts, LANE), F32)
                for h in range(NH):
                    dl = jnp.sum(do[:, _hs(h)] * o[:, _hs(h)].astype(F32), axis=1, keepdims=True)
                    dls = jnp.where(lane == h, dl, dls)
                dla_ref[:, 0, :] = jnp.transpose(dls)[0:NH, :]

    return _pcall(
        body, name="merge_bwd", grid=(S // ts,),
        in_specs=[_tok(ts, D), _tok(ts, 3 * D), _tok(ts, 1024), _tok(ts, 1024), _tok(ts, 512),
                  _full(1024, D), _full(1024, D), _full(512, D), _full(D, D)],
        out_specs=[_tok(ts, 3 * D), _tok(ts, 3 * D), _tok(ts, 1024), _tok(ts, 1024), _tok(ts, 512),
                   pl.BlockSpec((NH, 1, ts), lambda i: (0, 0, i)), _tok(ts, D), _full(1, 3 * D)],
        out_shape=[_sds((S, 3 * D), BF16), _sds((S, 3 * D), BF16), _sds((S, 1024), BF16), _sds((S, 1024), BF16),
                   _sds((S, 512), BF16), _sds((NH, 1, S), F32), _sds((S, D), BF16), _sds((1, 3 * D), F32)],
        sem=("arbitrary",),
    )(dx1, g, oa, ob, oc, wa, wb, wc, wout)


def _pre_bwd(x, dx1, dq, dk, dv, dqs, dks, dvs, dqm, dgp, an, qn, kvn, win, wuq, wukv, rc, rs1, rs2, ts):
    S = x.shape[0]

    def body(x_ref, dx1_ref, dq_ref, dk_ref, dv_ref, dqs_ref, dks_ref, dvs_ref, dqm_ref, dgp_ref,
             an_ref, qn_ref, kvn_ref, win_ref, wuq_ref, wukv_ref, rc_ref, rs1_ref, rs2_ref,
             dx_ref, dproj_ref, hb_ref, cqn_ref, ckvn_ref, dqpre_ref, dkv_ref, dan_ref, dqn_ref, dkvn_ref):
        @pl.when(pl.program_id(0) == 0)
        def _():
            dan_ref[...] = jnp.zeros_like(dan_ref)
            dqn_ref[...] = jnp.zeros_like(dqn_ref)
            dkvn_ref[...] = jnp.zeros_like(dkvn_ref)

        n, r = _rms(x_ref[...])
        hb = (n * an_ref[...]).astype(BF16)
        hb_ref[...] = hb
        pa = _dot(hb, win_ref[:, C_CQ:C_KPE])
        ncq, rq = _rms(pa[:, 0:256])
        cqn_ref[...] = (ncq * qn_ref[...]).astype(BF16)
        nkv, rkv = _rms(pa[:, 256:384])
        ckvn_ref[...] = (nkv * kvn_ref[...]).astype(BF16)
        c, s1, s2 = rc_ref[...], rs1_ref[...], rs2_ref[...]

        dkper = jnp.zeros((ts, LANE), F32)
        for h in range(NH):
            dqpre_ref[:, _hs(h)] = _rope_bwd(dq_ref[:, _hs(h)], c, s1, s2).astype(BF16)
            dkh = dk_ref[:, _hs(h)]
            dkper = dkper + dkh
            dkv_ref[:, _hs(h)] = dkh.astype(BF16)
        dkv_ref[:, NH * LANE:] = dv_ref[...].astype(BF16)

        dcqn = _dot_nt(dqpre_ref[...], wuq_ref[...])
        dqn_ref[...] += _colsum(dcqn * ncq)
        dproj_ref[:, C_CQ:C_CKV] = _rms_bwd(dcqn * qn_ref[...], ncq, rq).astype(BF16)
        dckvn = _dot_nt(dkv_ref[...], wukv_ref[...])
        dkvn_ref[...] += _colsum(dckvn * nkv)
        dproj_ref[:, C_CKV:C_KPE] = _rms_bwd(dckvn * kvn_ref[...], nkv, rkv).astype(BF16)
        lane = lax.broadcasted_iota(jnp.int32, (ts, LANE), 1)
        dkpe = jnp.where((lane >= 64) & (lane < 96), _rope_bwd(dkper, c, s1, s2), 0.0)
        dproj_ref[:, C_KPE:C_QS] = dkpe.astype(BF16)
        dproj_ref[:, C_QS:C_KS] = dqs_ref[...]
        dproj_ref[:, C_KS:C_VS] = dks_ref[...].astype(BF16)
        dproj_ref[:, C_VS:C_QM] = dvs_ref[...].astype(BF16)
        dproj_ref[:, C_QM:C_G] = dqm_ref[...]
        dproj_ref[:, C_G:C_END] = dgp_ref[...]

        dh = _dot_nt(dproj_ref[...], win_ref[...])
        dan_ref[...] += _colsum(dh * n)
        dx_ref[...] = dx1_ref[...] + _rms_bwd(dh * an_ref[...], n, r)

    return _pcall(
        body, name="pre_bwd", grid=(S // ts,),
        in_specs=[_tok(ts, D), _tok(ts, D), _tok(ts, 1024), _tok(ts, 1024), _tok(ts, 1024), _tok(ts, 1024),
                  _tok(ts, 128), _tok(ts, 128), _tok(ts, 512), _tok(ts, 3 * D),
                  _full(1, D), _full(1, 256), _full(1, 128), _full(D, C_END), _full(256, NH * LANE),
                  _full(128, 2 * NH * LANE), _tok(ts, LANE), _tok(ts, LANE), _tok(ts, LANE)],
        out_specs=[_tok(ts, D), _tok(ts, C_END), _tok(ts, D), _tok(ts, 256), _tok(ts, 128), _tok(ts, 1024),
                   _tok(ts, 2048), _full(1, D), _full(1, 256), _full(1, 128)],
        out_shape=[_sds((S, D), F32), _sds((S, C_END), BF16), _sds((S, D), BF16), _sds((S, 256), BF16),
                   _sds((S, 128), BF16), _sds((S, 1024), BF16), _sds((S, 2048), BF16), _sds((1, D), F32),
                   _sds((1, 256), F32), _sds((1, 128), F32)],
        sem=("arbitrary",),
    )(x, dx1, dq, dk, dv, dqs, dks, dvs, dqm, dgp, an, qn, kvn, win, wuq, wukv, rc, rs1, rs2)


def _pick_tile(n, cap):
    best = LANE
    for t in range(LANE, min(n, cap) + 1, LANE):
        if n % t == 0:
            best = t
    return best


def _matmul_tn(a, b, name, shard_axis=None):
    S, M = a.shape
    N = b.shape[1]
    tm = _pick_tile(M // 4 if shard_axis == 0 else M, 1024)
    tn = _pick_tile(N // 4 if shard_axis == 1 else N, 2048)
    ts = min(S, 1024)
    nk = S // ts

    def body(a_ref, b_ref, o_ref, *acc):
        acc_ref = acc[0] if acc else o_ref

        @pl.when(pl.program_id(2) == 0)
        def _():
            acc_ref[...] = jnp.zeros_like(acc_ref)

        acc_ref[...] += _dot_tn(a_ref[...], b_ref[...])
        if acc:
            @pl.when(pl.program_id(2) == nk - 1)
            def _():
                o_ref[0] = acc_ref[...].astype(o_ref.dtype)

    if shard_axis is None:
        out_spec = pl.BlockSpec((tm, tn), lambda i, j, k: (i, j))
        out_shape, scratch = _sds((M, N), F32), ()
    elif shard_axis == 0:
        per = (M // 4) // tm
        out_spec = pl.BlockSpec((1, tm, tn), lambda i, j, k: (i // per, i % per, j))
        out_shape, scratch = _sds((4, M // 4, N), BF16), (pltpu.VMEM((tm, tn), F32),)
    else:
        per = (N // 4) // tn
        out_spec = pl.BlockSpec((1, tm, tn), lambda i, j, k: (j // per, i, j % per))
        out_shape, scratch = _sds((4, M, N // 4), BF16), (pltpu.VMEM((tm, tn), F32),)
    return _pcall(
        body, name=name, grid=(M // tm, N // tn, nk),
        in_specs=[pl.BlockSpec((ts, tm), lambda i, j, k: (k, i)), pl.BlockSpec((ts, tn), lambda i, j, k: (k, j))],
        out_specs=out_spec, out_shape=out_shape, scratch=scratch, sem=("parallel", "parallel", "arbitrary"),
    )(a, b)


def _select_fwd(a, sel, name):
    _, L, M, K = a.shape
    N = sel.shape[2]
    tn = _pick_tile(N, 1792)

    def body(a_ref, s_ref, o_ref, acc_ref):
        s = pl.program_id(2)

        @pl.when(s == 0)
        def _():
            acc_ref[...] = jnp.zeros_like(acc_ref)

        acc_ref[...] += _dot(a_ref[0, 0], s_ref[0])

        @pl.when(s == 3)
        def _():
            o_ref[0] = acc_ref[...].astype(BF16)

    return _pcall(
        body, name=name, grid=(L, N // tn, 4),
        in_specs=[pl.BlockSpec((1, 1, M, K), lambda l, j, s: (s, l, 0, 0)),
                  pl.BlockSpec((1, K, tn), lambda l, j, s: (s, 0, j))],
        out_specs=pl.BlockSpec((1, M, tn), lambda l, j, s: (l, 0, j)),
        out_shape=_sds((L, M, N), BF16), scratch=(pltpu.VMEM((M, tn), F32),),
        sem=("parallel", "parallel", "arbitrary"),
    )(a, sel)


def _select_bwd(dw, sel, name):
    L, M, N = dw.shape
    K = sel.shape[1]
    tk = _pick_tile(N, 1792)
    nk = N // tk

    def body(d_ref, s_ref, o_ref, acc_ref):
        k = pl.program_id(2)

        @pl.when(k == 0)
        def _():
            acc_ref[...] = jnp.zeros_like(acc_ref)

        acc_ref[...] += _dot_nt(d_ref[0].astype(BF16), s_ref[0])

        @pl.when(k == nk - 1)
        def _():
            o_ref[0, 0] = acc_ref[...].astype(BF16)

    return _pcall(
        body, name=name, grid=(L, 4, nk),
        in_specs=[pl.BlockSpec((1, M, tk), lambda l, s, k: (l, 0, k)),
                  pl.BlockSpec((1, K, tk), lambda l, s, k: (s, 0, k))],
        out_specs=pl.BlockSpec((1, 1, M, K), lambda l, s, k: (l, s, 0, 0)),
        out_shape=_sds((L, 4, M, K), BF16), scratch=(pltpu.VMEM((M, K), F32),),
        sem=("parallel", "parallel", "arbitrary"),
    )(dw, sel)


MLA_RC = 128
MLA_RC_FWD = 128
LOG2E = math.log2(math.e)
MLA_QSCALE = MLA_SCALE * LOG2E


def _ride_refs(ride, rest, n_out, n_scratch):
    ni, no = (len(ride.inputs), len(ride.out_shapes)) if ride else (0, 0)
    own_out = rest[ni:ni + n_out]
    own_scratch = rest[ni + n_out + no:ni + n_out + no + n_scratch]
    parts = rest[:ni], rest[ni + n_out:ni + n_out + no], rest[ni + n_out + no + n_scratch:]
    return own_out, own_scratch, parts


def _ride_specs(ride):
    hbm = pl.BlockSpec(memory_space=pl.ANY)
    if not ride:
        return [], [], [], [], []
    return ([hbm] * len(ride.inputs), [hbm] * len(ride.out_shapes), list(ride.out_shapes), list(ride.sems),
            list(ride.inputs))


def _mla_fwd(q, k, v, tq, ride=None):
    S = q.shape[0]
    nq = S // tq
    pairs = [(i, j) for i in range(nq) for j in range(i + 1)]
    qi = jnp.asarray(np.array([p[0] for p in pairs], np.int32))
    kj = jnp.asarray(np.array([p[1] for p in pairs], np.int32))

    def body(qi_ref, kj_ref, q_ref, k_ref, v_ref, *rest):
        (o_ref, lse_ref), (m_s, l_s, acc_s), riding = _ride_refs(ride, rest, 2, 3)
        t = pl.program_id(1)
        i, j = qi_ref[t], kj_ref[t]
        if ride:
            @pl.when((pl.program_id(0) == 0) & (t == 0))
            def _():
                ride.start(*riding)

        @pl.when(j == 0)
        def _():
            m_s[...] = jnp.full_like(m_s, NEG)
            l_s[...] = jnp.zeros_like(l_s)
            acc_s[...] = jnp.zeros_like(acc_s)

        def step(masked):
            rc = min(MLA_RC_FWD, tq)
            nc = tq // rc
            keys = [(c + 1) * rc if masked else tq for c in range(nc)]
            scores = [_dot_nt(q_ref[c * rc:(c + 1) * rc, :], k_ref[0:keys[c], :]) for c in range(nc)]
            for c in range(nc):
                rows = slice(c * rc, (c + 1) * rc)
                s = scores[c]
                if masked:
                    row = lax.broadcasted_iota(jnp.int32, (rc, keys[c]), 0) + c * rc
                    col = lax.broadcasted_iota(jnp.int32, (rc, keys[c]), 1)
                    s = jnp.where(col <= row, s, NEG)
                tiles = [s[:, _hs(u)] for u in range(keys[c] // LANE)]
                mx = functools.reduce(jnp.maximum, tiles)
                m_old = m_s[rows, :]
                m_new = jnp.maximum(m_old, jnp.max(mx, axis=1, keepdims=True))
                alpha = jnp.exp2(m_old - m_new)
                ps = [jnp.exp2(u - m_new) for u in tiles]
                l_s[rows, :] = alpha * l_s[rows, :] + functools.reduce(jnp.add, ps)
                p = jnp.concatenate([u.astype(BF16) for u in ps], axis=1)
                acc_s[rows, :] = alpha * acc_s[rows, :] + _dot(p, v_ref[0:keys[c], :])
                m_s[rows, :] = m_new

        @pl.when(j < i)
        def _():
            step(False)

        @pl.when(j == i)
        def _():
            step(True)
            l = jnp.sum(l_s[...], axis=1, keepdims=True)
            o_ref[...] = (acc_s[...] / l).astype(BF16)
            lse_ref[0] = jnp.transpose(m_s[...] + jnp.log2(l))[0:1, :]

        if ride:
            @pl.when((pl.program_id(0) == NH - 1) & (t == len(pairs) - 1))
            def _():
                ride.finish(*riding)

    qmap = lambda h, t, qi_r, kj_r: (qi_r[t], h)
    kmap = lambda h, t, qi_r, kj_r: (kj_r[t], h)
    r_in, r_out, r_shapes, r_sems, r_args = _ride_specs(ride)
    o, lse, *carried = _pcall(
        body, name="mla_fwd", grid=(NH, len(pairs)), prefetch=2,
        in_specs=[pl.BlockSpec((tq, LANE), qmap), pl.BlockSpec((tq, LANE), kmap), pl.BlockSpec((tq, LANE), kmap)]
        + r_in,
        out_specs=[pl.BlockSpec((tq, LANE), qmap),
                   pl.BlockSpec((1, 1, tq), lambda h, t, qi_r, kj_r: (h, 0, qi_r[t]))] + r_out,
        out_shape=[_sds((S, NH * LANE), BF16), _sds((NH, 1, S), F32)] + r_shapes,
        scratch=[pltpu.VMEM((tq, LANE), F32), pltpu.VMEM((tq, LANE), F32), pltpu.VMEM((tq, LANE), F32)] + r_sems,
        sem=("arbitrary", "arbitrary"),
    )(qi, kj, q, k, v, *r_args)
    return o, lse, carried


def _mla_bwd(q, k, v, do, lse, delta, tq, ride=None):
    S = q.shape[0]
    nq = S // tq
    pairs = [(i, j) for j in range(nq) for i in range(j, nq)]
    qi = jnp.asarray(np.array([p[0] for p in pairs], np.int32))
    kj = jnp.asarray(np.array([p[1] for p in pairs], np.int32))

    def body(qi_ref, kj_ref, q_ref, k_ref, v_ref, do_ref, lse_ref, dl_ref, *rest):
        (dq_ref, dk_ref, dv_ref), (dq_s, dk_s, dv_s), riding = _ride_refs(ride, rest, 3, 3)
        t = pl.program_id(1)
        i, j = qi_ref[t], kj_ref[t]
        if ride:
            @pl.when((pl.program_id(0) == 0) & (t == 0))
            def _():
                ride.start(*riding)

        @pl.when(t == 0)
        def _():
            dq_s[...] = jnp.zeros_like(dq_s)

        @pl.when(i == j)
        def _():
            dk_s[...] = jnp.zeros_like(dk_s)
            dv_s[...] = jnp.zeros_like(dv_s)

        qrows = pl.ds(pl.multiple_of(i * tq, tq), tq)

        def step(masked):
            lse_r, dl_r = lse_ref[0], dl_ref[0]
            dq = jnp.zeros((tq, LANE), F32)
            nc = tq // MLA_RC
            q0 = [c * MLA_RC if masked else 0 for c in range(nc)]
            sts = [_dot_nt(k_ref[c * MLA_RC:(c + 1) * MLA_RC, :], q_ref[q0[c]:, :]) for c in range(nc)]
            dpts = [_dot_nt(v_ref[c * MLA_RC:(c + 1) * MLA_RC, :], do_ref[q0[c]:, :]) for c in range(nc)]
            for c in range(nc):
                rows = slice(c * MLA_RC, (c + 1) * MLA_RC)
                qb, dob = q_ref[q0[c]:, :], do_ref[q0[c]:, :]
                pt = jnp.exp2(sts[c] - lse_r[:, q0[c]:])
                if masked:
                    key = lax.broadcasted_iota(jnp.int32, (MLA_RC, tq - q0[c]), 0)
                    qry = lax.broadcasted_iota(jnp.int32, (MLA_RC, tq - q0[c]), 1)
                    pt = jnp.where(key <= qry, pt, 0.0)
                dv_s[rows, :] += _dot(pt.astype(BF16), dob)
                gt = (pt * (dpts[c] - dl_r[:, q0[c]:])).astype(BF16)
                dk_s[rows, :] += _dot(gt, qb)
                part = _dot_tn(gt, k_ref[rows, :])
                if masked:
                    at = pl.multiple_of(i * tq + q0[c], MLA_RC)
                    dq_s[pl.ds(at, tq - q0[c]), :] += part
                else:
                    dq = dq + part
            if not masked:
                dq_s[qrows, :] += dq

        @pl.when(i > j)
        def _():
            step(False)

        @pl.when(i == j)
        def _():
            step(True)
            dq_ref[...] = dq_s[qrows, :] * MLA_SCALE

        @pl.when(i == nq - 1)
        def _():
            dk_ref[...] = dk_s[...] * (1.0 / LOG2E)
            dv_ref[...] = dv_s[...]

        if ride:
            @pl.when((pl.program_id(0) == NH - 1) & (t == len(pairs) - 1))
            def _():
                ride.finish(*riding)

    qmap = lambda h, t, qi_r, kj_r: (qi_r[t], h)
    kmap = lambda h, t, qi_r, kj_r: (kj_r[t], h)
    rmap = lambda h, t, qi_r, kj_r: (h, 0, qi_r[t])
    r_in, r_out, r_shapes, r_sems, r_args = _ride_specs(ride)
    dq, dk, dv, *carried = _pcall(
        body, name="mla_bwd", grid=(NH, len(pairs)), prefetch=2,
        in_specs=[pl.BlockSpec((tq, LANE), qmap), pl.BlockSpec((tq, LANE), kmap), pl.BlockSpec((tq, LANE), kmap),
                  pl.BlockSpec((tq, LANE), qmap), pl.BlockSpec((1, 1, tq), rmap), pl.BlockSpec((1, 1, tq), rmap)]
        + r_in,
        out_specs=[pl.BlockSpec((tq, LANE), kmap), pl.BlockSpec((tq, LANE), kmap), pl.BlockSpec((tq, LANE), kmap)]
        + r_out,
        out_shape=[_sds((S, NH * LANE), F32), _sds((S, NH * LANE), F32), _sds((S, NH * LANE), F32)] + r_shapes,
        scratch=[pltpu.VMEM((S, LANE), F32), pltpu.VMEM((tq, LANE), F32), pltpu.VMEM((tq, LANE), F32)] + r_sems,
        sem=("arbitrary", "arbitrary"),
    )(qi, kj, q, k, v, do, lse, delta, *r_args)
    return dq, dk, dv, carried


SWA_SUB = 4
SWA_T = SWA_SUB * WIN


def _swa_specs(nsteps, rev):
    step = (lambda i: nsteps - 1 - i) if rev else (lambda i: i)
    cur = lambda w: pl.BlockSpec((SWA_T, w), lambda i: (step(i), 0))
    prev = pl.BlockSpec((WIN, LANE), lambda i: (jnp.maximum(step(i) * SWA_SUB - 1, 0), 0))
    return step, cur, prev


def _swa_probs(qk, bias_h, sink, first_mask):
    s = qk * SWA_SCALE + bias_h
    if first_mask is not None:
        s = jnp.where(first_mask, NEG, s)
    m = jnp.maximum(jnp.max(s, axis=1, keepdims=True), sink)
    e = jnp.exp(s - m)
    es = jnp.exp(sink - m)
    inv = 1.0 / (jnp.sum(e, axis=1, keepdims=True) + es)
    return e * inv, es * inv


SWA_GR = SWA_R * WIN


def _swa_group(ref, rows, g):
    return jnp.concatenate([ref[rows, _hs(g * SWA_R + r)] for r in range(SWA_R)], axis=0)


def _swa_rows(bias, sinks):
    sink_rows = jnp.broadcast_to(sinks[:, None, :], (NH, WIN, LANE)).reshape(NH * WIN, LANE)
    return bias.reshape(NH * WIN, 2 * WIN), sink_rows


def _swa_fwd(qs, ks, vs, bias, sinks):
    S = qs.shape[0]
    nsteps = S // SWA_T
    step, cur, prev = _swa_specs(nsteps, False)

    def body(qs_ref, kc_ref, kp_ref, vc_ref, vp_ref, bias_ref, sk_ref, o_ref):
        first = pl.program_id(0) == 0
        kk = jnp.concatenate([kp_ref[...], kc_ref[...]], axis=0)
        vv = jnp.concatenate([vp_ref[...], vc_ref[...]], axis=0)
        col = lax.broadcasted_iota(jnp.int32, (WIN, 2 * WIN), 1)
        for b in range(SWA_SUB):
            kkb = kk[b * WIN:(b + 2) * WIN]
            vvb = vv[b * WIN:(b + 2) * WIN]
            fm = (first & (col < WIN)) if b == 0 else None
            rows = slice(b * WIN, (b + 1) * WIN)
            qks = [_dot_nt(qs_ref[rows, _hs(h)], kkb) for h in range(NH)]
            for h in range(NH):
                p, _ = _swa_probs(qks[h], bias_ref[h], sk_ref[h:h + 1, 0:1], fm)
                o_ref[rows, _hs(h)] = _dot(p.astype(BF16), vvb).astype(BF16)

    return _pcall(
        body, name="swa_fwd", grid=(nsteps,),
        in_specs=[cur(NH * LANE), cur(LANE), prev, cur(LANE), prev, _full(NH, WIN, 2 * WIN), _full(NH, LANE)],
        out_specs=cur(NH * LANE), out_shape=_sds((S, NH * LANE), BF16), sem=("arbitrary",),
    )(qs, ks, ks, vs, vs, bias, sinks)


def _swa_bwd(qs, ks, vs, do, bias, sinks):
    S = qs.shape[0]
    nsteps = S // SWA_T
    step, cur, prev = _swa_specs(nsteps, True)
    bias_rows, sink_rows = _swa_rows(bias, sinks)

    def body(qs_ref, kc_ref, kp_ref, vc_ref, vp_ref, do_ref, bias_ref, sk_ref,
             dqs_ref, dks_ref, dvs_ref, dbias_ref, dsk_ref, dkk_s, dvv_s, ck_s, cv_s):
        pid = pl.program_id(0)
        first = step(pid) == 0

        @pl.when(pid == 0)
        def _():
            dbias_ref[...] = jnp.zeros_like(dbias_ref)
            dsk_ref[...] = jnp.zeros_like(dsk_ref)
            ck_s[...] = jnp.zeros_like(ck_s)
            cv_s[...] = jnp.zeros_like(cv_s)

        dkk_s[...] = jnp.zeros_like(dkk_s)
        dvv_s[...] = jnp.zeros_like(dvv_s)
        kk = jnp.concatenate([kp_ref[...], kc_ref[...]], axis=0)
        vv = jnp.concatenate([vp_ref[...], vc_ref[...]], axis=0)
        col = lax.broadcasted_iota(jnp.int32, (SWA_GR, 2 * WIN), 1)
        for b in range(SWA_SUB):
            kkb = kk[b * WIN:(b + 2) * WIN]
            vvb = vv[b * WIN:(b + 2) * WIN]
            fm = (first & (col < WIN)) if b == 0 else None
            rows = slice(b * WIN, (b + 1) * WIN)
            keys = slice(b * WIN, (b + 2) * WIN)
            qg = [_swa_group(qs_ref, rows, g) for g in range(2)]
            dog = [_swa_group(do_ref, rows, g) for g in range(2)]
            qks = [_dot_nt(qg[g], kkb) for g in range(2)]
            dps = [_dot_nt(dog[g], vvb) for g in range(2)]
            for g in range(2):
                grows = slice(g * SWA_GR, (g + 1) * SWA_GR)
                p, ps = _swa_probs(qks[g], bias_ref[grows, :], sk_ref[grows, 0:1], fm)
                dl = jnp.sum(p * dps[g], axis=1, keepdims=True)
                ds = p * (dps[g] - dl)
                sink_part = ps * dl
                for r in range(SWA_R):
                    h = g * SWA_R + r
                    dsk_ref[h:h + 1, :] += -jnp.sum(sink_part[r * WIN:(r + 1) * WIN])
                dbias_ref[grows, :] += ds
                dsb = (ds * SWA_SCALE).astype(BF16)
                dq = _dot(dsb, kkb).astype(BF16)
                for r in range(SWA_R):
                    dqs_ref[rows, _hs(g * SWA_R + r)] = dq[r * WIN:(r + 1) * WIN]
                dkk_s[keys, :] += _dot_tn(dsb, qg[g])
                dvv_s[keys, :] += _dot_tn(p.astype(BF16), dog[g])
        dks_ref[...] = dkk_s[WIN:, :]
        dvs_ref[...] = dvv_s[WIN:, :]
        dks_ref[SWA_T - WIN:, :] += ck_s[...]
        dvs_ref[SWA_T - WIN:, :] += cv_s[...]
        ck_s[...] = dkk_s[0:WIN, :]
        cv_s[...] = dvv_s[0:WIN, :]

    dqs, dks, dvs, dbias, dsink = _pcall(
        body, name="swa_bwd", grid=(nsteps,),
        in_specs=[cur(NH * LANE), cur(LANE), prev, cur(LANE), prev, cur(NH * LANE), _full(NH * WIN, 2 * WIN),
                  _full(NH * WIN, LANE)],
        out_specs=[cur(NH * LANE), cur(LANE), cur(LANE), _full(NH * WIN, 2 * WIN), _full(NH, LANE)],
        out_shape=[_sds((S, NH * LANE), BF16), _sds((S, LANE), F32), _sds((S, LANE), F32),
                   _sds((NH * WIN, 2 * WIN), F32), _sds((NH, LANE), F32)],
        scratch=[pltpu.VMEM((SWA_T + WIN, LANE), F32), pltpu.VMEM((SWA_T + WIN, LANE), F32),
                 pltpu.VMEM((WIN, LANE), F32), pltpu.VMEM((WIN, LANE), F32)],
        sem=("arbitrary",),
    )(qs, ks, ks, vs, vs, do, bias_rows, sink_rows)
    return dqs, dks, dvs, dbias.reshape(NH, WIN, 2 * WIN), dsink


def _bias_build(rel_bias, bmap):
    def body(rb_ref, bmap_ref, o_ref):
        bm = bmap_ref[...]
        for h in range(NH):
            acc = jnp.full((WIN, 2 * WIN), NEG, F32)
            for b in range(REL_BUCKETS):
                acc = jnp.where(bm == b, rb_ref[b, h], acc)
            o_ref[h] = acc

    return _pcall(
        body, name="bias_build", grid=(1,),
        in_specs=[pl.BlockSpec(memory_space=pltpu.SMEM), _full(WIN, 2 * WIN)],
        out_specs=_full(NH, WIN, 2 * WIN), out_shape=_sds((NH, WIN, 2 * WIN), F32), sem=("arbitrary",),
    )(rel_bias, bmap)


def _bias_reduce(dbias, bmap):
    def body(db_ref, bmap_ref, o_ref):
        bm = bmap_ref[...]
        for h in range(NH):
            dbh = db_ref[h]
            for b in range(REL_BUCKETS):
                o_ref[b, h] = jnp.sum(jnp.where(bm == b, dbh, 0.0))

    return _pcall(
        body, name="bias_reduce", grid=(1,),
        in_specs=[_full(NH, WIN, 2 * WIN), _full(WIN, 2 * WIN)],
        out_specs=pl.BlockSpec(memory_space=pltpu.SMEM), out_shape=_sds((REL_BUCKETS, NH), F32), sem=("arbitrary",),
    )(dbias, bmap)


def _memkv_fwd(mem, mnorm, wkv):
    def body(mem_ref, g_ref, w_ref, o_ref):
        n, _ = _rms(mem_ref[...])
        o_ref[...] = _dot((n * g_ref[...]).astype(BF16), w_ref[...]).astype(BF16)

    return _pcall(
        body, name="memkv_fwd", grid=(1,), in_specs=[_full(MEM_LEN, D), _full(1, D), _full(D, D)],
        out_specs=_full(MEM_LEN, D), out_shape=_sds((MEM_LEN, D), BF16), sem=("arbitrary",),
    )(mem, mnorm, wkv)


def _mem_probs(qk):
    s = qk * MEM_SCALE
    e = jnp.exp(s - jnp.max(s, axis=1, keepdims=True))
    return e / jnp.sum(e, axis=1, keepdims=True)


def _mem_fwd(qm, kvm, ts):
    S = qm.shape[0]

    def body(q_ref, kv_ref, o_ref):
        qks = [_dot_nt(q_ref[:, _hs(h)], kv_ref[:, _hs(h)]) for h in range(MEM_H)]
        for h in range(MEM_H):
            p = _mem_probs(qks[h])
            o_ref[:, _hs(h)] = _dot(p.astype(BF16), kv_ref[:, _hs(MEM_H + h)]).astype(BF16)

    return _pcall(
        body, name="mem_fwd", grid=(S // ts,), in_specs=[_tok(ts, 512), _full(MEM_LEN, D)],
        out_specs=_tok(ts, 512), out_shape=_sds((S, 512), BF16), sem=("arbitrary",),
    )(qm, kvm)


def _mem_bwd(qm, kvm, do, ts):
    S = qm.shape[0]

    def body(q_ref, kv_ref, do_ref, dq_ref, dkv_ref):
        @pl.when(pl.program_id(0) == 0)
        def _():
            dkv_ref[...] = jnp.zeros_like(dkv_ref)

        qks = [_dot_nt(q_ref[:, _hs(h)], kv_ref[:, _hs(h)]) for h in range(MEM_H)]
        dps = [_dot_nt(do_ref[:, _hs(h)], kv_ref[:, _hs(MEM_H + h)]) for h in range(MEM_H)]
        for h in range(MEM_H):
            qh, kh, doh = q_ref[:, _hs(h)], kv_ref[:, _hs(h)], do_ref[:, _hs(h)]
            p = _mem_probs(qks[h])
            dp = dps[h]
            ds = (p * (dp - jnp.sum(p * dp, axis=1, keepdims=True)) * MEM_SCALE).astype(BF16)
            dq_ref[:, _hs(h)] = _dot(ds, kh).astype(BF16)
            dkv_ref[:, _hs(h)] += _dot_tn(ds, qh)
            dkv_ref[:, _hs(MEM_H + h)] += _dot_tn(p.astype(BF16), doh)

    return _pcall(
        body, name="mem_bwd", grid=(S // ts,), in_specs=[_tok(ts, 512), _full(MEM_LEN, D), _tok(ts, 512)],
        out_specs=[_tok(ts, 512), _full(MEM_LEN, D)],
        out_shape=[_sds((S, 512), BF16), _sds((MEM_LEN, D), F32)], sem=("arbitrary",),
    )(qm, kvm, do)


def _memkv_bwd(mem, mnorm, wkv, dkvm):
    def body(mem_ref, g_ref, w_ref, dkv_ref, dw_ref, dg_ref):
        n, _ = _rms(mem_ref[...])
        dkvb = dkv_ref[...].astype(BF16)
        dw_ref[...] = _dot_tn((n * g_ref[...]).astype(BF16), dkvb)
        dg_ref[...] = _colsum(_dot_nt(dkvb, w_ref[...]) * n)

    return _pcall(
        body, name="memkv_bwd", grid=(1,), in_specs=[_full(MEM_LEN, D), _full(1, D), _full(D, D), _full(MEM_LEN, D)],
        out_specs=[_full(D, D), _full(1, D)], out_shape=[_sds((D, D), F32), _sds((1, D), F32)], sem=("arbitrary",),
    )(mem, mnorm, wkv, dkvm)


def _adamw(w, g, m, v, name):
    rows, cols = w.shape
    tr = min(rows, FLAT_TILE)
    assert rows % tr == 0

    def body(w_ref, g_ref, m_ref, v_ref, d_ref, nm_ref, nv_ref):
        gv = g_ref[...]
        nm = ADAM_B1 * m_ref[...] + (1.0 - ADAM_B1) * gv
        nv = ADAM_B2 * v_ref[...] + (1.0 - ADAM_B2) * jnp.square(gv)
        m_hat = nm / (1.0 - ADAM_B1 ** ADAM_STEP)
        v_hat = nv / (1.0 - ADAM_B2 ** ADAM_STEP)
        d_ref[...] = -ADAM_LR * (m_hat / (jnp.sqrt(v_hat) + ADAM_EPS) + ADAM_WD * w_ref[...])
        nm_ref[...] = nm
        nv_ref[...] = nv

    spec = _tok(tr, cols)
    return _pcall(
        body, name=name, grid=(rows // tr,), in_specs=[spec] * 4, out_specs=[spec] * 3,
        out_shape=[_sds((rows, cols), F32)] * 3, sem=("arbitrary",),
    )(w, g, m, v)


def _my_place():
    return lax.axis_index("x"), lax.axis_index("y"), lax.axis_index("c")


def _remote(src, dst, send_sems, recv_sems, k, to):
    return pltpu.make_async_remote_copy(src_ref=src, dst_ref=dst, send_sem=send_sems.at[k], recv_sem=recv_sems.at[k],
                                        device_id=to, device_id_type=MESH_ID)


class _Rider(NamedTuple):
    inputs: list
    out_shapes: list
    sems: list
    start: Callable
    finish: Callable


def _run_rider(rider, name):
    ni, no = len(rider.inputs), len(rider.out_shapes)

    def body(*refs):
        parts = refs[:ni], refs[ni:ni + no], refs[ni + no:]
        rider.start(*parts)
        rider.finish(*parts)

    hbm = pl.BlockSpec(memory_space=pl.ANY)
    return pl.pallas_call(body, name=name, out_shape=rider.out_shapes, in_specs=[hbm] * ni, out_specs=[hbm] * no,
                          scratch_shapes=rider.sems)(*rider.inputs)


def _join_riders(*riders):
    def cut(seq, lens):
        at, out = 0, []
        for n in lens:
            out.append(seq[at:at + n])
            at += n
        return out

    def each(ins, outs, sems):
        return zip(riders, cut(ins, [len(r.inputs) for r in riders]), cut(outs, [len(r.out_shapes) for r in riders]),
                   cut(sems, [len(r.sems) for r in riders]))

    def start(ins, outs, sems):
        for r, i, o, s in each(ins, outs, sems):
            r.start(i, o, s)

    def finish(ins, outs, sems):
        for r, i, o, s in each(ins, outs, sems):
            r.finish(i, o, s)

    return _Rider([a for r in riders for a in r.inputs], [a for r in riders for a in r.out_shapes],
                  [a for r in riders for a in r.sems], start, finish)


def _gather_rider(shards):
    n = len(shards)

    def copies(ins, outs, sems):
        send_sems, recv_sems, local_sems = sems
        x, y, c = _my_place()
        chips = [(1 - x, y), (x, 1 - y), (1 - x, 1 - y)]
        mine = [pltpu.make_async_copy(ins[i], outs[i].at[2 * x + y], local_sems.at[i]) for i in range(n)]

        def copy(i, k, slot, to):
            return _remote(ins[i], outs[i].at[slot], send_sems, recv_sems, 3 * i + k, to)

        sends = [copy(i, k, 2 * x + y, (px, py, c)) for i in range(n) for k, (px, py) in enumerate(chips)]
        lands = [copy(i, k, 2 * px + py, (px, py, c)) for i in range(n) for k, (px, py) in enumerate(chips)]
        return mine, sends, lands

    def start(ins, outs, sems):
        mine, sends, _ = copies(ins, outs, sems)
        for cp in mine + sends:
            cp.start()

    def finish(ins, outs, sems):
        mine, sends, lands = copies(ins, outs, sems)
        for cp in lands:
            cp.wait_recv()
        for cp in sends:
            cp.wait_send()
        for cp in mine:
            cp.wait()

    return _Rider(list(shards), [_sds((4,) + s.shape, s.dtype) for s in shards],
                  [pltpu.SemaphoreType.DMA((3 * n,)), pltpu.SemaphoreType.DMA((3 * n,)), pltpu.SemaphoreType.DMA((n,))],
                  start, finish)


def _scatter_rider(gws, layer, gsmall=None):
    n = len(gws)
    per = n + (gsmall is not None)

    def plan(ins, outs, sems):
        send_sems, recv_sems, local_sems = sems
        x, y, c = _my_place()
        me = 4 * x + 2 * y + c

        def peer(m):
            return (x ^ ((m >> 2) & 1), y ^ ((m >> 1) & 1), c ^ (m & 1))

        def slot_of(m):
            px, py, pc = peer(m)
            return 4 * px + 2 * py + pc

        def piece(i, m, slot):
            px, py, pc = peer(m)
            return _remote(ins[i].at[2 * px + py], outs[i].at[slot], send_sems, recv_sems, (m - 1) * per + i,
                           (px, py, pc))

        def small(m, slot):
            return _remote(ins[n], outs[n].at[slot], send_sems, recv_sems, (m - 1) * per + n, peer(m))

        own = [pltpu.make_async_copy(ins[i].at[2 * x + y], outs[i].at[me], local_sems.at[i]) for i in range(n)]
        return c, me, slot_of, piece, small, own

    def start(ins, outs, sems):
        c, me, slot_of, piece, small, own = plan(ins, outs, sems)

        @pl.when(c == layer)
        def _():
            for cp in own:
                cp.start()
            for m in (2, 4, 6):
                for i in range(n):
                    piece(i, m, me).start()

        @pl.when(c != layer)
        def _():
            for m in (1, 3, 5, 7):
                for i in range(n):
                    piece(i, m, me).start()

        if gsmall is not None:
            pltpu.make_async_copy(ins[n], outs[n].at[me], sems[2].at[n]).start()
            for m in range(1, 8):
                small(m, me).start()

    def finish(ins, outs, sems):
        c, me, slot_of, piece, small, own = plan(ins, outs, sems)

        @pl.when(c == layer)
        def _():
            for m in range(1, 8):
                for i in range(n):
                    piece(i, m, slot_of(m)).wait_recv()
            for m in (2, 4, 6):
                for i in range(n):
                    piece(i, m, me).wait_send()
            for cp in own:
                cp.wait()

        @pl.when(c != layer)
        def _():
            for m in (1, 3, 5, 7):
                for i in range(n):
                    piece(i, m, me).wait_send()

        if gsmall is not None:
            for m in range(1, 8):
                small(m, slot_of(m)).wait_recv()
            for m in range(1, 8):
                small(m, me).wait_send()
            pltpu.make_async_copy(ins[n], outs[n].at[me], sems[2].at[n]).wait()

    inputs = list(gws) + ([gsmall] if gsmall is not None else [])
    shapes = [_sds((8,) + g.shape[1:], g.dtype) for g in gws]
    if gsmall is not None:
        shapes.append(_sds((8,) + gsmall.shape, gsmall.dtype))
    nsem = 7 * per
    return _Rider(inputs, shapes, [pltpu.SemaphoreType.DMA((nsem,)), pltpu.SemaphoreType.DMA((nsem,)),
                                   pltpu.SemaphoreType.DMA((per,))], start, finish)


def _sum_slots(slots, name, other=None):
    _, r, c = slots.shape
    tr = min(r, FLAT_TILE)
    assert r % tr == 0
    out_dtype = F32 if other is None else BF16

    def total(ref):
        acc = ref[0].astype(F32)
        for d in range(1, 8):
            acc = acc + ref[d].astype(F32)
        return acc

    def body(*refs):
        o_ref = refs[-1]
        if other is None:
            o_ref[...] = total(refs[0])
        else:
            core = lax.axis_index("c")

            @pl.when(core == 0)
            def _():
                o_ref[...] = total(refs[0]).astype(out_dtype)

            @pl.when(core == 1)
            def _():
                o_ref[...] = total(refs[1]).astype(out_dtype)

    spec = pl.BlockSpec((8, tr, c), lambda i: (0, i, 0))
    ins = [slots] if other is None else [slots, other]
    return _pcall(
        body, name=name, grid=(r // tr,), in_specs=[spec] * len(ins),
        out_specs=_tok(tr, c), out_shape=_sds((r, c), out_dtype), sem=("arbitrary",),
    )(*ins)


def _swap_layers(reds):
    n = len(reds)
    tiles = [min(r.shape[0], FLAT_TILE) for r in reds]

    def body(*refs):
        ins, outs, bufs = refs[:n], refs[n:2 * n], refs[2 * n:3 * n]
        load_sems, send_sems, recv_sems, local_sems = refs[3 * n:]
        x, y, c = _my_place()
        sibling = (x, y, 1 - c)
        mine = [pltpu.make_async_copy(ins[i], outs[i].at[c], local_sems.at[i]) for i in range(n)]
        for cp in mine:
            cp.start()
        in_flight = []
        for i in range(n):
            tr = tiles[i]
            busy = {}
            for t in range(reds[i].shape[0] // tr):
                slot, rows = t % 2, pl.ds(t * tr, tr)
                if slot in busy:
                    busy[slot].wait_send()
                load = pltpu.make_async_copy(ins[i].at[rows], bufs[i].at[slot], load_sems.at[slot])
                load.start()
                load.wait()
                busy[slot] = pltpu.make_async_remote_copy(
                    src_ref=bufs[i].at[slot], dst_ref=outs[i].at[c, rows], send_sem=send_sems.at[2 * i + slot],
                    recv_sem=recv_sems.at[i], device_id=sibling, device_id_type=MESH_ID)
                busy[slot].start()
            in_flight += list(busy.values())
        for i in range(n):
            _remote(ins[i], outs[i].at[1 - c], send_sems, recv_sems, i, sibling).wait_recv()
        for cp in in_flight:
            cp.wait_send()
        for cp in mine:
            cp.wait()

    hbm = pl.BlockSpec(memory_space=pl.ANY)
    return pl.pallas_call(
        body, name="swap_layers", out_shape=[_sds((2,) + r.shape, r.dtype) for r in reds],
        in_specs=[hbm] * n, out_specs=[hbm] * n,
        scratch_shapes=[pltpu.VMEM((2, tiles[i]) + reds[i].shape[1:], reds[i].dtype) for i in range(n)]
        + [pltpu.SemaphoreType.DMA((2,)), pltpu.SemaphoreType.DMA((2 * n,)), pltpu.SemaphoreType.DMA((n,)),
           pltpu.SemaphoreType.DMA((n,))],
        compiler_params=pltpu.CompilerParams(vmem_limit_bytes=VMEM_LIMIT),
    )(*reds)


W_IN_SHARD, W_IN_SHARD_PAD = 1192, 1280
W_UQ_SHARD, W_UQ_SHARD_PAD = 192, 256


def _swa_place(t):
    z = jnp.zeros_like(t)
    lo = jnp.concatenate([t, z], axis=1)
    hi = jnp.concatenate([z, t], axis=1)
    group = (jnp.arange(NH) // SWA_R).reshape((NH,) + (1,) * (t.ndim - 1))
    full = jnp.where(group == 0, lo, hi)
    return full.reshape((NH * LANE,) + t.shape[2:])


def _swa_unplace(t):
    t = t.reshape((NH, 2, 64) + t.shape[1:])
    return jnp.concatenate([t[:SWA_R, 0], t[SWA_R:, 1]], axis=0)


def _pad_w_o_mla(w):
    return jnp.pad(w.reshape(NH, 64, D), ((0, 0), (0, 64), (0, 0))).reshape(NH * LANE, D)


def _unpad_w_o_mla(g):
    return g.reshape(NH, LANE, D)[:, :64].reshape(NH * 64, D)


def _w_in_cols():
    src = np.full((C_END,), -1, np.int64)
    src[C_CQ:C_KPE] = np.arange(0, 384)
    src[C_KPE + 64:C_KPE + 96] = np.arange(384, 416)
    for h in range(NH):
        at = C_QS + h * LANE + 64 * (h // SWA_R)
        src[at:at + 64] = 416 + h * 64 + np.arange(64)
    src[C_KS:C_END] = np.arange(928, IN_COLS)
    return src


def _w_uq_cols():
    src = np.full((NH * LANE,), -1, np.int64)
    for h in range(NH):
        src[h * LANE:h * LANE + 96] = h * 96 + np.arange(96)
    return src


def _w_ukv_cols():
    src = np.full((2 * NH * LANE,), -1, np.int64)
    for h in range(NH):
        src[h * LANE:h * LANE + 64] = h * 128 + np.arange(64)
        src[NH * LANE + h * LANE:NH * LANE + h * LANE + 64] = h * 128 + 64 + np.arange(64)
    return src


def _selection(src_cols, width, width_pad):
    want = jnp.asarray(np.asarray(src_cols, np.int32))[None, None, :]
    k = jnp.arange(width_pad, dtype=jnp.int32)[None, :, None]
    have = jnp.where(k < width, jnp.arange(4, dtype=jnp.int32)[:, None, None] * width + k, -2)
    return (want == have).astype(BF16)


def _selections():
    return dict(w_in=_selection(_w_in_cols(), W_IN_SHARD, W_IN_SHARD_PAD),
                w_uq=_selection(_w_uq_cols(), W_UQ_SHARD, W_UQ_SHARD_PAD),
                w_ukv=_selection(_w_ukv_cols(), 256, 256))


def _pad_last(a, width):
    return jnp.pad(a, ((0, 0),) * (a.ndim - 1) + ((0, width - a.shape[-1]),))


def _wire_shards(W, l):
    out = {n: W[n][l].astype(BF16) for n in SHARDED}
    out["w_in"] = _pad_last(out["w_in"], W_IN_SHARD_PAD)
    out["w_uq"] = _pad_last(out["w_uq"], W_UQ_SHARD_PAD)
    return [out[n] for n in SHARDED]


def _join_shards(t, axis):
    _, L, r, c = t.shape
    if axis == 2:
        return t.transpose(1, 2, 0, 3).reshape(L, r, 4 * c)
    return t.transpose(1, 0, 2, 3).reshape(L, 4 * r, c)


EARLY = SHARDED[:3]
LATE = SHARDED[3:]


def _kernel_weights_early(gathered, sels):
    lay = {n: _select_fwd(a[:, None], sels[n], "lay_" + n)[0] for n, a in zip(EARLY, gathered)}
    return dict(win=lay["w_in"], wuq=lay["w_uq"], wukv=lay["w_ukv"])


def _kernel_weights_late(gathered):
    whole = {n: _join_shards(a[:, None], SHARD_AXIS[n])[0] for n, a in zip(LATE, gathered)}
    return dict(wmem=whole["w_mem_kv"], wa=_pad_w_o_mla(whole["w_o_mla"]),
                wb=_swa_place(whole["w_o_swa"].reshape(NH, 64, D)), wc=whole["w_o_mem"], wout=whole["w_out"],
                wup=whole["w_up"], wdown=whole["w_down"])


def _kernel_weights(gathered, sels):
    return {**_kernel_weights_early(gathered[:3], sels), **_kernel_weights_late(gathered[3:])}


def _cols_to_shards(g):
    r, c4 = g.shape
    return g.reshape(r, 4, c4 // 4).transpose(1, 0, 2).astype(BF16)


def _rope_tables(S):
    pos = jnp.arange(S, dtype=F32)
    inv = 1.0 / (ROPE_THETA ** (jnp.arange(0, 32, 2, dtype=F32) / 32))
    ang = pos[:, None] * inv[None, :]
    cos, sin = jnp.cos(ang), jnp.sin(ang)
    one, zero = jnp.ones((S, 64), F32), jnp.zeros((S, 16), F32)
    rc = jnp.concatenate([one, cos, cos, jnp.ones((S, 32), F32)], axis=1)
    rs1 = jnp.concatenate([jnp.zeros((S, 64), F32), zero, sin, jnp.zeros((S, 32), F32)], axis=1)
    rs2 = jnp.concatenate([jnp.zeros((S, 64), F32), -sin, zero, jnp.zeros((S, 32), F32)], axis=1)
    return rc, rs1, rs2


def _bucket_map():
    qi = jnp.arange(WIN)[:, None]
    kj = jnp.arange(2 * WIN)[None, :]
    dist = qi + WIN - kj
    n = jnp.maximum(dist, 0)
    max_exact = REL_BUCKETS // 2
    nf = jnp.maximum(n, 1).astype(F32)
    large = max_exact + (jnp.log(nf / max_exact) / math.log(128 / max_exact)
                         * (REL_BUCKETS - max_exact)).astype(jnp.int32)
    large = jnp.minimum(large, REL_BUCKETS - 1)
    bucket = jnp.where(n < max_exact, n, large)
    return jnp.where((dist >= 0) & (dist < WIN), bucket, -1).astype(jnp.int32)


TS = 256
TQ = 1024
TQ_FWD = 1024


def _local_step(x, mem, tgt, kw0, sp, sels, kw1=None, wire=None):
    S = x.shape[0]
    ts = min(TS, S)
    tq = min(TQ, S)
    rc, rs1, rs2 = _rope_tables(S)
    bmap = _bucket_map()
    bias = _bias_build(sp["rel_bias"], bmap)
    row = lambda v: v.reshape(1, -1)

    saved = []
    kw = [dict(kw0), kw1]
    for l in range(DEPTH):
        w = kw[l]
        an, qn, kvn = row(sp["attn_norm"][l]), row(sp["mla_q_norm"][l]), row(sp["mla_kv_norm"][l])
        bg, mnorm, mlpn = row(sp["b_gate"][l]), row(sp["mem_norm"][l]), row(sp["mlp_norm"][l])
        sinks = jnp.broadcast_to(sp["attn_sinks"][l][:, None], (NH, LANE))
        q, k, v, qs, ks, vs, qm, g = _pre_fwd(x, an, w["win"], bg, qn, kvn, w["wuq"], w["wukv"], rc, rs1, rs2, ts)
        carry = _join_riders(_gather_rider(wire[0]), _gather_rider(wire[1])) if (l == 0 and wire) else None
        oa, lse, got = _mla_fwd(q, k, v, min(TQ_FWD, S), carry)
        if carry:
            w.update(_kernel_weights_late(got[:len(LATE)]))
            kw[1] = _kernel_weights(got[len(LATE):], sels)
        ob = _swa_fwd(qs, ks, vs, bias, sinks)
        kvm = _memkv_fwd(mem, mnorm, w["wmem"])
        oc = _mem_fwd(qm, kvm, ts)
        x1, yb = _merge_fwd(x, g, oa, ob, oc, w["wa"], w["wb"], w["wc"], w["wout"], ts)
        x2 = _mlp_fwd(x1, mlpn, w["wup"], w["wdown"], ts)
        saved.append(dict(w=w, x=x, x1=x1, q=q, k=k, v=v, qs=qs, ks=ks, vs=vs, qm=qm, g=g, oa=oa, lse=lse, ob=ob,
                          oc=oc, kvm=kvm, yb=yb, an=an, qn=qn, kvn=kvn, mnorm=mnorm, mlpn=mlpn, sinks=sinks))
        x = x2

    sq, dx, dfn = _loss_kernel(x, row(sp["final_norm"]), tgt, ts)

    big = {n: [None] * DEPTH for n in SHARDED}
    small = {n: [None] * DEPTH for n in SMALL if n not in ("rel_bias", "final_norm")}
    dbias_total = None
    slots1 = None
    for l in reversed(range(DEPTH)):
        sv = saved[l]
        w = sv["w"]
        dx1, hb2, dub, ab, dxb, dmlpn = _mlp_bwd(dx, sv["x1"], sv["mlpn"], w["wup"], w["wdown"], ts)
        big["w_up"][l] = _matmul_tn(hb2, dub, "dw_up", shard_axis=1)
        big["w_down"][l] = _matmul_tn(ab, dxb, "dw_down", shard_axis=0)
        small["mlp_norm"][l] = dmlpn[0]

        dgp, dyo, doa, dob, doc, dla, dx1b, dbg = _merge_bwd(dx1, sv["g"], sv["oa"], sv["ob"], sv["oc"], w["wa"],
                                                             w["wb"], w["wc"], w["wout"], ts)
        big["w_out"][l] = _matmul_tn(sv["yb"], dx1b, "dw_out", shard_axis=0)
        big["w_o_mla"][l] = _cols_to_shards(_unpad_w_o_mla(_matmul_tn(sv["oa"], dyo[:, 0:D], "dw_o_mla")))
        big["w_o_swa"][l] = _cols_to_shards(
            _swa_unplace(_matmul_tn(sv["ob"], dyo[:, D:2 * D], "dw_o_swa")).reshape(NH * 64, D))
        big["w_o_mem"][l] = _matmul_tn(sv["oc"], dyo[:, 2 * D:3 * D], "dw_o_mem", shard_axis=1)
        small["b_gate"][l] = dbg[0]

        dqm, dkvm = _mem_bwd(sv["qm"], sv["kvm"], doc, ts)
        dwmem, dmnorm = _memkv_bwd(mem, sv["mnorm"], w["wmem"], dkvm)
        big["w_mem_kv"][l] = dwmem.reshape(4, D // 4, D).astype(BF16)
        small["mem_norm"][l] = dmnorm[0]

        dqs, dks, dvs, dbias, dsink = _swa_bwd(sv["qs"], sv["ks"], sv["vs"], dob, bias, sv["sinks"])
        dbias_total = dbias if dbias_total is None else dbias_total + dbias
        small["attn_sinks"][l] = dsink[:, 0]

        carry = _join_riders(_scatter_rider([big[n][1] for n in SHARDED], 1),
                             _scatter_rider([big[n][0] for n in LATE], 0)) if (l == 0 and wire) else None
        dq, dk, dv, slots = _mla_bwd(sv["q"], sv["k"], sv["v"], doa, sv["lse"], dla, tq, carry)
        if carry:
            slots1 = (slots[:len(SHARDED)], slots[len(SHARDED):])

        dx, dproj, hb, cqn, ckvn, dqpre, dkv, dan, dqn, dkvn = _pre_bwd(
            sv["x"], dx1, dq, dk, dv, dqs, dks, dvs, dqm, dgp, sv["an"], sv["qn"], sv["kvn"], w["win"], w["wuq"],
            w["wukv"], rc, rs1, rs2, ts)
        for n, (a, b) in (("w_in", (hb, dproj)), ("w_uq", (cqn, dqpre)), ("w_ukv", (ckvn, dkv))):
            big[n][l] = _select_bwd(_matmul_tn(a, b, "d" + n)[None], sels[n], "shard_d" + n)[0]
        small["attn_norm"][l] = dan[0]
        small["mla_q_norm"][l] = dqn[0]
        small["mla_kv_norm"][l] = dkvn[0]

    gw = [[big[n][l] for n in SHARDED] for l in range(DEPTH)]
    gs = {n: jnp.stack(v) for n, v in small.items()}
    gs["rel_bias"] = _bias_reduce(dbias_total, bmap)
    gs["final_norm"] = dfn[0]
    return sq, dx, gw, slots1, gs


def _flatten(parts, rows):
    flat = jnp.concatenate([p.reshape(-1) for p in parts])
    return jnp.pad(flat, (0, rows * FLAT_W - flat.shape[0])).reshape(rows, FLAT_W)


def _unflatten(buf, shapes):
    flat = buf.reshape(-1)
    out, at = [], 0
    for s in shapes:
        n = int(np.prod(s))
        out.append(flat[at:at + n].reshape(s))
        at += n
    return out


def kernel(x, mem, rel_bias, attn_norm, mem_norm, w_in, b_gate, mla_q_norm, w_uq, mla_kv_norm, w_ukv, attn_sinks, w_mem_kv, w_o_mla, w_o_swa, w_o_mem, w_out, mlp_norm, w_up, w_down, final_norm, loss_target, m_rel_bias, m_attn_norm, m_mem_norm, m_w_in, m_b_gate, m_mla_q_norm, m_w_uq, m_mla_kv_norm, m_w_ukv, m_attn_sinks, m_w_mem_kv, m_w_o_mla, m_w_o_swa, m_w_o_mem, m_w_out, m_mlp_norm, m_w_up, m_w_down, m_final_norm, v_rel_bias, v_attn_norm, v_mem_norm, v_w_in, v_b_gate, v_mla_q_norm, v_w_uq, v_mla_kv_norm, v_w_ukv, v_attn_sinks, v_w_mem_kv, v_w_o_mla, v_w_o_swa, v_w_o_mem, v_w_out, v_mlp_norm, v_w_up, v_w_down, v_final_norm):
    args = dict(locals())
    W = {n: args[n] for n in WEIGHTS}
    M = {n: args["m_" + n] for n in WEIGHTS}
    V = {n: args["v_" + n] for n in WEIGHTS}
    small_shapes = [W[n].shape for n in SMALL]
    sels = _selections()

    wire0, wire1 = _wire_shards(W, 0), _wire_shards(W, 1)
    kw0 = _kernel_weights_early(_run_rider(_gather_rider(wire0[:len(EARLY)]), "gather_weights"), sels)
    sp = {n: W[n] for n in SMALL}

    sq, dx, gw, carried, gs = _local_step(x[0], mem[0], loss_target[0], kw0, sp, sels,
                                          wire=(wire0[len(EARLY):], wire1))

    slots1, slots0_late = carried
    gsmall = _flatten([gs[n] for n in SMALL], SMALL_ROWS)
    *slots0_early, small_slots = _run_rider(_scatter_rider(gw[0][:len(EARLY)], 0, gsmall), "scatter_grads")
    slots0 = list(slots0_early) + list(slots0_late)
    reds = [_sum_slots(s0, "sum_" + n, s1) for n, s0, s1 in zip(SHARDED, slots0, slots1)]
    red_small = _sum_slots(small_slots, "sum_small")
    G = {n: g.astype(F32) for n, g in zip(SHARDED, _swap_layers(reds))}
    G["w_in"] = G["w_in"][..., :W_IN_SHARD]
    G["w_uq"] = G["w_uq"][..., :W_UQ_SHARD]

    DW, NM, NV = {}, {}, {}
    for n in SHARDED:
        shape = W[n].shape
        two_d = lambda a: a.reshape(-1, shape[-1])
        d, nm, nv = _adamw(two_d(W[n]), two_d(G[n]), two_d(M[n]), two_d(V[n]), "adamw_" + n)
        DW[n], NM[n], NV[n] = d.reshape(shape), nm.reshape(shape), nv.reshape(shape)
    d_s, m_s, v_s = _adamw(_flatten([W[n] for n in SMALL], SMALL_ROWS), red_small,
                           _flatten([M[n] for n in SMALL], SMALL_ROWS), _flatten([V[n] for n in SMALL], SMALL_ROWS),
                           "adamw_small")
    for out, buf in ((G, red_small), (DW, d_s), (NM, m_s), (NV, v_s)):
        out.update(zip(SMALL, _unflatten(buf, small_shapes)))
    loss = lax.psum(0.5 * sq[0, 0] / D, ("x", "y", "c"))
    return (loss, dx[None], *[G[n] for n in WEIGHTS], *[DW[n] for n in WEIGHTS], *[NM[n] for n in WEIGHTS],
            *[NV[n] for n in WEIGHTS])
```

```python
import functools
import math
from typing import Callable, NamedTuple

import numpy as np
import jax
import jax.numpy as jnp
from jax import lax
from jax.experimental import pallas as pl
from jax.experimental.pallas import tpu as pltpu

F32 = jnp.float32
BF16 = jnp.bfloat16

D = 1024
DFF = 4096
DEPTH = 2
EPS = 1e-6
LANE = 128
NH = 8
SWA_R = 4
MEM_H = 4
MEM_LEN = 256
WIN = 128
NEG = -1e30
MLA_SCALE = 96 ** -0.5
SWA_SCALE = 64 ** -0.5
MEM_SCALE = 128 ** -0.5
REL_BUCKETS = 32
ROPE_THETA = 10000.0

C_CQ, C_CKV, C_KPE, C_QS, C_KS, C_VS, C_QM, C_G, C_END = 0, 256, 384, 512, 1536, 1664, 1792, 2304, 5376
IN_COLS = 4768

ADAM_LR = 0.001
ADAM_B1 = 0.9
ADAM_B2 = 0.999
ADAM_EPS = 1e-08
ADAM_WD = 0.01
ADAM_STEP = 10

VMEM_LIMIT = 56 * 1024 * 1024

SHARDED = ("w_in", "w_uq", "w_ukv", "w_mem_kv", "w_o_mla", "w_o_swa", "w_o_mem", "w_out", "w_up", "w_down")
SHARD_AXIS = {"w_in": 2, "w_uq": 2, "w_ukv": 2, "w_mem_kv": 1, "w_o_mla": 2, "w_o_swa": 2, "w_o_mem": 2,
              "w_out": 1, "w_up": 2, "w_down": 1}
SMALL = ("rel_bias", "attn_norm", "mem_norm", "b_gate", "mla_q_norm", "mla_kv_norm", "attn_sinks", "mlp_norm",
         "final_norm")
WEIGHTS = ("rel_bias", "attn_norm", "mem_norm", "w_in", "b_gate", "mla_q_norm", "w_uq", "mla_kv_norm", "w_ukv",
           "attn_sinks", "w_mem_kv", "w_o_mla", "w_o_swa", "w_o_mem", "w_out", "mlp_norm", "w_up", "w_down",
           "final_norm")
FLAT_W = 1024
FLAT_TILE = 256
SMALL_ROWS = 16
MESH_ID = pl.DeviceIdType.MESH


def _pcall(body, *, name, grid, in_specs, out_specs, out_shape, scratch=(), prefetch=0, sem=None):
    params = pltpu.CompilerParams(dimension_semantics=sem, vmem_limit_bytes=VMEM_LIMIT)
    if prefetch:
        spec = pltpu.PrefetchScalarGridSpec(num_scalar_prefetch=prefetch, grid=grid, in_specs=in_specs,
                                            out_specs=out_specs, scratch_shapes=scratch)
        return pl.pallas_call(body, name=name, grid_spec=spec, out_shape=out_shape, compiler_params=params)
    return pl.pallas_call(body, name=name, grid=grid, in_specs=in_specs, out_specs=out_specs, out_shape=out_shape,
                          scratch_shapes=scratch, compiler_params=params)


def _tok(ts, w):
    return pl.BlockSpec((ts, w), lambda i: (i, 0))


def _full(*shape):
    return pl.BlockSpec(shape, lambda *_: (0,) * len(shape))


def _sds(shape, dtype):
    return jax.ShapeDtypeStruct(shape, dtype)


def _dot(a, b):
    return jnp.dot(a, b, preferred_element_type=F32)


def _dot_nt(a, b):
    return lax.dot_general(a, b, (((1,), (1,)), ((), ())), preferred_element_type=F32)


def _dot_tn(a, b):
    return lax.dot_general(a, b, (((0,), (0,)), ((), ())), preferred_element_type=F32)


def _rms(x):
    r = lax.rsqrt(jnp.mean(x * x, axis=-1, keepdims=True) + EPS)
    return x * r, r


def _rms_bwd(dyg, n, r):
    return r * (dyg - n * jnp.mean(n * dyg, axis=-1, keepdims=True))


def _rope(t, c, s1, s2):
    return t * c + pltpu.roll(t, 16, 1) * s1 + pltpu.roll(t, LANE - 16, 1) * s2


def _rope_bwd(dy, c, s1, s2):
    return dy * c + pltpu.roll(dy * s1, LANE - 16, 1) + pltpu.roll(dy * s2, 16, 1)


def _hs(h):
    return slice(h * LANE, (h + 1) * LANE)


def _colsum(t):
    return jnp.sum(t, axis=0, keepdims=True)


def _pre_fwd(x, an, win, bg, qn, kvn, wuq, wukv, rc, rs1, rs2, ts):
    S = x.shape[0]

    def body(x_ref, an_ref, win_ref, bg_ref, qn_ref, kvn_ref, wuq_ref, wukv_ref, rc_ref, rs1_ref, rs2_ref,
             q_ref, k_ref, v_ref, qs_ref, ks_ref, vs_ref, qm_ref, g_ref):
        n, _ = _rms(x_ref[...])
        hb = (n * an_ref[...]).astype(BF16)
        pa = _dot(hb, win_ref[:, C_CQ:C_QS])
        ncq, _ = _rms(pa[:, 0:256])
        cqn = (ncq * qn_ref[...]).astype(BF16)
        nkv, _ = _rms(pa[:, 256:384])
        ckvn = (nkv * kvn_ref[...]).astype(BF16)
        c, s1, s2 = rc_ref[...], rs1_ref[...], rs2_ref[...]
        kper = _rope(pa[:, 384:512], c, s1, s2)
        qp = _dot(cqn, wuq_ref[...])
        kv = _dot(ckvn, wukv_ref[...])
        for h in range(NH):
            q_ref[:, _hs(h)] = (_rope(qp[:, _hs(h)], c, s1, s2) * MLA_QSCALE).astype(BF16)
            k_ref[:, _hs(h)] = (kv[:, _hs(h)] + kper).astype(BF16)
        v_ref[...] = kv[:, NH * LANE:].astype(BF16)
        pb = _dot(hb, win_ref[:, C_QS:C_G])
        qs_ref[...] = pb[:, 0:1024].astype(BF16)
        ks_ref[...] = pb[:, 1024:1152].astype(BF16)
        vs_ref[...] = pb[:, 1152:1280].astype(BF16)
        qm_ref[...] = pb[:, 1280:1792].astype(BF16)
        g_ref[...] = jax.nn.sigmoid(_dot(hb, win_ref[:, C_G:C_END]) + bg_ref[...])

    return _pcall(
        body, name="pre_fwd", grid=(S // ts,),
        in_specs=[_tok(ts, D), _full(1, D), _full(D, C_END), _full(1, 3 * D), _full(1, 256), _full(1, 128),
                  _full(256, NH * LANE), _full(128, 2 * NH * LANE), _tok(ts, LANE), _tok(ts, LANE), _tok(ts, LANE)],
        out_specs=[_tok(ts, 1024), _tok(ts, 1024), _tok(ts, 1024), _tok(ts, 1024), _tok(ts, 128), _tok(ts, 128),
                   _tok(ts, 512), _tok(ts, 3 * D)],
        out_shape=[_sds((S, 1024), BF16), _sds((S, 1024), BF16), _sds((S, 1024), BF16), _sds((S, 1024), BF16),
                   _sds((S, 128), BF16), _sds((S, 128), BF16), _sds((S, 512), BF16), _sds((S, 3 * D), F32)],
        sem=("arbitrary",),
    )(x, an, win, bg, qn, kvn, wuq, wukv, rc, rs1, rs2)


def _merge_fwd(x, g, oa, ob, oc, wa, wb, wc, wout, ts):
    S = x.shape[0]

    def body(x_ref, g_ref, oa_ref, ob_ref, oc_ref, wa_ref, wb_ref, wc_ref, wout_ref, x1_ref, yb_ref):
        y = g_ref[:, 0:D] * _dot(oa_ref[...], wa_ref[...])
        y = y + g_ref[:, D:2 * D] * _dot(ob_ref[...], wb_ref[...])
        y = y + g_ref[:, 2 * D:3 * D] * _dot(oc_ref[...], wc_ref[...])
        yb = y.astype(BF16)
        yb_ref[...] = yb
        x1_ref[...] = x_ref[...] + _dot(yb, wout_ref[...])

    return _pcall(
        body, name="merge_fwd", grid=(S // ts,),
        in_specs=[_tok(ts, D), _tok(ts, 3 * D), _tok(ts, 1024), _tok(ts, 1024), _tok(ts, 512),
                  _full(1024, D), _full(1024, D), _full(512, D), _full(D, D)],
        out_specs=[_tok(ts, D), _tok(ts, D)],
        out_shape=[_sds((S, D), F32), _sds((S, D), BF16)],
        sem=("arbitrary",),
    )(x, g, oa, ob, oc, wa, wb, wc, wout)


def _mlp_fwd(x1, mn, wup, wdown, ts):
    S = x1.shape[0]

    def body(x_ref, mn_ref, wup_ref, wdown_ref, x2_ref):
        xv = x_ref[...]
        n, _ = _rms(xv)
        u = _dot((n * mn_ref[...]).astype(BF16), wup_ref[...])
        a = jnp.square(jnp.maximum(u, 0.0))
        x2_ref[...] = xv + _dot(a.astype(BF16), wdown_ref[...])

    return _pcall(
        body, name="mlp_fwd", grid=(S // ts,),
        in_specs=[_tok(ts, D), _full(1, D), _full(D, DFF), _full(DFF, D)],
        out_specs=_tok(ts, D), out_shape=_sds((S, D), F32), sem=("arbitrary",),
    )(x1, mn, wup, wdown)


def _loss_kernel(x, fn, tgt, ts):
    S = x.shape[0]

    def body(x_ref, fn_ref, t_ref, loss_ref, dx_ref, dfn_ref):
        @pl.when(pl.program_id(0) == 0)
        def _():
            loss_ref[...] = jnp.zeros_like(loss_ref)
            dfn_ref[...] = jnp.zeros_like(dfn_ref)

        n, r = _rms(x_ref[...])
        err = n * fn_ref[...] - t_ref[...]
        loss_ref[...] += jnp.sum(err * err)
        dy = err * (1.0 / D)
        dfn_ref[...] += _colsum(dy * n)
        dx_ref[...] = _rms_bwd(dy * fn_ref[...], n, r)

    return _pcall(
        body, name="loss_head", grid=(S // ts,),
        in_specs=[_tok(ts, D), _full(1, D), _tok(ts, D)],
        out_specs=[_full(1, LANE), _tok(ts, D), _full(1, D)],
        out_shape=[_sds((1, LANE), F32), _sds((S, D), F32), _sds((1, D), F32)],
        sem=("arbitrary",),
    )(x, fn, tgt)


def _mlp_bwd(dx2, x1, mn, wup, wdown, ts):
    S = x1.shape[0]

    def body(dx_ref, x_ref, mn_ref, wup_ref, wdown_ref, dx1_ref, hb_ref, dub_ref, ab_ref, dxb_ref, dmn_ref):
        @pl.when(pl.program_id(0) == 0)
        def _():
            dmn_ref[...] = jnp.zeros_like(dmn_ref)

        dx = dx_ref[...]
        n, r = _rms(x_ref[...])
        g = mn_ref[...]
        hb = (n * g).astype(BF16)
        hb_ref[...] = hb
        rl = jnp.maximum(_dot(hb, wup_ref[...]), 0.0)
        ab_ref[...] = jnp.square(rl).astype(BF16)
        dxb = dx.astype(BF16)
        dxb_ref[...] = dxb
        dub = (_dot_nt(dxb, wdown_ref[...]) * (2.0 * rl)).astype(BF16)
        dub_ref[...] = dub
        dh = _dot_nt(dub, wup_ref[...])
        dmn_ref[...] += _colsum(dh * n)
        dx1_ref[...] = dx + _rms_bwd(dh * g, n, r)

    return _pcall(
        body, name="mlp_bwd", grid=(S // ts,),
        in_specs=[_tok(ts, D), _tok(ts, D), _full(1, D), _full(D, DFF), _full(DFF, D)],
        out_specs=[_tok(ts, D), _tok(ts, D), _tok(ts, DFF), _tok(ts, DFF), _tok(ts, D), _full(1, D)],
        out_shape=[_sds((S, D), F32), _sds((S, D), BF16), _sds((S, DFF), BF16), _sds((S, DFF), BF16),
                   _sds((S, D), BF16), _sds((1, D), F32)],
        sem=("arbitrary",),
    )(dx2, x1, mn, wup, wdown)


def _merge_bwd(dx1, g, oa, ob, oc, wa, wb, wc, wout, ts):
    S = dx1.shape[0]

    def body(dx_ref, g_ref, oa_ref, ob_ref, oc_ref, wa_ref, wb_ref, wc_ref, wout_ref,
             dgp_ref, dyo_ref, doa_ref, dob_ref, doc_ref, dla_ref, dxb_ref, dbg_ref):
        @pl.when(pl.program_id(0) == 0)
        def _():
            dbg_ref[...] = jnp.zeros_like(dbg_ref)

        dxb = dx_ref[...].astype(BF16)
        dxb_ref[...] = dxb
        dy = _dot_nt(dxb, wout_ref[...])
        branches = ((oa_ref, wa_ref, doa_ref), (ob_ref, wb_ref, dob_ref), (oc_ref, wc_ref, doc_ref))
        for b, (o_ref, w_ref, do_ref) in enumerate(branches):
            cols = slice(b * D, (b + 1) * D)
            gb = g_ref[:, cols]
            o = o_ref[...]
            dgpre = dy * _dot(o, w_ref[...]) * gb * (1.0 - gb)
            dgp_ref[:, cols] = dgpre.astype(BF16)
            dbg_ref[:, cols] += _colsum(dgpre)
            dyo = (dy * gb).astype(BF16)
            dyo_ref[:, cols] = dyo
            do = _dot_nt(dyo, w_ref[...])
            do_ref[...] = do.astype(BF16)
            if b == 0:
                lane = lax.broadcasted_iota(jnp.int32, (ts, LANE), 1)
                dls = jnp.zeros((ts, LANE), F32)
                for h in range(NH):
                    dl = jnp.sum(do[:, _hs(h)] * o[:, _hs(h)].astype(F32), axis=1, keepdims=True)
                    dls = jnp.where(lane == h, dl, dls)
                dla_ref[:, 0, :] = jnp.transpose(dls)[0:NH, :]

    return _pcall(
        body, name="merge_bwd", grid=(S // ts,),
        in_specs=[_tok(ts, D), _tok(ts, 3 * D), _tok(ts, 1024), _tok(ts, 1024), _tok(ts, 512),
                  _full(1024, D), _full(1024, D), _full(512, D), _full(D, D)],
        out_specs=[_tok(ts, 3 * D), _tok(ts, 3 * D), _tok(ts, 1024), _tok(ts, 1024), _tok(ts, 512),
                   pl.BlockSpec((NH, 1, ts), lambda i: (0, 0, i)), _tok(ts, D), _full(1, 3 * D)],
        out_shape=[_sds((S, 3 * D), BF16), _sds((S, 3 * D), BF16), _sds((S, 1024), BF16), _sds((S, 1024), BF16),
                   _sds((S, 512), BF16), _sds((NH, 1, S), F32), _sds((S, D), BF16), _sds((1, 3 * D), F32)],
        sem=("arbitrary",),
    )(dx1, g, oa, ob, oc, wa, wb, wc, wout)


def _pre_bwd(x, dx1, dq, dk, dv, dqs, dks, dvs, dqm, dgp, an, qn, kvn, win, wuq, wukv, rc, rs1, rs2, ts, ride=None):
    S = x.shape[0]

    def body(x_ref, dx1_ref, dq_ref, dk_ref, dv_ref, dqs_ref, dks_ref, dvs_ref, dqm_ref, dgp_ref,
             an_ref, qn_ref, kvn_ref, win_ref, wuq_ref, wukv_ref, rc_ref, rs1_ref, rs2_ref, *rest):
        own, _, riding = _ride_refs(ride, rest, 10, 0)
        dx_ref, dproj_ref, hb_ref, cqn_ref, ckvn_ref, dqpre_ref, dkv_ref, dan_ref, dqn_ref, dkvn_ref = own

        @pl.when(pl.program_id(0) == 0)
        def _():
            dan_ref[...] = jnp.zeros_like(dan_ref)
            dqn_ref[...] = jnp.zeros_like(dqn_ref)
            dkvn_ref[...] = jnp.zeros_like(dkvn_ref)
            if ride:
                ride.start(*riding)

        n, r = _rms(x_ref[...])
        hb = (n * an_ref[...]).astype(BF16)
        hb_ref[...] = hb
        pa = _dot(hb, win_ref[:, C_CQ:C_KPE])
        ncq, rq = _rms(pa[:, 0:256])
        cqn_ref[...] = (ncq * qn_ref[...]).astype(BF16)
        nkv, rkv = _rms(pa[:, 256:384])
        ckvn_ref[...] = (nkv * kvn_ref[...]).astype(BF16)
        c, s1, s2 = rc_ref[...], rs1_ref[...], rs2_ref[...]

        dkper = jnp.zeros((ts, LANE), F32)
        for h in range(NH):
            dqpre_ref[:, _hs(h)] = _rope_bwd(dq_ref[:, _hs(h)], c, s1, s2).astype(BF16)
            dkh = dk_ref[:, _hs(h)]
            dkper = dkper + dkh
            dkv_ref[:, _hs(h)] = dkh.astype(BF16)
        dkv_ref[:, NH * LANE:] = dv_ref[...].astype(BF16)

        dcqn = _dot_nt(dqpre_ref[...], wuq_ref[...])
        dqn_ref[...] += _colsum(dcqn * ncq)
        dproj_ref[:, C_CQ:C_CKV] = _rms_bwd(dcqn * qn_ref[...], ncq, rq).astype(BF16)
        dckvn = _dot_nt(dkv_ref[...], wukv_ref[...])
        dkvn_ref[...] += _colsum(dckvn * nkv)
        dproj_ref[:, C_CKV:C_KPE] = _rms_bwd(dckvn * kvn_ref[...], nkv, rkv).astype(BF16)
        lane = lax.broadcasted_iota(jnp.int32, (ts, LANE), 1)
        dkpe = jnp.where((lane >= 64) & (lane < 96), _rope_bwd(dkper, c, s1, s2), 0.0)
        dproj_ref[:, C_KPE:C_QS] = dkpe.astype(BF16)
        dproj_ref[:, C_QS:C_KS] = dqs_ref[...]
        dproj_ref[:, C_KS:C_VS] = dks_ref[...].astype(BF16)
        dproj_ref[:, C_VS:C_QM] = dvs_ref[...].astype(BF16)
        dproj_ref[:, C_QM:C_G] = dqm_ref[...]
        dproj_ref[:, C_G:C_END] = dgp_ref[...]

        dh = _dot_nt(dproj_ref[...], win_ref[...])
        dan_ref[...] += _colsum(dh * n)
        dx_ref[...] = dx1_ref[...] + _rms_bwd(dh * an_ref[...], n, r)

        if ride:
            @pl.when(pl.program_id(0) == S // ts - 1)
            def _():
                ride.finish(*riding)

    r_in, r_out, r_shapes, r_sems, r_args = _ride_specs(ride)
    results = _pcall(
        body, name="pre_bwd", grid=(S // ts,),
        in_specs=[_tok(ts, D), _tok(ts, D), _tok(ts, 1024), _tok(ts, 1024), _tok(ts, 1024), _tok(ts, 1024),
                  _tok(ts, 128), _tok(ts, 128), _tok(ts, 512), _tok(ts, 3 * D),
                  _full(1, D), _full(1, 256), _full(1, 128), _full(D, C_END), _full(256, NH * LANE),
                  _full(128, 2 * NH * LANE), _tok(ts, LANE), _tok(ts, LANE), _tok(ts, LANE)] + r_in,
        out_specs=[_tok(ts, D), _tok(ts, C_END), _tok(ts, D), _tok(ts, 256), _tok(ts, 128), _tok(ts, 1024),
                   _tok(ts, 2048), _full(1, D), _full(1, 256), _full(1, 128)] + r_out,
        out_shape=[_sds((S, D), F32), _sds((S, C_END), BF16), _sds((S, D), BF16), _sds((S, 256), BF16),
                   _sds((S, 128), BF16), _sds((S, 1024), BF16), _sds((S, 2048), BF16), _sds((1, D), F32),
                   _sds((1, 256), F32), _sds((1, 128), F32)] + r_shapes,
        scratch=r_sems, sem=("arbitrary",),
    )(x, dx1, dq, dk, dv, dqs, dks, dvs, dqm, dgp, an, qn, kvn, win, wuq, wukv, rc, rs1, rs2, *r_args)
    return (*results[:10], list(results[10:]))


def _pick_tile(n, cap):
    best = LANE
    for t in range(LANE, min(n, cap) + 1, LANE):
        if n % t == 0:
            best = t
    return best


def _matmul_tn(a, b, name, shard_axis=None):
    S, M = a.shape
    N = b.shape[1]
    tm = _pick_tile(M // 4 if shard_axis == 0 else M, 1024)
    tn = _pick_tile(N // 4 if shard_axis == 1 else N, 2048)
    ts = min(S, 1024)
    nk = S // ts

    def body(a_ref, b_ref, o_ref, *acc):
        acc_ref = acc[0] if acc else o_ref

        @pl.when(pl.program_id(2) == 0)
        def _():
            acc_ref[...] = jnp.zeros_like(acc_ref)

        acc_ref[...] += _dot_tn(a_ref[...], b_ref[...])
        if acc:
            @pl.when(pl.program_id(2) == nk - 1)
            def _():
                o_ref[0] = acc_ref[...].astype(o_ref.dtype)

    if shard_axis is None:
        out_spec = pl.BlockSpec((tm, tn), lambda i, j, k: (i, j))
        out_shape, scratch = _sds((M, N), F32), ()
    elif shard_axis == 0:
        per = (M // 4) // tm
        out_spec = pl.BlockSpec((1, tm, tn), lambda i, j, k: (i // per, i % per, j))
        out_shape, scratch = _sds((4, M // 4, N), BF16), (pltpu.VMEM((tm, tn), F32),)
    else:
        per = (N // 4) // tn
        out_spec = pl.BlockSpec((1, tm, tn), lambda i, j, k: (j // per, i, j % per))
        out_shape, scratch = _sds((4, M, N // 4), BF16), (pltpu.VMEM((tm, tn), F32),)
    return _pcall(
        body, name=name, grid=(M // tm, N // tn, nk),
        in_specs=[pl.BlockSpec((ts, tm), lambda i, j, k: (k, i)), pl.BlockSpec((ts, tn), lambda i, j, k: (k, j))],
        out_specs=out_spec, out_shape=out_shape, scratch=scratch, sem=("parallel", "parallel", "arbitrary"),
    )(a, b)


def _select_fwd(a, sel, name):
    _, L, M, K = a.shape
    N = sel.shape[2]
    tn = _pick_tile(N, 1792)

    def body(a_ref, s_ref, o_ref, acc_ref):
        s = pl.program_id(2)

        @pl.when(s == 0)
        def _():
            acc_ref[...] = jnp.zeros_like(acc_ref)

        acc_ref[...] += _dot(a_ref[0, 0], s_ref[0])

        @pl.when(s == 3)
        def _():
            o_ref[0] = acc_ref[...].astype(BF16)

    return _pcall(
        body, name=name, grid=(L, N // tn, 4),
        in_specs=[pl.BlockSpec((1, 1, M, K), lambda l, j, s: (s, l, 0, 0)),
                  pl.BlockSpec((1, K, tn), lambda l, j, s: (s, 0, j))],
        out_specs=pl.BlockSpec((1, M, tn), lambda l, j, s: (l, 0, j)),
        out_shape=_sds((L, M, N), BF16), scratch=(pltpu.VMEM((M, tn), F32),),
        sem=("parallel", "parallel", "arbitrary"),
    )(a, sel)


def _select_bwd(dw, sel, name):
    L, M, N = dw.shape
    K = sel.shape[1]
    tk = _pick_tile(N, 1792)
    nk = N // tk

    def body(d_ref, s_ref, o_ref, acc_ref):
        k = pl.program_id(2)

        @pl.when(k == 0)
        def _():
            acc_ref[...] = jnp.zeros_like(acc_ref)

        acc_ref[...] += _dot_nt(d_ref[0].astype(BF16), s_ref[0])

        @pl.when(k == nk - 1)
        def _():
            o_ref[0, 0] = acc_ref[...].astype(BF16)

    return _pcall(
        body, name=name, grid=(L, 4, nk),
        in_specs=[pl.BlockSpec((1, M, tk), lambda l, s, k: (l, 0, k)),
                  pl.BlockSpec((1, K, tk), lambda l, s, k: (s, 0, k))],
        out_specs=pl.BlockSpec((1, 1, M, K), lambda l, s, k: (l, s, 0, 0)),
        out_shape=_sds((L, 4, M, K), BF16), scratch=(pltpu.VMEM((M, K), F32),),
        sem=("parallel", "parallel", "arbitrary"),
    )(dw, sel)


MLA_RC = 128
MLA_RC_FWD = 128
MLA_AHEAD = 4
LOG2E = math.log2(math.e)
MLA_QSCALE = MLA_SCALE * LOG2E


def _ride_refs(ride, rest, n_out, n_scratch):
    ni, no = (len(ride.inputs), len(ride.out_shapes)) if ride else (0, 0)
    own_out = rest[ni:ni + n_out]
    own_scratch = rest[ni + n_out + no:ni + n_out + no + n_scratch]
    parts = rest[:ni], rest[ni + n_out:ni + n_out + no], rest[ni + n_out + no + n_scratch:]
    return own_out, own_scratch, parts


def _ride_specs(ride):
    hbm = pl.BlockSpec(memory_space=pl.ANY)
    if not ride:
        return [], [], [], [], []
    return ([hbm] * len(ride.inputs), [hbm] * len(ride.out_shapes), list(ride.out_shapes), list(ride.sems),
            list(ride.inputs))


def _mla_fwd(q, k, v, tq, ride=None):
    S = q.shape[0]
    nq = S // tq
    pairs = [(i, j) for i in range(nq) for j in range(i + 1)]
    qi = jnp.asarray(np.array([p[0] for p in pairs], np.int32))
    kj = jnp.asarray(np.array([p[1] for p in pairs], np.int32))

    def body(qi_ref, kj_ref, q_ref, k_ref, v_ref, *rest):
        (o_ref, lse_ref), (m_s, l_s, acc_s), riding = _ride_refs(ride, rest, 2, 3)
        t = pl.program_id(1)
        i, j = qi_ref[t], kj_ref[t]
        if ride:
            @pl.when((pl.program_id(0) == 0) & (t == 0))
            def _():
                ride.start(*riding)

        @pl.when(j == 0)
        def _():
            m_s[...] = jnp.full_like(m_s, NEG)
            l_s[...] = jnp.zeros_like(l_s)
            acc_s[...] = jnp.zeros_like(acc_s)

        def step(masked):
            rc = min(MLA_RC_FWD, tq)
            nc = tq // rc
            keys = [(c + 1) * rc if masked else tq for c in range(nc)]
            ahead = min(MLA_AHEAD, nc)
            qk = lambda c: _dot_nt(q_ref[c * rc:(c + 1) * rc, :], k_ref[0:keys[c], :])
            scores = [qk(c) for c in range(ahead)]
            for c in range(nc):
                rows = slice(c * rc, (c + 1) * rc)
                s = scores[c]
                if masked:
                    row = lax.broadcasted_iota(jnp.int32, (rc, keys[c]), 0) + c * rc
                    col = lax.broadcasted_iota(jnp.int32, (rc, keys[c]), 1)
                    s = jnp.where(col <= row, s, NEG)
                tiles = [s[:, _hs(u)] for u in range(keys[c] // LANE)]
                mx = functools.reduce(jnp.maximum, tiles)
                m_old = m_s[rows, :]
                m_new = jnp.maximum(m_old, jnp.max(mx, axis=1, keepdims=True))
                alpha = jnp.exp2(m_old - m_new)
                ps = [jnp.exp2(u - m_new) for u in tiles]
                l_s[rows, :] = alpha * l_s[rows, :] + functools.reduce(jnp.add, ps)
                p = jnp.concatenate([u.astype(BF16) for u in ps], axis=1)
                acc_s[rows, :] = alpha * acc_s[rows, :] + _dot(p, v_ref[0:keys[c], :])
                m_s[rows, :] = m_new
                if c + ahead < nc:
                    scores.append(qk(c + ahead))

        @pl.when(j < i)
        def _():
            step(False)

        @pl.when(j == i)
        def _():
            step(True)
            l = jnp.sum(l_s[...], axis=1, keepdims=True)
            o_ref[...] = (acc_s[...] / l).astype(BF16)
            lse_ref[0] = jnp.transpose(m_s[...] + jnp.log2(l))[0:1, :]

        if ride:
            @pl.when((pl.program_id(0) == NH - 1) & (t == len(pairs) - 1))
            def _():
                ride.finish(*riding)

    qmap = lambda h, t, qi_r, kj_r: (qi_r[t], h)
    kmap = lambda h, t, qi_r, kj_r: (kj_r[t], h)
    r_in, r_out, r_shapes, r_sems, r_args = _ride_specs(ride)
    o, lse, *carried = _pcall(
        body, name="mla_fwd", grid=(NH, len(pairs)), prefetch=2,
        in_specs=[pl.BlockSpec((tq, LANE), qmap), pl.BlockSpec((tq, LANE), kmap), pl.BlockSpec((tq, LANE), kmap)]
        + r_in,
        out_specs=[pl.BlockSpec((tq, LANE), qmap),
                   pl.BlockSpec((1, 1, tq), lambda h, t, qi_r, kj_r: (h, 0, qi_r[t]))] + r_out,
        out_shape=[_sds((S, NH * LANE), BF16), _sds((NH, 1, S), F32)] + r_shapes,
        scratch=[pltpu.VMEM((tq, LANE), F32), pltpu.VMEM((tq, LANE), F32), pltpu.VMEM((tq, LANE), F32)] + r_sems,
        sem=("arbitrary", "arbitrary"),
    )(qi, kj, q, k, v, *r_args)
    return o, lse, carried


def _mla_bwd(q, k, v, do, lse, delta, tq, ride=None):
    S = q.shape[0]
    nq = S // tq
    pairs = [(i, j) for j in range(nq) for i in range(j, nq)]
    qi = jnp.asarray(np.array([p[0] for p in pairs], np.int32))
    kj = jnp.asarray(np.array([p[1] for p in pairs], np.int32))

    def body(qi_ref, kj_ref, q_ref, k_ref, v_ref, do_ref, lse_ref, dl_ref, *rest):
        (dq_ref, dk_ref, dv_ref), (dq_s, dk_s, dv_s), riding = _ride_refs(ride, rest, 3, 3)
        t = pl.program_id(1)
        i, j = qi_ref[t], kj_ref[t]
        if ride:
            @pl.when((pl.program_id(0) == 0) & (t == 0))
            def _():
                ride.start(*riding)

        @pl.when(t == 0)
        def _():
            dq_s[...] = jnp.zeros_like(dq_s)

        @pl.when(i == j)
        def _():
            dk_s[...] = jnp.zeros_like(dk_s)
            dv_s[...] = jnp.zeros_like(dv_s)

        qrows = pl.ds(pl.multiple_of(i * tq, tq), tq)

        def step(masked):
            lse_r, dl_r = lse_ref[0], dl_ref[0]
            dq = jnp.zeros((tq, LANE), F32)
            nc = tq // MLA_RC
            q0 = [c * MLA_RC if masked else 0 for c in range(nc)]
            sts = [_dot_nt(k_ref[c * MLA_RC:(c + 1) * MLA_RC, :], q_ref[q0[c]:, :]) for c in range(nc)]
            dpts = [_dot_nt(v_ref[c * MLA_RC:(c + 1) * MLA_RC, :], do_ref[q0[c]:, :]) for c in range(nc)]
            for c in range(nc):
                rows = slice(c * MLA_RC, (c + 1) * MLA_RC)
                qb, dob = q_ref[q0[c]:, :], do_ref[q0[c]:, :]
                pt = jnp.exp2(sts[c] - lse_r[:, q0[c]:])
                if masked:
                    key = lax.broadcasted_iota(jnp.int32, (MLA_RC, tq - q0[c]), 0)
                    qry = lax.broadcasted_iota(jnp.int32, (MLA_RC, tq - q0[c]), 1)
                    pt = jnp.where(key <= qry, pt, 0.0)
                dv_s[rows, :] += _dot(pt.astype(BF16), dob)
                gt = (pt * (dpts[c] - dl_r[:, q0[c]:])).astype(BF16)
                dk_s[rows, :] += _dot(gt, qb)
                part = _dot_tn(gt, k_ref[rows, :])
                if masked:
                    at = pl.multiple_of(i * tq + q0[c], MLA_RC)
                    dq_s[pl.ds(at, tq - q0[c]), :] += part
                else:
                    dq = dq + part
            if not masked:
                dq_s[qrows, :] += dq

        @pl.when(i > j)
        def _():
            step(False)

        @pl.when(i == j)
        def _():
            step(True)
            dq_ref[...] = dq_s[qrows, :] * MLA_SCALE

        @pl.when(i == nq - 1)
        def _():
            dk_ref[...] = dk_s[...] * (1.0 / LOG2E)
            dv_ref[...] = dv_s[...]

        if ride:
            @pl.when((pl.program_id(0) == NH - 1) & (t == len(pairs) - 1))
            def _():
                ride.finish(*riding)

    qmap = lambda h, t, qi_r, kj_r: (qi_r[t], h)
    kmap = lambda h, t, qi_r, kj_r: (kj_r[t], h)
    rmap = lambda h, t, qi_r, kj_r: (h, 0, qi_r[t])
    r_in, r_out, r_shapes, r_sems, r_args = _ride_specs(ride)
    dq, dk, dv, *carried = _pcall(
        body, name="mla_bwd", grid=(NH, len(pairs)), prefetch=2,
        in_specs=[pl.BlockSpec((tq, LANE), qmap), pl.BlockSpec((tq, LANE), kmap), pl.BlockSpec((tq, LANE), kmap),
                  pl.BlockSpec((tq, LANE), qmap), pl.BlockSpec((1, 1, tq), rmap), pl.BlockSpec((1, 1, tq), rmap)]
        + r_in,
        out_specs=[pl.BlockSpec((tq, LANE), kmap), pl.BlockSpec((tq, LANE), kmap), pl.BlockSpec((tq, LANE), kmap)]
        + r_out,
        out_shape=[_sds((S, NH * LANE), F32), _sds((S, NH * LANE), F32), _sds((S, NH * LANE), F32)] + r_shapes,
        scratch=[pltpu.VMEM((S, LANE), F32), pltpu.VMEM((tq, LANE), F32), pltpu.VMEM((tq, LANE), F32)] + r_sems,
        sem=("arbitrary", "arbitrary"),
    )(qi, kj, q, k, v, do, lse, delta, *r_args)
    return dq, dk, dv, carried


SWA_SUB = 4
SWA_T = SWA_SUB * WIN


def _swa_specs(nsteps, rev):
    step = (lambda i: nsteps - 1 - i) if rev else (lambda i: i)
    cur = lambda w: pl.BlockSpec((SWA_T, w), lambda i: (step(i), 0))
    prev = pl.BlockSpec((WIN, LANE), lambda i: (jnp.maximum(step(i) * SWA_SUB - 1, 0), 0))
    return step, cur, prev


def _swa_probs(qk, bias_h, sink, first_mask):
    s = qk * SWA_SCALE + bias_h
    if first_mask is not None:
        s = jnp.where(first_mask, NEG, s)
    m = jnp.maximum(jnp.max(s, axis=1, keepdims=True), sink)
    e = jnp.exp(s - m)
    es = jnp.exp(sink - m)
    inv = 1.0 / (jnp.sum(e, axis=1, keepdims=True) + es)
    return e * inv, es * inv


SWA_GR = SWA_R * WIN


def _swa_group(ref, rows, g):
    return jnp.concatenate([ref[rows, _hs(g * SWA_R + r)] for r in range(SWA_R)], axis=0)


def _swa_rows(bias, sinks):
    sink_rows = jnp.broadcast_to(sinks[:, None, :], (NH, WIN, LANE)).reshape(NH * WIN, LANE)
    return bias.reshape(NH * WIN, 2 * WIN), sink_rows


def _swa_fwd(qs, ks, vs, bias, sinks):
    S = qs.shape[0]
    nsteps = S // SWA_T
    step, cur, prev = _swa_specs(nsteps, False)

    def body(qs_ref, kc_ref, kp_ref, vc_ref, vp_ref, bias_ref, sk_ref, o_ref):
        first = pl.program_id(0) == 0
        kk = jnp.concatenate([kp_ref[...], kc_ref[...]], axis=0)
        vv = jnp.concatenate([vp_ref[...], vc_ref[...]], axis=0)
        col = lax.broadcasted_iota(jnp.int32, (WIN, 2 * WIN), 1)
        for b in range(SWA_SUB):
            kkb = kk[b * WIN:(b + 2) * WIN]
            vvb = vv[b * WIN:(b + 2) * WIN]
            fm = (first & (col < WIN)) if b == 0 else None
            rows = slice(b * WIN, (b + 1) * WIN)
            qks = [_dot_nt(qs_ref[rows, _hs(h)], kkb) for h in range(NH)]
            for h in range(NH):
                p, _ = _swa_probs(qks[h], bias_ref[h], sk_ref[h:h + 1, 0:1], fm)
                o_ref[rows, _hs(h)] = _dot(p.astype(BF16), vvb).astype(BF16)

    return _pcall(
        body, name="swa_fwd", grid=(nsteps,),
        in_specs=[cur(NH * LANE), cur(LANE), prev, cur(LANE), prev, _full(NH, WIN, 2 * WIN), _full(NH, LANE)],
        out_specs=cur(NH * LANE), out_shape=_sds((S, NH * LANE), BF16), sem=("arbitrary",),
    )(qs, ks, ks, vs, vs, bias, sinks)


def _swa_bwd(qs, ks, vs, do, bias, sinks):
    S = qs.shape[0]
    nsteps = S // SWA_T
    step, cur, prev = _swa_specs(nsteps, True)
    bias_rows, sink_rows = _swa_rows(bias, sinks)

    def body(qs_ref, kc_ref, kp_ref, vc_ref, vp_ref, do_ref, bias_ref, sk_ref,
             dqs_ref, dks_ref, dvs_ref, dbias_ref, dsk_ref, dkk_s, dvv_s, ck_s, cv_s):
        pid = pl.program_id(0)
        first = step(pid) == 0

        @pl.when(pid == 0)
        def _():
            dbias_ref[...] = jnp.zeros_like(dbias_ref)
            dsk_ref[...] = jnp.zeros_like(dsk_ref)
            ck_s[...] = jnp.zeros_like(ck_s)
            cv_s[...] = jnp.zeros_like(cv_s)

        dkk_s[...] = jnp.zeros_like(dkk_s)
        dvv_s[...] = jnp.zeros_like(dvv_s)
        kk = jnp.concatenate([kp_ref[...], kc_ref[...]], axis=0)
        vv = jnp.concatenate([vp_ref[...], vc_ref[...]], axis=0)
        col = lax.broadcasted_iota(jnp.int32, (SWA_GR, 2 * WIN), 1)
        for b in range(SWA_SUB):
            kkb = kk[b * WIN:(b + 2) * WIN]
            vvb = vv[b * WIN:(b + 2) * WIN]
            fm = (first & (col < WIN)) if b == 0 else None
            rows = slice(b * WIN, (b + 1) * WIN)
            keys = slice(b * WIN, (b + 2) * WIN)
            qg = [_swa_group(qs_ref, rows, g) for g in range(2)]
            dog = [_swa_group(do_ref, rows, g) for g in range(2)]
            qks = [_dot_nt(qg[g], kkb) for g in range(2)]
            dps = [_dot_nt(dog[g], vvb) for g in range(2)]
            for g in range(2):
                grows = slice(g * SWA_GR, (g + 1) * SWA_GR)
                p, ps = _swa_probs(qks[g], bias_ref[grows, :], sk_ref[grows, 0:1], fm)
                dl = jnp.sum(p * dps[g], axis=1, keepdims=True)
                ds = p * (dps[g] - dl)
                sink_part = ps * dl
                for r in range(SWA_R):
                    h = g * SWA_R + r
                    dsk_ref[h:h + 1, :] += -jnp.sum(sink_part[r * WIN:(r + 1) * WIN])
                dbias_ref[grows, :] += ds
                dsb = (ds * SWA_SCALE).astype(BF16)
                dq = _dot(dsb, kkb).astype(BF16)
                for r in range(SWA_R):
                    dqs_ref[rows, _hs(g * SWA_R + r)] = dq[r * WIN:(r + 1) * WIN]
                dkk_s[keys, :] += _dot_tn(dsb, qg[g])
                dvv_s[keys, :] += _dot_tn(p.astype(BF16), dog[g])
        dks_ref[...] = dkk_s[WIN:, :]
        dvs_ref[...] = dvv_s[WIN:, :]
        dks_ref[SWA_T - WIN:, :] += ck_s[...]
        dvs_ref[SWA_T - WIN:, :] += cv_s[...]
        ck_s[...] = dkk_s[0:WIN, :]
        cv_s[...] = dvv_s[0:WIN, :]

    dqs, dks, dvs, dbias, dsink = _pcall(
        body, name="swa_bwd", grid=(nsteps,),
        in_specs=[cur(NH * LANE), cur(LANE), prev, cur(LANE), prev, cur(NH * LANE), _full(NH * WIN, 2 * WIN),
                  _full(NH * WIN, LANE)],
        out_specs=[cur(NH * LANE), cur(LANE), cur(LANE), _full(NH * WIN, 2 * WIN), _full(NH, LANE)],
        out_shape=[_sds((S, NH * LANE), BF16), _sds((S, LANE), F32), _sds((S, LANE), F32),
                   _sds((NH * WIN, 2 * WIN), F32), _sds((NH, LANE), F32)],
        scratch=[pltpu.VMEM((SWA_T + WIN, LANE), F32), pltpu.VMEM((SWA_T + WIN, LANE), F32),
                 pltpu.VMEM((WIN, LANE), F32), pltpu.VMEM((WIN, LANE), F32)],
        sem=("arbitrary",),
    )(qs, ks, ks, vs, vs, do, bias_rows, sink_rows)
    return dqs, dks, dvs, dbias.reshape(NH, WIN, 2 * WIN), dsink


def _bias_build(rel_bias, bmap):
    def body(rb_ref, bmap_ref, o_ref):
        bm = bmap_ref[...]
        for h in range(NH):
            acc = jnp.full((WIN, 2 * WIN), NEG, F32)
            for b in range(REL_BUCKETS):
                acc = jnp.where(bm == b, rb_ref[b, h], acc)
            o_ref[h] = acc

    return _pcall(
        body, name="bias_build", grid=(1,),
        in_specs=[pl.BlockSpec(memory_space=pltpu.SMEM), _full(WIN, 2 * WIN)],
        out_specs=_full(NH, WIN, 2 * WIN), out_shape=_sds((NH, WIN, 2 * WIN), F32), sem=("arbitrary",),
    )(rel_bias, bmap)


def _bias_reduce(dbias, bmap):
    def body(db_ref, bmap_ref, o_ref):
        bm = bmap_ref[...]
        for h in range(NH):
            dbh = db_ref[h]
            for b in range(REL_BUCKETS):
                o_ref[b, h] = jnp.sum(jnp.where(bm == b, dbh, 0.0))

    return _pcall(
        body, name="bias_reduce", grid=(1,),
        in_specs=[_full(NH, WIN, 2 * WIN), _full(WIN, 2 * WIN)],
        out_specs=pl.BlockSpec(memory_space=pltpu.SMEM), out_shape=_sds((REL_BUCKETS, NH), F32), sem=("arbitrary",),
    )(dbias, bmap)


def _memkv_fwd(mem, mnorm, wkv):
    def body(mem_ref, g_ref, w_ref, o_ref):
        n, _ = _rms(mem_ref[...])
        o_ref[...] = _dot((n * g_ref[...]).astype(BF16), w_ref[...]).astype(BF16)

    return _pcall(
        body, name="memkv_fwd", grid=(1,), in_specs=[_full(MEM_LEN, D), _full(1, D), _full(D, D)],
        out_specs=_full(MEM_LEN, D), out_shape=_sds((MEM_LEN, D), BF16), sem=("arbitrary",),
    )(mem, mnorm, wkv)


def _mem_probs(qk):
    s = qk * MEM_SCALE
    e = jnp.exp(s - jnp.max(s, axis=1, keepdims=True))
    return e / jnp.sum(e, axis=1, keepdims=True)


def _mem_fwd(qm, kvm, ts):
    S = qm.shape[0]

    def body(q_ref, kv_ref, o_ref):
        qks = [_dot_nt(q_ref[:, _hs(h)], kv_ref[:, _hs(h)]) for h in range(MEM_H)]
        for h in range(MEM_H):
            p = _mem_probs(qks[h])
            o_ref[:, _hs(h)] = _dot(p.astype(BF16), kv_ref[:, _hs(MEM_H + h)]).astype(BF16)

    return _pcall(
        body, name="mem_fwd", grid=(S // ts,), in_specs=[_tok(ts, 512), _full(MEM_LEN, D)],
        out_specs=_tok(ts, 512), out_shape=_sds((S, 512), BF16), sem=("arbitrary",),
    )(qm, kvm)


def _mem_bwd(qm, kvm, do, ts):
    S = qm.shape[0]

    def body(q_ref, kv_ref, do_ref, dq_ref, dkv_ref):
        @pl.when(pl.program_id(0) == 0)
        def _():
            dkv_ref[...] = jnp.zeros_like(dkv_ref)

        qks = [_dot_nt(q_ref[:, _hs(h)], kv_ref[:, _hs(h)]) for h in range(MEM_H)]
        dps = [_dot_nt(do_ref[:, _hs(h)], kv_ref[:, _hs(MEM_H + h)]) for h in range(MEM_H)]
        for h in range(MEM_H):
            qh, kh, doh = q_ref[:, _hs(h)], kv_ref[:, _hs(h)], do_ref[:, _hs(h)]
            p = _mem_probs(qks[h])
            dp = dps[h]
            ds = (p * (dp - jnp.sum(p * dp, axis=1, keepdims=True)) * MEM_SCALE).astype(BF16)
            dq_ref[:, _hs(h)] = _dot(ds, kh).astype(BF16)
            dkv_ref[:, _hs(h)] += _dot_tn(ds, qh)
            dkv_ref[:, _hs(MEM_H + h)] += _dot_tn(p.astype(BF16), doh)

    return _pcall(
        body, name="mem_bwd", grid=(S // ts,), in_specs=[_tok(ts, 512), _full(MEM_LEN, D), _tok(ts, 512)],
        out_specs=[_tok(ts, 512), _full(MEM_LEN, D)],
        out_shape=[_sds((S, 512), BF16), _sds((MEM_LEN, D), F32)], sem=("arbitrary",),
    )(qm, kvm, do)


def _memkv_bwd(mem, mnorm, wkv, dkvm):
    def body(mem_ref, g_ref, w_ref, dkv_ref, dw_ref, dg_ref):
        n, _ = _rms(mem_ref[...])
        dkvb = dkv_ref[...].astype(BF16)
        dw_ref[...] = _dot_tn((n * g_ref[...]).astype(BF16), dkvb)
        dg_ref[...] = _colsum(_dot_nt(dkvb, w_ref[...]) * n)

    return _pcall(
        body, name="memkv_bwd", grid=(1,), in_specs=[_full(MEM_LEN, D), _full(1, D), _full(D, D), _full(MEM_LEN, D)],
        out_specs=[_full(D, D), _full(1, D)], out_shape=[_sds((D, D), F32), _sds((1, D), F32)], sem=("arbitrary",),
    )(mem, mnorm, wkv, dkvm)


def _adamw(w, g, m, v, name):
    rows, cols = w.shape
    tr = min(rows, FLAT_TILE)
    assert rows % tr == 0

    def body(w_ref, g_ref, m_ref, v_ref, d_ref, nm_ref, nv_ref):
        gv = g_ref[...]
        nm = ADAM_B1 * m_ref[...] + (1.0 - ADAM_B1) * gv
        nv = ADAM_B2 * v_ref[...] + (1.0 - ADAM_B2) * jnp.square(gv)
        m_hat = nm / (1.0 - ADAM_B1 ** ADAM_STEP)
        v_hat = nv / (1.0 - ADAM_B2 ** ADAM_STEP)
        d_ref[...] = -ADAM_LR * (m_hat / (jnp.sqrt(v_hat) + ADAM_EPS) + ADAM_WD * w_ref[...])
        nm_ref[...] = nm
        nv_ref[...] = nv

    spec = _tok(tr, cols)
    return _pcall(
        body, name=name, grid=(rows // tr,), in_specs=[spec] * 4, out_specs=[spec] * 3,
        out_shape=[_sds((rows, cols), F32)] * 3, sem=("arbitrary",),
    )(w, g, m, v)


def _my_place():
    return lax.axis_index("x"), lax.axis_index("y"), lax.axis_index("c")


def _remote(src, dst, send_sems, recv_sems, k, to):
    return pltpu.make_async_remote_copy(src_ref=src, dst_ref=dst, send_sem=send_sems.at[k], recv_sem=recv_sems.at[k],
                                        device_id=to, device_id_type=MESH_ID)


class _Rider(NamedTuple):
    inputs: list
    out_shapes: list
    sems: list
    start: Callable
    finish: Callable


def _run_rider(rider, name):
    ni, no = len(rider.inputs), len(rider.out_shapes)

    def body(*refs):
        parts = refs[:ni], refs[ni:ni + no], refs[ni + no:]
        rider.start(*parts)
        rider.finish(*parts)

    hbm = pl.BlockSpec(memory_space=pl.ANY)
    return pl.pallas_call(body, name=name, out_shape=rider.out_shapes, in_specs=[hbm] * ni, out_specs=[hbm] * no,
                          scratch_shapes=rider.sems)(*rider.inputs)


def _join_riders(*riders):
    def cut(seq, lens):
        at, out = 0, []
        for n in lens:
            out.append(seq[at:at + n])
            at += n
        return out

    def each(ins, outs, sems):
        return zip(riders, cut(ins, [len(r.inputs) for r in riders]), cut(outs, [len(r.out_shapes) for r in riders]),
                   cut(sems, [len(r.sems) for r in riders]))

    def start(ins, outs, sems):
        for r, i, o, s in each(ins, outs, sems):
            r.start(i, o, s)

    def finish(ins, outs, sems):
        for r, i, o, s in each(ins, outs, sems):
            r.finish(i, o, s)

    return _Rider([a for r in riders for a in r.inputs], [a for r in riders for a in r.out_shapes],
                  [a for r in riders for a in r.sems], start, finish)


def _gather_rider(shards):
    n = len(shards)

    def copies(ins, outs, sems):
        send_sems, recv_sems, local_sems = sems
        x, y, c = _my_place()
        chips = [(1 - x, y), (x, 1 - y), (1 - x, 1 - y)]
        mine = [pltpu.make_async_copy(ins[i], outs[i].at[2 * x + y], local_sems.at[i]) for i in range(n)]

        def copy(i, k, slot, to):
            return _remote(ins[i], outs[i].at[slot], send_sems, recv_sems, 3 * i + k, to)

        sends = [copy(i, k, 2 * x + y, (px, py, c)) for i in range(n) for k, (px, py) in enumerate(chips)]
        lands = [copy(i, k, 2 * px + py, (px, py, c)) for i in range(n) for k, (px, py) in enumerate(chips)]
        return mine, sends, lands

    def start(ins, outs, sems):
        mine, sends, _ = copies(ins, outs, sems)
        for cp in mine + sends:
            cp.start()

    def finish(ins, outs, sems):
        mine, sends, lands = copies(ins, outs, sems)
        for cp in lands:
            cp.wait_recv()
        for cp in sends:
            cp.wait_send()
        for cp in mine:
            cp.wait()

    return _Rider(list(shards), [_sds((4,) + s.shape, s.dtype) for s in shards],
                  [pltpu.SemaphoreType.DMA((3 * n,)), pltpu.SemaphoreType.DMA((3 * n,)), pltpu.SemaphoreType.DMA((n,))],
                  start, finish)


def _scatter_rider(gws, layer, gsmall=None):
    n = len(gws)
    per = n + (gsmall is not None)

    def plan(ins, outs, sems):
        send_sems, recv_sems, local_sems = sems
        x, y, c = _my_place()
        me = 4 * x + 2 * y + c

        def peer(m):
            return (x ^ ((m >> 2) & 1), y ^ ((m >> 1) & 1), c ^ (m & 1))

        def slot_of(m):
            px, py, pc = peer(m)
            return 4 * px + 2 * py + pc

        def piece(i, m, slot):
            px, py, pc = peer(m)
            return _remote(ins[i].at[2 * px + py], outs[i].at[slot], send_sems, recv_sems, (m - 1) * per + i,
                           (px, py, pc))

        def small(m, slot):
            return _remote(ins[n], outs[n].at[slot], send_sems, recv_sems, (m - 1) * per + n, peer(m))

        own = [pltpu.make_async_copy(ins[i].at[2 * x + y], outs[i].at[me], local_sems.at[i]) for i in range(n)]
        return c, me, slot_of, piece, small, own

    def start(ins, outs, sems):
        c, me, slot_of, piece, small, own = plan(ins, outs, sems)

        @pl.when(c == layer)
        def _():
            for cp in own:
                cp.start()
            for m in (2, 4, 6):
                for i in range(n):
                    piece(i, m, me).start()

        @pl.when(c != layer)
        def _():
            for m in (1, 3, 5, 7):
                for i in range(n):
                    piece(i, m, me).start()

        if gsmall is not None:
            pltpu.make_async_copy(ins[n], outs[n].at[me], sems[2].at[n]).start()
            for m in range(1, 8):
                small(m, me).start()

    def finish(ins, outs, sems):
        c, me, slot_of, piece, small, own = plan(ins, outs, sems)

        @pl.when(c == layer)
        def _():
            for m in range(1, 8):
                for i in range(n):
                    piece(i, m, slot_of(m)).wait_recv()
            for m in (2, 4, 6):
                for i in range(n):
                    piece(i, m, me).wait_send()
            for cp in own:
                cp.wait()

        @pl.when(c != layer)
        def _():
            for m in (1, 3, 5, 7):
                for i in range(n):
                    piece(i, m, me).wait_send()

        if gsmall is not None:
            for m in range(1, 8):
                small(m, slot_of(m)).wait_recv()
            for m in range(1, 8):
                small(m, me).wait_send()
            pltpu.make_async_copy(ins[n], outs[n].at[me], sems[2].at[n]).wait()

    inputs = list(gws) + ([gsmall] if gsmall is not None else [])
    shapes = [_sds((8,) + g.shape[1:], g.dtype) for g in gws]
    if gsmall is not None:
        shapes.append(_sds((8,) + gsmall.shape, gsmall.dtype))
    nsem = 7 * per
    return _Rider(inputs, shapes, [pltpu.SemaphoreType.DMA((nsem,)), pltpu.SemaphoreType.DMA((nsem,)),
                                   pltpu.SemaphoreType.DMA((per,))], start, finish)


def _sum_slots(slots, name, other=None):
    _, r, c = slots.shape
    tr = min(r, FLAT_TILE)
    assert r % tr == 0
    out_dtype = F32 if other is None else BF16

    def total(ref):
        acc = ref[0].astype(F32)
        for d in range(1, 8):
            acc = acc + ref[d].astype(F32)
        return acc

    def body(*refs):
        o_ref = refs[-1]
        if other is None:
            o_ref[...] = total(refs[0])
        else:
            core = lax.axis_index("c")

            @pl.when(core == 0)
            def _():
                o_ref[...] = total(refs[0]).astype(out_dtype)

            @pl.when(core == 1)
            def _():
                o_ref[...] = total(refs[1]).astype(out_dtype)

    spec = pl.BlockSpec((8, tr, c), lambda i: (0, i, 0))
    ins = [slots] if other is None else [slots, other]
    return _pcall(
        body, name=name, grid=(r // tr,), in_specs=[spec] * len(ins),
        out_specs=_tok(tr, c), out_shape=_sds((r, c), out_dtype), sem=("arbitrary",),
    )(*ins)


def _swap_rider(reds):
    n = len(reds)

    def copies(ins, outs, sems):
        send_sems, recv_sems, local_sems = sems
        x, y, c = _my_place()
        sibling = (x, y, 1 - c)
        mine = [pltpu.make_async_copy(ins[i], outs[i].at[c], local_sems.at[i]) for i in range(n)]
        sends = [_remote(ins[i], outs[i].at[c], send_sems, recv_sems, i, sibling) for i in range(n)]
        lands = [_remote(ins[i], outs[i].at[1 - c], send_sems, recv_sems, i, sibling) for i in range(n)]
        return mine, sends, lands

    def start(ins, outs, sems):
        mine, sends, _ = copies(ins, outs, sems)
        for cp in mine + sends:
            cp.start()

    def finish(ins, outs, sems):
        mine, sends, lands = copies(ins, outs, sems)
        for cp in lands:
            cp.wait_recv()
        for cp in sends:
            cp.wait_send()
        for cp in mine:
            cp.wait()

    return _Rider(list(reds), [_sds((2,) + r.shape, r.dtype) for r in reds],
                  [pltpu.SemaphoreType.DMA((n,)), pltpu.SemaphoreType.DMA((n,)), pltpu.SemaphoreType.DMA((n,))],
                  start, finish)


W_IN_SHARD, W_IN_SHARD_PAD = 1192, 1280
W_UQ_SHARD, W_UQ_SHARD_PAD = 192, 256


def _swa_place(t):
    z = jnp.zeros_like(t)
    lo = jnp.concatenate([t, z], axis=1)
    hi = jnp.concatenate([z, t], axis=1)
    group = (jnp.arange(NH) // SWA_R).reshape((NH,) + (1,) * (t.ndim - 1))
    full = jnp.where(group == 0, lo, hi)
    return full.reshape((NH * LANE,) + t.shape[2:])


def _swa_unplace(t):
    t = t.reshape((NH, 2, 64) + t.shape[1:])
    return jnp.concatenate([t[:SWA_R, 0], t[SWA_R:, 1]], axis=0)


def _pad_w_o_mla(w):
    return jnp.pad(w.reshape(NH, 64, D), ((0, 0), (0, 64), (0, 0))).reshape(NH * LANE, D)


def _unpad_w_o_mla(g):
    return g.reshape(NH, LANE, D)[:, :64].reshape(NH * 64, D)


def _w_in_cols():
    src = np.full((C_END,), -1, np.int64)
    src[C_CQ:C_KPE] = np.arange(0, 384)
    src[C_KPE + 64:C_KPE + 96] = np.arange(384, 416)
    for h in range(NH):
        at = C_QS + h * LANE + 64 * (h // SWA_R)
        src[at:at + 64] = 416 + h * 64 + np.arange(64)
    src[C_KS:C_END] = np.arange(928, IN_COLS)
    return src


def _w_uq_cols():
    src = np.full((NH * LANE,), -1, np.int64)
    for h in range(NH):
        src[h * LANE:h * LANE + 96] = h * 96 + np.arange(96)
    return src


def _w_ukv_cols():
    src = np.full((2 * NH * LANE,), -1, np.int64)
    for h in range(NH):
        src[h * LANE:h * LANE + 64] = h * 128 + np.arange(64)
        src[NH * LANE + h * LANE:NH * LANE + h * LANE + 64] = h * 128 + 64 + np.arange(64)
    return src


def _selection(src_cols, width, width_pad):
    want = jnp.asarray(np.asarray(src_cols, np.int32))[None, None, :]
    k = jnp.arange(width_pad, dtype=jnp.int32)[None, :, None]
    have = jnp.where(k < width, jnp.arange(4, dtype=jnp.int32)[:, None, None] * width + k, -2)
    return (want == have).astype(BF16)


def _selections():
    return dict(w_in=_selection(_w_in_cols(), W_IN_SHARD, W_IN_SHARD_PAD),
                w_uq=_selection(_w_uq_cols(), W_UQ_SHARD, W_UQ_SHARD_PAD),
                w_ukv=_selection(_w_ukv_cols(), 256, 256))


def _pad_last(a, width):
    return jnp.pad(a, ((0, 0),) * (a.ndim - 1) + ((0, width - a.shape[-1]),))


def _wire_shards(W, l):
    out = {n: W[n][l].astype(BF16) for n in SHARDED}
    out["w_in"] = _pad_last(out["w_in"], W_IN_SHARD_PAD)
    out["w_uq"] = _pad_last(out["w_uq"], W_UQ_SHARD_PAD)
    return [out[n] for n in SHARDED]


def _join_shards(t, axis):
    _, L, r, c = t.shape
    if axis == 2:
        return t.transpose(1, 2, 0, 3).reshape(L, r, 4 * c)
    return t.transpose(1, 0, 2, 3).reshape(L, 4 * r, c)


EARLY = SHARDED[:3]
LATE = SHARDED[3:]


def _kernel_weights_early(gathered, sels):
    lay = {n: _select_fwd(a[:, None], sels[n], "lay_" + n)[0] for n, a in zip(EARLY, gathered)}
    return dict(win=lay["w_in"], wuq=lay["w_uq"], wukv=lay["w_ukv"])


def _kernel_weights_late(gathered):
    whole = {n: _join_shards(a[:, None], SHARD_AXIS[n])[0] for n, a in zip(LATE, gathered)}
    return dict(wmem=whole["w_mem_kv"], wa=_pad_w_o_mla(whole["w_o_mla"]),
                wb=_swa_place(whole["w_o_swa"].reshape(NH, 64, D)), wc=whole["w_o_mem"], wout=whole["w_out"],
                wup=whole["w_up"], wdown=whole["w_down"])


def _kernel_weights(gathered, sels):
    return {**_kernel_weights_early(gathered[:3], sels), **_kernel_weights_late(gathered[3:])}


def _cols_to_shards(g):
    r, c4 = g.shape
    return g.reshape(r, 4, c4 // 4).transpose(1, 0, 2).astype(BF16)


def _rope_tables(S):
    pos = jnp.arange(S, dtype=F32)
    inv = 1.0 / (ROPE_THETA ** (jnp.arange(0, 32, 2, dtype=F32) / 32))
    ang = pos[:, None] * inv[None, :]
    cos, sin = jnp.cos(ang), jnp.sin(ang)
    one, zero = jnp.ones((S, 64), F32), jnp.zeros((S, 16), F32)
    rc = jnp.concatenate([one, cos, cos, jnp.ones((S, 32), F32)], axis=1)
    rs1 = jnp.concatenate([jnp.zeros((S, 64), F32), zero, sin, jnp.zeros((S, 32), F32)], axis=1)
    rs2 = jnp.concatenate([jnp.zeros((S, 64), F32), -sin, zero, jnp.zeros((S, 32), F32)], axis=1)
    return rc, rs1, rs2


def _bucket_map():
    qi = jnp.arange(WIN)[:, None]
    kj = jnp.arange(2 * WIN)[None, :]
    dist = qi + WIN - kj
    n = jnp.maximum(dist, 0)
    max_exact = REL_BUCKETS // 2
    nf = jnp.maximum(n, 1).astype(F32)
    large = max_exact + (jnp.log(nf / max_exact) / math.log(128 / max_exact)
                         * (REL_BUCKETS - max_exact)).astype(jnp.int32)
    large = jnp.minimum(large, REL_BUCKETS - 1)
    bucket = jnp.where(n < max_exact, n, large)
    return jnp.where((dist >= 0) & (dist < WIN), bucket, -1).astype(jnp.int32)


TS = 256
TQ = 1024
TQ_FWD = 1024


def _local_step(x, mem, tgt, kw0, sp, sels, kw1=None, wire=None):
    S = x.shape[0]
    ts = min(TS, S)
    tq = min(TQ, S)
    rc, rs1, rs2 = _rope_tables(S)
    bmap = _bucket_map()
    bias = _bias_build(sp["rel_bias"], bmap)
    row = lambda v: v.reshape(1, -1)

    saved = []
    kw = [dict(kw0), kw1]
    for l in range(DEPTH):
        w = kw[l]
        an, qn, kvn = row(sp["attn_norm"][l]), row(sp["mla_q_norm"][l]), row(sp["mla_kv_norm"][l])
        bg, mnorm, mlpn = row(sp["b_gate"][l]), row(sp["mem_norm"][l]), row(sp["mlp_norm"][l])
        sinks = jnp.broadcast_to(sp["attn_sinks"][l][:, None], (NH, LANE))
        q, k, v, qs, ks, vs, qm, g = _pre_fwd(x, an, w["win"], bg, qn, kvn, w["wuq"], w["wukv"], rc, rs1, rs2, ts)
        carry = _join_riders(_gather_rider(wire[0]), _gather_rider(wire[1])) if (l == 0 and wire) else None
        oa, lse, got = _mla_fwd(q, k, v, min(TQ_FWD, S), carry)
        if carry:
            w.update(_kernel_weights_late(got[:len(LATE)]))
            kw[1] = _kernel_weights(got[len(LATE):], sels)
        ob = _swa_fwd(qs, ks, vs, bias, sinks)
        kvm = _memkv_fwd(mem, mnorm, w["wmem"])
        oc = _mem_fwd(qm, kvm, ts)
        x1, yb = _merge_fwd(x, g, oa, ob, oc, w["wa"], w["wb"], w["wc"], w["wout"], ts)
        x2 = _mlp_fwd(x1, mlpn, w["wup"], w["wdown"], ts)
        saved.append(dict(w=w, x=x, x1=x1, q=q, k=k, v=v, qs=qs, ks=ks, vs=vs, qm=qm, g=g, oa=oa, lse=lse, ob=ob,
                          oc=oc, kvm=kvm, yb=yb, an=an, qn=qn, kvn=kvn, mnorm=mnorm, mlpn=mlpn, sinks=sinks))
        x = x2

    sq, dx, dfn = _loss_kernel(x, row(sp["final_norm"]), tgt, ts)

    big = {n: [None] * DEPTH for n in SHARDED}
    small = {n: [None] * DEPTH for n in SMALL if n not in ("rel_bias", "final_norm")}
    dbias_total = None
    slots1 = None
    for l in reversed(range(DEPTH)):
        sv = saved[l]
        w = sv["w"]
        dx1, hb2, dub, ab, dxb, dmlpn = _mlp_bwd(dx, sv["x1"], sv["mlpn"], w["wup"], w["wdown"], ts)
        big["w_up"][l] = _matmul_tn(hb2, dub, "dw_up", shard_axis=1)
        big["w_down"][l] = _matmul_tn(ab, dxb, "dw_down", shard_axis=0)
        small["mlp_norm"][l] = dmlpn[0]

        dgp, dyo, doa, dob, doc, dla, dx1b, dbg = _merge_bwd(dx1, sv["g"], sv["oa"], sv["ob"], sv["oc"], w["wa"],
                                                             w["wb"], w["wc"], w["wout"], ts)
        big["w_out"][l] = _matmul_tn(sv["yb"], dx1b, "dw_out", shard_axis=0)
        big["w_o_mla"][l] = _cols_to_shards(_unpad_w_o_mla(_matmul_tn(sv["oa"], dyo[:, 0:D], "dw_o_mla")))
        big["w_o_swa"][l] = _cols_to_shards(
            _swa_unplace(_matmul_tn(sv["ob"], dyo[:, D:2 * D], "dw_o_swa")).reshape(NH * 64, D))
        big["w_o_mem"][l] = _matmul_tn(sv["oc"], dyo[:, 2 * D:3 * D], "dw_o_mem", shard_axis=1)
        small["b_gate"][l] = dbg[0]

        dqm, dkvm = _mem_bwd(sv["qm"], sv["kvm"], doc, ts)
        dwmem, dmnorm = _memkv_bwd(mem, sv["mnorm"], w["wmem"], dkvm)
        big["w_mem_kv"][l] = dwmem.reshape(4, D // 4, D).astype(BF16)
        small["mem_norm"][l] = dmnorm[0]

        dqs, dks, dvs, dbias, dsink = _swa_bwd(sv["qs"], sv["ks"], sv["vs"], dob, bias, sv["sinks"])
        dbias_total = dbias if dbias_total is None else dbias_total + dbias
        small["attn_sinks"][l] = dsink[:, 0]

        carry = _join_riders(_scatter_rider([big[n][1] for n in SHARDED], 1),
                             _scatter_rider([big[n][0] for n in LATE], 0)) if (l == 0 and wire) else None
        dq, dk, dv, slots = _mla_bwd(sv["q"], sv["k"], sv["v"], doa, sv["lse"], dla, tq, carry)
        if carry:
            late0, late1 = slots[len(SHARDED):], slots[len(EARLY):len(SHARDED)]
            carry = _swap_rider([_sum_slots(s0, "sum_" + n, s1) for n, s0, s1 in zip(LATE, late0, late1)])

        dx, dproj, hb, cqn, ckvn, dqpre, dkv, dan, dqn, dkvn, swapped = _pre_bwd(
            sv["x"], dx1, dq, dk, dv, dqs, dks, dvs, dqm, dgp, sv["an"], sv["qn"], sv["kvn"], w["win"], w["wuq"],
            w["wukv"], rc, rs1, rs2, ts, carry)
        if carry:
            slots1 = (slots[:len(EARLY)], swapped)
        for n, (a, b) in (("w_in", (hb, dproj)), ("w_uq", (cqn, dqpre)), ("w_ukv", (ckvn, dkv))):
            big[n][l] = _select_bwd(_matmul_tn(a, b, "d" + n)[None], sels[n], "shard_d" + n)[0]
        small["attn_norm"][l] = dan[0]
        small["mla_q_norm"][l] = dqn[0]
        small["mla_kv_norm"][l] = dkvn[0]

    gw = [[big[n][l] for n in SHARDED] for l in range(DEPTH)]
    gs = {n: jnp.stack(v) for n, v in small.items()}
    gs["rel_bias"] = _bias_reduce(dbias_total, bmap)
    gs["final_norm"] = dfn[0]
    return sq, dx, gw, slots1, gs


def _flatten(parts, rows):
    flat = jnp.concatenate([p.reshape(-1) for p in parts])
    return jnp.pad(flat, (0, rows * FLAT_W - flat.shape[0])).reshape(rows, FLAT_W)


def _unflatten(buf, shapes):
    flat = buf.reshape(-1)
    out, at = [], 0
    for s in shapes:
        n = int(np.prod(s))
        out.append(flat[at:at + n].reshape(s))
        at += n
    return out


def kernel(x, mem, rel_bias, attn_norm, mem_norm, w_in, b_gate, mla_q_norm, w_uq, mla_kv_norm, w_ukv, attn_sinks, w_mem_kv, w_o_mla, w_o_swa, w_o_mem, w_out, mlp_norm, w_up, w_down, final_norm, loss_target, m_rel_bias, m_attn_norm, m_mem_norm, m_w_in, m_b_gate, m_mla_q_norm, m_w_uq, m_mla_kv_norm, m_w_ukv, m_attn_sinks, m_w_mem_kv, m_w_o_mla, m_w_o_swa, m_w_o_mem, m_w_out, m_mlp_norm, m_w_up, m_w_down, m_final_norm, v_rel_bias, v_attn_norm, v_mem_norm, v_w_in, v_b_gate, v_mla_q_norm, v_w_uq, v_mla_kv_norm, v_w_ukv, v_attn_sinks, v_w_mem_kv, v_w_o_mla, v_w_o_swa, v_w_o_mem, v_w_out, v_mlp_norm, v_w_up, v_w_down, v_final_norm):
    args = dict(locals())
    W = {n: args[n] for n in WEIGHTS}
    M = {n: args["m_" + n] for n in WEIGHTS}
    V = {n: args["v_" + n] for n in WEIGHTS}
    small_shapes = [W[n].shape for n in SMALL]
    sels = _selections()

    wire0, wire1 = _wire_shards(W, 0), _wire_shards(W, 1)
    kw0 = _kernel_weights_early(_run_rider(_gather_rider(wire0[:len(EARLY)]), "gather_weights"), sels)
    sp = {n: W[n] for n in SMALL}

    sq, dx, gw, carried, gs = _local_step(x[0], mem[0], loss_target[0], kw0, sp, sels,
                                          wire=(wire0[len(EARLY):], wire1))

    slots1_early, swapped_late = carried
    gsmall = _flatten([gs[n] for n in SMALL], SMALL_ROWS)
    *slots0_early, small_slots = _run_rider(_scatter_rider(gw[0][:len(EARLY)], 0, gsmall), "scatter_grads")
    reds = [_sum_slots(s0, "sum_" + n, s1) for n, s0, s1 in zip(EARLY, slots0_early, slots1_early)]
    red_small = _sum_slots(small_slots, "sum_small")
    swapped = list(_run_rider(_swap_rider(reds), "swap_layers")) + list(swapped_late)
    G = {n: g.astype(F32) for n, g in zip(SHARDED, swapped)}
    G["w_in"] = G["w_in"][..., :W_IN_SHARD]
    G["w_uq"] = G["w_uq"][..., :W_UQ_SHARD]

    DW, NM, NV = {}, {}, {}
    for n in SHARDED:
        shape = W[n].shape
        two_d = lambda a: a.reshape(-1, shape[-1])
        d, nm, nv = _adamw(two_d(W[n]), two_d(G[n]), two_d(M[n]), two_d(V[n]), "adamw_" + n)
        DW[n], NM[n], NV[n] = d.reshape(shape), nm.reshape(shape), nv.reshape(shape)
    d_s, m_s, v_s = _adamw(_flatten([W[n] for n in SMALL], SMALL_ROWS), red_small,
                           _flatten([M[n] for n in SMALL], SMALL_ROWS), _flatten([V[n] for n in SMALL], SMALL_ROWS),
                           "adamw_small")
    for out, buf in ((G, red_small), (DW, d_s), (NM, m_s), (NV, v_s)):
        out.update(zip(SMALL, _unflatten(buf, small_shapes)))
    loss = lax.psum(0.5 * sq[0, 0] / D, ("x", "y", "c"))
    return (loss, dx[None], *[G[n] for n in WEIGHTS], *[DW[n] for n in WEIGHTS], *[NM[n] for n in WEIGHTS],
            *[NV[n] for n in WEIGHTS])
```

```python
import functools
import math
from typing import Callable, NamedTuple

import numpy as np
import jax
import jax.numpy as jnp
from jax import lax
from jax.experimental import pallas as pl
from jax.experimental.pallas import tpu as pltpu

F32 = jnp.float32
BF16 = jnp.bfloat16

D = 1024
DFF = 4096
DEPTH = 2
EPS = 1e-6
LANE = 128
NH = 8
SWA_R = 4
MEM_H = 4
MEM_LEN = 256
WIN = 128
NEG = -1e30
MLA_SCALE = 96 ** -0.5
SWA_SCALE = 64 ** -0.5
MEM_SCALE = 128 ** -0.5
REL_BUCKETS = 32
ROPE_THETA = 10000.0

C_CQ, C_CKV, C_KPE, C_QS, C_KS, C_VS, C_QM, C_G, C_END = 0, 256, 384, 512, 1536, 1664, 1792, 2304, 5376
IN_COLS = 4768

ADAM_LR = 0.001
ADAM_B1 = 0.9
ADAM_B2 = 0.999
ADAM_EPS = 1e-08
ADAM_WD = 0.01
ADAM_STEP = 10

VMEM_LIMIT = 56 * 1024 * 1024

SHARDED = ("w_in", "w_uq", "w_ukv", "w_mem_kv", "w_o_mla", "w_o_swa", "w_o_mem", "w_out", "w_up", "w_down")
SHARD_AXIS = {"w_in": 2, "w_uq": 2, "w_ukv": 2, "w_mem_kv": 1, "w_o_mla": 2, "w_o_swa": 2, "w_o_mem": 2,
              "w_out": 1, "w_up": 2, "w_down": 1}
SMALL = ("rel_bias", "attn_norm", "mem_norm", "b_gate", "mla_q_norm", "mla_kv_norm", "attn_sinks", "mlp_norm",
         "final_norm")
WEIGHTS = ("rel_bias", "attn_norm", "mem_norm", "w_in", "b_gate", "mla_q_norm", "w_uq", "mla_kv_norm", "w_ukv",
           "attn_sinks", "w_mem_kv", "w_o_mla", "w_o_swa", "w_o_mem", "w_out", "mlp_norm", "w_up", "w_down",
           "final_norm")
FLAT_W = 1024
FLAT_TILE = 256
SMALL_ROWS = 16
MESH_ID = pl.DeviceIdType.MESH


def _pcall(body, *, name, grid, in_specs, out_specs, out_shape, scratch=(), prefetch=0, sem=None):
    params = pltpu.CompilerParams(dimension_semantics=sem, vmem_limit_bytes=VMEM_LIMIT)
    if prefetch:
        spec = pltpu.PrefetchScalarGridSpec(num_scalar_prefetch=prefetch, grid=grid, in_specs=in_specs,
                                            out_specs=out_specs, scratch_shapes=scratch)
        return pl.pallas_call(body, name=name, grid_spec=spec, out_shape=out_shape, compiler_params=params)
    return pl.pallas_call(body, name=name, grid=grid, in_specs=in_specs, out_specs=out_specs, out_shape=out_shape,
                          scratch_shapes=scratch, compiler_params=params)


def _tok(ts, w):
    return pl.BlockSpec((ts, w), lambda i: (i, 0))


def _full(*shape):
    return pl.BlockSpec(shape, lambda *_: (0,) * len(shape))


def _sds(shape, dtype):
    return jax.ShapeDtypeStruct(shape, dtype)


def _dot(a, b):
    return jnp.dot(a, b, preferred_element_type=F32)


def _dot_nt(a, b):
    return lax.dot_general(a, b, (((1,), (1,)), ((), ())), preferred_element_type=F32)


def _dot_tn(a, b):
    return lax.dot_general(a, b, (((0,), (0,)), ((), ())), preferred_element_type=F32)


def _rms(x):
    r = lax.rsqrt(jnp.mean(x * x, axis=-1, keepdims=True) + EPS)
    return x * r, r


def _rms_bwd(dyg, n, r):
    return r * (dyg - n * jnp.mean(n * dyg, axis=-1, keepdims=True))


def _rope(t, c, s1, s2):
    return t * c + pltpu.roll(t, 16, 1) * s1 + pltpu.roll(t, LANE - 16, 1) * s2


def _rope_bwd(dy, c, s1, s2):
    return dy * c + pltpu.roll(dy * s1, LANE - 16, 1) + pltpu.roll(dy * s2, 16, 1)


def _hs(h):
    return slice(h * LANE, (h + 1) * LANE)


def _colsum(t):
    return jnp.sum(t, axis=0, keepdims=True)


def _pre_fwd(x, an, win, bg, qn, kvn, wuq, wukv, rc, rs1, rs2, ts):
    S = x.shape[0]

    def body(x_ref, an_ref, win_ref, bg_ref, qn_ref, kvn_ref, wuq_ref, wukv_ref, rc_ref, rs1_ref, rs2_ref,
             q_ref, k_ref, v_ref, qs_ref, ks_ref, vs_ref, qm_ref, g_ref):
        n, _ = _rms(x_ref[...])
        hb = (n * an_ref[...]).astype(BF16)
        pa = _dot(hb, win_ref[:, C_CQ:C_QS])
        ncq, _ = _rms(pa[:, 0:256])
        cqn = (ncq * qn_ref[...]).astype(BF16)
        nkv, _ = _rms(pa[:, 256:384])
        ckvn = (nkv * kvn_ref[...]).astype(BF16)
        c, s1, s2 = rc_ref[...], rs1_ref[...], rs2_ref[...]
        kper = _rope(pa[:, 384:512], c, s1, s2)
        qp = _dot(cqn, wuq_ref[...])
        kv = _dot(ckvn, wukv_ref[...])
        for h in range(NH):
            q_ref[:, _hs(h)] = (_rope(qp[:, _hs(h)], c, s1, s2) * MLA_QSCALE).astype(BF16)
            k_ref[:, _hs(h)] = (kv[:, _hs(h)] + kper).astype(BF16)
        v_ref[...] = kv[:, NH * LANE:].astype(BF16)
        pb = _dot(hb, win_ref[:, C_QS:C_G])
        qs_ref[...] = pb[:, 0:1024].astype(BF16)
        ks_ref[...] = pb[:, 1024:1152].astype(BF16)
        vs_ref[...] = pb[:, 1152:1280].astype(BF16)
        qm_ref[...] = pb[:, 1280:1792].astype(BF16)
        g_ref[...] = jax.nn.sigmoid(_dot(hb, win_ref[:, C_G:C_END]) + bg_ref[...])

    return _pcall(
        body, name="pre_fwd", grid=(S // ts,),
        in_specs=[_tok(ts, D), _full(1, D), _full(D, C_END), _full(1, 3 * D), _full(1, 256), _full(1, 128),
                  _full(256, NH * LANE), _full(128, 2 * NH * LANE), _tok(ts, LANE), _tok(ts, LANE), _tok(ts, LANE)],
        out_specs=[_tok(ts, 1024), _tok(ts, 1024), _tok(ts, 1024), _tok(ts, 1024), _tok(ts, 128), _tok(ts, 128),
                   _tok(ts, 512), _tok(ts, 3 * D)],
        out_shape=[_sds((S, 1024), BF16), _sds((S, 1024), BF16), _sds((S, 1024), BF16), _sds((S, 1024), BF16),
                   _sds((S, 128), BF16), _sds((S, 128), BF16), _sds((S, 512), BF16), _sds((S, 3 * D), F32)],
        sem=("arbitrary",),
    )(x, an, win, bg, qn, kvn, wuq, wukv, rc, rs1, rs2)


def _merge_fwd(x, g, oa, ob, oc, wa, wb, wc, wout, ts):
    S = x.shape[0]

    def body(x_ref, g_ref, oa_ref, ob_ref, oc_ref, wa_ref, wb_ref, wc_ref, wout_ref, x1_ref, yb_ref):
        y = g_ref[:, 0:D] * _dot(oa_ref[...], wa_ref[...])
        y = y + g_ref[:, D:2 * D] * _dot(ob_ref[...], wb_ref[...])
        y = y + g_ref[:, 2 * D:3 * D] * _dot(oc_ref[...], wc_ref[...])
        yb = y.astype(BF16)
        yb_ref[...] = yb
        x1_ref[...] = x_ref[...] + _dot(yb, wout_ref[...])

    return _pcall(
        body, name="merge_fwd", grid=(S // ts,),
        in_specs=[_tok(ts, D), _tok(ts, 3 * D), _tok(ts, 1024), _tok(ts, 1024), _tok(ts, 512),
                  _full(1024, D), _full(1024, D), _full(512, D), _full(D, D)],
        out_specs=[_tok(ts, D), _tok(ts, D)],
        out_shape=[_sds((S, D), F32), _sds((S, D), BF16)],
        sem=("arbitrary",),
    )(x, g, oa, ob, oc, wa, wb, wc, wout)


def _mlp_fwd(x1, mn, wup, wdown, ts):
    S = x1.shape[0]

    def body(x_ref, mn_ref, wup_ref, wdown_ref, x2_ref):
        xv = x_ref[...]
        n, _ = _rms(xv)
        u = _dot((n * mn_ref[...]).astype(BF16), wup_ref[...])
        a = jnp.square(jnp.maximum(u, 0.0))
        x2_ref[...] = xv + _dot(a.astype(BF16), wdown_ref[...])

    return _pcall(
        body, name="mlp_fwd", grid=(S // ts,),
        in_specs=[_tok(ts, D), _full(1, D), _full(D, DFF), _full(DFF, D)],
        out_specs=_tok(ts, D), out_shape=_sds((S, D), F32), sem=("arbitrary",),
    )(x1, mn, wup, wdown)


def _loss_kernel(x, fn, tgt, ts):
    S = x.shape[0]

    def body(x_ref, fn_ref, t_ref, loss_ref, dx_ref, dfn_ref):
        @pl.when(pl.program_id(0) == 0)
        def _():
            loss_ref[...] = jnp.zeros_like(loss_ref)
            dfn_ref[...] = jnp.zeros_like(dfn_ref)

        n, r = _rms(x_ref[...])
        err = n * fn_ref[...] - t_ref[...]
        loss_ref[...] += jnp.sum(err * err)
        dy = err * (1.0 / D)
        dfn_ref[...] += _colsum(dy * n)
        dx_ref[...] = _rms_bwd(dy * fn_ref[...], n, r)

    return _pcall(
        body, name="loss_head", grid=(S // ts,),
        in_specs=[_tok(ts, D), _full(1, D), _tok(ts, D)],
        out_specs=[_full(1, LANE), _tok(ts, D), _full(1, D)],
        out_shape=[_sds((1, LANE), F32), _sds((S, D), F32), _sds((1, D), F32)],
        sem=("arbitrary",),
    )(x, fn, tgt)


def _mlp_bwd(dx2, x1, mn, wup, wdown, ts):
    S = x1.shape[0]

    def body(dx_ref, x_ref, mn_ref, wup_ref, wdown_ref, dx1_ref, hb_ref, dub_ref, ab_ref, dxb_ref, dmn_ref):
        @pl.when(pl.program_id(0) == 0)
        def _():
            dmn_ref[...] = jnp.zeros_like(dmn_ref)

        dx = dx_ref[...]
        n, r = _rms(x_ref[...])
        g = mn_ref[...]
        hb = (n * g).astype(BF16)
        hb_ref[...] = hb
        rl = jnp.maximum(_dot(hb, wup_ref[...]), 0.0)
        ab_ref[...] = jnp.square(rl).astype(BF16)
        dxb = dx.astype(BF16)
        dxb_ref[...] = dxb
        dub = (_dot_nt(dxb, wdown_ref[...]) * (2.0 * rl)).astype(BF16)
        dub_ref[...] = dub
        dh = _dot_nt(dub, wup_ref[...])
        dmn_ref[...] += _colsum(dh * n)
        dx1_ref[...] = dx + _rms_bwd(dh * g, n, r)

    return _pcall(
        body, name="mlp_bwd", grid=(S // ts,),
        in_specs=[_tok(ts, D), _tok(ts, D), _full(1, D), _full(D, DFF), _full(DFF, D)],
        out_specs=[_tok(ts, D), _tok(ts, D), _tok(ts, DFF), _tok(ts, DFF), _tok(ts, D), _full(1, D)],
        out_shape=[_sds((S, D), F32), _sds((S, D), BF16), _sds((S, DFF), BF16), _sds((S, DFF), BF16),
                   _sds((S, D), BF16), _sds((1, D), F32)],
        sem=("arbitrary",),
    )(dx2, x1, mn, wup, wdown)


def _merge_bwd(dx1, g, oa, ob, oc, wa, wb, wc, wout, ts):
    S = dx1.shape[0]

    def body(dx_ref, g_ref, oa_ref, ob_ref, oc_ref, wa_ref, wb_ref, wc_ref, wout_ref,
             dgp_ref, dyo_ref, doa_ref, dob_ref, doc_ref, dla_ref, dxb_ref, dbg_ref):
        @pl.when(pl.program_id(0) == 0)
        def _():
            dbg_ref[...] = jnp.zeros_like(dbg_ref)

        dxb = dx_ref[...].astype(BF16)
        dxb_ref[...] = dxb
        dy = _dot_nt(dxb, wout_ref[...])
        branches = ((oa_ref, wa_ref, doa_ref), (ob_ref, wb_ref, dob_ref), (oc_ref, wc_ref, doc_ref))
        for b, (o_ref, w_ref, do_ref) in enumerate(branches):
            cols = slice(b * D, (b + 1) * D)
            gb = g_ref[:, cols]
            o = o_ref[...]
            dgpre = dy * _dot(o, w_ref[...]) * gb * (1.0 - gb)
            dgp_ref[:, cols] = dgpre.astype(BF16)
            dbg_ref[:, cols] += _colsum(dgpre)
            dyo = (dy * gb).astype(BF16)
            dyo_ref[:, cols] = dyo
            do = _dot_nt(dyo, w_ref[...])
            do_ref[...] = do.astype(BF16)
            if b == 0:
                lane = lax.broadcasted_iota(jnp.int32, (ts, LANE), 1)
                dls = jnp.zeros((ts, LANE), F32)
                for h in range(NH):
                    dl = jnp.sum(do[:, _hs(h)] * o[:, _hs(h)].astype(F32), axis=1, keepdims=True)
                    dls = jnp.where(lane == h, dl, dls)
                dla_ref[:, 0, :] = jnp.transpose(dls)[0:NH, :]

    return _pcall(
        body, name="merge_bwd", grid=(S // ts,),
        in_specs=[_tok(ts, D), _tok(ts, 3 * D), _tok(ts, 1024), _tok(ts, 1024), _tok(ts, 512),
                  _full(1024, D), _full(1024, D), _full(512, D), _full(D, D)],
        out_specs=[_tok(ts, 3 * D), _tok(ts, 3 * D), _tok(ts, 1024), _tok(ts, 1024), _tok(ts, 512),
                   pl.BlockSpec((NH, 1, ts), lambda i: (0, 0, i)), _tok(ts, D), _full(1, 3 * D)],
        out_shape=[_sds((S, 3 * D), BF16), _sds((S, 3 * D), BF16), _sds((S, 1024), BF16), _sds((S, 1024), BF16),
                   _sds((S, 512), BF16), _sds((NH, 1, S), F32), _sds((S, D), BF16), _sds((1, 3 * D), F32)],
        sem=("arbitrary",),
    )(dx1, g, oa, ob, oc, wa, wb, wc, wout)


def _pre_bwd(x, dx1, dq, dk, dv, dqs, dks, dvs, dqm, dgp, an, qn, kvn, win, wuq, wukv, rc, rs1, rs2, ts, ride=None):
    S = x.shape[0]

    def body(x_ref, dx1_ref, dq_ref, dk_ref, dv_ref, dqs_ref, dks_ref, dvs_ref, dqm_ref, dgp_ref,
             an_ref, qn_ref, kvn_ref, win_ref, wuq_ref, wukv_ref, rc_ref, rs1_ref, rs2_ref, *rest):
        own, _, riding = _ride_refs(ride, rest, 10, 0)
        dx_ref, dproj_ref, hb_ref, cqn_ref, ckvn_ref, dqpre_ref, dkv_ref, dan_ref, dqn_ref, dkvn_ref = own

        @pl.when(pl.program_id(0) == 0)
        def _():
            dan_ref[...] = jnp.zeros_like(dan_ref)
            dqn_ref[...] = jnp.zeros_like(dqn_ref)
            dkvn_ref[...] = jnp.zeros_like(dkvn_ref)
            if ride:
                ride.start(*riding)

        n, r = _rms(x_ref[...])
        hb = (n * an_ref[...]).astype(BF16)
        hb_ref[...] = hb
        pa = _dot(hb, win_ref[:, C_CQ:C_KPE])
        ncq, rq = _rms(pa[:, 0:256])
        cqn_ref[...] = (ncq * qn_ref[...]).astype(BF16)
        nkv, rkv = _rms(pa[:, 256:384])
        ckvn_ref[...] = (nkv * kvn_ref[...]).astype(BF16)
        c, s1, s2 = rc_ref[...], rs1_ref[...], rs2_ref[...]

        dkper = jnp.zeros((ts, LANE), F32)
        for h in range(NH):
            dqpre_ref[:, _hs(h)] = _rope_bwd(dq_ref[:, _hs(h)], c, s1, s2).astype(BF16)
            dkh = dk_ref[:, _hs(h)]
            dkper = dkper + dkh
            dkv_ref[:, _hs(h)] = dkh.astype(BF16)
        dkv_ref[:, NH * LANE:] = dv_ref[...].astype(BF16)

        dcqn = _dot_nt(dqpre_ref[...], wuq_ref[...])
        dqn_ref[...] += _colsum(dcqn * ncq)
        dproj_ref[:, C_CQ:C_CKV] = _rms_bwd(dcqn * qn_ref[...], ncq, rq).astype(BF16)
        dckvn = _dot_nt(dkv_ref[...], wukv_ref[...])
        dkvn_ref[...] += _colsum(dckvn * nkv)
        dproj_ref[:, C_CKV:C_KPE] = _rms_bwd(dckvn * kvn_ref[...], nkv, rkv).astype(BF16)
        lane = lax.broadcasted_iota(jnp.int32, (ts, LANE), 1)
        dkpe = jnp.where((lane >= 64) & (lane < 96), _rope_bwd(dkper, c, s1, s2), 0.0)
        dproj_ref[:, C_KPE:C_QS] = dkpe.astype(BF16)
        dproj_ref[:, C_QS:C_KS] = dqs_ref[...]
        dproj_ref[:, C_KS:C_VS] = dks_ref[...].astype(BF16)
        dproj_ref[:, C_VS:C_QM] = dvs_ref[...].astype(BF16)
        dproj_ref[:, C_QM:C_G] = dqm_ref[...]
        dproj_ref[:, C_G:C_END] = dgp_ref[...]

        dh = _dot_nt(dproj_ref[...], win_ref[...])
        dan_ref[...] += _colsum(dh * n)
        dx_ref[...] = dx1_ref[...] + _rms_bwd(dh * an_ref[...], n, r)

        if ride:
            @pl.when(pl.program_id(0) == S // ts - 1)
            def _():
                ride.finish(*riding)

    r_in, r_out, r_shapes, r_sems, r_args = _ride_specs(ride)
    results = _pcall(
        body, name="pre_bwd", grid=(S // ts,),
        in_specs=[_tok(ts, D), _tok(ts, D), _tok(ts, 1024), _tok(ts, 1024), _tok(ts, 1024), _tok(ts, 1024),
                  _tok(ts, 128), _tok(ts, 128), _tok(ts, 512), _tok(ts, 3 * D),
                  _full(1, D), _full(1, 256), _full(1, 128), _full(D, C_END), _full(256, NH * LANE),
                  _full(128, 2 * NH * LANE), _tok(ts, LANE), _tok(ts, LANE), _tok(ts, LANE)] + r_in,
        out_specs=[_tok(ts, D), _tok(ts, C_END), _tok(ts, D), _tok(ts, 256), _tok(ts, 128), _tok(ts, 1024),
                   _tok(ts, 2048), _full(1, D), _full(1, 256), _full(1, 128)] + r_out,
        out_shape=[_sds((S, D), F32), _sds((S, C_END), BF16), _sds((S, D), BF16), _sds((S, 256), BF16),
                   _sds((S, 128), BF16), _sds((S, 1024), BF16), _sds((S, 2048), BF16), _sds((1, D), F32),
                   _sds((1, 256), F32), _sds((1, 128), F32)] + r_shapes,
        scratch=r_sems, sem=("arbitrary",),
    )(x, dx1, dq, dk, dv, dqs, dks, dvs, dqm, dgp, an, qn, kvn, win, wuq, wukv, rc, rs1, rs2, *r_args)
    return (*results[:10], list(results[10:]))


def _pick_tile(n, cap):
    best = LANE
    for t in range(LANE, min(n, cap) + 1, LANE):
        if n % t == 0:
            best = t
    return best


def _matmul_tn(a, b, name, shard_axis=None):
    S, M = a.shape
    N = b.shape[1]
    tm = _pick_tile(M // 4 if shard_axis == 0 else M, 1024)
    tn = _pick_tile(N // 4 if shard_axis == 1 else N, 2048)
    ts = min(S, 1024)
    nk = S // ts

    def body(a_ref, b_ref, o_ref, *acc):
        acc_ref = acc[0] if acc else o_ref

        @pl.when(pl.program_id(2) == 0)
        def _():
            acc_ref[...] = jnp.zeros_like(acc_ref)

        acc_ref[...] += _dot_tn(a_ref[...], b_ref[...])
        if acc:
            @pl.when(pl.program_id(2) == nk - 1)
            def _():
                o_ref[0] = acc_ref[...].astype(o_ref.dtype)

    if shard_axis is None:
        out_spec = pl.BlockSpec((tm, tn), lambda i, j, k: (i, j))
        out_shape, scratch = _sds((M, N), F32), ()
    elif shard_axis == 0:
        per = (M // 4) // tm
        out_spec = pl.BlockSpec((1, tm, tn), lambda i, j, k: (i // per, i % per, j))
        out_shape, scratch = _sds((4, M // 4, N), BF16), (pltpu.VMEM((tm, tn), F32),)
    else:
        per = (N // 4) // tn
        out_spec = pl.BlockSpec((1, tm, tn), lambda i, j, k: (j // per, i, j % per))
        out_shape, scratch = _sds((4, M, N // 4), BF16), (pltpu.VMEM((tm, tn), F32),)
    return _pcall(
        body, name=name, grid=(M // tm, N // tn, nk),
        in_specs=[pl.BlockSpec((ts, tm), lambda i, j, k: (k, i)), pl.BlockSpec((ts, tn), lambda i, j, k: (k, j))],
        out_specs=out_spec, out_shape=out_shape, scratch=scratch, sem=("parallel", "parallel", "arbitrary"),
    )(a, b)


def _select_fwd(a, sel, name):
    _, L, M, K = a.shape
    N = sel.shape[2]
    tn = _pick_tile(N, 1792)

    def body(a_ref, s_ref, o_ref, acc_ref):
        s = pl.program_id(2)

        @pl.when(s == 0)
        def _():
            acc_ref[...] = jnp.zeros_like(acc_ref)

        acc_ref[...] += _dot(a_ref[0, 0], s_ref[0])

        @pl.when(s == 3)
        def _():
            o_ref[0] = acc_ref[...].astype(BF16)

    return _pcall(
        body, name=name, grid=(L, N // tn, 4),
        in_specs=[pl.BlockSpec((1, 1, M, K), lambda l, j, s: (s, l, 0, 0)),
                  pl.BlockSpec((1, K, tn), lambda l, j, s: (s, 0, j))],
        out_specs=pl.BlockSpec((1, M, tn), lambda l, j, s: (l, 0, j)),
        out_shape=_sds((L, M, N), BF16), scratch=(pltpu.VMEM((M, tn), F32),),
        sem=("parallel", "parallel", "arbitrary"),
    )(a, sel)


def _select_bwd(dw, sel, name):
    L, M, N = dw.shape
    K = sel.shape[1]
    tk = _pick_tile(N, 1792)
    nk = N // tk

    def body(d_ref, s_ref, o_ref, acc_ref):
        k = pl.program_id(2)

        @pl.when(k == 0)
        def _():
            acc_ref[...] = jnp.zeros_like(acc_ref)

        acc_ref[...] += _dot_nt(d_ref[0].astype(BF16), s_ref[0])

        @pl.when(k == nk - 1)
        def _():
            o_ref[0, 0] = acc_ref[...].astype(BF16)

    return _pcall(
        body, name=name, grid=(L, 4, nk),
        in_specs=[pl.BlockSpec((1, M, tk), lambda l, s, k: (l, 0, k)),
                  pl.BlockSpec((1, K, tk), lambda l, s, k: (s, 0, k))],
        out_specs=pl.BlockSpec((1, 1, M, K), lambda l, s, k: (l, s, 0, 0)),
        out_shape=_sds((L, 4, M, K), BF16), scratch=(pltpu.VMEM((M, K), F32),),
        sem=("parallel", "parallel", "arbitrary"),
    )(dw, sel)


MLA_RC = 128
MLA_RC_FWD = 128
MLA_AHEAD = 4
LOG2E = math.log2(math.e)
MLA_QSCALE = MLA_SCALE * LOG2E


def _ride_refs(ride, rest, n_out, n_scratch):
    ni, no = (len(ride.inputs), len(ride.out_shapes)) if ride else (0, 0)
    own_out = rest[ni:ni + n_out]
    own_scratch = rest[ni + n_out + no:ni + n_out + no + n_scratch]
    parts = rest[:ni], rest[ni + n_out:ni + n_out + no], rest[ni + n_out + no + n_scratch:]
    return own_out, own_scratch, parts


def _ride_specs(ride):
    hbm = pl.BlockSpec(memory_space=pl.ANY)
    if not ride:
        return [], [], [], [], []
    return ([hbm] * len(ride.inputs), [hbm] * len(ride.out_shapes), list(ride.out_shapes), list(ride.sems),
            list(ride.inputs))


def _mla_fwd(q, k, v, tq, ride=None):
    S = q.shape[0]
    nq = S // tq
    pairs = [(i, j) for i in range(nq) for j in range(i + 1)]
    qi = jnp.asarray(np.array([p[0] for p in pairs], np.int32))
    kj = jnp.asarray(np.array([p[1] for p in pairs], np.int32))

    def body(qi_ref, kj_ref, q_ref, k_ref, v_ref, *rest):
        (o_ref, lse_ref), (m_s, l_s, acc_s), riding = _ride_refs(ride, rest, 2, 3)
        t = pl.program_id(1)
        i, j = qi_ref[t], kj_ref[t]
        if ride:
            @pl.when((pl.program_id(0) == 0) & (t == 0))
            def _():
                ride.start(*riding)

        @pl.when(j == 0)
        def _():
            m_s[...] = jnp.full_like(m_s, NEG)
            l_s[...] = jnp.zeros_like(l_s)
            acc_s[...] = jnp.zeros_like(acc_s)

        def step(masked):
            rc = min(MLA_RC_FWD, tq)
            nc = tq // rc
            keys = [(c + 1) * rc if masked else tq for c in range(nc)]
            ahead = min(MLA_AHEAD, nc)
            qk = lambda c: _dot_nt(q_ref[c * rc:(c + 1) * rc, :], k_ref[0:keys[c], :])
            scores = [qk(c) for c in range(ahead)]
            for c in range(nc):
                rows = slice(c * rc, (c + 1) * rc)
                s = scores[c]
                if masked:
                    row = lax.broadcasted_iota(jnp.int32, (rc, keys[c]), 0) + c * rc
                    col = lax.broadcasted_iota(jnp.int32, (rc, keys[c]), 1)
                    s = jnp.where(col <= row, s, NEG)
                tiles = [s[:, _hs(u)] for u in range(keys[c] // LANE)]
                mx = functools.reduce(jnp.maximum, tiles)
                m_old = m_s[rows, :]
                m_new = jnp.maximum(m_old, jnp.max(mx, axis=1, keepdims=True))
                alpha = jnp.exp2(m_old - m_new)
                ps = [jnp.exp2(u - m_new) for u in tiles]
                l_s[rows, :] = alpha * l_s[rows, :] + functools.reduce(jnp.add, ps)
                p = jnp.concatenate([u.astype(BF16) for u in ps], axis=1)
                acc_s[rows, :] = alpha * acc_s[rows, :] + _dot(p, v_ref[0:keys[c], :])
                m_s[rows, :] = m_new
                if c + ahead < nc:
                    scores.append(qk(c + ahead))

        @pl.when(j < i)
        def _():
            step(False)

        @pl.when(j == i)
        def _():
            step(True)
            l = jnp.sum(l_s[...], axis=1, keepdims=True)
            o_ref[...] = (acc_s[...] / l).astype(BF16)
            lse_ref[0] = jnp.transpose(m_s[...] + jnp.log2(l))[0:1, :]

        if ride:
            @pl.when((pl.program_id(0) == NH - 1) & (t == len(pairs) - 1))
            def _():
                ride.finish(*riding)

    qmap = lambda h, t, qi_r, kj_r: (qi_r[t], h)
    kmap = lambda h, t, qi_r, kj_r: (kj_r[t], h)
    r_in, r_out, r_shapes, r_sems, r_args = _ride_specs(ride)
    o, lse, *carried = _pcall(
        body, name="mla_fwd", grid=(NH, len(pairs)), prefetch=2,
        in_specs=[pl.BlockSpec((tq, LANE), qmap), pl.BlockSpec((tq, LANE), kmap), pl.BlockSpec((tq, LANE), kmap)]
        + r_in,
        out_specs=[pl.BlockSpec((tq, LANE), qmap),
                   pl.BlockSpec((1, 1, tq), lambda h, t, qi_r, kj_r: (h, 0, qi_r[t]))] + r_out,
        out_shape=[_sds((S, NH * LANE), BF16), _sds((NH, 1, S), F32)] + r_shapes,
        scratch=[pltpu.VMEM((tq, LANE), F32), pltpu.VMEM((tq, LANE), F32), pltpu.VMEM((tq, LANE), F32)] + r_sems,
        sem=("arbitrary", "arbitrary"),
    )(qi, kj, q, k, v, *r_args)
    return o, lse, carried


def _mla_bwd(q, k, v, do, lse, delta, tq, ride=None):
    S = q.shape[0]
    nq = S // tq
    pairs = [(i, j) for j in range(nq) for i in range(j, nq)]
    qi = jnp.asarray(np.array([p[0] for p in pairs], np.int32))
    kj = jnp.asarray(np.array([p[1] for p in pairs], np.int32))

    def body(qi_ref, kj_ref, q_ref, k_ref, v_ref, do_ref, lse_ref, dl_ref, *rest):
        (dq_ref, dk_ref, dv_ref), (dq_s, dk_s, dv_s), riding = _ride_refs(ride, rest, 3, 3)
        t = pl.program_id(1)
        i, j = qi_ref[t], kj_ref[t]
        if ride:
            @pl.when((pl.program_id(0) == 0) & (t == 0))
            def _():
                ride.start(*riding)

        @pl.when(t == 0)
        def _():
            dq_s[...] = jnp.zeros_like(dq_s)

        @pl.when(i == j)
        def _():
            dk_s[...] = jnp.zeros_like(dk_s)
            dv_s[...] = jnp.zeros_like(dv_s)

        qrows = pl.ds(pl.multiple_of(i * tq, tq), tq)

        def step(masked):
            lse_r, dl_r = lse_ref[0], dl_ref[0]
            dq = jnp.zeros((tq, LANE), F32)
            nc = tq // MLA_RC
            q0 = [c * MLA_RC if masked else 0 for c in range(nc)]
            sts = [_dot_nt(k_ref[c * MLA_RC:(c + 1) * MLA_RC, :], q_ref[q0[c]:, :]) for c in range(nc)]
            dpts = [_dot_nt(v_ref[c * MLA_RC:(c + 1) * MLA_RC, :], do_ref[q0[c]:, :]) for c in range(nc)]
            for c in range(nc):
                rows = slice(c * MLA_RC, (c + 1) * MLA_RC)
                qb, dob = q_ref[q0[c]:, :], do_ref[q0[c]:, :]
                pt = jnp.exp2(sts[c] - lse_r[:, q0[c]:])
                if masked:
                    key = lax.broadcasted_iota(jnp.int32, (MLA_RC, tq - q0[c]), 0)
                    qry = lax.broadcasted_iota(jnp.int32, (MLA_RC, tq - q0[c]), 1)
                    pt = jnp.where(key <= qry, pt, 0.0)
                dv_s[rows, :] += _dot(pt.astype(BF16), dob)
                gt = (pt * (dpts[c] - dl_r[:, q0[c]:])).astype(BF16)
                dk_s[rows, :] += _dot(gt, qb)
                part = _dot_tn(gt, k_ref[rows, :])
                if masked:
                    at = pl.multiple_of(i * tq + q0[c], MLA_RC)
                    dq_s[pl.ds(at, tq - q0[c]), :] += part
                else:
                    dq = dq + part
            if not masked:
                dq_s[qrows, :] += dq

        @pl.when(i > j)
        def _():
            step(False)

        @pl.when(i == j)
        def _():
            step(True)
            dq_ref[...] = dq_s[qrows, :] * MLA_SCALE

        @pl.when(i == nq - 1)
        def _():
            dk_ref[...] = dk_s[...] * (1.0 / LOG2E)
            dv_ref[...] = dv_s[...]

        if ride:
            @pl.when((pl.program_id(0) == NH - 1) & (t == len(pairs) - 1))
            def _():
                ride.finish(*riding)

    qmap = lambda h, t, qi_r, kj_r: (qi_r[t], h)
    kmap = lambda h, t, qi_r, kj_r: (kj_r[t], h)
    rmap = lambda h, t, qi_r, kj_r: (h, 0, qi_r[t])
    r_in, r_out, r_shapes, r_sems, r_args = _ride_specs(ride)
    dq, dk, dv, *carried = _pcall(
        body, name="mla_bwd", grid=(NH, len(pairs)), prefetch=2,
        in_specs=[pl.BlockSpec((tq, LANE), qmap), pl.BlockSpec((tq, LANE), kmap), pl.BlockSpec((tq, LANE), kmap),
                  pl.BlockSpec((tq, LANE), qmap), pl.BlockSpec((1, 1, tq), rmap), pl.BlockSpec((1, 1, tq), rmap)]
        + r_in,
        out_specs=[pl.BlockSpec((tq, LANE), kmap), pl.BlockSpec((tq, LANE), kmap), pl.BlockSpec((tq, LANE), kmap)]
        + r_out,
        out_shape=[_sds((S, NH * LANE), F32), _sds((S, NH * LANE), F32), _sds((S, NH * LANE), F32)] + r_shapes,
        scratch=[pltpu.VMEM((S, LANE), F32), pltpu.VMEM((tq, LANE), F32), pltpu.VMEM((tq, LANE), F32)] + r_sems,
        sem=("arbitrary", "arbitrary"),
    )(qi, kj, q, k, v, do, lse, delta, *r_args)
    return dq, dk, dv, carried


SWA_SUB = 4
SWA_T = SWA_SUB * WIN


def _swa_specs(nsteps, rev):
    step = (lambda i: nsteps - 1 - i) if rev else (lambda i: i)
    cur = lambda w: pl.BlockSpec((SWA_T, w), lambda i: (step(i), 0))
    prev = pl.BlockSpec((WIN, LANE), lambda i: (jnp.maximum(step(i) * SWA_SUB - 1, 0), 0))
    return step, cur, prev


def _swa_probs(qk, bias_h, sink, first_mask):
    s = qk * SWA_SCALE + bias_h
    if first_mask is not None:
        s = jnp.where(first_mask, NEG, s)
    m = jnp.maximum(jnp.max(s, axis=1, keepdims=True), sink)
    e = jnp.exp(s - m)
    es = jnp.exp(sink - m)
    inv = 1.0 / (jnp.sum(e, axis=1, keepdims=True) + es)
    return e * inv, es * inv


SWA_GR = SWA_R * WIN


def _swa_group(ref, rows, g):
    return jnp.concatenate([ref[rows, _hs(g * SWA_R + r)] for r in range(SWA_R)], axis=0)


def _swa_rows(bias, sinks):
    sink_rows = jnp.broadcast_to(sinks[:, None, :], (NH, WIN, LANE)).reshape(NH * WIN, LANE)
    return bias.reshape(NH * WIN, 2 * WIN), sink_rows


def _swa_fwd(qs, ks, vs, bias, sinks):
    S = qs.shape[0]
    nsteps = S // SWA_T
    step, cur, prev = _swa_specs(nsteps, False)

    def body(qs_ref, kc_ref, kp_ref, vc_ref, vp_ref, bias_ref, sk_ref, o_ref):
        first = pl.program_id(0) == 0
        kk = jnp.concatenate([kp_ref[...], kc_ref[...]], axis=0)
        vv = jnp.concatenate([vp_ref[...], vc_ref[...]], axis=0)
        col = lax.broadcasted_iota(jnp.int32, (WIN, 2 * WIN), 1)
        qk = lambda b: [_dot_nt(qs_ref[b * WIN:(b + 1) * WIN, _hs(h)], kk[b * WIN:(b + 2) * WIN]) for h in range(NH)]
        ahead = qk(0)
        for b in range(SWA_SUB):
            vvb = vv[b * WIN:(b + 2) * WIN]
            fm = (first & (col < WIN)) if b == 0 else None
            rows = slice(b * WIN, (b + 1) * WIN)
            qks, ahead = ahead, (qk(b + 1) if b + 1 < SWA_SUB else None)
            for h in range(NH):
                p, _ = _swa_probs(qks[h], bias_ref[h], sk_ref[h:h + 1, 0:1], fm)
                o_ref[rows, _hs(h)] = _dot(p.astype(BF16), vvb).astype(BF16)

    return _pcall(
        body, name="swa_fwd", grid=(nsteps,),
        in_specs=[cur(NH * LANE), cur(LANE), prev, cur(LANE), prev, _full(NH, WIN, 2 * WIN), _full(NH, LANE)],
        out_specs=cur(NH * LANE), out_shape=_sds((S, NH * LANE), BF16), sem=("arbitrary",),
    )(qs, ks, ks, vs, vs, bias, sinks)


def _swa_bwd(qs, ks, vs, do, bias, sinks):
    S = qs.shape[0]
    nsteps = S // SWA_T
    step, cur, prev = _swa_specs(nsteps, True)
    bias_rows, sink_rows = _swa_rows(bias, sinks)

    def body(qs_ref, kc_ref, kp_ref, vc_ref, vp_ref, do_ref, bias_ref, sk_ref,
             dqs_ref, dks_ref, dvs_ref, dbias_ref, dsk_ref, dkk_s, dvv_s, ck_s, cv_s):
        pid = pl.program_id(0)
        first = step(pid) == 0

        @pl.when(pid == 0)
        def _():
            dbias_ref[...] = jnp.zeros_like(dbias_ref)
            dsk_ref[...] = jnp.zeros_like(dsk_ref)
            ck_s[...] = jnp.zeros_like(ck_s)
            cv_s[...] = jnp.zeros_like(cv_s)

        dkk_s[...] = jnp.zeros_like(dkk_s)
        dvv_s[...] = jnp.zeros_like(dvv_s)
        kk = jnp.concatenate([kp_ref[...], kc_ref[...]], axis=0)
        vv = jnp.concatenate([vp_ref[...], vc_ref[...]], axis=0)
        col = lax.broadcasted_iota(jnp.int32, (SWA_GR, 2 * WIN), 1)
        def products(b):
            rows = slice(b * WIN, (b + 1) * WIN)
            qg = [_swa_group(qs_ref, rows, g) for g in range(2)]
            dog = [_swa_group(do_ref, rows, g) for g in range(2)]
            return (qg, dog, [_dot_nt(qg[g], kk[b * WIN:(b + 2) * WIN]) for g in range(2)],
                    [_dot_nt(dog[g], vv[b * WIN:(b + 2) * WIN]) for g in range(2)])

        ahead = products(0)
        for b in range(SWA_SUB):
            kkb = kk[b * WIN:(b + 2) * WIN]
            fm = (first & (col < WIN)) if b == 0 else None
            rows = slice(b * WIN, (b + 1) * WIN)
            keys = slice(b * WIN, (b + 2) * WIN)
            (qg, dog, qks, dps), ahead = ahead, (products(b + 1) if b + 1 < SWA_SUB else None)
            for g in range(2):
                grows = slice(g * SWA_GR, (g + 1) * SWA_GR)
                p, ps = _swa_probs(qks[g], bias_ref[grows, :], sk_ref[grows, 0:1], fm)
                dl = jnp.sum(p * dps[g], axis=1, keepdims=True)
                ds = p * (dps[g] - dl)
                sink_part = ps * dl
                for r in range(SWA_R):
                    h = g * SWA_R + r
                    dsk_ref[h:h + 1, :] += -jnp.sum(sink_part[r * WIN:(r + 1) * WIN])
                dbias_ref[grows, :] += ds
                dsb = (ds * SWA_SCALE).astype(BF16)
                dq = _dot(dsb, kkb).astype(BF16)
                for r in range(SWA_R):
                    dqs_ref[rows, _hs(g * SWA_R + r)] = dq[r * WIN:(r + 1) * WIN]
                dkk_s[keys, :] += _dot_tn(dsb, qg[g])
                dvv_s[keys, :] += _dot_tn(p.astype(BF16), dog[g])
        dks_ref[...] = dkk_s[WIN:, :]
        dvs_ref[...] = dvv_s[WIN:, :]
        dks_ref[SWA_T - WIN:, :] += ck_s[...]
        dvs_ref[SWA_T - WIN:, :] += cv_s[...]
        ck_s[...] = dkk_s[0:WIN, :]
        cv_s[...] = dvv_s[0:WIN, :]

    dqs, dks, dvs, dbias, dsink = _pcall(
        body, name="swa_bwd", grid=(nsteps,),
        in_specs=[cur(NH * LANE), cur(LANE), prev, cur(LANE), prev, cur(NH * LANE), _full(NH * WIN, 2 * WIN),
                  _full(NH * WIN, LANE)],
        out_specs=[cur(NH * LANE), cur(LANE), cur(LANE), _full(NH * WIN, 2 * WIN), _full(NH, LANE)],
        out_shape=[_sds((S, NH * LANE), BF16), _sds((S, LANE), F32), _sds((S, LANE), F32),
                   _sds((NH * WIN, 2 * WIN), F32), _sds((NH, LANE), F32)],
        scratch=[pltpu.VMEM((SWA_T + WIN, LANE), F32), pltpu.VMEM((SWA_T + WIN, LANE), F32),
                 pltpu.VMEM((WIN, LANE), F32), pltpu.VMEM((WIN, LANE), F32)],
        sem=("arbitrary",),
    )(qs, ks, ks, vs, vs, do, bias_rows, sink_rows)
    return dqs, dks, dvs, dbias.reshape(NH, WIN, 2 * WIN), dsink


def _bias_build(rel_bias, bmap):
    def body(rb_ref, bmap_ref, o_ref):
        bm = bmap_ref[...]
        for h in range(NH):
            acc = jnp.full((WIN, 2 * WIN), NEG, F32)
            for b in range(REL_BUCKETS):
                acc = jnp.where(bm == b, rb_ref[b, h], acc)
            o_ref[h] = acc

    return _pcall(
        body, name="bias_build", grid=(1,),
        in_specs=[pl.BlockSpec(memory_space=pltpu.SMEM), _full(WIN, 2 * WIN)],
        out_specs=_full(NH, WIN, 2 * WIN), out_shape=_sds((NH, WIN, 2 * WIN), F32), sem=("arbitrary",),
    )(rel_bias, bmap)


def _bias_reduce(dbias, bmap):
    def body(db_ref, bmap_ref, o_ref):
        bm = bmap_ref[...]
        for h in range(NH):
            dbh = db_ref[h]
            for b in range(REL_BUCKETS):
                o_ref[b, h] = jnp.sum(jnp.where(bm == b, dbh, 0.0))

    return _pcall(
        body, name="bias_reduce", grid=(1,),
        in_specs=[_full(NH, WIN, 2 * WIN), _full(WIN, 2 * WIN)],
        out_specs=pl.BlockSpec(memory_space=pltpu.SMEM), out_shape=_sds((REL_BUCKETS, NH), F32), sem=("arbitrary",),
    )(dbias, bmap)


def _memkv_fwd(mem, mnorm, wkv):
    def body(mem_ref, g_ref, w_ref, o_ref):
        n, _ = _rms(mem_ref[...])
        o_ref[...] = _dot((n * g_ref[...]).astype(BF16), w_ref[...]).astype(BF16)

    return _pcall(
        body, name="memkv_fwd", grid=(1,), in_specs=[_full(MEM_LEN, D), _full(1, D), _full(D, D)],
        out_specs=_full(MEM_LEN, D), out_shape=_sds((MEM_LEN, D), BF16), sem=("arbitrary",),
    )(mem, mnorm, wkv)


def _mem_probs(qk):
    s = qk * MEM_SCALE
    e = jnp.exp(s - jnp.max(s, axis=1, keepdims=True))
    return e / jnp.sum(e, axis=1, keepdims=True)


def _mem_fwd(qm, kvm, ts):
    S = qm.shape[0]

    def body(q_ref, kv_ref, o_ref):
        qks = [_dot_nt(q_ref[:, _hs(h)], kv_ref[:, _hs(h)]) for h in range(MEM_H)]
        for h in range(MEM_H):
            p = _mem_probs(qks[h])
            o_ref[:, _hs(h)] = _dot(p.astype(BF16), kv_ref[:, _hs(MEM_H + h)]).astype(BF16)

    return _pcall(
        body, name="mem_fwd", grid=(S // ts,), in_specs=[_tok(ts, 512), _full(MEM_LEN, D)],
        out_specs=_tok(ts, 512), out_shape=_sds((S, 512), BF16), sem=("arbitrary",),
    )(qm, kvm)


def _mem_bwd(qm, kvm, do, ts):
    S = qm.shape[0]

    def body(q_ref, kv_ref, do_ref, dq_ref, dkv_ref):
        @pl.when(pl.program_id(0) == 0)
        def _():
            dkv_ref[...] = jnp.zeros_like(dkv_ref)

        qks = [_dot_nt(q_ref[:, _hs(h)], kv_ref[:, _hs(h)]) for h in range(MEM_H)]
        dps = [_dot_nt(do_ref[:, _hs(h)], kv_ref[:, _hs(MEM_H + h)]) for h in range(MEM_H)]
        for h in range(MEM_H):
            qh, kh, doh = q_ref[:, _hs(h)], kv_ref[:, _hs(h)], do_ref[:, _hs(h)]
            p = _mem_probs(qks[h])
            dp = dps[h]
            ds = (p * (dp - jnp.sum(p * dp, axis=1, keepdims=True)) * MEM_SCALE).astype(BF16)
            dq_ref[:, _hs(h)] = _dot(ds, kh).astype(BF16)
            dkv_ref[:, _hs(h)] += _dot_tn(ds, qh)
            dkv_ref[:, _hs(MEM_H + h)] += _dot_tn(p.astype(BF16), doh)

    return _pcall(
        body, name="mem_bwd", grid=(S // ts,), in_specs=[_tok(ts, 512), _full(MEM_LEN, D), _tok(ts, 512)],
        out_specs=[_tok(ts, 512), _full(MEM_LEN, D)],
        out_shape=[_sds((S, 512), BF16), _sds((MEM_LEN, D), F32)], sem=("arbitrary",),
    )(qm, kvm, do)


def _memkv_bwd(mem, mnorm, wkv, dkvm):
    def body(mem_ref, g_ref, w_ref, dkv_ref, dw_ref, dg_ref):
        n, _ = _rms(mem_ref[...])
        dkvb = dkv_ref[...].astype(BF16)
        dw_ref[...] = _dot_tn((n * g_ref[...]).astype(BF16), dkvb)
        dg_ref[...] = _colsum(_dot_nt(dkvb, w_ref[...]) * n)

    return _pcall(
        body, name="memkv_bwd", grid=(1,), in_specs=[_full(MEM_LEN, D), _full(1, D), _full(D, D), _full(MEM_LEN, D)],
        out_specs=[_full(D, D), _full(1, D)], out_shape=[_sds((D, D), F32), _sds((1, D), F32)], sem=("arbitrary",),
    )(mem, mnorm, wkv, dkvm)


def _adamw(w, g, m, v, name):
    rows, cols = w.shape
    tr = min(rows, FLAT_TILE)
    assert rows % tr == 0

    def body(w_ref, g_ref, m_ref, v_ref, d_ref, nm_ref, nv_ref):
        gv = g_ref[...]
        nm = ADAM_B1 * m_ref[...] + (1.0 - ADAM_B1) * gv
        nv = ADAM_B2 * v_ref[...] + (1.0 - ADAM_B2) * jnp.square(gv)
        m_hat = nm / (1.0 - ADAM_B1 ** ADAM_STEP)
        v_hat = nv / (1.0 - ADAM_B2 ** ADAM_STEP)
        d_ref[...] = -ADAM_LR * (m_hat / (jnp.sqrt(v_hat) + ADAM_EPS) + ADAM_WD * w_ref[...])
        nm_ref[...] = nm
        nv_ref[...] = nv

    spec = _tok(tr, cols)
    return _pcall(
        body, name=name, grid=(rows // tr,), in_specs=[spec] * 4, out_specs=[spec] * 3,
        out_shape=[_sds((rows, cols), F32)] * 3, sem=("arbitrary",),
    )(w, g, m, v)


def _my_place():
    return lax.axis_index("x"), lax.axis_index("y"), lax.axis_index("c")


def _remote(src, dst, send_sems, recv_sems, k, to):
    return pltpu.make_async_remote_copy(src_ref=src, dst_ref=dst, send_sem=send_sems.at[k], recv_sem=recv_sems.at[k],
                                        device_id=to, device_id_type=MESH_ID)


class _Rider(NamedTuple):
    inputs: list
    out_shapes: list
    sems: list
    start: Callable
    finish: Callable


def _run_rider(rider, name):
    ni, no = len(rider.inputs), len(rider.out_shapes)

    def body(*refs):
        parts = refs[:ni], refs[ni:ni + no], refs[ni + no:]
        rider.start(*parts)
        rider.finish(*parts)

    hbm = pl.BlockSpec(memory_space=pl.ANY)
    return pl.pallas_call(body, name=name, out_shape=rider.out_shapes, in_specs=[hbm] * ni, out_specs=[hbm] * no,
                          scratch_shapes=rider.sems)(*rider.inputs)


def _join_riders(*riders):
    def cut(seq, lens):
        at, out = 0, []
        for n in lens:
            out.append(seq[at:at + n])
            at += n
        return out

    def each(ins, outs, sems):
        return zip(riders, cut(ins, [len(r.inputs) for r in riders]), cut(outs, [len(r.out_shapes) for r in riders]),
                   cut(sems, [len(r.sems) for r in riders]))

    def start(ins, outs, sems):
        for r, i, o, s in each(ins, outs, sems):
            r.start(i, o, s)

    def finish(ins, outs, sems):
        for r, i, o, s in each(ins, outs, sems):
            r.finish(i, o, s)

    return _Rider([a for r in riders for a in r.inputs], [a for r in riders for a in r.out_shapes],
                  [a for r in riders for a in r.sems], start, finish)


def _gather_rider(shards):
    n = len(shards)

    def copies(ins, outs, sems):
        send_sems, recv_sems, local_sems = sems
        x, y, c = _my_place()
        chips = [(1 - x, y), (x, 1 - y), (1 - x, 1 - y)]
        mine = [pltpu.make_async_copy(ins[i], outs[i].at[2 * x + y], local_sems.at[i]) for i in range(n)]

        def copy(i, k, slot, to):
            return _remote(ins[i], outs[i].at[slot], send_sems, recv_sems, 3 * i + k, to)

        sends = [copy(i, k, 2 * x + y, (px, py, c)) for i in range(n) for k, (px, py) in enumerate(chips)]
        lands = [copy(i, k, 2 * px + py, (px, py, c)) for i in range(n) for k, (px, py) in enumerate(chips)]
        return mine, sends, lands

    def start(ins, outs, sems):
        mine, sends, _ = copies(ins, outs, sems)
        for cp in mine + sends:
            cp.start()

    def finish(ins, outs, sems):
        mine, sends, lands = copies(ins, outs, sems)
        for cp in lands:
            cp.wait_recv()
        for cp in sends:
            cp.wait_send()
        for cp in mine:
            cp.wait()

    return _Rider(list(shards), [_sds((4,) + s.shape, s.dtype) for s in shards],
                  [pltpu.SemaphoreType.DMA((3 * n,)), pltpu.SemaphoreType.DMA((3 * n,)), pltpu.SemaphoreType.DMA((n,))],
                  start, finish)


def _scatter_rider(gws, layer, gsmall=None):
    n = len(gws)
    per = n + (gsmall is not None)

    def plan(ins, outs, sems):
        send_sems, recv_sems, local_sems = sems
        x, y, c = _my_place()
        me = 4 * x + 2 * y + c

        def peer(m):
            return (x ^ ((m >> 2) & 1), y ^ ((m >> 1) & 1), c ^ (m & 1))

        def slot_of(m):
            px, py, pc = peer(m)
            return 4 * px + 2 * py + pc

        def piece(i, m, slot):
            px, py, pc = peer(m)
            return _remote(ins[i].at[2 * px + py], outs[i].at[slot], send_sems, recv_sems, (m - 1) * per + i,
                           (px, py, pc))

        def small(m, slot):
            return _remote(ins[n], outs[n].at[slot], send_sems, recv_sems, (m - 1) * per + n, peer(m))

        own = [pltpu.make_async_copy(ins[i].at[2 * x + y], outs[i].at[me], local_sems.at[i]) for i in range(n)]
        return c, me, slot_of, piece, small, own

    def start(ins, outs, sems):
        c, me, slot_of, piece, small, own = plan(ins, outs, sems)

        @pl.when(c == layer)
        def _():
            for cp in own:
                cp.start()
            for m in (2, 4, 6):
                for i in range(n):
                    piece(i, m, me).start()

        @pl.when(c != layer)
        def _():
            for m in (1, 3, 5, 7):
                for i in range(n):
                    piece(i, m, me).start()

        if gsmall is not None:
            pltpu.make_async_copy(ins[n], outs[n].at[me], sems[2].at[n]).start()
            for m in range(1, 8):
                small(m, me).start()

    def finish(ins, outs, sems):
        c, me, slot_of, piece, small, own = plan(ins, outs, sems)

        @pl.when(c == layer)
        def _():
            for m in range(1, 8):
                for i in range(n):
                    piece(i, m, slot_of(m)).wait_recv()
            for m in (2, 4, 6):
                for i in range(n):
                    piece(i, m, me).wait_send()
            for cp in own:
                cp.wait()

        @pl.when(c != layer)
        def _():
            for m in (1, 3, 5, 7):
                for i in range(n):
                    piece(i, m, me).wait_send()

        if gsmall is not None:
            for m in range(1, 8):
                small(m, slot_of(m)).wait_recv()
            for m in range(1, 8):
                small(m, me).wait_send()
            pltpu.make_async_copy(ins[n], outs[n].at[me], sems[2].at[n]).wait()

    inputs = list(gws) + ([gsmall] if gsmall is not None else [])
    shapes = [_sds((8,) + g.shape[1:], g.dtype) for g in gws]
    if gsmall is not None:
        shapes.append(_sds((8,) + gsmall.shape, gsmall.dtype))
    nsem = 7 * per
    return _Rider(inputs, shapes, [pltpu.SemaphoreType.DMA((nsem,)), pltpu.SemaphoreType.DMA((nsem,)),
                                   pltpu.SemaphoreType.DMA((per,))], start, finish)


def _sum_slots(slots, name, other=None):
    _, r, c = slots.shape
    tr = min(r, FLAT_TILE)
    assert r % tr == 0
    out_dtype = F32 if other is None else BF16

    def total(ref):
        acc = ref[0].astype(F32)
        for d in range(1, 8):
            acc = acc + ref[d].astype(F32)
        return acc

    def body(*refs):
        o_ref = refs[-1]
        if other is None:
            o_ref[...] = total(refs[0])
        else:
            core = lax.axis_index("c")

            @pl.when(core == 0)
            def _():
                o_ref[...] = total(refs[0]).astype(out_dtype)

            @pl.when(core == 1)
            def _():
                o_ref[...] = total(refs[1]).astype(out_dtype)

    spec = pl.BlockSpec((8, tr, c), lambda i: (0, i, 0))
    ins = [slots] if other is None else [slots, other]
    return _pcall(
        body, name=name, grid=(r // tr,), in_specs=[spec] * len(ins),
        out_specs=_tok(tr, c), out_shape=_sds((r, c), out_dtype), sem=("arbitrary",),
    )(*ins)


def _swap_rider(reds):
    n = len(reds)

    def copies(ins, outs, sems):
        send_sems, recv_sems, local_sems = sems
        x, y, c = _my_place()
        sibling = (x, y, 1 - c)
        mine = [pltpu.make_async_copy(ins[i], outs[i].at[c], local_sems.at[i]) for i in range(n)]
        sends = [_remote(ins[i], outs[i].at[c], send_sems, recv_sems, i, sibling) for i in range(n)]
        lands = [_remote(ins[i], outs[i].at[1 - c], send_sems, recv_sems, i, sibling) for i in range(n)]
        return mine, sends, lands

    def start(ins, outs, sems):
        mine, sends, _ = copies(ins, outs, sems)
        for cp in mine + sends:
            cp.start()

    def finish(ins, outs, sems):
        mine, sends, lands = copies(ins, outs, sems)
        for cp in lands:
            cp.wait_recv()
        for cp in sends:
            cp.wait_send()
        for cp in mine:
            cp.wait()

    return _Rider(list(reds), [_sds((2,) + r.shape, r.dtype) for r in reds],
                  [pltpu.SemaphoreType.DMA((n,)), pltpu.SemaphoreType.DMA((n,)), pltpu.SemaphoreType.DMA((n,))],
                  start, finish)


W_IN_SHARD, W_IN_SHARD_PAD = 1192, 1280
W_UQ_SHARD, W_UQ_SHARD_PAD = 192, 256


def _swa_place(t):
    z = jnp.zeros_like(t)
    lo = jnp.concatenate([t, z], axis=1)
    hi = jnp.concatenate([z, t], axis=1)
    group = (jnp.arange(NH) // SWA_R).reshape((NH,) + (1,) * (t.ndim - 1))
    full = jnp.where(group == 0, lo, hi)
    return full.reshape((NH * LANE,) + t.shape[2:])


def _swa_unplace(t):
    t = t.reshape((NH, 2, 64) + t.shape[1:])
    return jnp.concatenate([t[:SWA_R, 0], t[SWA_R:, 1]], axis=0)


def _pad_w_o_mla(w):
    return jnp.pad(w.reshape(NH, 64, D), ((0, 0), (0, 64), (0, 0))).reshape(NH * LANE, D)


def _unpad_w_o_mla(g):
    return g.reshape(NH, LANE, D)[:, :64].reshape(NH * 64, D)


def _w_in_cols():
    src = np.full((C_END,), -1, np.int64)
    src[C_CQ:C_KPE] = np.arange(0, 384)
    src[C_KPE + 64:C_KPE + 96] = np.arange(384, 416)
    for h in range(NH):
        at = C_QS + h * LANE + 64 * (h // SWA_R)
        src[at:at + 64] = 416 + h * 64 + np.arange(64)
    src[C_KS:C_END] = np.arange(928, IN_COLS)
    return src


def _w_uq_cols():
    src = np.full((NH * LANE,), -1, np.int64)
    for h in range(NH):
        src[h * LANE:h * LANE + 96] = h * 96 + np.arange(96)
    return src


def _w_ukv_cols():
    src = np.full((2 * NH * LANE,), -1, np.int64)
    for h in range(NH):
        src[h * LANE:h * LANE + 64] = h * 128 + np.arange(64)
        src[NH * LANE + h * LANE:NH * LANE + h * LANE + 64] = h * 128 + 64 + np.arange(64)
    return src


def _selection(src_cols, width, width_pad):
    want = jnp.asarray(np.asarray(src_cols, np.int32))[None, None, :]
    k = jnp.arange(width_pad, dtype=jnp.int32)[None, :, None]
    have = jnp.where(k < width, jnp.arange(4, dtype=jnp.int32)[:, None, None] * width + k, -2)
    return (want == have).astype(BF16)


def _selections():
    return dict(w_in=_selection(_w_in_cols(), W_IN_SHARD, W_IN_SHARD_PAD),
                w_uq=_selection(_w_uq_cols(), W_UQ_SHARD, W_UQ_SHARD_PAD),
                w_ukv=_selection(_w_ukv_cols(), 256, 256))


def _pad_last(a, width):
    return jnp.pad(a, ((0, 0),) * (a.ndim - 1) + ((0, width - a.shape[-1]),))


def _wire_shards(W, l):
    out = {n: W[n][l].astype(BF16) for n in SHARDED}
    out["w_in"] = _pad_last(out["w_in"], W_IN_SHARD_PAD)
    out["w_uq"] = _pad_last(out["w_uq"], W_UQ_SHARD_PAD)
    return [out[n] for n in SHARDED]


def _join_shards(t, axis):
    _, L, r, c = t.shape
    if axis == 2:
        return t.transpose(1, 2, 0, 3).reshape(L, r, 4 * c)
    return t.transpose(1, 0, 2, 3).reshape(L, 4 * r, c)


EARLY = SHARDED[:3]
LATE = SHARDED[3:]


def _kernel_weights_early(gathered, sels):
    lay = {n: _select_fwd(a[:, None], sels[n], "lay_" + n)[0] for n, a in zip(EARLY, gathered)}
    return dict(win=lay["w_in"], wuq=lay["w_uq"], wukv=lay["w_ukv"])


def _kernel_weights_late(gathered):
    whole = {n: _join_shards(a[:, None], SHARD_AXIS[n])[0] for n, a in zip(LATE, gathered)}
    return dict(wmem=whole["w_mem_kv"], wa=_pad_w_o_mla(whole["w_o_mla"]),
                wb=_swa_place(whole["w_o_swa"].reshape(NH, 64, D)), wc=whole["w_o_mem"], wout=whole["w_out"],
                wup=whole["w_up"], wdown=whole["w_down"])


def _kernel_weights(gathered, sels):
    return {**_kernel_weights_early(gathered[:3], sels), **_kernel_weights_late(gathered[3:])}


def _cols_to_shards(g):
    r, c4 = g.shape
    return g.reshape(r, 4, c4 // 4).transpose(1, 0, 2).astype(BF16)


def _rope_tables(S):
    pos = jnp.arange(S, dtype=F32)
    inv = 1.0 / (ROPE_THETA ** (jnp.arange(0, 32, 2, dtype=F32) / 32))
    ang = pos[:, None] * inv[None, :]
    cos, sin = jnp.cos(ang), jnp.sin(ang)
    one, zero = jnp.ones((S, 64), F32), jnp.zeros((S, 16), F32)
    rc = jnp.concatenate([one, cos, cos, jnp.ones((S, 32), F32)], axis=1)
    rs1 = jnp.concatenate([jnp.zeros((S, 64), F32), zero, sin, jnp.zeros((S, 32), F32)], axis=1)
    rs2 = jnp.concatenate([jnp.zeros((S, 64), F32), -sin, zero, jnp.zeros((S, 32), F32)], axis=1)
    return rc, rs1, rs2


def _bucket_map():
    qi = jnp.arange(WIN)[:, None]
    kj = jnp.arange(2 * WIN)[None, :]
    dist = qi + WIN - kj
    n = jnp.maximum(dist, 0)
    max_exact = REL_BUCKETS // 2
    nf = jnp.maximum(n, 1).astype(F32)
    large = max_exact + (jnp.log(nf / max_exact) / math.log(128 / max_exact)
                         * (REL_BUCKETS - max_exact)).astype(jnp.int32)
    large = jnp.minimum(large, REL_BUCKETS - 1)
    bucket = jnp.where(n < max_exact, n, large)
    return jnp.where((dist >= 0) & (dist < WIN), bucket, -1).astype(jnp.int32)


TS = 256
TQ = 1024
TQ_FWD = 1024


def _local_step(x, mem, tgt, kw0, sp, sels, kw1=None, wire=None):
    S = x.shape[0]
    ts = min(TS, S)
    tq = min(TQ, S)
    rc, rs1, rs2 = _rope_tables(S)
    bmap = _bucket_map()
    bias = _bias_build(sp["rel_bias"], bmap)
    row = lambda v: v.reshape(1, -1)

    saved = []
    kw = [dict(kw0), kw1]
    for l in range(DEPTH):
        w = kw[l]
        an, qn, kvn = row(sp["attn_norm"][l]), row(sp["mla_q_norm"][l]), row(sp["mla_kv_norm"][l])
        bg, mnorm, mlpn = row(sp["b_gate"][l]), row(sp["mem_norm"][l]), row(sp["mlp_norm"][l])
        sinks = jnp.broadcast_to(sp["attn_sinks"][l][:, None], (NH, LANE))
        q, k, v, qs, ks, vs, qm, g = _pre_fwd(x, an, w["win"], bg, qn, kvn, w["wuq"], w["wukv"], rc, rs1, rs2, ts)
        carry = _join_riders(_gather_rider(wire[0]), _gather_rider(wire[1])) if (l == 0 and wire) else None
        oa, lse, got = _mla_fwd(q, k, v, min(TQ_FWD, S), carry)
        if carry:
            w.update(_kernel_weights_late(got[:len(LATE)]))
            kw[1] = _kernel_weights(got[len(LATE):], sels)
        ob = _swa_fwd(qs, ks, vs, bias, sinks)
        kvm = _memkv_fwd(mem, mnorm, w["wmem"])
        oc = _mem_fwd(qm, kvm, ts)
        x1, yb = _merge_fwd(x, g, oa, ob, oc, w["wa"], w["wb"], w["wc"], w["wout"], ts)
        x2 = _mlp_fwd(x1, mlpn, w["wup"], w["wdown"], ts)
        saved.append(dict(w=w, x=x, x1=x1, q=q, k=k, v=v, qs=qs, ks=ks, vs=vs, qm=qm, g=g, oa=oa, lse=lse, ob=ob,
                          oc=oc, kvm=kvm, yb=yb, an=an, qn=qn, kvn=kvn, mnorm=mnorm, mlpn=mlpn, sinks=sinks))
        x = x2

    sq, dx, dfn = _loss_kernel(x, row(sp["final_norm"]), tgt, ts)

    big = {n: [None] * DEPTH for n in SHARDED}
    small = {n: [None] * DEPTH for n in SMALL if n not in ("rel_bias", "final_norm")}
    dbias_total = None
    slots1 = None
    for l in reversed(range(DEPTH)):
        sv = saved[l]
        w = sv["w"]
        dx1, hb2, dub, ab, dxb, dmlpn = _mlp_bwd(dx, sv["x1"], sv["mlpn"], w["wup"], w["wdown"], ts)
        big["w_up"][l] = _matmul_tn(hb2, dub, "dw_up", shard_axis=1)
        big["w_down"][l] = _matmul_tn(ab, dxb, "dw_down", shard_axis=0)
        small["mlp_norm"][l] = dmlpn[0]

        dgp, dyo, doa, dob, doc, dla, dx1b, dbg = _merge_bwd(dx1, sv["g"], sv["oa"], sv["ob"], sv["oc"], w["wa"],
                                                             w["wb"], w["wc"], w["wout"], ts)
        big["w_out"][l] = _matmul_tn(sv["yb"], dx1b, "dw_out", shard_axis=0)
        big["w_o_mla"][l] = _cols_to_shards(_unpad_w_o_mla(_matmul_tn(sv["oa"], dyo[:, 0:D], "dw_o_mla")))
        big["w_o_swa"][l] = _cols_to_shards(
            _swa_unplace(_matmul_tn(sv["ob"], dyo[:, D:2 * D], "dw_o_swa")).reshape(NH * 64, D))
        big["w_o_mem"][l] = _matmul_tn(sv["oc"], dyo[:, 2 * D:3 * D], "dw_o_mem", shard_axis=1)
        small["b_gate"][l] = dbg[0]

        dqm, dkvm = _mem_bwd(sv["qm"], sv["kvm"], doc, ts)
        dwmem, dmnorm = _memkv_bwd(mem, sv["mnorm"], w["wmem"], dkvm)
        big["w_mem_kv"][l] = dwmem.reshape(4, D // 4, D).astype(BF16)
        small["mem_norm"][l] = dmnorm[0]

        dqs, dks, dvs, dbias, dsink = _swa_bwd(sv["qs"], sv["ks"], sv["vs"], dob, bias, sv["sinks"])
        dbias_total = dbias if dbias_total is None else dbias_total + dbias
        small["attn_sinks"][l] = dsink[:, 0]

        carry = _join_riders(_scatter_rider([big[n][1] for n in SHARDED], 1),
                             _scatter_rider([big[n][0] for n in LATE], 0)) if (l == 0 and wire) else None
        dq, dk, dv, slots = _mla_bwd(sv["q"], sv["k"], sv["v"], doa, sv["lse"], dla, tq, carry)
        if carry:
            late0, late1 = slots[len(SHARDED):], slots[len(EARLY):len(SHARDED)]
            carry = _swap_rider([_sum_slots(s0, "sum_" + n, s1) for n, s0, s1 in zip(LATE, late0, late1)])

        dx, dproj, hb, cqn, ckvn, dqpre, dkv, dan, dqn, dkvn, swapped = _pre_bwd(
            sv["x"], dx1, dq, dk, dv, dqs, dks, dvs, dqm, dgp, sv["an"], sv["qn"], sv["kvn"], w["win"], w["wuq"],
            w["wukv"], rc, rs1, rs2, ts, carry)
        if carry:
            slots1 = (slots[:len(EARLY)], swapped)
        for n, (a, b) in (("w_in", (hb, dproj)), ("w_uq", (cqn, dqpre)), ("w_ukv", (ckvn, dkv))):
            big[n][l] = _select_bwd(_matmul_tn(a, b, "d" + n)[None], sels[n], "shard_d" + n)[0]
        small["attn_norm"][l] = dan[0]
        small["mla_q_norm"][l] = dqn[0]
        small["mla_kv_norm"][l] = dkvn[0]

    gw = [[big[n][l] for n in SHARDED] for l in range(DEPTH)]
    gs = {n: jnp.stack(v) for n, v in small.items()}
    gs["rel_bias"] = _bias_reduce(dbias_total, bmap)
    gs["final_norm"] = dfn[0]
    return sq, dx, gw, slots1, gs


def _flatten(parts, rows):
    flat = jnp.concatenate([p.reshape(-1) for p in parts])
    return jnp.pad(flat, (0, rows * FLAT_W - flat.shape[0])).reshape(rows, FLAT_W)


def _unflatten(buf, shapes):
    flat = buf.reshape(-1)
    out, at = [], 0
    for s in shapes:
        n = int(np.prod(s))
        out.append(flat[at:at + n].reshape(s))
        at += n
    return out


def kernel(x, mem, rel_bias, attn_norm, mem_norm, w_in, b_gate, mla_q_norm, w_uq, mla_kv_norm, w_ukv, attn_sinks, w_mem_kv, w_o_mla, w_o_swa, w_o_mem, w_out, mlp_norm, w_up, w_down, final_norm, loss_target, m_rel_bias, m_attn_norm, m_mem_norm, m_w_in, m_b_gate, m_mla_q_norm, m_w_uq, m_mla_kv_norm, m_w_ukv, m_attn_sinks, m_w_mem_kv, m_w_o_mla, m_w_o_swa, m_w_o_mem, m_w_out, m_mlp_norm, m_w_up, m_w_down, m_final_norm, v_rel_bias, v_attn_norm, v_mem_norm, v_w_in, v_b_gate, v_mla_q_norm, v_w_uq, v_mla_kv_norm, v_w_ukv, v_attn_sinks, v_w_mem_kv, v_w_o_mla, v_w_o_swa, v_w_o_mem, v_w_out, v_mlp_norm, v_w_up, v_w_down, v_final_norm):
    args = dict(locals())
    W = {n: args[n] for n in WEIGHTS}
    M = {n: args["m_" + n] for n in WEIGHTS}
    V = {n: args["v_" + n] for n in WEIGHTS}
    small_shapes = [W[n].shape for n in SMALL]
    sels = _selections()

    wire0, wire1 = _wire_shards(W, 0), _wire_shards(W, 1)
    kw0 = _kernel_weights_early(_run_rider(_gather_rider(wire0[:len(EARLY)]), "gather_weights"), sels)
    sp = {n: W[n] for n in SMALL}

    sq, dx, gw, carried, gs = _local_step(x[0], mem[0], loss_target[0], kw0, sp, sels,
                                          wire=(wire0[len(EARLY):], wire1))

    slots1_early, swapped_late = carried
    gsmall = _flatten([gs[n] for n in SMALL], SMALL_ROWS)
    *slots0_early, small_slots = _run_rider(_scatter_rider(gw[0][:len(EARLY)], 0, gsmall), "scatter_grads")
    reds = [_sum_slots(s0, "sum_" + n, s1) for n, s0, s1 in zip(EARLY, slots0_early, slots1_early)]
    red_small = _sum_slots(small_slots, "sum_small")
    swapped = list(_run_rider(_swap_rider(reds), "swap_layers")) + list(swapped_late)
    G = {n: g.astype(F32) for n, g in zip(SHARDED, swapped)}
    G["w_in"] = G["w_in"][..., :W_IN_SHARD]
    G["w_uq"] = G["w_uq"][..., :W_UQ_SHARD]

    DW, NM, NV = {}, {}, {}
    for n in SHARDED:
        shape = W[n].shape
        two_d = lambda a: a.reshape(-1, shape[-1])
        d, nm, nv = _adamw(two_d(W[n]), two_d(G[n]), two_d(M[n]), two_d(V[n]), "adamw_" + n)
        DW[n], NM[n], NV[n] = d.reshape(shape), nm.reshape(shape), nv.reshape(shape)
    d_s, m_s, v_s = _adamw(_flatten([W[n] for n in SMALL], SMALL_ROWS), red_small,
                           _flatten([M[n] for n in SMALL], SMALL_ROWS), _flatten([V[n] for n in SMALL], SMALL_ROWS),
                           "adamw_small")
    for out, buf in ((G, red_small), (DW, d_s), (NM, m_s), (NV, v_s)):
        out.update(zip(SMALL, _unflatten(buf, small_shapes)))
    loss = lax.psum(0.5 * sq[0, 0] / D, ("x", "y", "c"))
    return (loss, dx[None], *[G[n] for n in WEIGHTS], *[DW[n] for n in WEIGHTS], *[NM[n] for n in WEIGHTS],
            *[NV[n] for n in WEIGHTS])
```

```python
import functools
import math
from typing import Callable, NamedTuple

import numpy as np
import jax
import jax.numpy as jnp
from jax import lax
from jax.experimental import pallas as pl
from jax.experimental.pallas import tpu as pltpu

F32 = jnp.float32
BF16 = jnp.bfloat16

D = 1024
DFF = 4096
DEPTH = 2
EPS = 1e-6
LANE = 128
NH = 8
SWA_R = 4
MEM_H = 4
MEM_LEN = 256
WIN = 128
NEG = -1e30
MLA_SCALE = 96 ** -0.5
SWA_SCALE = 64 ** -0.5
MEM_SCALE = 128 ** -0.5
REL_BUCKETS = 32
ROPE_THETA = 10000.0

C_CQ, C_CKV, C_KPE, C_QS, C_KS, C_VS, C_QM, C_G, C_END = 0, 256, 384, 512, 1536, 1664, 1792, 2304, 5376
IN_COLS = 4768

ADAM_LR = 0.001
ADAM_B1 = 0.9
ADAM_B2 = 0.999
ADAM_EPS = 1e-08
ADAM_WD = 0.01
ADAM_STEP = 10

VMEM_LIMIT = 56 * 1024 * 1024

SHARDED = ("w_in", "w_uq", "w_ukv", "w_mem_kv", "w_o_mla", "w_o_swa", "w_o_mem", "w_out", "w_up", "w_down")
SHARD_AXIS = {"w_in": 2, "w_uq": 2, "w_ukv": 2, "w_mem_kv": 1, "w_o_mla": 2, "w_o_swa": 2, "w_o_mem": 2,
              "w_out": 1, "w_up": 2, "w_down": 1}
SMALL = ("rel_bias", "attn_norm", "mem_norm", "b_gate", "mla_q_norm", "mla_kv_norm", "attn_sinks", "mlp_norm",
         "final_norm")
WEIGHTS = ("rel_bias", "attn_norm", "mem_norm", "w_in", "b_gate", "mla_q_norm", "w_uq", "mla_kv_norm", "w_ukv",
           "attn_sinks", "w_mem_kv", "w_o_mla", "w_o_swa", "w_o_mem", "w_out", "mlp_norm", "w_up", "w_down",
           "final_norm")
FLAT_W = 1024
FLAT_TILE = 256
SMALL_ROWS = 16
MESH_ID = pl.DeviceIdType.MESH


def _pcall(body, *, name, grid, in_specs, out_specs, out_shape, scratch=(), prefetch=0, sem=None):
    params = pltpu.CompilerParams(dimension_semantics=sem, vmem_limit_bytes=VMEM_LIMIT)
    if prefetch:
        spec = pltpu.PrefetchScalarGridSpec(num_scalar_prefetch=prefetch, grid=grid, in_specs=in_specs,
                                            out_specs=out_specs, scratch_shapes=scratch)
        return pl.pallas_call(body, name=name, grid_spec=spec, out_shape=out_shape, compiler_params=params)
    return pl.pallas_call(body, name=name, grid=grid, in_specs=in_specs, out_specs=out_specs, out_shape=out_shape,
                          scratch_shapes=scratch, compiler_params=params)


def _tok(ts, w):
    return pl.BlockSpec((ts, w), lambda i: (i, 0))


def _full(*shape):
    return pl.BlockSpec(shape, lambda *_: (0,) * len(shape))


def _sds(shape, dtype):
    return jax.ShapeDtypeStruct(shape, dtype)


def _dot(a, b):
    return jnp.dot(a, b, preferred_element_type=F32)


def _dot_nt(a, b):
    return lax.dot_general(a, b, (((1,), (1,)), ((), ())), preferred_element_type=F32)


def _dot_tn(a, b):
    return lax.dot_general(a, b, (((0,), (0,)), ((), ())), preferred_element_type=F32)


def _rms(x):
    r = lax.rsqrt(jnp.mean(x * x, axis=-1, keepdims=True) + EPS)
    return x * r, r


def _rms_bwd(dyg, n, r):
    return r * (dyg - n * jnp.mean(n * dyg, axis=-1, keepdims=True))


def _rope(t, c, s1, s2):
    return t * c + pltpu.roll(t, 16, 1) * s1 + pltpu.roll(t, LANE - 16, 1) * s2


def _rope_bwd(dy, c, s1, s2):
    return dy * c + pltpu.roll(dy * s1, LANE - 16, 1) + pltpu.roll(dy * s2, 16, 1)


def _hs(h):
    return slice(h * LANE, (h + 1) * LANE)


def _colsum(t):
    return jnp.sum(t, axis=0, keepdims=True)


def _pre_fwd(x, an, win, bg, qn, kvn, wuq, wukv, rc, rs1, rs2, ts):
    S = x.shape[0]

    def body(x_ref, an_ref, win_ref, bg_ref, qn_ref, kvn_ref, wuq_ref, wukv_ref, rc_ref, rs1_ref, rs2_ref,
             q_ref, k_ref, v_ref, qs_ref, ks_ref, vs_ref, qm_ref, g_ref):
        n, _ = _rms(x_ref[...])
        hb = (n * an_ref[...]).astype(BF16)
        pa = _dot(hb, win_ref[:, C_CQ:C_QS])
        ncq, _ = _rms(pa[:, 0:256])
        cqn = (ncq * qn_ref[...]).astype(BF16)
        nkv, _ = _rms(pa[:, 256:384])
        ckvn = (nkv * kvn_ref[...]).astype(BF16)
        c, s1, s2 = rc_ref[...], rs1_ref[...], rs2_ref[...]
        kper = _rope(pa[:, 384:512], c, s1, s2)
        qp = _dot(cqn, wuq_ref[...])
        kv = _dot(ckvn, wukv_ref[...])
        for h in range(NH):
            q_ref[:, _hs(h)] = (_rope(qp[:, _hs(h)], c, s1, s2) * MLA_QSCALE).astype(BF16)
            k_ref[:, _hs(h)] = (kv[:, _hs(h)] + kper).astype(BF16)
        v_ref[...] = kv[:, NH * LANE:].astype(BF16)
        pb = _dot(hb, win_ref[:, C_QS:C_G])
        qs_ref[...] = pb[:, 0:1024].astype(BF16)
        ks_ref[...] = pb[:, 1024:1152].astype(BF16)
        vs_ref[...] = pb[:, 1152:1280].astype(BF16)
        qm_ref[...] = pb[:, 1280:1792].astype(BF16)
        g_ref[...] = jax.nn.sigmoid(_dot(hb, win_ref[:, C_G:C_END]) + bg_ref[...])

    return _pcall(
        body, name="pre_fwd", grid=(S // ts,),
        in_specs=[_tok(ts, D), _full(1, D), _full(D, C_END), _full(1, 3 * D), _full(1, 256), _full(1, 128),
                  _full(256, NH * LANE), _full(128, 2 * NH * LANE), _tok(ts, LANE), _tok(ts, LANE), _tok(ts, LANE)],
        out_specs=[_tok(ts, 1024), _tok(ts, 1024), _tok(ts, 1024), _tok(ts, 1024), _tok(ts, 128), _tok(ts, 128),
                   _tok(ts, 512), _tok(ts, 3 * D)],
        out_shape=[_sds((S, 1024), BF16), _sds((S, 1024), BF16), _sds((S, 1024), BF16), _sds((S, 1024), BF16),
                   _sds((S, 128), BF16), _sds((S, 128), BF16), _sds((S, 512), BF16), _sds((S, 3 * D), F32)],
        sem=("arbitrary",),
    )(x, an, win, bg, qn, kvn, wuq, wukv, rc, rs1, rs2)


def _merge_fwd(x, g, oa, ob, oc, wa, wb, wc, wout, ts):
    S = x.shape[0]

    def body(x_ref, g_ref, oa_ref, ob_ref, oc_ref, wa_ref, wb_ref, wc_ref, wout_ref, x1_ref, yb_ref):
        y = g_ref[:, 0:D] * _dot(oa_ref[...], wa_ref[...])
        y = y + g_ref[:, D:2 * D] * _dot(ob_ref[...], wb_ref[...])
        y = y + g_ref[:, 2 * D:3 * D] * _dot(oc_ref[...], wc_ref[...])
        yb = y.astype(BF16)
        yb_ref[...] = yb
        x1_ref[...] = x_ref[...] + _dot(yb, wout_ref[...])

    return _pcall(
        body, name="merge_fwd", grid=(S // ts,),
        in_specs=[_tok(ts, D), _tok(ts, 3 * D), _tok(ts, 1024), _tok(ts, 1024), _tok(ts, 512),
                  _full(1024, D), _full(1024, D), _full(512, D), _full(D, D)],
        out_specs=[_tok(ts, D), _tok(ts, D)],
        out_shape=[_sds((S, D), F32), _sds((S, D), BF16)],
        sem=("arbitrary",),
    )(x, g, oa, ob, oc, wa, wb, wc, wout)


def _mlp_fwd(x1, mn, wup, wdown, ts):
    S = x1.shape[0]

    def body(x_ref, mn_ref, wup_ref, wdown_ref, x2_ref):
        xv = x_ref[...]
        n, _ = _rms(xv)
        u = _dot((n * mn_ref[...]).astype(BF16), wup_ref[...])
        a = jnp.square(jnp.maximum(u, 0.0))
        x2_ref[...] = xv + _dot(a.astype(BF16), wdown_ref[...])

    return _pcall(
        body, name="mlp_fwd", grid=(S // ts,),
        in_specs=[_tok(ts, D), _full(1, D), _full(D, DFF), _full(DFF, D)],
        out_specs=_tok(ts, D), out_shape=_sds((S, D), F32), sem=("arbitrary",),
    )(x1, mn, wup, wdown)


def _loss_kernel(x, fn, tgt, ts):
    S = x.shape[0]

    def body(x_ref, fn_ref, t_ref, loss_ref, dx_ref, dfn_ref):
        @pl.when(pl.program_id(0) == 0)
        def _():
            loss_ref[...] = jnp.zeros_like(loss_ref)
            dfn_ref[...] = jnp.zeros_like(dfn_ref)

        n, r = _rms(x_ref[...])
        err = n * fn_ref[...] - t_ref[...]
        loss_ref[...] += jnp.sum(err * err)
        dy = err * (1.0 / D)
        dfn_ref[...] += _colsum(dy * n)
        dx_ref[...] = _rms_bwd(dy * fn_ref[...], n, r)

    return _pcall(
        body, name="loss_head", grid=(S // ts,),
        in_specs=[_tok(ts, D), _full(1, D), _tok(ts, D)],
        out_specs=[_full(1, LANE), _tok(ts, D), _full(1, D)],
        out_shape=[_sds((1, LANE), F32), _sds((S, D), F32), _sds((1, D), F32)],
        sem=("arbitrary",),
    )(x, fn, tgt)


def _mlp_bwd(dx2, x1, mn, wup, wdown, ts):
    S = x1.shape[0]

    def body(dx_ref, x_ref, mn_ref, wup_ref, wdown_ref, dx1_ref, hb_ref, dub_ref, ab_ref, dxb_ref, dmn_ref):
        @pl.when(pl.program_id(0) == 0)
        def _():
            dmn_ref[...] = jnp.zeros_like(dmn_ref)

        dx = dx_ref[...]
        n, r = _rms(x_ref[...])
        g = mn_ref[...]
        hb = (n * g).astype(BF16)
        hb_ref[...] = hb
        rl = jnp.maximum(_dot(hb, wup_ref[...]), 0.0)
        ab_ref[...] = jnp.square(rl).astype(BF16)
        dxb = dx.astype(BF16)
        dxb_ref[...] = dxb
        dub = (_dot_nt(dxb, wdown_ref[...]) * (2.0 * rl)).astype(BF16)
        dub_ref[...] = dub
        dh = _dot_nt(dub, wup_ref[...])
        dmn_ref[...] += _colsum(dh * n)
        dx1_ref[...] = dx + _rms_bwd(dh * g, n, r)

    return _pcall(
        body, name="mlp_bwd", grid=(S // ts,),
        in_specs=[_tok(ts, D), _tok(ts, D), _full(1, D), _full(D, DFF), _full(DFF, D)],
        out_specs=[_tok(ts, D), _tok(ts, D), _tok(ts, DFF), _tok(ts, DFF), _tok(ts, D), _full(1, D)],
        out_shape=[_sds((S, D), F32), _sds((S, D), BF16), _sds((S, DFF), BF16), _sds((S, DFF), BF16),
                   _sds((S, D), BF16), _sds((1, D), F32)],
        sem=("arbitrary",),
    )(dx2, x1, mn, wup, wdown)


def _merge_bwd(dx1, g, oa, ob, oc, wa, wb, wc, wout, ts):
    S = dx1.shape[0]

    def body(dx_ref, g_ref, oa_ref, ob_ref, oc_ref, wa_ref, wb_ref, wc_ref, wout_ref,
             dgp_ref, dyo_ref, doa_ref, dob_ref, doc_ref, dla_ref, dxb_ref, dbg_ref):
        @pl.when(pl.program_id(0) == 0)
        def _():
            dbg_ref[...] = jnp.zeros_like(dbg_ref)

        dxb = dx_ref[...].astype(BF16)
        dxb_ref[...] = dxb
        dy = _dot_nt(dxb, wout_ref[...])
        branches = ((oa_ref, wa_ref, doa_ref), (ob_ref, wb_ref, dob_ref), (oc_ref, wc_ref, doc_ref))
        for b, (o_ref, w_ref, do_ref) in enumerate(branches):
            cols = slice(b * D, (b + 1) * D)
            gb = g_ref[:, cols]
            o = o_ref[...]
            dgpre = dy * _dot(o, w_ref[...]) * gb * (1.0 - gb)
            dgp_ref[:, cols] = dgpre.astype(BF16)
            dbg_ref[:, cols] += _colsum(dgpre)
            dyo = (dy * gb).astype(BF16)
            dyo_ref[:, cols] = dyo
            do = _dot_nt(dyo, w_ref[...])
            do_ref[...] = do.astype(BF16)
            if b == 0:
                lane = lax.broadcasted_iota(jnp.int32, (ts, LANE), 1)
                dls = jnp.zeros((ts, LANE), F32)
                for h in range(NH):
                    dl = jnp.sum(do[:, _hs(h)] * o[:, _hs(h)].astype(F32), axis=1, keepdims=True)
                    dls = jnp.where(lane == h, dl, dls)
                dla_ref[:, 0, :] = jnp.transpose(dls)[0:NH, :]

    return _pcall(
        body, name="merge_bwd", grid=(S // ts,),
        in_specs=[_tok(ts, D), _tok(ts, 3 * D), _tok(ts, 1024), _tok(ts, 1024), _tok(ts, 512),
                  _full(1024, D), _full(1024, D), _full(512, D), _full(D, D)],
        out_specs=[_tok(ts, 3 * D), _tok(ts, 3 * D), _tok(ts, 1024), _tok(ts, 1024), _tok(ts, 512),
                   pl.BlockSpec((NH, 1, ts), lambda i: (0, 0, i)), _tok(ts, D), _full(1, 3 * D)],
        out_shape=[_sds((S, 3 * D), BF16), _sds((S, 3 * D), BF16), _sds((S, 1024), BF16), _sds((S, 1024), BF16),
                   _sds((S, 512), BF16), _sds((NH, 1, S), F32), _sds((S, D), BF16), _sds((1, 3 * D), F32)],
        sem=("arbitrary",),
    )(dx1, g, oa, ob, oc, wa, wb, wc, wout)


def _pre_bwd(x, dx1, dq, dk, dv, dqs, dks, dvs, dqm, dgp, an, qn, kvn, win, wuq, wukv, rc, rs1, rs2, ts, ride=None):
    S = x.shape[0]

    def body(x_ref, dx1_ref, dq_ref, dk_ref, dv_ref, dqs_ref, dks_ref, dvs_ref, dqm_ref, dgp_ref,
             an_ref, qn_ref, kvn_ref, win_ref, wuq_ref, wukv_ref, rc_ref, rs1_ref, rs2_ref, *rest):
        own, _, riding = _ride_refs(ride, rest, 10, 0)
        dx_ref, dproj_ref, hb_ref, cqn_ref, ckvn_ref, dqpre_ref, dkv_ref, dan_ref, dqn_ref, dkvn_ref = own

        @pl.when(pl.program_id(0) == 0)
        def _():
            dan_ref[...] = jnp.zeros_like(dan_ref)
            dqn_ref[...] = jnp.zeros_like(dqn_ref)
            dkvn_ref[...] = jnp.zeros_like(dkvn_ref)
            if ride:
                ride.start(*riding)

        n, r = _rms(x_ref[...])
        hb = (n * an_ref[...]).astype(BF16)
        hb_ref[...] = hb
        pa = _dot(hb, win_ref[:, C_CQ:C_KPE])
        ncq, rq = _rms(pa[:, 0:256])
        cqn_ref[...] = (ncq * qn_ref[...]).astype(BF16)
        nkv, rkv = _rms(pa[:, 256:384])
        ckvn_ref[...] = (nkv * kvn_ref[...]).astype(BF16)
        c, s1, s2 = rc_ref[...], rs1_ref[...], rs2_ref[...]

        dkper = jnp.zeros((ts, LANE), F32)
        for h in range(NH):
            dqpre_ref[:, _hs(h)] = _rope_bwd(dq_ref[:, _hs(h)], c, s1, s2).astype(BF16)
            dkh = dk_ref[:, _hs(h)]
            dkper = dkper + dkh
            dkv_ref[:, _hs(h)] = dkh.astype(BF16)
        dkv_ref[:, NH * LANE:] = dv_ref[...].astype(BF16)

        dcqn = _dot_nt(dqpre_ref[...], wuq_ref[...])
        dqn_ref[...] += _colsum(dcqn * ncq)
        dproj_ref[:, C_CQ:C_CKV] = _rms_bwd(dcqn * qn_ref[...], ncq, rq).astype(BF16)
        dckvn = _dot_nt(dkv_ref[...], wukv_ref[...])
        dkvn_ref[...] += _colsum(dckvn * nkv)
        dproj_ref[:, C_CKV:C_KPE] = _rms_bwd(dckvn * kvn_ref[...], nkv, rkv).astype(BF16)
        lane = lax.broadcasted_iota(jnp.int32, (ts, LANE), 1)
        dkpe = jnp.where((lane >= 64) & (lane < 96), _rope_bwd(dkper, c, s1, s2), 0.0)
        dproj_ref[:, C_KPE:C_QS] = dkpe.astype(BF16)
        dproj_ref[:, C_QS:C_KS] = dqs_ref[...]
        dproj_ref[:, C_KS:C_VS] = dks_ref[...].astype(BF16)
        dproj_ref[:, C_VS:C_QM] = dvs_ref[...].astype(BF16)
        dproj_ref[:, C_QM:C_G] = dqm_ref[...]
        dproj_ref[:, C_G:C_END] = dgp_ref[...]

        dh = _dot_nt(dproj_ref[...], win_ref[...])
        dan_ref[...] += _colsum(dh * n)
        dx_ref[...] = dx1_ref[...] + _rms_bwd(dh * an_ref[...], n, r)

        if ride:
            @pl.when(pl.program_id(0) == S // ts - 1)
            def _():
                ride.finish(*riding)

    r_in, r_out, r_shapes, r_sems, r_args = _ride_specs(ride)
    results = _pcall(
        body, name="pre_bwd", grid=(S // ts,),
        in_specs=[_tok(ts, D), _tok(ts, D), _tok(ts, 1024), _tok(ts, 1024), _tok(ts, 1024), _tok(ts, 1024),
                  _tok(ts, 128), _tok(ts, 128), _tok(ts, 512), _tok(ts, 3 * D),
                  _full(1, D), _full(1, 256), _full(1, 128), _full(D, C_END), _full(256, NH * LANE),
                  _full(128, 2 * NH * LANE), _tok(ts, LANE), _tok(ts, LANE), _tok(ts, LANE)] + r_in,
        out_specs=[_tok(ts, D), _tok(ts, C_END), _tok(ts, D), _tok(ts, 256), _tok(ts, 128), _tok(ts, 1024),
                   _tok(ts, 2048), _full(1, D), _full(1, 256), _full(1, 128)] + r_out,
        out_shape=[_sds((S, D), F32), _sds((S, C_END), BF16), _sds((S, D), BF16), _sds((S, 256), BF16),
                   _sds((S, 128), BF16), _sds((S, 1024), BF16), _sds((S, 2048), BF16), _sds((1, D), F32),
                   _sds((1, 256), F32), _sds((1, 128), F32)] + r_shapes,
        scratch=r_sems, sem=("arbitrary",),
    )(x, dx1, dq, dk, dv, dqs, dks, dvs, dqm, dgp, an, qn, kvn, win, wuq, wukv, rc, rs1, rs2, *r_args)
    return (*results[:10], list(results[10:]))


def _pick_tile(n, cap):
    best = LANE
    for t in range(LANE, min(n, cap) + 1, LANE):
        if n % t == 0:
            best = t
    return best


def _matmul_tn(a, b, name, shard_axis=None):
    S, M = a.shape
    N = b.shape[1]
    tm = _pick_tile(M // 4 if shard_axis == 0 else M, 1024)
    tn = _pick_tile(N // 4 if shard_axis == 1 else N, 2048)
    ts = min(S, 1024)
    nk = S // ts

    def body(a_ref, b_ref, o_ref, *acc):
        acc_ref = acc[0] if acc else o_ref

        @pl.when(pl.program_id(2) == 0)
        def _():
            acc_ref[...] = jnp.zeros_like(acc_ref)

        acc_ref[...] += _dot_tn(a_ref[...], b_ref[...])
        if acc:
            @pl.when(pl.program_id(2) == nk - 1)
            def _():
                o_ref[0] = acc_ref[...].astype(o_ref.dtype)

    if shard_axis is None:
        out_spec = pl.BlockSpec((tm, tn), lambda i, j, k: (i, j))
        out_shape, scratch = _sds((M, N), F32), ()
    elif shard_axis == 0:
        per = (M // 4) // tm
        out_spec = pl.BlockSpec((1, tm, tn), lambda i, j, k: (i // per, i % per, j))
        out_shape, scratch = _sds((4, M // 4, N), BF16), (pltpu.VMEM((tm, tn), F32),)
    else:
        per = (N // 4) // tn
        out_spec = pl.BlockSpec((1, tm, tn), lambda i, j, k: (j // per, i, j % per))
        out_shape, scratch = _sds((4, M, N // 4), BF16), (pltpu.VMEM((tm, tn), F32),)
    return _pcall(
        body, name=name, grid=(M // tm, N // tn, nk),
        in_specs=[pl.BlockSpec((ts, tm), lambda i, j, k: (k, i)), pl.BlockSpec((ts, tn), lambda i, j, k: (k, j))],
        out_specs=out_spec, out_shape=out_shape, scratch=scratch, sem=("parallel", "parallel", "arbitrary"),
    )(a, b)


def _select_fwd(a, sel, name):
    _, L, M, K = a.shape
    N = sel.shape[2]
    tn = _pick_tile(N, 1792)

    def body(a_ref, s_ref, o_ref, acc_ref):
        s = pl.program_id(2)

        @pl.when(s == 0)
        def _():
            acc_ref[...] = jnp.zeros_like(acc_ref)

        acc_ref[...] += _dot(a_ref[0, 0], s_ref[0])

        @pl.when(s == 3)
        def _():
            o_ref[0] = acc_ref[...].astype(BF16)

    return _pcall(
        body, name=name, grid=(L, N // tn, 4),
        in_specs=[pl.BlockSpec((1, 1, M, K), lambda l, j, s: (s, l, 0, 0)),
                  pl.BlockSpec((1, K, tn), lambda l, j, s: (s, 0, j))],
        out_specs=pl.BlockSpec((1, M, tn), lambda l, j, s: (l, 0, j)),
        out_shape=_sds((L, M, N), BF16), scratch=(pltpu.VMEM((M, tn), F32),),
        sem=("parallel", "parallel", "arbitrary"),
    )(a, sel)


def _select_bwd(dw, sel, name):
    L, M, N = dw.shape
    K = sel.shape[1]
    tk = _pick_tile(N, 1792)
    nk = N // tk

    def body(d_ref, s_ref, o_ref, acc_ref):
        k = pl.program_id(2)

        @pl.when(k == 0)
        def _():
            acc_ref[...] = jnp.zeros_like(acc_ref)

        acc_ref[...] += _dot_nt(d_ref[0].astype(BF16), s_ref[0])

        @pl.when(k == nk - 1)
        def _():
            o_ref[0, 0] = acc_ref[...].astype(BF16)

    return _pcall(
        body, name=name, grid=(L, 4, nk),
        in_specs=[pl.BlockSpec((1, M, tk), lambda l, s, k: (l, 0, k)),
                  pl.BlockSpec((1, K, tk), lambda l, s, k: (s, 0, k))],
        out_specs=pl.BlockSpec((1, 1, M, K), lambda l, s, k: (l, s, 0, 0)),
        out_shape=_sds((L, 4, M, K), BF16), scratch=(pltpu.VMEM((M, K), F32),),
        sem=("parallel", "parallel", "arbitrary"),
    )(dw, sel)


MLA_RC = 128
MLA_RC_FWD = 128
MLA_AHEAD = 4
MLA_HEADS_FWD = 2
LOG2E = math.log2(math.e)
MLA_QSCALE = MLA_SCALE * LOG2E


def _ride_refs(ride, rest, n_out, n_scratch):
    ni, no = (len(ride.inputs), len(ride.out_shapes)) if ride else (0, 0)
    own_out = rest[ni:ni + n_out]
    own_scratch = rest[ni + n_out + no:ni + n_out + no + n_scratch]
    parts = rest[:ni], rest[ni + n_out:ni + n_out + no], rest[ni + n_out + no + n_scratch:]
    return own_out, own_scratch, parts


def _ride_specs(ride):
    hbm = pl.BlockSpec(memory_space=pl.ANY)
    if not ride:
        return [], [], [], [], []
    return ([hbm] * len(ride.inputs), [hbm] * len(ride.out_shapes), list(ride.out_shapes), list(ride.sems),
            list(ride.inputs))


def _mla_fwd(q, k, v, tq, ride=None):
    S = q.shape[0]
    nq = S // tq
    hp = MLA_HEADS_FWD
    pairs = [(i, j) for i in range(nq) for j in range(i + 1)]
    qi = jnp.asarray(np.array([p[0] for p in pairs], np.int32))
    kj = jnp.asarray(np.array([p[1] for p in pairs], np.int32))

    def body(qi_ref, kj_ref, q_ref, k_ref, v_ref, *rest):
        (o_ref, lse_ref), (m_s, l_s, acc_s), riding = _ride_refs(ride, rest, 2, 3)
        t = pl.program_id(1)
        i, j = qi_ref[t], kj_ref[t]
        if ride:
            @pl.when((pl.program_id(0) == 0) & (t == 0))
            def _():
                ride.start(*riding)

        @pl.when(j == 0)
        def _():
            m_s[...] = jnp.full_like(m_s, NEG)
            l_s[...] = jnp.zeros_like(l_s)
            acc_s[...] = jnp.zeros_like(acc_s)

        def step(masked):
            rc = min(MLA_RC_FWD, tq)
            nc = tq // rc
            keys = [(c + 1) * rc if masked else tq for c in range(nc)]
            units = [(hh, c) for hh in range(hp) for c in range(nc)]
            ahead = min(MLA_AHEAD, len(units))

            def qk(u):
                hh, c = units[u]
                return _dot_nt(q_ref[c * rc:(c + 1) * rc, _hs(hh)], k_ref[0:keys[c], _hs(hh)])

            scores = [qk(u) for u in range(ahead)]
            for u, (hh, c) in enumerate(units):
                rows = slice(c * rc, (c + 1) * rc)
                s = scores[u]
                if masked:
                    row = lax.broadcasted_iota(jnp.int32, (rc, keys[c]), 0) + c * rc
                    col = lax.broadcasted_iota(jnp.int32, (rc, keys[c]), 1)
                    s = jnp.where(col <= row, s, NEG)
                tiles = [s[:, _hs(w)] for w in range(keys[c] // LANE)]
                mx = functools.reduce(jnp.maximum, tiles)
                m_old = m_s[rows, _hs(hh)]
                m_new = jnp.maximum(m_old, jnp.max(mx, axis=1, keepdims=True))
                alpha = jnp.exp2(m_old - m_new)
                ps = [jnp.exp2(w - m_new) for w in tiles]
                l_s[rows, _hs(hh)] = alpha * l_s[rows, _hs(hh)] + functools.reduce(jnp.add, ps)
                p = jnp.concatenate([w.astype(BF16) for w in ps], axis=1)
                acc_s[rows, _hs(hh)] = alpha * acc_s[rows, _hs(hh)] + _dot(p, v_ref[0:keys[c], _hs(hh)])
                m_s[rows, _hs(hh)] = m_new
                if u + ahead < len(units):
                    scores.append(qk(u + ahead))

        @pl.when(j < i)
        def _():
            step(False)

        @pl.when(j == i)
        def _():
            step(True)
            for hh in range(hp):
                l = jnp.sum(l_s[:, _hs(hh)], axis=1, keepdims=True)
                o_ref[:, _hs(hh)] = (acc_s[:, _hs(hh)] / l).astype(BF16)
                lse_ref[hh] = jnp.transpose(m_s[:, _hs(hh)] + jnp.log2(l))[0:1, :]

        if ride:
            @pl.when((pl.program_id(0) == NH // hp - 1) & (t == len(pairs) - 1))
            def _():
                ride.finish(*riding)

    qmap = lambda h, t, qi_r, kj_r: (qi_r[t], h)
    kmap = lambda h, t, qi_r, kj_r: (kj_r[t], h)
    r_in, r_out, r_shapes, r_sems, r_args = _ride_specs(ride)
    wide = hp * LANE
    o, lse, *carried = _pcall(
        body, name="mla_fwd", grid=(NH // hp, len(pairs)), prefetch=2,
        in_specs=[pl.BlockSpec((tq, wide), qmap), pl.BlockSpec((tq, wide), kmap), pl.BlockSpec((tq, wide), kmap)]
        + r_in,
        out_specs=[pl.BlockSpec((tq, wide), qmap),
                   pl.BlockSpec((hp, 1, tq), lambda h, t, qi_r, kj_r: (h, 0, qi_r[t]))] + r_out,
        out_shape=[_sds((S, NH * LANE), BF16), _sds((NH, 1, S), F32)] + r_shapes,
        scratch=[pltpu.VMEM((tq, wide), F32), pltpu.VMEM((tq, wide), F32), pltpu.VMEM((tq, wide), F32)] + r_sems,
        sem=("arbitrary", "arbitrary"),
    )(qi, kj, q, k, v, *r_args)
    return o, lse, carried


def _mla_bwd(q, k, v, do, lse, delta, tq, ride=None):
    S = q.shape[0]
    nq = S // tq
    pairs = [(i, j) for j in range(nq) for i in range(j, nq)]
    qi = jnp.asarray(np.array([p[0] for p in pairs], np.int32))
    kj = jnp.asarray(np.array([p[1] for p in pairs], np.int32))

    def body(qi_ref, kj_ref, q_ref, k_ref, v_ref, do_ref, lse_ref, dl_ref, *rest):
        (dq_ref, dk_ref, dv_ref), (dq_s, dk_s, dv_s), riding = _ride_refs(ride, rest, 3, 3)
        t = pl.program_id(1)
        i, j = qi_ref[t], kj_ref[t]
        if ride:
            @pl.when((pl.program_id(0) == 0) & (t == 0))
            def _():
                ride.start(*riding)

        @pl.when(t == 0)
        def _():
            dq_s[...] = jnp.zeros_like(dq_s)

        @pl.when(i == j)
        def _():
            dk_s[...] = jnp.zeros_like(dk_s)
            dv_s[...] = jnp.zeros_like(dv_s)

        qrows = pl.ds(pl.multiple_of(i * tq, tq), tq)

        def step(masked):
            lse_r, dl_r = lse_ref[0], dl_ref[0]
            dq = jnp.zeros((tq, LANE), F32)
            nc = tq // MLA_RC
            q0 = [c * MLA_RC if masked else 0 for c in range(nc)]
            sts = [_dot_nt(k_ref[c * MLA_RC:(c + 1) * MLA_RC, :], q_ref[q0[c]:, :]) for c in range(nc)]
            dpts = [_dot_nt(v_ref[c * MLA_RC:(c + 1) * MLA_RC, :], do_ref[q0[c]:, :]) for c in range(nc)]
            for c in range(nc):
                rows = slice(c * MLA_RC, (c + 1) * MLA_RC)
                qb, dob = q_ref[q0[c]:, :], do_ref[q0[c]:, :]
                pt = jnp.exp2(sts[c] - lse_r[:, q0[c]:])
                if masked:
                    key = lax.broadcasted_iota(jnp.int32, (MLA_RC, tq - q0[c]), 0)
                    qry = lax.broadcasted_iota(jnp.int32, (MLA_RC, tq - q0[c]), 1)
                    pt = jnp.where(key <= qry, pt, 0.0)
                dv_s[rows, :] += _dot(pt.astype(BF16), dob)
                gt = (pt * (dpts[c] - dl_r[:, q0[c]:])).astype(BF16)
                dk_s[rows, :] += _dot(gt, qb)
                part = _dot_tn(gt, k_ref[rows, :])
                if masked:
                    at = pl.multiple_of(i * tq + q0[c], MLA_RC)
                    dq_s[pl.ds(at, tq - q0[c]), :] += part
                else:
                    dq = dq + part
            if not masked:
                dq_s[qrows, :] += dq

        @pl.when(i > j)
        def _():
            step(False)

        @pl.when(i == j)
        def _():
            step(True)
            dq_ref[...] = dq_s[qrows, :] * MLA_SCALE

        @pl.when(i == nq - 1)
        def _():
            dk_ref[...] = dk_s[...] * (1.0 / LOG2E)
            dv_ref[...] = dv_s[...]

        if ride:
            @pl.when((pl.program_id(0) == NH - 1) & (t == len(pairs) - 1))
            def _():
                ride.finish(*riding)

    qmap = lambda h, t, qi_r, kj_r: (qi_r[t], h)
    kmap = lambda h, t, qi_r, kj_r: (kj_r[t], h)
    rmap = lambda h, t, qi_r, kj_r: (h, 0, qi_r[t])
    r_in, r_out, r_shapes, r_sems, r_args = _ride_specs(ride)
    dq, dk, dv, *carried = _pcall(
        body, name="mla_bwd", grid=(NH, len(pairs)), prefetch=2,
        in_specs=[pl.BlockSpec((tq, LANE), qmap), pl.BlockSpec((tq, LANE), kmap), pl.BlockSpec((tq, LANE), kmap),
                  pl.BlockSpec((tq, LANE), qmap), pl.BlockSpec((1, 1, tq), rmap), pl.BlockSpec((1, 1, tq), rmap)]
        + r_in,
        out_specs=[pl.BlockSpec((tq, LANE), kmap), pl.BlockSpec((tq, LANE), kmap), pl.BlockSpec((tq, LANE), kmap)]
        + r_out,
        out_shape=[_sds((S, NH * LANE), F32), _sds((S, NH * LANE), F32), _sds((S, NH * LANE), F32)] + r_shapes,
        scratch=[pltpu.VMEM((S, LANE), F32), pltpu.VMEM((tq, LANE), F32), pltpu.VMEM((tq, LANE), F32)] + r_sems,
        sem=("arbitrary", "arbitrary"),
    )(qi, kj, q, k, v, do, lse, delta, *r_args)
    return dq, dk, dv, carried


SWA_SUB = 4
SWA_T = SWA_SUB * WIN


def _swa_specs(nsteps, rev):
    step = (lambda i: nsteps - 1 - i) if rev else (lambda i: i)
    cur = lambda w: pl.BlockSpec((SWA_T, w), lambda i: (step(i), 0))
    prev = pl.BlockSpec((WIN, LANE), lambda i: (jnp.maximum(step(i) * SWA_SUB - 1, 0), 0))
    return step, cur, prev


def _swa_probs(qk, bias_h, sink, first_mask):
    s = qk * SWA_SCALE + bias_h
    if first_mask is not None:
        s = jnp.where(first_mask, NEG, s)
    m = jnp.maximum(jnp.max(s, axis=1, keepdims=True), sink)
    e = jnp.exp(s - m)
    es = jnp.exp(sink - m)
    inv = 1.0 / (jnp.sum(e, axis=1, keepdims=True) + es)
    return e * inv, es * inv


SWA_GR = SWA_R * WIN


def _swa_group(ref, rows, g):
    return jnp.concatenate([ref[rows, _hs(g * SWA_R + r)] for r in range(SWA_R)], axis=0)


def _swa_rows(bias, sinks):
    sink_rows = jnp.broadcast_to(sinks[:, None, :], (NH, WIN, LANE)).reshape(NH * WIN, LANE)
    return bias.reshape(NH * WIN, 2 * WIN), sink_rows


def _swa_fwd(qs, ks, vs, bias, sinks):
    S = qs.shape[0]
    nsteps = S // SWA_T
    step, cur, prev = _swa_specs(nsteps, False)

    def body(qs_ref, kc_ref, kp_ref, vc_ref, vp_ref, bias_ref, sk_ref, o_ref):
        first = pl.program_id(0) == 0
        kk = jnp.concatenate([kp_ref[...], kc_ref[...]], axis=0)
        vv = jnp.concatenate([vp_ref[...], vc_ref[...]], axis=0)
        col = lax.broadcasted_iota(jnp.int32, (WIN, 2 * WIN), 1)
        qk = lambda b: [_dot_nt(qs_ref[b * WIN:(b + 1) * WIN, _hs(h)], kk[b * WIN:(b + 2) * WIN]) for h in range(NH)]
        ahead = qk(0)
        for b in range(SWA_SUB):
            vvb = vv[b * WIN:(b + 2) * WIN]
            fm = (first & (col < WIN)) if b == 0 else None
            rows = slice(b * WIN, (b + 1) * WIN)
            qks, ahead = ahead, (qk(b + 1) if b + 1 < SWA_SUB else None)
            for h in range(NH):
                p, _ = _swa_probs(qks[h], bias_ref[h], sk_ref[h:h + 1, 0:1], fm)
                o_ref[rows, _hs(h)] = _dot(p.astype(BF16), vvb).astype(BF16)

    return _pcall(
        body, name="swa_fwd", grid=(nsteps,),
        in_specs=[cur(NH * LANE), cur(LANE), prev, cur(LANE), prev, _full(NH, WIN, 2 * WIN), _full(NH, LANE)],
        out_specs=cur(NH * LANE), out_shape=_sds((S, NH * LANE), BF16), sem=("arbitrary",),
    )(qs, ks, ks, vs, vs, bias, sinks)


def _swa_bwd(qs, ks, vs, do, bias, sinks):
    S = qs.shape[0]
    nsteps = S // SWA_T
    step, cur, prev = _swa_specs(nsteps, True)
    bias_rows, sink_rows = _swa_rows(bias, sinks)

    def body(qs_ref, kc_ref, kp_ref, vc_ref, vp_ref, do_ref, bias_ref, sk_ref,
             dqs_ref, dks_ref, dvs_ref, dbias_ref, dsk_ref, dkk_s, dvv_s, ck_s, cv_s):
        pid = pl.program_id(0)
        first = step(pid) == 0

        @pl.when(pid == 0)
        def _():
            dbias_ref[...] = jnp.zeros_like(dbias_ref)
            dsk_ref[...] = jnp.zeros_like(dsk_ref)
            ck_s[...] = jnp.zeros_like(ck_s)
            cv_s[...] = jnp.zeros_like(cv_s)

        dkk_s[...] = jnp.zeros_like(dkk_s)
        dvv_s[...] = jnp.zeros_like(dvv_s)
        kk = jnp.concatenate([kp_ref[...], kc_ref[...]], axis=0)
        vv = jnp.concatenate([vp_ref[...], vc_ref[...]], axis=0)
        col = lax.broadcasted_iota(jnp.int32, (SWA_GR, 2 * WIN), 1)

        def products(b):
            rows = slice(b * WIN, (b + 1) * WIN)
            qg = [_swa_group(qs_ref, rows, g) for g in range(2)]
            dog = [_swa_group(do_ref, rows, g) for g in range(2)]
            return (qg, dog, [_dot_nt(qg[g], kk[b * WIN:(b + 2) * WIN]) for g in range(2)],
                    [_dot_nt(dog[g], vv[b * WIN:(b + 2) * WIN]) for g in range(2)])

        ahead = products(0)
        for b in range(SWA_SUB):
            kkb = kk[b * WIN:(b + 2) * WIN]
            fm = (first & (col < WIN)) if b == 0 else None
            rows = slice(b * WIN, (b + 1) * WIN)
            keys = slice(b * WIN, (b + 2) * WIN)
            (qg, dog, qks, dps), ahead = ahead, (products(b + 1) if b + 1 < SWA_SUB else None)
            for g in range(2):
                grows = slice(g * SWA_GR, (g + 1) * SWA_GR)
                p, ps = _swa_probs(qks[g], bias_ref[grows, :], sk_ref[grows, 0:1], fm)
                dl = jnp.sum(p * dps[g], axis=1, keepdims=True)
                ds = p * (dps[g] - dl)
                sink_part = ps * dl
                for r in range(SWA_R):
                    h = g * SWA_R + r
                    dsk_ref[h:h + 1, :] += -jnp.sum(sink_part[r * WIN:(r + 1) * WIN])
                dbias_ref[grows, :] += ds
                dsb = (ds * SWA_SCALE).astype(BF16)
                dq = _dot(dsb, kkb).astype(BF16)
                for r in range(SWA_R):
                    dqs_ref[rows, _hs(g * SWA_R + r)] = dq[r * WIN:(r + 1) * WIN]
                dkk_s[keys, :] += _dot_tn(dsb, qg[g])
                dvv_s[keys, :] += _dot_tn(p.astype(BF16), dog[g])
        dks_ref[...] = dkk_s[WIN:, :]
        dvs_ref[...] = dvv_s[WIN:, :]
        dks_ref[SWA_T - WIN:, :] += ck_s[...]
        dvs_ref[SWA_T - WIN:, :] += cv_s[...]
        ck_s[...] = dkk_s[0:WIN, :]
        cv_s[...] = dvv_s[0:WIN, :]

    dqs, dks, dvs, dbias, dsink = _pcall(
        body, name="swa_bwd", grid=(nsteps,),
        in_specs=[cur(NH * LANE), cur(LANE), prev, cur(LANE), prev, cur(NH * LANE), _full(NH * WIN, 2 * WIN),
                  _full(NH * WIN, LANE)],
        out_specs=[cur(NH * LANE), cur(LANE), cur(LANE), _full(NH * WIN, 2 * WIN), _full(NH, LANE)],
        out_shape=[_sds((S, NH * LANE), BF16), _sds((S, LANE), F32), _sds((S, LANE), F32),
                   _sds((NH * WIN, 2 * WIN), F32), _sds((NH, LANE), F32)],
        scratch=[pltpu.VMEM((SWA_T + WIN, LANE), F32), pltpu.VMEM((SWA_T + WIN, LANE), F32),
                 pltpu.VMEM((WIN, LANE), F32), pltpu.VMEM((WIN, LANE), F32)],
        sem=("arbitrary",),
    )(qs, ks, ks, vs, vs, do, bias_rows, sink_rows)
    return dqs, dks, dvs, dbias.reshape(NH, WIN, 2 * WIN), dsink


def _bias_build(rel_bias, bmap):
    def body(rb_ref, bmap_ref, o_ref):
        bm = bmap_ref[...]
        for h in range(NH):
            acc = jnp.full((WIN, 2 * WIN), NEG, F32)
            for b in range(REL_BUCKETS):
                acc = jnp.where(bm == b, rb_ref[b, h], acc)
            o_ref[h] = acc

    return _pcall(
        body, name="bias_build", grid=(1,),
        in_specs=[pl.BlockSpec(memory_space=pltpu.SMEM), _full(WIN, 2 * WIN)],
        out_specs=_full(NH, WIN, 2 * WIN), out_shape=_sds((NH, WIN, 2 * WIN), F32), sem=("arbitrary",),
    )(rel_bias, bmap)


def _bias_reduce(dbias, bmap):
    def body(db_ref, bmap_ref, o_ref):
        bm = bmap_ref[...]
        for h in range(NH):
            dbh = db_ref[h]
            for b in range(REL_BUCKETS):
                o_ref[b, h] = jnp.sum(jnp.where(bm == b, dbh, 0.0))

    return _pcall(
        body, name="bias_reduce", grid=(1,),
        in_specs=[_full(NH, WIN, 2 * WIN), _full(WIN, 2 * WIN)],
        out_specs=pl.BlockSpec(memory_space=pltpu.SMEM), out_shape=_sds((REL_BUCKETS, NH), F32), sem=("arbitrary",),
    )(dbias, bmap)


def _memkv_fwd(mem, mnorm, wkv):
    def body(mem_ref, g_ref, w_ref, o_ref):
        n, _ = _rms(mem_ref[...])
        o_ref[...] = _dot((n * g_ref[...]).astype(BF16), w_ref[...]).astype(BF16)

    return _pcall(
        body, name="memkv_fwd", grid=(1,), in_specs=[_full(MEM_LEN, D), _full(1, D), _full(D, D)],
        out_specs=_full(MEM_LEN, D), out_shape=_sds((MEM_LEN, D), BF16), sem=("arbitrary",),
    )(mem, mnorm, wkv)


def _mem_probs(qk):
    s = qk * MEM_SCALE
    e = jnp.exp(s - jnp.max(s, axis=1, keepdims=True))
    return e / jnp.sum(e, axis=1, keepdims=True)


def _mem_fwd(qm, kvm, ts):
    S = qm.shape[0]

    def body(q_ref, kv_ref, o_ref):
        qks = [_dot_nt(q_ref[:, _hs(h)], kv_ref[:, _hs(h)]) for h in range(MEM_H)]
        for h in range(MEM_H):
            p = _mem_probs(qks[h])
            o_ref[:, _hs(h)] = _dot(p.astype(BF16), kv_ref[:, _hs(MEM_H + h)]).astype(BF16)

    return _pcall(
        body, name="mem_fwd", grid=(S // ts,), in_specs=[_tok(ts, 512), _full(MEM_LEN, D)],
        out_specs=_tok(ts, 512), out_shape=_sds((S, 512), BF16), sem=("arbitrary",),
    )(qm, kvm)


def _mem_bwd(qm, kvm, do, ts):
    S = qm.shape[0]

    def body(q_ref, kv_ref, do_ref, dq_ref, dkv_ref):
        @pl.when(pl.program_id(0) == 0)
        def _():
            dkv_ref[...] = jnp.zeros_like(dkv_ref)

        qks = [_dot_nt(q_ref[:, _hs(h)], kv_ref[:, _hs(h)]) for h in range(MEM_H)]
        dps = [_dot_nt(do_ref[:, _hs(h)], kv_ref[:, _hs(MEM_H + h)]) for h in range(MEM_H)]
        for h in range(MEM_H):
            qh, kh, doh = q_ref[:, _hs(h)], kv_ref[:, _hs(h)], do_ref[:, _hs(h)]
            p = _mem_probs(qks[h])
            dp = dps[h]
            ds = (p * (dp - jnp.sum(p * dp, axis=1, keepdims=True)) * MEM_SCALE).astype(BF16)
            dq_ref[:, _hs(h)] = _dot(ds, kh).astype(BF16)
            dkv_ref[:, _hs(h)] += _dot_tn(ds, qh)
            dkv_ref[:, _hs(MEM_H + h)] += _dot_tn(p.astype(BF16), doh)

    return _pcall(
        body, name="mem_bwd", grid=(S // ts,), in_specs=[_tok(ts, 512), _full(MEM_LEN, D), _tok(ts, 512)],
        out_specs=[_tok(ts, 512), _full(MEM_LEN, D)],
        out_shape=[_sds((S, 512), BF16), _sds((MEM_LEN, D), F32)], sem=("arbitrary",),
    )(qm, kvm, do)


def _memkv_bwd(mem, mnorm, wkv, dkvm):
    def body(mem_ref, g_ref, w_ref, dkv_ref, dw_ref, dg_ref):
        n, _ = _rms(mem_ref[...])
        dkvb = dkv_ref[...].astype(BF16)
        dw_ref[...] = _dot_tn((n * g_ref[...]).astype(BF16), dkvb)
        dg_ref[...] = _colsum(_dot_nt(dkvb, w_ref[...]) * n)

    return _pcall(
        body, name="memkv_bwd", grid=(1,), in_specs=[_full(MEM_LEN, D), _full(1, D), _full(D, D), _full(MEM_LEN, D)],
        out_specs=[_full(D, D), _full(1, D)], out_shape=[_sds((D, D), F32), _sds((1, D), F32)], sem=("arbitrary",),
    )(mem, mnorm, wkv, dkvm)


def _adamw(w, g, m, v, name):
    rows, cols = w.shape
    tr = min(rows, FLAT_TILE)
    assert rows % tr == 0

    def body(w_ref, g_ref, m_ref, v_ref, d_ref, nm_ref, nv_ref):
        gv = g_ref[...]
        nm = ADAM_B1 * m_ref[...] + (1.0 - ADAM_B1) * gv
        nv = ADAM_B2 * v_ref[...] + (1.0 - ADAM_B2) * jnp.square(gv)
        m_hat = nm / (1.0 - ADAM_B1 ** ADAM_STEP)
        v_hat = nv / (1.0 - ADAM_B2 ** ADAM_STEP)
        d_ref[...] = -ADAM_LR * (m_hat / (jnp.sqrt(v_hat) + ADAM_EPS) + ADAM_WD * w_ref[...])
        nm_ref[...] = nm
        nv_ref[...] = nv

    spec = _tok(tr, cols)
    return _pcall(
        body, name=name, grid=(rows // tr,), in_specs=[spec] * 4, out_specs=[spec] * 3,
        out_shape=[_sds((rows, cols), F32)] * 3, sem=("arbitrary",),
    )(w, g, m, v)


def _my_place():
    return lax.axis_index("x"), lax.axis_index("y"), lax.axis_index("c")


def _remote(src, dst, send_sems, recv_sems, k, to):
    return pltpu.make_async_remote_copy(src_ref=src, dst_ref=dst, send_sem=send_sems.at[k], recv_sem=recv_sems.at[k],
                                        device_id=to, device_id_type=MESH_ID)


class _Rider(NamedTuple):
    inputs: list
    out_shapes: list
    sems: list
    start: Callable
    finish: Callable


def _run_rider(rider, name):
    ni, no = len(rider.inputs), len(rider.out_shapes)

    def body(*refs):
        parts = refs[:ni], refs[ni:ni + no], refs[ni + no:]
        rider.start(*parts)
        rider.finish(*parts)

    hbm = pl.BlockSpec(memory_space=pl.ANY)
    return pl.pallas_call(body, name=name, out_shape=rider.out_shapes, in_specs=[hbm] * ni, out_specs=[hbm] * no,
                          scratch_shapes=rider.sems)(*rider.inputs)


def _join_riders(*riders):
    def cut(seq, lens):
        at, out = 0, []
        for n in lens:
            out.append(seq[at:at + n])
            at += n
        return out

    def each(ins, outs, sems):
        return zip(riders, cut(ins, [len(r.inputs) for r in riders]), cut(outs, [len(r.out_shapes) for r in riders]),
                   cut(sems, [len(r.sems) for r in riders]))

    def start(ins, outs, sems):
        for r, i, o, s in each(ins, outs, sems):
            r.start(i, o, s)

    def finish(ins, outs, sems):
        for r, i, o, s in each(ins, outs, sems):
            r.finish(i, o, s)

    return _Rider([a for r in riders for a in r.inputs], [a for r in riders for a in r.out_shapes],
                  [a for r in riders for a in r.sems], start, finish)


def _gather_rider(shards):
    n = len(shards)

    def copies(ins, outs, sems):
        send_sems, recv_sems, local_sems = sems
        x, y, c = _my_place()
        chips = [(1 - x, y), (x, 1 - y), (1 - x, 1 - y)]
        mine = [pltpu.make_async_copy(ins[i], outs[i].at[2 * x + y], local_sems.at[i]) for i in range(n)]

        def copy(i, k, slot, to):
            return _remote(ins[i], outs[i].at[slot], send_sems, recv_sems, 3 * i + k, to)

        sends = [copy(i, k, 2 * x + y, (px, py, c)) for i in range(n) for k, (px, py) in enumerate(chips)]
        lands = [copy(i, k, 2 * px + py, (px, py, c)) for i in range(n) for k, (px, py) in enumerate(chips)]
        return mine, sends, lands

    def start(ins, outs, sems):
        mine, sends, _ = copies(ins, outs, sems)
        for cp in mine + sends:
            cp.start()

    def finish(ins, outs, sems):
        mine, sends, lands = copies(ins, outs, sems)
        for cp in lands:
            cp.wait_recv()
        for cp in sends:
            cp.wait_send()
        for cp in mine:
            cp.wait()

    return _Rider(list(shards), [_sds((4,) + s.shape, s.dtype) for s in shards],
                  [pltpu.SemaphoreType.DMA((3 * n,)), pltpu.SemaphoreType.DMA((3 * n,)), pltpu.SemaphoreType.DMA((n,))],
                  start, finish)


def _scatter_rider(gws, layer, gsmall=None):
    n = len(gws)
    per = n + (gsmall is not None)

    def plan(ins, outs, sems):
        send_sems, recv_sems, local_sems = sems
        x, y, c = _my_place()
        me = 4 * x + 2 * y + c

        def peer(m):
            return (x ^ ((m >> 2) & 1), y ^ ((m >> 1) & 1), c ^ (m & 1))

        def slot_of(m):
            px, py, pc = peer(m)
            return 4 * px + 2 * py + pc

        def piece(i, m, slot):
            px, py, pc = peer(m)
            return _remote(ins[i].at[2 * px + py], outs[i].at[slot], send_sems, recv_sems, (m - 1) * per + i,
                           (px, py, pc))

        def small(m, slot):
            return _remote(ins[n], outs[n].at[slot], send_sems, recv_sems, (m - 1) * per + n, peer(m))

        own = [pltpu.make_async_copy(ins[i].at[2 * x + y], outs[i].at[me], local_sems.at[i]) for i in range(n)]
        return c, me, slot_of, piece, small, own

    def start(ins, outs, sems):
        c, me, slot_of, piece, small, own = plan(ins, outs, sems)

        @pl.when(c == layer)
        def _():
            for cp in own:
                cp.start()
            for m in (2, 4, 6):
                for i in range(n):
                    piece(i, m, me).start()

        @pl.when(c != layer)
        def _():
            for m in (1, 3, 5, 7):
                for i in range(n):
                    piece(i, m, me).start()

        if gsmall is not None:
            pltpu.make_async_copy(ins[n], outs[n].at[me], sems[2].at[n]).start()
            for m in range(1, 8):
                small(m, me).start()

    def finish(ins, outs, sems):
        c, me, slot_of, piece, small, own = plan(ins, outs, sems)

        @pl.when(c == layer)
        def _():
            for m in range(1, 8):
                for i in range(n):
                    piece(i, m, slot_of(m)).wait_recv()
            for m in (2, 4, 6):
                for i in range(n):
                    piece(i, m, me).wait_send()
            for cp in own:
                cp.wait()

        @pl.when(c != layer)
        def _():
            for m in (1, 3, 5, 7):
                for i in range(n):
                    piece(i, m, me).wait_send()

        if gsmall is not None:
            for m in range(1, 8):
                small(m, slot_of(m)).wait_recv()
            for m in range(1, 8):
                small(m, me).wait_send()
            pltpu.make_async_copy(ins[n], outs[n].at[me], sems[2].at[n]).wait()

    inputs = list(gws) + ([gsmall] if gsmall is not None else [])
    shapes = [_sds((8,) + g.shape[1:], g.dtype) for g in gws]
    if gsmall is not None:
        shapes.append(_sds((8,) + gsmall.shape, gsmall.dtype))
    nsem = 7 * per
    return _Rider(inputs, shapes, [pltpu.SemaphoreType.DMA((nsem,)), pltpu.SemaphoreType.DMA((nsem,)),
                                   pltpu.SemaphoreType.DMA((per,))], start, finish)


def _sum_slots(slots, name, other=None):
    _, r, c = slots.shape
    tr = min(r, FLAT_TILE)
    assert r % tr == 0
    out_dtype = F32 if other is None else BF16

    def total(ref):
        acc = ref[0].astype(F32)
        for d in range(1, 8):
            acc = acc + ref[d].astype(F32)
        return acc

    def body(*refs):
        o_ref = refs[-1]
        if other is None:
            o_ref[...] = total(refs[0])
        else:
            core = lax.axis_index("c")

            @pl.when(core == 0)
            def _():
                o_ref[...] = total(refs[0]).astype(out_dtype)

            @pl.when(core == 1)
            def _():
                o_ref[...] = total(refs[1]).astype(out_dtype)

    spec = pl.BlockSpec((8, tr, c), lambda i: (0, i, 0))
    ins = [slots] if other is None else [slots, other]
    return _pcall(
        body, name=name, grid=(r // tr,), in_specs=[spec] * len(ins),
        out_specs=_tok(tr, c), out_shape=_sds((r, c), out_dtype), sem=("arbitrary",),
    )(*ins)


def _swap_rider(reds):
    n = len(reds)

    def copies(ins, outs, sems):
        send_sems, recv_sems, local_sems = sems
        x, y, c = _my_place()
        sibling = (x, y, 1 - c)
        mine = [pltpu.make_async_copy(ins[i], outs[i].at[c], local_sems.at[i]) for i in range(n)]
        sends = [_remote(ins[i], outs[i].at[c], send_sems, recv_sems, i, sibling) for i in range(n)]
        lands = [_remote(ins[i], outs[i].at[1 - c], send_sems, recv_sems, i, sibling) for i in range(n)]
        return mine, sends, lands

    def start(ins, outs, sems):
        mine, sends, _ = copies(ins, outs, sems)
        for cp in mine + sends:
            cp.start()

    def finish(ins, outs, sems):
        mine, sends, lands = copies(ins, outs, sems)
        for cp in lands:
            cp.wait_recv()
        for cp in sends:
            cp.wait_send()
        for cp in mine:
            cp.wait()

    return _Rider(list(reds), [_sds((2,) + r.shape, r.dtype) for r in reds],
                  [pltpu.SemaphoreType.DMA((n,)), pltpu.SemaphoreType.DMA((n,)), pltpu.SemaphoreType.DMA((n,))],
                  start, finish)


W_IN_SHARD, W_IN_SHARD_PAD = 1192, 1280
W_UQ_SHARD, W_UQ_SHARD_PAD = 192, 256


def _swa_place(t):
    z = jnp.zeros_like(t)
    lo = jnp.concatenate([t, z], axis=1)
    hi = jnp.concatenate([z, t], axis=1)
    group = (jnp.arange(NH) // SWA_R).reshape((NH,) + (1,) * (t.ndim - 1))
    full = jnp.where(group == 0, lo, hi)
    return full.reshape((NH * LANE,) + t.shape[2:])


def _swa_unplace(t):
    t = t.reshape((NH, 2, 64) + t.shape[1:])
    return jnp.concatenate([t[:SWA_R, 0], t[SWA_R:, 1]], axis=0)


def _pad_w_o_mla(w):
    return jnp.pad(w.reshape(NH, 64, D), ((0, 0), (0, 64), (0, 0))).reshape(NH * LANE, D)


def _unpad_w_o_mla(g):
    return g.reshape(NH, LANE, D)[:, :64].reshape(NH * 64, D)


def _w_in_cols():
    src = np.full((C_END,), -1, np.int64)
    src[C_CQ:C_KPE] = np.arange(0, 384)
    src[C_KPE + 64:C_KPE + 96] = np.arange(384, 416)
    for h in range(NH):
        at = C_QS + h * LANE + 64 * (h // SWA_R)
        src[at:at + 64] = 416 + h * 64 + np.arange(64)
    src[C_KS:C_END] = np.arange(928, IN_COLS)
    return src


def _w_uq_cols():
    src = np.full((NH * LANE,), -1, np.int64)
    for h in range(NH):
        src[h * LANE:h * LANE + 96] = h * 96 + np.arange(96)
    return src


def _w_ukv_cols():
    src = np.full((2 * NH * LANE,), -1, np.int64)
    for h in range(NH):
        src[h * LANE:h * LANE + 64] = h * 128 + np.arange(64)
        src[NH * LANE + h * LANE:NH * LANE + h * LANE + 64] = h * 128 + 64 + np.arange(64)
    return src


def _selection(src_cols, width, width_pad):
    want = jnp.asarray(np.asarray(src_cols, np.int32))[None, None, :]
    k = jnp.arange(width_pad, dtype=jnp.int32)[None, :, None]
    have = jnp.where(k < width, jnp.arange(4, dtype=jnp.int32)[:, None, None] * width + k, -2)
    return (want == have).astype(BF16)


def _selections():
    return dict(w_in=_selection(_w_in_cols(), W_IN_SHARD, W_IN_SHARD_PAD),
                w_uq=_selection(_w_uq_cols(), W_UQ_SHARD, W_UQ_SHARD_PAD),
                w_ukv=_selection(_w_ukv_cols(), 256, 256))


def _pad_last(a, width):
    return jnp.pad(a, ((0, 0),) * (a.ndim - 1) + ((0, width - a.shape[-1]),))


def _wire_shards(W, l):
    out = {n: W[n][l].astype(BF16) for n in SHARDED}
    out["w_in"] = _pad_last(out["w_in"], W_IN_SHARD_PAD)
    out["w_uq"] = _pad_last(out["w_uq"], W_UQ_SHARD_PAD)
    return [out[n] for n in SHARDED]


def _join_shards(t, axis):
    _, L, r, c = t.shape
    if axis == 2:
        return t.transpose(1, 2, 0, 3).reshape(L, r, 4 * c)
    return t.transpose(1, 0, 2, 3).reshape(L, 4 * r, c)


EARLY = SHARDED[:3]
LATE = SHARDED[3:]


def _kernel_weights_early(gathered, sels):
    lay = {n: _select_fwd(a[:, None], sels[n], "lay_" + n)[0] for n, a in zip(EARLY, gathered)}
    return dict(win=lay["w_in"], wuq=lay["w_uq"], wukv=lay["w_ukv"])


def _kernel_weights_late(gathered):
    whole = {n: _join_shards(a[:, None], SHARD_AXIS[n])[0] for n, a in zip(LATE, gathered)}
    return dict(wmem=whole["w_mem_kv"], wa=_pad_w_o_mla(whole["w_o_mla"]),
                wb=_swa_place(whole["w_o_swa"].reshape(NH, 64, D)), wc=whole["w_o_mem"], wout=whole["w_out"],
                wup=whole["w_up"], wdown=whole["w_down"])


def _kernel_weights(gathered, sels):
    return {**_kernel_weights_early(gathered[:3], sels), **_kernel_weights_late(gathered[3:])}


def _cols_to_shards(g):
    r, c4 = g.shape
    return g.reshape(r, 4, c4 // 4).transpose(1, 0, 2).astype(BF16)


def _rope_tables(S):
    pos = jnp.arange(S, dtype=F32)
    inv = 1.0 / (ROPE_THETA ** (jnp.arange(0, 32, 2, dtype=F32) / 32))
    ang = pos[:, None] * inv[None, :]
    cos, sin = jnp.cos(ang), jnp.sin(ang)
    one, zero = jnp.ones((S, 64), F32), jnp.zeros((S, 16), F32)
    rc = jnp.concatenate([one, cos, cos, jnp.ones((S, 32), F32)], axis=1)
    rs1 = jnp.concatenate([jnp.zeros((S, 64), F32), zero, sin, jnp.zeros((S, 32), F32)], axis=1)
    rs2 = jnp.concatenate([jnp.zeros((S, 64), F32), -sin, zero, jnp.zeros((S, 32), F32)], axis=1)
    return rc, rs1, rs2


def _bucket_map():
    qi = jnp.arange(WIN)[:, None]
    kj = jnp.arange(2 * WIN)[None, :]
    dist = qi + WIN - kj
    n = jnp.maximum(dist, 0)
    max_exact = REL_BUCKETS // 2
    nf = jnp.maximum(n, 1).astype(F32)
    large = max_exact + (jnp.log(nf / max_exact) / math.log(128 / max_exact)
                         * (REL_BUCKETS - max_exact)).astype(jnp.int32)
    large = jnp.minimum(large, REL_BUCKETS - 1)
    bucket = jnp.where(n < max_exact, n, large)
    return jnp.where((dist >= 0) & (dist < WIN), bucket, -1).astype(jnp.int32)


TS = 256
TQ = 1024
TQ_FWD = 1024


def _local_step(x, mem, tgt, kw0, sp, sels, kw1=None, wire=None):
    S = x.shape[0]
    ts = min(TS, S)
    tq = min(TQ, S)
    rc, rs1, rs2 = _rope_tables(S)
    bmap = _bucket_map()
    bias = _bias_build(sp["rel_bias"], bmap)
    row = lambda v: v.reshape(1, -1)

    saved = []
    kw = [dict(kw0), kw1]
    for l in range(DEPTH):
        w = kw[l]
        an, qn, kvn = row(sp["attn_norm"][l]), row(sp["mla_q_norm"][l]), row(sp["mla_kv_norm"][l])
        bg, mnorm, mlpn = row(sp["b_gate"][l]), row(sp["mem_norm"][l]), row(sp["mlp_norm"][l])
        sinks = jnp.broadcast_to(sp["attn_sinks"][l][:, None], (NH, LANE))
        q, k, v, qs, ks, vs, qm, g = _pre_fwd(x, an, w["win"], bg, qn, kvn, w["wuq"], w["wukv"], rc, rs1, rs2, ts)
        carry = _join_riders(_gather_rider(wire[0]), _gather_rider(wire[1])) if (l == 0 and wire) else None
        oa, lse, got = _mla_fwd(q, k, v, min(TQ_FWD, S), carry)
        if carry:
            w.update(_kernel_weights_late(got[:len(LATE)]))
            kw[1] = _kernel_weights(got[len(LATE):], sels)
        ob = _swa_fwd(qs, ks, vs, bias, sinks)
        kvm = _memkv_fwd(mem, mnorm, w["wmem"])
        oc = _mem_fwd(qm, kvm, ts)
        x1, yb = _merge_fwd(x, g, oa, ob, oc, w["wa"], w["wb"], w["wc"], w["wout"], ts)
        x2 = _mlp_fwd(x1, mlpn, w["wup"], w["wdown"], ts)
        saved.append(dict(w=w, x=x, x1=x1, q=q, k=k, v=v, qs=qs, ks=ks, vs=vs, qm=qm, g=g, oa=oa, lse=lse, ob=ob,
                          oc=oc, kvm=kvm, yb=yb, an=an, qn=qn, kvn=kvn, mnorm=mnorm, mlpn=mlpn, sinks=sinks))
        x = x2

    sq, dx, dfn = _loss_kernel(x, row(sp["final_norm"]), tgt, ts)

    big = {n: [None] * DEPTH for n in SHARDED}
    small = {n: [None] * DEPTH for n in SMALL if n not in ("rel_bias", "final_norm")}
    dbias_total = None
    slots1 = None
    for l in reversed(range(DEPTH)):
        sv = saved[l]
        w = sv["w"]
        dx1, hb2, dub, ab, dxb, dmlpn = _mlp_bwd(dx, sv["x1"], sv["mlpn"], w["wup"], w["wdown"], ts)
        big["w_up"][l] = _matmul_tn(hb2, dub, "dw_up", shard_axis=1)
        big["w_down"][l] = _matmul_tn(ab, dxb, "dw_down", shard_axis=0)
        small["mlp_norm"][l] = dmlpn[0]

        dgp, dyo, doa, dob, doc, dla, dx1b, dbg = _merge_bwd(dx1, sv["g"], sv["oa"], sv["ob"], sv["oc"], w["wa"],
                                                             w["wb"], w["wc"], w["wout"], ts)
        big["w_out"][l] = _matmul_tn(sv["yb"], dx1b, "dw_out", shard_axis=0)
        big["w_o_mla"][l] = _cols_to_shards(_unpad_w_o_mla(_matmul_tn(sv["oa"], dyo[:, 0:D], "dw_o_mla")))
        big["w_o_swa"][l] = _cols_to_shards(
            _swa_unplace(_matmul_tn(sv["ob"], dyo[:, D:2 * D], "dw_o_swa")).reshape(NH * 64, D))
        big["w_o_mem"][l] = _matmul_tn(sv["oc"], dyo[:, 2 * D:3 * D], "dw_o_mem", shard_axis=1)
        small["b_gate"][l] = dbg[0]

        dqm, dkvm = _mem_bwd(sv["qm"], sv["kvm"], doc, ts)
        dwmem, dmnorm = _memkv_bwd(mem, sv["mnorm"], w["wmem"], dkvm)
        big["w_mem_kv"][l] = dwmem.reshape(4, D // 4, D).astype(BF16)
        small["mem_norm"][l] = dmnorm[0]

        dqs, dks, dvs, dbias, dsink = _swa_bwd(sv["qs"], sv["ks"], sv["vs"], dob, bias, sv["sinks"])
        dbias_total = dbias if dbias_total is None else dbias_total + dbias
        small["attn_sinks"][l] = dsink[:, 0]

        carry = _join_riders(_scatter_rider([big[n][1] for n in SHARDED], 1),
                             _scatter_rider([big[n][0] for n in LATE], 0)) if (l == 0 and wire) else None
        dq, dk, dv, slots = _mla_bwd(sv["q"], sv["k"], sv["v"], doa, sv["lse"], dla, tq, carry)
        if carry:
            late0, late1 = slots[len(SHARDED):], slots[len(EARLY):len(SHARDED)]
            carry = _swap_rider([_sum_slots(s0, "sum_" + n, s1) for n, s0, s1 in zip(LATE, late0, late1)])

        dx, dproj, hb, cqn, ckvn, dqpre, dkv, dan, dqn, dkvn, swapped = _pre_bwd(
            sv["x"], dx1, dq, dk, dv, dqs, dks, dvs, dqm, dgp, sv["an"], sv["qn"], sv["kvn"], w["win"], w["wuq"],
            w["wukv"], rc, rs1, rs2, ts, carry)
        if carry:
            slots1 = (slots[:len(EARLY)], swapped)
        for n, (a, b) in (("w_in", (hb, dproj)), ("w_uq", (cqn, dqpre)), ("w_ukv", (ckvn, dkv))):
            big[n][l] = _select_bwd(_matmul_tn(a, b, "d" + n)[None], sels[n], "shard_d" + n)[0]
        small["attn_norm"][l] = dan[0]
        small["mla_q_norm"][l] = dqn[0]
        small["mla_kv_norm"][l] = dkvn[0]

    gw = [[big[n][l] for n in SHARDED] for l in range(DEPTH)]
    gs = {n: jnp.stack(v) for n, v in small.items()}
    gs["rel_bias"] = _bias_reduce(dbias_total, bmap)
    gs["final_norm"] = dfn[0]
    return sq, dx, gw, slots1, gs


def _flatten(parts, rows):
    flat = jnp.concatenate([p.reshape(-1) for p in parts])
    return jnp.pad(flat, (0, rows * FLAT_W - flat.shape[0])).reshape(rows, FLAT_W)


def _unflatten(buf, shapes):
    flat = buf.reshape(-1)
    out, at = [], 0
    for s in shapes:
        n = int(np.prod(s))
        out.append(flat[at:at + n].reshape(s))
        at += n
    return out


def kernel(x, mem, rel_bias, attn_norm, mem_norm, w_in, b_gate, mla_q_norm, w_uq, mla_kv_norm, w_ukv, attn_sinks, w_mem_kv, w_o_mla, w_o_swa, w_o_mem, w_out, mlp_norm, w_up, w_down, final_norm, loss_target, m_rel_bias, m_attn_norm, m_mem_norm, m_w_in, m_b_gate, m_mla_q_norm, m_w_uq, m_mla_kv_norm, m_w_ukv, m_attn_sinks, m_w_mem_kv, m_w_o_mla, m_w_o_swa, m_w_o_mem, m_w_out, m_mlp_norm, m_w_up, m_w_down, m_final_norm, v_rel_bias, v_attn_norm, v_mem_norm, v_w_in, v_b_gate, v_mla_q_norm, v_w_uq, v_mla_kv_norm, v_w_ukv, v_attn_sinks, v_w_mem_kv, v_w_o_mla, v_w_o_swa, v_w_o_mem, v_w_out, v_mlp_norm, v_w_up, v_w_down, v_final_norm):
    args = dict(locals())
    W = {n: args[n] for n in WEIGHTS}
    M = {n: args["m_" + n] for n in WEIGHTS}
    V = {n: args["v_" + n] for n in WEIGHTS}
    small_shapes = [W[n].shape for n in SMALL]
    sels = _selections()

    wire0, wire1 = _wire_shards(W, 0), _wire_shards(W, 1)
    kw0 = _kernel_weights_early(_run_rider(_gather_rider(wire0[:len(EARLY)]), "gather_weights"), sels)
    sp = {n: W[n] for n in SMALL}

    sq, dx, gw, carried, gs = _local_step(x[0], mem[0], loss_target[0], kw0, sp, sels,
                                          wire=(wire0[len(EARLY):], wire1))

    slots1_early, swapped_late = carried
    gsmall = _flatten([gs[n] for n in SMALL], SMALL_ROWS)
    *slots0_early, small_slots = _run_rider(_scatter_rider(gw[0][:len(EARLY)], 0, gsmall), "scatter_grads")
    reds = [_sum_slots(s0, "sum_" + n, s1) for n, s0, s1 in zip(EARLY, slots0_early, slots1_early)]
    red_small = _sum_slots(small_slots, "sum_small")
    swapped = list(_run_rider(_swap_rider(reds), "swap_layers")) + list(swapped_late)
    G = {n: g.astype(F32) for n, g in zip(SHARDED, swapped)}
    G["w_in"] = G["w_in"][..., :W_IN_SHARD]
    G["w_uq"] = G["w_uq"][..., :W_UQ_SHARD]

    DW, NM, NV = {}, {}, {}
    for n in SHARDED:
        shape = W[n].shape
        two_d = lambda a: a.reshape(-1, shape[-1])
        d, nm, nv = _adamw(two_d(W[n]), two_d(G[n]), two_d(M[n]), two_d(V[n]), "adamw_" + n)
        DW[n], NM[n], NV[n] = d.reshape(shape), nm.reshape(shape), nv.reshape(shape)
    d_s, m_s, v_s = _adamw(_flatten([W[n] for n in SMALL], SMALL_ROWS), red_small,
                           _flatten([M[n] for n in SMALL], SMALL_ROWS), _flatten([V[n] for n in SMALL], SMALL_ROWS),
                           "adamw_small")
    for out, buf in ((G, red_small), (DW, d_s), (NM, m_s), (NV, v_s)):
        out.update(zip(SMALL, _unflatten(buf, small_shapes)))
    loss = lax.psum(0.5 * sq[0, 0] / D, ("x", "y", "c"))
    return (loss, dx[None], *[G[n] for n in WEIGHTS], *[DW[n] for n in WEIGHTS], *[NM[n] for n in WEIGHTS],
            *[NV[n] for n in WEIGHTS])
```

```python
import functools
import math
from typing import Callable, NamedTuple

import numpy as np
import jax
import jax.numpy as jnp
from jax import lax
from jax.experimental import pallas as pl
from jax.experimental.pallas import tpu as pltpu

F32 = jnp.float32
BF16 = jnp.bfloat16

D = 1024
DFF = 4096
DEPTH = 2
EPS = 1e-6
LANE = 128
NH = 8
SWA_R = 4
MEM_H = 4
MEM_LEN = 256
WIN = 128
NEG = -1e30
MLA_SCALE = 96 ** -0.5
SWA_SCALE = 64 ** -0.5
MEM_SCALE = 128 ** -0.5
REL_BUCKETS = 32
ROPE_THETA = 10000.0

C_CQ, C_CKV, C_KPE, C_QS, C_KS, C_VS, C_QM, C_G, C_END = 0, 256, 384, 512, 1536, 1664, 1792, 2304, 5376
IN_COLS = 4768

ADAM_LR = 0.001
ADAM_B1 = 0.9
ADAM_B2 = 0.999
ADAM_EPS = 1e-08
ADAM_WD = 0.01
ADAM_STEP = 10

VMEM_LIMIT = 56 * 1024 * 1024

SHARDED = ("w_in", "w_uq", "w_ukv", "w_mem_kv", "w_o_mla", "w_o_swa", "w_o_mem", "w_out", "w_up", "w_down")
SHARD_AXIS = {"w_in": 2, "w_uq": 2, "w_ukv": 2, "w_mem_kv": 1, "w_o_mla": 2, "w_o_swa": 2, "w_o_mem": 2,
              "w_out": 1, "w_up": 2, "w_down": 1}
SMALL = ("rel_bias", "attn_norm", "mem_norm", "b_gate", "mla_q_norm", "mla_kv_norm", "attn_sinks", "mlp_norm",
         "final_norm")
WEIGHTS = ("rel_bias", "attn_norm", "mem_norm", "w_in", "b_gate", "mla_q_norm", "w_uq", "mla_kv_norm", "w_ukv",
           "attn_sinks", "w_mem_kv", "w_o_mla", "w_o_swa", "w_o_mem", "w_out", "mlp_norm", "w_up", "w_down",
           "final_norm")
FLAT_W = 1024
FLAT_TILE = 256
SMALL_ROWS = 16
MESH_ID = pl.DeviceIdType.MESH


def _pcall(body, *, name, grid, in_specs, out_specs, out_shape, scratch=(), prefetch=0, sem=None):
    params = pltpu.CompilerParams(dimension_semantics=sem, vmem_limit_bytes=VMEM_LIMIT)
    if prefetch:
        spec = pltpu.PrefetchScalarGridSpec(num_scalar_prefetch=prefetch, grid=grid, in_specs=in_specs,
                                            out_specs=out_specs, scratch_shapes=scratch)
        return pl.pallas_call(body, name=name, grid_spec=spec, out_shape=out_shape, compiler_params=params)
    return pl.pallas_call(body, name=name, grid=grid, in_specs=in_specs, out_specs=out_specs, out_shape=out_shape,
                          scratch_shapes=scratch, compiler_params=params)


def _tok(ts, w):
    return pl.BlockSpec((ts, w), lambda i: (i, 0))


def _full(*shape):
    return pl.BlockSpec(shape, lambda *_: (0,) * len(shape))


def _sds(shape, dtype):
    return jax.ShapeDtypeStruct(shape, dtype)


def _dot(a, b):
    return jnp.dot(a, b, preferred_element_type=F32)


def _dot_nt(a, b):
    return lax.dot_general(a, b, (((1,), (1,)), ((), ())), preferred_element_type=F32)


def _dot_tn(a, b):
    return lax.dot_general(a, b, (((0,), (0,)), ((), ())), preferred_element_type=F32)


def _rms(x):
    r = lax.rsqrt(jnp.mean(x * x, axis=-1, keepdims=True) + EPS)
    return x * r, r


def _rms_bwd(dyg, n, r):
    return r * (dyg - n * jnp.mean(n * dyg, axis=-1, keepdims=True))


def _rope(t, c, s1, s2):
    return t * c + pltpu.roll(t, 16, 1) * s1 + pltpu.roll(t, LANE - 16, 1) * s2


def _rope_bwd(dy, c, s1, s2):
    return dy * c + pltpu.roll(dy * s1, LANE - 16, 1) + pltpu.roll(dy * s2, 16, 1)


def _hs(h):
    return slice(h * LANE, (h + 1) * LANE)


def _colsum(t):
    return jnp.sum(t, axis=0, keepdims=True)


def _pre_fwd(x, an, win, bg, qn, kvn, wuq, wukv, rc, rs1, rs2, ts):
    S = x.shape[0]

    def body(x_ref, an_ref, win_ref, bg_ref, qn_ref, kvn_ref, wuq_ref, wukv_ref, rc_ref, rs1_ref, rs2_ref,
             q_ref, k_ref, v_ref, qs_ref, ks_ref, vs_ref, qm_ref, g_ref):
        n, _ = _rms(x_ref[...])
        hb = (n * an_ref[...]).astype(BF16)
        pa = _dot(hb, win_ref[:, C_CQ:C_QS])
        ncq, _ = _rms(pa[:, 0:256])
        cqn = (ncq * qn_ref[...]).astype(BF16)
        nkv, _ = _rms(pa[:, 256:384])
        ckvn = (nkv * kvn_ref[...]).astype(BF16)
        c, s1, s2 = rc_ref[...], rs1_ref[...], rs2_ref[...]
        kper = _rope(pa[:, 384:512], c, s1, s2)
        qp = _dot(cqn, wuq_ref[...])
        kv = _dot(ckvn, wukv_ref[...])
        for h in range(NH):
            q_ref[:, _hs(h)] = (_rope(qp[:, _hs(h)], c, s1, s2) * MLA_QSCALE).astype(BF16)
            k_ref[:, _hs(h)] = (kv[:, _hs(h)] + kper).astype(BF16)
        v_ref[...] = kv[:, NH * LANE:].astype(BF16)
        pb = _dot(hb, win_ref[:, C_QS:C_G])
        qs_ref[...] = pb[:, 0:1024].astype(BF16)
        ks_ref[...] = pb[:, 1024:1152].astype(BF16)
        vs_ref[...] = pb[:, 1152:1280].astype(BF16)
        qm_ref[...] = pb[:, 1280:1792].astype(BF16)
        g_ref[...] = jax.nn.sigmoid(_dot(hb, win_ref[:, C_G:C_END]) + bg_ref[...])

    return _pcall(
        body, name="pre_fwd", grid=(S // ts,),
        in_specs=[_tok(ts, D), _full(1, D), _full(D, C_END), _full(1, 3 * D), _full(1, 256), _full(1, 128),
                  _full(256, NH * LANE), _full(128, 2 * NH * LANE), _tok(ts, LANE), _tok(ts, LANE), _tok(ts, LANE)],
        out_specs=[_tok(ts, 1024), _tok(ts, 1024), _tok(ts, 1024), _tok(ts, 1024), _tok(ts, 128), _tok(ts, 128),
                   _tok(ts, 512), _tok(ts, 3 * D)],
        out_shape=[_sds((S, 1024), BF16), _sds((S, 1024), BF16), _sds((S, 1024), BF16), _sds((S, 1024), BF16),
                   _sds((S, 128), BF16), _sds((S, 128), BF16), _sds((S, 512), BF16), _sds((S, 3 * D), F32)],
        sem=("arbitrary",),
    )(x, an, win, bg, qn, kvn, wuq, wukv, rc, rs1, rs2)


def _merge_fwd(x, g, oa, ob, oc, wa, wb, wc, wout, ts):
    S = x.shape[0]

    def body(x_ref, g_ref, oa_ref, ob_ref, oc_ref, wa_ref, wb_ref, wc_ref, wout_ref, x1_ref, yb_ref):
        y = g_ref[:, 0:D] * _dot(oa_ref[...], wa_ref[...])
        y = y + g_ref[:, D:2 * D] * _dot(ob_ref[...], wb_ref[...])
        y = y + g_ref[:, 2 * D:3 * D] * _dot(oc_ref[...], wc_ref[...])
        yb = y.astype(BF16)
        yb_ref[...] = yb
        x1_ref[...] = x_ref[...] + _dot(yb, wout_ref[...])

    return _pcall(
        body, name="merge_fwd", grid=(S // ts,),
        in_specs=[_tok(ts, D), _tok(ts, 3 * D), _tok(ts, 1024), _tok(ts, 1024), _tok(ts, 512),
                  _full(1024, D), _full(1024, D), _full(512, D), _full(D, D)],
        out_specs=[_tok(ts, D), _tok(ts, D)],
        out_shape=[_sds((S, D), F32), _sds((S, D), BF16)],
        sem=("arbitrary",),
    )(x, g, oa, ob, oc, wa, wb, wc, wout)


def _mlp_fwd(x1, mn, wup, wdown, ts):
    S = x1.shape[0]

    def body(x_ref, mn_ref, wup_ref, wdown_ref, x2_ref):
        xv = x_ref[...]
        n, _ = _rms(xv)
        u = _dot((n * mn_ref[...]).astype(BF16), wup_ref[...])
        a = jnp.square(jnp.maximum(u, 0.0))
        x2_ref[...] = xv + _dot(a.astype(BF16), wdown_ref[...])

    return _pcall(
        body, name="mlp_fwd", grid=(S // ts,),
        in_specs=[_tok(ts, D), _full(1, D), _full(D, DFF), _full(DFF, D)],
        out_specs=_tok(ts, D), out_shape=_sds((S, D), F32), sem=("arbitrary",),
    )(x1, mn, wup, wdown)


def _loss_kernel(x, fn, tgt, ts):
    S = x.shape[0]

    def body(x_ref, fn_ref, t_ref, loss_ref, dx_ref, dfn_ref):
        @pl.when(pl.program_id(0) == 0)
        def _():
            loss_ref[...] = jnp.zeros_like(loss_ref)
            dfn_ref[...] = jnp.zeros_like(dfn_ref)

        n, r = _rms(x_ref[...])
        err = n * fn_ref[...] - t_ref[...]
        loss_ref[...] += jnp.sum(err * err)
        dy = err * (1.0 / D)
        dfn_ref[...] += _colsum(dy * n)
        dx_ref[...] = _rms_bwd(dy * fn_ref[...], n, r)

    return _pcall(
        body, name="loss_head", grid=(S // ts,),
        in_specs=[_tok(ts, D), _full(1, D), _tok(ts, D)],
        out_specs=[_full(1, LANE), _tok(ts, D), _full(1, D)],
        out_shape=[_sds((1, LANE), F32), _sds((S, D), F32), _sds((1, D), F32)],
        sem=("arbitrary",),
    )(x, fn, tgt)


def _mlp_bwd(dx2, x1, mn, wup, wdown, ts):
    S = x1.shape[0]

    def body(dx_ref, x_ref, mn_ref, wup_ref, wdown_ref, dx1_ref, hb_ref, dub_ref, ab_ref, dxb_ref, dmn_ref):
        @pl.when(pl.program_id(0) == 0)
        def _():
            dmn_ref[...] = jnp.zeros_like(dmn_ref)

        dx = dx_ref[...]
        n, r = _rms(x_ref[...])
        g = mn_ref[...]
        hb = (n * g).astype(BF16)
        hb_ref[...] = hb
        rl = jnp.maximum(_dot(hb, wup_ref[...]), 0.0)
        ab_ref[...] = jnp.square(rl).astype(BF16)
        dxb = dx.astype(BF16)
        dxb_ref[...] = dxb
        dub = (_dot_nt(dxb, wdown_ref[...]) * (2.0 * rl)).astype(BF16)
        dub_ref[...] = dub
        dh = _dot_nt(dub, wup_ref[...])
        dmn_ref[...] += _colsum(dh * n)
        dx1_ref[...] = dx + _rms_bwd(dh * g, n, r)

    return _pcall(
        body, name="mlp_bwd", grid=(S // ts,),
        in_specs=[_tok(ts, D), _tok(ts, D), _full(1, D), _full(D, DFF), _full(DFF, D)],
        out_specs=[_tok(ts, D), _tok(ts, D), _tok(ts, DFF), _tok(ts, DFF), _tok(ts, D), _full(1, D)],
        out_shape=[_sds((S, D), F32), _sds((S, D), BF16), _sds((S, DFF), BF16), _sds((S, DFF), BF16),
                   _sds((S, D), BF16), _sds((1, D), F32)],
        sem=("arbitrary",),
    )(dx2, x1, mn, wup, wdown)


def _merge_bwd(dx1, g, oa, ob, oc, wa, wb, wc, wout, ts):
    S = dx1.shape[0]

    def body(dx_ref, g_ref, oa_ref, ob_ref, oc_ref, wa_ref, wb_ref, wc_ref, wout_ref,
             dgp_ref, dyo_ref, doa_ref, dob_ref, doc_ref, dla_ref, dxb_ref, dbg_ref):
        @pl.when(pl.program_id(0) == 0)
        def _():
            dbg_ref[...] = jnp.zeros_like(dbg_ref)

        dxb = dx_ref[...].astype(BF16)
        dxb_ref[...] = dxb
        dy = _dot_nt(dxb, wout_ref[...])
        branches = ((oa_ref, wa_ref, doa_ref), (ob_ref, wb_ref, dob_ref), (oc_ref, wc_ref, doc_ref))
        for b, (o_ref, w_ref, do_ref) in enumerate(branches):
            cols = slice(b * D, (b + 1) * D)
            gb = g_ref[:, cols]
            o = o_ref[...]
            dgpre = dy * _dot(o, w_ref[...]) * gb * (1.0 - gb)
            dgp_ref[:, cols] = dgpre.astype(BF16)
            dbg_ref[:, cols] += _colsum(dgpre)
            dyo = (dy * gb).astype(BF16)
            dyo_ref[:, cols] = dyo
            do = _dot_nt(dyo, w_ref[...])
            do_ref[...] = do.astype(BF16)
            if b == 0:
                lane = lax.broadcasted_iota(jnp.int32, (ts, LANE), 1)
                dls = jnp.zeros((ts, LANE), F32)
                for h in range(NH):
                    dl = jnp.sum(do[:, _hs(h)] * o[:, _hs(h)].astype(F32), axis=1, keepdims=True)
                    dls = jnp.where(lane == h, dl, dls)
                dla_ref[:, 0, :] = jnp.transpose(dls)[0:NH, :]

    return _pcall(
        body, name="merge_bwd", grid=(S // ts,),
        in_specs=[_tok(ts, D), _tok(ts, 3 * D), _tok(ts, 1024), _tok(ts, 1024), _tok(ts, 512),
                  _full(1024, D), _full(1024, D), _full(512, D), _full(D, D)],
        out_specs=[_tok(ts, 3 * D), _tok(ts, 3 * D), _tok(ts, 1024), _tok(ts, 1024), _tok(ts, 512),
                   pl.BlockSpec((NH, 1, ts), lambda i: (0, 0, i)), _tok(ts, D), _full(1, 3 * D)],
        out_shape=[_sds((S, 3 * D), BF16), _sds((S, 3 * D), BF16), _sds((S, 1024), BF16), _sds((S, 1024), BF16),
                   _sds((S, 512), BF16), _sds((NH, 1, S), F32), _sds((S, D), BF16), _sds((1, 3 * D), F32)],
        sem=("arbitrary",),
    )(dx1, g, oa, ob, oc, wa, wb, wc, wout)


def _pre_bwd(x, dx1, dq, dk, dv, dqs, dks, dvs, dqm, dgp, an, qn, kvn, win, wuq, wukv, rc, rs1, rs2, ts, ride=None):
    S = x.shape[0]

    def body(x_ref, dx1_ref, dq_ref, dk_ref, dv_ref, dqs_ref, dks_ref, dvs_ref, dqm_ref, dgp_ref,
             an_ref, qn_ref, kvn_ref, win_ref, wuq_ref, wukv_ref, rc_ref, rs1_ref, rs2_ref, *rest):
        own, _, riding = _ride_refs(ride, rest, 10, 0)
        dx_ref, dproj_ref, hb_ref, cqn_ref, ckvn_ref, dqpre_ref, dkv_ref, dan_ref, dqn_ref, dkvn_ref = own

        @pl.when(pl.program_id(0) == 0)
        def _():
            dan_ref[...] = jnp.zeros_like(dan_ref)
            dqn_ref[...] = jnp.zeros_like(dqn_ref)
            dkvn_ref[...] = jnp.zeros_like(dkvn_ref)
            if ride:
                ride.start(*riding)

        n, r = _rms(x_ref[...])
        hb = (n * an_ref[...]).astype(BF16)
        hb_ref[...] = hb
        pa = _dot(hb, win_ref[:, C_CQ:C_KPE])
        ncq, rq = _rms(pa[:, 0:256])
        cqn_ref[...] = (ncq * qn_ref[...]).astype(BF16)
        nkv, rkv = _rms(pa[:, 256:384])
        ckvn_ref[...] = (nkv * kvn_ref[...]).astype(BF16)
        c, s1, s2 = rc_ref[...], rs1_ref[...], rs2_ref[...]

        dkper = jnp.zeros((ts, LANE), F32)
        for h in range(NH):
            dqpre_ref[:, _hs(h)] = _rope_bwd(dq_ref[:, _hs(h)], c, s1, s2).astype(BF16)
            dkh = dk_ref[:, _hs(h)]
            dkper = dkper + dkh
            dkv_ref[:, _hs(h)] = dkh.astype(BF16)
        dkv_ref[:, NH * LANE:] = dv_ref[...].astype(BF16)

        dcqn = _dot_nt(dqpre_ref[...], wuq_ref[...])
        dqn_ref[...] += _colsum(dcqn * ncq)
        dproj_ref[:, C_CQ:C_CKV] = _rms_bwd(dcqn * qn_ref[...], ncq, rq).astype(BF16)
        dckvn = _dot_nt(dkv_ref[...], wukv_ref[...])
        dkvn_ref[...] += _colsum(dckvn * nkv)
        dproj_ref[:, C_CKV:C_KPE] = _rms_bwd(dckvn * kvn_ref[...], nkv, rkv).astype(BF16)
        lane = lax.broadcasted_iota(jnp.int32, (ts, LANE), 1)
        dkpe = jnp.where((lane >= 64) & (lane < 96), _rope_bwd(dkper, c, s1, s2), 0.0)
        dproj_ref[:, C_KPE:C_QS] = dkpe.astype(BF16)
        dproj_ref[:, C_QS:C_KS] = dqs_ref[...]
        dproj_ref[:, C_KS:C_VS] = dks_ref[...].astype(BF16)
        dproj_ref[:, C_VS:C_QM] = dvs_ref[...].astype(BF16)
        dproj_ref[:, C_QM:C_G] = dqm_ref[...]
        dproj_ref[:, C_G:C_END] = dgp_ref[...]

        dh = _dot_nt(dproj_ref[...], win_ref[...])
        dan_ref[...] += _colsum(dh * n)
        dx_ref[...] = dx1_ref[...] + _rms_bwd(dh * an_ref[...], n, r)

        if ride:
            @pl.when(pl.program_id(0) == S // ts - 1)
            def _():
                ride.finish(*riding)

    r_in, r_out, r_shapes, r_sems, r_args = _ride_specs(ride)
    results = _pcall(
        body, name="pre_bwd", grid=(S // ts,),
        in_specs=[_tok(ts, D), _tok(ts, D), _tok(ts, 1024), _tok(ts, 1024), _tok(ts, 1024), _tok(ts, 1024),
                  _tok(ts, 128), _tok(ts, 128), _tok(ts, 512), _tok(ts, 3 * D),
                  _full(1, D), _full(1, 256), _full(1, 128), _full(D, C_END), _full(256, NH * LANE),
                  _full(128, 2 * NH * LANE), _tok(ts, LANE), _tok(ts, LANE), _tok(ts, LANE)] + r_in,
        out_specs=[_tok(ts, D), _tok(ts, C_END), _tok(ts, D), _tok(ts, 256), _tok(ts, 128), _tok(ts, 1024),
                   _tok(ts, 2048), _full(1, D), _full(1, 256), _full(1, 128)] + r_out,
        out_shape=[_sds((S, D), F32), _sds((S, C_END), BF16), _sds((S, D), BF16), _sds((S, 256), BF16),
                   _sds((S, 128), BF16), _sds((S, 1024), BF16), _sds((S, 2048), BF16), _sds((1, D), F32),
                   _sds((1, 256), F32), _sds((1, 128), F32)] + r_shapes,
        scratch=r_sems, sem=("arbitrary",),
    )(x, dx1, dq, dk, dv, dqs, dks, dvs, dqm, dgp, an, qn, kvn, win, wuq, wukv, rc, rs1, rs2, *r_args)
    return (*results[:10], list(results[10:]))


def _pick_tile(n, cap):
    best = LANE
    for t in range(LANE, min(n, cap) + 1, LANE):
        if n % t == 0:
            best = t
    return best


def _matmul_tn(a, b, name, shard_axis=None):
    S, M = a.shape
    N = b.shape[1]
    tm = _pick_tile(M // 4 if shard_axis == 0 else M, 1024)
    tn = _pick_tile(N // 4 if shard_axis == 1 else N, 2048)
    ts = min(S, 1024)
    nk = S // ts

    def body(a_ref, b_ref, o_ref, *acc):
        acc_ref = acc[0] if acc else o_ref

        @pl.when(pl.program_id(2) == 0)
        def _():
            acc_ref[...] = jnp.zeros_like(acc_ref)

        acc_ref[...] += _dot_tn(a_ref[...], b_ref[...])
        if acc:
            @pl.when(pl.program_id(2) == nk - 1)
            def _():
                o_ref[0] = acc_ref[...].astype(o_ref.dtype)

    if shard_axis is None:
        out_spec = pl.BlockSpec((tm, tn), lambda i, j, k: (i, j))
        out_shape, scratch = _sds((M, N), F32), ()
    elif shard_axis == 0:
        per = (M // 4) // tm
        out_spec = pl.BlockSpec((1, tm, tn), lambda i, j, k: (i // per, i % per, j))
        out_shape, scratch = _sds((4, M // 4, N), BF16), (pltpu.VMEM((tm, tn), F32),)
    else:
        per = (N // 4) // tn
        out_spec = pl.BlockSpec((1, tm, tn), lambda i, j, k: (j // per, i, j % per))
        out_shape, scratch = _sds((4, M, N // 4), BF16), (pltpu.VMEM((tm, tn), F32),)
    return _pcall(
        body, name=name, grid=(M // tm, N // tn, nk),
        in_specs=[pl.BlockSpec((ts, tm), lambda i, j, k: (k, i)), pl.BlockSpec((ts, tn), lambda i, j, k: (k, j))],
        out_specs=out_spec, out_shape=out_shape, scratch=scratch, sem=("parallel", "parallel", "arbitrary"),
    )(a, b)


def _select_fwd(a, sel, name):
    _, L, M, K = a.shape
    N = sel.shape[2]
    tn = _pick_tile(N, 1792)

    def body(a_ref, s_ref, o_ref, acc_ref):
        s = pl.program_id(2)

        @pl.when(s == 0)
        def _():
            acc_ref[...] = jnp.zeros_like(acc_ref)

        acc_ref[...] += _dot(a_ref[0, 0], s_ref[0])

        @pl.when(s == 3)
        def _():
            o_ref[0] = acc_ref[...].astype(BF16)

    return _pcall(
        body, name=name, grid=(L, N // tn, 4),
        in_specs=[pl.BlockSpec((1, 1, M, K), lambda l, j, s: (s, l, 0, 0)),
                  pl.BlockSpec((1, K, tn), lambda l, j, s: (s, 0, j))],
        out_specs=pl.BlockSpec((1, M, tn), lambda l, j, s: (l, 0, j)),
        out_shape=_sds((L, M, N), BF16), scratch=(pltpu.VMEM((M, tn), F32),),
        sem=("parallel", "parallel", "arbitrary"),
    )(a, sel)


def _select_bwd(dw, sel, name):
    L, M, N = dw.shape
    K = sel.shape[1]
    tk = _pick_tile(N, 1792)
    nk = N // tk

    def body(d_ref, s_ref, o_ref, acc_ref):
        k = pl.program_id(2)

        @pl.when(k == 0)
        def _():
            acc_ref[...] = jnp.zeros_like(acc_ref)

        acc_ref[...] += _dot_nt(d_ref[0].astype(BF16), s_ref[0])

        @pl.when(k == nk - 1)
        def _():
            o_ref[0, 0] = acc_ref[...].astype(BF16)

    return _pcall(
        body, name=name, grid=(L, 4, nk),
        in_specs=[pl.BlockSpec((1, M, tk), lambda l, s, k: (l, 0, k)),
                  pl.BlockSpec((1, K, tk), lambda l, s, k: (s, 0, k))],
        out_specs=pl.BlockSpec((1, 1, M, K), lambda l, s, k: (l, s, 0, 0)),
        out_shape=_sds((L, 4, M, K), BF16), scratch=(pltpu.VMEM((M, K), F32),),
        sem=("parallel", "parallel", "arbitrary"),
    )(dw, sel)


MLA_RC = 128
MLA_RC_FWD = 128
MLA_AHEAD = 4
MLA_HEADS_FWD = 2
MLA_HEADS_BWD = 2
LOG2E = math.log2(math.e)
MLA_QSCALE = MLA_SCALE * LOG2E


def _ride_refs(ride, rest, n_out, n_scratch):
    ni, no = (len(ride.inputs), len(ride.out_shapes)) if ride else (0, 0)
    own_out = rest[ni:ni + n_out]
    own_scratch = rest[ni + n_out + no:ni + n_out + no + n_scratch]
    parts = rest[:ni], rest[ni + n_out:ni + n_out + no], rest[ni + n_out + no + n_scratch:]
    return own_out, own_scratch, parts


def _ride_specs(ride):
    hbm = pl.BlockSpec(memory_space=pl.ANY)
    if not ride:
        return [], [], [], [], []
    return ([hbm] * len(ride.inputs), [hbm] * len(ride.out_shapes), list(ride.out_shapes), list(ride.sems),
            list(ride.inputs))


def _mla_fwd(q, k, v, tq, ride=None):
    S = q.shape[0]
    nq = S // tq
    hp = MLA_HEADS_FWD
    pairs = [(i, j) for i in range(nq) for j in range(i + 1)]
    qi = jnp.asarray(np.array([p[0] for p in pairs], np.int32))
    kj = jnp.asarray(np.array([p[1] for p in pairs], np.int32))

    def body(qi_ref, kj_ref, q_ref, k_ref, v_ref, *rest):
        (o_ref, lse_ref), (m_s, l_s, acc_s), riding = _ride_refs(ride, rest, 2, 3)
        t = pl.program_id(1)
        i, j = qi_ref[t], kj_ref[t]
        if ride:
            @pl.when((pl.program_id(0) == 0) & (t == 0))
            def _():
                ride.start(*riding)

        @pl.when(j == 0)
        def _():
            m_s[...] = jnp.full_like(m_s, NEG)
            l_s[...] = jnp.zeros_like(l_s)
            acc_s[...] = jnp.zeros_like(acc_s)

        def step(masked):
            rc = min(MLA_RC_FWD, tq)
            nc = tq // rc
            keys = [(c + 1) * rc if masked else tq for c in range(nc)]
            units = [(hh, c) for hh in range(hp) for c in range(nc)]
            ahead = min(MLA_AHEAD, len(units))

            def qk(u):
                hh, c = units[u]
                return _dot_nt(q_ref[c * rc:(c + 1) * rc, _hs(hh)], k_ref[0:keys[c], _hs(hh)])

            scores = [qk(u) for u in range(ahead)]
            for u, (hh, c) in enumerate(units):
                rows = slice(c * rc, (c + 1) * rc)
                s = scores[u]
                if masked:
                    row = lax.broadcasted_iota(jnp.int32, (rc, keys[c]), 0) + c * rc
                    col = lax.broadcasted_iota(jnp.int32, (rc, keys[c]), 1)
                    s = jnp.where(col <= row, s, NEG)
                tiles = [s[:, _hs(w)] for w in range(keys[c] // LANE)]
                mx = functools.reduce(jnp.maximum, tiles)
                m_old = m_s[rows, _hs(hh)]
                m_new = jnp.maximum(m_old, jnp.max(mx, axis=1, keepdims=True))
                alpha = jnp.exp2(m_old - m_new)
                ps = [jnp.exp2(w - m_new) for w in tiles]
                l_s[rows, _hs(hh)] = alpha * l_s[rows, _hs(hh)] + functools.reduce(jnp.add, ps)
                p = jnp.concatenate([w.astype(BF16) for w in ps], axis=1)
                acc_s[rows, _hs(hh)] = alpha * acc_s[rows, _hs(hh)] + _dot(p, v_ref[0:keys[c], _hs(hh)])
                m_s[rows, _hs(hh)] = m_new
                if u + ahead < len(units):
                    scores.append(qk(u + ahead))

        @pl.when(j < i)
        def _():
            step(False)

        @pl.when(j == i)
        def _():
            step(True)
            for hh in range(hp):
                l = jnp.sum(l_s[:, _hs(hh)], axis=1, keepdims=True)
                o_ref[:, _hs(hh)] = (acc_s[:, _hs(hh)] / l).astype(BF16)
                lse_ref[hh] = jnp.transpose(m_s[:, _hs(hh)] + jnp.log2(l))[0:1, :]

        if ride:
            @pl.when((pl.program_id(0) == NH // hp - 1) & (t == len(pairs) - 1))
            def _():
                ride.finish(*riding)

    qmap = lambda h, t, qi_r, kj_r: (qi_r[t], h)
    kmap = lambda h, t, qi_r, kj_r: (kj_r[t], h)
    r_in, r_out, r_shapes, r_sems, r_args = _ride_specs(ride)
    wide = hp * LANE
    o, lse, *carried = _pcall(
        body, name="mla_fwd", grid=(NH // hp, len(pairs)), prefetch=2,
        in_specs=[pl.BlockSpec((tq, wide), qmap), pl.BlockSpec((tq, wide), kmap), pl.BlockSpec((tq, wide), kmap)]
        + r_in,
        out_specs=[pl.BlockSpec((tq, wide), qmap),
                   pl.BlockSpec((hp, 1, tq), lambda h, t, qi_r, kj_r: (h, 0, qi_r[t]))] + r_out,
        out_shape=[_sds((S, NH * LANE), BF16), _sds((NH, 1, S), F32)] + r_shapes,
        scratch=[pltpu.VMEM((tq, wide), F32), pltpu.VMEM((tq, wide), F32), pltpu.VMEM((tq, wide), F32)] + r_sems,
        sem=("arbitrary", "arbitrary"),
    )(qi, kj, q, k, v, *r_args)
    return o, lse, carried


def _mla_bwd(q, k, v, do, lse, delta, tq, ride=None):
    S = q.shape[0]
    nq = S // tq
    hp = MLA_HEADS_BWD
    pairs = [(i, j) for j in range(nq) for i in range(j, nq)]
    qi = jnp.asarray(np.array([p[0] for p in pairs], np.int32))
    kj = jnp.asarray(np.array([p[1] for p in pairs], np.int32))

    def body(qi_ref, kj_ref, q_ref, k_ref, v_ref, do_ref, lse_ref, dl_ref, *rest):
        (dq_ref, dk_ref, dv_ref), (dq_s, dk_s, dv_s), riding = _ride_refs(ride, rest, 3, 3)
        t = pl.program_id(1)
        i, j = qi_ref[t], kj_ref[t]
        if ride:
            @pl.when((pl.program_id(0) == 0) & (t == 0))
            def _():
                ride.start(*riding)

        @pl.when(t == 0)
        def _():
            dq_s[...] = jnp.zeros_like(dq_s)

        @pl.when(i == j)
        def _():
            dk_s[...] = jnp.zeros_like(dk_s)
            dv_s[...] = jnp.zeros_like(dv_s)

        qrows = pl.ds(pl.multiple_of(i * tq, tq), tq)

        def step(masked):
            dq = [jnp.zeros((tq, LANE), F32) for _ in range(hp)]
            nc = tq // MLA_RC
            units = [(hh, c) for hh in range(hp) for c in range(nc)]
            q0 = [c * MLA_RC if masked else 0 for c in range(nc)]
            sts = [_dot_nt(k_ref[c * MLA_RC:(c + 1) * MLA_RC, _hs(hh)], q_ref[q0[c]:, _hs(hh)]) for hh, c in units]
            dpts = [_dot_nt(v_ref[c * MLA_RC:(c + 1) * MLA_RC, _hs(hh)], do_ref[q0[c]:, _hs(hh)]) for hh, c in units]
            for u, (hh, c) in enumerate(units):
                rows = slice(c * MLA_RC, (c + 1) * MLA_RC)
                qb, dob = q_ref[q0[c]:, _hs(hh)], do_ref[q0[c]:, _hs(hh)]
                pt = jnp.exp2(sts[u] - lse_ref[hh][:, q0[c]:])
                if masked:
                    key = lax.broadcasted_iota(jnp.int32, (MLA_RC, tq - q0[c]), 0)
                    qry = lax.broadcasted_iota(jnp.int32, (MLA_RC, tq - q0[c]), 1)
                    pt = jnp.where(key <= qry, pt, 0.0)
                dv_s[rows, _hs(hh)] += _dot(pt.astype(BF16), dob)
                gt = (pt * (dpts[u] - dl_ref[hh][:, q0[c]:])).astype(BF16)
                dk_s[rows, _hs(hh)] += _dot(gt, qb)
                part = _dot_tn(gt, k_ref[rows, _hs(hh)])
                if masked:
                    at = pl.multiple_of(i * tq + q0[c], MLA_RC)
                    dq_s[pl.ds(at, tq - q0[c]), _hs(hh)] += part
                else:
                    dq[hh] = dq[hh] + part
            if not masked:
                for hh in range(hp):
                    dq_s[qrows, _hs(hh)] += dq[hh]

        @pl.when(i > j)
        def _():
            step(False)

        @pl.when(i == j)
        def _():
            step(True)
            dq_ref[...] = dq_s[qrows, :] * MLA_SCALE

        @pl.when(i == nq - 1)
        def _():
            dk_ref[...] = dk_s[...] * (1.0 / LOG2E)
            dv_ref[...] = dv_s[...]

        if ride:
            @pl.when((pl.program_id(0) == NH // hp - 1) & (t == len(pairs) - 1))
            def _():
                ride.finish(*riding)

    qmap = lambda h, t, qi_r, kj_r: (qi_r[t], h)
    kmap = lambda h, t, qi_r, kj_r: (kj_r[t], h)
    rmap = lambda h, t, qi_r, kj_r: (h, 0, qi_r[t])
    r_in, r_out, r_shapes, r_sems, r_args = _ride_specs(ride)
    wide = hp * LANE
    dq, dk, dv, *carried = _pcall(
        body, name="mla_bwd", grid=(NH // hp, len(pairs)), prefetch=2,
        in_specs=[pl.BlockSpec((tq, wide), qmap), pl.BlockSpec((tq, wide), kmap), pl.BlockSpec((tq, wide), kmap),
                  pl.BlockSpec((tq, wide), qmap), pl.BlockSpec((hp, 1, tq), rmap), pl.BlockSpec((hp, 1, tq), rmap)]
        + r_in,
        out_specs=[pl.BlockSpec((tq, wide), kmap), pl.BlockSpec((tq, wide), kmap), pl.BlockSpec((tq, wide), kmap)]
        + r_out,
        out_shape=[_sds((S, NH * LANE), F32), _sds((S, NH * LANE), F32), _sds((S, NH * LANE), F32)] + r_shapes,
        scratch=[pltpu.VMEM((S, wide), F32), pltpu.VMEM((tq, wide), F32), pltpu.VMEM((tq, wide), F32)] + r_sems,
        sem=("arbitrary", "arbitrary"),
    )(qi, kj, q, k, v, do, lse, delta, *r_args)
    return dq, dk, dv, carried


SWA_SUB = 4
SWA_T = SWA_SUB * WIN


def _swa_specs(nsteps, rev):
    step = (lambda i: nsteps - 1 - i) if rev else (lambda i: i)
    cur = lambda w: pl.BlockSpec((SWA_T, w), lambda i: (step(i), 0))
    prev = pl.BlockSpec((WIN, LANE), lambda i: (jnp.maximum(step(i) * SWA_SUB - 1, 0), 0))
    return step, cur, prev


def _swa_probs(qk, bias_h, sink, first_mask):
    s = qk * SWA_SCALE + bias_h
    if first_mask is not None:
        s = jnp.where(first_mask, NEG, s)
    m = jnp.maximum(jnp.max(s, axis=1, keepdims=True), sink)
    e = jnp.exp(s - m)
    es = jnp.exp(sink - m)
    inv = 1.0 / (jnp.sum(e, axis=1, keepdims=True) + es)
    return e * inv, es * inv


SWA_GR = SWA_R * WIN


def _swa_group(ref, rows, g):
    return jnp.concatenate([ref[rows, _hs(g * SWA_R + r)] for r in range(SWA_R)], axis=0)


def _swa_rows(bias, sinks):
    sink_rows = jnp.broadcast_to(sinks[:, None, :], (NH, WIN, LANE)).reshape(NH * WIN, LANE)
    return bias.reshape(NH * WIN, 2 * WIN), sink_rows


def _swa_fwd(qs, ks, vs, bias, sinks):
    S = qs.shape[0]
    nsteps = S // SWA_T
    step, cur, prev = _swa_specs(nsteps, False)

    def body(qs_ref, kc_ref, kp_ref, vc_ref, vp_ref, bias_ref, sk_ref, o_ref):
        first = pl.program_id(0) == 0
        kk = jnp.concatenate([kp_ref[...], kc_ref[...]], axis=0)
        vv = jnp.concatenate([vp_ref[...], vc_ref[...]], axis=0)
        col = lax.broadcasted_iota(jnp.int32, (WIN, 2 * WIN), 1)
        qk = lambda b: [_dot_nt(qs_ref[b * WIN:(b + 1) * WIN, _hs(h)], kk[b * WIN:(b + 2) * WIN]) for h in range(NH)]
        ahead = qk(0)
        for b in range(SWA_SUB):
            vvb = vv[b * WIN:(b + 2) * WIN]
            fm = (first & (col < WIN)) if b == 0 else None
            rows = slice(b * WIN, (b + 1) * WIN)
            qks, ahead = ahead, (qk(b + 1) if b + 1 < SWA_SUB else None)
            for h in range(NH):
                p, _ = _swa_probs(qks[h], bias_ref[h], sk_ref[h:h + 1, 0:1], fm)
                o_ref[rows, _hs(h)] = _dot(p.astype(BF16), vvb).astype(BF16)

    return _pcall(
        body, name="swa_fwd", grid=(nsteps,),
        in_specs=[cur(NH * LANE), cur(LANE), prev, cur(LANE), prev, _full(NH, WIN, 2 * WIN), _full(NH, LANE)],
        out_specs=cur(NH * LANE), out_shape=_sds((S, NH * LANE), BF16), sem=("arbitrary",),
    )(qs, ks, ks, vs, vs, bias, sinks)


def _swa_bwd(qs, ks, vs, do, bias, sinks):
    S = qs.shape[0]
    nsteps = S // SWA_T
    step, cur, prev = _swa_specs(nsteps, True)
    bias_rows, sink_rows = _swa_rows(bias, sinks)

    def body(qs_ref, kc_ref, kp_ref, vc_ref, vp_ref, do_ref, bias_ref, sk_ref,
             dqs_ref, dks_ref, dvs_ref, dbias_ref, dsk_ref, dkk_s, dvv_s, ck_s, cv_s):
        pid = pl.program_id(0)
        first = step(pid) == 0

        @pl.when(pid == 0)
        def _():
            dbias_ref[...] = jnp.zeros_like(dbias_ref)
            dsk_ref[...] = jnp.zeros_like(dsk_ref)
            ck_s[...] = jnp.zeros_like(ck_s)
            cv_s[...] = jnp.zeros_like(cv_s)

        dkk_s[...] = jnp.zeros_like(dkk_s)
        dvv_s[...] = jnp.zeros_like(dvv_s)
        kk = jnp.concatenate([kp_ref[...], kc_ref[...]], axis=0)
        vv = jnp.concatenate([vp_ref[...], vc_ref[...]], axis=0)
        col = lax.broadcasted_iota(jnp.int32, (SWA_GR, 2 * WIN), 1)

        def products(b):
            rows = slice(b * WIN, (b + 1) * WIN)
            qg = [_swa_group(qs_ref, rows, g) for g in range(2)]
            dog = [_swa_group(do_ref, rows, g) for g in range(2)]
            return (qg, dog, [_dot_nt(qg[g], kk[b * WIN:(b + 2) * WIN]) for g in range(2)],
                    [_dot_nt(dog[g], vv[b * WIN:(b + 2) * WIN]) for g in range(2)])

        ahead = products(0)
        for b in range(SWA_SUB):
            kkb = kk[b * WIN:(b + 2) * WIN]
            fm = (first & (col < WIN)) if b == 0 else None
            rows = slice(b * WIN, (b + 1) * WIN)
            keys = slice(b * WIN, (b + 2) * WIN)
            (qg, dog, qks, dps), ahead = ahead, (products(b + 1) if b + 1 < SWA_SUB else None)
            for g in range(2):
                grows = slice(g * SWA_GR, (g + 1) * SWA_GR)
                p, ps = _swa_probs(qks[g], bias_ref[grows, :], sk_ref[grows, 0:1], fm)
                dl = jnp.sum(p * dps[g], axis=1, keepdims=True)
                ds = p * (dps[g] - dl)
                sink_part = ps * dl
                for r in range(SWA_R):
                    h = g * SWA_R + r
                    dsk_ref[h:h + 1, :] += -jnp.sum(sink_part[r * WIN:(r + 1) * WIN])
                dbias_ref[grows, :] += ds
                dsb = (ds * SWA_SCALE).astype(BF16)
                dq = _dot(dsb, kkb).astype(BF16)
                for r in range(SWA_R):
                    dqs_ref[rows, _hs(g * SWA_R + r)] = dq[r * WIN:(r + 1) * WIN]
                dkk_s[keys, :] += _dot_tn(dsb, qg[g])
                dvv_s[keys, :] += _dot_tn(p.astype(BF16), dog[g])
        dks_ref[...] = dkk_s[WIN:, :]
        dvs_ref[...] = dvv_s[WIN:, :]
        dks_ref[SWA_T - WIN:, :] += ck_s[...]
        dvs_ref[SWA_T - WIN:, :] += cv_s[...]
        ck_s[...] = dkk_s[0:WIN, :]
        cv_s[...] = dvv_s[0:WIN, :]

    dqs, dks, dvs, dbias, dsink = _pcall(
        body, name="swa_bwd", grid=(nsteps,),
        in_specs=[cur(NH * LANE), cur(LANE), prev, cur(LANE), prev, cur(NH * LANE), _full(NH * WIN, 2 * WIN),
                  _full(NH * WIN, LANE)],
        out_specs=[cur(NH * LANE), cur(LANE), cur(LANE), _full(NH * WIN, 2 * WIN), _full(NH, LANE)],
        out_shape=[_sds((S, NH * LANE), BF16), _sds((S, LANE), F32), _sds((S, LANE), F32),
                   _sds((NH * WIN, 2 * WIN), F32), _sds((NH, LANE), F32)],
        scratch=[pltpu.VMEM((SWA_T + WIN, LANE), F32), pltpu.VMEM((SWA_T + WIN, LANE), F32),
                 pltpu.VMEM((WIN, LANE), F32), pltpu.VMEM((WIN, LANE), F32)],
        sem=("arbitrary",),
    )(qs, ks, ks, vs, vs, do, bias_rows, sink_rows)
    return dqs, dks, dvs, dbias.reshape(NH, WIN, 2 * WIN), dsink


def _bias_build(rel_bias, bmap):
    def body(rb_ref, bmap_ref, o_ref):
        bm = bmap_ref[...]
        for h in range(NH):
            acc = jnp.full((WIN, 2 * WIN), NEG, F32)
            for b in range(REL_BUCKETS):
                acc = jnp.where(bm == b, rb_ref[b, h], acc)
            o_ref[h] = acc

    return _pcall(
        body, name="bias_build", grid=(1,),
        in_specs=[pl.BlockSpec(memory_space=pltpu.SMEM), _full(WIN, 2 * WIN)],
        out_specs=_full(NH, WIN, 2 * WIN), out_shape=_sds((NH, WIN, 2 * WIN), F32), sem=("arbitrary",),
    )(rel_bias, bmap)


def _bias_reduce(dbias, bmap):
    def body(db_ref, bmap_ref, o_ref):
        bm = bmap_ref[...]
        for h in range(NH):
            dbh = db_ref[h]
            for b in range(REL_BUCKETS):
                o_ref[b, h] = jnp.sum(jnp.where(bm == b, dbh, 0.0))

    return _pcall(
        body, name="bias_reduce", grid=(1,),
        in_specs=[_full(NH, WIN, 2 * WIN), _full(WIN, 2 * WIN)],
        out_specs=pl.BlockSpec(memory_space=pltpu.SMEM), out_shape=_sds((REL_BUCKETS, NH), F32), sem=("arbitrary",),
    )(dbias, bmap)


def _memkv_fwd(mem, mnorm, wkv):
    def body(mem_ref, g_ref, w_ref, o_ref):
        n, _ = _rms(mem_ref[...])
        o_ref[...] = _dot((n * g_ref[...]).astype(BF16), w_ref[...]).astype(BF16)

    return _pcall(
        body, name="memkv_fwd", grid=(1,), in_specs=[_full(MEM_LEN, D), _full(1, D), _full(D, D)],
        out_specs=_full(MEM_LEN, D), out_shape=_sds((MEM_LEN, D), BF16), sem=("arbitrary",),
    )(mem, mnorm, wkv)


def _mem_probs(qk):
    s = qk * MEM_SCALE
    e = jnp.exp(s - jnp.max(s, axis=1, keepdims=True))
    return e / jnp.sum(e, axis=1, keepdims=True)


def _mem_fwd(qm, kvm, ts):
    S = qm.shape[0]

    def body(q_ref, kv_ref, o_ref):
        qks = [_dot_nt(q_ref[:, _hs(h)], kv_ref[:, _hs(h)]) for h in range(MEM_H)]
        for h in range(MEM_H):
            p = _mem_probs(qks[h])
            o_ref[:, _hs(h)] = _dot(p.astype(BF16), kv_ref[:, _hs(MEM_H + h)]).astype(BF16)

    return _pcall(
        body, name="mem_fwd", grid=(S // ts,), in_specs=[_tok(ts, 512), _full(MEM_LEN, D)],
        out_specs=_tok(ts, 512), out_shape=_sds((S, 512), BF16), sem=("arbitrary",),
    )(qm, kvm)


def _mem_bwd(qm, kvm, do, ts):
    S = qm.shape[0]

    def body(q_ref, kv_ref, do_ref, dq_ref, dkv_ref):
        @pl.when(pl.program_id(0) == 0)
        def _():
            dkv_ref[...] = jnp.zeros_like(dkv_ref)

        qks = [_dot_nt(q_ref[:, _hs(h)], kv_ref[:, _hs(h)]) for h in range(MEM_H)]
        dps = [_dot_nt(do_ref[:, _hs(h)], kv_ref[:, _hs(MEM_H + h)]) for h in range(MEM_H)]
        for h in range(MEM_H):
            qh, kh, doh = q_ref[:, _hs(h)], kv_ref[:, _hs(h)], do_ref[:, _hs(h)]
            p = _mem_probs(qks[h])
            dp = dps[h]
            ds = (p * (dp - jnp.sum(p * dp, axis=1, keepdims=True)) * MEM_SCALE).astype(BF16)
            dq_ref[:, _hs(h)] = _dot(ds, kh).astype(BF16)
            dkv_ref[:, _hs(h)] += _dot_tn(ds, qh)
            dkv_ref[:, _hs(MEM_H + h)] += _dot_tn(p.astype(BF16), doh)

    return _pcall(
        body, name="mem_bwd", grid=(S // ts,), in_specs=[_tok(ts, 512), _full(MEM_LEN, D), _tok(ts, 512)],
        out_specs=[_tok(ts, 512), _full(MEM_LEN, D)],
        out_shape=[_sds((S, 512), BF16), _sds((MEM_LEN, D), F32)], sem=("arbitrary",),
    )(qm, kvm, do)


def _memkv_bwd(mem, mnorm, wkv, dkvm):
    def body(mem_ref, g_ref, w_ref, dkv_ref, dw_ref, dg_ref):
        n, _ = _rms(mem_ref[...])
        dkvb = dkv_ref[...].astype(BF16)
        dw_ref[...] = _dot_tn((n * g_ref[...]).astype(BF16), dkvb)
        dg_ref[...] = _colsum(_dot_nt(dkvb, w_ref[...]) * n)

    return _pcall(
        body, name="memkv_bwd", grid=(1,), in_specs=[_full(MEM_LEN, D), _full(1, D), _full(D, D), _full(MEM_LEN, D)],
        out_specs=[_full(D, D), _full(1, D)], out_shape=[_sds((D, D), F32), _sds((1, D), F32)], sem=("arbitrary",),
    )(mem, mnorm, wkv, dkvm)


def _adamw(w, g, m, v, name):
    rows, cols = w.shape
    tr = min(rows, FLAT_TILE)
    assert rows % tr == 0

    def body(w_ref, g_ref, m_ref, v_ref, d_ref, nm_ref, nv_ref):
        gv = g_ref[...]
        nm = ADAM_B1 * m_ref[...] + (1.0 - ADAM_B1) * gv
        nv = ADAM_B2 * v_ref[...] + (1.0 - ADAM_B2) * jnp.square(gv)
        m_hat = nm / (1.0 - ADAM_B1 ** ADAM_STEP)
        v_hat = nv / (1.0 - ADAM_B2 ** ADAM_STEP)
        d_ref[...] = -ADAM_LR * (m_hat / (jnp.sqrt(v_hat) + ADAM_EPS) + ADAM_WD * w_ref[...])
        nm_ref[...] = nm
        nv_ref[...] = nv

    spec = _tok(tr, cols)
    return _pcall(
        body, name=name, grid=(rows // tr,), in_specs=[spec] * 4, out_specs=[spec] * 3,
        out_shape=[_sds((rows, cols), F32)] * 3, sem=("arbitrary",),
    )(w, g, m, v)


def _my_place():
    return lax.axis_index("x"), lax.axis_index("y"), lax.axis_index("c")


def _remote(src, dst, send_sems, recv_sems, k, to):
    return pltpu.make_async_remote_copy(src_ref=src, dst_ref=dst, send_sem=send_sems.at[k], recv_sem=recv_sems.at[k],
                                        device_id=to, device_id_type=MESH_ID)


class _Rider(NamedTuple):
    inputs: list
    out_shapes: list
    sems: list
    start: Callable
    finish: Callable


def _run_rider(rider, name):
    ni, no = len(rider.inputs), len(rider.out_shapes)

    def body(*refs):
        parts = refs[:ni], refs[ni:ni + no], refs[ni + no:]
        rider.start(*parts)
        rider.finish(*parts)

    hbm = pl.BlockSpec(memory_space=pl.ANY)
    return pl.pallas_call(body, name=name, out_shape=rider.out_shapes, in_specs=[hbm] * ni, out_specs=[hbm] * no,
                          scratch_shapes=rider.sems)(*rider.inputs)


def _join_riders(*riders):
    def cut(seq, lens):
        at, out = 0, []
        for n in lens:
            out.append(seq[at:at + n])
            at += n
        return out

    def each(ins, outs, sems):
        return zip(riders, cut(ins, [len(r.inputs) for r in riders]), cut(outs, [len(r.out_shapes) for r in riders]),
                   cut(sems, [len(r.sems) for r in riders]))

    def start(ins, outs, sems):
        for r, i, o, s in each(ins, outs, sems):
            r.start(i, o, s)

    def finish(ins, outs, sems):
        for r, i, o, s in each(ins, outs, sems):
            r.finish(i, o, s)

    return _Rider([a for r in riders for a in r.inputs], [a for r in riders for a in r.out_shapes],
                  [a for r in riders for a in r.sems], start, finish)


def _gather_rider(shards):
    n = len(shards)

    def copies(ins, outs, sems):
        send_sems, recv_sems, local_sems = sems
        x, y, c = _my_place()
        chips = [(1 - x, y), (x, 1 - y), (1 - x, 1 - y)]
        mine = [pltpu.make_async_copy(ins[i], outs[i].at[2 * x + y], local_sems.at[i]) for i in range(n)]

        def copy(i, k, slot, to):
            return _remote(ins[i], outs[i].at[slot], send_sems, recv_sems, 3 * i + k, to)

        sends = [copy(i, k, 2 * x + y, (px, py, c)) for i in range(n) for k, (px, py) in enumerate(chips)]
        lands = [copy(i, k, 2 * px + py, (px, py, c)) for i in range(n) for k, (px, py) in enumerate(chips)]
        return mine, sends, lands

    def start(ins, outs, sems):
        mine, sends, _ = copies(ins, outs, sems)
        for cp in mine + sends:
            cp.start()

    def finish(ins, outs, sems):
        mine, sends, lands = copies(ins, outs, sems)
        for cp in lands:
            cp.wait_recv()
        for cp in sends:
            cp.wait_send()
        for cp in mine:
            cp.wait()

    return _Rider(list(shards), [_sds((4,) + s.shape, s.dtype) for s in shards],
                  [pltpu.SemaphoreType.DMA((3 * n,)), pltpu.SemaphoreType.DMA((3 * n,)), pltpu.SemaphoreType.DMA((n,))],
                  start, finish)


def _scatter_rider(gws, layer, gsmall=None):
    n = len(gws)
    per = n + (gsmall is not None)

    def plan(ins, outs, sems):
        send_sems, recv_sems, local_sems = sems
        x, y, c = _my_place()
        me = 4 * x + 2 * y + c

        def peer(m):
            return (x ^ ((m >> 2) & 1), y ^ ((m >> 1) & 1), c ^ (m & 1))

        def slot_of(m):
            px, py, pc = peer(m)
            return 4 * px + 2 * py + pc

        def piece(i, m, slot):
            px, py, pc = peer(m)
            return _remote(ins[i].at[2 * px + py], outs[i].at[slot], send_sems, recv_sems, (m - 1) * per + i,
                           (px, py, pc))

        def small(m, slot):
            return _remote(ins[n], outs[n].at[slot], send_sems, recv_sems, (m - 1) * per + n, peer(m))

        own = [pltpu.make_async_copy(ins[i].at[2 * x + y], outs[i].at[me], local_sems.at[i]) for i in range(n)]
        return c, me, slot_of, piece, small, own

    def start(ins, outs, sems):
        c, me, slot_of, piece, small, own = plan(ins, outs, sems)

        @pl.when(c == layer)
        def _():
            for cp in own:
                cp.start()
            for m in (2, 4, 6):
                for i in range(n):
                    piece(i, m, me).start()

        @pl.when(c != layer)
        def _():
            for m in (1, 3, 5, 7):
                for i in range(n):
                    piece(i, m, me).start()

        if gsmall is not None:
            pltpu.make_async_copy(ins[n], outs[n].at[me], sems[2].at[n]).start()
            for m in range(1, 8):
                small(m, me).start()

    def finish(ins, outs, sems):
        c, me, slot_of, piece, small, own = plan(ins, outs, sems)

        @pl.when(c == layer)
        def _():
            for m in range(1, 8):
                for i in range(n):
                    piece(i, m, slot_of(m)).wait_recv()
            for m in (2, 4, 6):
                for i in range(n):
                    piece(i, m, me).wait_send()
            for cp in own:
                cp.wait()

        @pl.when(c != layer)
        def _():
            for m in (1, 3, 5, 7):
                for i in range(n):
                    piece(i, m, me).wait_send()

        if gsmall is not None:
            for m in range(1, 8):
                small(m, slot_of(m)).wait_recv()
            for m in range(1, 8):
                small(m, me).wait_send()
            pltpu.make_async_copy(ins[n], outs[n].at[me], sems[2].at[n]).wait()

    inputs = list(gws) + ([gsmall] if gsmall is not None else [])
    shapes = [_sds((8,) + g.shape[1:], g.dtype) for g in gws]
    if gsmall is not None:
        shapes.append(_sds((8,) + gsmall.shape, gsmall.dtype))
    nsem = 7 * per
    return _Rider(inputs, shapes, [pltpu.SemaphoreType.DMA((nsem,)), pltpu.SemaphoreType.DMA((nsem,)),
                                   pltpu.SemaphoreType.DMA((per,))], start, finish)


def _sum_slots(slots, name, other=None):
    _, r, c = slots.shape
    tr = min(r, FLAT_TILE)
    assert r % tr == 0
    out_dtype = F32 if other is None else BF16

    def total(ref):
        acc = ref[0].astype(F32)
        for d in range(1, 8):
            acc = acc + ref[d].astype(F32)
        return acc

    def body(*refs):
        o_ref = refs[-1]
        if other is None:
            o_ref[...] = total(refs[0])
        else:
            core = lax.axis_index("c")

            @pl.when(core == 0)
            def _():
                o_ref[...] = total(refs[0]).astype(out_dtype)

            @pl.when(core == 1)
            def _():
                o_ref[...] = total(refs[1]).astype(out_dtype)

    spec = pl.BlockSpec((8, tr, c), lambda i: (0, i, 0))
    ins = [slots] if other is None else [slots, other]
    return _pcall(
        body, name=name, grid=(r // tr,), in_specs=[spec] * len(ins),
        out_specs=_tok(tr, c), out_shape=_sds((r, c), out_dtype), sem=("arbitrary",),
    )(*ins)


def _swap_rider(reds):
    n = len(reds)

    def copies(ins, outs, sems):
        send_sems, recv_sems, local_sems = sems
        x, y, c = _my_place()
        sibling = (x, y, 1 - c)
        mine = [pltpu.make_async_copy(ins[i], outs[i].at[c], local_sems.at[i]) for i in range(n)]
        sends = [_remote(ins[i], outs[i].at[c], send_sems, recv_sems, i, sibling) for i in range(n)]
        lands = [_remote(ins[i], outs[i].at[1 - c], send_sems, recv_sems, i, sibling) for i in range(n)]
        return mine, sends, lands

    def start(ins, outs, sems):
        mine, sends, _ = copies(ins, outs, sems)
        for cp in mine + sends:
            cp.start()

    def finish(ins, outs, sems):
        mine, sends, lands = copies(ins, outs, sems)
        for cp in lands:
            cp.wait_recv()
        for cp in sends:
            cp.wait_send()
        for cp in mine:
            cp.wait()

    return _Rider(list(reds), [_sds((2,) + r.shape, r.dtype) for r in reds],
                  [pltpu.SemaphoreType.DMA((n,)), pltpu.SemaphoreType.DMA((n,)), pltpu.SemaphoreType.DMA((n,))],
                  start, finish)


W_IN_SHARD, W_IN_SHARD_PAD = 1192, 1280
W_UQ_SHARD, W_UQ_SHARD_PAD = 192, 256


def _swa_place(t):
    z = jnp.zeros_like(t)
    lo = jnp.concatenate([t, z], axis=1)
    hi = jnp.concatenate([z, t], axis=1)
    group = (jnp.arange(NH) // SWA_R).reshape((NH,) + (1,) * (t.ndim - 1))
    full = jnp.where(group == 0, lo, hi)
    return full.reshape((NH * LANE,) + t.shape[2:])


def _swa_unplace(t):
    t = t.reshape((NH, 2, 64) + t.shape[1:])
    return jnp.concatenate([t[:SWA_R, 0], t[SWA_R:, 1]], axis=0)


def _pad_w_o_mla(w):
    return jnp.pad(w.reshape(NH, 64, D), ((0, 0), (0, 64), (0, 0))).reshape(NH * LANE, D)


def _unpad_w_o_mla(g):
    return g.reshape(NH, LANE, D)[:, :64].reshape(NH * 64, D)


def _w_in_cols():
    src = np.full((C_END,), -1, np.int64)
    src[C_CQ:C_KPE] = np.arange(0, 384)
    src[C_KPE + 64:C_KPE + 96] = np.arange(384, 416)
    for h in range(NH):
        at = C_QS + h * LANE + 64 * (h // SWA_R)
        src[at:at + 64] = 416 + h * 64 + np.arange(64)
    src[C_KS:C_END] = np.arange(928, IN_COLS)
    return src


def _w_uq_cols():
    src = np.full((NH * LANE,), -1, np.int64)
    for h in range(NH):
        src[h * LANE:h * LANE + 96] = h * 96 + np.arange(96)
    return src


def _w_ukv_cols():
    src = np.full((2 * NH * LANE,), -1, np.int64)
    for h in range(NH):
        src[h * LANE:h * LANE + 64] = h * 128 + np.arange(64)
        src[NH * LANE + h * LANE:NH * LANE + h * LANE + 64] = h * 128 + 64 + np.arange(64)
    return src


def _selection(src_cols, width, width_pad):
    want = jnp.asarray(np.asarray(src_cols, np.int32))[None, None, :]
    k = jnp.arange(width_pad, dtype=jnp.int32)[None, :, None]
    have = jnp.where(k < width, jnp.arange(4, dtype=jnp.int32)[:, None, None] * width + k, -2)
    return (want == have).astype(BF16)


def _selections():
    return dict(w_in=_selection(_w_in_cols(), W_IN_SHARD, W_IN_SHARD_PAD),
                w_uq=_selection(_w_uq_cols(), W_UQ_SHARD, W_UQ_SHARD_PAD),
                w_ukv=_selection(_w_ukv_cols(), 256, 256))


def _pad_last(a, width):
    return jnp.pad(a, ((0, 0),) * (a.ndim - 1) + ((0, width - a.shape[-1]),))


def _wire_shards(W, l):
    out = {n: W[n][l].astype(BF16) for n in SHARDED}
    out["w_in"] = _pad_last(out["w_in"], W_IN_SHARD_PAD)
    out["w_uq"] = _pad_last(out["w_uq"], W_UQ_SHARD_PAD)
    return [out[n] for n in SHARDED]


def _join_shards(t, axis):
    _, L, r, c = t.shape
    if axis == 2:
        return t.transpose(1, 2, 0, 3).reshape(L, r, 4 * c)
    return t.transpose(1, 0, 2, 3).reshape(L, 4 * r, c)


EARLY = SHARDED[:3]
LATE = SHARDED[3:]


def _kernel_weights_early(gathered, sels):
    lay = {n: _select_fwd(a[:, None], sels[n], "lay_" + n)[0] for n, a in zip(EARLY, gathered)}
    return dict(win=lay["w_in"], wuq=lay["w_uq"], wukv=lay["w_ukv"])


def _kernel_weights_late(gathered):
    whole = {n: _join_shards(a[:, None], SHARD_AXIS[n])[0] for n, a in zip(LATE, gathered)}
    return dict(wmem=whole["w_mem_kv"], wa=_pad_w_o_mla(whole["w_o_mla"]),
                wb=_swa_place(whole["w_o_swa"].reshape(NH, 64, D)), wc=whole["w_o_mem"], wout=whole["w_out"],
                wup=whole["w_up"], wdown=whole["w_down"])


def _kernel_weights(gathered, sels):
    return {**_kernel_weights_early(gathered[:3], sels), **_kernel_weights_late(gathered[3:])}


def _cols_to_shards(g):
    r, c4 = g.shape
    return g.reshape(r, 4, c4 // 4).transpose(1, 0, 2).astype(BF16)


def _rope_tables(S):
    pos = jnp.arange(S, dtype=F32)
    inv = 1.0 / (ROPE_THETA ** (jnp.arange(0, 32, 2, dtype=F32) / 32))
    ang = pos[:, None] * inv[None, :]
    cos, sin = jnp.cos(ang), jnp.sin(ang)
    one, zero = jnp.ones((S, 64), F32), jnp.zeros((S, 16), F32)
    rc = jnp.concatenate([one, cos, cos, jnp.ones((S, 32), F32)], axis=1)
    rs1 = jnp.concatenate([jnp.zeros((S, 64), F32), zero, sin, jnp.zeros((S, 32), F32)], axis=1)
    rs2 = jnp.concatenate([jnp.zeros((S, 64), F32), -sin, zero, jnp.zeros((S, 32), F32)], axis=1)
    return rc, rs1, rs2


def _bucket_map():
    qi = jnp.arange(WIN)[:, None]
    kj = jnp.arange(2 * WIN)[None, :]
    dist = qi + WIN - kj
    n = jnp.maximum(dist, 0)
    max_exact = REL_BUCKETS // 2
    nf = jnp.maximum(n, 1).astype(F32)
    large = max_exact + (jnp.log(nf / max_exact) / math.log(128 / max_exact)
                         * (REL_BUCKETS - max_exact)).astype(jnp.int32)
    large = jnp.minimum(large, REL_BUCKETS - 1)
    bucket = jnp.where(n < max_exact, n, large)
    return jnp.where((dist >= 0) & (dist < WIN), bucket, -1).astype(jnp.int32)


TS = 256
TQ = 1024
TQ_FWD = 1024


def _local_step(x, mem, tgt, kw0, sp, sels, kw1=None, wire=None):
    S = x.shape[0]
    ts = min(TS, S)
    tq = min(TQ, S)
    rc, rs1, rs2 = _rope_tables(S)
    bmap = _bucket_map()
    bias = _bias_build(sp["rel_bias"], bmap)
    row = lambda v: v.reshape(1, -1)

    saved = []
    kw = [dict(kw0), kw1]
    for l in range(DEPTH):
        w = kw[l]
        an, qn, kvn = row(sp["attn_norm"][l]), row(sp["mla_q_norm"][l]), row(sp["mla_kv_norm"][l])
        bg, mnorm, mlpn = row(sp["b_gate"][l]), row(sp["mem_norm"][l]), row(sp["mlp_norm"][l])
        sinks = jnp.broadcast_to(sp["attn_sinks"][l][:, None], (NH, LANE))
        q, k, v, qs, ks, vs, qm, g = _pre_fwd(x, an, w["win"], bg, qn, kvn, w["wuq"], w["wukv"], rc, rs1, rs2, ts)
        carry = _join_riders(_gather_rider(wire[0]), _gather_rider(wire[1])) if (l == 0 and wire) else None
        oa, lse, got = _mla_fwd(q, k, v, min(TQ_FWD, S), carry)
        if carry:
            w.update(_kernel_weights_late(got[:len(LATE)]))
            kw[1] = _kernel_weights(got[len(LATE):], sels)
        ob = _swa_fwd(qs, ks, vs, bias, sinks)
        kvm = _memkv_fwd(mem, mnorm, w["wmem"])
        oc = _mem_fwd(qm, kvm, ts)
        x1, yb = _merge_fwd(x, g, oa, ob, oc, w["wa"], w["wb"], w["wc"], w["wout"], ts)
        x2 = _mlp_fwd(x1, mlpn, w["wup"], w["wdown"], ts)
        saved.append(dict(w=w, x=x, x1=x1, q=q, k=k, v=v, qs=qs, ks=ks, vs=vs, qm=qm, g=g, oa=oa, lse=lse, ob=ob,
                          oc=oc, kvm=kvm, yb=yb, an=an, qn=qn, kvn=kvn, mnorm=mnorm, mlpn=mlpn, sinks=sinks))
        x = x2

    sq, dx, dfn = _loss_kernel(x, row(sp["final_norm"]), tgt, ts)

    big = {n: [None] * DEPTH for n in SHARDED}
    small = {n: [None] * DEPTH for n in SMALL if n not in ("rel_bias", "final_norm")}
    dbias_total = None
    slots1 = None
    for l in reversed(range(DEPTH)):
        sv = saved[l]
        w = sv["w"]
        dx1, hb2, dub, ab, dxb, dmlpn = _mlp_bwd(dx, sv["x1"], sv["mlpn"], w["wup"], w["wdown"], ts)
        big["w_up"][l] = _matmul_tn(hb2, dub, "dw_up", shard_axis=1)
        big["w_down"][l] = _matmul_tn(ab, dxb, "dw_down", shard_axis=0)
        small["mlp_norm"][l] = dmlpn[0]

        dgp, dyo, doa, dob, doc, dla, dx1b, dbg = _merge_bwd(dx1, sv["g"], sv["oa"], sv["ob"], sv["oc"], w["wa"],
                                                             w["wb"], w["wc"], w["wout"], ts)
        big["w_out"][l] = _matmul_tn(sv["yb"], dx1b, "dw_out", shard_axis=0)
        big["w_o_mla"][l] = _cols_to_shards(_unpad_w_o_mla(_matmul_tn(sv["oa"], dyo[:, 0:D], "dw_o_mla")))
        big["w_o_swa"][l] = _cols_to_shards(
            _swa_unplace(_matmul_tn(sv["ob"], dyo[:, D:2 * D], "dw_o_swa")).reshape(NH * 64, D))
        big["w_o_mem"][l] = _matmul_tn(sv["oc"], dyo[:, 2 * D:3 * D], "dw_o_mem", shard_axis=1)
        small["b_gate"][l] = dbg[0]

        dqm, dkvm = _mem_bwd(sv["qm"], sv["kvm"], doc, ts)
        dwmem, dmnorm = _memkv_bwd(mem, sv["mnorm"], w["wmem"], dkvm)
        big["w_mem_kv"][l] = dwmem.reshape(4, D // 4, D).astype(BF16)
        small["mem_norm"][l] = dmnorm[0]

        dqs, dks, dvs, dbias, dsink = _swa_bwd(sv["qs"], sv["ks"], sv["vs"], dob, bias, sv["sinks"])
        dbias_total = dbias if dbias_total is None else dbias_total + dbias
        small["attn_sinks"][l] = dsink[:, 0]

        carry = _join_riders(_scatter_rider([big[n][1] for n in SHARDED], 1),
                             _scatter_rider([big[n][0] for n in LATE], 0)) if (l == 0 and wire) else None
        dq, dk, dv, slots = _mla_bwd(sv["q"], sv["k"], sv["v"], doa, sv["lse"], dla, tq, carry)
        if carry:
            late0, late1 = slots[len(SHARDED):], slots[len(EARLY):len(SHARDED)]
            carry = _swap_rider([_sum_slots(s0, "sum_" + n, s1) for n, s0, s1 in zip(LATE, late0, late1)])

        dx, dproj, hb, cqn, ckvn, dqpre, dkv, dan, dqn, dkvn, swapped = _pre_bwd(
            sv["x"], dx1, dq, dk, dv, dqs, dks, dvs, dqm, dgp, sv["an"], sv["qn"], sv["kvn"], w["win"], w["wuq"],
            w["wukv"], rc, rs1, rs2, ts, carry)
        if carry:
            slots1 = (slots[:len(EARLY)], swapped)
        for n, (a, b) in (("w_in", (hb, dproj)), ("w_uq", (cqn, dqpre)), ("w_ukv", (ckvn, dkv))):
            big[n][l] = _select_bwd(_matmul_tn(a, b, "d" + n)[None], sels[n], "shard_d" + n)[0]
        small["attn_norm"][l] = dan[0]
        small["mla_q_norm"][l] = dqn[0]
        small["mla_kv_norm"][l] = dkvn[0]

    gw = [[big[n][l] for n in SHARDED] for l in range(DEPTH)]
    gs = {n: jnp.stack(v) for n, v in small.items()}
    gs["rel_bias"] = _bias_reduce(dbias_total, bmap)
    gs["final_norm"] = dfn[0]
    return sq, dx, gw, slots1, gs


def _flatten(parts, rows):
    flat = jnp.concatenate([p.reshape(-1) for p in parts])
    return jnp.pad(flat, (0, rows * FLAT_W - flat.shape[0])).reshape(rows, FLAT_W)


def _unflatten(buf, shapes):
    flat = buf.reshape(-1)
    out, at = [], 0
    for s in shapes:
        n = int(np.prod(s))
        out.append(flat[at:at + n].reshape(s))
        at += n
    return out


def kernel(x, mem, rel_bias, attn_norm, mem_norm, w_in, b_gate, mla_q_norm, w_uq, mla_kv_norm, w_ukv, attn_sinks, w_mem_kv, w_o_mla, w_o_swa, w_o_mem, w_out, mlp_norm, w_up, w_down, final_norm, loss_target, m_rel_bias, m_attn_norm, m_mem_norm, m_w_in, m_b_gate, m_mla_q_norm, m_w_uq, m_mla_kv_norm, m_w_ukv, m_attn_sinks, m_w_mem_kv, m_w_o_mla, m_w_o_swa, m_w_o_mem, m_w_out, m_mlp_norm, m_w_up, m_w_down, m_final_norm, v_rel_bias, v_attn_norm, v_mem_norm, v_w_in, v_b_gate, v_mla_q_norm, v_w_uq, v_mla_kv_norm, v_w_ukv, v_attn_sinks, v_w_mem_kv, v_w_o_mla, v_w_o_swa, v_w_o_mem, v_w_out, v_mlp_norm, v_w_up, v_w_down, v_final_norm):
    args = dict(locals())
    W = {n: args[n] for n in WEIGHTS}
    M = {n: args["m_" + n] for n in WEIGHTS}
    V = {n: args["v_" + n] for n in WEIGHTS}
    small_shapes = [W[n].shape for n in SMALL]
    sels = _selections()

    wire0, wire1 = _wire_shards(W, 0), _wire_shards(W, 1)
    kw0 = _kernel_weights_early(_run_rider(_gather_rider(wire0[:len(EARLY)]), "gather_weights"), sels)
    sp = {n: W[n] for n in SMALL}

    sq, dx, gw, carried, gs = _local_step(x[0], mem[0], loss_target[0], kw0, sp, sels,
                                          wire=(wire0[len(EARLY):], wire1))

    slots1_early, swapped_late = carried
    gsmall = _flatten([gs[n] for n in SMALL], SMALL_ROWS)
    *slots0_early, small_slots = _run_rider(_scatter_rider(gw[0][:len(EARLY)], 0, gsmall), "scatter_grads")
    reds = [_sum_slots(s0, "sum_" + n, s1) for n, s0, s1 in zip(EARLY, slots0_early, slots1_early)]
    red_small = _sum_slots(small_slots, "sum_small")
    swapped = list(_run_rider(_swap_rider(reds), "swap_layers")) + list(swapped_late)
    G = {n: g.astype(F32) for n, g in zip(SHARDED, swapped)}
    G["w_in"] = G["w_in"][..., :W_IN_SHARD]
    G["w_uq"] = G["w_uq"][..., :W_UQ_SHARD]

    DW, NM, NV = {}, {}, {}
    for n in SHARDED:
        shape = W[n].shape
        two_d = lambda a: a.reshape(-1, shape[-1])
        d, nm, nv = _adamw(two_d(W[n]), two_d(G[n]), two_d(M[n]), two_d(V[n]), "adamw_" + n)
        DW[n], NM[n], NV[n] = d.reshape(shape), nm.reshape(shape), nv.reshape(shape)
    d_s, m_s, v_s = _adamw(_flatten([W[n] for n in SMALL], SMALL_ROWS), red_small,
                           _flatten([M[n] for n in SMALL], SMALL_ROWS), _flatten([V[n] for n in SMALL], SMALL_ROWS),
                           "adamw_small")
    for out, buf in ((G, red_small), (DW, d_s), (NM, m_s), (NV, v_s)):
        out.update(zip(SMALL, _unflatten(buf, small_shapes)))
    loss = lax.psum(0.5 * sq[0, 0] / D, ("x", "y", "c"))
    return (loss, dx[None], *[G[n] for n in WEIGHTS], *[DW[n] for n in WEIGHTS], *[NM[n] for n in WEIGHTS],
            *[NV[n] for n in WEIGHTS])
```

```python
import functools
import math
from typing import Callable, NamedTuple

import numpy as np
import jax
import jax.numpy as jnp
from jax import lax
from jax.experimental import pallas as pl
from jax.experimental.pallas import tpu as pltpu

F32 = jnp.float32
BF16 = jnp.bfloat16

D = 1024
DFF = 4096
DEPTH = 2
EPS = 1e-6
LANE = 128
NH = 8
SWA_R = 4
MEM_H = 4
MEM_LEN = 256
WIN = 128
NEG = -1e30
MLA_SCALE = 96 ** -0.5
SWA_SCALE = 64 ** -0.5
MEM_SCALE = 128 ** -0.5
REL_BUCKETS = 32
ROPE_THETA = 10000.0

C_CQ, C_CKV, C_KPE, C_QS, C_KS, C_VS, C_QM, C_G, C_END = 0, 256, 384, 512, 1536, 1664, 1792, 2304, 5376
IN_COLS = 4768

ADAM_LR = 0.001
ADAM_B1 = 0.9
ADAM_B2 = 0.999
ADAM_EPS = 1e-08
ADAM_WD = 0.01
ADAM_STEP = 10

VMEM_LIMIT = 56 * 1024 * 1024

SHARDED = ("w_in", "w_uq", "w_ukv", "w_mem_kv", "w_o_mla", "w_o_swa", "w_o_mem", "w_out", "w_up", "w_down")
SHARD_AXIS = {"w_in": 2, "w_uq": 2, "w_ukv": 2, "w_mem_kv": 1, "w_o_mla": 2, "w_o_swa": 2, "w_o_mem": 2,
              "w_out": 1, "w_up": 2, "w_down": 1}
SMALL = ("rel_bias", "attn_norm", "mem_norm", "b_gate", "mla_q_norm", "mla_kv_norm", "attn_sinks", "mlp_norm",
         "final_norm")
WEIGHTS = ("rel_bias", "attn_norm", "mem_norm", "w_in", "b_gate", "mla_q_norm", "w_uq", "mla_kv_norm", "w_ukv",
           "attn_sinks", "w_mem_kv", "w_o_mla", "w_o_swa", "w_o_mem", "w_out", "mlp_norm", "w_up", "w_down",
           "final_norm")
FLAT_W = 1024
FLAT_TILE = 256
SMALL_ROWS = 16
MESH_ID = pl.DeviceIdType.MESH


def _pcall(body, *, name, grid, in_specs, out_specs, out_shape, scratch=(), prefetch=0, sem=None):
    params = pltpu.CompilerParams(dimension_semantics=sem, vmem_limit_bytes=VMEM_LIMIT)
    if prefetch:
        spec = pltpu.PrefetchScalarGridSpec(num_scalar_prefetch=prefetch, grid=grid, in_specs=in_specs,
                                            out_specs=out_specs, scratch_shapes=scratch)
        return pl.pallas_call(body, name=name, grid_spec=spec, out_shape=out_shape, compiler_params=params)
    return pl.pallas_call(body, name=name, grid=grid, in_specs=in_specs, out_specs=out_specs, out_shape=out_shape,
                          scratch_shapes=scratch, compiler_params=params)


def _tok(ts, w):
    return pl.BlockSpec((ts, w), lambda i: (i, 0))


def _full(*shape):
    return pl.BlockSpec(shape, lambda *_: (0,) * len(shape))


def _const(*shape):
    return pl.BlockSpec(shape, lambda *_: (0,) * len(shape), pipeline_mode=pl.Buffered(1))


def _sds(shape, dtype):
    return jax.ShapeDtypeStruct(shape, dtype)


def _dot(a, b):
    return jnp.dot(a, b, preferred_element_type=F32)


def _dot_nt(a, b):
    return lax.dot_general(a, b, (((1,), (1,)), ((), ())), preferred_element_type=F32)


def _dot_tn(a, b):
    return lax.dot_general(a, b, (((0,), (0,)), ((), ())), preferred_element_type=F32)


def _rms(x):
    r = lax.rsqrt(jnp.mean(x * x, axis=-1, keepdims=True) + EPS)
    return x * r, r


def _rms_bwd(dyg, n, r):
    return r * (dyg - n * jnp.mean(n * dyg, axis=-1, keepdims=True))


def _rope(t, c, s1, s2):
    return t * c + pltpu.roll(t, 16, 1) * s1 + pltpu.roll(t, LANE - 16, 1) * s2


def _rope_bwd(dy, c, s1, s2):
    return dy * c + pltpu.roll(dy * s1, LANE - 16, 1) + pltpu.roll(dy * s2, 16, 1)


def _hs(h):
    return slice(h * LANE, (h + 1) * LANE)


def _colsum(t):
    return jnp.sum(t, axis=0, keepdims=True)


def _pre_fwd(x, an, win, bg, qn, kvn, wuq, wukv, rc, rs1, rs2, ts):
    S = x.shape[0]

    def body(x_ref, an_ref, win_ref, bg_ref, qn_ref, kvn_ref, wuq_ref, wukv_ref, rc_ref, rs1_ref, rs2_ref,
             q_ref, k_ref, v_ref, qs_ref, ks_ref, vs_ref, qm_ref, g_ref):
        n, _ = _rms(x_ref[...])
        hb = (n * an_ref[...]).astype(BF16)
        pa = _dot(hb, win_ref[:, C_CQ:C_QS])
        ncq, _ = _rms(pa[:, 0:256])
        cqn = (ncq * qn_ref[...]).astype(BF16)
        nkv, _ = _rms(pa[:, 256:384])
        ckvn = (nkv * kvn_ref[...]).astype(BF16)
        c, s1, s2 = rc_ref[...], rs1_ref[...], rs2_ref[...]
        kper = _rope(pa[:, 384:512], c, s1, s2)
        qp = _dot(cqn, wuq_ref[...])
        kv = _dot(ckvn, wukv_ref[...])
        for h in range(NH):
            q_ref[:, _hs(h)] = (_rope(qp[:, _hs(h)], c, s1, s2) * MLA_QSCALE).astype(BF16)
            k_ref[:, _hs(h)] = (kv[:, _hs(h)] + kper).astype(BF16)
        v_ref[...] = kv[:, NH * LANE:].astype(BF16)
        pb = _dot(hb, win_ref[:, C_QS:C_G])
        qs_ref[...] = pb[:, 0:1024].astype(BF16)
        ks_ref[...] = pb[:, 1024:1152].astype(BF16)
        vs_ref[...] = pb[:, 1152:1280].astype(BF16)
        qm_ref[...] = pb[:, 1280:1792].astype(BF16)
        g_ref[...] = jax.nn.sigmoid(_dot(hb, win_ref[:, C_G:C_END]) + bg_ref[...])

    return _pcall(
        body, name="pre_fwd", grid=(S // ts,),
        in_specs=[_tok(ts, D), _full(1, D), _full(D, C_END), _full(1, 3 * D), _full(1, 256), _full(1, 128),
                  _full(256, NH * LANE), _full(128, 2 * NH * LANE), _tok(ts, LANE), _tok(ts, LANE), _tok(ts, LANE)],
        out_specs=[_tok(ts, 1024), _tok(ts, 1024), _tok(ts, 1024), _tok(ts, 1024), _tok(ts, 128), _tok(ts, 128),
                   _tok(ts, 512), _tok(ts, 3 * D)],
        out_shape=[_sds((S, 1024), BF16), _sds((S, 1024), BF16), _sds((S, 1024), BF16), _sds((S, 1024), BF16),
                   _sds((S, 128), BF16), _sds((S, 128), BF16), _sds((S, 512), BF16), _sds((S, 3 * D), F32)],
        sem=("arbitrary",),
    )(x, an, win, bg, qn, kvn, wuq, wukv, rc, rs1, rs2)


def _merge_fwd(x, g, oa, ob, oc, wa, wb, wc, wout, ts):
    S = x.shape[0]

    def body(x_ref, g_ref, oa_ref, ob_ref, oc_ref, wa_ref, wb_ref, wc_ref, wout_ref, x1_ref, yb_ref):
        y = g_ref[:, 0:D] * _dot(oa_ref[...], wa_ref[...])
        y = y + g_ref[:, D:2 * D] * _dot(ob_ref[...], wb_ref[...])
        y = y + g_ref[:, 2 * D:3 * D] * _dot(oc_ref[...], wc_ref[...])
        yb = y.astype(BF16)
        yb_ref[...] = yb
        x1_ref[...] = x_ref[...] + _dot(yb, wout_ref[...])

    return _pcall(
        body, name="merge_fwd", grid=(S // ts,),
        in_specs=[_tok(ts, D), _tok(ts, 3 * D), _tok(ts, 1024), _tok(ts, 1024), _tok(ts, 512),
                  _const(1024, D), _const(1024, D), _const(512, D), _const(D, D)],
        out_specs=[_tok(ts, D), _tok(ts, D)],
        out_shape=[_sds((S, D), F32), _sds((S, D), BF16)],
        sem=("arbitrary",),
    )(x, g, oa, ob, oc, wa, wb, wc, wout)


def _mlp_fwd(x1, mn, wup, wdown, ts):
    S = x1.shape[0]

    def body(x_ref, mn_ref, wup_ref, wdown_ref, x2_ref):
        xv = x_ref[...]
        n, _ = _rms(xv)
        u = _dot((n * mn_ref[...]).astype(BF16), wup_ref[...])
        a = jnp.square(jnp.maximum(u, 0.0))
        x2_ref[...] = xv + _dot(a.astype(BF16), wdown_ref[...])

    return _pcall(
        body, name="mlp_fwd", grid=(S // ts,),
        in_specs=[_tok(ts, D), _full(1, D), _const(D, DFF), _const(DFF, D)],
        out_specs=_tok(ts, D), out_shape=_sds((S, D), F32), sem=("arbitrary",),
    )(x1, mn, wup, wdown)


def _loss_kernel(x, fn, tgt, ts):
    S = x.shape[0]

    def body(x_ref, fn_ref, t_ref, loss_ref, dx_ref, dfn_ref):
        @pl.when(pl.program_id(0) == 0)
        def _():
            loss_ref[...] = jnp.zeros_like(loss_ref)
            dfn_ref[...] = jnp.zeros_like(dfn_ref)

        n, r = _rms(x_ref[...])
        err = n * fn_ref[...] - t_ref[...]
        loss_ref[...] += jnp.sum(err * err)
        dy = err * (1.0 / D)
        dfn_ref[...] += _colsum(dy * n)
        dx_ref[...] = _rms_bwd(dy * fn_ref[...], n, r)

    return _pcall(
        body, name="loss_head", grid=(S // ts,),
        in_specs=[_tok(ts, D), _full(1, D), _tok(ts, D)],
        out_specs=[_full(1, LANE), _tok(ts, D), _full(1, D)],
        out_shape=[_sds((1, LANE), F32), _sds((S, D), F32), _sds((1, D), F32)],
        sem=("arbitrary",),
    )(x, fn, tgt)


def _mlp_bwd(dx2, x1, mn, wup, wdown, ts):
    S = x1.shape[0]

    def body(dx_ref, x_ref, mn_ref, wup_ref, wdown_ref, dx1_ref, hb_ref, dub_ref, ab_ref, dxb_ref, dmn_ref):
        @pl.when(pl.program_id(0) == 0)
        def _():
            dmn_ref[...] = jnp.zeros_like(dmn_ref)

        dx = dx_ref[...]
        n, r = _rms(x_ref[...])
        g = mn_ref[...]
        hb = (n * g).astype(BF16)
        hb_ref[...] = hb
        rl = jnp.maximum(_dot(hb, wup_ref[...]), 0.0)
        ab_ref[...] = jnp.square(rl).astype(BF16)
        dxb = dx.astype(BF16)
        dxb_ref[...] = dxb
        dub = (_dot_nt(dxb, wdown_ref[...]) * (2.0 * rl)).astype(BF16)
        dub_ref[...] = dub
        dh = _dot_nt(dub, wup_ref[...])
        dmn_ref[...] += _colsum(dh * n)
        dx1_ref[...] = dx + _rms_bwd(dh * g, n, r)

    return _pcall(
        body, name="mlp_bwd", grid=(S // ts,),
        in_specs=[_tok(ts, D), _tok(ts, D), _full(1, D), _full(D, DFF), _full(DFF, D)],
        out_specs=[_tok(ts, D), _tok(ts, D), _tok(ts, DFF), _tok(ts, DFF), _tok(ts, D), _full(1, D)],
        out_shape=[_sds((S, D), F32), _sds((S, D), BF16), _sds((S, DFF), BF16), _sds((S, DFF), BF16),
                   _sds((S, D), BF16), _sds((1, D), F32)],
        sem=("arbitrary",),
    )(dx2, x1, mn, wup, wdown)


def _merge_bwd(dx1, g, oa, ob, oc, wa, wb, wc, wout, ts):
    S = dx1.shape[0]

    def body(dx_ref, g_ref, oa_ref, ob_ref, oc_ref, wa_ref, wb_ref, wc_ref, wout_ref,
             dgp_ref, dyo_ref, doa_ref, dob_ref, doc_ref, dla_ref, dxb_ref, dbg_ref):
        @pl.when(pl.program_id(0) == 0)
        def _():
            dbg_ref[...] = jnp.zeros_like(dbg_ref)

        dxb = dx_ref[...].astype(BF16)
        dxb_ref[...] = dxb
        dy = _dot_nt(dxb, wout_ref[...])
        branches = ((oa_ref, wa_ref, doa_ref), (ob_ref, wb_ref, dob_ref), (oc_ref, wc_ref, doc_ref))
        for b, (o_ref, w_ref, do_ref) in enumerate(branches):
            cols = slice(b * D, (b + 1) * D)
            gb = g_ref[:, cols]
            o = o_ref[...]
            dgpre = dy * _dot(o, w_ref[...]) * gb * (1.0 - gb)
            dgp_ref[:, cols] = dgpre.astype(BF16)
            dbg_ref[:, cols] += _colsum(dgpre)
            dyo = (dy * gb).astype(BF16)
            dyo_ref[:, cols] = dyo
            do = _dot_nt(dyo, w_ref[...])
            do_ref[...] = do.astype(BF16)
            if b == 0:
                lane = lax.broadcasted_iota(jnp.int32, (ts, LANE), 1)
                dls = jnp.zeros((ts, LANE), F32)
                for h in range(NH):
                    dl = jnp.sum(do[:, _hs(h)] * o[:, _hs(h)].astype(F32), axis=1, keepdims=True)
                    dls = jnp.where(lane == h, dl, dls)
                dla_ref[:, 0, :] = jnp.transpose(dls)[0:NH, :]

    return _pcall(
        body, name="merge_bwd", grid=(S // ts,),
        in_specs=[_tok(ts, D), _tok(ts, 3 * D), _tok(ts, 1024), _tok(ts, 1024), _tok(ts, 512),
                  _full(1024, D), _full(1024, D), _full(512, D), _full(D, D)],
        out_specs=[_tok(ts, 3 * D), _tok(ts, 3 * D), _tok(ts, 1024), _tok(ts, 1024), _tok(ts, 512),
                   pl.BlockSpec((NH, 1, ts), lambda i: (0, 0, i)), _tok(ts, D), _full(1, 3 * D)],
        out_shape=[_sds((S, 3 * D), BF16), _sds((S, 3 * D), BF16), _sds((S, 1024), BF16), _sds((S, 1024), BF16),
                   _sds((S, 512), BF16), _sds((NH, 1, S), F32), _sds((S, D), BF16), _sds((1, 3 * D), F32)],
        sem=("arbitrary",),
    )(dx1, g, oa, ob, oc, wa, wb, wc, wout)


def _pre_bwd(x, dx1, dq, dk, dv, dqs, dks, dvs, dqm, dgp, an, qn, kvn, win, wuq, wukv, rc, rs1, rs2, ts, ride=None):
    S = x.shape[0]

    def body(x_ref, dx1_ref, dq_ref, dk_ref, dv_ref, dqs_ref, dks_ref, dvs_ref, dqm_ref, dgp_ref,
             an_ref, qn_ref, kvn_ref, win_ref, wuq_ref, wukv_ref, rc_ref, rs1_ref, rs2_ref, *rest):
        own, _, riding = _ride_refs(ride, rest, 10, 0)
        dx_ref, dproj_ref, hb_ref, cqn_ref, ckvn_ref, dqpre_ref, dkv_ref, dan_ref, dqn_ref, dkvn_ref = own

        @pl.when(pl.program_id(0) == 0)
        def _():
            dan_ref[...] = jnp.zeros_like(dan_ref)
            dqn_ref[...] = jnp.zeros_like(dqn_ref)
            dkvn_ref[...] = jnp.zeros_like(dkvn_ref)
            if ride:
                ride.start(*riding)

        n, r = _rms(x_ref[...])
        hb = (n * an_ref[...]).astype(BF16)
        hb_ref[...] = hb
        pa = _dot(hb, win_ref[:, C_CQ:C_KPE])
        ncq, rq = _rms(pa[:, 0:256])
        cqn_ref[...] = (ncq * qn_ref[...]).astype(BF16)
        nkv, rkv = _rms(pa[:, 256:384])
        ckvn_ref[...] = (nkv * kvn_ref[...]).astype(BF16)
        c, s1, s2 = rc_ref[...], rs1_ref[...], rs2_ref[...]

        dkper = jnp.zeros((ts, LANE), F32)
        for h in range(NH):
            dqpre_ref[:, _hs(h)] = _rope_bwd(dq_ref[:, _hs(h)], c, s1, s2).astype(BF16)
            dkh = dk_ref[:, _hs(h)]
            dkper = dkper + dkh
            dkv_ref[:, _hs(h)] = dkh.astype(BF16)
        dkv_ref[:, NH * LANE:] = dv_ref[...].astype(BF16)

        dcqn = _dot_nt(dqpre_ref[...], wuq_ref[...])
        dqn_ref[...] += _colsum(dcqn * ncq)
        dproj_ref[:, C_CQ:C_CKV] = _rms_bwd(dcqn * qn_ref[...], ncq, rq).astype(BF16)
        dckvn = _dot_nt(dkv_ref[...], wukv_ref[...])
        dkvn_ref[...] += _colsum(dckvn * nkv)
        dproj_ref[:, C_CKV:C_KPE] = _rms_bwd(dckvn * kvn_ref[...], nkv, rkv).astype(BF16)
        lane = lax.broadcasted_iota(jnp.int32, (ts, LANE), 1)
        dkpe = jnp.where((lane >= 64) & (lane < 96), _rope_bwd(dkper, c, s1, s2), 0.0)
        dproj_ref[:, C_KPE:C_QS] = dkpe.astype(BF16)
        dproj_ref[:, C_QS:C_KS] = dqs_ref[...]
        dproj_ref[:, C_KS:C_VS] = dks_ref[...].astype(BF16)
        dproj_ref[:, C_VS:C_QM] = dvs_ref[...].astype(BF16)
        dproj_ref[:, C_QM:C_G] = dqm_ref[...]
        dproj_ref[:, C_G:C_END] = dgp_ref[...]

        dh = _dot_nt(dproj_ref[...], win_ref[...])
        dan_ref[...] += _colsum(dh * n)
        dx_ref[...] = dx1_ref[...] + _rms_bwd(dh * an_ref[...], n, r)

        if ride:
            @pl.when(pl.program_id(0) == S // ts - 1)
            def _():
                ride.finish(*riding)

    r_in, r_out, r_shapes, r_sems, r_args = _ride_specs(ride)
    results = _pcall(
        body, name="pre_bwd", grid=(S // ts,),
        in_specs=[_tok(ts, D), _tok(ts, D), _tok(ts, 1024), _tok(ts, 1024), _tok(ts, 1024), _tok(ts, 1024),
                  _tok(ts, 128), _tok(ts, 128), _tok(ts, 512), _tok(ts, 3 * D),
                  _full(1, D), _full(1, 256), _full(1, 128), _full(D, C_END), _full(256, NH * LANE),
                  _full(128, 2 * NH * LANE), _tok(ts, LANE), _tok(ts, LANE), _tok(ts, LANE)] + r_in,
        out_specs=[_tok(ts, D), _tok(ts, C_END), _tok(ts, D), _tok(ts, 256), _tok(ts, 128), _tok(ts, 1024),
                   _tok(ts, 2048), _full(1, D), _full(1, 256), _full(1, 128)] + r_out,
        out_shape=[_sds((S, D), F32), _sds((S, C_END), BF16), _sds((S, D), BF16), _sds((S, 256), BF16),
                   _sds((S, 128), BF16), _sds((S, 1024), BF16), _sds((S, 2048), BF16), _sds((1, D), F32),
                   _sds((1, 256), F32), _sds((1, 128), F32)] + r_shapes,
        scratch=r_sems, sem=("arbitrary",),
    )(x, dx1, dq, dk, dv, dqs, dks, dvs, dqm, dgp, an, qn, kvn, win, wuq, wukv, rc, rs1, rs2, *r_args)
    return (*results[:10], list(results[10:]))


def _pick_tile(n, cap):
    best = LANE
    for t in range(LANE, min(n, cap) + 1, LANE):
        if n % t == 0:
            best = t
    return best


def _matmul_tn(a, b, name, shard_axis=None):
    S, M = a.shape
    N = b.shape[1]
    tm = _pick_tile(M // 4 if shard_axis == 0 else M, 1024)
    tn = _pick_tile(N // 4 if shard_axis == 1 else N, 2048)
    ts = min(S, 1024)
    nk = S // ts

    def body(a_ref, b_ref, o_ref, *acc):
        acc_ref = acc[0] if acc else o_ref

        @pl.when(pl.program_id(2) == 0)
        def _():
            acc_ref[...] = jnp.zeros_like(acc_ref)

        acc_ref[...] += _dot_tn(a_ref[...], b_ref[...])
        if acc:
            @pl.when(pl.program_id(2) == nk - 1)
            def _():
                o_ref[0] = acc_ref[...].astype(o_ref.dtype)

    if shard_axis is None:
        out_spec = pl.BlockSpec((tm, tn), lambda i, j, k: (i, j))
        out_shape, scratch = _sds((M, N), F32), ()
    elif shard_axis == 0:
        per = (M // 4) // tm
        out_spec = pl.BlockSpec((1, tm, tn), lambda i, j, k: (i // per, i % per, j))
        out_shape, scratch = _sds((4, M // 4, N), BF16), (pltpu.VMEM((tm, tn), F32),)
    else:
        per = (N // 4) // tn
        out_spec = pl.BlockSpec((1, tm, tn), lambda i, j, k: (j // per, i, j % per))
        out_shape, scratch = _sds((4, M, N // 4), BF16), (pltpu.VMEM((tm, tn), F32),)
    return _pcall(
        body, name=name, grid=(M // tm, N // tn, nk),
        in_specs=[pl.BlockSpec((ts, tm), lambda i, j, k: (k, i)), pl.BlockSpec((ts, tn), lambda i, j, k: (k, j))],
        out_specs=out_spec, out_shape=out_shape, scratch=scratch, sem=("parallel", "parallel", "arbitrary"),
    )(a, b)


def _select_fwd(a, sel, name):
    _, L, M, K = a.shape
    N = sel.shape[2]
    tn = _pick_tile(N, 1792)

    def body(a_ref, s_ref, o_ref, acc_ref):
        s = pl.program_id(2)

        @pl.when(s == 0)
        def _():
            acc_ref[...] = jnp.zeros_like(acc_ref)

        acc_ref[...] += _dot(a_ref[0, 0], s_ref[0])

        @pl.when(s == 3)
        def _():
            o_ref[0] = acc_ref[...].astype(BF16)

    return _pcall(
        body, name=name, grid=(L, N // tn, 4),
        in_specs=[pl.BlockSpec((1, 1, M, K), lambda l, j, s: (s, l, 0, 0)),
                  pl.BlockSpec((1, K, tn), lambda l, j, s: (s, 0, j))],
        out_specs=pl.BlockSpec((1, M, tn), lambda l, j, s: (l, 0, j)),
        out_shape=_sds((L, M, N), BF16), scratch=(pltpu.VMEM((M, tn), F32),),
        sem=("parallel", "parallel", "arbitrary"),
    )(a, sel)


def _select_bwd(dw, sel, name):
    L, M, N = dw.shape
    K = sel.shape[1]
    tk = _pick_tile(N, 1792)
    nk = N // tk

    def body(d_ref, s_ref, o_ref, acc_ref):
        k = pl.program_id(2)

        @pl.when(k == 0)
        def _():
            acc_ref[...] = jnp.zeros_like(acc_ref)

        acc_ref[...] += _dot_nt(d_ref[0].astype(BF16), s_ref[0])

        @pl.when(k == nk - 1)
        def _():
            o_ref[0, 0] = acc_ref[...].astype(BF16)

    return _pcall(
        body, name=name, grid=(L, 4, nk),
        in_specs=[pl.BlockSpec((1, M, tk), lambda l, s, k: (l, 0, k)),
                  pl.BlockSpec((1, K, tk), lambda l, s, k: (s, 0, k))],
        out_specs=pl.BlockSpec((1, 1, M, K), lambda l, s, k: (l, s, 0, 0)),
        out_shape=_sds((L, 4, M, K), BF16), scratch=(pltpu.VMEM((M, K), F32),),
        sem=("parallel", "parallel", "arbitrary"),
    )(dw, sel)


MLA_RC = 128
MLA_RC_FWD = 128
MLA_AHEAD = 4
MLA_HEADS_FWD = 2
MLA_HEADS_BWD = 2
LOG2E = math.log2(math.e)
MLA_QSCALE = MLA_SCALE * LOG2E


def _ride_refs(ride, rest, n_out, n_scratch):
    ni, no = (len(ride.inputs), len(ride.out_shapes)) if ride else (0, 0)
    own_out = rest[ni:ni + n_out]
    own_scratch = rest[ni + n_out + no:ni + n_out + no + n_scratch]
    parts = rest[:ni], rest[ni + n_out:ni + n_out + no], rest[ni + n_out + no + n_scratch:]
    return own_out, own_scratch, parts


def _ride_specs(ride):
    hbm = pl.BlockSpec(memory_space=pl.ANY)
    if not ride:
        return [], [], [], [], []
    return ([hbm] * len(ride.inputs), [hbm] * len(ride.out_shapes), list(ride.out_shapes), list(ride.sems),
            list(ride.inputs))


def _mla_fwd(q, k, v, tq, ride=None):
    S = q.shape[0]
    nq = S // tq
    hp = MLA_HEADS_FWD
    pairs = [(i, j) for i in range(nq) for j in range(i + 1)]
    qi = jnp.asarray(np.array([p[0] for p in pairs], np.int32))
    kj = jnp.asarray(np.array([p[1] for p in pairs], np.int32))

    def body(qi_ref, kj_ref, q_ref, k_ref, v_ref, *rest):
        (o_ref, lse_ref), (m_s, l_s, acc_s), riding = _ride_refs(ride, rest, 2, 3)
        t = pl.program_id(1)
        i, j = qi_ref[t], kj_ref[t]
        if ride:
            @pl.when((pl.program_id(0) == 0) & (t == 0))
            def _():
                ride.start(*riding)

        @pl.when(j == 0)
        def _():
            m_s[...] = jnp.full_like(m_s, NEG)
            l_s[...] = jnp.zeros_like(l_s)
            acc_s[...] = jnp.zeros_like(acc_s)

        def step(masked):
            rc = min(MLA_RC_FWD, tq)
            nc = tq // rc
            keys = [(c + 1) * rc if masked else tq for c in range(nc)]
            units = [(hh, c) for hh in range(hp) for c in range(nc)]
            ahead = min(MLA_AHEAD, len(units))

            def qk(u):
                hh, c = units[u]
                return _dot_nt(q_ref[c * rc:(c + 1) * rc, _hs(hh)], k_ref[0:keys[c], _hs(hh)])

            scores = [qk(u) for u in range(ahead)]
            for u, (hh, c) in enumerate(units):
                rows = slice(c * rc, (c + 1) * rc)
                s = scores[u]
                if masked:
                    row = lax.broadcasted_iota(jnp.int32, (rc, keys[c]), 0) + c * rc
                    col = lax.broadcasted_iota(jnp.int32, (rc, keys[c]), 1)
                    s = jnp.where(col <= row, s, NEG)
                tiles = [s[:, _hs(w)] for w in range(keys[c] // LANE)]
                mx = functools.reduce(jnp.maximum, tiles)
                m_old = m_s[rows, _hs(hh)]
                m_new = jnp.maximum(m_old, jnp.max(mx, axis=1, keepdims=True))
                alpha = jnp.exp2(m_old - m_new)
                ps = [jnp.exp2(w - m_new) for w in tiles]
                l_s[rows, _hs(hh)] = alpha * l_s[rows, _hs(hh)] + functools.reduce(jnp.add, ps)
                p = jnp.concatenate([w.astype(BF16) for w in ps], axis=1)
                acc_s[rows, _hs(hh)] = alpha * acc_s[rows, _hs(hh)] + _dot(p, v_ref[0:keys[c], _hs(hh)])
                m_s[rows, _hs(hh)] = m_new
                if u + ahead < len(units):
                    scores.append(qk(u + ahead))

        @pl.when(j < i)
        def _():
            step(False)

        @pl.when(j == i)
        def _():
            step(True)
            for hh in range(hp):
                l = jnp.sum(l_s[:, _hs(hh)], axis=1, keepdims=True)
                o_ref[:, _hs(hh)] = (acc_s[:, _hs(hh)] / l).astype(BF16)
                lse_ref[hh] = jnp.transpose(m_s[:, _hs(hh)] + jnp.log2(l))[0:1, :]

        if ride:
            @pl.when((pl.program_id(0) == NH // hp - 1) & (t == len(pairs) - 1))
            def _():
                ride.finish(*riding)

    qmap = lambda h, t, qi_r, kj_r: (qi_r[t], h)
    kmap = lambda h, t, qi_r, kj_r: (kj_r[t], h)
    r_in, r_out, r_shapes, r_sems, r_args = _ride_specs(ride)
    wide = hp * LANE
    o, lse, *carried = _pcall(
        body, name="mla_fwd", grid=(NH // hp, len(pairs)), prefetch=2,
        in_specs=[pl.BlockSpec((tq, wide), qmap), pl.BlockSpec((tq, wide), kmap), pl.BlockSpec((tq, wide), kmap)]
        + r_in,
        out_specs=[pl.BlockSpec((tq, wide), qmap),
                   pl.BlockSpec((hp, 1, tq), lambda h, t, qi_r, kj_r: (h, 0, qi_r[t]))] + r_out,
        out_shape=[_sds((S, NH * LANE), BF16), _sds((NH, 1, S), F32)] + r_shapes,
        scratch=[pltpu.VMEM((tq, wide), F32), pltpu.VMEM((tq, wide), F32), pltpu.VMEM((tq, wide), F32)] + r_sems,
        sem=("arbitrary", "arbitrary"),
    )(qi, kj, q, k, v, *r_args)
    return o, lse, carried


def _mla_bwd(q, k, v, do, lse, delta, tq, ride=None):
    S = q.shape[0]
    nq = S // tq
    hp = MLA_HEADS_BWD
    pairs = [(i, j) for j in range(nq) for i in range(j, nq)]
    qi = jnp.asarray(np.array([p[0] for p in pairs], np.int32))
    kj = jnp.asarray(np.array([p[1] for p in pairs], np.int32))

    def body(qi_ref, kj_ref, q_ref, k_ref, v_ref, do_ref, lse_ref, dl_ref, *rest):
        (dq_ref, dk_ref, dv_ref), (dq_s, dk_s, dv_s), riding = _ride_refs(ride, rest, 3, 3)
        t = pl.program_id(1)
        i, j = qi_ref[t], kj_ref[t]
        if ride:
            @pl.when((pl.program_id(0) == 0) & (t == 0))
            def _():
                ride.start(*riding)

        @pl.when(t == 0)
        def _():
            dq_s[...] = jnp.zeros_like(dq_s)

        @pl.when(i == j)
        def _():
            dk_s[...] = jnp.zeros_like(dk_s)
            dv_s[...] = jnp.zeros_like(dv_s)

        qrows = pl.ds(pl.multiple_of(i * tq, tq), tq)

        def step(masked):
            dq = [jnp.zeros((tq, LANE), F32) for _ in range(hp)]
            nc = tq // MLA_RC
            units = [(hh, c) for hh in range(hp) for c in range(nc)]
            q0 = [c * MLA_RC if masked else 0 for c in range(nc)]
            sts = [_dot_nt(k_ref[c * MLA_RC:(c + 1) * MLA_RC, _hs(hh)], q_ref[q0[c]:, _hs(hh)]) for hh, c in units]
            dpts = [_dot_nt(v_ref[c * MLA_RC:(c + 1) * MLA_RC, _hs(hh)], do_ref[q0[c]:, _hs(hh)]) for hh, c in units]
            for u, (hh, c) in enumerate(units):
                rows = slice(c * MLA_RC, (c + 1) * MLA_RC)
                qb, dob = q_ref[q0[c]:, _hs(hh)], do_ref[q0[c]:, _hs(hh)]
                pt = jnp.exp2(sts[u] - lse_ref[hh][:, q0[c]:])
                if masked:
                    key = lax.broadcasted_iota(jnp.int32, (MLA_RC, tq - q0[c]), 0)
                    qry = lax.broadcasted_iota(jnp.int32, (MLA_RC, tq - q0[c]), 1)
                    pt = jnp.where(key <= qry, pt, 0.0)
                dv_s[rows, _hs(hh)] += _dot(pt.astype(BF16), dob)
                gt = (pt * (dpts[u] - dl_ref[hh][:, q0[c]:])).astype(BF16)
                dk_s[rows, _hs(hh)] += _dot(gt, qb)
                part = _dot_tn(gt, k_ref[rows, _hs(hh)])
                if masked:
                    at = pl.multiple_of(i * tq + q0[c], MLA_RC)
                    dq_s[pl.ds(at, tq - q0[c]), _hs(hh)] += part
                else:
                    dq[hh] = dq[hh] + part
            if not masked:
                for hh in range(hp):
                    dq_s[qrows, _hs(hh)] += dq[hh]

        @pl.when(i > j)
        def _():
            step(False)

        @pl.when(i == j)
        def _():
            step(True)
            dq_ref[...] = dq_s[qrows, :] * MLA_SCALE

        @pl.when(i == nq - 1)
        def _():
            dk_ref[...] = dk_s[...] * (1.0 / LOG2E)
            dv_ref[...] = dv_s[...]

        if ride:
            @pl.when((pl.program_id(0) == NH // hp - 1) & (t == len(pairs) - 1))
            def _():
                ride.finish(*riding)

    qmap = lambda h, t, qi_r, kj_r: (qi_r[t], h)
    kmap = lambda h, t, qi_r, kj_r: (kj_r[t], h)
    rmap = lambda h, t, qi_r, kj_r: (h, 0, qi_r[t])
    r_in, r_out, r_shapes, r_sems, r_args = _ride_specs(ride)
    wide = hp * LANE
    dq, dk, dv, *carried = _pcall(
        body, name="mla_bwd", grid=(NH // hp, len(pairs)), prefetch=2,
        in_specs=[pl.BlockSpec((tq, wide), qmap), pl.BlockSpec((tq, wide), kmap), pl.BlockSpec((tq, wide), kmap),
                  pl.BlockSpec((tq, wide), qmap), pl.BlockSpec((hp, 1, tq), rmap), pl.BlockSpec((hp, 1, tq), rmap)]
        + r_in,
        out_specs=[pl.BlockSpec((tq, wide), kmap), pl.BlockSpec((tq, wide), kmap), pl.BlockSpec((tq, wide), kmap)]
        + r_out,
        out_shape=[_sds((S, NH * LANE), F32), _sds((S, NH * LANE), F32), _sds((S, NH * LANE), F32)] + r_shapes,
        scratch=[pltpu.VMEM((S, wide), F32), pltpu.VMEM((tq, wide), F32), pltpu.VMEM((tq, wide), F32)] + r_sems,
        sem=("arbitrary", "arbitrary"),
    )(qi, kj, q, k, v, do, lse, delta, *r_args)
    return dq, dk, dv, carried


SWA_SUB = 4
SWA_T = SWA_SUB * WIN


def _swa_specs(nsteps, rev):
    step = (lambda i: nsteps - 1 - i) if rev else (lambda i: i)
    cur = lambda w: pl.BlockSpec((SWA_T, w), lambda i: (step(i), 0))
    prev = pl.BlockSpec((WIN, LANE), lambda i: (jnp.maximum(step(i) * SWA_SUB - 1, 0), 0))
    return step, cur, prev


def _swa_probs(qk, bias_h, sink, first_mask):
    s = qk * SWA_SCALE + bias_h
    if first_mask is not None:
        s = jnp.where(first_mask, NEG, s)
    m = jnp.maximum(jnp.max(s, axis=1, keepdims=True), sink)
    e = jnp.exp(s - m)
    es = jnp.exp(sink - m)
    inv = 1.0 / (jnp.sum(e, axis=1, keepdims=True) + es)
    return e * inv, es * inv


SWA_GR = SWA_R * WIN


def _swa_group(ref, rows, g):
    return jnp.concatenate([ref[rows, _hs(g * SWA_R + r)] for r in range(SWA_R)], axis=0)


def _swa_rows(bias, sinks):
    sink_rows = jnp.broadcast_to(sinks[:, None, :], (NH, WIN, LANE)).reshape(NH * WIN, LANE)
    return bias.reshape(NH * WIN, 2 * WIN), sink_rows


def _swa_fwd(qs, ks, vs, bias, sinks):
    S = qs.shape[0]
    nsteps = S // SWA_T
    step, cur, prev = _swa_specs(nsteps, False)

    def body(qs_ref, kc_ref, kp_ref, vc_ref, vp_ref, bias_ref, sk_ref, o_ref):
        first = pl.program_id(0) == 0
        kk = jnp.concatenate([kp_ref[...], kc_ref[...]], axis=0)
        vv = jnp.concatenate([vp_ref[...], vc_ref[...]], axis=0)
        col = lax.broadcasted_iota(jnp.int32, (WIN, 2 * WIN), 1)
        qk = lambda b: [_dot_nt(qs_ref[b * WIN:(b + 1) * WIN, _hs(h)], kk[b * WIN:(b + 2) * WIN]) for h in range(NH)]
        ahead = qk(0)
        for b in range(SWA_SUB):
            vvb = vv[b * WIN:(b + 2) * WIN]
            fm = (first & (col < WIN)) if b == 0 else None
            rows = slice(b * WIN, (b + 1) * WIN)
            qks, ahead = ahead, (qk(b + 1) if b + 1 < SWA_SUB else None)
            for h in range(NH):
                p, _ = _swa_probs(qks[h], bias_ref[h], sk_ref[h:h + 1, 0:1], fm)
                o_ref[rows, _hs(h)] = _dot(p.astype(BF16), vvb).astype(BF16)

    return _pcall(
        body, name="swa_fwd", grid=(nsteps,),
        in_specs=[cur(NH * LANE), cur(LANE), prev, cur(LANE), prev, _full(NH, WIN, 2 * WIN), _full(NH, LANE)],
        out_specs=cur(NH * LANE), out_shape=_sds((S, NH * LANE), BF16), sem=("arbitrary",),
    )(qs, ks, ks, vs, vs, bias, sinks)


def _swa_bwd(qs, ks, vs, do, bias, sinks):
    S = qs.shape[0]
    nsteps = S // SWA_T
    step, cur, prev = _swa_specs(nsteps, True)
    bias_rows, sink_rows = _swa_rows(bias, sinks)

    def body(qs_ref, kc_ref, kp_ref, vc_ref, vp_ref, do_ref, bias_ref, sk_ref,
             dqs_ref, dks_ref, dvs_ref, dbias_ref, dsk_ref, dkk_s, dvv_s, ck_s, cv_s):
        pid = pl.program_id(0)
        first = step(pid) == 0

        @pl.when(pid == 0)
        def _():
            dbias_ref[...] = jnp.zeros_like(dbias_ref)
            dsk_ref[...] = jnp.zeros_like(dsk_ref)
            ck_s[...] = jnp.zeros_like(ck_s)
            cv_s[...] = jnp.zeros_like(cv_s)

        dkk_s[...] = jnp.zeros_like(dkk_s)
        dvv_s[...] = jnp.zeros_like(dvv_s)
        kk = jnp.concatenate([kp_ref[...], kc_ref[...]], axis=0)
        vv = jnp.concatenate([vp_ref[...], vc_ref[...]], axis=0)
        col = lax.broadcasted_iota(jnp.int32, (SWA_GR, 2 * WIN), 1)

        def products(b):
            rows = slice(b * WIN, (b + 1) * WIN)
            qg = [_swa_group(qs_ref, rows, g) for g in range(2)]
            dog = [_swa_group(do_ref, rows, g) for g in range(2)]
            return (qg, dog, [_dot_nt(qg[g], kk[b * WIN:(b + 2) * WIN]) for g in range(2)],
                    [_dot_nt(dog[g], vv[b * WIN:(b + 2) * WIN]) for g in range(2)])

        ahead = products(0)
        for b in range(SWA_SUB):
            kkb = kk[b * WIN:(b + 2) * WIN]
            fm = (first & (col < WIN)) if b == 0 else None
            rows = slice(b * WIN, (b + 1) * WIN)
            keys = slice(b * WIN, (b + 2) * WIN)
            (qg, dog, qks, dps), ahead = ahead, (products(b + 1) if b + 1 < SWA_SUB else None)
            for g in range(2):
                grows = slice(g * SWA_GR, (g + 1) * SWA_GR)
                p, ps = _swa_probs(qks[g], bias_ref[grows, :], sk_ref[grows, 0:1], fm)
                dl = jnp.sum(p * dps[g], axis=1, keepdims=True)
                ds = p * (dps[g] - dl)
                sink_part = ps * dl
                for r in range(SWA_R):
                    h = g * SWA_R + r
                    dsk_ref[h:h + 1, :] += -jnp.sum(sink_part[r * WIN:(r + 1) * WIN])
                dbias_ref[grows, :] += ds
                dsb = (ds * SWA_SCALE).astype(BF16)
                dq = _dot(dsb, kkb).astype(BF16)
                for r in range(SWA_R):
                    dqs_ref[rows, _hs(g * SWA_R + r)] = dq[r * WIN:(r + 1) * WIN]
                dkk_s[keys, :] += _dot_tn(dsb, qg[g])
                dvv_s[keys, :] += _dot_tn(p.astype(BF16), dog[g])
        dks_ref[...] = dkk_s[WIN:, :]
        dvs_ref[...] = dvv_s[WIN:, :]
        dks_ref[SWA_T - WIN:, :] += ck_s[...]
        dvs_ref[SWA_T - WIN:, :] += cv_s[...]
        ck_s[...] = dkk_s[0:WIN, :]
        cv_s[...] = dvv_s[0:WIN, :]

    dqs, dks, dvs, dbias, dsink = _pcall(
        body, name="swa_bwd", grid=(nsteps,),
        in_specs=[cur(NH * LANE), cur(LANE), prev, cur(LANE), prev, cur(NH * LANE), _full(NH * WIN, 2 * WIN),
                  _full(NH * WIN, LANE)],
        out_specs=[cur(NH * LANE), cur(LANE), cur(LANE), _full(NH * WIN, 2 * WIN), _full(NH, LANE)],
        out_shape=[_sds((S, NH * LANE), BF16), _sds((S, LANE), F32), _sds((S, LANE), F32),
                   _sds((NH * WIN, 2 * WIN), F32), _sds((NH, LANE), F32)],
        scratch=[pltpu.VMEM((SWA_T + WIN, LANE), F32), pltpu.VMEM((SWA_T + WIN, LANE), F32),
                 pltpu.VMEM((WIN, LANE), F32), pltpu.VMEM((WIN, LANE), F32)],
        sem=("arbitrary",),
    )(qs, ks, ks, vs, vs, do, bias_rows, sink_rows)
    return dqs, dks, dvs, dbias.reshape(NH, WIN, 2 * WIN), dsink


def _bias_build(rel_bias, bmap):
    def body(rb_ref, bmap_ref, o_ref):
        bm = bmap_ref[...]
        for h in range(NH):
            acc = jnp.full((WIN, 2 * WIN), NEG, F32)
            for b in range(REL_BUCKETS):
                acc = jnp.where(bm == b, rb_ref[b, h], acc)
            o_ref[h] = acc

    return _pcall(
        body, name="bias_build", grid=(1,),
        in_specs=[pl.BlockSpec(memory_space=pltpu.SMEM), _full(WIN, 2 * WIN)],
        out_specs=_full(NH, WIN, 2 * WIN), out_shape=_sds((NH, WIN, 2 * WIN), F32), sem=("arbitrary",),
    )(rel_bias, bmap)


def _bias_reduce(dbias, bmap):
    def body(db_ref, bmap_ref, o_ref):
        bm = bmap_ref[...]
        for h in range(NH):
            dbh = db_ref[h]
            for b in range(REL_BUCKETS):
                o_ref[b, h] = jnp.sum(jnp.where(bm == b, dbh, 0.0))

    return _pcall(
        body, name="bias_reduce", grid=(1,),
        in_specs=[_full(NH, WIN, 2 * WIN), _full(WIN, 2 * WIN)],
        out_specs=pl.BlockSpec(memory_space=pltpu.SMEM), out_shape=_sds((REL_BUCKETS, NH), F32), sem=("arbitrary",),
    )(dbias, bmap)


def _memkv_fwd(mem, mnorm, wkv):
    def body(mem_ref, g_ref, w_ref, o_ref):
        n, _ = _rms(mem_ref[...])
        o_ref[...] = _dot((n * g_ref[...]).astype(BF16), w_ref[...]).astype(BF16)

    return _pcall(
        body, name="memkv_fwd", grid=(1,), in_specs=[_full(MEM_LEN, D), _full(1, D), _full(D, D)],
        out_specs=_full(MEM_LEN, D), out_shape=_sds((MEM_LEN, D), BF16), sem=("arbitrary",),
    )(mem, mnorm, wkv)


def _mem_probs(qk):
    s = qk * MEM_SCALE
    e = jnp.exp(s - jnp.max(s, axis=1, keepdims=True))
    return e / jnp.sum(e, axis=1, keepdims=True)


def _mem_fwd(qm, kvm, ts):
    S = qm.shape[0]

    def body(q_ref, kv_ref, o_ref):
        qks = [_dot_nt(q_ref[:, _hs(h)], kv_ref[:, _hs(h)]) for h in range(MEM_H)]
        for h in range(MEM_H):
            p = _mem_probs(qks[h])
            o_ref[:, _hs(h)] = _dot(p.astype(BF16), kv_ref[:, _hs(MEM_H + h)]).astype(BF16)

    return _pcall(
        body, name="mem_fwd", grid=(S // ts,), in_specs=[_tok(ts, 512), _full(MEM_LEN, D)],
        out_specs=_tok(ts, 512), out_shape=_sds((S, 512), BF16), sem=("arbitrary",),
    )(qm, kvm)


def _mem_bwd(qm, kvm, do, ts):
    S = qm.shape[0]

    def body(q_ref, kv_ref, do_ref, dq_ref, dkv_ref):
        @pl.when(pl.program_id(0) == 0)
        def _():
            dkv_ref[...] = jnp.zeros_like(dkv_ref)

        qks = [_dot_nt(q_ref[:, _hs(h)], kv_ref[:, _hs(h)]) for h in range(MEM_H)]
        dps = [_dot_nt(do_ref[:, _hs(h)], kv_ref[:, _hs(MEM_H + h)]) for h in range(MEM_H)]
        for h in range(MEM_H):
            qh, kh, doh = q_ref[:, _hs(h)], kv_ref[:, _hs(h)], do_ref[:, _hs(h)]
            p = _mem_probs(qks[h])
            dp = dps[h]
            ds = (p * (dp - jnp.sum(p * dp, axis=1, keepdims=True)) * MEM_SCALE).astype(BF16)
            dq_ref[:, _hs(h)] = _dot(ds, kh).astype(BF16)
            dkv_ref[:, _hs(h)] += _dot_tn(ds, qh)
            dkv_ref[:, _hs(MEM_H + h)] += _dot_tn(p.astype(BF16), doh)

    return _pcall(
        body, name="mem_bwd", grid=(S // ts,), in_specs=[_tok(ts, 512), _full(MEM_LEN, D), _tok(ts, 512)],
        out_specs=[_tok(ts, 512), _full(MEM_LEN, D)],
        out_shape=[_sds((S, 512), BF16), _sds((MEM_LEN, D), F32)], sem=("arbitrary",),
    )(qm, kvm, do)


def _memkv_bwd(mem, mnorm, wkv, dkvm):
    def body(mem_ref, g_ref, w_ref, dkv_ref, dw_ref, dg_ref):
        n, _ = _rms(mem_ref[...])
        dkvb = dkv_ref[...].astype(BF16)
        dw_ref[...] = _dot_tn((n * g_ref[...]).astype(BF16), dkvb)
        dg_ref[...] = _colsum(_dot_nt(dkvb, w_ref[...]) * n)

    return _pcall(
        body, name="memkv_bwd", grid=(1,), in_specs=[_full(MEM_LEN, D), _full(1, D), _full(D, D), _full(MEM_LEN, D)],
        out_specs=[_full(D, D), _full(1, D)], out_shape=[_sds((D, D), F32), _sds((1, D), F32)], sem=("arbitrary",),
    )(mem, mnorm, wkv, dkvm)


def _adamw(w, g, m, v, name):
    rows, cols = w.shape
    tr = min(rows, FLAT_TILE)
    assert rows % tr == 0

    def body(w_ref, g_ref, m_ref, v_ref, d_ref, nm_ref, nv_ref):
        gv = g_ref[...]
        nm = ADAM_B1 * m_ref[...] + (1.0 - ADAM_B1) * gv
        nv = ADAM_B2 * v_ref[...] + (1.0 - ADAM_B2) * jnp.square(gv)
        m_hat = nm / (1.0 - ADAM_B1 ** ADAM_STEP)
        v_hat = nv / (1.0 - ADAM_B2 ** ADAM_STEP)
        d_ref[...] = -ADAM_LR * (m_hat / (jnp.sqrt(v_hat) + ADAM_EPS) + ADAM_WD * w_ref[...])
        nm_ref[...] = nm
        nv_ref[...] = nv

    spec = _tok(tr, cols)
    return _pcall(
        body, name=name, grid=(rows // tr,), in_specs=[spec] * 4, out_specs=[spec] * 3,
        out_shape=[_sds((rows, cols), F32)] * 3, sem=("arbitrary",),
    )(w, g, m, v)


def _my_place():
    return lax.axis_index("x"), lax.axis_index("y"), lax.axis_index("c")


def _remote(src, dst, send_sems, recv_sems, k, to):
    return pltpu.make_async_remote_copy(src_ref=src, dst_ref=dst, send_sem=send_sems.at[k], recv_sem=recv_sems.at[k],
                                        device_id=to, device_id_type=MESH_ID)


class _Rider(NamedTuple):
    inputs: list
    out_shapes: list
    sems: list
    start: Callable
    finish: Callable


def _run_rider(rider, name):
    ni, no = len(rider.inputs), len(rider.out_shapes)

    def body(*refs):
        parts = refs[:ni], refs[ni:ni + no], refs[ni + no:]
        rider.start(*parts)
        rider.finish(*parts)

    hbm = pl.BlockSpec(memory_space=pl.ANY)
    return pl.pallas_call(body, name=name, out_shape=rider.out_shapes, in_specs=[hbm] * ni, out_specs=[hbm] * no,
                          scratch_shapes=rider.sems)(*rider.inputs)


def _join_riders(*riders):
    def cut(seq, lens):
        at, out = 0, []
        for n in lens:
            out.append(seq[at:at + n])
            at += n
        return out

    def each(ins, outs, sems):
        return zip(riders, cut(ins, [len(r.inputs) for r in riders]), cut(outs, [len(r.out_shapes) for r in riders]),
                   cut(sems, [len(r.sems) for r in riders]))

    def start(ins, outs, sems):
        for r, i, o, s in each(ins, outs, sems):
            r.start(i, o, s)

    def finish(ins, outs, sems):
        for r, i, o, s in each(ins, outs, sems):
            r.finish(i, o, s)

    return _Rider([a for r in riders for a in r.inputs], [a for r in riders for a in r.out_shapes],
                  [a for r in riders for a in r.sems], start, finish)


def _gather_rider(shards):
    n = len(shards)

    def copies(ins, outs, sems):
        send_sems, recv_sems, local_sems = sems
        x, y, c = _my_place()
        chips = [(1 - x, y), (x, 1 - y), (1 - x, 1 - y)]
        mine = [pltpu.make_async_copy(ins[i], outs[i].at[2 * x + y], local_sems.at[i]) for i in range(n)]

        def copy(i, k, slot, to):
            return _remote(ins[i], outs[i].at[slot], send_sems, recv_sems, 3 * i + k, to)

        sends = [copy(i, k, 2 * x + y, (px, py, c)) for i in range(n) for k, (px, py) in enumerate(chips)]
        lands = [copy(i, k, 2 * px + py, (px, py, c)) for i in range(n) for k, (px, py) in enumerate(chips)]
        return mine, sends, lands

    def start(ins, outs, sems):
        mine, sends, _ = copies(ins, outs, sems)
        for cp in mine + sends:
            cp.start()

    def finish(ins, outs, sems):
        mine, sends, lands = copies(ins, outs, sems)
        for cp in lands:
            cp.wait_recv()
        for cp in sends:
            cp.wait_send()
        for cp in mine:
            cp.wait()

    return _Rider(list(shards), [_sds((4,) + s.shape, s.dtype) for s in shards],
                  [pltpu.SemaphoreType.DMA((3 * n,)), pltpu.SemaphoreType.DMA((3 * n,)), pltpu.SemaphoreType.DMA((n,))],
                  start, finish)


def _scatter_rider(gws, layer, gsmall=None):
    n = len(gws)
    per = n + (gsmall is not None)

    def plan(ins, outs, sems):
        send_sems, recv_sems, local_sems = sems
        x, y, c = _my_place()
        me = 4 * x + 2 * y + c

        def peer(m):
            return (x ^ ((m >> 2) & 1), y ^ ((m >> 1) & 1), c ^ (m & 1))

        def slot_of(m):
            px, py, pc = peer(m)
            return 4 * px + 2 * py + pc

        def piece(i, m, slot):
            px, py, pc = peer(m)
            return _remote(ins[i].at[2 * px + py], outs[i].at[slot], send_sems, recv_sems, (m - 1) * per + i,
                           (px, py, pc))

        def small(m, slot):
            return _remote(ins[n], outs[n].at[slot], send_sems, recv_sems, (m - 1) * per + n, peer(m))

        own = [pltpu.make_async_copy(ins[i].at[2 * x + y], outs[i].at[me], local_sems.at[i]) for i in range(n)]
        return c, me, slot_of, piece, small, own

    def start(ins, outs, sems):
        c, me, slot_of, piece, small, own = plan(ins, outs, sems)

        @pl.when(c == layer)
        def _():
            for cp in own:
                cp.start()
            for m in (2, 4, 6):
                for i in range(n):
                    piece(i, m, me).start()

        @pl.when(c != layer)
        def _():
            for m in (1, 3, 5, 7):
                for i in range(n):
                    piece(i, m, me).start()

        if gsmall is not None:
            pltpu.make_async_copy(ins[n], outs[n].at[me], sems[2].at[n]).start()
            for m in range(1, 8):
                small(m, me).start()

    def finish(ins, outs, sems):
        c, me, slot_of, piece, small, own = plan(ins, outs, sems)

        @pl.when(c == layer)
        def _():
            for m in range(1, 8):
                for i in range(n):
                    piece(i, m, slot_of(m)).wait_recv()
            for m in (2, 4, 6):
                for i in range(n):
                    piece(i, m, me).wait_send()
            for cp in own:
                cp.wait()

        @pl.when(c != layer)
        def _():
            for m in (1, 3, 5, 7):
                for i in range(n):
                    piece(i, m, me).wait_send()

        if gsmall is not None:
            for m in range(1, 8):
                small(m, slot_of(m)).wait_recv()
            for m in range(1, 8):
                small(m, me).wait_send()
            pltpu.make_async_copy(ins[n], outs[n].at[me], sems[2].at[n]).wait()

    inputs = list(gws) + ([gsmall] if gsmall is not None else [])
    shapes = [_sds((8,) + g.shape[1:], g.dtype) for g in gws]
    if gsmall is not None:
        shapes.append(_sds((8,) + gsmall.shape, gsmall.dtype))
    nsem = 7 * per
    return _Rider(inputs, shapes, [pltpu.SemaphoreType.DMA((nsem,)), pltpu.SemaphoreType.DMA((nsem,)),
                                   pltpu.SemaphoreType.DMA((per,))], start, finish)


def _sum_slots(slots, name, other=None):
    _, r, c = slots.shape
    tr = min(r, FLAT_TILE)
    assert r % tr == 0
    out_dtype = F32 if other is None else BF16

    def total(ref):
        acc = ref[0].astype(F32)
        for d in range(1, 8):
            acc = acc + ref[d].astype(F32)
        return acc

    def body(*refs):
        o_ref = refs[-1]
        if other is None:
            o_ref[...] = total(refs[0])
        else:
            core = lax.axis_index("c")

            @pl.when(core == 0)
            def _():
                o_ref[...] = total(refs[0]).astype(out_dtype)

            @pl.when(core == 1)
            def _():
                o_ref[...] = total(refs[1]).astype(out_dtype)

    spec = pl.BlockSpec((8, tr, c), lambda i: (0, i, 0))
    ins = [slots] if other is None else [slots, other]
    return _pcall(
        body, name=name, grid=(r // tr,), in_specs=[spec] * len(ins),
        out_specs=_tok(tr, c), out_shape=_sds((r, c), out_dtype), sem=("arbitrary",),
    )(*ins)


def _swap_rider(reds):
    n = len(reds)

    def copies(ins, outs, sems):
        send_sems, recv_sems, local_sems = sems
        x, y, c = _my_place()
        sibling = (x, y, 1 - c)
        mine = [pltpu.make_async_copy(ins[i], outs[i].at[c], local_sems.at[i]) for i in range(n)]
        sends = [_remote(ins[i], outs[i].at[c], send_sems, recv_sems, i, sibling) for i in range(n)]
        lands = [_remote(ins[i], outs[i].at[1 - c], send_sems, recv_sems, i, sibling) for i in range(n)]
        return mine, sends, lands

    def start(ins, outs, sems):
        mine, sends, _ = copies(ins, outs, sems)
        for cp in mine + sends:
            cp.start()

    def finish(ins, outs, sems):
        mine, sends, lands = copies(ins, outs, sems)
        for cp in lands:
            cp.wait_recv()
        for cp in sends:
            cp.wait_send()
        for cp in mine:
            cp.wait()

    return _Rider(list(reds), [_sds((2,) + r.shape, r.dtype) for r in reds],
                  [pltpu.SemaphoreType.DMA((n,)), pltpu.SemaphoreType.DMA((n,)), pltpu.SemaphoreType.DMA((n,))],
                  start, finish)


W_IN_SHARD, W_IN_SHARD_PAD = 1192, 1280
W_UQ_SHARD, W_UQ_SHARD_PAD = 192, 256


def _swa_place(t):
    z = jnp.zeros_like(t)
    lo = jnp.concatenate([t, z], axis=1)
    hi = jnp.concatenate([z, t], axis=1)
    group = (jnp.arange(NH) // SWA_R).reshape((NH,) + (1,) * (t.ndim - 1))
    full = jnp.where(group == 0, lo, hi)
    return full.reshape((NH * LANE,) + t.shape[2:])


def _swa_unplace(t):
    t = t.reshape((NH, 2, 64) + t.shape[1:])
    return jnp.concatenate([t[:SWA_R, 0], t[SWA_R:, 1]], axis=0)


def _pad_w_o_mla(w):
    return jnp.pad(w.reshape(NH, 64, D), ((0, 0), (0, 64), (0, 0))).reshape(NH * LANE, D)


def _unpad_w_o_mla(g):
    return g.reshape(NH, LANE, D)[:, :64].reshape(NH * 64, D)


def _w_in_cols():
    src = np.full((C_END,), -1, np.int64)
    src[C_CQ:C_KPE] = np.arange(0, 384)
    src[C_KPE + 64:C_KPE + 96] = np.arange(384, 416)
    for h in range(NH):
        at = C_QS + h * LANE + 64 * (h // SWA_R)
        src[at:at + 64] = 416 + h * 64 + np.arange(64)
    src[C_KS:C_END] = np.arange(928, IN_COLS)
    return src


def _w_uq_cols():
    src = np.full((NH * LANE,), -1, np.int64)
    for h in range(NH):
        src[h * LANE:h * LANE + 96] = h * 96 + np.arange(96)
    return src


def _w_ukv_cols():
    src = np.full((2 * NH * LANE,), -1, np.int64)
    for h in range(NH):
        src[h * LANE:h * LANE + 64] = h * 128 + np.arange(64)
        src[NH * LANE + h * LANE:NH * LANE + h * LANE + 64] = h * 128 + 64 + np.arange(64)
    return src


def _selection(src_cols, width, width_pad):
    want = jnp.asarray(np.asarray(src_cols, np.int32))[None, None, :]
    k = jnp.arange(width_pad, dtype=jnp.int32)[None, :, None]
    have = jnp.where(k < width, jnp.arange(4, dtype=jnp.int32)[:, None, None] * width + k, -2)
    return (want == have).astype(BF16)


def _selections():
    return dict(w_in=_selection(_w_in_cols(), W_IN_SHARD, W_IN_SHARD_PAD),
                w_uq=_selection(_w_uq_cols(), W_UQ_SHARD, W_UQ_SHARD_PAD),
                w_ukv=_selection(_w_ukv_cols(), 256, 256))


def _pad_last(a, width):
    return jnp.pad(a, ((0, 0),) * (a.ndim - 1) + ((0, width - a.shape[-1]),))


def _wire_shards(W, l):
    out = {n: W[n][l].astype(BF16) for n in SHARDED}
    out["w_in"] = _pad_last(out["w_in"], W_IN_SHARD_PAD)
    out["w_uq"] = _pad_last(out["w_uq"], W_UQ_SHARD_PAD)
    return [out[n] for n in SHARDED]


def _join_shards(t, axis):
    _, L, r, c = t.shape
    if axis == 2:
        return t.transpose(1, 2, 0, 3).reshape(L, r, 4 * c)
    return t.transpose(1, 0, 2, 3).reshape(L, 4 * r, c)


EARLY = SHARDED[:3]
LATE = SHARDED[3:]


def _kernel_weights_early(gathered, sels):
    lay = {n: _select_fwd(a[:, None], sels[n], "lay_" + n)[0] for n, a in zip(EARLY, gathered)}
    return dict(win=lay["w_in"], wuq=lay["w_uq"], wukv=lay["w_ukv"])


def _kernel_weights_late(gathered):
    whole = {n: _join_shards(a[:, None], SHARD_AXIS[n])[0] for n, a in zip(LATE, gathered)}
    return dict(wmem=whole["w_mem_kv"], wa=_pad_w_o_mla(whole["w_o_mla"]),
                wb=_swa_place(whole["w_o_swa"].reshape(NH, 64, D)), wc=whole["w_o_mem"], wout=whole["w_out"],
                wup=whole["w_up"], wdown=whole["w_down"])


def _kernel_weights(gathered, sels):
    return {**_kernel_weights_early(gathered[:3], sels), **_kernel_weights_late(gathered[3:])}


def _cols_to_shards(g):
    r, c4 = g.shape
    return g.reshape(r, 4, c4 // 4).transpose(1, 0, 2).astype(BF16)


def _rope_tables(S):
    pos = jnp.arange(S, dtype=F32)
    inv = 1.0 / (ROPE_THETA ** (jnp.arange(0, 32, 2, dtype=F32) / 32))
    ang = pos[:, None] * inv[None, :]
    cos, sin = jnp.cos(ang), jnp.sin(ang)
    one, zero = jnp.ones((S, 64), F32), jnp.zeros((S, 16), F32)
    rc = jnp.concatenate([one, cos, cos, jnp.ones((S, 32), F32)], axis=1)
    rs1 = jnp.concatenate([jnp.zeros((S, 64), F32), zero, sin, jnp.zeros((S, 32), F32)], axis=1)
    rs2 = jnp.concatenate([jnp.zeros((S, 64), F32), -sin, zero, jnp.zeros((S, 32), F32)], axis=1)
    return rc, rs1, rs2


def _bucket_map():
    qi = jnp.arange(WIN)[:, None]
    kj = jnp.arange(2 * WIN)[None, :]
    dist = qi + WIN - kj
    n = jnp.maximum(dist, 0)
    max_exact = REL_BUCKETS // 2
    nf = jnp.maximum(n, 1).astype(F32)
    large = max_exact + (jnp.log(nf / max_exact) / math.log(128 / max_exact)
                         * (REL_BUCKETS - max_exact)).astype(jnp.int32)
    large = jnp.minimum(large, REL_BUCKETS - 1)
    bucket = jnp.where(n < max_exact, n, large)
    return jnp.where((dist >= 0) & (dist < WIN), bucket, -1).astype(jnp.int32)


TS = 256
TS_FWD = 512
TQ = 1024
TQ_FWD = 1024


def _local_step(x, mem, tgt, kw0, sp, sels, kw1=None, wire=None):
    S = x.shape[0]
    ts = min(TS, S)
    tq = min(TQ, S)
    rc, rs1, rs2 = _rope_tables(S)
    bmap = _bucket_map()
    bias = _bias_build(sp["rel_bias"], bmap)
    row = lambda v: v.reshape(1, -1)

    saved = []
    kw = [dict(kw0), kw1]
    for l in range(DEPTH):
        w = kw[l]
        an, qn, kvn = row(sp["attn_norm"][l]), row(sp["mla_q_norm"][l]), row(sp["mla_kv_norm"][l])
        bg, mnorm, mlpn = row(sp["b_gate"][l]), row(sp["mem_norm"][l]), row(sp["mlp_norm"][l])
        sinks = jnp.broadcast_to(sp["attn_sinks"][l][:, None], (NH, LANE))
        q, k, v, qs, ks, vs, qm, g = _pre_fwd(x, an, w["win"], bg, qn, kvn, w["wuq"], w["wukv"], rc, rs1, rs2, ts)
        carry = _join_riders(_gather_rider(wire[0]), _gather_rider(wire[1])) if (l == 0 and wire) else None
        oa, lse, got = _mla_fwd(q, k, v, min(TQ_FWD, S), carry)
        if carry:
            w.update(_kernel_weights_late(got[:len(LATE)]))
            kw[1] = _kernel_weights(got[len(LATE):], sels)
        ob = _swa_fwd(qs, ks, vs, bias, sinks)
        kvm = _memkv_fwd(mem, mnorm, w["wmem"])
        oc = _mem_fwd(qm, kvm, ts)
        x1, yb = _merge_fwd(x, g, oa, ob, oc, w["wa"], w["wb"], w["wc"], w["wout"], min(TS_FWD, S))
        x2 = _mlp_fwd(x1, mlpn, w["wup"], w["wdown"], min(TS_FWD, S))
        saved.append(dict(w=w, x=x, x1=x1, q=q, k=k, v=v, qs=qs, ks=ks, vs=vs, qm=qm, g=g, oa=oa, lse=lse, ob=ob,
                          oc=oc, kvm=kvm, yb=yb, an=an, qn=qn, kvn=kvn, mnorm=mnorm, mlpn=mlpn, sinks=sinks))
        x = x2

    sq, dx, dfn = _loss_kernel(x, row(sp["final_norm"]), tgt, ts)

    big = {n: [None] * DEPTH for n in SHARDED}
    small = {n: [None] * DEPTH for n in SMALL if n not in ("rel_bias", "final_norm")}
    dbias_total = None
    slots1 = None
    for l in reversed(range(DEPTH)):
        sv = saved[l]
        w = sv["w"]
        dx1, hb2, dub, ab, dxb, dmlpn = _mlp_bwd(dx, sv["x1"], sv["mlpn"], w["wup"], w["wdown"], ts)
        big["w_up"][l] = _matmul_tn(hb2, dub, "dw_up", shard_axis=1)
        big["w_down"][l] = _matmul_tn(ab, dxb, "dw_down", shard_axis=0)
        small["mlp_norm"][l] = dmlpn[0]

        dgp, dyo, doa, dob, doc, dla, dx1b, dbg = _merge_bwd(dx1, sv["g"], sv["oa"], sv["ob"], sv["oc"], w["wa"],
                                                             w["wb"], w["wc"], w["wout"], ts)
        big["w_out"][l] = _matmul_tn(sv["yb"], dx1b, "dw_out", shard_axis=0)
        big["w_o_mla"][l] = _cols_to_shards(_unpad_w_o_mla(_matmul_tn(sv["oa"], dyo[:, 0:D], "dw_o_mla")))
        big["w_o_swa"][l] = _cols_to_shards(
            _swa_unplace(_matmul_tn(sv["ob"], dyo[:, D:2 * D], "dw_o_swa")).reshape(NH * 64, D))
        big["w_o_mem"][l] = _matmul_tn(sv["oc"], dyo[:, 2 * D:3 * D], "dw_o_mem", shard_axis=1)
        small["b_gate"][l] = dbg[0]

        dqm, dkvm = _mem_bwd(sv["qm"], sv["kvm"], doc, ts)
        dwmem, dmnorm = _memkv_bwd(mem, sv["mnorm"], w["wmem"], dkvm)
        big["w_mem_kv"][l] = dwmem.reshape(4, D // 4, D).astype(BF16)
        small["mem_norm"][l] = dmnorm[0]

        dqs, dks, dvs, dbias, dsink = _swa_bwd(sv["qs"], sv["ks"], sv["vs"], dob, bias, sv["sinks"])
        dbias_total = dbias if dbias_total is None else dbias_total + dbias
        small["attn_sinks"][l] = dsink[:, 0]

        carry = _join_riders(_scatter_rider([big[n][1] for n in SHARDED], 1),
                             _scatter_rider([big[n][0] for n in LATE], 0)) if (l == 0 and wire) else None
        dq, dk, dv, slots = _mla_bwd(sv["q"], sv["k"], sv["v"], doa, sv["lse"], dla, tq, carry)
        if carry:
            late0, late1 = slots[len(SHARDED):], slots[len(EARLY):len(SHARDED)]
            carry = _swap_rider([_sum_slots(s0, "sum_" + n, s1) for n, s0, s1 in zip(LATE, late0, late1)])

        dx, dproj, hb, cqn, ckvn, dqpre, dkv, dan, dqn, dkvn, swapped = _pre_bwd(
            sv["x"], dx1, dq, dk, dv, dqs, dks, dvs, dqm, dgp, sv["an"], sv["qn"], sv["kvn"], w["win"], w["wuq"],
            w["wukv"], rc, rs1, rs2, ts, carry)
        if carry:
            slots1 = (slots[:len(EARLY)], swapped)
        for n, (a, b) in (("w_in", (hb, dproj)), ("w_uq", (cqn, dqpre)), ("w_ukv", (ckvn, dkv))):
            big[n][l] = _select_bwd(_matmul_tn(a, b, "d" + n)[None], sels[n], "shard_d" + n)[0]
        small["attn_norm"][l] = dan[0]
        small["mla_q_norm"][l] = dqn[0]
        small["mla_kv_norm"][l] = dkvn[0]

    gw = [[big[n][l] for n in SHARDED] for l in range(DEPTH)]
    gs = {n: jnp.stack(v) for n, v in small.items()}
    gs["rel_bias"] = _bias_reduce(dbias_total, bmap)
    gs["final_norm"] = dfn[0]
    return sq, dx, gw, slots1, gs


def _flatten(parts, rows):
    flat = jnp.concatenate([p.reshape(-1) for p in parts])
    return jnp.pad(flat, (0, rows * FLAT_W - flat.shape[0])).reshape(rows, FLAT_W)


def _unflatten(buf, shapes):
    flat = buf.reshape(-1)
    out, at = [], 0
    for s in shapes:
        n = int(np.prod(s))
        out.append(flat[at:at + n].reshape(s))
        at += n
    return out


def kernel(x, mem, rel_bias, attn_norm, mem_norm, w_in, b_gate, mla_q_norm, w_uq, mla_kv_norm, w_ukv, attn_sinks, w_mem_kv, w_o_mla, w_o_swa, w_o_mem, w_out, mlp_norm, w_up, w_down, final_norm, loss_target, m_rel_bias, m_attn_norm, m_mem_norm, m_w_in, m_b_gate, m_mla_q_norm, m_w_uq, m_mla_kv_norm, m_w_ukv, m_attn_sinks, m_w_mem_kv, m_w_o_mla, m_w_o_swa, m_w_o_mem, m_w_out, m_mlp_norm, m_w_up, m_w_down, m_final_norm, v_rel_bias, v_attn_norm, v_mem_norm, v_w_in, v_b_gate, v_mla_q_norm, v_w_uq, v_mla_kv_norm, v_w_ukv, v_attn_sinks, v_w_mem_kv, v_w_o_mla, v_w_o_swa, v_w_o_mem, v_w_out, v_mlp_norm, v_w_up, v_w_down, v_final_norm):
    args = dict(locals())
    W = {n: args[n] for n in WEIGHTS}
    M = {n: args["m_" + n] for n in WEIGHTS}
    V = {n: args["v_" + n] for n in WEIGHTS}
    small_shapes = [W[n].shape for n in SMALL]
    sels = _selections()

    wire0, wire1 = _wire_shards(W, 0), _wire_shards(W, 1)
    kw0 = _kernel_weights_early(_run_rider(_gather_rider(wire0[:len(EARLY)]), "gather_weights"), sels)
    sp = {n: W[n] for n in SMALL}

    sq, dx, gw, carried, gs = _local_step(x[0], mem[0], loss_target[0], kw0, sp, sels,
                                          wire=(wire0[len(EARLY):], wire1))

    slots1_early, swapped_late = carried
    gsmall = _flatten([gs[n] for n in SMALL], SMALL_ROWS)
    *slots0_early, small_slots = _run_rider(_scatter_rider(gw[0][:len(EARLY)], 0, gsmall), "scatter_grads")
    reds = [_sum_slots(s0, "sum_" + n, s1) for n, s0, s1 in zip(EARLY, slots0_early, slots1_early)]
    red_small = _sum_slots(small_slots, "sum_small")
    swapped = list(_run_rider(_swap_rider(reds), "swap_layers")) + list(swapped_late)
    G = {n: g.astype(F32) for n, g in zip(SHARDED, swapped)}
    G["w_in"] = G["w_in"][..., :W_IN_SHARD]
    G["w_uq"] = G["w_uq"][..., :W_UQ_SHARD]

    DW, NM, NV = {}, {}, {}
    for n in SHARDED:
        shape = W[n].shape
        two_d = lambda a: a.reshape(-1, shape[-1])
        d, nm, nv = _adamw(two_d(W[n]), two_d(G[n]), two_d(M[n]), two_d(V[n]), "adamw_" + n)
        DW[n], NM[n], NV[n] = d.reshape(shape), nm.reshape(shape), nv.reshape(shape)
    d_s, m_s, v_s = _adamw(_flatten([W[n] for n in SMALL], SMALL_ROWS), red_small,
                           _flatten([M[n] for n in SMALL], SMALL_ROWS), _flatten([V[n] for n in SMALL], SMALL_ROWS),
                           "adamw_small")
    for out, buf in ((G, red_small), (DW, d_s), (NM, m_s), (NV, v_s)):
        out.update(zip(SMALL, _unflatten(buf, small_shapes)))
    loss = lax.psum(0.5 * sq[0, 0] / D, ("x", "y", "c"))
    return (loss, dx[None], *[G[n] for n in WEIGHTS], *[DW[n] for n in WEIGHTS], *[NM[n] for n in WEIGHTS],
            *[NV[n] for n in WEIGHTS])
```
